```python
import jax, jax.numpy as jnp
from jax import lax
import numpy as np

D_MODEL = 1024
BATCH = 4
SEQ = 4096
DEPTH = 2
DEC_BATCH = 128
DEC_SEQ = 8
PAST_LEN = 8192
PAGE_SIZE = 128

BRANCH_W = D_MODEL // 2
N_BRANCHES = 5
N_MIXERS = 4
GLA_HEADS = 4
GLA_DK = BRANCH_W // (2 * GLA_HEADS)
GLA_DV = BRANCH_W // GLA_HEADS
GLA_RANK = 16
GLA_TAU = 16.0
GLA_CHUNK = 64
LRU_W = BRANCH_W
LRU_BLOCKS = 8
LRU_BS = LRU_W // LRU_BLOCKS
CONV_W = 4
LRU_C = 8.0
HEAD_DIM = 64
SWA_HEADS = BRANCH_W // HEAD_DIM
SWA_KV_HEADS = 2
SWA_GROUP = SWA_HEADS // SWA_KV_HEADS
WINDOW = 128
SWA_BLOCK = 128
ROT_DIM = HEAD_DIM // 4
ROPE_THETA = 500000.0
SGU_W = BRANCH_W
SGU_GROUPS = 4
SGU_GC = SGU_W // SGU_GROUPS
SGU_CHUNK = 128
N_MEM = 256
MEM_HEADS = 4
MEM_W = MEM_HEADS * HEAD_DIM
LN_EPS = 1e-5
RMS_EPS = 1e-6
DN_ALPHA = (2 * DEPTH) ** 0.25
DN_BETA = (8 * DEPTH) ** -0.25

IN_SPLITS = (GLA_HEADS * GLA_DK, GLA_HEADS * GLA_DK, GLA_HEADS * GLA_DV, GLA_RANK, BRANCH_W,
             LRU_W, LRU_W,
             SWA_HEADS * HEAD_DIM, SWA_KV_HEADS * HEAD_DIM, SWA_KV_HEADS * HEAD_DIM, BRANCH_W,
             SGU_W, SGU_W, SGU_W,
             MEM_W, MEM_W,
             N_BRANCHES * D_MODEL)
N_IN = sum(IN_SPLITS)

kernel_name = 'hybrid_gla_rglru_swa_gmlp_mem_decoder_step'


def layer_norm(x, g, b):
    xf = x.astype(jnp.float32)
    mu = jnp.mean(xf, axis=-1, keepdims=True)
    var = jnp.mean(jnp.square(xf - mu), axis=-1, keepdims=True)
    return ((xf - mu) * lax.rsqrt(var + LN_EPS) * g + b).astype(x.dtype)


def partial_rope(x, pos):
    half = ROT_DIM // 2
    inv = ROPE_THETA ** (-jnp.arange(half, dtype=jnp.float32) / half)
    ang = pos.astype(jnp.float32)[:, None] * inv[None, :]
    cos = jnp.cos(ang)[None, :, None, :]
    sin = jnp.sin(ang)[None, :, None, :]
    xf = x.astype(jnp.float32)
    x1, x2 = xf[..., :half], xf[..., half:ROT_DIM]
    rot = jnp.concatenate([x1 * cos - x2 * sin, x2 * cos + x1 * sin], axis=-1).astype(x.dtype)
    return jnp.concatenate([rot, x[..., ROT_DIM:]], axis=-1)


def gla_mix(q, k, v, log_a, s0):
    B, T, H, DK = q.shape
    C = min(GLA_CHUNK, T)
    n = T // C

    def blocks(t):
        return jnp.moveaxis(t.reshape(B, n, C, H, t.shape[-1]), 1, 0)

    causal = jnp.tril(jnp.ones((C, C), dtype=bool))[None, :, :, None, None]

    def step(S, inp):
        qc, kc, vc, ac = inp
        b = jnp.cumsum(ac, axis=1)
        o_inter = jnp.einsum('bthk,bhkv->bthv', qc * jnp.exp(b), S)
        diff = jnp.where(causal, b[:, :, None] - b[:, None, :], -jnp.inf)
        att = jnp.einsum('btshk,bshk->bhts', qc[:, :, None] * jnp.exp(diff), kc)
        o_intra = jnp.einsum('bhts,bshv->bthv', att, vc)
        b_last = b[:, -1]
        S = S * jnp.exp(b_last)[..., None] + jnp.einsum(
            'bshk,bshv->bhkv', kc * jnp.exp(b_last[:, None] - b), vc)
        return S, o_inter + o_intra

    s_fin, o = lax.scan(step, s0, (blocks(q), blocks(k), blocks(v), blocks(log_a)))
    return jnp.moveaxis(o, 0, 1).reshape(B, T, H, v.shape[-1]), s_fin


def causal_conv(x, buf, w, b):
    T = x.shape[1]
    xp = jnp.concatenate([buf.astype(x.dtype), x], axis=1)
    y = b
    for j in range(CONV_W):
        y = y + xp[:, j:j + T] * w[j]
    return y, xp[:, -(CONV_W - 1):]


def rg_lru(xc, h0, wr, br, wi, bi, lam):
    B, T, W = xc.shape
    xb = xc.reshape(B, T, LRU_BLOCKS, LRU_BS)
    r = jax.nn.sigmoid(jnp.einsum('btnc,ncd->btnd', xb, wr).reshape(B, T, W) + br)
    i = jax.nn.sigmoid(jnp.einsum('btnc,ncd->btnd', xb, wi).reshape(B, T, W) + bi)
    log_a = (LRU_C * r * jax.nn.log_sigmoid(lam)).astype(jnp.float32)
    a = jnp.exp(log_a)
    u = jnp.sqrt(-jnp.expm1(2.0 * log_a)) * (i * xc).astype(jnp.float32)

    def step(h, inp):
        a_t, u_t = inp
        h = a_t * h + u_t
        return h, h

    h_fin, hs = lax.scan(step, h0.astype(jnp.float32), (jnp.swapaxes(a, 0, 1), jnp.swapaxes(u, 0, 1)))
    return jnp.swapaxes(hs, 0, 1).astype(xc.dtype), h_fin


def swa_sink_attention(q, k, v, k_past, v_past, pos0, sinks):
    B, T = q.shape[:2]
    QB = min(SWA_BLOCK, T)
    nb = T // QB
    k_all = jnp.concatenate([k_past.astype(k.dtype), k], axis=1)
    v_all = jnp.concatenate([v_past.astype(v.dtype), v], axis=1)
    band = jnp.arange(nb)[:, None] * QB + jnp.arange(QB + WINDOW)[None, :]
    kb = k_all[:, band]
    vb = v_all[:, band]
    qb = q.reshape(B, nb, QB, SWA_KV_HEADS, SWA_GROUP, HEAD_DIM)
    q_pos = pos0 + jnp.arange(nb)[:, None] * QB + jnp.arange(QB)[None, :]
    k_pos = pos0 - WINDOW + band
    dist = q_pos[:, :, None] - k_pos[:, None, :]
    mask = (dist >= 0) & (dist <= WINDOW) & (k_pos[:, None, :] >= 0)
    s = jnp.einsum('bnqhgd,bnshd->bnhgqs', qb, kb).astype(jnp.float32) * (HEAD_DIM ** -0.5)
    s = jnp.where(mask[None, :, None, None], s, -jnp.inf)
    sink = jnp.broadcast_to(sinks.reshape(SWA_KV_HEADS, SWA_GROUP, 1, 1).astype(jnp.float32),
                            s.shape[:-1] + (1,))
    p = jax.nn.softmax(jnp.concatenate([s, sink], axis=-1), axis=-1)[..., :-1].astype(v.dtype)
    o = jnp.einsum('bnhgqs,bnshd->bnqhgd', p, vb)
    return o.reshape(B, T, SWA_HEADS * HEAD_DIM), k_all[:, -WINDOW:], v_all[:, -WINDOW:]


def sgu_mix(u, v, ln_g, ln_b, w_s, b_s):
    B, T, _ = u.shape
    vn = layer_norm(v, ln_g, ln_b)
    Tp = -(-T // SGU_CHUNK) * SGU_CHUNK
    vp = jnp.pad(vn, ((0, 0), (0, Tp - T), (0, 0))).reshape(B, Tp // SGU_CHUNK, SGU_CHUNK, SGU_GROUPS, SGU_GC)
    w_c = w_s * jnp.tril(jnp.ones((SGU_CHUNK, SGU_CHUNK), dtype=w_s.dtype))
    mixed = jnp.einsum('gts,bnsgc->bntgc', w_c, vp) + b_s.T[None, None, :, :, None]
    mixed = mixed.reshape(B, Tp, SGU_W)[:, :T]
    return u * mixed, vn


def mem_attention(q, mk, mv):
    B, T = q.shape[:2]
    s = jnp.einsum('bthd,bshd->bhts', q, mk.astype(q.dtype)).astype(jnp.float32) * (HEAD_DIM ** -0.5)
    p = jax.nn.softmax(s, axis=-1).astype(q.dtype)
    return jnp.einsum('bhts,bshd->bthd', p, mv.astype(q.dtype)).reshape(B, T, MEM_W)


def trunk_layer(x, pos0, gla_s0, lru_h0, conv_buf, swa_k_past, swa_v_past, mem_k, mem_v, lp):
    (w_in, gla_wa2, gla_ba, gla_norm_g, lru_conv_w, lru_conv_b, lru_wr, lru_br, lru_wi, lru_bi, lru_L,
     swa_sinks, sgu_ln_g, sgu_ln_b, sgu_w, sgu_b, w_branch, w_branch_mem, w_out, ln_g, ln_b) = lp
    B, T, _ = x.shape
    f32 = jnp.float32
    proj = x @ w_in
    offs = [int(o) for o in np.cumsum(IN_SPLITS)[:-1]]
    (gq, gk, gv, glr, gz, lx, lz, sq, sk, sv, sz, su, svv, suz, mq, mz, gates) = jnp.split(proj, offs, axis=-1)

    log_a = jax.nn.log_sigmoid((glr @ gla_wa2 + gla_ba).astype(f32)) / GLA_TAU
    qh = gq.reshape(B, T, GLA_HEADS, GLA_DK).astype(f32) * (GLA_DK ** -0.5)
    kh = gk.reshape(B, T, GLA_HEADS, GLA_DK).astype(f32)
    vh = gv.reshape(B, T, GLA_HEADS, GLA_DV).astype(f32)
    o, gla_s = gla_mix(qh, kh, vh, log_a.reshape(B, T, GLA_HEADS, GLA_DK), gla_s0.astype(f32))
    o = o * lax.rsqrt(jnp.mean(o * o, axis=-1, keepdims=True) + RMS_EPS) * gla_norm_g
    y_gla = o.reshape(B, T, BRANCH_W).astype(x.dtype) * jax.nn.silu(gz)

    xc, conv_new = causal_conv(lx, conv_buf, lru_conv_w, lru_conv_b)
    h, lru_h = rg_lru(xc, lru_h0, lru_wr, lru_br, lru_wi, lru_bi, lru_L)
    y_lru = h * jax.nn.silu(lz)

    pos = pos0 + jnp.arange(T)
    qs = partial_rope(sq.reshape(B, T, SWA_HEADS, HEAD_DIM), pos)
    ks = partial_rope(sk.reshape(B, T, SWA_KV_HEADS, HEAD_DIM), pos)
    vs = sv.reshape(B, T, SWA_KV_HEADS, HEAD_DIM)
    o_swa, swa_k, swa_v = swa_sink_attention(qs, ks, vs, swa_k_past, swa_v_past, pos0, swa_sinks)
    y_swa = o_swa * jax.nn.silu(sz)

    o_sgu, sgu_v = sgu_mix(su, svv, sgu_ln_g, sgu_ln_b, sgu_w, sgu_b)
    y_sgu = o_sgu * jax.nn.silu(suz)

    o_mem = mem_attention(mq.reshape(B, T, MEM_HEADS, HEAD_DIM), mem_k, mem_v)
    y_mem = o_mem * jax.nn.silu(mz)

    g = jax.nn.sigmoid(gates).reshape(B, T, N_BRANCHES, D_MODEL)
    ys = (y_gla, y_lru, y_swa, y_sgu)
    merged = g[:, :, N_MIXERS] * (y_mem @ w_branch_mem)
    for n in range(N_MIXERS):
        merged = merged + g[:, :, n] * (ys[n] @ w_branch[n])
    out = merged @ w_out
    x_new = layer_norm(DN_ALPHA * x + out, ln_g, ln_b)
    return x_new, (gla_s.astype(x.dtype), lru_h.astype(x.dtype), conv_new, swa_k, swa_v, sgu_v)


def setup_inputs(seed: int = 0) -> dict:
    key = jax.random.key(seed)
    keys = iter(jax.random.split(key, 48))
    f32 = jnp.float32

    def nrm(shape, scale):
        return jax.random.normal(next(keys), shape, f32) * scale

    D = D_MODEL
    inp = {}
    inp['x_prompt'] = nrm((BATCH, SEQ, D), 1.0)
    inp['x_sample'] = nrm((DEC_BATCH, DEC_SEQ, D), 1.0)
    inp['mem_prompt'] = nrm((BATCH, N_MEM, D), 1.0)
    inp['state_gla'] = nrm((DEPTH, DEC_BATCH, GLA_HEADS, GLA_DK, GLA_DV), 0.5)
    inp['state_lru_h'] = nrm((DEPTH, DEC_BATCH, LRU_W), 0.5)
    inp['state_lru_conv'] = nrm((DEPTH, DEC_BATCH, CONV_W - 1, LRU_W), 1.0)
    inp['cache_swa_k'] = nrm((DEPTH, DEC_BATCH, WINDOW, SWA_KV_HEADS, HEAD_DIM), 1.0)
    inp['cache_swa_v'] = nrm((DEPTH, DEC_BATCH, WINDOW, SWA_KV_HEADS, HEAD_DIM), 1.0)
    inp['cache_mem_k'] = nrm((DEPTH, DEC_BATCH, N_MEM, MEM_HEADS, HEAD_DIM), 1.0)
    inp['cache_mem_v'] = nrm((DEPTH, DEC_BATCH, N_MEM, MEM_HEADS, HEAD_DIM), 1.0)
    inp['ln_in_g'] = 1.0 + nrm((D,), 0.02)
    inp['ln_in_b'] = nrm((D,), 0.02)
    inp['w_in'] = nrm((DEPTH, D, N_IN), D ** -0.5)
    inp['gla_wa2'] = nrm((DEPTH, GLA_RANK, GLA_HEADS * GLA_DK), GLA_RANK ** -0.5)
    inp['gla_ba'] = nrm((DEPTH, GLA_HEADS * GLA_DK), 0.1)
    inp['gla_norm_g'] = 1.0 + nrm((DEPTH, GLA_DV), 0.02)
    inp['lru_conv_w'] = nrm((DEPTH, CONV_W, LRU_W), CONV_W ** -0.5)
    inp['lru_conv_b'] = nrm((DEPTH, LRU_W), 0.02)
    inp['lru_wr'] = nrm((DEPTH, LRU_BLOCKS, LRU_BS, LRU_BS), LRU_BS ** -0.5)
    inp['lru_br'] = nrm((DEPTH, LRU_W), 0.02)
    inp['lru_wi'] = nrm((DEPTH, LRU_BLOCKS, LRU_BS, LRU_BS), LRU_BS ** -0.5)
    inp['lru_bi'] = nrm((DEPTH, LRU_W), 0.02)
    a0 = jax.random.uniform(next(keys), (DEPTH, LRU_W), f32, 0.9, 0.999)
    inp['lru_L'] = jnp.log(a0) - jnp.log1p(-a0)
    inp['swa_sinks'] = nrm((DEPTH, SWA_HEADS), 1.0)
    inp['sgu_ln_g'] = 1.0 + nrm((DEPTH, SGU_W), 0.02)
    inp['sgu_ln_b'] = nrm((DEPTH, SGU_W), 0.02)
    inp['sgu_w'] = nrm((DEPTH, SGU_GROUPS, SGU_CHUNK, SGU_CHUNK), SGU_CHUNK ** -0.5)
    inp['sgu_b'] = 1.0 + nrm((DEPTH, SGU_GROUPS, SGU_CHUNK), 0.02)
    inp['w_mem_kv'] = nrm((DEPTH, D, 2 * MEM_W), D ** -0.5)
    inp['w_branch'] = nrm((DEPTH, N_MIXERS, BRANCH_W, D), BRANCH_W ** -0.5 * DN_BETA)
    inp['w_branch_mem'] = nrm((DEPTH, MEM_W, D), MEM_W ** -0.5 * DN_BETA)
    inp['w_out'] = nrm((DEPTH, D, D), D ** -0.5 * DN_BETA)
    inp['ln_g'] = 1.0 + nrm((DEPTH, D), 0.02)
    inp['ln_b'] = nrm((DEPTH, D), 0.02)
    return inp


def reference(x_prompt, x_sample, mem_prompt, state_gla, state_lru_h, state_lru_conv,
              cache_swa_k, cache_swa_v, cache_mem_k, cache_mem_v,
              ln_in_g, ln_in_b, w_in, gla_wa2, gla_ba, gla_norm_g,
              lru_conv_w, lru_conv_b, lru_wr, lru_br, lru_wi, lru_bi, lru_L,
              swa_sinks, sgu_ln_g, sgu_ln_b, sgu_w, sgu_b,
              w_mem_kv, w_branch, w_branch_mem, w_out, ln_g, ln_b):
    dt = x_prompt.dtype
    xp = layer_norm(x_prompt, ln_in_g, ln_in_b)
    xs = layer_norm(x_sample, ln_in_g, ln_in_b)
    gla0 = jnp.zeros((BATCH, GLA_HEADS, GLA_DK, GLA_DV), jnp.float32)
    h0 = jnp.zeros((BATCH, LRU_W), dt)
    conv0 = jnp.zeros((BATCH, CONV_W - 1, LRU_W), dt)
    kv0 = jnp.zeros((BATCH, WINDOW, SWA_KV_HEADS, HEAD_DIM), dt)
    st_p, st_s, mk_p, mv_p = [], [], [], []
    for l in range(DEPTH):
        lp = (w_in[l], gla_wa2[l], gla_ba[l], gla_norm_g[l], lru_conv_w[l], lru_conv_b[l],
              lru_wr[l], lru_br[l], lru_wi[l], lru_bi[l], lru_L[l], swa_sinks[l],
              sgu_ln_g[l], sgu_ln_b[l], sgu_w[l], sgu_b[l], w_branch[l], w_branch_mem[l],
              w_out[l], ln_g[l], ln_b[l])
        mkv = mem_prompt @ w_mem_kv[l]
        mk = mkv[..., :MEM_W].reshape(BATCH, N_MEM, MEM_HEADS, HEAD_DIM)
        mv = mkv[..., MEM_W:].reshape(BATCH, N_MEM, MEM_HEADS, HEAD_DIM)
        xp, sp = trunk_layer(xp, 0, gla0, h0, conv0, kv0, kv0, mk, mv, lp)
        xs, ss = trunk_layer(xs, PAST_LEN, state_gla[l], state_lru_h[l], state_lru_conv[l],
                             cache_swa_k[l], cache_swa_v[l], cache_mem_k[l], cache_mem_v[l], lp)
        st_p.append(sp)
        st_s.append(ss)
        mk_p.append(mk)
        mv_p.append(mv)
    P = [jnp.stack([s[i] for s in st_p]) for i in range(5)]
    S = [jnp.stack([s[i] for s in st_s]) for i in range(6)]
    mem_k_prompt = jnp.stack(mk_p)
    mem_v_prompt = jnp.stack(mv_p)
    return (xp, xs, P[0], S[0], P[1], S[1], P[2], S[2], P[3], S[3], P[4], S[4],
            mem_k_prompt, mem_v_prompt, S[5])
```

```python
import functools
import math

import jax
import jax.numpy as jnp
import numpy as np
from jax import lax
from jax.experimental import pallas as pl
from jax.experimental.pallas import tpu as pltpu

f32 = jnp.float32
bf16 = jnp.bfloat16

D_MODEL = 1024
DEPTH = 2
PAST_LEN = 8192
BRANCH_W = 512
GLA_HEADS = 4
GLA_DK = 64
GLA_DV = 128
GLA_RANK = 16
GLA_TAU = 16.0
LRU_BLOCKS = 8
LRU_BS = 64
CONV_W = 4
LRU_C = 8.0
HEAD_DIM = 64
SWA_HEADS = 8
SWA_KV_HEADS = 2
SWA_GROUP = 4
WINDOW = 128
ROT_DIM = 16
ROPE_THETA = 500000.0
SGU_GROUPS = 4
SGU_GC = 128
SGU_CHUNK = 128
N_MEM = 256
MEM_HEADS = 4
MEM_W = 256
LN_EPS = 1e-5
RMS_EPS = 1e-6
DN_ALPHA = (2 * DEPTH) ** 0.25

LANE = 128
SUBLANE = 8

_SEG_ORDER = (("gates", 5 * D_MODEL), ("gv", 512), ("gz", 512), ("lx", 512), ("lz", 512), ("sq", 512),
              ("sz", 512), ("su", 512), ("svv", 512), ("suz", 512), ("gq", 256), ("gk", 256), ("mq", 256),
              ("mz", 256), ("sk", 128), ("sv", 128), ("glr", 128))
_SEG = {}
_off = 0
for _n, _w in _SEG_ORDER:
    assert _off % _w == 0
    _SEG[_n] = (_off, _w)
    _off += _w
N_PROJ = -(-_off // 1024) * 1024
_ORIG = (("gq", 256), ("gk", 256), ("gv", 512), ("glr", 16), ("gz", 512), ("lx", 512), ("lz", 512), ("sq", 512),
         ("sk", 128), ("sv", 128), ("sz", 512), ("su", 512), ("svv", 512), ("suz", 512), ("mq", 256), ("mz", 256),
         ("gates", 5 * D_MODEL))


def _dot(a, b):
    return jnp.dot(a.astype(bf16), b.astype(bf16), preferred_element_type=f32)


def _dot_nt(a, b):
    return lax.dot_general(a.astype(bf16), b.astype(bf16), (((1,), (1,)), ((), ())), preferred_element_type=f32)


def _dot_tn(a, b):
    return lax.dot_general(a.astype(bf16), b.astype(bf16), (((0,), (0,)), ((), ())), preferred_element_type=f32)


def _sigmoid(x):
    return 1.0 / (1.0 + jnp.exp(-x))


def _silu(x):
    return x * _sigmoid(x)


def _log_sigmoid(x):
    return jnp.minimum(x, 0.0) - jnp.log1p(jnp.exp(-jnp.abs(x)))


def _layer_norm(x, g, b):
    mu = jnp.mean(x, axis=-1, keepdims=True)
    xc = x - mu
    var = jnp.mean(xc * xc, axis=-1, keepdims=True)
    return xc * lax.rsqrt(var + LN_EPS) * g + b


def _params(*sem):
    return pltpu.CompilerParams(dimension_semantics=sem)


def _ln_kernel(x_ref, g_ref, b_ref, o_ref):
    o_ref[...] = _layer_norm(x_ref[...], g_ref[...], b_ref[...])


def _ln_call(x, g, b, tm=512):
    n, d = x.shape
    tm = min(tm, n)
    return pl.pallas_call(
        _ln_kernel,
        grid=(n // tm,),
        in_specs=[pl.BlockSpec((tm, d), lambda i: (i, 0)), pl.BlockSpec((1, d), lambda i: (0, 0)),
                  pl.BlockSpec((1, d), lambda i: (0, 0))],
        out_specs=pl.BlockSpec((tm, d), lambda i: (i, 0)),
        out_shape=jax.ShapeDtypeStruct((n, d), f32),
        compiler_params=_params("parallel"),
        name="ln_in",
    )(x, g.reshape(1, d), b.reshape(1, d))


def _matmul_kernel(x_ref, w_ref, o_ref):
    o_ref[...] = _dot(x_ref[...], w_ref[...])


def _matmul_call(x, w, tm, tn, name):
    m, k = x.shape
    n = w.shape[1]
    return pl.pallas_call(
        _matmul_kernel,
        grid=(m // tm, n // tn),
        in_specs=[pl.BlockSpec((tm, k), lambda i, j: (i, 0)), pl.BlockSpec((k, tn), lambda i, j: (0, j))],
        out_specs=pl.BlockSpec((tm, tn), lambda i, j: (i, j)),
        out_shape=jax.ShapeDtypeStruct((m, n), f32),
        compiler_params=_params("parallel", "arbitrary"),
        name=name,
    )(x, w)


def _seg_spec(name, rows, nc):
    off, width = _SEG[name]
    cb = off // width
    return pl.BlockSpec((rows, width), lambda b, c: (b * nc + c, cb))


def _const_spec(shape):
    nd = len(shape)
    return pl.BlockSpec(shape, lambda b, c: (0,) * nd)


def _lru_kernel(lx_ref, lz_ref, hist0_ref, h0_ref, cw_ref, cb_ref, wr_ref, br_ref, wi_ref, bi_ref, lam_ref,
                y_ref, hlast_ref, hist_out_ref, hist_ref, hc_ref, *, nseq, tc):
    c = pl.program_id(1)
    w = BRANCH_W

    @pl.when(c == 0)
    def _():
        hist_ref[...] = hist0_ref[...]
        hc_ref[...] = h0_ref[...]

    x = lx_ref[...].reshape(nseq, tc, w)
    xfull = jnp.concatenate([hist_ref[...], x], axis=1)
    def tap(j):
        return cw_ref[j:j + 1, :].reshape(1, 1, w)

    y = cb_ref[...].reshape(1, 1, w) + x * tap(CONV_W - 1)
    for s in range(1, CONV_W):
        y = y + pltpu.roll(xfull, s, 1)[:, SUBLANE:, :] * tap(CONV_W - 1 - s)
    hist_ref[...] = xfull[:, tc:, :]
    hist_out_ref[...] = xfull[:, tc:, :]

    xc = y.reshape(nseq * tc, w)
    r = _sigmoid(_dot(xc, wr_ref[...]) + br_ref[...])
    i = _sigmoid(_dot(xc, wi_ref[...]) + bi_ref[...])
    log_a = (LRU_C * r) * _log_sigmoid(lam_ref[...])
    a = jnp.exp(log_a)
    u = jnp.sqrt(jnp.tanh(-log_a) * (a * a + 1.0)) * (i * xc)

    acc_a = a.reshape(nseq, tc, w)
    acc_u = u.reshape(nseq, tc, w)
    t = lax.broadcasted_iota(jnp.int32, (nseq, tc, w), 1)
    d = 1
    while d < tc:
        ok = t >= d
        a_sh = jnp.where(ok, pltpu.roll(acc_a, d, 1), 1.0)
        u_sh = jnp.where(ok, pltpu.roll(acc_u, d, 1), 0.0)
        acc_u = acc_a * u_sh + acc_u
        acc_a = acc_a * a_sh
        d *= 2
    h = acc_a * hc_ref[...] + acc_u
    hc_ref[...] = h[:, tc - 1:tc, :]
    hlast_ref[...] = h[:, tc - SUBLANE:, :]
    y_ref[...] = (h.reshape(nseq * tc, w) * _silu(lz_ref[...])).astype(y_ref.dtype)


def _lru_call(proj, hist0, h0, lw, nb, nseq, nc, tc, name):
    rows = nseq * tc
    w = BRANCH_W
    kern = functools.partial(_lru_kernel, nseq=nseq, tc=tc)
    n_tok = nb * nc * rows
    return pl.pallas_call(
        kern,
        grid=(nb, nc),
        in_specs=[_seg_spec("lx", rows, nc), _seg_spec("lz", rows, nc),
                  pl.BlockSpec((nseq, SUBLANE, w), lambda b, c: (b, 0, 0)),
                  pl.BlockSpec((nseq, 1, w), lambda b, c: (b, 0, 0)),
                  _const_spec((CONV_W, w)), _const_spec((1, w)), _const_spec((w, w)), _const_spec((1, w)),
                  _const_spec((w, w)), _const_spec((1, w)), _const_spec((1, w))],
        out_specs=[pl.BlockSpec((rows, w), lambda b, c: (b * nc + c, 0)),
                   pl.BlockSpec((nseq, SUBLANE, w), lambda b, c: (b, 0, 0)),
                   pl.BlockSpec((nseq, SUBLANE, w), lambda b, c: (b, 0, 0))],
        out_shape=[jax.ShapeDtypeStruct((n_tok, w), bf16),
                   jax.ShapeDtypeStruct((nb * nseq, SUBLANE, w), f32),
                   jax.ShapeDtypeStruct((nb * nseq, SUBLANE, w), f32)],
        scratch_shapes=[pltpu.VMEM((nseq, SUBLANE, w), f32), pltpu.VMEM((nseq, 1, w), f32)],
        compiler_params=_params("parallel", "arbitrary"),
        name=name,
    )(proj, proj, hist0, h0, lw["conv_w"], lw["conv_b"], lw["wr"], lw["br"], lw["wi"], lw["bi"], lw["lam"])


def _gla_consts(c):
    t = np.arange(c)[:, None]
    u = np.arange(c)[None, :]
    blocks = [u <= t, u > t]
    masks = [t == u]
    m = 1
    while m < c:
        t0 = (t // m) * m
        odd = (t // m) % 2 == 1
        blocks.append(odd & (u >= t0) & (u <= t))
        blocks.append((~odd) & (u > t) & (u <= t0 + m - 1))
        masks.append((t // (2 * m) == u // (2 * m)) & odd & ((u // m) % 2 == 0))
        m *= 2
    return (np.concatenate(blocks, 0).astype(np.float32), np.stack(masks).astype(np.float32))


def _gla_kernel(gq_ref, gk_ref, gv_ref, glr_ref, gz_ref, s0_ref, wa_ref, ba_ref, ng_ref, d_ref, m_ref,
                y_ref, sout_ref, s_ref, *, nseq, c):
    ci = pl.program_id(1)
    nlev = int(math.log2(c))
    hk = GLA_HEADS * GLA_DK

    @pl.when(ci == 0)
    def _():
        s_ref[...] = s0_ref[...]

    eye_r = lax.broadcasted_iota(jnp.int32, (GLA_DK, GLA_DK), 0)
    eye_c = lax.broadcasted_iota(jnp.int32, (GLA_DK, GLA_DK), 1)
    eye = eye_r == eye_c

    for n in range(nseq):
        rows = slice(n * c, (n + 1) * c)
        q = gq_ref[rows, :] * (GLA_DK ** -0.5)
        k = gk_ref[rows, :]
        v = gv_ref[rows, :]
        z = _dot(glr_ref[rows, :], wa_ref[...]) + ba_ref[...]
        la = _log_sigmoid(z) * (1.0 / GLA_TAU)
        hi = la.astype(bf16)
        r1 = la - hi.astype(f32)
        mid = r1.astype(bf16)
        lo = (r1 - mid.astype(f32)).astype(bf16)
        hml = jnp.concatenate([hi, mid, lo], axis=1)

        def sums(blk):
            p = jnp.dot(d_ref[blk * c:(blk + 1) * c, :].astype(bf16), hml, preferred_element_type=f32)
            return p[:, :hk] + p[:, hk:2 * hk] + p[:, 2 * hk:]

        b = sums(0)
        q_in = q * jnp.exp(b)
        k_st = k * jnp.exp(sums(1))
        dec_row = jnp.exp(b[c - 1:c, :])
        qf = [q]
        kf = [k]
        for lev in range(nlev):
            qf.append(q * jnp.exp(sums(2 + 2 * lev)))
            kf.append(k * jnp.exp(sums(3 + 2 * lev)))

        for h in range(GLA_HEADS):
            ks = slice(h * GLA_DK, (h + 1) * GLA_DK)
            vs = slice(h * GLA_DV, (h + 1) * GLA_DV)
            att = jnp.zeros((c, c), f32)
            for lev in range(nlev + 1):
                att = att + m_ref[lev] * _dot_nt(qf[lev][:, ks], kf[lev][:, ks])
            s_h = s_ref[n, h]
            v_h = v[:, vs]
            o = _dot(q_in[:, ks], s_h) + _dot(att, v_h)
            dec_col = jnp.sum(jnp.where(eye, jnp.broadcast_to(dec_row[:, ks], (GLA_DK, GLA_DK)), 0.0),
                              axis=1, keepdims=True)
            s_ref[n, h] = s_h * dec_col + _dot_tn(k_st[:, ks], v_h)
            o = o * lax.rsqrt(jnp.mean(o * o, axis=-1, keepdims=True) + RMS_EPS) * ng_ref[...]
            y_ref[rows, vs] = (o * _silu(gz_ref[rows, vs])).astype(y_ref.dtype)

    sout_ref[...] = s_ref[...]


def _gla_call(proj, s0, gw, nb, nseq, nc, c, name):
    rows = nseq * c
    dstack, masks = _gla_consts(c)
    kern = functools.partial(_gla_kernel, nseq=nseq, c=c)
    hk = GLA_HEADS * GLA_DK
    st_spec = pl.BlockSpec((nseq, GLA_HEADS, GLA_DK, GLA_DV), lambda b, ci: (b, 0, 0, 0))
    return pl.pallas_call(
        kern,
        grid=(nb, nc),
        in_specs=[_seg_spec("gq", rows, nc), _seg_spec("gk", rows, nc), _seg_spec("gv", rows, nc),
                  _seg_spec("glr", rows, nc), _seg_spec("gz", rows, nc), st_spec,
                  _const_spec((LANE, hk)), _const_spec((1, hk)), _const_spec((1, GLA_DV)),
                  _const_spec(dstack.shape), _const_spec(masks.shape)],
        out_specs=[pl.BlockSpec((rows, BRANCH_W), lambda b, ci: (b * nc + ci, 0)), st_spec],
        out_shape=[jax.ShapeDtypeStruct((nb * nc * rows, BRANCH_W), bf16),
                   jax.ShapeDtypeStruct((nb * nseq, GLA_HEADS, GLA_DK, GLA_DV), f32)],
        scratch_shapes=[pltpu.VMEM((nseq, GLA_HEADS, GLA_DK, GLA_DV), f32)],
        compiler_params=_params("parallel", "arbitrary"),
        name=name,
    )(proj, proj, proj, proj, proj, s0, gw["wa"], gw["ba"], gw["ng"], jnp.asarray(dstack), jnp.asarray(masks))


def _rope_tables(pos0, t):
    half = ROT_DIM // 2
    inv = ROPE_THETA ** (-jnp.arange(half, dtype=f32) / half)
    ang = (pos0 + jnp.arange(t)).astype(f32)[:, None] * inv[None, :]
    cos, sin = jnp.cos(ang), jnp.sin(ang)
    zeros = jnp.zeros((t, HEAD_DIM - ROT_DIM), f32)
    z8 = jnp.zeros((t, half), f32)
    c_tab = jnp.concatenate([cos, cos, jnp.ones((t, HEAD_DIM - ROT_DIM), f32)], axis=1)
    sa_tab = jnp.concatenate([-sin, z8, zeros], axis=1)
    sb_tab = jnp.concatenate([z8, sin, zeros], axis=1)
    return tuple(jnp.concatenate([x, x], axis=1) for x in (c_tab, sa_tab, sb_tab))


def _rope(x, c_tab, sa_tab, sb_tab):
    wd = x.shape[1]
    rep = wd // LANE
    half = ROT_DIM // 2
    if rep > 1:
        c_tab, sa_tab, sb_tab = (jnp.concatenate([tb] * rep, axis=1) for tb in (c_tab, sa_tab, sb_tab))
    return x * c_tab + pltpu.roll(x, wd - half, 1) * sa_tab + pltpu.roll(x, half, 1) * sb_tab


def _swa_kernel(sink_ref, sq_ref, sk_ref, sv_ref, sz_ref, ct_ref, sat_ref, sbt_ref, kp_ref, vp_ref,
                y_ref, kr_ref, kprev_ref, vprev_ref, *, nseq, qb, pos0, carry):
    blk = pl.program_id(1)

    @pl.when(blk == 0)
    def _():
        kprev_ref[...] = kp_ref[...]
        vprev_ref[...] = vp_ref[...]

    tabs = (ct_ref[...], sat_ref[...], sbt_ref[...])
    mrows = SWA_GROUP * qb
    qi = lax.broadcasted_iota(jnp.int32, (mrows, WINDOW), 0) % qb
    kj = lax.broadcasted_iota(jnp.int32, (mrows, WINDOW), 1)
    past_ok = (kj >= qi) & (kj >= (WINDOW - pos0) - blk * qb)
    qi_c = lax.broadcasted_iota(jnp.int32, (mrows, qb), 0) % qb
    kj_c = lax.broadcasted_iota(jnp.int32, (mrows, qb), 1)
    cur_ok = kj_c <= qi_c
    neg = -jnp.inf

    for n in range(nseq):
        rows = slice(n * qb, (n + 1) * qb)
        q = _rope(sq_ref[rows, :], *tabs) * (HEAD_DIM ** -0.5)
        k = _rope(sk_ref[rows, :], *tabs)
        v = sv_ref[rows, :]
        kr_ref[rows, :] = k
        kprev = kprev_ref[n]
        vprev = vprev_ref[n]
        outs = []
        for kv in range(SWA_KV_HEADS):
            ds = slice(kv * HEAD_DIM, (kv + 1) * HEAD_DIM)
            qs = jnp.concatenate(
                [q[:, (kv * SWA_GROUP + g) * HEAD_DIM:(kv * SWA_GROUP + g + 1) * HEAD_DIM] for g in range(SWA_GROUP)],
                axis=0)
            sink = jnp.concatenate(
                [jnp.full((qb, 1), sink_ref[kv * SWA_GROUP + g], f32) for g in range(SWA_GROUP)], axis=0)
            s_p = jnp.where(past_ok, _dot_nt(qs, kprev[:, ds]), neg)
            s_c = jnp.where(cur_ok, _dot_nt(qs, k[:, ds]), neg)
            m = jnp.maximum(jnp.maximum(jnp.max(s_p, axis=1, keepdims=True), jnp.max(s_c, axis=1, keepdims=True)),
                            sink)
            p_p = jnp.exp(s_p - m)
            p_c = jnp.exp(s_c - m)
            den = jnp.sum(p_p, axis=1, keepdims=True) + jnp.sum(p_c, axis=1, keepdims=True) + jnp.exp(sink - m)
            o = (_dot(p_p, vprev[:, ds]) + _dot(p_c, v[:, ds])) / den
            outs.extend(o[g * qb:(g + 1) * qb, :] for g in range(SWA_GROUP))
        o_all = jnp.concatenate(outs, axis=1)
        y_ref[rows, :] = (o_all * _silu(sz_ref[rows, :])).astype(y_ref.dtype)
        if carry:
            kprev_ref[n] = k
            vprev_ref[n] = v


def _swa_call(proj, sinks, k_past, v_past, pos0, nb, nseq, nc, qb, name):
    rows = nseq * qb
    assert nc == 1 or qb == WINDOW
    t_total = nc * qb
    c_tab, sa_tab, sb_tab = _rope_tables(pos0, t_total)
    kern = functools.partial(_swa_kernel, nseq=nseq, qb=qb, pos0=pos0, carry=nc > 1)
    kvw = SWA_KV_HEADS * HEAD_DIM
    tab_spec = pl.BlockSpec((qb, LANE), lambda b, c: (c, 0))
    past_spec = pl.BlockSpec((nseq, WINDOW, kvw), lambda b, c: (b, 0, 0))
    return pl.pallas_call(
        kern,
        grid=(nb, nc),
        in_specs=[pl.BlockSpec(memory_space=pltpu.SMEM),
                  _seg_spec("sq", rows, nc), _seg_spec("sk", rows, nc), _seg_spec("sv", rows, nc),
                  _seg_spec("sz", rows, nc), tab_spec, tab_spec, tab_spec, past_spec, past_spec],
        out_specs=[pl.BlockSpec((rows, BRANCH_W), lambda b, c: (b * nc + c, 0)),
                   pl.BlockSpec((rows, kvw), lambda b, c: (b * nc + c, 0))],
        out_shape=[jax.ShapeDtypeStruct((nb * nc * rows, BRANCH_W), bf16),
                   jax.ShapeDtypeStruct((nb * nc * rows, kvw), f32)],
        scratch_shapes=[pltpu.VMEM((nseq, WINDOW, kvw), f32), pltpu.VMEM((nseq, WINDOW, kvw), f32)],
        compiler_params=_params("parallel", "arbitrary"),
        name=name,
    )(sinks, proj, proj, proj, proj, c_tab, sa_tab, sb_tab, k_past, v_past)


def _sgu_kernel(su_ref, sv_ref, sz_ref, g_ref, b_ref, wm_ref, bias_ref, y_ref, *vn_ref, ntile):
    vn = _layer_norm(sv_ref[...], g_ref[...], b_ref[...])
    if vn_ref:
        vn_ref[0][...] = vn
    for r in range(ntile):
        rows = slice(r * SGU_CHUNK, (r + 1) * SGU_CHUNK)
        mixed = jnp.concatenate(
            [jnp.dot(wm_ref[g], vn[rows, g * SGU_GC:(g + 1) * SGU_GC].astype(bf16), preferred_element_type=f32)
             for g in range(SGU_GROUPS)], axis=1)
        y = su_ref[rows, :] * (mixed + bias_ref[...]) * _silu(sz_ref[rows, :])
        y_ref[rows, :] = y.astype(y_ref.dtype)


def _sgu_call(proj, ln_g, ln_b, wmix, bias, n_tok, ntile, want_vn, name):
    rows = ntile * SGU_CHUNK
    w = BRANCH_W
    kern = functools.partial(_sgu_kernel, ntile=ntile)

    def seg(nm):
        off, width = _SEG[nm]
        return pl.BlockSpec((rows, width), lambda i: (i, off // width))

    def const(shape):
        return pl.BlockSpec(shape, lambda i: (0,) * len(shape))

    out_specs = [pl.BlockSpec((rows, w), lambda i: (i, 0))]
    out_shape = [jax.ShapeDtypeStruct((n_tok, w), bf16)]
    if want_vn:
        out_specs.append(pl.BlockSpec((rows, w), lambda i: (i, 0)))
        out_shape.append(jax.ShapeDtypeStruct((n_tok, w), f32))
    return pl.pallas_call(
        kern,
        grid=(n_tok // rows,),
        in_specs=[seg("su"), seg("svv"), seg("suz"), const((1, w)), const((1, w)),
                  const((SGU_GROUPS, SGU_CHUNK, SGU_CHUNK)), const((SGU_CHUNK, w))],
        out_specs=out_specs,
        out_shape=out_shape,
        compiler_params=_params("parallel"),
        name=name,
    )(proj, proj, proj, ln_g, ln_b, wmix, bias)


def _mem_kernel(mq_ref, mz_ref, mk_ref, mv_ref, y_ref, *, nseq, tq):
    for n in range(nseq):
        rows = slice(n * tq, (n + 1) * tq)
        q = mq_ref[rows, :] * (HEAD_DIM ** -0.5)
        mk = mk_ref[n]
        mv = mv_ref[n]
        outs = []
        for h in range(MEM_HEADS):
            ds = slice(h * HEAD_DIM, (h + 1) * HEAD_DIM)
            s = _dot_nt(q[:, ds], mk[:, ds])
            m = jnp.max(s, axis=1, keepdims=True)
            p = jnp.exp(s - m)
            den = jnp.sum(p, axis=1, keepdims=True)
            outs.append(_dot(p, mv[:, ds]) / den)
        o = jnp.concatenate(outs, axis=1)
        y_ref[rows, :] = (o * _silu(mz_ref[rows, :])).astype(y_ref.dtype)


def _mem_call(proj, mk, mv, nb, nseq, nc, tq, name):
    rows = nseq * tq
    kern = functools.partial(_mem_kernel, nseq=nseq, tq=tq)
    kv_spec = pl.BlockSpec((nseq, N_MEM, MEM_W), lambda b, c: (b, 0, 0))
    return pl.pallas_call(
        kern,
        grid=(nb, nc),
        in_specs=[_seg_spec("mq", rows, nc), _seg_spec("mz", rows, nc), kv_spec, kv_spec],
        out_specs=pl.BlockSpec((rows, MEM_W), lambda b, c: (b * nc + c, 0)),
        out_shape=jax.ShapeDtypeStruct((nb * nc * rows, MEM_W), bf16),
        compiler_params=_params("parallel", "arbitrary"),
        name=name,
    )(proj, proj, mk, mv)


def _merge_kernel(gates_ref, yg_ref, yl_ref, ys_ref, yu_ref, ym_ref, x_ref, wb_ref, wm_ref, wo_ref, g_ref, b_ref,
                  o_ref):
    d = D_MODEL
    merged = _sigmoid(gates_ref[:, 4 * d:5 * d]) * jnp.dot(ym_ref[...], wm_ref[...], preferred_element_type=f32)
    for n, y_ref in enumerate((yg_ref, yl_ref, ys_ref, yu_ref)):
        merged = merged + _sigmoid(gates_ref[:, n * d:(n + 1) * d]) * jnp.dot(
            y_ref[...], wb_ref[n], preferred_element_type=f32)
    out = _dot(merged, wo_ref[...])
    o_ref[...] = _layer_norm(DN_ALPHA * x_ref[...] + out, g_ref[...], b_ref[...])


def _merge_call(proj, ys, x, mw, tm, name):
    n_tok, d = x.shape
    w = BRANCH_W

    def rows(width):
        return pl.BlockSpec((tm, width), lambda i: (i, 0))

    def const(shape):
        return pl.BlockSpec(shape, lambda i: (0,) * len(shape))

    return pl.pallas_call(
        _merge_kernel,
        grid=(n_tok // tm,),
        in_specs=[rows(5 * d), rows(w), rows(w), rows(w), rows(w), rows(MEM_W), rows(d),
                  const((4, w, d)), const((MEM_W, d)), const((d, d)), const((1, d)), const((1, d))],
        out_specs=rows(d),
        out_shape=jax.ShapeDtypeStruct((n_tok, d), f32),
        compiler_params=_params("parallel"),
        name=name,
    )(proj, *ys, x, mw["wb"], mw["wm"], mw["wo"], mw["g"], mw["b"])


def _prep_layer(l, w_in, gla_wa2, gla_ba, gla_norm_g, lru_conv_w, lru_conv_b, lru_wr, lru_br, lru_wi, lru_bi, lru_L,
                swa_sinks, sgu_ln_g, sgu_ln_b, sgu_w, sgu_b, w_mem_kv, w_branch, w_branch_mem, w_out, ln_g, ln_b):
    d = D_MODEL
    w = BRANCH_W
    orig = {}
    o = 0
    for nm, width in _ORIG:
        orig[nm] = (o, width)
        o += width
    cols = []
    for nm, width in _SEG_ORDER:
        oo, ow = orig[nm]
        seg = w_in[l][:, oo:oo + ow]
        if ow < width:
            seg = jnp.pad(seg, ((0, 0), (0, width - ow)))
        cols.append(seg)
    used = sum(wd for _, wd in _SEG_ORDER)
    cols.append(jnp.zeros((d, N_PROJ - used), f32))
    w_proj = jnp.concatenate(cols, axis=1).astype(bf16)

    def block_diag(wb):
        eye = jnp.eye(LRU_BLOCKS, dtype=f32)
        return (eye[:, None, :, None] * wb[:, :, None, :]).reshape(w, w).astype(bf16)

    tril = jnp.tril(jnp.ones((SGU_CHUNK, SGU_CHUNK), f32))
    wmix_p = (sgu_w[l] * tril).astype(bf16)
    bias_p = jnp.repeat(sgu_b[l].T, SGU_GC, axis=1)
    t8 = SUBLANE
    rep = SGU_CHUNK // t8
    w8 = (sgu_w[l] * tril)[:, :t8, :t8]
    seq_eye = jnp.eye(rep, dtype=f32)
    wmix_s = (seq_eye[None, :, None, :, None] * w8[:, None, :, None, :]).reshape(
        SGU_GROUPS, SGU_CHUNK, SGU_CHUNK).astype(bf16)
    bias_s = jnp.tile(bias_p[:t8], (rep, 1))
    return dict(
        w_proj=w_proj,
        w_mem_kv=w_mem_kv[l].astype(bf16),
        gla=dict(wa=jnp.pad(gla_wa2[l], ((0, LANE - GLA_RANK), (0, 0))).astype(bf16),
                 ba=gla_ba[l].reshape(1, -1), ng=gla_norm_g[l].reshape(1, -1)),
        lru=dict(conv_w=lru_conv_w[l], conv_b=lru_conv_b[l].reshape(1, w), wr=block_diag(lru_wr[l]),
                 br=lru_br[l].reshape(1, w), wi=block_diag(lru_wi[l]), bi=lru_bi[l].reshape(1, w),
                 lam=lru_L[l].reshape(1, w)),
        sinks=swa_sinks[l],
        sgu=dict(g=sgu_ln_g[l].reshape(1, w), b=sgu_ln_b[l].reshape(1, w), wmix_p=wmix_p, bias_p=bias_p,
                 wmix_s=wmix_s, bias_s=bias_s),
        merge=dict(wb=w_branch[l].astype(bf16), wm=w_branch_mem[l].astype(bf16), wo=w_out[l].astype(bf16),
                   g=ln_g[l].reshape(1, d), b=ln_b[l].reshape(1, d)),
    )


def _seg_cols(proj, name):
    off, width = _SEG[name]
    return proj[:, off:off + width]


def _layer(x, lw, grp, st, tag):
    nseq_total, t = grp["batch"], grp["seq"]
    n_tok = nseq_total * t
    proj = _matmul_call(x, lw["w_proj"], min(1024, n_tok), 1024, "proj_" + tag)

    lt = grp["lru"]
    y_lru, hlast, hist = _lru_call(proj, st["hist0"], st["h0"], lw["lru"], nseq_total // lt[0], lt[0], t // lt[1],
                                   lt[1], "lru_" + tag)
    gt = grp["gla"]
    y_gla, s_out = _gla_call(proj, st["gla0"], lw["gla"], nseq_total // gt[0], gt[0], t // gt[1], gt[1], "gla_" + tag)
    wt = grp["swa"]
    y_swa, k_rope = _swa_call(proj, lw["sinks"], st["k_past"], st["v_past"], grp["pos0"], nseq_total // wt[0], wt[0],
                              t // wt[1], wt[1], "swa_" + tag)
    sg = lw["sgu"]
    sgu_out = _sgu_call(proj, sg["g"], sg["b"], sg["wmix_" + grp["kind"]], sg["bias_" + grp["kind"]], n_tok,
                        grp["sgu_tiles"], grp["kind"] == "s", "sgu_" + tag)
    mt = grp["mem"]
    y_mem = _mem_call(proj, st["mk"], st["mv"], nseq_total // mt[0], mt[0], t // mt[1], mt[1], "mem_" + tag)
    x_new = _merge_call(proj, (y_gla, y_lru, y_swa, sgu_out[0], y_mem), x, lw["merge"], min(256, n_tok),
                        "merge_" + tag)
    return x_new, dict(gla=s_out, hlast=hlast, hist=hist, k_rope=k_rope, sv=_seg_cols(proj, "sv"),
                       vn=sgu_out[1] if len(sgu_out) > 1 else None)


_PROMPT = dict(kind="p", pos0=0, lru=(1, 256), gla=(1, 128), swa=(1, 128), sgu_tiles=4, mem=(1, 512))
_SAMPLE = dict(kind="s", pos0=PAST_LEN, lru=(32, 8), gla=(8, 8), swa=(8, 8), sgu_tiles=8, mem=(8, 8))


def kernel(x_prompt, x_sample, mem_prompt, state_gla, state_lru_h, state_lru_conv, cache_swa_k, cache_swa_v,
           cache_mem_k, cache_mem_v, ln_in_g, ln_in_b, w_in, gla_wa2, gla_ba, gla_norm_g, lru_conv_w, lru_conv_b,
           lru_wr, lru_br, lru_wi, lru_bi, lru_L, swa_sinks, sgu_ln_g, sgu_ln_b, sgu_w, sgu_b, w_mem_kv, w_branch,
           w_branch_mem, w_out, ln_g, ln_b):
    bp, tp, d = x_prompt.shape
    bs, ts, _ = x_sample.shape
    w = BRANCH_W
    kvw = SWA_KV_HEADS * HEAD_DIM
    gp = dict(_PROMPT, batch=bp, seq=tp)
    gs = dict(_SAMPLE, batch=bs, seq=ts)

    xp = _ln_call(x_prompt.reshape(bp * tp, d), ln_in_g, ln_in_b)
    xs = _ln_call(x_sample.reshape(bs * ts, d), ln_in_g, ln_in_b)
    mem2 = mem_prompt.reshape(bp * N_MEM, d)

    outs_p, outs_s, mks, mvs = [], [], [], []
    for l in range(DEPTH):
        lw = _prep_layer(l, w_in, gla_wa2, gla_ba, gla_norm_g, lru_conv_w, lru_conv_b, lru_wr, lru_br, lru_wi,
                         lru_bi, lru_L, swa_sinks, sgu_ln_g, sgu_ln_b, sgu_w, sgu_b, w_mem_kv, w_branch,
                         w_branch_mem, w_out, ln_g, ln_b)
        mkv = _matmul_call(mem2, lw["w_mem_kv"], bp * N_MEM, 2 * MEM_W, "memkv_%d" % l)
        mk = mkv[:, :MEM_W].reshape(bp, N_MEM, MEM_W)
        mv = mkv[:, MEM_W:].reshape(bp, N_MEM, MEM_W)
        st_p = dict(hist0=jnp.zeros((bp, SUBLANE, w), f32), h0=jnp.zeros((bp, 1, w), f32),
                    gla0=jnp.zeros((bp, GLA_HEADS, GLA_DK, GLA_DV), f32),
                    k_past=jnp.zeros((bp, WINDOW, kvw), f32), v_past=jnp.zeros((bp, WINDOW, kvw), f32),
                    mk=mk, mv=mv)
        st_s = dict(hist0=jnp.pad(state_lru_conv[l], ((0, 0), (SUBLANE - (CONV_W - 1), 0), (0, 0))),
                    h0=state_lru_h[l][:, None, :], gla0=state_gla[l],
                    k_past=cache_swa_k[l].reshape(bs, WINDOW, kvw), v_past=cache_swa_v[l].reshape(bs, WINDOW, kvw),
                    mk=cache_mem_k[l].reshape(bs, N_MEM, MEM_W), mv=cache_mem_v[l].reshape(bs, N_MEM, MEM_W))
        xp, op = _layer(xp, lw, gp, st_p, "p%d" % l)
        xs, os_ = _layer(xs, lw, gs, st_s, "s%d" % l)
        outs_p.append(op)
        outs_s.append(os_)
        mks.append(mk.reshape(bp, N_MEM, MEM_HEADS, HEAD_DIM))
        mvs.append(mv.reshape(bp, N_MEM, MEM_HEADS, HEAD_DIM))

    def stack(outs, fn):
        return jnp.stack([fn(o) for o in outs])

    def last_window(a, b, t):
        return a.reshape(b, t, SWA_KV_HEADS, HEAD_DIM)[:, -WINDOW:]

    def shifted(cache, a, b, t):
        new = a.reshape(b, t, SWA_KV_HEADS, HEAD_DIM)
        return jnp.concatenate([cache[:, t:], new], axis=1)

    return (
        xp.reshape(bp, tp, d), xs.reshape(bs, ts, d),
        stack(outs_p, lambda o: o["gla"]), stack(outs_s, lambda o: o["gla"]),
        stack(outs_p, lambda o: o["hlast"][:, SUBLANE - 1]), stack(outs_s, lambda o: o["hlast"][:, SUBLANE - 1]),
        stack(outs_p, lambda o: o["hist"][:, SUBLANE - (CONV_W - 1):]),
        stack(outs_s, lambda o: o["hist"][:, SUBLANE - (CONV_W - 1):]),
        stack(outs_p, lambda o: last_window(o["k_rope"], bp, tp)),
        jnp.stack([shifted(cache_swa_k[l], outs_s[l]["k_rope"], bs, ts) for l in range(DEPTH)]),
        stack(outs_p, lambda o: last_window(o["sv"], bp, tp)),
        jnp.stack([shifted(cache_swa_v[l], outs_s[l]["sv"], bs, ts) for l in range(DEPTH)]),
        jnp.stack(mks), jnp.stack(mvs),
        stack(outs_s, lambda o: o["vn"].reshape(bs, ts, w)),
    )
```

```python
import functools
import math

import jax
import jax.numpy as jnp
import numpy as np
from jax import lax
from jax.experimental import pallas as pl
from jax.experimental.pallas import tpu as pltpu

f32 = jnp.float32
bf16 = jnp.bfloat16

D_MODEL = 1024
DEPTH = 2
PAST_LEN = 8192
BRANCH_W = 512
GLA_HEADS = 4
GLA_DK = 64
GLA_DV = 128
GLA_RANK = 16
GLA_TAU = 16.0
LRU_BLOCKS = 8
LRU_BS = 64
CONV_W = 4
LRU_C = 8.0
HEAD_DIM = 64
SWA_HEADS = 8
SWA_KV_HEADS = 2
SWA_GROUP = 4
WINDOW = 128
ROT_DIM = 16
ROPE_THETA = 500000.0
SGU_GROUPS = 4
SGU_GC = 128
SGU_CHUNK = 128
N_MEM = 256
MEM_HEADS = 4
MEM_W = 256
LN_EPS = 1e-5
RMS_EPS = 1e-6
DN_ALPHA = (2 * DEPTH) ** 0.25

LANE = 128
SUBLANE = 8

_ORIG = (("gq", 256), ("gk", 256), ("gv", 512), ("glr", 16), ("gz", 512), ("lx", 512), ("lz", 512), ("sq", 512),
         ("sk", 128), ("sv", 128), ("sz", 512), ("su", 512), ("svv", 512), ("suz", 512), ("mq", 256), ("mz", 256),
         ("gates", 5 * D_MODEL))
_ORIG_OFF = {}
_off = 0
for _n, _w in _ORIG:
    _ORIG_OFF[_n] = (_off, _w)
    _off += _w
_PROJ_RUNS = (("gq", "gv"), ("gz", "sq"), ("sz", "mz"), ("sk", "sv"), ("glr", "glr"))
_SEG = {}
_off = 0
for _a, _b in _PROJ_RUNS:
    _names = [n for n, _ in _ORIG]
    for _n in _names[_names.index(_a):_names.index(_b) + 1]:
        _w = max(_ORIG_OFF[_n][1], 128)
        assert _off % _w == 0
        _SEG[_n] = (_off, _w)
        _off += _w
N_PROJ = -(-_off // 1024) * 1024


def _dot(a, b):
    return jnp.dot(a.astype(bf16), b.astype(bf16), preferred_element_type=f32)


def _dot_nt(a, b):
    return lax.dot_general(a.astype(bf16), b.astype(bf16), (((1,), (1,)), ((), ())), preferred_element_type=f32)


def _dot_tn(a, b):
    return lax.dot_general(a.astype(bf16), b.astype(bf16), (((0,), (0,)), ((), ())), preferred_element_type=f32)


def _sigmoid(x):
    return 1.0 / (1.0 + jnp.exp(-x))


def _silu(x):
    return x * _sigmoid(x)


def _log_sigmoid(x):
    return jnp.minimum(x, 0.0) - jnp.log1p(jnp.exp(-jnp.abs(x)))


def _layer_norm(x, g, b):
    mu = jnp.mean(x, axis=-1, keepdims=True)
    xc = x - mu
    var = jnp.mean(xc * xc, axis=-1, keepdims=True)
    return xc * lax.rsqrt(var + LN_EPS) * g + b


def _params(*sem):
    return pltpu.CompilerParams(dimension_semantics=sem)


def _ln_kernel(x_ref, g_ref, b_ref, o_ref):
    o_ref[...] = _layer_norm(x_ref[...], g_ref[...], b_ref[...])


def _ln_call(x, g, b, tm=512):
    n, d = x.shape
    tm = min(tm, n)
    return pl.pallas_call(
        _ln_kernel,
        grid=(n // tm,),
        in_specs=[pl.BlockSpec((tm, d), lambda i: (i, 0)), pl.BlockSpec((1, d), lambda i: (0, 0)),
                  pl.BlockSpec((1, d), lambda i: (0, 0))],
        out_specs=pl.BlockSpec((tm, d), lambda i: (i, 0)),
        out_shape=jax.ShapeDtypeStruct((n, d), f32),
        compiler_params=_params("parallel"),
        name="ln_in",
    )(x, g.reshape(1, d), b.reshape(1, d))


def _matmul_kernel(x_ref, w_ref, o_ref):
    o_ref[...] = _dot(x_ref[...], w_ref[...])


def _matmul_call(x, w, tm, tn, name):
    m, k = x.shape
    n = w.shape[1]
    return pl.pallas_call(
        _matmul_kernel,
        grid=(m // tm, n // tn),
        in_specs=[pl.BlockSpec((tm, k), lambda i, j: (i, 0)), pl.BlockSpec((k, tn), lambda i, j: (0, j))],
        out_specs=pl.BlockSpec((tm, tn), lambda i, j: (i, j)),
        out_shape=jax.ShapeDtypeStruct((m, n), f32),
        compiler_params=_params("parallel", "arbitrary"),
        name=name,
    )(x, w)


def _seg_spec(name, nseq, rows):
    off, width = _SEG[name]
    cb = off // width
    return pl.BlockSpec((nseq, rows, width), lambda b, c: (b, c, cb))


def _tok_spec(nseq, rows, width):
    return pl.BlockSpec((nseq, rows, width), lambda b, c: (b, c, 0))


def _const_spec(shape):
    nd = len(shape)
    return pl.BlockSpec(shape, lambda b, c: (0,) * nd)


def _lru_kernel(lx_ref, lz_ref, hist0_ref, h0_ref, cw_ref, cb_ref, wr_ref, br_ref, wi_ref, bi_ref, lam_ref,
                y_ref, hlast_ref, hist_out_ref, hist_ref, hc_ref, *, nseq, tc):
    c = pl.program_id(1)
    w = BRANCH_W

    @pl.when(c == 0)
    def _():
        hist_ref[...] = hist0_ref[...]
        hc_ref[...] = h0_ref[...]

    x = lx_ref[...]
    xfull = jnp.concatenate([hist_ref[...], x], axis=1)

    def tap(j):
        return cw_ref[j:j + 1, :].reshape(1, 1, w)

    y = cb_ref[...].reshape(1, 1, w) + x * tap(CONV_W - 1)
    for s in range(1, CONV_W):
        y = y + pltpu.roll(xfull, s, 1)[:, SUBLANE:, :] * tap(CONV_W - 1 - s)
    hist_ref[...] = xfull[:, tc:, :]
    hist_out_ref[...] = xfull[:, tc:, :]

    xc = y.reshape(nseq * tc, w)
    r = _sigmoid(_dot(xc, wr_ref[...]) + br_ref[...])
    i = _sigmoid(_dot(xc, wi_ref[...]) + bi_ref[...])
    log_a = (LRU_C * r) * _log_sigmoid(lam_ref[...])
    a = jnp.exp(log_a)
    u = jnp.sqrt(jnp.tanh(-log_a) * (a * a + 1.0)) * (i * xc)

    acc_a = a.reshape(nseq, tc, w)
    acc_u = u.reshape(nseq, tc, w)
    t = lax.broadcasted_iota(jnp.int32, (nseq, tc, w), 1)
    d = 1
    while d < tc:
        ok = t >= d
        a_sh = jnp.where(ok, pltpu.roll(acc_a, d, 1), 1.0)
        u_sh = jnp.where(ok, pltpu.roll(acc_u, d, 1), 0.0)
        acc_u = acc_a * u_sh + acc_u
        acc_a = acc_a * a_sh
        d *= 2
    h = acc_a * hc_ref[...] + acc_u
    hc_ref[...] = h[:, tc - 1:tc, :]
    hlast_ref[...] = h[:, tc - SUBLANE:, :]
    y_ref[...] = (h * _silu(lz_ref[...])).astype(y_ref.dtype)


def _lru_call(proj, hist0, h0, lw, nb, nseq, nc, tc, name):
    w = BRANCH_W
    kern = functools.partial(_lru_kernel, nseq=nseq, tc=tc)
    return pl.pallas_call(
        kern,
        grid=(nb, nc),
        in_specs=[_seg_spec("lx", nseq, tc), _seg_spec("lz", nseq, tc),
                  pl.BlockSpec((nseq, SUBLANE, w), lambda b, c: (b, 0, 0)),
                  pl.BlockSpec((nseq, 1, w), lambda b, c: (b, 0, 0)),
                  _const_spec((CONV_W, w)), _const_spec((1, w)), _const_spec((w, w)), _const_spec((1, w)),
                  _const_spec((w, w)), _const_spec((1, w)), _const_spec((1, w))],
        out_specs=[_tok_spec(nseq, tc, w),
                   pl.BlockSpec((nseq, SUBLANE, w), lambda b, c: (b, 0, 0)),
                   pl.BlockSpec((nseq, SUBLANE, w), lambda b, c: (b, 0, 0))],
        out_shape=[jax.ShapeDtypeStruct((nb * nseq, nc * tc, w), bf16),
                   jax.ShapeDtypeStruct((nb * nseq, SUBLANE, w), f32),
                   jax.ShapeDtypeStruct((nb * nseq, SUBLANE, w), f32)],
        scratch_shapes=[pltpu.VMEM((nseq, SUBLANE, w), f32), pltpu.VMEM((nseq, 1, w), f32)],
        compiler_params=_params("parallel", "arbitrary"),
        name=name,
    )(proj, proj, hist0, h0, lw["conv_w"], lw["conv_b"], lw["wr"], lw["br"], lw["wi"], lw["bi"], lw["lam"])


def _gla_consts(c):
    t = np.arange(c)[:, None]
    u = np.arange(c)[None, :]
    blocks = [u <= t, u > t]
    masks = [t == u]
    m = 1
    while m < c:
        t0 = (t // m) * m
        odd = (t // m) % 2 == 1
        blocks.append(odd & (u >= t0) & (u <= t))
        blocks.append((~odd) & (u > t) & (u <= t0 + m - 1))
        masks.append((t // (2 * m) == u // (2 * m)) & odd & ((u // m) % 2 == 0))
        m *= 2
    return (np.concatenate(blocks, 0).astype(np.float32), np.stack(masks).astype(np.float32))


def _gla_kernel(gq_ref, gk_ref, gv_ref, glr_ref, gz_ref, s0_ref, wa_ref, ba_ref, ng_ref, d_ref, m_ref,
                y_ref, sout_ref, s_ref, *, nseq, c):
    ci = pl.program_id(1)
    nlev = int(math.log2(c))
    hk = GLA_HEADS * GLA_DK

    @pl.when(ci == 0)
    def _():
        s_ref[...] = s0_ref[...]

    eye_r = lax.broadcasted_iota(jnp.int32, (GLA_DK, GLA_DK), 0)
    eye_c = lax.broadcasted_iota(jnp.int32, (GLA_DK, GLA_DK), 1)
    eye = eye_r == eye_c

    for n in range(nseq):
        q = gq_ref[n] * (GLA_DK ** -0.5)
        k = gk_ref[n]
        v = gv_ref[n]
        z = _dot(glr_ref[n], wa_ref[...]) + ba_ref[...]
        la = _log_sigmoid(z) * (1.0 / GLA_TAU)
        hi = la.astype(bf16)
        r1 = la - hi.astype(f32)
        mid = r1.astype(bf16)
        lo = (r1 - mid.astype(f32)).astype(bf16)
        hml = jnp.concatenate([hi, mid, lo], axis=1)

        def sums(blk):
            p = jnp.dot(d_ref[blk * c:(blk + 1) * c, :].astype(bf16), hml, preferred_element_type=f32)
            return p[:, :hk] + p[:, hk:2 * hk] + p[:, 2 * hk:]

        b = sums(0)
        q_in = q * jnp.exp(b)
        k_st = k * jnp.exp(sums(1))
        dec_row = jnp.exp(b[c - 1:c, :])
        qf = [q]
        kf = [k]
        for lev in range(nlev):
            qf.append(q * jnp.exp(sums(2 + 2 * lev)))
            kf.append(k * jnp.exp(sums(3 + 2 * lev)))

        for h in range(GLA_HEADS):
            ks = slice(h * GLA_DK, (h + 1) * GLA_DK)
            vs = slice(h * GLA_DV, (h + 1) * GLA_DV)
            att = jnp.zeros((c, c), f32)
            for lev in range(nlev + 1):
                att = att + m_ref[lev] * _dot_nt(qf[lev][:, ks], kf[lev][:, ks])
            s_h = s_ref[n, h]
            v_h = v[:, vs]
            o = _dot(q_in[:, ks], s_h) + _dot(att, v_h)
            dec_col = jnp.sum(jnp.where(eye, jnp.broadcast_to(dec_row[:, ks], (GLA_DK, GLA_DK)), 0.0),
                              axis=1, keepdims=True)
            s_ref[n, h] = s_h * dec_col + _dot_tn(k_st[:, ks], v_h)
            o = o * lax.rsqrt(jnp.mean(o * o, axis=-1, keepdims=True) + RMS_EPS) * ng_ref[...]
            y_ref[n, :, vs] = (o * _silu(gz_ref[n, :, vs])).astype(y_ref.dtype)

    sout_ref[...] = s_ref[...]


def _gla_call(proj, s0, gw, nb, nseq, nc, c, name):
    dstack, masks = _gla_consts(c)
    kern = functools.partial(_gla_kernel, nseq=nseq, c=c)
    hk = GLA_HEADS * GLA_DK
    st_spec = pl.BlockSpec((nseq, GLA_HEADS, GLA_DK, GLA_DV), lambda b, ci: (b, 0, 0, 0))
    return pl.pallas_call(
        kern,
        grid=(nb, nc),
        in_specs=[_seg_spec("gq", nseq, c), _seg_spec("gk", nseq, c), _seg_spec("gv", nseq, c),
                  _seg_spec("glr", nseq, c), _seg_spec("gz", nseq, c), st_spec,
                  _const_spec((LANE, hk)), _const_spec((1, hk)), _const_spec((1, GLA_DV)),
                  _const_spec(dstack.shape), _const_spec(masks.shape)],
        out_specs=[_tok_spec(nseq, c, BRANCH_W), st_spec],
        out_shape=[jax.ShapeDtypeStruct((nb * nseq, nc * c, BRANCH_W), bf16),
                   jax.ShapeDtypeStruct((nb * nseq, GLA_HEADS, GLA_DK, GLA_DV), f32)],
        scratch_shapes=[pltpu.VMEM((nseq, GLA_HEADS, GLA_DK, GLA_DV), f32)],
        compiler_params=_params("parallel", "arbitrary"),
        name=name,
    )(proj, proj, proj, proj, proj, s0, gw["wa"], gw["ba"], gw["ng"], jnp.asarray(dstack), jnp.asarray(masks))


def _rope_tables(pos0, t):
    half = ROT_DIM // 2
    inv = ROPE_THETA ** (-jnp.arange(half, dtype=f32) / half)
    ang = (pos0 + jnp.arange(t)).astype(f32)[:, None] * inv[None, :]
    cos, sin = jnp.cos(ang), jnp.sin(ang)
    zeros = jnp.zeros((t, HEAD_DIM - ROT_DIM), f32)
    z8 = jnp.zeros((t, half), f32)
    c_tab = jnp.concatenate([cos, cos, jnp.ones((t, HEAD_DIM - ROT_DIM), f32)], axis=1)
    sa_tab = jnp.concatenate([-sin, z8, zeros], axis=1)
    sb_tab = jnp.concatenate([z8, sin, zeros], axis=1)
    return tuple(jnp.concatenate([x, x], axis=1) for x in (c_tab, sa_tab, sb_tab))


def _rope(x, c_tab, sa_tab, sb_tab):
    wd = x.shape[1]
    rep = wd // LANE
    half = ROT_DIM // 2
    if rep > 1:
        c_tab, sa_tab, sb_tab = (jnp.concatenate([tb] * rep, axis=1) for tb in (c_tab, sa_tab, sb_tab))
    return x * c_tab + pltpu.roll(x, wd - half, 1) * sa_tab + pltpu.roll(x, half, 1) * sb_tab


def _swa_kernel(sink_ref, sq_ref, sk_ref, sv_ref, sz_ref, ct_ref, sat_ref, sbt_ref, kp_ref, vp_ref,
                y_ref, klast_ref, vlast_ref, kprev_ref, vprev_ref, *, nseq, qb, pos0, carry):
    blk = pl.program_id(1)

    @pl.when(blk == 0)
    def _():
        kprev_ref[...] = kp_ref[...]
        vprev_ref[...] = vp_ref[...]

    tabs = (ct_ref[...], sat_ref[...], sbt_ref[...])
    mrows = SWA_GROUP * qb
    qi = lax.broadcasted_iota(jnp.int32, (mrows, WINDOW), 0) % qb
    kj = lax.broadcasted_iota(jnp.int32, (mrows, WINDOW), 1)
    past_ok = (kj >= qi) & (kj >= (WINDOW - pos0) - blk * qb)
    qi_c = lax.broadcasted_iota(jnp.int32, (mrows, qb), 0) % qb
    kj_c = lax.broadcasted_iota(jnp.int32, (mrows, qb), 1)
    cur_ok = kj_c <= qi_c
    neg = -jnp.inf

    for n in range(nseq):
        q = _rope(sq_ref[n], *tabs) * (HEAD_DIM ** -0.5)
        k = _rope(sk_ref[n], *tabs)
        v = sv_ref[n]
        kprev = kprev_ref[n]
        vprev = vprev_ref[n]
        klast_ref[n] = k if qb == WINDOW else jnp.concatenate([kprev[qb:], k], axis=0)
        vlast_ref[n] = v if qb == WINDOW else jnp.concatenate([vprev[qb:], v], axis=0)
        outs = []
        for kv in range(SWA_KV_HEADS):
            ds = slice(kv * HEAD_DIM, (kv + 1) * HEAD_DIM)
            qs = jnp.concatenate(
                [q[:, (kv * SWA_GROUP + g) * HEAD_DIM:(kv * SWA_GROUP + g + 1) * HEAD_DIM] for g in range(SWA_GROUP)],
                axis=0)
            sink = jnp.concatenate(
                [jnp.full((qb, 1), sink_ref[kv * SWA_GROUP + g], f32) for g in range(SWA_GROUP)], axis=0)
            s_p = jnp.where(past_ok, _dot_nt(qs, kprev[:, ds]), neg)
            s_c = jnp.where(cur_ok, _dot_nt(qs, k[:, ds]), neg)
            m = jnp.maximum(jnp.maximum(jnp.max(s_p, axis=1, keepdims=True), jnp.max(s_c, axis=1, keepdims=True)),
                            sink)
            p_p = jnp.exp(s_p - m)
            p_c = jnp.exp(s_c - m)
            den = jnp.sum(p_p, axis=1, keepdims=True) + jnp.sum(p_c, axis=1, keepdims=True) + jnp.exp(sink - m)
            o = (_dot(p_p, vprev[:, ds]) + _dot(p_c, v[:, ds])) / den
            outs.extend(o[g * qb:(g + 1) * qb, :] for g in range(SWA_GROUP))
        o_all = jnp.concatenate(outs, axis=1)
        y_ref[n] = (o_all * _silu(sz_ref[n])).astype(y_ref.dtype)
        if carry:
            kprev_ref[n] = k
            vprev_ref[n] = v


def _swa_call(proj, sinks, k_past, v_past, pos0, nb, nseq, nc, qb, name):
    assert nc == 1 or qb == WINDOW
    t_total = nc * qb
    c_tab, sa_tab, sb_tab = _rope_tables(pos0, t_total)
    kern = functools.partial(_swa_kernel, nseq=nseq, qb=qb, pos0=pos0, carry=nc > 1)
    kvw = SWA_KV_HEADS * HEAD_DIM
    tab_spec = pl.BlockSpec((qb, LANE), lambda b, c: (c, 0))
    past_spec = pl.BlockSpec((nseq, WINDOW, kvw), lambda b, c: (b, 0, 0))
    return pl.pallas_call(
        kern,
        grid=(nb, nc),
        in_specs=[pl.BlockSpec(memory_space=pltpu.SMEM),
                  _seg_spec("sq", nseq, qb), _seg_spec("sk", nseq, qb), _seg_spec("sv", nseq, qb),
                  _seg_spec("sz", nseq, qb), tab_spec, tab_spec, tab_spec, past_spec, past_spec],
        out_specs=[_tok_spec(nseq, qb, BRANCH_W), past_spec, past_spec],
        out_shape=[jax.ShapeDtypeStruct((nb * nseq, nc * qb, BRANCH_W), bf16),
                   jax.ShapeDtypeStruct((nb * nseq, WINDOW, kvw), f32),
                   jax.ShapeDtypeStruct((nb * nseq, WINDOW, kvw), f32)],
        scratch_shapes=[pltpu.VMEM((nseq, WINDOW, kvw), f32), pltpu.VMEM((nseq, WINDOW, kvw), f32)],
        compiler_params=_params("parallel", "arbitrary"),
        name=name,
    )(sinks, proj, proj, proj, proj, c_tab, sa_tab, sb_tab, k_past, v_past)


def _sgu_kernel(su_ref, sv_ref, sz_ref, g_ref, b_ref, wm_ref, bias_ref, y_ref, *vn_ref, ntile):
    vn = _layer_norm(sv_ref[...], g_ref[...], b_ref[...])
    if vn_ref:
        vn_ref[0][...] = vn
    for r in range(ntile):
        rows = slice(r * SGU_CHUNK, (r + 1) * SGU_CHUNK)
        mixed = jnp.concatenate(
            [jnp.dot(wm_ref[g], vn[rows, g * SGU_GC:(g + 1) * SGU_GC].astype(bf16), preferred_element_type=f32)
             for g in range(SGU_GROUPS)], axis=1)
        y = su_ref[rows, :] * (mixed + bias_ref[...]) * _silu(sz_ref[rows, :])
        y_ref[rows, :] = y.astype(y_ref.dtype)


def _sgu_call(proj, ln_g, ln_b, wmix, bias, n_tok, ntile, want_vn, name):
    rows = ntile * SGU_CHUNK
    w = BRANCH_W
    kern = functools.partial(_sgu_kernel, ntile=ntile)

    def seg(nm):
        off, width = _SEG[nm]
        return pl.BlockSpec((rows, width), lambda i: (i, off // width))

    def const(shape):
        return pl.BlockSpec(shape, lambda i: (0,) * len(shape))

    out_specs = [pl.BlockSpec((rows, w), lambda i: (i, 0))]
    out_shape = [jax.ShapeDtypeStruct((n_tok, w), bf16)]
    if want_vn:
        out_specs.append(pl.BlockSpec((rows, w), lambda i: (i, 0)))
        out_shape.append(jax.ShapeDtypeStruct((n_tok, w), f32))
    return pl.pallas_call(
        kern,
        grid=(n_tok // rows,),
        in_specs=[seg("su"), seg("svv"), seg("suz"), const((1, w)), const((1, w)),
                  const((SGU_GROUPS, SGU_CHUNK, SGU_CHUNK)), const((SGU_CHUNK, w))],
        out_specs=out_specs,
        out_shape=out_shape,
        compiler_params=_params("parallel"),
        name=name,
    )(proj, proj, proj, ln_g, ln_b, wmix, bias)


def _mem_kernel(mq_ref, mz_ref, mk_ref, mv_ref, y_ref, *, nseq, tq):
    for n in range(nseq):
        q = mq_ref[n] * (HEAD_DIM ** -0.5)
        mk = mk_ref[n]
        mv = mv_ref[n]
        outs = []
        for h in range(MEM_HEADS):
            ds = slice(h * HEAD_DIM, (h + 1) * HEAD_DIM)
            s = _dot_nt(q[:, ds], mk[:, ds])
            m = jnp.max(s, axis=1, keepdims=True)
            p = jnp.exp(s - m)
            den = jnp.sum(p, axis=1, keepdims=True)
            outs.append(_dot(p, mv[:, ds]) / den)
        o = jnp.concatenate(outs, axis=1)
        y_ref[n] = (o * _silu(mz_ref[n])).astype(y_ref.dtype)


def _mem_call(proj, mk, mv, nb, nseq, nc, tq, name):
    kern = functools.partial(_mem_kernel, nseq=nseq, tq=tq)
    kv_spec = pl.BlockSpec((nseq, N_MEM, MEM_W), lambda b, c: (b, 0, 0))
    return pl.pallas_call(
        kern,
        grid=(nb, nc),
        in_specs=[_seg_spec("mq", nseq, tq), _seg_spec("mz", nseq, tq), kv_spec, kv_spec],
        out_specs=_tok_spec(nseq, tq, MEM_W),
        out_shape=jax.ShapeDtypeStruct((nb * nseq, nc * tq, MEM_W), bf16),
        compiler_params=_params("parallel", "arbitrary"),
        name=name,
    )(proj, proj, mk, mv)


def _merge_kernel(yg_ref, yl_ref, ys_ref, yu_ref, ym_ref, x_ref, wg_ref, wb_ref, wm_ref, wo_ref, g_ref, b_ref,
                  o_ref):
    d = D_MODEL
    x = x_ref[...]
    xb = x.astype(bf16)

    def gate(n):
        return _sigmoid(jnp.dot(xb, wg_ref[:, n * d:(n + 1) * d], preferred_element_type=f32))

    merged = gate(4) * jnp.dot(ym_ref[...], wm_ref[...], preferred_element_type=f32)
    for n, y_ref in enumerate((yg_ref, yl_ref, ys_ref, yu_ref)):
        merged = merged + gate(n) * jnp.dot(y_ref[...], wb_ref[n], preferred_element_type=f32)
    out = _dot(merged, wo_ref[...])
    o_ref[...] = _layer_norm(DN_ALPHA * x + out, g_ref[...], b_ref[...])


def _merge_call(ys, x, mw, tm, name):
    n_tok, d = x.shape
    w = BRANCH_W

    def rows(width):
        return pl.BlockSpec((tm, width), lambda i: (i, 0))

    def const(shape):
        return pl.BlockSpec(shape, lambda i: (0,) * len(shape), pipeline_mode=pl.Buffered(1))

    return pl.pallas_call(
        _merge_kernel,
        grid=(n_tok // tm,),
        in_specs=[rows(w), rows(w), rows(w), rows(w), rows(MEM_W), rows(d),
                  const((d, 5 * d)), const((4, w, d)), const((MEM_W, d)), const((d, d)), const((1, d)),
                  const((1, d))],
        out_specs=rows(d),
        out_shape=jax.ShapeDtypeStruct((n_tok, d), f32),
        compiler_params=_params("parallel"),
        name=name,
    )(*ys, x, mw["wg"], mw["wb"], mw["wm"], mw["wo"], mw["g"], mw["b"])


def _prep_layer(l, w_in, gla_wa2, gla_ba, gla_norm_g, lru_conv_w, lru_conv_b, lru_wr, lru_br, lru_wi, lru_bi, lru_L,
                swa_sinks, sgu_ln_g, sgu_ln_b, sgu_w, sgu_b, w_mem_kv, w_branch, w_branch_mem, w_out, ln_g, ln_b):
    d = D_MODEL
    w = BRANCH_W
    cols = []
    used = 0
    for a, b in _PROJ_RUNS:
        lo = _ORIG_OFF[a][0]
        hi = _ORIG_OFF[b][0] + _ORIG_OFF[b][1]
        cols.append(w_in[l][:, lo:hi].astype(bf16))
        used += hi - lo
    cols.append(jnp.zeros((d, N_PROJ - used), bf16))
    w_proj = jnp.concatenate(cols, axis=1)
    g_lo = _ORIG_OFF["gates"][0]
    w_gates = w_in[l][:, g_lo:].astype(bf16)

    def block_diag(wb):
        eye = jnp.eye(LRU_BLOCKS, dtype=f32)
        return (eye[:, None, :, None] * wb[:, :, None, :]).reshape(w, w).astype(bf16)

    tril = jnp.tril(jnp.ones((SGU_CHUNK, SGU_CHUNK), f32))
    wmix_p = (sgu_w[l] * tril).astype(bf16)
    bias_p = jnp.repeat(sgu_b[l].T, SGU_GC, axis=1)
    t8 = SUBLANE
    rep = SGU_CHUNK // t8
    w8 = (sgu_w[l] * tril)[:, :t8, :t8]
    seq_eye = jnp.eye(rep, dtype=f32)
    wmix_s = (seq_eye[None, :, None, :, None] * w8[:, None, :, None, :]).reshape(
        SGU_GROUPS, SGU_CHUNK, SGU_CHUNK).astype(bf16)
    bias_s = jnp.tile(bias_p[:t8], (rep, 1))
    return dict(
        w_proj=w_proj,
        w_mem_kv=w_mem_kv[l].astype(bf16),
        gla=dict(wa=jnp.pad(gla_wa2[l], ((0, LANE - GLA_RANK), (0, 0))).astype(bf16),
                 ba=gla_ba[l].reshape(1, -1), ng=gla_norm_g[l].reshape(1, -1)),
        lru=dict(conv_w=lru_conv_w[l], conv_b=lru_conv_b[l].reshape(1, w), wr=block_diag(lru_wr[l]),
                 br=lru_br[l].reshape(1, w), wi=block_diag(lru_wi[l]), bi=lru_bi[l].reshape(1, w),
                 lam=lru_L[l].reshape(1, w)),
        sinks=swa_sinks[l],
        sgu=dict(g=sgu_ln_g[l].reshape(1, w), b=sgu_ln_b[l].reshape(1, w), wmix_p=wmix_p, bias_p=bias_p,
                 wmix_s=wmix_s, bias_s=bias_s),
        merge=dict(wg=w_gates, wb=w_branch[l].astype(bf16), wm=w_branch_mem[l].astype(bf16), wo=w_out[l].astype(bf16),
                   g=ln_g[l].reshape(1, d), b=ln_b[l].reshape(1, d)),
    )


def _layer(x, lw, grp, st, tag):
    nseq_total, t = grp["batch"], grp["seq"]
    n_tok = nseq_total * t
    proj = _matmul_call(x, lw["w_proj"], min(1024, n_tok), 1024, "proj_" + tag)
    proj3 = proj.reshape(nseq_total, t, N_PROJ)

    lt = grp["lru"]
    y_lru, hlast, hist = _lru_call(proj3, st["hist0"], st["h0"], lw["lru"], nseq_total // lt[0], lt[0], t // lt[1],
                                   lt[1], "lru_" + tag)
    gt = grp["gla"]
    y_gla, s_out = _gla_call(proj3, st["gla0"], lw["gla"], nseq_total // gt[0], gt[0], t // gt[1], gt[1],
                             "gla_" + tag)
    wt = grp["swa"]
    y_swa, k_last, v_last = _swa_call(proj3, lw["sinks"], st["k_past"], st["v_past"], grp["pos0"],
                                      nseq_total // wt[0], wt[0], t // wt[1], wt[1], "swa_" + tag)
    sg = lw["sgu"]
    sgu_out = _sgu_call(proj, sg["g"], sg["b"], sg["wmix_" + grp["kind"]], sg["bias_" + grp["kind"]], n_tok,
                        grp["sgu_tiles"], grp["kind"] == "s", "sgu_" + tag)
    mt = grp["mem"]
    y_mem = _mem_call(proj3, st["mk"], st["mv"], nseq_total // mt[0], mt[0], t // mt[1], mt[1], "mem_" + tag)
    ys = tuple(y.reshape(n_tok, y.shape[-1]) for y in (y_gla, y_lru, y_swa, sgu_out[0], y_mem))
    x_new = _merge_call(ys, x, lw["merge"], min(256, n_tok), "merge_" + tag)
    return x_new, dict(gla=s_out, hlast=hlast, hist=hist, k_last=k_last, v_last=v_last,
                       vn=sgu_out[1] if len(sgu_out) > 1 else None)


_PROMPT = dict(kind="p", pos0=0, lru=(1, 256), gla=(2, 128), swa=(1, 128), sgu_tiles=4, mem=(1, 512))
_SAMPLE = dict(kind="s", pos0=PAST_LEN, lru=(32, 8), gla=(8, 8), swa=(8, 8), sgu_tiles=8, mem=(8, 8))


def kernel(x_prompt, x_sample, mem_prompt, state_gla, state_lru_h, state_lru_conv, cache_swa_k, cache_swa_v,
           cache_mem_k, cache_mem_v, ln_in_g, ln_in_b, w_in, gla_wa2, gla_ba, gla_norm_g, lru_conv_w, lru_conv_b,
           lru_wr, lru_br, lru_wi, lru_bi, lru_L, swa_sinks, sgu_ln_g, sgu_ln_b, sgu_w, sgu_b, w_mem_kv, w_branch,
           w_branch_mem, w_out, ln_g, ln_b):
    bp, tp, d = x_prompt.shape
    bs, ts, _ = x_sample.shape
    w = BRANCH_W
    kvw = SWA_KV_HEADS * HEAD_DIM
    gp = dict(_PROMPT, batch=bp, seq=tp)
    gs = dict(_SAMPLE, batch=bs, seq=ts)

    xp = _ln_call(x_prompt.reshape(bp * tp, d), ln_in_g, ln_in_b)
    xs = _ln_call(x_sample.reshape(bs * ts, d), ln_in_g, ln_in_b)
    mem2 = mem_prompt.reshape(bp * N_MEM, d)

    outs_p, outs_s, mks, mvs = [], [], [], []
    for l in range(DEPTH):
        lw = _prep_layer(l, w_in, gla_wa2, gla_ba, gla_norm_g, lru_conv_w, lru_conv_b, lru_wr, lru_br, lru_wi,
                         lru_bi, lru_L, swa_sinks, sgu_ln_g, sgu_ln_b, sgu_w, sgu_b, w_mem_kv, w_branch,
                         w_branch_mem, w_out, ln_g, ln_b)
        mkv = _matmul_call(mem2, lw["w_mem_kv"], bp * N_MEM, 2 * MEM_W, "memkv_%d" % l)
        mk = mkv[:, :MEM_W].reshape(bp, N_MEM, MEM_W)
        mv = mkv[:, MEM_W:].reshape(bp, N_MEM, MEM_W)
        st_p = dict(hist0=jnp.zeros((bp, SUBLANE, w), f32), h0=jnp.zeros((bp, 1, w), f32),
                    gla0=jnp.zeros((bp, GLA_HEADS, GLA_DK, GLA_DV), f32),
                    k_past=jnp.zeros((bp, WINDOW, kvw), f32), v_past=jnp.zeros((bp, WINDOW, kvw), f32),
                    mk=mk, mv=mv)
        st_s = dict(hist0=jnp.pad(state_lru_conv[l], ((0, 0), (SUBLANE - (CONV_W - 1), 0), (0, 0))),
                    h0=state_lru_h[l][:, None, :], gla0=state_gla[l],
                    k_past=cache_swa_k[l].reshape(bs, WINDOW, kvw), v_past=cache_swa_v[l].reshape(bs, WINDOW, kvw),
                    mk=cache_mem_k[l].reshape(bs, N_MEM, MEM_W), mv=cache_mem_v[l].reshape(bs, N_MEM, MEM_W))
        xp, op = _layer(xp, lw, gp, st_p, "p%d" % l)
        xs, os_ = _layer(xs, lw, gs, st_s, "s%d" % l)
        outs_p.append(op)
        outs_s.append(os_)
        mks.append(mk.reshape(bp, N_MEM, MEM_HEADS, HEAD_DIM))
        mvs.append(mv.reshape(bp, N_MEM, MEM_HEADS, HEAD_DIM))

    def stack(outs, fn):
        return jnp.stack([fn(o) for o in outs])

    def window(a):
        return a.reshape(a.shape[0], WINDOW, SWA_KV_HEADS, HEAD_DIM)

    return (
        xp.reshape(bp, tp, d), xs.reshape(bs, ts, d),
        stack(outs_p, lambda o: o["gla"]), stack(outs_s, lambda o: o["gla"]),
        stack(outs_p, lambda o: o["hlast"][:, SUBLANE - 1]), stack(outs_s, lambda o: o["hlast"][:, SUBLANE - 1]),
        stack(outs_p, lambda o: o["hist"][:, SUBLANE - (CONV_W - 1):]),
        stack(outs_s, lambda o: o["hist"][:, SUBLANE - (CONV_W - 1):]),
        stack(outs_p, lambda o: window(o["k_last"])), stack(outs_s, lambda o: window(o["k_last"])),
        stack(outs_p, lambda o: window(o["v_last"])), stack(outs_s, lambda o: window(o["v_last"])),
        jnp.stack(mks), jnp.stack(mvs),
        stack(outs_s, lambda o: o["vn"].reshape(bs, ts, w)),
    )
```

```python
import functools
import math

import jax
import jax.numpy as jnp
import numpy as np
from jax import lax
from jax.experimental import pallas as pl
from jax.experimental.pallas import tpu as pltpu

f32 = jnp.float32
bf16 = jnp.bfloat16

D_MODEL = 1024
DEPTH = 2
PAST_LEN = 8192
BRANCH_W = 512
GLA_HEADS = 4
GLA_DK = 64
GLA_DV = 128
GLA_RANK = 16
GLA_TAU = 16.0
LRU_BLOCKS = 8
LRU_BS = 64
CONV_W = 4
LRU_C = 8.0
HEAD_DIM = 64
SWA_HEADS = 8
SWA_KV_HEADS = 2
SWA_GROUP = 4
WINDOW = 128
ROT_DIM = 16
ROPE_THETA = 500000.0
SGU_GROUPS = 4
SGU_GC = 128
SGU_CHUNK = 128
N_MEM = 256
MEM_HEADS = 4
MEM_W = 256
LN_EPS = 1e-5
RMS_EPS = 1e-6
DN_ALPHA = (2 * DEPTH) ** 0.25

LANE = 128
SUBLANE = 8

_ORIG = (("gq", 256), ("gk", 256), ("gv", 512), ("glr", 16), ("gz", 512), ("lx", 512), ("lz", 512), ("sq", 512),
         ("sk", 128), ("sv", 128), ("sz", 512), ("su", 512), ("svv", 512), ("suz", 512), ("mq", 256), ("mz", 256),
         ("gates", 5 * D_MODEL))
_ORIG_OFF = {}
_off = 0
for _n, _w in _ORIG:
    _ORIG_OFF[_n] = (_off, _w)
    _off += _w
_PROJ_RUNS = (("gq", "gv"), ("gz", "sq"), ("sz", "mz"), ("sk", "sv"), ("glr", "glr"))
_SEG = {}
_off = 0
for _a, _b in _PROJ_RUNS:
    _names = [n for n, _ in _ORIG]
    for _n in _names[_names.index(_a):_names.index(_b) + 1]:
        _w = max(_ORIG_OFF[_n][1], 128)
        assert _off % _w == 0
        _SEG[_n] = (_off, _w)
        _off += _w
N_PROJ = -(-_off // 1024) * 1024


def _dot(a, b):
    return jnp.dot(a.astype(bf16), b.astype(bf16), preferred_element_type=f32)


def _dot_nt(a, b):
    return lax.dot_general(a.astype(bf16), b.astype(bf16), (((1,), (1,)), ((), ())), preferred_element_type=f32)


def _dot_tn(a, b):
    return lax.dot_general(a.astype(bf16), b.astype(bf16), (((0,), (0,)), ((), ())), preferred_element_type=f32)


def _sigmoid(x):
    return 1.0 / (1.0 + jnp.exp(-x))


def _silu(x):
    return x * _sigmoid(x)


def _log_sigmoid(x):
    return jnp.minimum(x, 0.0) - jnp.log1p(jnp.exp(-jnp.abs(x)))


def _layer_norm(x, g, b):
    mu = jnp.mean(x, axis=-1, keepdims=True)
    xc = x - mu
    var = jnp.mean(xc * xc, axis=-1, keepdims=True)
    return xc * lax.rsqrt(var + LN_EPS) * g + b


def _params(*sem):
    return pltpu.CompilerParams(dimension_semantics=sem)


def _ln_kernel(x_ref, g_ref, b_ref, o_ref):
    o_ref[...] = _layer_norm(x_ref[...], g_ref[...], b_ref[...])


def _ln_call(x, g, b, tm=512):
    n, d = x.shape
    tm = min(tm, n)
    return pl.pallas_call(
        _ln_kernel,
        grid=(n // tm,),
        in_specs=[pl.BlockSpec((tm, d), lambda i: (i, 0)), pl.BlockSpec((1, d), lambda i: (0, 0)),
                  pl.BlockSpec((1, d), lambda i: (0, 0))],
        out_specs=pl.BlockSpec((tm, d), lambda i: (i, 0)),
        out_shape=jax.ShapeDtypeStruct((n, d), f32),
        compiler_params=_params("parallel"),
        name="ln_in",
    )(x, g.reshape(1, d), b.reshape(1, d))


def _matmul_kernel(x_ref, w_ref, o_ref):
    o_ref[...] = _dot(x_ref[...], w_ref[...])


def _matmul_call(x, w, tm, tn, name):
    m, k = x.shape
    n = w.shape[1]
    return pl.pallas_call(
        _matmul_kernel,
        grid=(m // tm, n // tn),
        in_specs=[pl.BlockSpec((tm, k), lambda i, j: (i, 0)), pl.BlockSpec((k, tn), lambda i, j: (0, j))],
        out_specs=pl.BlockSpec((tm, tn), lambda i, j: (i, j)),
        out_shape=jax.ShapeDtypeStruct((m, n), f32),
        compiler_params=_params("parallel", "arbitrary"),
        name=name,
    )(x, w)


def _seg_spec(name, nseq, rows):
    off, width = _SEG[name]
    cb = off // width
    return pl.BlockSpec((nseq, rows, width), lambda b, c: (b, c, cb))


def _tok_spec(nseq, rows, width):
    return pl.BlockSpec((nseq, rows, width), lambda b, c: (b, c, 0))


def _const_spec(shape):
    nd = len(shape)
    return pl.BlockSpec(shape, lambda b, c: (0,) * nd)


def _lru_kernel(lx_ref, lz_ref, hist0_ref, h0_ref, cw_ref, cb_ref, wr_ref, br_ref, wi_ref, bi_ref, lam_ref,
                y_ref, hlast_ref, hist_out_ref, hist_ref, hc_ref, *, nseq, tc):
    c = pl.program_id(1)
    w = BRANCH_W

    @pl.when(c == 0)
    def _():
        hist_ref[...] = hist0_ref[...]
        hc_ref[...] = h0_ref[...]

    x = lx_ref[...]
    xfull = jnp.concatenate([hist_ref[...], x], axis=1)

    def tap(j):
        return cw_ref[j:j + 1, :].reshape(1, 1, w)

    y = cb_ref[...].reshape(1, 1, w) + x * tap(CONV_W - 1)
    for s in range(1, CONV_W):
        y = y + pltpu.roll(xfull, s, 1)[:, SUBLANE:, :] * tap(CONV_W - 1 - s)
    hist_ref[...] = xfull[:, tc:, :]
    hist_out_ref[...] = xfull[:, tc:, :]

    xc = y.reshape(nseq * tc, w)
    r = _sigmoid(_dot(xc, wr_ref[...]) + br_ref[...])
    i = _sigmoid(_dot(xc, wi_ref[...]) + bi_ref[...])
    log_a = (LRU_C * r) * _log_sigmoid(lam_ref[...])
    a = jnp.exp(log_a)
    u = jnp.sqrt(jnp.tanh(-log_a) * (a * a + 1.0)) * (i * xc)

    acc_a = a.reshape(nseq, tc, w)
    acc_u = u.reshape(nseq, tc, w)
    t = lax.broadcasted_iota(jnp.int32, (nseq, tc, w), 1)
    d = 1
    while d < tc:
        ok = t >= d
        a_sh = jnp.where(ok, pltpu.roll(acc_a, d, 1), 1.0)
        u_sh = jnp.where(ok, pltpu.roll(acc_u, d, 1), 0.0)
        acc_u = acc_a * u_sh + acc_u
        acc_a = acc_a * a_sh
        d *= 2
    h = acc_a * hc_ref[...] + acc_u
    hc_ref[...] = h[:, tc - 1:tc, :]
    hlast_ref[...] = h[:, tc - SUBLANE:, :]
    y_ref[...] = (h * _silu(lz_ref[...])).astype(y_ref.dtype)


def _lru_call(proj, hist0, h0, lw, nb, nseq, nc, tc, name):
    w = BRANCH_W
    kern = functools.partial(_lru_kernel, nseq=nseq, tc=tc)
    return pl.pallas_call(
        kern,
        grid=(nb, nc),
        in_specs=[_seg_spec("lx", nseq, tc), _seg_spec("lz", nseq, tc),
                  pl.BlockSpec((nseq, SUBLANE, w), lambda b, c: (b, 0, 0)),
                  pl.BlockSpec((nseq, 1, w), lambda b, c: (b, 0, 0)),
                  _const_spec((CONV_W, w)), _const_spec((1, w)), _const_spec((w, w)), _const_spec((1, w)),
                  _const_spec((w, w)), _const_spec((1, w)), _const_spec((1, w))],
        out_specs=[_tok_spec(nseq, tc, w),
                   pl.BlockSpec((nseq, SUBLANE, w), lambda b, c: (b, 0, 0)),
                   pl.BlockSpec((nseq, SUBLANE, w), lambda b, c: (b, 0, 0))],
        out_shape=[jax.ShapeDtypeStruct((nb * nseq, nc * tc, w), bf16),
                   jax.ShapeDtypeStruct((nb * nseq, SUBLANE, w), f32),
                   jax.ShapeDtypeStruct((nb * nseq, SUBLANE, w), f32)],
        scratch_shapes=[pltpu.VMEM((nseq, SUBLANE, w), f32), pltpu.VMEM((nseq, 1, w), f32)],
        compiler_params=_params("parallel", "arbitrary"),
        name=name,
    )(proj, proj, hist0, h0, lw["conv_w"], lw["conv_b"], lw["wr"], lw["br"], lw["wi"], lw["bi"], lw["lam"])


def _gla_consts(c):
    t = np.arange(c)[:, None]
    u = np.arange(c)[None, :]
    blocks = [u <= t, u > t]
    masks = [t == u]
    m = 1
    while m < c:
        t0 = (t // m) * m
        odd = (t // m) % 2 == 1
        blocks.append(odd & (u >= t0) & (u <= t))
        blocks.append((~odd) & (u > t) & (u <= t0 + m - 1))
        masks.append((t // (2 * m) == u // (2 * m)) & odd & ((u // m) % 2 == 0))
        m *= 2
    return (np.concatenate(blocks, 0).astype(np.float32), np.stack(masks).astype(np.float32))


def _gla_kernel(gq_ref, gk_ref, gv_ref, glr_ref, gz_ref, s0_ref, wa_ref, ba_ref, ng_ref, d_ref, m_ref,
                y_ref, sout_ref, s_ref, *, nseq, c):
    ci = pl.program_id(1)
    nlev = int(math.log2(c))
    hk = GLA_HEADS * GLA_DK

    @pl.when(ci == 0)
    def _():
        s_ref[...] = s0_ref[...]

    eye_r = lax.broadcasted_iota(jnp.int32, (GLA_DK, GLA_DK), 0)
    eye_c = lax.broadcasted_iota(jnp.int32, (GLA_DK, GLA_DK), 1)
    eye = eye_r == eye_c

    for n in range(nseq):
        q = gq_ref[n] * (GLA_DK ** -0.5)
        k = gk_ref[n]
        v = gv_ref[n]
        z = _dot(glr_ref[n], wa_ref[...]) + ba_ref[...]
        la = _log_sigmoid(z) * (1.0 / GLA_TAU)
        hi = la.astype(bf16)
        r1 = la - hi.astype(f32)
        mid = r1.astype(bf16)
        lo = (r1 - mid.astype(f32)).astype(bf16)
        hml = jnp.concatenate([hi, mid, lo], axis=1)

        def sums(blk):
            p = jnp.dot(d_ref[blk * c:(blk + 1) * c, :].astype(bf16), hml, preferred_element_type=f32)
            return p[:, :hk] + p[:, hk:2 * hk] + p[:, 2 * hk:]

        b = sums(0)
        q_in = q * jnp.exp(b)
        k_st = k * jnp.exp(sums(1))
        dec_row = jnp.exp(b[c - 1:c, :])
        qf = [q]
        kf = [k]
        for lev in range(nlev):
            qf.append(q * jnp.exp(sums(2 + 2 * lev)))
            kf.append(k * jnp.exp(sums(3 + 2 * lev)))

        for h in range(GLA_HEADS):
            ks = slice(h * GLA_DK, (h + 1) * GLA_DK)
            vs = slice(h * GLA_DV, (h + 1) * GLA_DV)
            att = jnp.zeros((c, c), f32)
            for lev in range(nlev + 1):
                att = att + m_ref[lev] * _dot_nt(qf[lev][:, ks], kf[lev][:, ks])
            s_h = s_ref[n, h]
            v_h = v[:, vs]
            o = _dot(q_in[:, ks], s_h) + _dot(att, v_h)
            dec_col = jnp.sum(jnp.where(eye, jnp.broadcast_to(dec_row[:, ks], (GLA_DK, GLA_DK)), 0.0),
                              axis=1, keepdims=True)
            s_ref[n, h] = s_h * dec_col + _dot_tn(k_st[:, ks], v_h)
            o = o * lax.rsqrt(jnp.mean(o * o, axis=-1, keepdims=True) + RMS_EPS) * ng_ref[...]
            y_ref[n, :, vs] = (o * _silu(gz_ref[n, :, vs])).astype(y_ref.dtype)

    sout_ref[...] = s_ref[...]


def _gla_call(proj, s0, gw, nb, nseq, nc, c, name):
    dstack, masks = _gla_consts(c)
    kern = functools.partial(_gla_kernel, nseq=nseq, c=c)
    hk = GLA_HEADS * GLA_DK
    st_spec = pl.BlockSpec((nseq, GLA_HEADS, GLA_DK, GLA_DV), lambda b, ci: (b, 0, 0, 0))
    return pl.pallas_call(
        kern,
        grid=(nb, nc),
        in_specs=[_seg_spec("gq", nseq, c), _seg_spec("gk", nseq, c), _seg_spec("gv", nseq, c),
                  _seg_spec("glr", nseq, c), _seg_spec("gz", nseq, c), st_spec,
                  _const_spec((LANE, hk)), _const_spec((1, hk)), _const_spec((1, GLA_DV)),
                  _const_spec(dstack.shape), _const_spec(masks.shape)],
        out_specs=[_tok_spec(nseq, c, BRANCH_W), st_spec],
        out_shape=[jax.ShapeDtypeStruct((nb * nseq, nc * c, BRANCH_W), bf16),
                   jax.ShapeDtypeStruct((nb * nseq, GLA_HEADS, GLA_DK, GLA_DV), f32)],
        scratch_shapes=[pltpu.VMEM((nseq, GLA_HEADS, GLA_DK, GLA_DV), f32)],
        compiler_params=_params("parallel", "arbitrary"),
        name=name,
    )(proj, proj, proj, proj, proj, s0, gw["wa"], gw["ba"], gw["ng"], jnp.asarray(dstack), jnp.asarray(masks))


def _gla_step_consts(t, nseq):
    dstack, masks = _gla_consts(t)
    eye = np.eye(nseq, dtype=np.float32)
    dbd = np.stack([np.kron(eye, dstack[i * t:(i + 1) * t]) for i in range(dstack.shape[0] // t)])
    mbd = np.stack([np.kron(eye, m) for m in masks])
    return dbd, mbd


def _gla_step_kernel(gq_ref, gk_ref, gv_ref, glr_ref, gz_ref, s0_ref, wa_ref, ba_ref, ng_ref, d_ref, m_ref,
                     y_ref, sout_ref, *, nseq, t):
    r = nseq * t
    nlev = int(math.log2(t))
    hk = GLA_HEADS * GLA_DK
    q = gq_ref[...].reshape(r, hk) * (GLA_DK ** -0.5)
    k = gk_ref[...].reshape(r, hk)
    v = gv_ref[...].reshape(r, BRANCH_W)
    gz = gz_ref[...].reshape(r, BRANCH_W)
    z = _dot(glr_ref[...].reshape(r, LANE), wa_ref[...]) + ba_ref[...]
    la = _log_sigmoid(z) * (1.0 / GLA_TAU)
    hi = la.astype(bf16)
    r1 = la - hi.astype(f32)
    mid = r1.astype(bf16)
    lo = (r1 - mid.astype(f32)).astype(bf16)
    hml = jnp.concatenate([hi, mid, lo], axis=1)

    def sums(blk):
        p = jnp.dot(d_ref[blk].astype(bf16), hml, preferred_element_type=f32)
        return p[:, :hk] + p[:, hk:2 * hk] + p[:, 2 * hk:]

    b = sums(0)
    q_in = q * jnp.exp(b)
    k_st = k * jnp.exp(sums(1))
    dec3 = jnp.exp(b.reshape(nseq, t, hk)[:, t - 1:t, :])
    qf = [q]
    kf = [k]
    for lev in range(nlev):
        qf.append(q * jnp.exp(sums(2 + 2 * lev)))
        kf.append(k * jnp.exp(sums(3 + 2 * lev)))

    own = (lax.broadcasted_iota(jnp.int32, (r, nseq * GLA_DK), 0) // t
           == lax.broadcasted_iota(jnp.int32, (r, nseq * GLA_DK), 1) // GLA_DK)
    eye = (lax.broadcasted_iota(jnp.int32, (GLA_DK, GLA_DK), 0)
           == lax.broadcasted_iota(jnp.int32, (GLA_DK, GLA_DK), 1))[None]

    def spread(x):
        x2 = jnp.concatenate([x, x], axis=1)
        return jnp.where(own, jnp.concatenate([x2] * (nseq // 2), axis=1), 0.0)

    ys = []
    for h in range(GLA_HEADS):
        ks = slice(h * GLA_DK, (h + 1) * GLA_DK)
        vs = slice(h * GLA_DV, (h + 1) * GLA_DV)
        att = jnp.zeros((r, r), f32)
        for lev in range(nlev + 1):
            att = att + m_ref[lev] * _dot_nt(qf[lev][:, ks], kf[lev][:, ks])
        s_h = s0_ref[:, h]
        v_h = v[:, vs]
        o = _dot(spread(q_in[:, ks]), s_h.reshape(nseq * GLA_DK, GLA_DV)) + _dot(att, v_h)
        upd = _dot_tn(spread(k_st[:, ks]), v_h)
        dec_col = jnp.sum(jnp.where(eye, jnp.broadcast_to(dec3[:, :, ks], (nseq, GLA_DK, GLA_DK)), 0.0),
                          axis=2, keepdims=True)
        sout_ref[:, h] = s_h * dec_col + upd.reshape(nseq, GLA_DK, GLA_DV)
        o = o * lax.rsqrt(jnp.mean(o * o, axis=-1, keepdims=True) + RMS_EPS) * ng_ref[...]
        ys.append(o * _silu(gz[:, vs]))
    y_ref[...] = jnp.concatenate(ys, axis=1).reshape(nseq, t, BRANCH_W).astype(y_ref.dtype)


def _gla_step_call(proj, s0, gw, nb, nseq, t, name):
    dbd, mbd = _gla_step_consts(t, nseq)
    kern = functools.partial(_gla_step_kernel, nseq=nseq, t=t)
    hk = GLA_HEADS * GLA_DK
    st_spec = pl.BlockSpec((nseq, GLA_HEADS, GLA_DK, GLA_DV), lambda b, ci: (b, 0, 0, 0))
    return pl.pallas_call(
        kern,
        grid=(nb, 1),
        in_specs=[_seg_spec("gq", nseq, t), _seg_spec("gk", nseq, t), _seg_spec("gv", nseq, t),
                  _seg_spec("glr", nseq, t), _seg_spec("gz", nseq, t), st_spec,
                  _const_spec((LANE, hk)), _const_spec((1, hk)), _const_spec((1, GLA_DV)),
                  _const_spec(dbd.shape), _const_spec(mbd.shape)],
        out_specs=[_tok_spec(nseq, t, BRANCH_W), st_spec],
        out_shape=[jax.ShapeDtypeStruct((nb * nseq, t, BRANCH_W), bf16),
                   jax.ShapeDtypeStruct((nb * nseq, GLA_HEADS, GLA_DK, GLA_DV), f32)],
        compiler_params=_params("parallel", "arbitrary"),
        name=name,
    )(proj, proj, proj, proj, proj, s0, gw["wa"], gw["ba"], gw["ng"], jnp.asarray(dbd), jnp.asarray(mbd))


def _rope_tables(pos0, t):
    half = ROT_DIM // 2
    inv = ROPE_THETA ** (-jnp.arange(half, dtype=f32) / half)
    ang = (pos0 + jnp.arange(t)).astype(f32)[:, None] * inv[None, :]
    cos, sin = jnp.cos(ang), jnp.sin(ang)
    zeros = jnp.zeros((t, HEAD_DIM - ROT_DIM), f32)
    z8 = jnp.zeros((t, half), f32)
    c_tab = jnp.concatenate([cos, cos, jnp.ones((t, HEAD_DIM - ROT_DIM), f32)], axis=1)
    sa_tab = jnp.concatenate([-sin, z8, zeros], axis=1)
    sb_tab = jnp.concatenate([z8, sin, zeros], axis=1)
    return tuple(jnp.concatenate([x, x], axis=1) for x in (c_tab, sa_tab, sb_tab))


def _rope(x, c_tab, sa_tab, sb_tab):
    wd = x.shape[-1]
    ax = x.ndim - 1
    rep = wd // LANE
    half = ROT_DIM // 2
    if rep > 1:
        c_tab, sa_tab, sb_tab = (jnp.concatenate([tb] * rep, axis=-1) for tb in (c_tab, sa_tab, sb_tab))
    return x * c_tab + pltpu.roll(x, wd - half, ax) * sa_tab + pltpu.roll(x, half, ax) * sb_tab


def _swa_kernel(sink_ref, sq_ref, sk_ref, sv_ref, sz_ref, ct_ref, sat_ref, sbt_ref, kp_ref, vp_ref,
                y_ref, klast_ref, vlast_ref, kprev_ref, vprev_ref, *, nseq, qb, pos0, carry):
    blk = pl.program_id(1)

    @pl.when(blk == 0)
    def _():
        kprev_ref[...] = kp_ref[...]
        vprev_ref[...] = vp_ref[...]

    tabs = (ct_ref[...], sat_ref[...], sbt_ref[...])
    mrows = SWA_GROUP * qb
    qi = lax.broadcasted_iota(jnp.int32, (mrows, WINDOW), 0) % qb
    kj = lax.broadcasted_iota(jnp.int32, (mrows, WINDOW), 1)
    past_ok = (kj >= qi) & (kj >= (WINDOW - pos0) - blk * qb)
    qi_c = lax.broadcasted_iota(jnp.int32, (mrows, qb), 0) % qb
    kj_c = lax.broadcasted_iota(jnp.int32, (mrows, qb), 1)
    cur_ok = kj_c <= qi_c
    neg = -jnp.inf

    for n in range(nseq):
        q = _rope(sq_ref[n], *tabs) * (HEAD_DIM ** -0.5)
        k = _rope(sk_ref[n], *tabs)
        v = sv_ref[n]
        kprev = kprev_ref[n]
        vprev = vprev_ref[n]
        klast_ref[n] = k if qb == WINDOW else jnp.concatenate([kprev[qb:], k], axis=0)
        vlast_ref[n] = v if qb == WINDOW else jnp.concatenate([vprev[qb:], v], axis=0)
        outs = []
        for kv in range(SWA_KV_HEADS):
            ds = slice(kv * HEAD_DIM, (kv + 1) * HEAD_DIM)
            qs = jnp.concatenate(
                [q[:, (kv * SWA_GROUP + g) * HEAD_DIM:(kv * SWA_GROUP + g + 1) * HEAD_DIM] for g in range(SWA_GROUP)],
                axis=0)
            sink = jnp.concatenate(
                [jnp.full((qb, 1), sink_ref[kv * SWA_GROUP + g], f32) for g in range(SWA_GROUP)], axis=0)
            s_p = jnp.where(past_ok, _dot_nt(qs, kprev[:, ds]), neg)
            s_c = jnp.where(cur_ok, _dot_nt(qs, k[:, ds]), neg)
            m = jnp.maximum(jnp.maximum(jnp.max(s_p, axis=1, keepdims=True), jnp.max(s_c, axis=1, keepdims=True)),
                            sink)
            p_p = jnp.exp(s_p - m)
            p_c = jnp.exp(s_c - m)
            den = jnp.sum(p_p, axis=1, keepdims=True) + jnp.sum(p_c, axis=1, keepdims=True) + jnp.exp(sink - m)
            o = (_dot(p_p, vprev[:, ds]) + _dot(p_c, v[:, ds])) / den
            outs.extend(o[g * qb:(g + 1) * qb, :] for g in range(SWA_GROUP))
        o_all = jnp.concatenate(outs, axis=1)
        y_ref[n] = (o_all * _silu(sz_ref[n])).astype(y_ref.dtype)
        if carry:
            kprev_ref[n] = k
            vprev_ref[n] = v


def _swa_call(proj, sinks, k_past, v_past, pos0, nb, nseq, nc, qb, name):
    assert nc == 1 or qb == WINDOW
    t_total = nc * qb
    c_tab, sa_tab, sb_tab = _rope_tables(pos0, t_total)
    kern = functools.partial(_swa_kernel, nseq=nseq, qb=qb, pos0=pos0, carry=nc > 1)
    kvw = SWA_KV_HEADS * HEAD_DIM
    tab_spec = pl.BlockSpec((qb, LANE), lambda b, c: (c, 0))
    past_spec = pl.BlockSpec((nseq, WINDOW, kvw), lambda b, c: (b, 0, 0))
    return pl.pallas_call(
        kern,
        grid=(nb, nc),
        in_specs=[pl.BlockSpec(memory_space=pltpu.SMEM),
                  _seg_spec("sq", nseq, qb), _seg_spec("sk", nseq, qb), _seg_spec("sv", nseq, qb),
                  _seg_spec("sz", nseq, qb), tab_spec, tab_spec, tab_spec, past_spec, past_spec],
        out_specs=[_tok_spec(nseq, qb, BRANCH_W), past_spec, past_spec],
        out_shape=[jax.ShapeDtypeStruct((nb * nseq, nc * qb, BRANCH_W), bf16),
                   jax.ShapeDtypeStruct((nb * nseq, WINDOW, kvw), f32),
                   jax.ShapeDtypeStruct((nb * nseq, WINDOW, kvw), f32)],
        scratch_shapes=[pltpu.VMEM((nseq, WINDOW, kvw), f32), pltpu.VMEM((nseq, WINDOW, kvw), f32)],
        compiler_params=_params("parallel", "arbitrary"),
        name=name,
    )(sinks, proj, proj, proj, proj, c_tab, sa_tab, sb_tab, k_past, v_past)


def _swa_step_kernel(sink_ref, sq_ref, sk_ref, sv_ref, sz_ref, ct_ref, sat_ref, sbt_ref, kp_ref, vp_ref,
                     y_ref, klast_ref, vlast_ref, sp_ref, sc_ref, o_ref, *, nseq, t, pos0):
    hd = HEAD_DIM
    kvw = SWA_KV_HEADS * hd
    mrows = SWA_HEADS * t
    tabs = tuple(r[...][None] for r in (ct_ref, sat_ref, sbt_ref))
    q3 = _rope(sq_ref[...], *tabs) * (hd ** -0.5)
    k3 = _rope(sk_ref[...], *tabs)
    v3 = sv_ref[...]
    klast_ref[...] = jnp.concatenate([kp_ref[:, t:, :], k3], axis=1)
    vlast_ref[...] = jnp.concatenate([vp_ref[:, t:, :], v3], axis=1)

    zero = jnp.zeros((nseq, t, hd), f32)
    pieces = []
    for j in range(SWA_HEADS):
        qj = q3[:, :, j * hd:(j + 1) * hd]
        pieces.append(jnp.concatenate([qj, zero] if j // SWA_GROUP == 0 else [zero, qj], axis=2))
    qbd = jnp.concatenate(pieces, axis=1).astype(bf16)

    for n in range(nseq):
        sp_ref[n] = _dot_nt(qbd[n], kp_ref[n])
        sc_ref[n] = _dot_nt(qbd[n], k3[n])

    qi = lax.broadcasted_iota(jnp.int32, (mrows, WINDOW), 0) % t
    kj = lax.broadcasted_iota(jnp.int32, (mrows, WINDOW), 1)
    past_ok = kj >= qi
    if pos0 < WINDOW:
        past_ok = past_ok & (kj >= WINDOW - pos0)
    qi_c = lax.broadcasted_iota(jnp.int32, (mrows, t), 0) % t
    kj_c = lax.broadcasted_iota(jnp.int32, (mrows, t), 1)
    cur_ok = kj_c <= qi_c
    sink = jnp.concatenate([jnp.full((t, 1), sink_ref[j], f32) for j in range(SWA_HEADS)], axis=0)[None]
    s_p = jnp.where(past_ok[None], sp_ref[...], -jnp.inf)
    s_c = jnp.where(cur_ok[None], sc_ref[...], -jnp.inf)
    m = jnp.maximum(jnp.maximum(jnp.max(s_p, axis=2, keepdims=True), jnp.max(s_c, axis=2, keepdims=True)), sink)
    p_p = jnp.exp(s_p - m)
    p_c = jnp.exp(s_c - m)
    den = jnp.sum(p_p, axis=2, keepdims=True) + jnp.sum(p_c, axis=2, keepdims=True) + jnp.exp(sink - m)
    p_p = p_p.astype(bf16)
    p_c = p_c.astype(bf16)
    for n in range(nseq):
        o_ref[n] = _dot(p_p[n], vp_ref[n]) + _dot(p_c[n], v3[n])
    o = o_ref[...] / den
    outs = []
    for j in range(SWA_HEADS):
        kv = j // SWA_GROUP
        outs.append(o[:, j * t:(j + 1) * t, kv * hd:(kv + 1) * hd])
    y_ref[...] = (jnp.concatenate(outs, axis=2) * _silu(sz_ref[...])).astype(y_ref.dtype)


def _swa_step_call(proj, sinks, k_past, v_past, pos0, nb, nseq, t, name):
    c_tab, sa_tab, sb_tab = _rope_tables(pos0, t)
    kern = functools.partial(_swa_step_kernel, nseq=nseq, t=t, pos0=pos0)
    kvw = SWA_KV_HEADS * HEAD_DIM
    mrows = SWA_HEADS * t
    tab_spec = pl.BlockSpec((t, LANE), lambda b, c: (0, 0))
    past_spec = pl.BlockSpec((nseq, WINDOW, kvw), lambda b, c: (b, 0, 0))
    return pl.pallas_call(
        kern,
        grid=(nb, 1),
        in_specs=[pl.BlockSpec(memory_space=pltpu.SMEM),
                  _seg_spec("sq", nseq, t), _seg_spec("sk", nseq, t), _seg_spec("sv", nseq, t),
                  _seg_spec("sz", nseq, t), tab_spec, tab_spec, tab_spec, past_spec, past_spec],
        out_specs=[_tok_spec(nseq, t, BRANCH_W), past_spec, past_spec],
        out_shape=[jax.ShapeDtypeStruct((nb * nseq, t, BRANCH_W), bf16),
                   jax.ShapeDtypeStruct((nb * nseq, WINDOW, kvw), f32),
                   jax.ShapeDtypeStruct((nb * nseq, WINDOW, kvw), f32)],
        scratch_shapes=[pltpu.VMEM((nseq, mrows, WINDOW), f32), pltpu.VMEM((nseq, mrows, t), f32),
                        pltpu.VMEM((nseq, mrows, kvw), f32)],
        compiler_params=_params("parallel", "arbitrary"),
        name=name,
    )(sinks, proj, proj, proj, proj, c_tab, sa_tab, sb_tab, k_past, v_past)


def _sgu_kernel(su_ref, sv_ref, sz_ref, g_ref, b_ref, wm_ref, bias_ref, y_ref, *vn_ref, ntile):
    vn = _layer_norm(sv_ref[...], g_ref[...], b_ref[...])
    if vn_ref:
        vn_ref[0][...] = vn
    for r in range(ntile):
        rows = slice(r * SGU_CHUNK, (r + 1) * SGU_CHUNK)
        mixed = jnp.concatenate(
            [jnp.dot(wm_ref[g], vn[rows, g * SGU_GC:(g + 1) * SGU_GC].astype(bf16), preferred_element_type=f32)
             for g in range(SGU_GROUPS)], axis=1)
        y = su_ref[rows, :] * (mixed + bias_ref[...]) * _silu(sz_ref[rows, :])
        y_ref[rows, :] = y.astype(y_ref.dtype)


def _sgu_call(proj, ln_g, ln_b, wmix, bias, n_tok, ntile, want_vn, name):
    rows = ntile * SGU_CHUNK
    w = BRANCH_W
    kern = functools.partial(_sgu_kernel, ntile=ntile)

    def seg(nm):
        off, width = _SEG[nm]
        return pl.BlockSpec((rows, width), lambda i: (i, off // width))

    def const(shape):
        return pl.BlockSpec(shape, lambda i: (0,) * len(shape))

    out_specs = [pl.BlockSpec((rows, w), lambda i: (i, 0))]
    out_shape = [jax.ShapeDtypeStruct((n_tok, w), bf16)]
    if want_vn:
        out_specs.append(pl.BlockSpec((rows, w), lambda i: (i, 0)))
        out_shape.append(jax.ShapeDtypeStruct((n_tok, w), f32))
    return pl.pallas_call(
        kern,
        grid=(n_tok // rows,),
        in_specs=[seg("su"), seg("svv"), seg("suz"), const((1, w)), const((1, w)),
                  const((SGU_GROUPS, SGU_CHUNK, SGU_CHUNK)), const((SGU_CHUNK, w))],
        out_specs=out_specs,
        out_shape=out_shape,
        compiler_params=_params("parallel"),
        name=name,
    )(proj, proj, proj, ln_g, ln_b, wmix, bias)


def _mem_kernel(mq_ref, mz_ref, mk_ref, mv_ref, y_ref, *, nseq, tq):
    for n in range(nseq):
        q = mq_ref[n] * (HEAD_DIM ** -0.5)
        mk = mk_ref[n]
        mv = mv_ref[n]
        outs = []
        for h in range(MEM_HEADS):
            ds = slice(h * HEAD_DIM, (h + 1) * HEAD_DIM)
            s = _dot_nt(q[:, ds], mk[:, ds])
            m = jnp.max(s, axis=1, keepdims=True)
            p = jnp.exp(s - m)
            den = jnp.sum(p, axis=1, keepdims=True)
            outs.append(_dot(p, mv[:, ds]) / den)
        o = jnp.concatenate(outs, axis=1)
        y_ref[n] = (o * _silu(mz_ref[n])).astype(y_ref.dtype)


def _mem_call(proj, mk, mv, nb, nseq, nc, tq, name):
    kern = functools.partial(_mem_kernel, nseq=nseq, tq=tq)
    kv_spec = pl.BlockSpec((nseq, N_MEM, MEM_W), lambda b, c: (b, 0, 0))
    return pl.pallas_call(
        kern,
        grid=(nb, nc),
        in_specs=[_seg_spec("mq", nseq, tq), _seg_spec("mz", nseq, tq), kv_spec, kv_spec],
        out_specs=_tok_spec(nseq, tq, MEM_W),
        out_shape=jax.ShapeDtypeStruct((nb * nseq, nc * tq, MEM_W), bf16),
        compiler_params=_params("parallel", "arbitrary"),
        name=name,
    )(proj, proj, mk, mv)


def _mem_step_kernel(mq_ref, mz_ref, mk_ref, mv_ref, y_ref, s_ref, o_ref, *, nseq, t):
    mrows = MEM_HEADS * t
    row_head = lax.broadcasted_iota(jnp.int32, (mrows, MEM_W), 0) // t
    lane_head = lax.broadcasted_iota(jnp.int32, (mrows, MEM_W), 1) // HEAD_DIM
    own = (row_head == lane_head)[None]
    q3 = mq_ref[...] * (HEAD_DIM ** -0.5)
    qbd = jnp.where(own, jnp.concatenate([q3] * MEM_HEADS, axis=1), 0.0).astype(bf16)
    for n in range(nseq):
        s_ref[n] = _dot_nt(qbd[n], mk_ref[n])
    s = s_ref[...]
    p = jnp.exp(s - jnp.max(s, axis=2, keepdims=True))
    den = jnp.sum(p, axis=2, keepdims=True)
    p = p.astype(bf16)
    for n in range(nseq):
        o_ref[n] = _dot(p[n], mv_ref[n])
    o = jnp.where(own, o_ref[...] / den, 0.0)
    acc = o[:, 0:t, :]
    for h in range(1, MEM_HEADS):
        acc = acc + o[:, h * t:(h + 1) * t, :]
    y_ref[...] = (acc * _silu(mz_ref[...])).astype(y_ref.dtype)


def _mem_step_call(proj, mk, mv, nb, nseq, t, name):
    kern = functools.partial(_mem_step_kernel, nseq=nseq, t=t)
    kv_spec = pl.BlockSpec((nseq, N_MEM, MEM_W), lambda b, c: (b, 0, 0))
    mrows = MEM_HEADS * t
    return pl.pallas_call(
        kern,
        grid=(nb, 1),
        in_specs=[_seg_spec("mq", nseq, t), _seg_spec("mz", nseq, t), kv_spec, kv_spec],
        out_specs=_tok_spec(nseq, t, MEM_W),
        out_shape=jax.ShapeDtypeStruct((nb * nseq, t, MEM_W), bf16),
        scratch_shapes=[pltpu.VMEM((nseq, mrows, N_MEM), f32), pltpu.VMEM((nseq, mrows, MEM_W), f32)],
        compiler_params=_params("parallel", "arbitrary"),
        name=name,
    )(proj, proj, mk, mv)


def _merge_kernel(yg_ref, yl_ref, ys_ref, yu_ref, ym_ref, x_ref, wg_ref, wb_ref, wm_ref, wo_ref, g_ref, b_ref,
                  o_ref):
    d = D_MODEL
    x = x_ref[...]
    xb = x.astype(bf16)

    def gate(n):
        return _sigmoid(jnp.dot(xb, wg_ref[:, n * d:(n + 1) * d], preferred_element_type=f32))

    merged = gate(4) * jnp.dot(ym_ref[...], wm_ref[...], preferred_element_type=f32)
    for n, y_ref in enumerate((yg_ref, yl_ref, ys_ref, yu_ref)):
        merged = merged + gate(n) * jnp.dot(y_ref[...], wb_ref[n], preferred_element_type=f32)
    out = _dot(merged, wo_ref[...])
    o_ref[...] = _layer_norm(DN_ALPHA * x + out, g_ref[...], b_ref[...])


def _merge_call(ys, x, mw, tm, name):
    n_tok, d = x.shape
    w = BRANCH_W

    def rows(width):
        return pl.BlockSpec((tm, width), lambda i: (i, 0))

    def const(shape):
        return pl.BlockSpec(shape, lambda i: (0,) * len(shape), pipeline_mode=pl.Buffered(1))

    return pl.pallas_call(
        _merge_kernel,
        grid=(n_tok // tm,),
        in_specs=[rows(w), rows(w), rows(w), rows(w), rows(MEM_W), rows(d),
                  const((d, 5 * d)), const((4, w, d)), const((MEM_W, d)), const((d, d)), const((1, d)),
                  const((1, d))],
        out_specs=rows(d),
        out_shape=jax.ShapeDtypeStruct((n_tok, d), f32),
        compiler_params=_params("parallel"),
        name=name,
    )(*ys, x, mw["wg"], mw["wb"], mw["wm"], mw["wo"], mw["g"], mw["b"])


def _prep_layer(l, w_in, gla_wa2, gla_ba, gla_norm_g, lru_conv_w, lru_conv_b, lru_wr, lru_br, lru_wi, lru_bi, lru_L,
                swa_sinks, sgu_ln_g, sgu_ln_b, sgu_w, sgu_b, w_mem_kv, w_branch, w_branch_mem, w_out, ln_g, ln_b):
    d = D_MODEL
    w = BRANCH_W
    cols = []
    used = 0
    for a, b in _PROJ_RUNS:
        lo = _ORIG_OFF[a][0]
        hi = _ORIG_OFF[b][0] + _ORIG_OFF[b][1]
        cols.append(w_in[l][:, lo:hi].astype(bf16))
        used += hi - lo
    cols.append(jnp.zeros((d, N_PROJ - used), bf16))
    w_proj = jnp.concatenate(cols, axis=1)
    g_lo = _ORIG_OFF["gates"][0]
    w_gates = w_in[l][:, g_lo:].astype(bf16)

    def block_diag(wb):
        eye = jnp.eye(LRU_BLOCKS, dtype=f32)
        return (eye[:, None, :, None] * wb[:, :, None, :]).reshape(w, w).astype(bf16)

    tril = jnp.tril(jnp.ones((SGU_CHUNK, SGU_CHUNK), f32))
    wmix_p = (sgu_w[l] * tril).astype(bf16)
    bias_p = jnp.repeat(sgu_b[l].T, SGU_GC, axis=1)
    t8 = SUBLANE
    rep = SGU_CHUNK // t8
    w8 = (sgu_w[l] * tril)[:, :t8, :t8]
    seq_eye = jnp.eye(rep, dtype=f32)
    wmix_s = (seq_eye[None, :, None, :, None] * w8[:, None, :, None, :]).reshape(
        SGU_GROUPS, SGU_CHUNK, SGU_CHUNK).astype(bf16)
    bias_s = jnp.tile(bias_p[:t8], (rep, 1))
    return dict(
        w_proj=w_proj,
        w_mem_kv=w_mem_kv[l].astype(bf16),
        gla=dict(wa=jnp.pad(gla_wa2[l], ((0, LANE - GLA_RANK), (0, 0))).astype(bf16),
                 ba=gla_ba[l].reshape(1, -1), ng=gla_norm_g[l].reshape(1, -1)),
        lru=dict(conv_w=lru_conv_w[l], conv_b=lru_conv_b[l].reshape(1, w), wr=block_diag(lru_wr[l]),
                 br=lru_br[l].reshape(1, w), wi=block_diag(lru_wi[l]), bi=lru_bi[l].reshape(1, w),
                 lam=lru_L[l].reshape(1, w)),
        sinks=swa_sinks[l],
        sgu=dict(g=sgu_ln_g[l].reshape(1, w), b=sgu_ln_b[l].reshape(1, w), wmix_p=wmix_p, bias_p=bias_p,
                 wmix_s=wmix_s, bias_s=bias_s),
        merge=dict(wg=w_gates, wb=w_branch[l].astype(bf16), wm=w_branch_mem[l].astype(bf16), wo=w_out[l].astype(bf16),
                   g=ln_g[l].reshape(1, d), b=ln_b[l].reshape(1, d)),
    )


def _layer(x, lw, grp, st, tag):
    nseq_total, t = grp["batch"], grp["seq"]
    n_tok = nseq_total * t
    proj = _matmul_call(x, lw["w_proj"], min(1024, n_tok), 1024, "proj_" + tag)
    proj3 = proj.reshape(nseq_total, t, N_PROJ)

    lt = grp["lru"]
    y_lru, hlast, hist = _lru_call(proj3, st["hist0"], st["h0"], lw["lru"], nseq_total // lt[0], lt[0], t // lt[1],
                                   lt[1], "lru_" + tag)
    short = grp["kind"] == "s"
    gt = grp["gla"]
    if short:
        y_gla, s_out = _gla_step_call(proj3, st["gla0"], lw["gla"], nseq_total // gt[0], gt[0], t, "gla_" + tag)
    else:
        y_gla, s_out = _gla_call(proj3, st["gla0"], lw["gla"], nseq_total // gt[0], gt[0], t // gt[1], gt[1],
                                 "gla_" + tag)
    wt = grp["swa"]
    if short:
        y_swa, k_last, v_last = _swa_step_call(proj3, lw["sinks"], st["k_past"], st["v_past"], grp["pos0"],
                                               nseq_total // wt[0], wt[0], t, "swa_" + tag)
    else:
        y_swa, k_last, v_last = _swa_call(proj3, lw["sinks"], st["k_past"], st["v_past"], grp["pos0"],
                                          nseq_total // wt[0], wt[0], t // wt[1], wt[1], "swa_" + tag)
    sg = lw["sgu"]
    sgu_out = _sgu_call(proj, sg["g"], sg["b"], sg["wmix_" + grp["kind"]], sg["bias_" + grp["kind"]], n_tok,
                        grp["sgu_tiles"], grp["kind"] == "s", "sgu_" + tag)
    mt = grp["mem"]
    if short:
        y_mem = _mem_step_call(proj3, st["mk"], st["mv"], nseq_total // mt[0], mt[0], t, "mem_" + tag)
    else:
        y_mem = _mem_call(proj3, st["mk"], st["mv"], nseq_total // mt[0], mt[0], t // mt[1], mt[1], "mem_" + tag)
    ys = tuple(y.reshape(n_tok, y.shape[-1]) for y in (y_gla, y_lru, y_swa, sgu_out[0], y_mem))
    x_new = _merge_call(ys, x, lw["merge"], min(256, n_tok), "merge_" + tag)
    return x_new, dict(gla=s_out, hlast=hlast, hist=hist, k_last=k_last, v_last=v_last,
                       vn=sgu_out[1] if len(sgu_out) > 1 else None)


_PROMPT = dict(kind="p", pos0=0, lru=(1, 256), gla=(2, 128), swa=(1, 128), sgu_tiles=4, mem=(1, 512))
_SAMPLE = dict(kind="s", pos0=PAST_LEN, lru=(32, 8), gla=(16, 8), swa=(16, 8), sgu_tiles=8, mem=(16, 8))


def kernel(x_prompt, x_sample, mem_prompt, state_gla, state_lru_h, state_lru_conv, cache_swa_k, cache_swa_v,
           cache_mem_k, cache_mem_v, ln_in_g, ln_in_b, w_in, gla_wa2, gla_ba, gla_norm_g, lru_conv_w, lru_conv_b,
           lru_wr, lru_br, lru_wi, lru_bi, lru_L, swa_sinks, sgu_ln_g, sgu_ln_b, sgu_w, sgu_b, w_mem_kv, w_branch,
           w_branch_mem, w_out, ln_g, ln_b):
    bp, tp, d = x_prompt.shape
    bs, ts, _ = x_sample.shape
    w = BRANCH_W
    kvw = SWA_KV_HEADS * HEAD_DIM
    gp = dict(_PROMPT, batch=bp, seq=tp)
    gs = dict(_SAMPLE, batch=bs, seq=ts)

    xp = _ln_call(x_prompt.reshape(bp * tp, d), ln_in_g, ln_in_b)
    xs = _ln_call(x_sample.reshape(bs * ts, d), ln_in_g, ln_in_b)
    mem2 = mem_prompt.reshape(bp * N_MEM, d)

    outs_p, outs_s, mks, mvs = [], [], [], []
    for l in range(DEPTH):
        lw = _prep_layer(l, w_in, gla_wa2, gla_ba, gla_norm_g, lru_conv_w, lru_conv_b, lru_wr, lru_br, lru_wi,
                         lru_bi, lru_L, swa_sinks, sgu_ln_g, sgu_ln_b, sgu_w, sgu_b, w_mem_kv, w_branch,
                         w_branch_mem, w_out, ln_g, ln_b)
        mkv = _matmul_call(mem2, lw["w_mem_kv"], bp * N_MEM, 2 * MEM_W, "memkv_%d" % l)
        mk = mkv[:, :MEM_W].reshape(bp, N_MEM, MEM_W)
        mv = mkv[:, MEM_W:].reshape(bp, N_MEM, MEM_W)
        st_p = dict(hist0=jnp.zeros((bp, SUBLANE, w), f32), h0=jnp.zeros((bp, 1, w), f32),
                    gla0=jnp.zeros((bp, GLA_HEADS, GLA_DK, GLA_DV), f32),
                    k_past=jnp.zeros((bp, WINDOW, kvw), f32), v_past=jnp.zeros((bp, WINDOW, kvw), f32),
                    mk=mk, mv=mv)
        st_s = dict(hist0=jnp.pad(state_lru_conv[l], ((0, 0), (SUBLANE - (CONV_W - 1), 0), (0, 0))),
                    h0=state_lru_h[l][:, None, :], gla0=state_gla[l],
                    k_past=cache_swa_k[l].reshape(bs, WINDOW, kvw), v_past=cache_swa_v[l].reshape(bs, WINDOW, kvw),
                    mk=cache_mem_k[l].reshape(bs, N_MEM, MEM_W), mv=cache_mem_v[l].reshape(bs, N_MEM, MEM_W))
        xp, op = _layer(xp, lw, gp, st_p, "p%d" % l)
        xs, os_ = _layer(xs, lw, gs, st_s, "s%d" % l)
        outs_p.append(op)
        outs_s.append(os_)
        mks.append(mk.reshape(bp, N_MEM, MEM_HEADS, HEAD_DIM))
        mvs.append(mv.reshape(bp, N_MEM, MEM_HEADS, HEAD_DIM))

    def stack(outs, fn):
        return jnp.stack([fn(o) for o in outs])

    def window(a):
        return a.reshape(a.shape[0], WINDOW, SWA_KV_HEADS, HEAD_DIM)

    return (
        xp.reshape(bp, tp, d), xs.reshape(bs, ts, d),
        stack(outs_p, lambda o: o["gla"]), stack(outs_s, lambda o: o["gla"]),
        stack(outs_p, lambda o: o["hlast"][:, SUBLANE - 1]), stack(outs_s, lambda o: o["hlast"][:, SUBLANE - 1]),
        stack(outs_p, lambda o: o["hist"][:, SUBLANE - (CONV_W - 1):]),
        stack(outs_s, lambda o: o["hist"][:, SUBLANE - (CONV_W - 1):]),
        stack(outs_p, lambda o: window(o["k_last"])), stack(outs_s, lambda o: window(o["k_last"])),
        stack(outs_p, lambda o: window(o["v_last"])), stack(outs_s, lambda o: window(o["v_last"])),
        jnp.stack(mks), jnp.stack(mvs),
        stack(outs_s, lambda o: o["vn"].reshape(bs, ts, w)),
    )
```

```python
import functools
import math

import jax
import jax.numpy as jnp
import numpy as np
from jax import lax
from jax.experimental import pallas as pl
from jax.experimental.pallas import tpu as pltpu

f32 = jnp.float32
bf16 = jnp.bfloat16

D_MODEL = 1024
DEPTH = 2
PAST_LEN = 8192
BRANCH_W = 512
GLA_HEADS = 4
GLA_DK = 64
GLA_DV = 128
GLA_RANK = 16
GLA_TAU = 16.0
LRU_BLOCKS = 8
LRU_BS = 64
CONV_W = 4
LRU_C = 8.0
HEAD_DIM = 64
SWA_HEADS = 8
SWA_KV_HEADS = 2
SWA_GROUP = 4
WINDOW = 128
ROT_DIM = 16
ROPE_THETA = 500000.0
SGU_GROUPS = 4
SGU_GC = 128
SGU_CHUNK = 128
N_MEM = 256
MEM_HEADS = 4
MEM_W = 256
LN_EPS = 1e-5
RMS_EPS = 1e-6
DN_ALPHA = (2 * DEPTH) ** 0.25

LANE = 128
SUBLANE = 8

_ORIG = (("gq", 256), ("gk", 256), ("gv", 512), ("glr", 16), ("gz", 512), ("lx", 512), ("lz", 512), ("sq", 512),
         ("sk", 128), ("sv", 128), ("sz", 512), ("su", 512), ("svv", 512), ("suz", 512), ("mq", 256), ("mz", 256),
         ("gates", 5 * D_MODEL))
_ORIG_OFF = {}
_off = 0
for _n, _w in _ORIG:
    _ORIG_OFF[_n] = (_off, _w)
    _off += _w
_PROJ_RUNS = (("gq", "gv"), ("gz", "sq"), ("sz", "mz"), ("sk", "sv"), ("glr", "glr"))
_SEG = {}
_off = 0
for _a, _b in _PROJ_RUNS:
    _names = [n for n, _ in _ORIG]
    for _n in _names[_names.index(_a):_names.index(_b) + 1]:
        _w = max(_ORIG_OFF[_n][1], 128)
        assert _off % _w == 0
        _SEG[_n] = (_off, _w)
        _off += _w
N_PROJ = -(-_off // 1024) * 1024


def _dot(a, b):
    return jnp.dot(a.astype(bf16), b.astype(bf16), preferred_element_type=f32)


def _dot_nt(a, b):
    return lax.dot_general(a.astype(bf16), b.astype(bf16), (((1,), (1,)), ((), ())), preferred_element_type=f32)


def _dot_tn(a, b):
    return lax.dot_general(a.astype(bf16), b.astype(bf16), (((0,), (0,)), ((), ())), preferred_element_type=f32)


def _sigmoid(x):
    return 1.0 / (1.0 + jnp.exp(-x))


def _silu(x):
    return x * _sigmoid(x)


def _log_sigmoid(x):
    return jnp.minimum(x, 0.0) - jnp.log1p(jnp.exp(-jnp.abs(x)))


def _layer_norm(x, g, b):
    mu = jnp.mean(x, axis=-1, keepdims=True)
    xc = x - mu
    var = jnp.mean(xc * xc, axis=-1, keepdims=True)
    return xc * lax.rsqrt(var + LN_EPS) * g + b


def _params(*sem):
    return pltpu.CompilerParams(dimension_semantics=sem)


def _ln_kernel(x_ref, g_ref, b_ref, o_ref):
    o_ref[...] = _layer_norm(x_ref[...], g_ref[...], b_ref[...])


def _ln_call(x, g, b, tm=512):
    n, d = x.shape
    tm = min(tm, n)
    return pl.pallas_call(
        _ln_kernel,
        grid=(n // tm,),
        in_specs=[pl.BlockSpec((tm, d), lambda i: (i, 0)), pl.BlockSpec((1, d), lambda i: (0, 0)),
                  pl.BlockSpec((1, d), lambda i: (0, 0))],
        out_specs=pl.BlockSpec((tm, d), lambda i: (i, 0)),
        out_shape=jax.ShapeDtypeStruct((n, d), f32),
        compiler_params=_params("parallel"),
        name="ln_in",
    )(x, g.reshape(1, d), b.reshape(1, d))


def _matmul_kernel(x_ref, w_ref, o_ref, *, w_transposed):
    o_ref[...] = (_dot_nt if w_transposed else _dot)(x_ref[...], w_ref[...])


def _matmul_call(x, w, tm, tn, name, w_transposed=False):
    m, k = x.shape
    n = w.shape[0] if w_transposed else w.shape[1]
    w_spec = pl.BlockSpec((tn, k), lambda i, j: (j, 0)) if w_transposed else pl.BlockSpec((k, tn), lambda i, j: (0, j))
    return pl.pallas_call(
        functools.partial(_matmul_kernel, w_transposed=w_transposed),
        grid=(m // tm, n // tn),
        in_specs=[pl.BlockSpec((tm, k), lambda i, j: (i, 0)), w_spec],
        out_specs=pl.BlockSpec((tm, tn), lambda i, j: (i, j)),
        out_shape=jax.ShapeDtypeStruct((m, n), f32),
        compiler_params=_params("parallel", "arbitrary"),
        name=name,
    )(x, w)


def _seg_spec(name, nseq, rows):
    off, width = _SEG[name]
    cb = off // width
    return pl.BlockSpec((nseq, rows, width), lambda b, c: (b, c, cb))


def _tok_spec(nseq, rows, width):
    return pl.BlockSpec((nseq, rows, width), lambda b, c: (b, c, 0))


def _const_spec(shape):
    nd = len(shape)
    return pl.BlockSpec(shape, lambda b, c: (0,) * nd)


def _lru_kernel(lx_ref, lz_ref, hist0_ref, h0_ref, cw_ref, cb_ref, wr_ref, br_ref, wi_ref, bi_ref, lam_ref,
                y_ref, hlast_ref, hist_out_ref, hist_ref, hc_ref, *, nseq, tc):
    c = pl.program_id(1)
    w = BRANCH_W

    @pl.when(c == 0)
    def _():
        hist_ref[...] = hist0_ref[...]
        hc_ref[...] = h0_ref[...]

    x = lx_ref[...]
    xfull = jnp.concatenate([hist_ref[...], x], axis=1)

    def tap(j):
        return cw_ref[j:j + 1, :].reshape(1, 1, w)

    y = cb_ref[...].reshape(1, 1, w) + x * tap(CONV_W - 1)
    for s in range(1, CONV_W):
        y = y + pltpu.roll(xfull, s, 1)[:, SUBLANE:, :] * tap(CONV_W - 1 - s)
    hist_ref[...] = xfull[:, tc:, :]
    hist_out_ref[...] = xfull[:, tc:, :]

    xc = y.reshape(nseq * tc, w)
    r = _sigmoid(_dot(xc, wr_ref[...]) + br_ref[...])
    i = _sigmoid(_dot(xc, wi_ref[...]) + bi_ref[...])
    log_a = (LRU_C * r) * _log_sigmoid(lam_ref[...])
    a = jnp.exp(log_a)
    u = jnp.sqrt(jnp.tanh(-log_a) * (a * a + 1.0)) * (i * xc)

    acc_a = a.reshape(nseq, tc, w)
    acc_u = u.reshape(nseq, tc, w)
    t = lax.broadcasted_iota(jnp.int32, (nseq, tc, w), 1)
    d = 1
    while d < tc:
        ok = t >= d
        a_sh = jnp.where(ok, pltpu.roll(acc_a, d, 1), 1.0)
        u_sh = jnp.where(ok, pltpu.roll(acc_u, d, 1), 0.0)
        acc_u = acc_a * u_sh + acc_u
        acc_a = acc_a * a_sh
        d *= 2
    h = acc_a * hc_ref[...] + acc_u
    hc_ref[...] = h[:, tc - 1:tc, :]
    hlast_ref[...] = h[:, tc - SUBLANE:, :]
    y_ref[...] = (h * _silu(lz_ref[...])).astype(y_ref.dtype)


def _lru_call(proj, hist0, h0, lw, nb, nseq, nc, tc, name):
    w = BRANCH_W
    kern = functools.partial(_lru_kernel, nseq=nseq, tc=tc)
    return pl.pallas_call(
        kern,
        grid=(nb, nc),
        in_specs=[_seg_spec("lx", nseq, tc), _seg_spec("lz", nseq, tc),
                  pl.BlockSpec((nseq, SUBLANE, w), lambda b, c: (b, 0, 0)),
                  pl.BlockSpec((nseq, 1, w), lambda b, c: (b, 0, 0)),
                  _const_spec((CONV_W, w)), _const_spec((1, w)), _const_spec((w, w)), _const_spec((1, w)),
                  _const_spec((w, w)), _const_spec((1, w)), _const_spec((1, w))],
        out_specs=[_tok_spec(nseq, tc, w),
                   pl.BlockSpec((nseq, SUBLANE, w), lambda b, c: (b, 0, 0)),
                   pl.BlockSpec((nseq, SUBLANE, w), lambda b, c: (b, 0, 0))],
        out_shape=[jax.ShapeDtypeStruct((nb * nseq, nc * tc, w), bf16),
                   jax.ShapeDtypeStruct((nb * nseq, SUBLANE, w), f32),
                   jax.ShapeDtypeStruct((nb * nseq, SUBLANE, w), f32)],
        scratch_shapes=[pltpu.VMEM((nseq, SUBLANE, w), f32), pltpu.VMEM((nseq, 1, w), f32)],
        compiler_params=_params("parallel", "arbitrary"),
        name=name,
    )(proj, proj, hist0, h0, lw["conv_w"], lw["conv_b"], lw["wr"], lw["br"], lw["wi"], lw["bi"], lw["lam"])


def _gla_consts(c):
    t = np.arange(c)[:, None]
    u = np.arange(c)[None, :]
    blocks = [u <= t, u > t]
    masks = [t == u]
    m = 1
    while m < c:
        t0 = (t // m) * m
        odd = (t // m) % 2 == 1
        blocks.append(odd & (u >= t0) & (u <= t))
        blocks.append((~odd) & (u > t) & (u <= t0 + m - 1))
        masks.append((t // (2 * m) == u // (2 * m)) & odd & ((u // m) % 2 == 0))
        m *= 2
    return (np.concatenate(blocks, 0).astype(np.float32), np.stack(masks).astype(np.float32))


def _gla_kernel(gq_ref, gk_ref, gv_ref, glr_ref, gz_ref, s0_ref, wa_ref, ba_ref, ng_ref, d_ref, m_ref,
                y_ref, sout_ref, s_ref, *, nseq, c):
    ci = pl.program_id(1)
    nlev = int(math.log2(c))
    hk = GLA_HEADS * GLA_DK

    @pl.when(ci == 0)
    def _():
        s_ref[...] = s0_ref[...]

    eye_r = lax.broadcasted_iota(jnp.int32, (GLA_DK, GLA_DK), 0)
    eye_c = lax.broadcasted_iota(jnp.int32, (GLA_DK, GLA_DK), 1)
    eye = eye_r == eye_c

    for n in range(nseq):
        q = gq_ref[n] * (GLA_DK ** -0.5)
        k = gk_ref[n]
        v = gv_ref[n]
        z = _dot(glr_ref[n], wa_ref[...]) + ba_ref[...]
        la = _log_sigmoid(z) * (1.0 / GLA_TAU)
        hi = la.astype(bf16)
        r1 = la - hi.astype(f32)
        mid = r1.astype(bf16)
        lo = (r1 - mid.astype(f32)).astype(bf16)
        hml = jnp.concatenate([hi, mid, lo], axis=1)

        def sums(blk):
            p = jnp.dot(d_ref[blk * c:(blk + 1) * c, :].astype(bf16), hml, preferred_element_type=f32)
            return p[:, :hk] + p[:, hk:2 * hk] + p[:, 2 * hk:]

        b = sums(0)
        q_in = q * jnp.exp(b)
        k_st = k * jnp.exp(sums(1))
        dec_row = jnp.exp(b[c - 1:c, :])
        qf = [q]
        kf = [k]
        for lev in range(nlev):
            qf.append(q * jnp.exp(sums(2 + 2 * lev)))
            kf.append(k * jnp.exp(sums(3 + 2 * lev)))

        for h in range(GLA_HEADS):
            ks = slice(h * GLA_DK, (h + 1) * GLA_DK)
            vs = slice(h * GLA_DV, (h + 1) * GLA_DV)
            att = jnp.zeros((c, c), f32)
            for lev in range(nlev + 1):
                att = att + m_ref[lev] * _dot_nt(qf[lev][:, ks], kf[lev][:, ks])
            s_h = s_ref[n, h]
            v_h = v[:, vs]
            o = _dot(q_in[:, ks], s_h) + _dot(att, v_h)
            dec_col = jnp.sum(jnp.where(eye, jnp.broadcast_to(dec_row[:, ks], (GLA_DK, GLA_DK)), 0.0),
                              axis=1, keepdims=True)
            s_ref[n, h] = s_h * dec_col + _dot_tn(k_st[:, ks], v_h)
            o = o * lax.rsqrt(jnp.mean(o * o, axis=-1, keepdims=True) + RMS_EPS) * ng_ref[...]
            y_ref[n, :, vs] = (o * _silu(gz_ref[n, :, vs])).astype(y_ref.dtype)

    sout_ref[...] = s_ref[...]


def _gla_call(proj, s0, gw, nb, nseq, nc, c, name):
    dstack, masks = _gla_consts(c)
    kern = functools.partial(_gla_kernel, nseq=nseq, c=c)
    hk = GLA_HEADS * GLA_DK
    st_spec = pl.BlockSpec((nseq, GLA_HEADS, GLA_DK, GLA_DV), lambda b, ci: (b, 0, 0, 0))
    return pl.pallas_call(
        kern,
        grid=(nb, nc),
        in_specs=[_seg_spec("gq", nseq, c), _seg_spec("gk", nseq, c), _seg_spec("gv", nseq, c),
                  _seg_spec("glr", nseq, c), _seg_spec("gz", nseq, c), st_spec,
                  _const_spec((LANE, hk)), _const_spec((1, hk)), _const_spec((1, GLA_DV)),
                  _const_spec(dstack.shape), _const_spec(masks.shape)],
        out_specs=[_tok_spec(nseq, c, BRANCH_W), st_spec],
        out_shape=[jax.ShapeDtypeStruct((nb * nseq, nc * c, BRANCH_W), bf16),
                   jax.ShapeDtypeStruct((nb * nseq, GLA_HEADS, GLA_DK, GLA_DV), f32)],
        scratch_shapes=[pltpu.VMEM((nseq, GLA_HEADS, GLA_DK, GLA_DV), f32)],
        compiler_params=_params("parallel", "arbitrary"),
        name=name,
    )(proj, proj, proj, proj, proj, s0, gw["wa"], gw["ba"], gw["ng"], jnp.asarray(dstack), jnp.asarray(masks))


def _gla_step_consts(t, nseq):
    dstack, masks = _gla_consts(t)
    eye = np.eye(nseq, dtype=np.float32)
    dbd = np.stack([np.kron(eye, dstack[i * t:(i + 1) * t]) for i in range(dstack.shape[0] // t)])
    mbd = np.stack([np.kron(eye, m) for m in masks])
    return dbd, mbd


def _gla_step_kernel(gq_ref, gk_ref, gv_ref, glr_ref, gz_ref, s0_ref, wa_ref, ba_ref, ng_ref, d_ref, m_ref,
                     y_ref, sout_ref, *, nseq, t):
    r = nseq * t
    nlev = int(math.log2(t))
    hk = GLA_HEADS * GLA_DK
    q = gq_ref[...].reshape(r, hk) * (GLA_DK ** -0.5)
    k = gk_ref[...].reshape(r, hk)
    v = gv_ref[...].reshape(r, BRANCH_W)
    gz = gz_ref[...].reshape(r, BRANCH_W)
    z = _dot(glr_ref[...].reshape(r, LANE), wa_ref[...]) + ba_ref[...]
    la = _log_sigmoid(z) * (1.0 / GLA_TAU)
    hi = la.astype(bf16)
    r1 = la - hi.astype(f32)
    mid = r1.astype(bf16)
    lo = (r1 - mid.astype(f32)).astype(bf16)
    hml = jnp.concatenate([hi, mid, lo], axis=1)

    def sums(blk):
        p = jnp.dot(d_ref[blk].astype(bf16), hml, preferred_element_type=f32)
        return p[:, :hk] + p[:, hk:2 * hk] + p[:, 2 * hk:]

    b = sums(0)
    q_in = q * jnp.exp(b)
    k_st = k * jnp.exp(sums(1))
    dec3 = jnp.exp(b.reshape(nseq, t, hk)[:, t - 1:t, :])
    qf = [q]
    kf = [k]
    for lev in range(nlev):
        qf.append(q * jnp.exp(sums(2 + 2 * lev)))
        kf.append(k * jnp.exp(sums(3 + 2 * lev)))

    own = (lax.broadcasted_iota(jnp.int32, (r, nseq * GLA_DK), 0) // t
           == lax.broadcasted_iota(jnp.int32, (r, nseq * GLA_DK), 1) // GLA_DK)
    eye = (lax.broadcasted_iota(jnp.int32, (GLA_DK, GLA_DK), 0)
           == lax.broadcasted_iota(jnp.int32, (GLA_DK, GLA_DK), 1))[None]

    def spread(x):
        x2 = jnp.concatenate([x, x], axis=1)
        return jnp.where(own, jnp.concatenate([x2] * (nseq // 2), axis=1), 0.0)

    ys = []
    for h in range(GLA_HEADS):
        ks = slice(h * GLA_DK, (h + 1) * GLA_DK)
        vs = slice(h * GLA_DV, (h + 1) * GLA_DV)
        att = jnp.zeros((r, r), f32)
        for lev in range(nlev + 1):
            att = att + m_ref[lev] * _dot_nt(qf[lev][:, ks], kf[lev][:, ks])
        s_h = s0_ref[:, h]
        v_h = v[:, vs]
        o = _dot(spread(q_in[:, ks]), s_h.reshape(nseq * GLA_DK, GLA_DV)) + _dot(att, v_h)
        upd = _dot_tn(spread(k_st[:, ks]), v_h)
        dec_col = jnp.sum(jnp.where(eye, jnp.broadcast_to(dec3[:, :, ks], (nseq, GLA_DK, GLA_DK)), 0.0),
                          axis=2, keepdims=True)
        sout_ref[:, h] = s_h * dec_col + upd.reshape(nseq, GLA_DK, GLA_DV)
        o = o * lax.rsqrt(jnp.mean(o * o, axis=-1, keepdims=True) + RMS_EPS) * ng_ref[...]
        ys.append(o * _silu(gz[:, vs]))
    y_ref[...] = jnp.concatenate(ys, axis=1).reshape(nseq, t, BRANCH_W).astype(y_ref.dtype)


def _gla_step_call(proj, s0_all, layer, gw, nb, nseq, t, name):
    dbd, mbd = _gla_step_consts(t, nseq)
    kern = functools.partial(_gla_step_kernel, nseq=nseq, t=t)
    hk = GLA_HEADS * GLA_DK
    st_spec = pl.BlockSpec((nseq, GLA_HEADS, GLA_DK, GLA_DV), lambda b, ci: (b, 0, 0, 0))
    s0_spec = pl.BlockSpec((None, nseq, GLA_HEADS, GLA_DK, GLA_DV), lambda b, ci: (layer, b, 0, 0, 0))
    return pl.pallas_call(
        kern,
        grid=(nb, 1),
        in_specs=[_seg_spec("gq", nseq, t), _seg_spec("gk", nseq, t), _seg_spec("gv", nseq, t),
                  _seg_spec("glr", nseq, t), _seg_spec("gz", nseq, t), s0_spec,
                  _const_spec((LANE, hk)), _const_spec((1, hk)), _const_spec((1, GLA_DV)),
                  _const_spec(dbd.shape), _const_spec(mbd.shape)],
        out_specs=[_tok_spec(nseq, t, BRANCH_W), st_spec],
        out_shape=[jax.ShapeDtypeStruct((nb * nseq, t, BRANCH_W), bf16),
                   jax.ShapeDtypeStruct((nb * nseq, GLA_HEADS, GLA_DK, GLA_DV), f32)],
        compiler_params=_params("parallel", "arbitrary"),
        name=name,
    )(proj, proj, proj, proj, proj, s0_all, gw["wa"], gw["ba"], gw["ng"], jnp.asarray(dbd), jnp.asarray(mbd))


def _rope_tables(pos0, t):
    half = ROT_DIM // 2
    inv = ROPE_THETA ** (-jnp.arange(half, dtype=f32) / half)
    ang = (pos0 + jnp.arange(t)).astype(f32)[:, None] * inv[None, :]
    cos, sin = jnp.cos(ang), jnp.sin(ang)
    zeros = jnp.zeros((t, HEAD_DIM - ROT_DIM), f32)
    z8 = jnp.zeros((t, half), f32)
    c_tab = jnp.concatenate([cos, cos, jnp.ones((t, HEAD_DIM - ROT_DIM), f32)], axis=1)
    sa_tab = jnp.concatenate([-sin, z8, zeros], axis=1)
    sb_tab = jnp.concatenate([z8, sin, zeros], axis=1)
    return tuple(jnp.concatenate([x, x], axis=1) for x in (c_tab, sa_tab, sb_tab))


def _rope(x, c_tab, sa_tab, sb_tab):
    wd = x.shape[-1]
    ax = x.ndim - 1
    rep = wd // LANE
    half = ROT_DIM // 2
    if rep > 1:
        c_tab, sa_tab, sb_tab = (jnp.concatenate([tb] * rep, axis=-1) for tb in (c_tab, sa_tab, sb_tab))
    return x * c_tab + pltpu.roll(x, wd - half, ax) * sa_tab + pltpu.roll(x, half, ax) * sb_tab


def _swa_kernel(sink_ref, sq_ref, sk_ref, sv_ref, sz_ref, ct_ref, sat_ref, sbt_ref, kp_ref, vp_ref,
                y_ref, klast_ref, vlast_ref, kprev_ref, vprev_ref, *, nseq, qb, pos0, carry):
    blk = pl.program_id(1)

    @pl.when(blk == 0)
    def _():
        kprev_ref[...] = kp_ref[...]
        vprev_ref[...] = vp_ref[...]

    tabs = (ct_ref[...], sat_ref[...], sbt_ref[...])
    mrows = SWA_GROUP * qb
    qi = lax.broadcasted_iota(jnp.int32, (mrows, WINDOW), 0) % qb
    kj = lax.broadcasted_iota(jnp.int32, (mrows, WINDOW), 1)
    past_ok = (kj >= qi) & (kj >= (WINDOW - pos0) - blk * qb)
    qi_c = lax.broadcasted_iota(jnp.int32, (mrows, qb), 0) % qb
    kj_c = lax.broadcasted_iota(jnp.int32, (mrows, qb), 1)
    cur_ok = kj_c <= qi_c
    neg = -jnp.inf

    for n in range(nseq):
        q = _rope(sq_ref[n], *tabs) * (HEAD_DIM ** -0.5)
        k = _rope(sk_ref[n], *tabs)
        v = sv_ref[n]
        kprev = kprev_ref[n]
        vprev = vprev_ref[n]
        klast_ref[n] = k if qb == WINDOW else jnp.concatenate([kprev[qb:], k], axis=0)
        vlast_ref[n] = v if qb == WINDOW else jnp.concatenate([vprev[qb:], v], axis=0)
        outs = []
        for kv in range(SWA_KV_HEADS):
            ds = slice(kv * HEAD_DIM, (kv + 1) * HEAD_DIM)
            qs = jnp.concatenate(
                [q[:, (kv * SWA_GROUP + g) * HEAD_DIM:(kv * SWA_GROUP + g + 1) * HEAD_DIM] for g in range(SWA_GROUP)],
                axis=0)
            sink = jnp.concatenate(
                [jnp.full((qb, 1), sink_ref[kv * SWA_GROUP + g], f32) for g in range(SWA_GROUP)], axis=0)
            s_p = jnp.where(past_ok, _dot_nt(qs, kprev[:, ds]), neg)
            s_c = jnp.where(cur_ok, _dot_nt(qs, k[:, ds]), neg)
            m = jnp.maximum(jnp.maximum(jnp.max(s_p, axis=1, keepdims=True), jnp.max(s_c, axis=1, keepdims=True)),
                            sink)
            p_p = jnp.exp(s_p - m)
            p_c = jnp.exp(s_c - m)
            den = jnp.sum(p_p, axis=1, keepdims=True) + jnp.sum(p_c, axis=1, keepdims=True) + jnp.exp(sink - m)
            o = (_dot(p_p, vprev[:, ds]) + _dot(p_c, v[:, ds])) / den
            outs.extend(o[g * qb:(g + 1) * qb, :] for g in range(SWA_GROUP))
        o_all = jnp.concatenate(outs, axis=1)
        y_ref[n] = (o_all * _silu(sz_ref[n])).astype(y_ref.dtype)
        if carry:
            kprev_ref[n] = k
            vprev_ref[n] = v


def _swa_call(proj, sinks, k_past, v_past, pos0, nb, nseq, nc, qb, name):
    assert nc == 1 or qb == WINDOW
    t_total = nc * qb
    c_tab, sa_tab, sb_tab = _rope_tables(pos0, t_total)
    kern = functools.partial(_swa_kernel, nseq=nseq, qb=qb, pos0=pos0, carry=nc > 1)
    kvw = SWA_KV_HEADS * HEAD_DIM
    tab_spec = pl.BlockSpec((qb, LANE), lambda b, c: (c, 0))
    past_spec = pl.BlockSpec((nseq, WINDOW, kvw), lambda b, c: (b, 0, 0))
    return pl.pallas_call(
        kern,
        grid=(nb, nc),
        in_specs=[pl.BlockSpec(memory_space=pltpu.SMEM),
                  _seg_spec("sq", nseq, qb), _seg_spec("sk", nseq, qb), _seg_spec("sv", nseq, qb),
                  _seg_spec("sz", nseq, qb), tab_spec, tab_spec, tab_spec, past_spec, past_spec],
        out_specs=[_tok_spec(nseq, qb, BRANCH_W), past_spec, past_spec],
        out_shape=[jax.ShapeDtypeStruct((nb * nseq, nc * qb, BRANCH_W), bf16),
                   jax.ShapeDtypeStruct((nb * nseq, WINDOW, kvw), f32),
                   jax.ShapeDtypeStruct((nb * nseq, WINDOW, kvw), f32)],
        scratch_shapes=[pltpu.VMEM((nseq, WINDOW, kvw), f32), pltpu.VMEM((nseq, WINDOW, kvw), f32)],
        compiler_params=_params("parallel", "arbitrary"),
        name=name,
    )(sinks, proj, proj, proj, proj, c_tab, sa_tab, sb_tab, k_past, v_past)


def _swa_step_kernel(sink_ref, sq_ref, sk_ref, sv_ref, sz_ref, ct_ref, sat_ref, sbt_ref, kp_ref, vp_ref,
                     y_ref, klast_ref, vlast_ref, sp_ref, sc_ref, o_ref, *, nseq, t, pos0):
    hd = HEAD_DIM
    kvw = SWA_KV_HEADS * hd
    mrows = SWA_HEADS * t
    tabs = tuple(r[...][None] for r in (ct_ref, sat_ref, sbt_ref))
    q3 = _rope(sq_ref[...], *tabs) * (hd ** -0.5)
    k3 = _rope(sk_ref[...], *tabs)
    v3 = sv_ref[...]

    lane = lax.broadcasted_iota(jnp.int32, (kvw, WINDOW), 1)
    pad = jnp.zeros((WINDOW - t, kvw), f32)

    def shifted(old_t, new):
        new_t = jnp.concatenate([pad, new], axis=0).T
        out = jnp.where(lane >= WINDOW - t, new_t, pltpu.roll(old_t, WINDOW - t, 1))
        return out.reshape(SWA_KV_HEADS, hd, WINDOW)

    for n in range(nseq):
        klast_ref[n] = shifted(kp_ref[n].reshape(kvw, WINDOW), k3[n])
        vlast_ref[n] = shifted(vp_ref[n].reshape(kvw, WINDOW), v3[n])

    zero = jnp.zeros((nseq, t, hd), f32)
    pieces = []
    for j in range(SWA_HEADS):
        qj = q3[:, :, j * hd:(j + 1) * hd]
        pieces.append(jnp.concatenate([qj, zero] if j // SWA_GROUP == 0 else [zero, qj], axis=2))
    qbd = jnp.concatenate(pieces, axis=1).astype(bf16)

    for n in range(nseq):
        sp_ref[n] = _dot(qbd[n], kp_ref[n].reshape(kvw, WINDOW))
        sc_ref[n] = _dot_nt(qbd[n], k3[n])

    qi = lax.broadcasted_iota(jnp.int32, (mrows, WINDOW), 0) % t
    kj = lax.broadcasted_iota(jnp.int32, (mrows, WINDOW), 1)
    past_ok = kj >= qi
    if pos0 < WINDOW:
        past_ok = past_ok & (kj >= WINDOW - pos0)
    qi_c = lax.broadcasted_iota(jnp.int32, (mrows, t), 0) % t
    kj_c = lax.broadcasted_iota(jnp.int32, (mrows, t), 1)
    cur_ok = kj_c <= qi_c
    sink = jnp.concatenate([jnp.full((t, 1), sink_ref[j], f32) for j in range(SWA_HEADS)], axis=0)[None]
    s_p = jnp.where(past_ok[None], sp_ref[...], -jnp.inf)
    s_c = jnp.where(cur_ok[None], sc_ref[...], -jnp.inf)
    m = jnp.maximum(jnp.maximum(jnp.max(s_p, axis=2, keepdims=True), jnp.max(s_c, axis=2, keepdims=True)), sink)
    p_p = jnp.exp(s_p - m)
    p_c = jnp.exp(s_c - m)
    den = jnp.sum(p_p, axis=2, keepdims=True) + jnp.sum(p_c, axis=2, keepdims=True) + jnp.exp(sink - m)
    p_p = p_p.astype(bf16)
    p_c = p_c.astype(bf16)
    for n in range(nseq):
        o_ref[n] = _dot_nt(p_p[n], vp_ref[n].reshape(kvw, WINDOW)) + _dot(p_c[n], v3[n])
    o = o_ref[...] / den
    outs = []
    for j in range(SWA_HEADS):
        kv = j // SWA_GROUP
        outs.append(o[:, j * t:(j + 1) * t, kv * hd:(kv + 1) * hd])
    y_ref[...] = (jnp.concatenate(outs, axis=2) * _silu(sz_ref[...])).astype(y_ref.dtype)


def _swa_step_call(proj, sinks, k_past_t, v_past_t, layer, pos0, nb, nseq, t, name):
    c_tab, sa_tab, sb_tab = _rope_tables(pos0, t)
    kern = functools.partial(_swa_step_kernel, nseq=nseq, t=t, pos0=pos0)
    kvw = SWA_KV_HEADS * HEAD_DIM
    mrows = SWA_HEADS * t
    tab_spec = pl.BlockSpec((t, LANE), lambda b, c: (0, 0))
    past_spec = pl.BlockSpec((None, nseq, SWA_KV_HEADS, HEAD_DIM, WINDOW), lambda b, c: (layer, b, 0, 0, 0))
    new_spec = pl.BlockSpec((nseq, SWA_KV_HEADS, HEAD_DIM, WINDOW), lambda b, c: (b, 0, 0, 0))
    new_shape = jax.ShapeDtypeStruct((nb * nseq, SWA_KV_HEADS, HEAD_DIM, WINDOW), f32)
    return pl.pallas_call(
        kern,
        grid=(nb, 1),
        in_specs=[pl.BlockSpec(memory_space=pltpu.SMEM),
                  _seg_spec("sq", nseq, t), _seg_spec("sk", nseq, t), _seg_spec("sv", nseq, t),
                  _seg_spec("sz", nseq, t), tab_spec, tab_spec, tab_spec, past_spec, past_spec],
        out_specs=[_tok_spec(nseq, t, BRANCH_W), new_spec, new_spec],
        out_shape=[jax.ShapeDtypeStruct((nb * nseq, t, BRANCH_W), bf16), new_shape, new_shape],
        scratch_shapes=[pltpu.VMEM((nseq, mrows, WINDOW), f32), pltpu.VMEM((nseq, mrows, t), f32),
                        pltpu.VMEM((nseq, mrows, kvw), f32)],
        compiler_params=_params("parallel", "arbitrary"),
        name=name,
    )(sinks, proj, proj, proj, proj, c_tab, sa_tab, sb_tab, k_past_t, v_past_t)


def _sgu_kernel(su_ref, sv_ref, sz_ref, g_ref, b_ref, wm_ref, bias_ref, y_ref, *vn_ref, ntile):
    vn = _layer_norm(sv_ref[...], g_ref[...], b_ref[...])
    if vn_ref:
        vn_ref[0][...] = vn
    for r in range(ntile):
        rows = slice(r * SGU_CHUNK, (r + 1) * SGU_CHUNK)
        mixed = jnp.concatenate(
            [jnp.dot(wm_ref[g], vn[rows, g * SGU_GC:(g + 1) * SGU_GC].astype(bf16), preferred_element_type=f32)
             for g in range(SGU_GROUPS)], axis=1)
        y = su_ref[rows, :] * (mixed + bias_ref[...]) * _silu(sz_ref[rows, :])
        y_ref[rows, :] = y.astype(y_ref.dtype)


def _sgu_call(proj, ln_g, ln_b, wmix, bias, n_tok, ntile, want_vn, name):
    rows = ntile * SGU_CHUNK
    w = BRANCH_W
    kern = functools.partial(_sgu_kernel, ntile=ntile)

    def seg(nm):
        off, width = _SEG[nm]
        return pl.BlockSpec((rows, width), lambda i: (i, off // width))

    def const(shape):
        return pl.BlockSpec(shape, lambda i: (0,) * len(shape))

    out_specs = [pl.BlockSpec((rows, w), lambda i: (i, 0))]
    out_shape = [jax.ShapeDtypeStruct((n_tok, w), bf16)]
    if want_vn:
        out_specs.append(pl.BlockSpec((rows, w), lambda i: (i, 0)))
        out_shape.append(jax.ShapeDtypeStruct((n_tok, w), f32))
    return pl.pallas_call(
        kern,
        grid=(n_tok // rows,),
        in_specs=[seg("su"), seg("svv"), seg("suz"), const((1, w)), const((1, w)),
                  const((SGU_GROUPS, SGU_CHUNK, SGU_CHUNK)), const((SGU_CHUNK, w))],
        out_specs=out_specs,
        out_shape=out_shape,
        compiler_params=_params("parallel"),
        name=name,
    )(proj, proj, proj, ln_g, ln_b, wmix, bias)


def _mem_kernel(mq_ref, mz_ref, mk_ref, mv_ref, y_ref, *, nseq, tq):
    for n in range(nseq):
        q = mq_ref[n] * (HEAD_DIM ** -0.5)
        mk = mk_ref[n]
        mv = mv_ref[n]
        outs = []
        for h in range(MEM_HEADS):
            ds = slice(h * HEAD_DIM, (h + 1) * HEAD_DIM)
            s = _dot_nt(q[:, ds], mk[:, ds])
            m = jnp.max(s, axis=1, keepdims=True)
            p = jnp.exp(s - m)
            den = jnp.sum(p, axis=1, keepdims=True)
            outs.append(_dot(p, mv[:, ds]) / den)
        o = jnp.concatenate(outs, axis=1)
        y_ref[n] = (o * _silu(mz_ref[n])).astype(y_ref.dtype)


def _mem_call(proj, mk, mv, nb, nseq, nc, tq, name):
    kern = functools.partial(_mem_kernel, nseq=nseq, tq=tq)
    kv_spec = pl.BlockSpec((nseq, N_MEM, MEM_W), lambda b, c: (b, 0, 0))
    return pl.pallas_call(
        kern,
        grid=(nb, nc),
        in_specs=[_seg_spec("mq", nseq, tq), _seg_spec("mz", nseq, tq), kv_spec, kv_spec],
        out_specs=_tok_spec(nseq, tq, MEM_W),
        out_shape=jax.ShapeDtypeStruct((nb * nseq, nc * tq, MEM_W), bf16),
        compiler_params=_params("parallel", "arbitrary"),
        name=name,
    )(proj, proj, mk, mv)


def _mem_step_kernel(mq_ref, mz_ref, mk_ref, mv_ref, y_ref, s_ref, o_ref, *, nseq, t):
    mrows = MEM_HEADS * t
    row_head = lax.broadcasted_iota(jnp.int32, (mrows, MEM_W), 0) // t
    lane_head = lax.broadcasted_iota(jnp.int32, (mrows, MEM_W), 1) // HEAD_DIM
    own = (row_head == lane_head)[None]
    q3 = mq_ref[...] * (HEAD_DIM ** -0.5)
    qbd = jnp.where(own, jnp.concatenate([q3] * MEM_HEADS, axis=1), 0.0).astype(bf16)
    for n in range(nseq):
        s_ref[n] = _dot(qbd[n], mk_ref[n].reshape(MEM_W, N_MEM))
    s = s_ref[...]
    p = jnp.exp(s - jnp.max(s, axis=2, keepdims=True))
    den = jnp.sum(p, axis=2, keepdims=True)
    p = p.astype(bf16)
    for n in range(nseq):
        o_ref[n] = _dot_nt(p[n], mv_ref[n].reshape(MEM_W, N_MEM))
    o = jnp.where(own, o_ref[...] / den, 0.0)
    acc = o[:, 0:t, :]
    for h in range(1, MEM_HEADS):
        acc = acc + o[:, h * t:(h + 1) * t, :]
    y_ref[...] = (acc * _silu(mz_ref[...])).astype(y_ref.dtype)


def _mem_step_call(proj, mk_t, mv_t, layer, nb, nseq, t, name):
    kern = functools.partial(_mem_step_kernel, nseq=nseq, t=t)
    kv_spec = pl.BlockSpec((None, nseq, MEM_HEADS, HEAD_DIM, N_MEM), lambda b, c: (layer, b, 0, 0, 0))
    mrows = MEM_HEADS * t
    return pl.pallas_call(
        kern,
        grid=(nb, 1),
        in_specs=[_seg_spec("mq", nseq, t), _seg_spec("mz", nseq, t), kv_spec, kv_spec],
        out_specs=_tok_spec(nseq, t, MEM_W),
        out_shape=jax.ShapeDtypeStruct((nb * nseq, t, MEM_W), bf16),
        scratch_shapes=[pltpu.VMEM((nseq, mrows, N_MEM), f32), pltpu.VMEM((nseq, mrows, MEM_W), f32)],
        compiler_params=_params("parallel", "arbitrary"),
        name=name,
    )(proj, proj, mk_t, mv_t)


def _merge_kernel(yg_ref, yl_ref, ys_ref, yu_ref, ym_ref, x_ref, wg_ref, wb_ref, wm_ref, wo_ref, g_ref, b_ref,
                  o_ref):
    d = D_MODEL
    x = x_ref[...]
    xb = x.astype(bf16)

    def gate(n):
        return _sigmoid(_dot_nt(xb, wg_ref[n * d:(n + 1) * d, :]))

    merged = gate(4) * jnp.dot(ym_ref[...], wm_ref[...], preferred_element_type=f32)
    for n, y_ref in enumerate((yg_ref, yl_ref, ys_ref, yu_ref)):
        merged = merged + gate(n) * jnp.dot(y_ref[...], wb_ref[n], preferred_element_type=f32)
    out = _dot(merged, wo_ref[...])
    o_ref[...] = _layer_norm(DN_ALPHA * x + out, g_ref[...], b_ref[...])


def _merge_call(ys, x, mw, tm, name):
    n_tok, d = x.shape
    w = BRANCH_W

    def rows(width):
        return pl.BlockSpec((tm, width), lambda i: (i, 0))

    def const(shape):
        return pl.BlockSpec(shape, lambda i: (0,) * len(shape), pipeline_mode=pl.Buffered(1))

    return pl.pallas_call(
        _merge_kernel,
        grid=(n_tok // tm,),
        in_specs=[rows(w), rows(w), rows(w), rows(w), rows(MEM_W), rows(d),
                  const((5 * d, d)), const((4, w, d)), const((MEM_W, d)), const((d, d)), const((1, d)),
                  const((1, d))],
        out_specs=rows(d),
        out_shape=jax.ShapeDtypeStruct((n_tok, d), f32),
        compiler_params=_params("parallel"),
        name=name,
    )(*ys, x, mw["wg"], mw["wb"], mw["wm"], mw["wo"], mw["g"], mw["b"])


def _prep_layer(l, w_in, gla_wa2, gla_ba, gla_norm_g, lru_conv_w, lru_conv_b, lru_wr, lru_br, lru_wi, lru_bi, lru_L,
                swa_sinks, sgu_ln_g, sgu_ln_b, sgu_w, sgu_b, w_mem_kv, w_branch, w_branch_mem, w_out, ln_g, ln_b):
    d = D_MODEL
    w = BRANCH_W
    w_t = jnp.swapaxes(w_in, 1, 2)[l]
    runs = []
    used = 0
    for a, b in _PROJ_RUNS:
        lo = _ORIG_OFF[a][0]
        hi = _ORIG_OFF[b][0] + _ORIG_OFF[b][1]
        runs.append(w_t[lo:hi].astype(bf16))
        used += hi - lo
    runs.append(jnp.zeros((N_PROJ - used, d), bf16))
    w_proj = jnp.concatenate(runs, axis=0)
    g_lo = _ORIG_OFF["gates"][0]
    w_gates = w_t[g_lo:].astype(bf16)

    def block_diag(wb):
        eye = jnp.eye(LRU_BLOCKS, dtype=f32)
        return (eye[:, None, :, None] * wb[:, :, None, :]).reshape(w, w).astype(bf16)

    tril = jnp.tril(jnp.ones((SGU_CHUNK, SGU_CHUNK), f32))
    wmix_p = (sgu_w[l] * tril).astype(bf16)
    bias_p = jnp.repeat(sgu_b[l].T, SGU_GC, axis=1)
    t8 = SUBLANE
    rep = SGU_CHUNK // t8
    w8 = (sgu_w[l] * tril)[:, :t8, :t8]
    seq_eye = jnp.eye(rep, dtype=f32)
    wmix_s = (seq_eye[None, :, None, :, None] * w8[:, None, :, None, :]).reshape(
        SGU_GROUPS, SGU_CHUNK, SGU_CHUNK).astype(bf16)
    bias_s = jnp.tile(bias_p[:t8], (rep, 1))
    return dict(
        w_proj=w_proj,
        w_mem_kv=w_mem_kv[l].astype(bf16),
        gla=dict(wa=jnp.pad(gla_wa2[l], ((0, LANE - GLA_RANK), (0, 0))).astype(bf16),
                 ba=gla_ba[l].reshape(1, -1), ng=gla_norm_g[l].reshape(1, -1)),
        lru=dict(conv_w=lru_conv_w[l], conv_b=lru_conv_b[l].reshape(1, w), wr=block_diag(lru_wr[l]),
                 br=lru_br[l].reshape(1, w), wi=block_diag(lru_wi[l]), bi=lru_bi[l].reshape(1, w),
                 lam=lru_L[l].reshape(1, w)),
        sinks=swa_sinks[l],
        sgu=dict(g=sgu_ln_g[l].reshape(1, w), b=sgu_ln_b[l].reshape(1, w), wmix_p=wmix_p, bias_p=bias_p,
                 wmix_s=wmix_s, bias_s=bias_s),
        merge=dict(wg=w_gates, wb=w_branch[l].astype(bf16), wm=w_branch_mem[l].astype(bf16), wo=w_out[l].astype(bf16),
                   g=ln_g[l].reshape(1, d), b=ln_b[l].reshape(1, d)),
    )


def _layer(x, lw, grp, st, layer, tag):
    nseq_total, t = grp["batch"], grp["seq"]
    n_tok = nseq_total * t
    proj = _matmul_call(x, lw["w_proj"], min(1024, n_tok), 1024, "proj_" + tag, w_transposed=True)
    proj3 = proj.reshape(nseq_total, t, N_PROJ)

    lt = grp["lru"]
    y_lru, hlast, hist = _lru_call(proj3, st["hist0"], st["h0"], lw["lru"], nseq_total // lt[0], lt[0], t // lt[1],
                                   lt[1], "lru_" + tag)
    short = grp["kind"] == "s"
    gt = grp["gla"]
    if short:
        y_gla, s_out = _gla_step_call(proj3, st["gla0"], layer, lw["gla"], nseq_total // gt[0], gt[0], t,
                                      "gla_" + tag)
    else:
        y_gla, s_out = _gla_call(proj3, st["gla0"], lw["gla"], nseq_total // gt[0], gt[0], t // gt[1], gt[1],
                                 "gla_" + tag)
    wt = grp["swa"]
    if short:
        y_swa, k_last, v_last = _swa_step_call(proj3, lw["sinks"], st["k_past"], st["v_past"], layer, grp["pos0"],
                                               nseq_total // wt[0], wt[0], t, "swa_" + tag)
    else:
        y_swa, k_last, v_last = _swa_call(proj3, lw["sinks"], st["k_past"], st["v_past"], grp["pos0"],
                                          nseq_total // wt[0], wt[0], t // wt[1], wt[1], "swa_" + tag)
    sg = lw["sgu"]
    sgu_out = _sgu_call(proj, sg["g"], sg["b"], sg["wmix_" + grp["kind"]], sg["bias_" + grp["kind"]], n_tok,
                        grp["sgu_tiles"], grp["kind"] == "s", "sgu_" + tag)
    mt = grp["mem"]
    if short:
        y_mem = _mem_step_call(proj3, st["mk"], st["mv"], layer, nseq_total // mt[0], mt[0], t, "mem_" + tag)
    else:
        y_mem = _mem_call(proj3, st["mk"], st["mv"], nseq_total // mt[0], mt[0], t // mt[1], mt[1], "mem_" + tag)
    ys = tuple(y.reshape(n_tok, y.shape[-1]) for y in (y_gla, y_lru, y_swa, sgu_out[0], y_mem))
    x_new = _merge_call(ys, x, lw["merge"], min(256, n_tok), "merge_" + tag)
    return x_new, dict(gla=s_out, hlast=hlast, hist=hist, k_last=k_last, v_last=v_last,
                       vn=sgu_out[1] if len(sgu_out) > 1 else None)


_PROMPT = dict(kind="p", pos0=0, lru=(1, 256), gla=(2, 128), swa=(1, 128), sgu_tiles=4, mem=(1, 512))
_SAMPLE = dict(kind="s", pos0=PAST_LEN, lru=(32, 8), gla=(16, 8), swa=(16, 8), sgu_tiles=8, mem=(16, 8))


def kernel(x_prompt, x_sample, mem_prompt, state_gla, state_lru_h, state_lru_conv, cache_swa_k, cache_swa_v,
           cache_mem_k, cache_mem_v, ln_in_g, ln_in_b, w_in, gla_wa2, gla_ba, gla_norm_g, lru_conv_w, lru_conv_b,
           lru_wr, lru_br, lru_wi, lru_bi, lru_L, swa_sinks, sgu_ln_g, sgu_ln_b, sgu_w, sgu_b, w_mem_kv, w_branch,
           w_branch_mem, w_out, ln_g, ln_b):
    bp, tp, d = x_prompt.shape
    bs, ts, _ = x_sample.shape
    w = BRANCH_W
    kvw = SWA_KV_HEADS * HEAD_DIM
    gp = dict(_PROMPT, batch=bp, seq=tp)
    gs = dict(_SAMPLE, batch=bs, seq=ts)

    xp = _ln_call(x_prompt.reshape(bp * tp, d), ln_in_g, ln_in_b)
    xs = _ln_call(x_sample.reshape(bs * ts, d), ln_in_g, ln_in_b)
    mem2 = mem_prompt.reshape(bp * N_MEM, d)

    swa_k_t, swa_v_t, mem_k_t, mem_v_t = (jnp.transpose(c, (0, 1, 3, 4, 2))
                                          for c in (cache_swa_k, cache_swa_v, cache_mem_k, cache_mem_v))

    outs_p, outs_s, mks, mvs = [], [], [], []
    for l in range(DEPTH):
        lw = _prep_layer(l, w_in, gla_wa2, gla_ba, gla_norm_g, lru_conv_w, lru_conv_b, lru_wr, lru_br, lru_wi,
                         lru_bi, lru_L, swa_sinks, sgu_ln_g, sgu_ln_b, sgu_w, sgu_b, w_mem_kv, w_branch,
                         w_branch_mem, w_out, ln_g, ln_b)
        mkv = _matmul_call(mem2, lw["w_mem_kv"], bp * N_MEM, 2 * MEM_W, "memkv_%d" % l)
        mk = mkv[:, :MEM_W].reshape(bp, N_MEM, MEM_W)
        mv = mkv[:, MEM_W:].reshape(bp, N_MEM, MEM_W)
        st_p = dict(hist0=jnp.zeros((bp, SUBLANE, w), f32), h0=jnp.zeros((bp, 1, w), f32),
                    gla0=jnp.zeros((bp, GLA_HEADS, GLA_DK, GLA_DV), f32),
                    k_past=jnp.zeros((bp, WINDOW, kvw), f32), v_past=jnp.zeros((bp, WINDOW, kvw), f32),
                    mk=mk, mv=mv)
        st_s = dict(hist0=jnp.pad(state_lru_conv[l], ((0, 0), (SUBLANE - (CONV_W - 1), 0), (0, 0))),
                    h0=state_lru_h[l][:, None, :], gla0=state_gla,
                    k_past=swa_k_t, v_past=swa_v_t, mk=mem_k_t, mv=mem_v_t)
        xp, op = _layer(xp, lw, gp, st_p, l, "p%d" % l)
        xs, os_ = _layer(xs, lw, gs, st_s, l, "s%d" % l)
        outs_p.append(op)
        outs_s.append(os_)
        mks.append(mk.reshape(bp, N_MEM, MEM_HEADS, HEAD_DIM))
        mvs.append(mv.reshape(bp, N_MEM, MEM_HEADS, HEAD_DIM))

    def stack(outs, fn):
        return jnp.stack([fn(o) for o in outs])

    def window(a):
        return a.reshape(a.shape[0], WINDOW, SWA_KV_HEADS, HEAD_DIM)

    def window_t(a):
        return jnp.transpose(a, (0, 3, 1, 2))

    return (
        xp.reshape(bp, tp, d), xs.reshape(bs, ts, d),
        stack(outs_p, lambda o: o["gla"]), stack(outs_s, lambda o: o["gla"]),
        stack(outs_p, lambda o: o["hlast"][:, SUBLANE - 1]), stack(outs_s, lambda o: o["hlast"][:, SUBLANE - 1]),
        stack(outs_p, lambda o: o["hist"][:, SUBLANE - (CONV_W - 1):]),
        stack(outs_s, lambda o: o["hist"][:, SUBLANE - (CONV_W - 1):]),
        stack(outs_p, lambda o: window(o["k_last"])), stack(outs_s, lambda o: window_t(o["k_last"])),
        stack(outs_p, lambda o: window(o["v_last"])), stack(outs_s, lambda o: window_t(o["v_last"])),
        jnp.stack(mks), jnp.stack(mvs),
        stack(outs_s, lambda o: o["vn"].reshape(bs, ts, w)),
    )
```

```python
import functools
import math

import jax
import jax.numpy as jnp
import numpy as np
from jax import lax
from jax.experimental import pallas as pl
from jax.experimental.pallas import tpu as pltpu

f32 = jnp.float32
bf16 = jnp.bfloat16

D_MODEL = 1024
DEPTH = 2
PAST_LEN = 8192
BRANCH_W = 512
GLA_HEADS = 4
GLA_DK = 64
GLA_DV = 128
GLA_RANK = 16
GLA_TAU = 16.0
LRU_BLOCKS = 8
LRU_BS = 64
CONV_W = 4
LRU_C = 8.0
HEAD_DIM = 64
SWA_HEADS = 8
SWA_KV_HEADS = 2
SWA_GROUP = 4
WINDOW = 128
ROT_DIM = 16
ROPE_THETA = 500000.0
SGU_GROUPS = 4
SGU_GC = 128
SGU_CHUNK = 128
N_MEM = 256
MEM_HEADS = 4
MEM_W = 256
LN_EPS = 1e-5
RMS_EPS = 1e-6
DN_ALPHA = (2 * DEPTH) ** 0.25

LANE = 128
SUBLANE = 8

_ORIG = (("gq", 256), ("gk", 256), ("gv", 512), ("glr", 16), ("gz", 512), ("lx", 512), ("lz", 512), ("sq", 512),
         ("sk", 128), ("sv", 128), ("sz", 512), ("su", 512), ("svv", 512), ("suz", 512), ("mq", 256), ("mz", 256),
         ("gates", 5 * D_MODEL))
_ORIG_OFF = {}
_off = 0
for _n, _w in _ORIG:
    _ORIG_OFF[_n] = (_off, _w)
    _off += _w
_PROJ_RUNS = (("gq", "gv"), ("gz", "sq"), ("sz", "mz"), ("sk", "sv"), ("glr", "glr"))
_SEG = {}
_off = 0
for _a, _b in _PROJ_RUNS:
    _names = [n for n, _ in _ORIG]
    for _n in _names[_names.index(_a):_names.index(_b) + 1]:
        _w = max(_ORIG_OFF[_n][1], 128)
        assert _off % _w == 0
        _SEG[_n] = (_off, _w)
        _off += _w
N_PROJ = -(-_off // 1024) * 1024


def _dot(a, b):
    return jnp.dot(a.astype(bf16), b.astype(bf16), preferred_element_type=f32)


def _dot_nt(a, b):
    return lax.dot_general(a.astype(bf16), b.astype(bf16), (((1,), (1,)), ((), ())), preferred_element_type=f32)


def _dot_tn(a, b):
    return lax.dot_general(a.astype(bf16), b.astype(bf16), (((0,), (0,)), ((), ())), preferred_element_type=f32)


def _sigmoid(x):
    return 1.0 / (1.0 + jnp.exp(-x))


def _silu(x):
    return x * _sigmoid(x)


def _log_sigmoid(x):
    return jnp.minimum(x, 0.0) - jnp.log1p(jnp.exp(-jnp.abs(x)))


def _layer_norm(x, g, b):
    mu = jnp.mean(x, axis=-1, keepdims=True)
    xc = x - mu
    var = jnp.mean(xc * xc, axis=-1, keepdims=True)
    return xc * lax.rsqrt(var + LN_EPS) * g + b


def _params(*sem):
    return pltpu.CompilerParams(dimension_semantics=sem)


def _ln_kernel(x_ref, g_ref, b_ref, o_ref, ob_ref):
    y = _layer_norm(x_ref[...], g_ref[...], b_ref[...])
    o_ref[...] = y
    ob_ref[...] = y.astype(bf16)


def _ln_call(x, g, b, tm=512):
    n, d = x.shape
    tm = min(tm, n)
    return pl.pallas_call(
        _ln_kernel,
        grid=(n // tm,),
        in_specs=[pl.BlockSpec((tm, d), lambda i: (i, 0)), pl.BlockSpec((1, d), lambda i: (0, 0)),
                  pl.BlockSpec((1, d), lambda i: (0, 0))],
        out_specs=[pl.BlockSpec((tm, d), lambda i: (i, 0)), pl.BlockSpec((tm, d), lambda i: (i, 0))],
        out_shape=[jax.ShapeDtypeStruct((n, d), f32), jax.ShapeDtypeStruct((n, d), bf16)],
        compiler_params=_params("parallel"),
        name="ln_in",
    )(x, g.reshape(1, d), b.reshape(1, d))


def _matmul_kernel(x_ref, w_ref, o_ref, *, w_transposed):
    o_ref[...] = (_dot_nt if w_transposed else _dot)(x_ref[...], w_ref[...])


def _matmul_call(x, w, tm, tn, name, w_transposed=False):
    m, k = x.shape
    n = w.shape[0] if w_transposed else w.shape[1]
    w_spec = pl.BlockSpec((tn, k), lambda i, j: (j, 0)) if w_transposed else pl.BlockSpec((k, tn), lambda i, j: (0, j))
    return pl.pallas_call(
        functools.partial(_matmul_kernel, w_transposed=w_transposed),
        grid=(m // tm, n // tn),
        in_specs=[pl.BlockSpec((tm, k), lambda i, j: (i, 0)), w_spec],
        out_specs=pl.BlockSpec((tm, tn), lambda i, j: (i, j)),
        out_shape=jax.ShapeDtypeStruct((m, n), f32),
        compiler_params=_params("parallel", "arbitrary"),
        name=name,
    )(x, w)


def _tok_spec(nseq, rows, width):
    return pl.BlockSpec((nseq, rows, width), lambda b, c: (b, c, 0))


def _w_spec(first, last):
    off = _SEG[first][0]
    rows = _SEG[last][0] + _SEG[last][1] - off
    assert off % rows == 0
    return pl.BlockSpec((rows, D_MODEL), lambda *_: (off // rows, 0))


def _seg_scratch(name, *lead):
    return pltpu.VMEM((*lead, _SEG[name][1]), f32)


def _project(x_ref, pairs):
    x2 = x_ref[...].reshape(-1, D_MODEL)
    for w_ref, seg_refs in pairs:
        p = lax.dot_general(x2, w_ref[...], (((1,), (1,)), ((), ())), preferred_element_type=f32)
        off = 0
        for s_ref in seg_refs:
            width = s_ref.shape[-1]
            s_ref[...] = p[:, off:off + width].reshape(s_ref.shape)
            off += width


def _const_spec(shape):
    nd = len(shape)
    return pl.BlockSpec(shape, lambda b, c: (0,) * nd)


def _lru_kernel(x_ref, wlx_ref, wlz_ref, hist0_ref, h0_ref, cw_ref, cb_ref, wr_ref, br_ref, wi_ref, bi_ref, lam_ref,
                y_ref, hlast_ref, hist_out_ref, hist_ref, hc_ref, lx_ref, lz_ref, *, nseq, tc):
    c = pl.program_id(1)
    w = BRANCH_W
    _project(x_ref, ((wlx_ref, (lx_ref,)), (wlz_ref, (lz_ref,))))

    @pl.when(c == 0)
    def _():
        hist_ref[...] = hist0_ref[...]
        hc_ref[...] = h0_ref[...]

    x = lx_ref[...]
    xfull = jnp.concatenate([hist_ref[...], x], axis=1)

    def tap(j):
        return cw_ref[j:j + 1, :].reshape(1, 1, w)

    y = cb_ref[...].reshape(1, 1, w) + x * tap(CONV_W - 1)
    for s in range(1, CONV_W):
        y = y + pltpu.roll(xfull, s, 1)[:, SUBLANE:, :] * tap(CONV_W - 1 - s)
    hist_ref[...] = xfull[:, tc:, :]
    hist_out_ref[...] = xfull[:, tc:, :]

    xc = y.reshape(nseq * tc, w)
    r = _sigmoid(_dot(xc, wr_ref[...]) + br_ref[...])
    i = _sigmoid(_dot(xc, wi_ref[...]) + bi_ref[...])
    log_a = (LRU_C * r) * _log_sigmoid(lam_ref[...])
    a = jnp.exp(log_a)
    u = jnp.sqrt(jnp.tanh(-log_a) * (a * a + 1.0)) * (i * xc)

    acc_a = a.reshape(nseq, tc, w)
    acc_u = u.reshape(nseq, tc, w)
    t = lax.broadcasted_iota(jnp.int32, (nseq, tc, w), 1)
    d = 1
    while d < tc:
        ok = t >= d
        a_sh = jnp.where(ok, pltpu.roll(acc_a, d, 1), 1.0)
        u_sh = jnp.where(ok, pltpu.roll(acc_u, d, 1), 0.0)
        acc_u = acc_a * u_sh + acc_u
        acc_a = acc_a * a_sh
        d *= 2
    h = acc_a * hc_ref[...] + acc_u
    hc_ref[...] = h[:, tc - 1:tc, :]
    hlast_ref[...] = h[:, tc - SUBLANE:, :]
    y_ref[...] = (h * _silu(lz_ref[...])).astype(y_ref.dtype)


def _lru_call(xb, w_t, hist0, h0, lw, nb, nseq, nc, tc, name):
    w = BRANCH_W
    kern = functools.partial(_lru_kernel, nseq=nseq, tc=tc)
    return pl.pallas_call(
        kern,
        grid=(nb, nc),
        in_specs=[_tok_spec(nseq, tc, D_MODEL), _w_spec("lx", "lx"), _w_spec("lz", "lz"),
                  pl.BlockSpec((nseq, SUBLANE, w), lambda b, c: (b, 0, 0)),
                  pl.BlockSpec((nseq, 1, w), lambda b, c: (b, 0, 0)),
                  _const_spec((CONV_W, w)), _const_spec((1, w)), _const_spec((w, w)), _const_spec((1, w)),
                  _const_spec((w, w)), _const_spec((1, w)), _const_spec((1, w))],
        out_specs=[_tok_spec(nseq, tc, w),
                   pl.BlockSpec((nseq, SUBLANE, w), lambda b, c: (b, 0, 0)),
                   pl.BlockSpec((nseq, SUBLANE, w), lambda b, c: (b, 0, 0))],
        out_shape=[jax.ShapeDtypeStruct((nb * nseq, nc * tc, w), bf16),
                   jax.ShapeDtypeStruct((nb * nseq, SUBLANE, w), f32),
                   jax.ShapeDtypeStruct((nb * nseq, SUBLANE, w), f32)],
        scratch_shapes=[pltpu.VMEM((nseq, SUBLANE, w), f32), pltpu.VMEM((nseq, 1, w), f32),
                        _seg_scratch("lx", nseq, tc), _seg_scratch("lz", nseq, tc)],
        compiler_params=_params("parallel", "arbitrary"),
        name=name,
    )(xb, w_t, w_t, hist0, h0, lw["conv_w"], lw["conv_b"], lw["wr"], lw["br"], lw["wi"], lw["bi"], lw["lam"])


def _gla_consts(c):
    t = np.arange(c)[:, None]
    u = np.arange(c)[None, :]
    blocks = [u <= t, u > t]
    masks = [t == u]
    m = 1
    while m < c:
        t0 = (t // m) * m
        odd = (t // m) % 2 == 1
        blocks.append(odd & (u >= t0) & (u <= t))
        blocks.append((~odd) & (u > t) & (u <= t0 + m - 1))
        masks.append((t // (2 * m) == u // (2 * m)) & odd & ((u // m) % 2 == 0))
        m *= 2
    return (np.concatenate(blocks, 0).astype(np.float32), np.stack(masks).astype(np.float32))


def _gla_kernel(x_ref, wg_ref, wglr_ref, s0_ref, wa_ref, ba_ref, ng_ref, d_ref, m_ref,
                y_ref, sout_ref, s_ref, gq_ref, gk_ref, gv_ref, gz_ref, glr_ref, *, nseq, c):
    ci = pl.program_id(1)
    nlev = int(math.log2(c))
    hk = GLA_HEADS * GLA_DK
    _project(x_ref, ((wg_ref, (gq_ref, gk_ref, gv_ref, gz_ref)), (wglr_ref, (glr_ref,))))

    @pl.when(ci == 0)
    def _():
        s_ref[...] = s0_ref[...]

    eye_r = lax.broadcasted_iota(jnp.int32, (GLA_DK, GLA_DK), 0)
    eye_c = lax.broadcasted_iota(jnp.int32, (GLA_DK, GLA_DK), 1)
    eye = eye_r == eye_c

    for n in range(nseq):
        q = gq_ref[n] * (GLA_DK ** -0.5)
        k = gk_ref[n]
        v = gv_ref[n]
        z = _dot(glr_ref[n], wa_ref[...]) + ba_ref[...]
        la = _log_sigmoid(z) * (1.0 / GLA_TAU)
        hi = la.astype(bf16)
        r1 = la - hi.astype(f32)
        mid = r1.astype(bf16)
        lo = (r1 - mid.astype(f32)).astype(bf16)
        hml = jnp.concatenate([hi, mid, lo], axis=1)

        def sums(blk):
            p = jnp.dot(d_ref[blk * c:(blk + 1) * c, :].astype(bf16), hml, preferred_element_type=f32)
            return p[:, :hk] + p[:, hk:2 * hk] + p[:, 2 * hk:]

        b = sums(0)
        q_in = q * jnp.exp(b)
        k_st = k * jnp.exp(sums(1))
        dec_row = jnp.exp(b[c - 1:c, :])
        qf = [q]
        kf = [k]
        for lev in range(nlev):
            qf.append(q * jnp.exp(sums(2 + 2 * lev)))
            kf.append(k * jnp.exp(sums(3 + 2 * lev)))

        for h in range(GLA_HEADS):
            ks = slice(h * GLA_DK, (h + 1) * GLA_DK)
            vs = slice(h * GLA_DV, (h + 1) * GLA_DV)
            att = jnp.zeros((c, c), f32)
            for lev in range(nlev + 1):
                att = att + m_ref[lev] * _dot_nt(qf[lev][:, ks], kf[lev][:, ks])
            s_h = s_ref[n, h]
            v_h = v[:, vs]
            o = _dot(q_in[:, ks], s_h) + _dot(att, v_h)
            dec_col = jnp.sum(jnp.where(eye, jnp.broadcast_to(dec_row[:, ks], (GLA_DK, GLA_DK)), 0.0),
                              axis=1, keepdims=True)
            s_ref[n, h] = s_h * dec_col + _dot_tn(k_st[:, ks], v_h)
            o = o * lax.rsqrt(jnp.mean(o * o, axis=-1, keepdims=True) + RMS_EPS) * ng_ref[...]
            y_ref[n, :, vs] = (o * _silu(gz_ref[n, :, vs])).astype(y_ref.dtype)

    sout_ref[...] = s_ref[...]


_GLA_SEGS = ("gq", "gk", "gv", "gz", "glr")


def _gla_call(xb, w_t, s0, gw, nb, nseq, nc, c, name):
    dstack, masks = _gla_consts(c)
    kern = functools.partial(_gla_kernel, nseq=nseq, c=c)
    hk = GLA_HEADS * GLA_DK
    st_spec = pl.BlockSpec((nseq, GLA_HEADS, GLA_DK, GLA_DV), lambda b, ci: (b, 0, 0, 0))
    return pl.pallas_call(
        kern,
        grid=(nb, nc),
        in_specs=[_tok_spec(nseq, c, D_MODEL), _w_spec("gq", "gz"), _w_spec("glr", "glr"), st_spec,
                  _const_spec((LANE, hk)), _const_spec((1, hk)), _const_spec((1, GLA_DV)),
                  _const_spec(dstack.shape), _const_spec(masks.shape)],
        out_specs=[_tok_spec(nseq, c, BRANCH_W), st_spec],
        out_shape=[jax.ShapeDtypeStruct((nb * nseq, nc * c, BRANCH_W), bf16),
                   jax.ShapeDtypeStruct((nb * nseq, GLA_HEADS, GLA_DK, GLA_DV), f32)],
        scratch_shapes=[pltpu.VMEM((nseq, GLA_HEADS, GLA_DK, GLA_DV), f32)]
        + [_seg_scratch(s, nseq, c) for s in _GLA_SEGS],
        compiler_params=_params("parallel", "arbitrary"),
        name=name,
    )(xb, w_t, w_t, s0, gw["wa"], gw["ba"], gw["ng"], jnp.asarray(dstack), jnp.asarray(masks))


def _gla_step_consts(t, nseq):
    dstack, masks = _gla_consts(t)
    eye = np.eye(nseq, dtype=np.float32)
    dbd = np.stack([np.kron(eye, dstack[i * t:(i + 1) * t]) for i in range(dstack.shape[0] // t)])
    mbd = np.stack([np.kron(eye, m) for m in masks])
    return dbd, mbd


def _gla_step_kernel(x_ref, wg_ref, wglr_ref, s0_ref, wa_ref, ba_ref, ng_ref, d_ref, m_ref,
                     y_ref, sout_ref, gq_ref, gk_ref, gv_ref, gz_ref, glr_ref, *, nseq, t):
    r = nseq * t
    nlev = int(math.log2(t))
    hk = GLA_HEADS * GLA_DK
    _project(x_ref, ((wg_ref, (gq_ref, gk_ref, gv_ref, gz_ref)), (wglr_ref, (glr_ref,))))
    q = gq_ref[...].reshape(r, hk) * (GLA_DK ** -0.5)
    k = gk_ref[...].reshape(r, hk)
    v = gv_ref[...].reshape(r, BRANCH_W)
    gz = gz_ref[...].reshape(r, BRANCH_W)
    z = _dot(glr_ref[...].reshape(r, LANE), wa_ref[...]) + ba_ref[...]
    la = _log_sigmoid(z) * (1.0 / GLA_TAU)
    hi = la.astype(bf16)
    r1 = la - hi.astype(f32)
    mid = r1.astype(bf16)
    lo = (r1 - mid.astype(f32)).astype(bf16)
    hml = jnp.concatenate([hi, mid, lo], axis=1)

    def sums(blk):
        p = jnp.dot(d_ref[blk].astype(bf16), hml, preferred_element_type=f32)
        return p[:, :hk] + p[:, hk:2 * hk] + p[:, 2 * hk:]

    b = sums(0)
    q_in = q * jnp.exp(b)
    k_st = k * jnp.exp(sums(1))
    dec3 = jnp.exp(b.reshape(nseq, t, hk)[:, t - 1:t, :])
    qf = [q]
    kf = [k]
    for lev in range(nlev):
        qf.append(q * jnp.exp(sums(2 + 2 * lev)))
        kf.append(k * jnp.exp(sums(3 + 2 * lev)))

    own = (lax.broadcasted_iota(jnp.int32, (r, nseq * GLA_DK), 0) // t
           == lax.broadcasted_iota(jnp.int32, (r, nseq * GLA_DK), 1) // GLA_DK)
    eye = (lax.broadcasted_iota(jnp.int32, (GLA_DK, GLA_DK), 0)
           == lax.broadcasted_iota(jnp.int32, (GLA_DK, GLA_DK), 1))[None]

    def spread(x):
        x2 = jnp.concatenate([x, x], axis=1)
        return jnp.where(own, jnp.concatenate([x2] * (nseq // 2), axis=1), 0.0)

    ys = []
    for h in range(GLA_HEADS):
        ks = slice(h * GLA_DK, (h + 1) * GLA_DK)
        vs = slice(h * GLA_DV, (h + 1) * GLA_DV)
        att = jnp.zeros((r, r), f32)
        for lev in range(nlev + 1):
            att = att + m_ref[lev] * _dot_nt(qf[lev][:, ks], kf[lev][:, ks])
        s_h = s0_ref[:, h]
        v_h = v[:, vs]
        o = _dot(spread(q_in[:, ks]), s_h.reshape(nseq * GLA_DK, GLA_DV)) + _dot(att, v_h)
        upd = _dot_tn(spread(k_st[:, ks]), v_h)
        dec_col = jnp.sum(jnp.where(eye, jnp.broadcast_to(dec3[:, :, ks], (nseq, GLA_DK, GLA_DK)), 0.0),
                          axis=2, keepdims=True)
        sout_ref[:, h] = s_h * dec_col + upd.reshape(nseq, GLA_DK, GLA_DV)
        o = o * lax.rsqrt(jnp.mean(o * o, axis=-1, keepdims=True) + RMS_EPS) * ng_ref[...]
        ys.append(o * _silu(gz[:, vs]))
    y_ref[...] = jnp.concatenate(ys, axis=1).reshape(nseq, t, BRANCH_W).astype(y_ref.dtype)


def _gla_step_call(xb, w_t, s0_all, layer, gw, nb, nseq, t, name):
    dbd, mbd = _gla_step_consts(t, nseq)
    kern = functools.partial(_gla_step_kernel, nseq=nseq, t=t)
    hk = GLA_HEADS * GLA_DK
    st_spec = pl.BlockSpec((nseq, GLA_HEADS, GLA_DK, GLA_DV), lambda b, ci: (b, 0, 0, 0))
    s0_spec = pl.BlockSpec((None, nseq, GLA_HEADS, GLA_DK, GLA_DV), lambda b, ci: (layer, b, 0, 0, 0))
    return pl.pallas_call(
        kern,
        grid=(nb, 1),
        in_specs=[_tok_spec(nseq, t, D_MODEL), _w_spec("gq", "gz"), _w_spec("glr", "glr"), s0_spec,
                  _const_spec((LANE, hk)), _const_spec((1, hk)), _const_spec((1, GLA_DV)),
                  _const_spec(dbd.shape), _const_spec(mbd.shape)],
        out_specs=[_tok_spec(nseq, t, BRANCH_W), st_spec],
        out_shape=[jax.ShapeDtypeStruct((nb * nseq, t, BRANCH_W), bf16),
                   jax.ShapeDtypeStruct((nb * nseq, GLA_HEADS, GLA_DK, GLA_DV), f32)],
        scratch_shapes=[_seg_scratch(s, nseq, t) for s in _GLA_SEGS],
        compiler_params=_params("parallel", "arbitrary"),
        name=name,
    )(xb, w_t, w_t, s0_all, gw["wa"], gw["ba"], gw["ng"], jnp.asarray(dbd), jnp.asarray(mbd))


def _rope_tables(pos0, t):
    half = ROT_DIM // 2
    inv = ROPE_THETA ** (-jnp.arange(half, dtype=f32) / half)
    ang = (pos0 + jnp.arange(t)).astype(f32)[:, None] * inv[None, :]
    cos, sin = jnp.cos(ang), jnp.sin(ang)
    zeros = jnp.zeros((t, HEAD_DIM - ROT_DIM), f32)
    z8 = jnp.zeros((t, half), f32)
    c_tab = jnp.concatenate([cos, cos, jnp.ones((t, HEAD_DIM - ROT_DIM), f32)], axis=1)
    sa_tab = jnp.concatenate([-sin, z8, zeros], axis=1)
    sb_tab = jnp.concatenate([z8, sin, zeros], axis=1)
    return tuple(jnp.concatenate([x, x], axis=1) for x in (c_tab, sa_tab, sb_tab))


def _rope(x, c_tab, sa_tab, sb_tab):
    wd = x.shape[-1]
    ax = x.ndim - 1
    rep = wd // LANE
    half = ROT_DIM // 2
    if rep > 1:
        c_tab, sa_tab, sb_tab = (jnp.concatenate([tb] * rep, axis=-1) for tb in (c_tab, sa_tab, sb_tab))
    return x * c_tab + pltpu.roll(x, wd - half, ax) * sa_tab + pltpu.roll(x, half, ax) * sb_tab


def _swa_kernel(sink_ref, x_ref, wq_ref, wz_ref, wkv_ref, ct_ref, sat_ref, sbt_ref, kp_ref, vp_ref,
                y_ref, klast_ref, vlast_ref, kprev_ref, vprev_ref, sq_ref, sz_ref, sk_ref, sv_ref,
                *, nseq, qb, pos0, carry):
    blk = pl.program_id(1)
    _project(x_ref, ((wq_ref, (sq_ref,)), (wz_ref, (sz_ref,)), (wkv_ref, (sk_ref, sv_ref))))

    @pl.when(blk == 0)
    def _():
        kprev_ref[...] = kp_ref[...]
        vprev_ref[...] = vp_ref[...]

    tabs = (ct_ref[...], sat_ref[...], sbt_ref[...])
    mrows = SWA_GROUP * qb
    qi = lax.broadcasted_iota(jnp.int32, (mrows, WINDOW), 0) % qb
    kj = lax.broadcasted_iota(jnp.int32, (mrows, WINDOW), 1)
    past_ok = (kj >= qi) & (kj >= (WINDOW - pos0) - blk * qb)
    qi_c = lax.broadcasted_iota(jnp.int32, (mrows, qb), 0) % qb
    kj_c = lax.broadcasted_iota(jnp.int32, (mrows, qb), 1)
    cur_ok = kj_c <= qi_c
    neg = -jnp.inf

    for n in range(nseq):
        q = _rope(sq_ref[n], *tabs) * (HEAD_DIM ** -0.5)
        k = _rope(sk_ref[n], *tabs)
        v = sv_ref[n]
        kprev = kprev_ref[n]
        vprev = vprev_ref[n]
        klast_ref[n] = k if qb == WINDOW else jnp.concatenate([kprev[qb:], k], axis=0)
        vlast_ref[n] = v if qb == WINDOW else jnp.concatenate([vprev[qb:], v], axis=0)
        outs = []
        for kv in range(SWA_KV_HEADS):
            ds = slice(kv * HEAD_DIM, (kv + 1) * HEAD_DIM)
            qs = jnp.concatenate(
                [q[:, (kv * SWA_GROUP + g) * HEAD_DIM:(kv * SWA_GROUP + g + 1) * HEAD_DIM] for g in range(SWA_GROUP)],
                axis=0)
            sink = jnp.concatenate(
                [jnp.full((qb, 1), sink_ref[kv * SWA_GROUP + g], f32) for g in range(SWA_GROUP)], axis=0)
            s_p = jnp.where(past_ok, _dot_nt(qs, kprev[:, ds]), neg)
            s_c = jnp.where(cur_ok, _dot_nt(qs, k[:, ds]), neg)
            m = jnp.maximum(jnp.maximum(jnp.max(s_p, axis=1, keepdims=True), jnp.max(s_c, axis=1, keepdims=True)),
                            sink)
            p_p = jnp.exp(s_p - m)
            p_c = jnp.exp(s_c - m)
            den = jnp.sum(p_p, axis=1, keepdims=True) + jnp.sum(p_c, axis=1, keepdims=True) + jnp.exp(sink - m)
            o = (_dot(p_p, vprev[:, ds]) + _dot(p_c, v[:, ds])) / den
            outs.extend(o[g * qb:(g + 1) * qb, :] for g in range(SWA_GROUP))
        o_all = jnp.concatenate(outs, axis=1)
        y_ref[n] = (o_all * _silu(sz_ref[n])).astype(y_ref.dtype)
        if carry:
            kprev_ref[n] = k
            vprev_ref[n] = v


_SWA_SEGS = ("sq", "sz", "sk", "sv")


def _swa_call(xb, w_t, sinks, k_past, v_past, pos0, nb, nseq, nc, qb, name):
    assert nc == 1 or qb == WINDOW
    t_total = nc * qb
    c_tab, sa_tab, sb_tab = _rope_tables(pos0, t_total)
    kern = functools.partial(_swa_kernel, nseq=nseq, qb=qb, pos0=pos0, carry=nc > 1)
    kvw = SWA_KV_HEADS * HEAD_DIM
    tab_spec = pl.BlockSpec((qb, LANE), lambda b, c: (c, 0))
    past_spec = pl.BlockSpec((nseq, WINDOW, kvw), lambda b, c: (b, 0, 0))
    return pl.pallas_call(
        kern,
        grid=(nb, nc),
        in_specs=[pl.BlockSpec(memory_space=pltpu.SMEM),
                  _tok_spec(nseq, qb, D_MODEL), _w_spec("sq", "sq"), _w_spec("sz", "sz"), _w_spec("sk", "sv"),
                  tab_spec, tab_spec, tab_spec, past_spec, past_spec],
        out_specs=[_tok_spec(nseq, qb, BRANCH_W), past_spec, past_spec],
        out_shape=[jax.ShapeDtypeStruct((nb * nseq, nc * qb, BRANCH_W), bf16),
                   jax.ShapeDtypeStruct((nb * nseq, WINDOW, kvw), f32),
                   jax.ShapeDtypeStruct((nb * nseq, WINDOW, kvw), f32)],
        scratch_shapes=[pltpu.VMEM((nseq, WINDOW, kvw), f32), pltpu.VMEM((nseq, WINDOW, kvw), f32)]
        + [_seg_scratch(s, nseq, qb) for s in _SWA_SEGS],
        compiler_params=_params("parallel", "arbitrary"),
        name=name,
    )(sinks, xb, w_t, w_t, w_t, c_tab, sa_tab, sb_tab, k_past, v_past)


def _swa_step_kernel(sink_ref, x_ref, wq_ref, wz_ref, wkv_ref, ct_ref, sat_ref, sbt_ref, kp_ref, vp_ref,
                     y_ref, klast_ref, vlast_ref, sp_ref, sc_ref, o_ref, sq_ref, sz_ref, sk_ref, sv_ref,
                     *, nseq, t, pos0):
    hd = HEAD_DIM
    kvw = SWA_KV_HEADS * hd
    mrows = SWA_HEADS * t
    _project(x_ref, ((wq_ref, (sq_ref,)), (wz_ref, (sz_ref,)), (wkv_ref, (sk_ref, sv_ref))))
    tabs = tuple(r[...][None] for r in (ct_ref, sat_ref, sbt_ref))
    q3 = _rope(sq_ref[...], *tabs) * (hd ** -0.5)
    k3 = _rope(sk_ref[...], *tabs)
    v3 = sv_ref[...]

    lane = lax.broadcasted_iota(jnp.int32, (kvw, WINDOW), 1)
    pad = jnp.zeros((WINDOW - t, kvw), f32)

    def shifted(old_t, new):
        new_t = jnp.concatenate([pad, new], axis=0).T
        out = jnp.where(lane >= WINDOW - t, new_t, pltpu.roll(old_t, WINDOW - t, 1))
        return out.reshape(SWA_KV_HEADS, hd, WINDOW)

    for n in range(nseq):
        klast_ref[n] = shifted(kp_ref[n].reshape(kvw, WINDOW), k3[n])
        vlast_ref[n] = shifted(vp_ref[n].reshape(kvw, WINDOW), v3[n])

    zero = jnp.zeros((nseq, t, hd), f32)
    pieces = []
    for j in range(SWA_HEADS):
        qj = q3[:, :, j * hd:(j + 1) * hd]
        pieces.append(jnp.concatenate([qj, zero] if j // SWA_GROUP == 0 else [zero, qj], axis=2))
    qbd = jnp.concatenate(pieces, axis=1).astype(bf16)

    for n in range(nseq):
        sp_ref[n] = _dot(qbd[n], kp_ref[n].reshape(kvw, WINDOW))
        sc_ref[n] = _dot_nt(qbd[n], k3[n])

    qi = lax.broadcasted_iota(jnp.int32, (mrows, WINDOW), 0) % t
    kj = lax.broadcasted_iota(jnp.int32, (mrows, WINDOW), 1)
    past_ok = kj >= qi
    if pos0 < WINDOW:
        past_ok = past_ok & (kj >= WINDOW - pos0)
    qi_c = lax.broadcasted_iota(jnp.int32, (mrows, t), 0) % t
    kj_c = lax.broadcasted_iota(jnp.int32, (mrows, t), 1)
    cur_ok = kj_c <= qi_c
    sink = jnp.concatenate([jnp.full((t, 1), sink_ref[j], f32) for j in range(SWA_HEADS)], axis=0)[None]
    s_p = jnp.where(past_ok[None], sp_ref[...], -jnp.inf)
    s_c = jnp.where(cur_ok[None], sc_ref[...], -jnp.inf)
    m = jnp.maximum(jnp.maximum(jnp.max(s_p, axis=2, keepdims=True), jnp.max(s_c, axis=2, keepdims=True)), sink)
    p_p = jnp.exp(s_p - m)
    p_c = jnp.exp(s_c - m)
    den = jnp.sum(p_p, axis=2, keepdims=True) + jnp.sum(p_c, axis=2, keepdims=True) + jnp.exp(sink - m)
    p_p = p_p.astype(bf16)
    p_c = p_c.astype(bf16)
    for n in range(nseq):
        o_ref[n] = _dot_nt(p_p[n], vp_ref[n].reshape(kvw, WINDOW)) + _dot(p_c[n], v3[n])
    o = o_ref[...] / den
    outs = []
    for j in range(SWA_HEADS):
        kv = j // SWA_GROUP
        outs.append(o[:, j * t:(j + 1) * t, kv * hd:(kv + 1) * hd])
    y_ref[...] = (jnp.concatenate(outs, axis=2) * _silu(sz_ref[...])).astype(y_ref.dtype)


def _swa_step_call(xb, w_t, sinks, k_past_t, v_past_t, layer, pos0, nb, nseq, t, name):
    c_tab, sa_tab, sb_tab = _rope_tables(pos0, t)
    kern = functools.partial(_swa_step_kernel, nseq=nseq, t=t, pos0=pos0)
    kvw = SWA_KV_HEADS * HEAD_DIM
    mrows = SWA_HEADS * t
    tab_spec = pl.BlockSpec((t, LANE), lambda b, c: (0, 0))
    past_spec = pl.BlockSpec((None, nseq, SWA_KV_HEADS, HEAD_DIM, WINDOW), lambda b, c: (layer, b, 0, 0, 0))
    new_spec = pl.BlockSpec((nseq, SWA_KV_HEADS, HEAD_DIM, WINDOW), lambda b, c: (b, 0, 0, 0))
    new_shape = jax.ShapeDtypeStruct((nb * nseq, SWA_KV_HEADS, HEAD_DIM, WINDOW), f32)
    return pl.pallas_call(
        kern,
        grid=(nb, 1),
        in_specs=[pl.BlockSpec(memory_space=pltpu.SMEM),
                  _tok_spec(nseq, t, D_MODEL), _w_spec("sq", "sq"), _w_spec("sz", "sz"), _w_spec("sk", "sv"),
                  tab_spec, tab_spec, tab_spec, past_spec, past_spec],
        out_specs=[_tok_spec(nseq, t, BRANCH_W), new_spec, new_spec],
        out_shape=[jax.ShapeDtypeStruct((nb * nseq, t, BRANCH_W), bf16), new_shape, new_shape],
        scratch_shapes=[pltpu.VMEM((nseq, mrows, WINDOW), f32), pltpu.VMEM((nseq, mrows, t), f32),
                        pltpu.VMEM((nseq, mrows, kvw), f32)] + [_seg_scratch(s, nseq, t) for s in _SWA_SEGS],
        compiler_params=_params("parallel", "arbitrary"),
        name=name,
    )(sinks, xb, w_t, w_t, w_t, c_tab, sa_tab, sb_tab, k_past_t, v_past_t)


def _sgu_kernel(x_ref, wu_ref, wv_ref, wz_ref, g_ref, b_ref, wm_ref, bias_ref, y_ref, *rest, ntile, want_vn):
    vn_ref = rest[0] if want_vn else None
    su_ref, sv_ref, sz_ref = rest[-3:]
    _project(x_ref, ((wu_ref, (su_ref,)), (wv_ref, (sv_ref,)), (wz_ref, (sz_ref,))))
    vn = _layer_norm(sv_ref[...], g_ref[...], b_ref[...])
    if want_vn:
        vn_ref[...] = vn
    for r in range(ntile):
        rows = slice(r * SGU_CHUNK, (r + 1) * SGU_CHUNK)
        mixed = jnp.concatenate(
            [jnp.dot(wm_ref[g], vn[rows, g * SGU_GC:(g + 1) * SGU_GC].astype(bf16), preferred_element_type=f32)
             for g in range(SGU_GROUPS)], axis=1)
        y = su_ref[rows, :] * (mixed + bias_ref[...]) * _silu(sz_ref[rows, :])
        y_ref[rows, :] = y.astype(y_ref.dtype)


def _sgu_call(xb, w_t, ln_g, ln_b, wmix, bias, n_tok, ntile, want_vn, name):
    rows = ntile * SGU_CHUNK
    w = BRANCH_W
    kern = functools.partial(_sgu_kernel, ntile=ntile, want_vn=want_vn)

    def const(shape):
        return pl.BlockSpec(shape, lambda i: (0,) * len(shape))

    out_specs = [pl.BlockSpec((rows, w), lambda i: (i, 0))]
    out_shape = [jax.ShapeDtypeStruct((n_tok, w), bf16)]
    if want_vn:
        out_specs.append(pl.BlockSpec((rows, w), lambda i: (i, 0)))
        out_shape.append(jax.ShapeDtypeStruct((n_tok, w), f32))
    return pl.pallas_call(
        kern,
        grid=(n_tok // rows,),
        in_specs=[pl.BlockSpec((rows, D_MODEL), lambda i: (i, 0)), _w_spec("su", "su"), _w_spec("svv", "svv"),
                  _w_spec("suz", "suz"), const((1, w)), const((1, w)),
                  const((SGU_GROUPS, SGU_CHUNK, SGU_CHUNK)), const((SGU_CHUNK, w))],
        out_specs=out_specs,
        out_shape=out_shape,
        scratch_shapes=[_seg_scratch(s, rows) for s in ("su", "svv", "suz")],
        compiler_params=_params("parallel"),
        name=name,
    )(xb, w_t, w_t, w_t, ln_g, ln_b, wmix, bias)


def _mem_kernel(x_ref, wm_ref, mk_ref, mv_ref, y_ref, mq_ref, mz_ref, *, nseq, tq):
    _project(x_ref, ((wm_ref, (mq_ref, mz_ref)),))
    for n in range(nseq):
        q = mq_ref[n] * (HEAD_DIM ** -0.5)
        mk = mk_ref[n]
        mv = mv_ref[n]
        outs = []
        for h in range(MEM_HEADS):
            ds = slice(h * HEAD_DIM, (h + 1) * HEAD_DIM)
            s = _dot_nt(q[:, ds], mk[:, ds])
            m = jnp.max(s, axis=1, keepdims=True)
            p = jnp.exp(s - m)
            den = jnp.sum(p, axis=1, keepdims=True)
            outs.append(_dot(p, mv[:, ds]) / den)
        o = jnp.concatenate(outs, axis=1)
        y_ref[n] = (o * _silu(mz_ref[n])).astype(y_ref.dtype)


def _mem_call(xb, w_t, mk, mv, nb, nseq, nc, tq, name):
    kern = functools.partial(_mem_kernel, nseq=nseq, tq=tq)
    kv_spec = pl.BlockSpec((nseq, N_MEM, MEM_W), lambda b, c: (b, 0, 0))
    return pl.pallas_call(
        kern,
        grid=(nb, nc),
        in_specs=[_tok_spec(nseq, tq, D_MODEL), _w_spec("mq", "mz"), kv_spec, kv_spec],
        out_specs=_tok_spec(nseq, tq, MEM_W),
        out_shape=jax.ShapeDtypeStruct((nb * nseq, nc * tq, MEM_W), bf16),
        scratch_shapes=[_seg_scratch("mq", nseq, tq), _seg_scratch("mz", nseq, tq)],
        compiler_params=_params("parallel", "arbitrary"),
        name=name,
    )(xb, w_t, mk, mv)


def _mem_step_kernel(x_ref, wm_ref, mk_ref, mv_ref, y_ref, s_ref, o_ref, mq_ref, mz_ref, *, nseq, t):
    mrows = MEM_HEADS * t
    _project(x_ref, ((wm_ref, (mq_ref, mz_ref)),))
    row_head = lax.broadcasted_iota(jnp.int32, (mrows, MEM_W), 0) // t
    lane_head = lax.broadcasted_iota(jnp.int32, (mrows, MEM_W), 1) // HEAD_DIM
    own = (row_head == lane_head)[None]
    q3 = mq_ref[...] * (HEAD_DIM ** -0.5)
    qbd = jnp.where(own, jnp.concatenate([q3] * MEM_HEADS, axis=1), 0.0).astype(bf16)
    for n in range(nseq):
        s_ref[n] = _dot(qbd[n], mk_ref[n].reshape(MEM_W, N_MEM))
    s = s_ref[...]
    p = jnp.exp(s - jnp.max(s, axis=2, keepdims=True))
    den = jnp.sum(p, axis=2, keepdims=True)
    p = p.astype(bf16)
    for n in range(nseq):
        o_ref[n] = _dot_nt(p[n], mv_ref[n].reshape(MEM_W, N_MEM))
    o = jnp.where(own, o_ref[...] / den, 0.0)
    acc = o[:, 0:t, :]
    for h in range(1, MEM_HEADS):
        acc = acc + o[:, h * t:(h + 1) * t, :]
    y_ref[...] = (acc * _silu(mz_ref[...])).astype(y_ref.dtype)


def _mem_step_call(xb, w_t, mk_t, mv_t, layer, nb, nseq, t, name):
    kern = functools.partial(_mem_step_kernel, nseq=nseq, t=t)
    kv_spec = pl.BlockSpec((None, nseq, MEM_HEADS, HEAD_DIM, N_MEM), lambda b, c: (layer, b, 0, 0, 0))
    mrows = MEM_HEADS * t
    return pl.pallas_call(
        kern,
        grid=(nb, 1),
        in_specs=[_tok_spec(nseq, t, D_MODEL), _w_spec("mq", "mz"), kv_spec, kv_spec],
        out_specs=_tok_spec(nseq, t, MEM_W),
        out_shape=jax.ShapeDtypeStruct((nb * nseq, t, MEM_W), bf16),
        scratch_shapes=[pltpu.VMEM((nseq, mrows, N_MEM), f32), pltpu.VMEM((nseq, mrows, MEM_W), f32),
                        _seg_scratch("mq", nseq, t), _seg_scratch("mz", nseq, t)],
        compiler_params=_params("parallel", "arbitrary"),
        name=name,
    )(xb, w_t, mk_t, mv_t)


def _merge_kernel(yg_ref, yl_ref, ys_ref, yu_ref, ym_ref, x_ref, wg_ref, wb_ref, wm_ref, wo_ref, g_ref, b_ref,
                  o_ref, ob_ref):
    d = D_MODEL
    x = x_ref[...]
    xb = x.astype(bf16)

    def gate(n):
        return _sigmoid(_dot_nt(xb, wg_ref[n * d:(n + 1) * d, :]))

    merged = gate(4) * jnp.dot(ym_ref[...], wm_ref[...], preferred_element_type=f32)
    for n, y_ref in enumerate((yg_ref, yl_ref, ys_ref, yu_ref)):
        merged = merged + gate(n) * jnp.dot(y_ref[...], wb_ref[n], preferred_element_type=f32)
    out = _dot(merged, wo_ref[...])
    y = _layer_norm(DN_ALPHA * x + out, g_ref[...], b_ref[...])
    o_ref[...] = y
    ob_ref[...] = y.astype(bf16)


def _merge_call(ys, x, mw, tm, name):
    n_tok, d = x.shape
    w = BRANCH_W

    def rows(width):
        return pl.BlockSpec((tm, width), lambda i: (i, 0))

    def const(shape):
        return pl.BlockSpec(shape, lambda i: (0,) * len(shape), pipeline_mode=pl.Buffered(1))

    return pl.pallas_call(
        _merge_kernel,
        grid=(n_tok // tm,),
        in_specs=[rows(w), rows(w), rows(w), rows(w), rows(MEM_W), rows(d),
                  const((5 * d, d)), const((4, w, d)), const((MEM_W, d)), const((d, d)), const((1, d)),
                  const((1, d))],
        out_specs=[rows(d), rows(d)],
        out_shape=[jax.ShapeDtypeStruct((n_tok, d), f32), jax.ShapeDtypeStruct((n_tok, d), bf16)],
        compiler_params=_params("parallel"),
        name=name,
    )(*ys, x, mw["wg"], mw["wb"], mw["wm"], mw["wo"], mw["g"], mw["b"])


def _prep_layer(l, w_in, gla_wa2, gla_ba, gla_norm_g, lru_conv_w, lru_conv_b, lru_wr, lru_br, lru_wi, lru_bi, lru_L,
                swa_sinks, sgu_ln_g, sgu_ln_b, sgu_w, sgu_b, w_mem_kv, w_branch, w_branch_mem, w_out, ln_g, ln_b):
    d = D_MODEL
    w = BRANCH_W
    w_t = jnp.swapaxes(w_in, 1, 2)[l]
    runs = []
    used = 0
    for a, b in _PROJ_RUNS:
        lo = _ORIG_OFF[a][0]
        hi = _ORIG_OFF[b][0] + _ORIG_OFF[b][1]
        runs.append(w_t[lo:hi].astype(bf16))
        used += hi - lo
    runs.append(jnp.zeros((N_PROJ - used, d), bf16))
    w_proj = jnp.concatenate(runs, axis=0)
    g_lo = _ORIG_OFF["gates"][0]
    w_gates = w_t[g_lo:].astype(bf16)

    def block_diag(wb):
        eye = jnp.eye(LRU_BLOCKS, dtype=f32)
        return (eye[:, None, :, None] * wb[:, :, None, :]).reshape(w, w).astype(bf16)

    tril = jnp.tril(jnp.ones((SGU_CHUNK, SGU_CHUNK), f32))
    wmix_p = (sgu_w[l] * tril).astype(bf16)
    bias_p = jnp.repeat(sgu_b[l].T, SGU_GC, axis=1)
    t8 = SUBLANE
    rep = SGU_CHUNK // t8
    w8 = (sgu_w[l] * tril)[:, :t8, :t8]
    seq_eye = jnp.eye(rep, dtype=f32)
    wmix_s = (seq_eye[None, :, None, :, None] * w8[:, None, :, None, :]).reshape(
        SGU_GROUPS, SGU_CHUNK, SGU_CHUNK).astype(bf16)
    bias_s = jnp.tile(bias_p[:t8], (rep, 1))
    return dict(
        w_proj=w_proj,
        w_mem_kv=w_mem_kv[l].astype(bf16),
        gla=dict(wa=jnp.pad(gla_wa2[l], ((0, LANE - GLA_RANK), (0, 0))).astype(bf16),
                 ba=gla_ba[l].reshape(1, -1), ng=gla_norm_g[l].reshape(1, -1)),
        lru=dict(conv_w=lru_conv_w[l], conv_b=lru_conv_b[l].reshape(1, w), wr=block_diag(lru_wr[l]),
                 br=lru_br[l].reshape(1, w), wi=block_diag(lru_wi[l]), bi=lru_bi[l].reshape(1, w),
                 lam=lru_L[l].reshape(1, w)),
        sinks=swa_sinks[l],
        sgu=dict(g=sgu_ln_g[l].reshape(1, w), b=sgu_ln_b[l].reshape(1, w), wmix_p=wmix_p, bias_p=bias_p,
                 wmix_s=wmix_s, bias_s=bias_s),
        merge=dict(wg=w_gates, wb=w_branch[l].astype(bf16), wm=w_branch_mem[l].astype(bf16), wo=w_out[l].astype(bf16),
                   g=ln_g[l].reshape(1, d), b=ln_b[l].reshape(1, d)),
    )


def _layer(x, lw, grp, st, layer, tag):
    nseq_total, t = grp["batch"], grp["seq"]
    n_tok = nseq_total * t
    x, xb = x
    xb3 = xb.reshape(nseq_total, t, D_MODEL)
    w_t = lw["w_proj"]

    lt = grp["lru"]
    y_lru, hlast, hist = _lru_call(xb3, w_t, st["hist0"], st["h0"], lw["lru"], nseq_total // lt[0], lt[0],
                                   t // lt[1], lt[1], "lru_" + tag)
    short = grp["kind"] == "s"
    gt = grp["gla"]
    if short:
        y_gla, s_out = _gla_step_call(xb3, w_t, st["gla0"], layer, lw["gla"], nseq_total // gt[0], gt[0], t,
                                      "gla_" + tag)
    else:
        y_gla, s_out = _gla_call(xb3, w_t, st["gla0"], lw["gla"], nseq_total // gt[0], gt[0], t // gt[1], gt[1],
                                 "gla_" + tag)
    wt = grp["swa"]
    if short:
        y_swa, k_last, v_last = _swa_step_call(xb3, w_t, lw["sinks"], st["k_past"], st["v_past"], layer,
                                               grp["pos0"], nseq_total // wt[0], wt[0], t, "swa_" + tag)
    else:
        y_swa, k_last, v_last = _swa_call(xb3, w_t, lw["sinks"], st["k_past"], st["v_past"], grp["pos0"],
                                          nseq_total // wt[0], wt[0], t // wt[1], wt[1], "swa_" + tag)
    sg = lw["sgu"]
    sgu_out = _sgu_call(xb, w_t, sg["g"], sg["b"], sg["wmix_" + grp["kind"]], sg["bias_" + grp["kind"]], n_tok,
                        grp["sgu_tiles"], grp["kind"] == "s", "sgu_" + tag)
    mt = grp["mem"]
    if short:
        y_mem = _mem_step_call(xb3, w_t, st["mk"], st["mv"], layer, nseq_total // mt[0], mt[0], t, "mem_" + tag)
    else:
        y_mem = _mem_call(xb3, w_t, st["mk"], st["mv"], nseq_total // mt[0], mt[0], t // mt[1], mt[1],
                          "mem_" + tag)
    ys = tuple(y.reshape(n_tok, y.shape[-1]) for y in (y_gla, y_lru, y_swa, sgu_out[0], y_mem))
    x_new = _merge_call(ys, x, lw["merge"], min(256, n_tok), "merge_" + tag)
    return x_new, dict(gla=s_out, hlast=hlast, hist=hist, k_last=k_last, v_last=v_last,
                       vn=sgu_out[1] if len(sgu_out) > 1 else None)


_PROMPT = dict(kind="p", pos0=0, lru=(1, 256), gla=(2, 128), swa=(1, 128), sgu_tiles=4, mem=(1, 512))
_SAMPLE = dict(kind="s", pos0=PAST_LEN, lru=(32, 8), gla=(16, 8), swa=(16, 8), sgu_tiles=8, mem=(16, 8))


def kernel(x_prompt, x_sample, mem_prompt, state_gla, state_lru_h, state_lru_conv, cache_swa_k, cache_swa_v,
           cache_mem_k, cache_mem_v, ln_in_g, ln_in_b, w_in, gla_wa2, gla_ba, gla_norm_g, lru_conv_w, lru_conv_b,
           lru_wr, lru_br, lru_wi, lru_bi, lru_L, swa_sinks, sgu_ln_g, sgu_ln_b, sgu_w, sgu_b, w_mem_kv, w_branch,
           w_branch_mem, w_out, ln_g, ln_b):
    bp, tp, d = x_prompt.shape
    bs, ts, _ = x_sample.shape
    w = BRANCH_W
    kvw = SWA_KV_HEADS * HEAD_DIM
    gp = dict(_PROMPT, batch=bp, seq=tp)
    gs = dict(_SAMPLE, batch=bs, seq=ts)

    xp = _ln_call(x_prompt.reshape(bp * tp, d), ln_in_g, ln_in_b)
    xs = _ln_call(x_sample.reshape(bs * ts, d), ln_in_g, ln_in_b)
    mem2 = mem_prompt.reshape(bp * N_MEM, d)

    swa_k_t, swa_v_t, mem_k_t, mem_v_t = (jnp.transpose(c, (0, 1, 3, 4, 2))
                                          for c in (cache_swa_k, cache_swa_v, cache_mem_k, cache_mem_v))

    outs_p, outs_s, mks, mvs = [], [], [], []
    for l in range(DEPTH):
        lw = _prep_layer(l, w_in, gla_wa2, gla_ba, gla_norm_g, lru_conv_w, lru_conv_b, lru_wr, lru_br, lru_wi,
                         lru_bi, lru_L, swa_sinks, sgu_ln_g, sgu_ln_b, sgu_w, sgu_b, w_mem_kv, w_branch,
                         w_branch_mem, w_out, ln_g, ln_b)
        mkv = _matmul_call(mem2, lw["w_mem_kv"], bp * N_MEM, 2 * MEM_W, "memkv_%d" % l)
        mk = mkv[:, :MEM_W].reshape(bp, N_MEM, MEM_W)
        mv = mkv[:, MEM_W:].reshape(bp, N_MEM, MEM_W)
        st_p = dict(hist0=jnp.zeros((bp, SUBLANE, w), f32), h0=jnp.zeros((bp, 1, w), f32),
                    gla0=jnp.zeros((bp, GLA_HEADS, GLA_DK, GLA_DV), f32),
                    k_past=jnp.zeros((bp, WINDOW, kvw), f32), v_past=jnp.zeros((bp, WINDOW, kvw), f32),
                    mk=mk, mv=mv)
        st_s = dict(hist0=jnp.pad(state_lru_conv[l], ((0, 0), (SUBLANE - (CONV_W - 1), 0), (0, 0))),
                    h0=state_lru_h[l][:, None, :], gla0=state_gla,
                    k_past=swa_k_t, v_past=swa_v_t, mk=mem_k_t, mv=mem_v_t)
        xp, op = _layer(xp, lw, gp, st_p, l, "p%d" % l)
        xs, os_ = _layer(xs, lw, gs, st_s, l, "s%d" % l)
        outs_p.append(op)
        outs_s.append(os_)
        mks.append(mk.reshape(bp, N_MEM, MEM_HEADS, HEAD_DIM))
        mvs.append(mv.reshape(bp, N_MEM, MEM_HEADS, HEAD_DIM))

    def stack(outs, fn):
        return jnp.stack([fn(o) for o in outs])

    def window(a):
        return a.reshape(a.shape[0], WINDOW, SWA_KV_HEADS, HEAD_DIM)

    def window_t(a):
        return jnp.transpose(a, (0, 3, 1, 2))

    return (
        xp[0].reshape(bp, tp, d), xs[0].reshape(bs, ts, d),
        stack(outs_p, lambda o: o["gla"]), stack(outs_s, lambda o: o["gla"]),
        stack(outs_p, lambda o: o["hlast"][:, SUBLANE - 1]), stack(outs_s, lambda o: o["hlast"][:, SUBLANE - 1]),
        stack(outs_p, lambda o: o["hist"][:, SUBLANE - (CONV_W - 1):]),
        stack(outs_s, lambda o: o["hist"][:, SUBLANE - (CONV_W - 1):]),
        stack(outs_p, lambda o: window(o["k_last"])), stack(outs_s, lambda o: window_t(o["k_last"])),
        stack(outs_p, lambda o: window(o["v_last"])), stack(outs_s, lambda o: window_t(o["v_last"])),
        jnp.stack(mks), jnp.stack(mvs),
        stack(outs_s, lambda o: o["vn"].reshape(bs, ts, w)),
    )
```

```python
import functools
import math

import jax
import jax.numpy as jnp
import numpy as np
from jax import lax
from jax.experimental import pallas as pl
from jax.experimental.pallas import tpu as pltpu

f32 = jnp.float32
bf16 = jnp.bfloat16

D_MODEL = 1024
DEPTH = 2
PAST_LEN = 8192
BRANCH_W = 512
GLA_HEADS = 4
GLA_DK = 64
GLA_DV = 128
GLA_RANK = 16
GLA_TAU = 16.0
LRU_BLOCKS = 8
LRU_BS = 64
CONV_W = 4
LRU_C = 8.0
HEAD_DIM = 64
SWA_HEADS = 8
SWA_KV_HEADS = 2
SWA_GROUP = 4
WINDOW = 128
ROT_DIM = 16
ROPE_THETA = 500000.0
SGU_GROUPS = 4
SGU_GC = 128
SGU_CHUNK = 128
N_MEM = 256
MEM_HEADS = 4
MEM_W = 256
LN_EPS = 1e-5
RMS_EPS = 1e-6
DN_ALPHA = (2 * DEPTH) ** 0.25

LANE = 128
SUBLANE = 8

_ORIG = (("gq", 256), ("gk", 256), ("gv", 512), ("glr", 16), ("gz", 512), ("lx", 512), ("lz", 512), ("sq", 512),
         ("sk", 128), ("sv", 128), ("sz", 512), ("su", 512), ("svv", 512), ("suz", 512), ("mq", 256), ("mz", 256),
         ("gates", 5 * D_MODEL))
_ORIG_OFF = {}
_off = 0
for _n, _w in _ORIG:
    _ORIG_OFF[_n] = (_off, _w)
    _off += _w
_PROJ_RUNS = (("gq", "gv"), ("gz", "sq"), ("sz", "sz"), ("sk", "sv"), ("glr", "glr"), None, ("su", "mz"))
_SEG = {}
_off = 0
for _run in _PROJ_RUNS:
    if _run is None:
        N_PROJ = -(-_off // 1024) * 1024
        _off = N_PROJ
        continue
    _names = [n for n, _ in _ORIG]
    for _n in _names[_names.index(_run[0]):_names.index(_run[1]) + 1]:
        _w = max(_ORIG_OFF[_n][1], 128)
        assert _off % _w == 0
        _SEG[_n] = (_off, _w)
        _off += _w
N_WROWS = _off


def _dot(a, b):
    return jnp.dot(a.astype(bf16), b.astype(bf16), preferred_element_type=f32)


def _dot_nt(a, b):
    return lax.dot_general(a.astype(bf16), b.astype(bf16), (((1,), (1,)), ((), ())), preferred_element_type=f32)


def _dot_tn(a, b):
    return lax.dot_general(a.astype(bf16), b.astype(bf16), (((0,), (0,)), ((), ())), preferred_element_type=f32)


def _sigmoid(x):
    return 1.0 / (1.0 + jnp.exp(-x))


def _silu(x):
    return x * _sigmoid(x)


def _log_sigmoid(x):
    return jnp.minimum(x, 0.0) - jnp.log(1.0 + jnp.exp(-jnp.abs(x)))


def _layer_norm(x, g, b):
    mu = jnp.mean(x, axis=-1, keepdims=True)
    xc = x - mu
    var = jnp.mean(xc * xc, axis=-1, keepdims=True)
    return xc * lax.rsqrt(var + LN_EPS) * g + b


def _params(*sem):
    return pltpu.CompilerParams(dimension_semantics=sem)


def _ln_kernel(x_ref, g_ref, b_ref, o_ref, ob_ref):
    y = _layer_norm(x_ref[...], g_ref[...], b_ref[...])
    o_ref[...] = y
    ob_ref[...] = y.astype(bf16)


def _ln_call(x, g, b, tm=512):
    n, d = x.shape
    tm = min(tm, n)
    return pl.pallas_call(
        _ln_kernel,
        grid=(n // tm,),
        in_specs=[pl.BlockSpec((tm, d), lambda i: (i, 0)), pl.BlockSpec((1, d), lambda i: (0, 0)),
                  pl.BlockSpec((1, d), lambda i: (0, 0))],
        out_specs=[pl.BlockSpec((tm, d), lambda i: (i, 0)), pl.BlockSpec((tm, d), lambda i: (i, 0))],
        out_shape=[jax.ShapeDtypeStruct((n, d), f32), jax.ShapeDtypeStruct((n, d), bf16)],
        compiler_params=_params("parallel"),
        name="ln_in",
    )(x, g.reshape(1, d), b.reshape(1, d))


def _matmul_kernel(x_ref, w_ref, o_ref, *, w_transposed):
    o_ref[...] = (_dot_nt if w_transposed else _dot)(x_ref[...], w_ref[...])


def _matmul_call(x, w, tm, tn, name, w_transposed=False, n_out=None):
    m, k = x.shape
    n = n_out or (w.shape[0] if w_transposed else w.shape[1])
    w_spec = pl.BlockSpec((tn, k), lambda i, j: (j, 0)) if w_transposed else pl.BlockSpec((k, tn), lambda i, j: (0, j))
    return pl.pallas_call(
        functools.partial(_matmul_kernel, w_transposed=w_transposed),
        grid=(m // tm, n // tn),
        in_specs=[pl.BlockSpec((tm, k), lambda i, j: (i, 0)), w_spec],
        out_specs=pl.BlockSpec((tm, tn), lambda i, j: (i, j)),
        out_shape=jax.ShapeDtypeStruct((m, n), f32),
        compiler_params=_params("parallel", "arbitrary"),
        name=name,
    )(x, w)


def _seg_spec(name, nseq, rows):
    off, width = _SEG[name]
    assert off + width <= N_PROJ
    cb = off // width
    return pl.BlockSpec((nseq, rows, width), lambda b, c: (b, c, cb))


def _tok_spec(nseq, rows, width):
    return pl.BlockSpec((nseq, rows, width), lambda b, c: (b, c, 0))


def _w_spec(first, last):
    off = _SEG[first][0]
    rows = _SEG[last][0] + _SEG[last][1] - off
    assert off % rows == 0
    return pl.BlockSpec((rows, D_MODEL), lambda *_: (off // rows, 0))


def _seg_scratch(name, *lead):
    return pltpu.VMEM((*lead, _SEG[name][1]), f32)


def _project(x_ref, pairs):
    x2 = x_ref[...].reshape(-1, D_MODEL)
    for w_ref, seg_refs in pairs:
        p = lax.dot_general(x2, w_ref[...], (((1,), (1,)), ((), ())), preferred_element_type=f32)
        off = 0
        for s_ref in seg_refs:
            width = s_ref.shape[-1]
            s_ref[...] = p[:, off:off + width].reshape(s_ref.shape)
            off += width


def _const_spec(shape):
    nd = len(shape)
    return pl.BlockSpec(shape, lambda b, c: (0,) * nd)


def _lru_kernel(lx_ref, lz_ref, hist0_ref, h0_ref, cw_ref, cb_ref, wr_ref, br_ref, wi_ref, bi_ref, lam_ref,
                y_ref, hlast_ref, hist_out_ref, hist_ref, hc_ref, *, nseq, tc):
    c = pl.program_id(1)
    w = BRANCH_W

    @pl.when(c == 0)
    def _():
        hist_ref[...] = hist0_ref[...]
        hc_ref[...] = h0_ref[...]

    x = lx_ref[...]
    xfull = jnp.concatenate([hist_ref[...], x], axis=1)

    def tap(j):
        return cw_ref[j:j + 1, :].reshape(1, 1, w)

    y = cb_ref[...].reshape(1, 1, w) + x * tap(CONV_W - 1)
    for s in range(1, CONV_W):
        y = y + pltpu.roll(xfull, s, 1)[:, SUBLANE:, :] * tap(CONV_W - 1 - s)
    hist_ref[...] = xfull[:, tc:, :]
    hist_out_ref[...] = xfull[:, tc:, :]

    xc = y.reshape(nseq * tc, w)
    r = _sigmoid(_dot(xc, wr_ref[...]) + br_ref[...])
    i = _sigmoid(_dot(xc, wi_ref[...]) + bi_ref[...])
    log_a = (LRU_C * r) * _log_sigmoid(lam_ref[...])
    a = jnp.exp(log_a)
    u = jnp.sqrt(jnp.tanh(-log_a) * (a * a + 1.0)) * (i * xc)

    acc_a = a.reshape(nseq, tc, w)
    acc_u = u.reshape(nseq, tc, w)
    t = lax.broadcasted_iota(jnp.int32, (nseq, tc, w), 1)
    d = 1
    while d < tc:
        ok = t >= d
        a_sh = jnp.where(ok, pltpu.roll(acc_a, d, 1), 1.0)
        u_sh = jnp.where(ok, pltpu.roll(acc_u, d, 1), 0.0)
        acc_u = acc_a * u_sh + acc_u
        acc_a = acc_a * a_sh
        d *= 2
    h = acc_a * hc_ref[...] + acc_u
    hc_ref[...] = h[:, tc - 1:tc, :]
    hlast_ref[...] = h[:, tc - SUBLANE:, :]
    y_ref[...] = (h * _silu(lz_ref[...])).astype(y_ref.dtype)


def _lru_call(proj, hist0, h0, lw, nb, nseq, nc, tc, name):
    w = BRANCH_W
    kern = functools.partial(_lru_kernel, nseq=nseq, tc=tc)
    return pl.pallas_call(
        kern,
        grid=(nb, nc),
        in_specs=[_seg_spec("lx", nseq, tc), _seg_spec("lz", nseq, tc),
                  pl.BlockSpec((nseq, SUBLANE, w), lambda b, c: (b, 0, 0)),
                  pl.BlockSpec((nseq, 1, w), lambda b, c: (b, 0, 0)),
                  _const_spec((CONV_W, w)), _const_spec((1, w)), _const_spec((w, w)), _const_spec((1, w)),
                  _const_spec((w, w)), _const_spec((1, w)), _const_spec((1, w))],
        out_specs=[_tok_spec(nseq, tc, w),
                   pl.BlockSpec((nseq, SUBLANE, w), lambda b, c: (b, 0, 0)),
                   pl.BlockSpec((nseq, SUBLANE, w), lambda b, c: (b, 0, 0))],
        out_shape=[jax.ShapeDtypeStruct((nb * nseq, nc * tc, w), bf16),
                   jax.ShapeDtypeStruct((nb * nseq, SUBLANE, w), f32),
                   jax.ShapeDtypeStruct((nb * nseq, SUBLANE, w), f32)],
        scratch_shapes=[pltpu.VMEM((nseq, SUBLANE, w), f32), pltpu.VMEM((nseq, 1, w), f32)],
        compiler_params=_params("parallel", "arbitrary"),
        name=name,
    )(proj, proj, hist0, h0, lw["conv_w"], lw["conv_b"], lw["wr"], lw["br"], lw["wi"], lw["bi"], lw["lam"])


def _gla_consts(c):
    t = np.arange(c)[:, None]
    u = np.arange(c)[None, :]
    blocks = [u <= t, u > t]
    masks = [t == u]
    m = 1
    while m < c:
        t0 = (t // m) * m
        odd = (t // m) % 2 == 1
        blocks.append(odd & (u >= t0) & (u <= t))
        blocks.append((~odd) & (u > t) & (u <= t0 + m - 1))
        masks.append((t // (2 * m) == u // (2 * m)) & odd & ((u // m) % 2 == 0))
        m *= 2
    return (np.concatenate(blocks, 0).astype(np.float32), np.stack(masks).astype(np.float32))


def _gla_kernel(gq_ref, gk_ref, gv_ref, gz_ref, glr_ref, s0_ref, wa_ref, ba_ref, ng_ref, d_ref, m_ref,
                y_ref, sout_ref, s_ref, *, nseq, c):
    ci = pl.program_id(1)
    nlev = int(math.log2(c))
    hk = GLA_HEADS * GLA_DK

    @pl.when(ci == 0)
    def _():
        s_ref[...] = s0_ref[...]

    eye_r = lax.broadcasted_iota(jnp.int32, (GLA_DK, GLA_DK), 0)
    eye_c = lax.broadcasted_iota(jnp.int32, (GLA_DK, GLA_DK), 1)
    eye = eye_r == eye_c

    for n in range(nseq):
        q = gq_ref[n] * (GLA_DK ** -0.5)
        k = gk_ref[n]
        v = gv_ref[n]
        z = _dot(glr_ref[n], wa_ref[...]) + ba_ref[...]
        la = _log_sigmoid(z) * (1.0 / GLA_TAU)
        hi = la.astype(bf16)
        r1 = la - hi.astype(f32)
        mid = r1.astype(bf16)
        lo = (r1 - mid.astype(f32)).astype(bf16)
        hml = jnp.concatenate([hi, mid, lo], axis=1)

        def sums(blk):
            p = jnp.dot(d_ref[blk * c:(blk + 1) * c, :].astype(bf16), hml, preferred_element_type=f32)
            return p[:, :hk] + p[:, hk:2 * hk] + p[:, 2 * hk:]

        b = sums(0)
        q_in = q * jnp.exp(b)
        k_st = k * jnp.exp(sums(1))
        dec_row = jnp.exp(b[c - 1:c, :])
        qf = [q]
        kf = [k]
        for lev in range(nlev):
            qf.append(q * jnp.exp(sums(2 + 2 * lev)))
            kf.append(k * jnp.exp(sums(3 + 2 * lev)))

        for h in range(GLA_HEADS):
            ks = slice(h * GLA_DK, (h + 1) * GLA_DK)
            vs = slice(h * GLA_DV, (h + 1) * GLA_DV)
            att = jnp.zeros((c, c), f32)
            for lev in range(nlev + 1):
                att = att + m_ref[lev] * _dot_nt(qf[lev][:, ks], kf[lev][:, ks])
            s_h = s_ref[n, h]
            v_h = v[:, vs]
            o = _dot(q_in[:, ks], s_h) + _dot(att, v_h)
            dec_col = jnp.sum(jnp.where(eye, jnp.broadcast_to(dec_row[:, ks], (GLA_DK, GLA_DK)), 0.0),
                              axis=1, keepdims=True)
            s_ref[n, h] = s_h * dec_col + _dot_tn(k_st[:, ks], v_h)
            o = o * lax.rsqrt(jnp.mean(o * o, axis=-1, keepdims=True) + RMS_EPS) * ng_ref[...]
            y_ref[n, :, vs] = (o * _silu(gz_ref[n, :, vs])).astype(y_ref.dtype)

    sout_ref[...] = s_ref[...]


_GLA_SEGS = ("gq", "gk", "gv", "gz", "glr")


def _gla_call(proj, s0, gw, nb, nseq, nc, c, name):
    dstack, masks = _gla_consts(c)
    kern = functools.partial(_gla_kernel, nseq=nseq, c=c)
    hk = GLA_HEADS * GLA_DK
    st_spec = pl.BlockSpec((nseq, GLA_HEADS, GLA_DK, GLA_DV), lambda b, ci: (b, 0, 0, 0))
    return pl.pallas_call(
        kern,
        grid=(nb, nc),
        in_specs=[_seg_spec(s, nseq, c) for s in _GLA_SEGS] + [
            st_spec, _const_spec((LANE, hk)), _const_spec((1, hk)), _const_spec((1, GLA_DV)),
            _const_spec(dstack.shape), _const_spec(masks.shape)],
        out_specs=[_tok_spec(nseq, c, BRANCH_W), st_spec],
        out_shape=[jax.ShapeDtypeStruct((nb * nseq, nc * c, BRANCH_W), bf16),
                   jax.ShapeDtypeStruct((nb * nseq, GLA_HEADS, GLA_DK, GLA_DV), f32)],
        scratch_shapes=[pltpu.VMEM((nseq, GLA_HEADS, GLA_DK, GLA_DV), f32)],
        compiler_params=_params("parallel", "arbitrary"),
        name=name,
    )(*([proj] * len(_GLA_SEGS)), s0, gw["wa"], gw["ba"], gw["ng"], jnp.asarray(dstack), jnp.asarray(masks))


def _gla_step_consts(t, nseq):
    dstack, masks = _gla_consts(t)
    eye = np.eye(nseq, dtype=np.float32)
    dbd = np.stack([np.kron(eye, dstack[i * t:(i + 1) * t]) for i in range(dstack.shape[0] // t)])
    mbd = np.stack([np.kron(eye, m) for m in masks])
    return dbd, mbd


def _gla_step_kernel(gq_ref, gk_ref, gv_ref, gz_ref, glr_ref, s0_ref, wa_ref, ba_ref, ng_ref, d_ref, m_ref,
                     y_ref, sout_ref, *, nseq, t):
    r = nseq * t
    nlev = int(math.log2(t))
    hk = GLA_HEADS * GLA_DK
    q = gq_ref[...].reshape(r, hk) * (GLA_DK ** -0.5)
    k = gk_ref[...].reshape(r, hk)
    v = gv_ref[...].reshape(r, BRANCH_W)
    gz = gz_ref[...].reshape(r, BRANCH_W)
    z = _dot(glr_ref[...].reshape(r, LANE), wa_ref[...]) + ba_ref[...]
    la = _log_sigmoid(z) * (1.0 / GLA_TAU)
    hi = la.astype(bf16)
    r1 = la - hi.astype(f32)
    mid = r1.astype(bf16)
    lo = (r1 - mid.astype(f32)).astype(bf16)
    hml = jnp.concatenate([hi, mid, lo], axis=1)

    def sums(blk):
        p = jnp.dot(d_ref[blk].astype(bf16), hml, preferred_element_type=f32)
        return p[:, :hk] + p[:, hk:2 * hk] + p[:, 2 * hk:]

    b = sums(0)
    q_in = q * jnp.exp(b)
    k_st = k * jnp.exp(sums(1))
    dec3 = jnp.exp(b.reshape(nseq, t, hk)[:, t - 1:t, :])
    qf = [q]
    kf = [k]
    for lev in range(nlev):
        qf.append(q * jnp.exp(sums(2 + 2 * lev)))
        kf.append(k * jnp.exp(sums(3 + 2 * lev)))

    own = (lax.broadcasted_iota(jnp.int32, (r, nseq * GLA_DK), 0) // t
           == lax.broadcasted_iota(jnp.int32, (r, nseq * GLA_DK), 1) // GLA_DK)
    eye = (lax.broadcasted_iota(jnp.int32, (GLA_DK, GLA_DK), 0)
           == lax.broadcasted_iota(jnp.int32, (GLA_DK, GLA_DK), 1))[None]

    def spread(x):
        x2 = jnp.concatenate([x, x], axis=1)
        return jnp.where(own, jnp.concatenate([x2] * (nseq // 2), axis=1), 0.0)

    ys = []
    for h in range(GLA_HEADS):
        ks = slice(h * GLA_DK, (h + 1) * GLA_DK)
        vs = slice(h * GLA_DV, (h + 1) * GLA_DV)
        att = jnp.zeros((r, r), f32)
        for lev in range(nlev + 1):
            att = att + m_ref[lev] * _dot_nt(qf[lev][:, ks], kf[lev][:, ks])
        s_h = s0_ref[:, h]
        v_h = v[:, vs]
        o = _dot(spread(q_in[:, ks]), s_h.reshape(nseq * GLA_DK, GLA_DV)) + _dot(att, v_h)
        upd = _dot_tn(spread(k_st[:, ks]), v_h)
        dec_col = jnp.sum(jnp.where(eye, jnp.broadcast_to(dec3[:, :, ks], (nseq, GLA_DK, GLA_DK)), 0.0),
                          axis=2, keepdims=True)
        sout_ref[:, h] = s_h * dec_col + upd.reshape(nseq, GLA_DK, GLA_DV)
        o = o * lax.rsqrt(jnp.mean(o * o, axis=-1, keepdims=True) + RMS_EPS) * ng_ref[...]
        ys.append(o * _silu(gz[:, vs]))
    y_ref[...] = jnp.concatenate(ys, axis=1).reshape(nseq, t, BRANCH_W).astype(y_ref.dtype)


def _gla_step_call(proj, s0_all, layer, gw, nb, nseq, t, name):
    dbd, mbd = _gla_step_consts(t, nseq)
    kern = functools.partial(_gla_step_kernel, nseq=nseq, t=t)
    hk = GLA_HEADS * GLA_DK
    st_spec = pl.BlockSpec((nseq, GLA_HEADS, GLA_DK, GLA_DV), lambda b, ci: (b, 0, 0, 0))
    s0_spec = pl.BlockSpec((None, nseq, GLA_HEADS, GLA_DK, GLA_DV), lambda b, ci: (layer, b, 0, 0, 0))
    return pl.pallas_call(
        kern,
        grid=(nb, 1),
        in_specs=[_seg_spec(s, nseq, t) for s in _GLA_SEGS] + [
            s0_spec, _const_spec((LANE, hk)), _const_spec((1, hk)), _const_spec((1, GLA_DV)),
            _const_spec(dbd.shape), _const_spec(mbd.shape)],
        out_specs=[_tok_spec(nseq, t, BRANCH_W), st_spec],
        out_shape=[jax.ShapeDtypeStruct((nb * nseq, t, BRANCH_W), bf16),
                   jax.ShapeDtypeStruct((nb * nseq, GLA_HEADS, GLA_DK, GLA_DV), f32)],
        compiler_params=_params("parallel", "arbitrary"),
        name=name,
    )(*([proj] * len(_GLA_SEGS)), s0_all, gw["wa"], gw["ba"], gw["ng"], jnp.asarray(dbd), jnp.asarray(mbd))


def _rope_tables(pos0, t):
    half = ROT_DIM // 2
    inv = ROPE_THETA ** (-jnp.arange(half, dtype=f32) / half)
    ang = (pos0 + jnp.arange(t)).astype(f32)[:, None] * inv[None, :]
    cos, sin = jnp.cos(ang), jnp.sin(ang)
    zeros = jnp.zeros((t, HEAD_DIM - ROT_DIM), f32)
    z8 = jnp.zeros((t, half), f32)
    c_tab = jnp.concatenate([cos, cos, jnp.ones((t, HEAD_DIM - ROT_DIM), f32)], axis=1)
    sa_tab = jnp.concatenate([-sin, z8, zeros], axis=1)
    sb_tab = jnp.concatenate([z8, sin, zeros], axis=1)
    return tuple(jnp.concatenate([x, x], axis=1) for x in (c_tab, sa_tab, sb_tab))


def _rope(x, c_tab, sa_tab, sb_tab):
    wd = x.shape[-1]
    ax = x.ndim - 1
    rep = wd // LANE
    half = ROT_DIM // 2
    if rep > 1:
        c_tab, sa_tab, sb_tab = (jnp.concatenate([tb] * rep, axis=-1) for tb in (c_tab, sa_tab, sb_tab))
    return x * c_tab + pltpu.roll(x, wd - half, ax) * sa_tab + pltpu.roll(x, half, ax) * sb_tab


def _swa_kernel(sink_ref, sq_ref, sz_ref, sk_ref, sv_ref, ct_ref, sat_ref, sbt_ref, kp_ref, vp_ref,
                y_ref, klast_ref, vlast_ref, kprev_ref, vprev_ref, sp_ref, sc_ref, o_ref, *, nseq, qb, pos0):
    blk = pl.program_id(1)
    hd = HEAD_DIM

    @pl.when(blk == 0)
    def _():
        kprev_ref[...] = kp_ref[...]
        vprev_ref[...] = vp_ref[...]

    tabs = tuple(r[...][None] for r in (ct_ref, sat_ref, sbt_ref))
    q3 = _rope(sq_ref[...], *tabs) * (hd ** -0.5)
    k3 = _rope(sk_ref[...], *tabs)
    v3 = sv_ref[...]
    klast_ref[...] = k3
    vlast_ref[...] = v3

    for n in range(nseq):
        for kv in range(SWA_KV_HEADS):
            ds = slice(kv * hd, (kv + 1) * hd)
            qs = jnp.concatenate(
                [q3[n][:, (kv * SWA_GROUP + g) * hd:(kv * SWA_GROUP + g + 1) * hd] for g in range(SWA_GROUP)],
                axis=0).astype(bf16)
            sp_ref[n, kv] = _dot_nt(qs, kprev_ref[n][:, ds])
            sc_ref[n, kv] = _dot_nt(qs, k3[n][:, ds])

    mrows = SWA_GROUP * qb
    qi = lax.broadcasted_iota(jnp.int32, (mrows, WINDOW), 0) % qb
    kj = lax.broadcasted_iota(jnp.int32, (mrows, WINDOW), 1)
    past_ok = (kj >= qi) & (kj >= (WINDOW - pos0) - blk * qb)
    cur_ok = kj <= qi
    sink = jnp.stack([jnp.concatenate([jnp.full((qb, 1), sink_ref[kv * SWA_GROUP + g], f32)
                                       for g in range(SWA_GROUP)], axis=0) for kv in range(SWA_KV_HEADS)])[None]
    s_p = jnp.where(past_ok[None, None], sp_ref[...], -jnp.inf)
    s_c = jnp.where(cur_ok[None, None], sc_ref[...], -jnp.inf)
    m = jnp.maximum(jnp.maximum(jnp.max(s_p, axis=3, keepdims=True), jnp.max(s_c, axis=3, keepdims=True)), sink)
    p_p = jnp.exp(s_p - m)
    p_c = jnp.exp(s_c - m)
    den = jnp.sum(p_p, axis=3, keepdims=True) + jnp.sum(p_c, axis=3, keepdims=True) + jnp.exp(sink - m)
    p_p = p_p.astype(bf16)
    p_c = p_c.astype(bf16)
    for n in range(nseq):
        for kv in range(SWA_KV_HEADS):
            ds = slice(kv * hd, (kv + 1) * hd)
            o_ref[n, kv] = _dot(p_p[n, kv], vprev_ref[n][:, ds]) + _dot(p_c[n, kv], v3[n][:, ds])
    o = o_ref[...] / den
    outs = [o[:, j // SWA_GROUP, (j % SWA_GROUP) * qb:(j % SWA_GROUP + 1) * qb, :] for j in range(SWA_HEADS)]
    y_ref[...] = (jnp.concatenate(outs, axis=2) * _silu(sz_ref[...])).astype(y_ref.dtype)
    kprev_ref[...] = k3
    vprev_ref[...] = v3


_SWA_SEGS = ("sq", "sz", "sk", "sv")


def _swa_call(proj, sinks, k_past, v_past, pos0, nb, nseq, nc, qb, name):
    assert qb == WINDOW
    t_total = nc * qb
    c_tab, sa_tab, sb_tab = _rope_tables(pos0, t_total)
    kern = functools.partial(_swa_kernel, nseq=nseq, qb=qb, pos0=pos0)
    kvw = SWA_KV_HEADS * HEAD_DIM
    mrows = SWA_GROUP * qb
    tab_spec = pl.BlockSpec((qb, LANE), lambda b, c: (c, 0))
    past_spec = pl.BlockSpec((nseq, WINDOW, kvw), lambda b, c: (b, 0, 0))
    return pl.pallas_call(
        kern,
        grid=(nb, nc),
        in_specs=[pl.BlockSpec(memory_space=pltpu.SMEM)] + [_seg_spec(s, nseq, qb) for s in _SWA_SEGS] + [
            tab_spec, tab_spec, tab_spec, past_spec, past_spec],
        out_specs=[_tok_spec(nseq, qb, BRANCH_W), past_spec, past_spec],
        out_shape=[jax.ShapeDtypeStruct((nb * nseq, nc * qb, BRANCH_W), bf16),
                   jax.ShapeDtypeStruct((nb * nseq, WINDOW, kvw), f32),
                   jax.ShapeDtypeStruct((nb * nseq, WINDOW, kvw), f32)],
        scratch_shapes=[pltpu.VMEM((nseq, WINDOW, kvw), f32), pltpu.VMEM((nseq, WINDOW, kvw), f32),
                        pltpu.VMEM((nseq, SWA_KV_HEADS, mrows, WINDOW), f32),
                        pltpu.VMEM((nseq, SWA_KV_HEADS, mrows, qb), f32),
                        pltpu.VMEM((nseq, SWA_KV_HEADS, mrows, HEAD_DIM), f32)],
        compiler_params=_params("parallel", "arbitrary"),
        name=name,
    )(sinks, *([proj] * len(_SWA_SEGS)), c_tab, sa_tab, sb_tab, k_past, v_past)


def _swa_step_kernel(sink_ref, sq_ref, sz_ref, sk_ref, sv_ref, ct_ref, sat_ref, sbt_ref, kp_ref, vp_ref,
                     y_ref, klast_ref, vlast_ref, sp_ref, sc_ref, o_ref, *, nseq, t, pos0):
    hd = HEAD_DIM
    kvw = SWA_KV_HEADS * hd
    mrows = SWA_HEADS * t
    tabs = tuple(r[...][None] for r in (ct_ref, sat_ref, sbt_ref))
    q3 = _rope(sq_ref[...], *tabs) * (hd ** -0.5)
    k3 = _rope(sk_ref[...], *tabs)
    v3 = sv_ref[...]

    lane = lax.broadcasted_iota(jnp.int32, (kvw, WINDOW), 1)
    pad = jnp.zeros((WINDOW - t, kvw), f32)

    def shifted(old_t, new):
        new_t = jnp.concatenate([pad, new], axis=0).T
        out = jnp.where(lane >= WINDOW - t, new_t, pltpu.roll(old_t, WINDOW - t, 1))
        return out.reshape(SWA_KV_HEADS, hd, WINDOW)

    for n in range(nseq):
        klast_ref[n] = shifted(kp_ref[n].reshape(kvw, WINDOW), k3[n])
        vlast_ref[n] = shifted(vp_ref[n].reshape(kvw, WINDOW), v3[n])

    zero = jnp.zeros((nseq, t, hd), f32)
    pieces = []
    for j in range(SWA_HEADS):
        qj = q3[:, :, j * hd:(j + 1) * hd]
        pieces.append(jnp.concatenate([qj, zero] if j // SWA_GROUP == 0 else [zero, qj], axis=2))
    qbd = jnp.concatenate(pieces, axis=1).astype(bf16)

    for n in range(nseq):
        sp_ref[n] = _dot(qbd[n], kp_ref[n].reshape(kvw, WINDOW))
        sc_ref[n] = _dot_nt(qbd[n], k3[n])

    qi = lax.broadcasted_iota(jnp.int32, (mrows, WINDOW), 0) % t
    kj = lax.broadcasted_iota(jnp.int32, (mrows, WINDOW), 1)
    past_ok = kj >= qi
    if pos0 < WINDOW:
        past_ok = past_ok & (kj >= WINDOW - pos0)
    qi_c = lax.broadcasted_iota(jnp.int32, (mrows, t), 0) % t
    kj_c = lax.broadcasted_iota(jnp.int32, (mrows, t), 1)
    cur_ok = kj_c <= qi_c
    sink = jnp.concatenate([jnp.full((t, 1), sink_ref[j], f32) for j in range(SWA_HEADS)], axis=0)[None]
    s_p = jnp.where(past_ok[None], sp_ref[...], -jnp.inf)
    s_c = jnp.where(cur_ok[None], sc_ref[...], -jnp.inf)
    m = jnp.maximum(jnp.maximum(jnp.max(s_p, axis=2, keepdims=True), jnp.max(s_c, axis=2, keepdims=True)), sink)
    p_p = jnp.exp(s_p - m)
    p_c = jnp.exp(s_c - m)
    den = jnp.sum(p_p, axis=2, keepdims=True) + jnp.sum(p_c, axis=2, keepdims=True) + jnp.exp(sink - m)
    p_p = p_p.astype(bf16)
    p_c = p_c.astype(bf16)
    for n in range(nseq):
        o_ref[n] = _dot_nt(p_p[n], vp_ref[n].reshape(kvw, WINDOW)) + _dot(p_c[n], v3[n])
    o = o_ref[...] / den
    outs = []
    for j in range(SWA_HEADS):
        kv = j // SWA_GROUP
        outs.append(o[:, j * t:(j + 1) * t, kv * hd:(kv + 1) * hd])
    y_ref[...] = (jnp.concatenate(outs, axis=2) * _silu(sz_ref[...])).astype(y_ref.dtype)


def _swa_step_call(proj, sinks, k_past_t, v_past_t, layer, pos0, nb, nseq, t, name):
    c_tab, sa_tab, sb_tab = _rope_tables(pos0, t)
    kern = functools.partial(_swa_step_kernel, nseq=nseq, t=t, pos0=pos0)
    kvw = SWA_KV_HEADS * HEAD_DIM
    mrows = SWA_HEADS * t
    tab_spec = pl.BlockSpec((t, LANE), lambda b, c: (0, 0))
    past_spec = pl.BlockSpec((None, nseq, SWA_KV_HEADS, HEAD_DIM, WINDOW), lambda b, c: (layer, b, 0, 0, 0))
    new_spec = pl.BlockSpec((nseq, SWA_KV_HEADS, HEAD_DIM, WINDOW), lambda b, c: (b, 0, 0, 0))
    new_shape = jax.ShapeDtypeStruct((nb * nseq, SWA_KV_HEADS, HEAD_DIM, WINDOW), f32)
    return pl.pallas_call(
        kern,
        grid=(nb, 1),
        in_specs=[pl.BlockSpec(memory_space=pltpu.SMEM)] + [_seg_spec(s, nseq, t) for s in _SWA_SEGS] + [
            tab_spec, tab_spec, tab_spec, past_spec, past_spec],
        out_specs=[_tok_spec(nseq, t, BRANCH_W), new_spec, new_spec],
        out_shape=[jax.ShapeDtypeStruct((nb * nseq, t, BRANCH_W), bf16), new_shape, new_shape],
        scratch_shapes=[pltpu.VMEM((nseq, mrows, WINDOW), f32), pltpu.VMEM((nseq, mrows, t), f32),
                        pltpu.VMEM((nseq, mrows, kvw), f32)],
        compiler_params=_params("parallel", "arbitrary"),
        name=name,
    )(sinks, *([proj] * len(_SWA_SEGS)), c_tab, sa_tab, sb_tab, k_past_t, v_past_t)


def _sgu_kernel(x_ref, wu_ref, wv_ref, wz_ref, g_ref, b_ref, wm_ref, bias_ref, y_ref, *rest, ntile, want_vn):
    vn_ref = rest[0] if want_vn else None
    su_ref, sv_ref, sz_ref = rest[-3:]
    _project(x_ref, ((wu_ref, (su_ref,)), (wv_ref, (sv_ref,)), (wz_ref, (sz_ref,))))
    vn = _layer_norm(sv_ref[...], g_ref[...], b_ref[...])
    if want_vn:
        vn_ref[...] = vn
    for r in range(ntile):
        rows = slice(r * SGU_CHUNK, (r + 1) * SGU_CHUNK)
        mixed = jnp.concatenate(
            [jnp.dot(wm_ref[g], vn[rows, g * SGU_GC:(g + 1) * SGU_GC].astype(bf16), preferred_element_type=f32)
             for g in range(SGU_GROUPS)], axis=1)
        y = su_ref[rows, :] * (mixed + bias_ref[...]) * _silu(sz_ref[rows, :])
        y_ref[rows, :] = y.astype(y_ref.dtype)


def _sgu_call(xb, w_t, ln_g, ln_b, wmix, bias, n_tok, ntile, want_vn, name):
    rows = ntile * SGU_CHUNK
    w = BRANCH_W
    kern = functools.partial(_sgu_kernel, ntile=ntile, want_vn=want_vn)

    def const(shape):
        return pl.BlockSpec(shape, lambda i: (0,) * len(shape))

    out_specs = [pl.BlockSpec((rows, w), lambda i: (i, 0))]
    out_shape = [jax.ShapeDtypeStruct((n_tok, w), bf16)]
    if want_vn:
        out_specs.append(pl.BlockSpec((rows, w), lambda i: (i, 0)))
        out_shape.append(jax.ShapeDtypeStruct((n_tok, w), f32))
    return pl.pallas_call(
        kern,
        grid=(n_tok // rows,),
        in_specs=[pl.BlockSpec((rows, D_MODEL), lambda i: (i, 0)), _w_spec("su", "su"), _w_spec("svv", "svv"),
                  _w_spec("suz", "suz"), const((1, w)), const((1, w)),
                  const((SGU_GROUPS, SGU_CHUNK, SGU_CHUNK)), const((SGU_CHUNK, w))],
        out_specs=out_specs,
        out_shape=out_shape,
        scratch_shapes=[_seg_scratch(s, rows) for s in ("su", "svv", "suz")],
        compiler_params=_params("parallel"),
        name=name,
    )(xb, w_t, w_t, w_t, ln_g, ln_b, wmix, bias)


def _mem_kernel(x_ref, wm_ref, mk_ref, mv_ref, y_ref, mq_ref, mz_ref, *, nseq, tq):
    _project(x_ref, ((wm_ref, (mq_ref, mz_ref)),))
    for n in range(nseq):
        q = mq_ref[n] * (HEAD_DIM ** -0.5)
        mk = mk_ref[n]
        mv = mv_ref[n]
        outs = []
        for h in range(MEM_HEADS):
            ds = slice(h * HEAD_DIM, (h + 1) * HEAD_DIM)
            s = _dot_nt(q[:, ds], mk[:, ds])
            m = jnp.max(s, axis=1, keepdims=True)
            p = jnp.exp(s - m)
            den = jnp.sum(p, axis=1, keepdims=True)
            outs.append(_dot(p, mv[:, ds]) / den)
        o = jnp.concatenate(outs, axis=1)
        y_ref[n] = (o * _silu(mz_ref[n])).astype(y_ref.dtype)


def _mem_call(xb, w_t, mk, mv, nb, nseq, nc, tq, name):
    kern = functools.partial(_mem_kernel, nseq=nseq, tq=tq)
    kv_spec = pl.BlockSpec((nseq, N_MEM, MEM_W), lambda b, c: (b, 0, 0))
    return pl.pallas_call(
        kern,
        grid=(nb, nc),
        in_specs=[_tok_spec(nseq, tq, D_MODEL), _w_spec("mq", "mz"), kv_spec, kv_spec],
        out_specs=_tok_spec(nseq, tq, MEM_W),
        out_shape=jax.ShapeDtypeStruct((nb * nseq, nc * tq, MEM_W), bf16),
        scratch_shapes=[_seg_scratch("mq", nseq, tq), _seg_scratch("mz", nseq, tq)],
        compiler_params=_params("parallel", "arbitrary"),
        name=name,
    )(xb, w_t, mk, mv)


def _mem_step_kernel(x_ref, wm_ref, mk_ref, mv_ref, y_ref, s_ref, o_ref, mq_ref, mz_ref, *, nseq, t):
    mrows = MEM_HEADS * t
    _project(x_ref, ((wm_ref, (mq_ref, mz_ref)),))
    row_head = lax.broadcasted_iota(jnp.int32, (mrows, MEM_W), 0) // t
    lane_head = lax.broadcasted_iota(jnp.int32, (mrows, MEM_W), 1) // HEAD_DIM
    own = (row_head == lane_head)[None]
    q3 = mq_ref[...] * (HEAD_DIM ** -0.5)
    qbd = jnp.where(own, jnp.concatenate([q3] * MEM_HEADS, axis=1), 0.0).astype(bf16)
    for n in range(nseq):
        s_ref[n] = _dot(qbd[n], mk_ref[n].reshape(MEM_W, N_MEM))
    s = s_ref[...]
    p = jnp.exp(s - jnp.max(s, axis=2, keepdims=True))
    den = jnp.sum(p, axis=2, keepdims=True)
    p = p.astype(bf16)
    for n in range(nseq):
        o_ref[n] = _dot_nt(p[n], mv_ref[n].reshape(MEM_W, N_MEM))
    o = jnp.where(own, o_ref[...] / den, 0.0)
    acc = o[:, 0:t, :]
    for h in range(1, MEM_HEADS):
        acc = acc + o[:, h * t:(h + 1) * t, :]
    y_ref[...] = (acc * _silu(mz_ref[...])).astype(y_ref.dtype)


def _mem_step_call(xb, w_t, mk_t, mv_t, layer, nb, nseq, t, name):
    kern = functools.partial(_mem_step_kernel, nseq=nseq, t=t)
    kv_spec = pl.BlockSpec((None, nseq, MEM_HEADS, HEAD_DIM, N_MEM), lambda b, c: (layer, b, 0, 0, 0))
    mrows = MEM_HEADS * t
    return pl.pallas_call(
        kern,
        grid=(nb, 1),
        in_specs=[_tok_spec(nseq, t, D_MODEL), _w_spec("mq", "mz"), kv_spec, kv_spec],
        out_specs=_tok_spec(nseq, t, MEM_W),
        out_shape=jax.ShapeDtypeStruct((nb * nseq, t, MEM_W), bf16),
        scratch_shapes=[pltpu.VMEM((nseq, mrows, N_MEM), f32), pltpu.VMEM((nseq, mrows, MEM_W), f32),
                        _seg_scratch("mq", nseq, t), _seg_scratch("mz", nseq, t)],
        compiler_params=_params("parallel", "arbitrary"),
        name=name,
    )(xb, w_t, mk_t, mv_t)


def _merge_kernel(yg_ref, yl_ref, ys_ref, yu_ref, ym_ref, x_ref, wg_ref, wb_ref, wm_ref, wo_ref, g_ref, b_ref,
                  o_ref, ob_ref):
    d = D_MODEL
    x = x_ref[...]
    xb = x.astype(bf16)

    def gate(n):
        return _sigmoid(_dot_nt(xb, wg_ref[n * d:(n + 1) * d, :]))

    merged = gate(4) * jnp.dot(ym_ref[...], wm_ref[...], preferred_element_type=f32)
    for n, y_ref in enumerate((yg_ref, yl_ref, ys_ref, yu_ref)):
        merged = merged + gate(n) * jnp.dot(y_ref[...], wb_ref[n], preferred_element_type=f32)
    out = _dot(merged, wo_ref[...])
    y = _layer_norm(DN_ALPHA * x + out, g_ref[...], b_ref[...])
    o_ref[...] = y
    ob_ref[...] = y.astype(bf16)


def _merge_call(ys, x, mw, tm, name):
    n_tok, d = x.shape
    w = BRANCH_W

    def rows(width):
        return pl.BlockSpec((tm, width), lambda i: (i, 0))

    def const(shape):
        return pl.BlockSpec(shape, lambda i: (0,) * len(shape), pipeline_mode=pl.Buffered(1))

    return pl.pallas_call(
        _merge_kernel,
        grid=(n_tok // tm,),
        in_specs=[rows(w), rows(w), rows(w), rows(w), rows(MEM_W), rows(d),
                  const((5 * d, d)), const((4, w, d)), const((MEM_W, d)), const((d, d)), const((1, d)),
                  const((1, d))],
        out_specs=[rows(d), rows(d)],
        out_shape=[jax.ShapeDtypeStruct((n_tok, d), f32), jax.ShapeDtypeStruct((n_tok, d), bf16)],
        compiler_params=_params("parallel"),
        name=name,
    )(*ys, x, mw["wg"], mw["wb"], mw["wm"], mw["wo"], mw["g"], mw["b"])


def _prep_layer(l, w_in, gla_wa2, gla_ba, gla_norm_g, lru_conv_w, lru_conv_b, lru_wr, lru_br, lru_wi, lru_bi, lru_L,
                swa_sinks, sgu_ln_g, sgu_ln_b, sgu_w, sgu_b, w_mem_kv, w_branch, w_branch_mem, w_out, ln_g, ln_b):
    d = D_MODEL
    w = BRANCH_W
    w_t = jnp.swapaxes(w_in, 1, 2)[l]
    runs = []
    used = 0
    for run in _PROJ_RUNS:
        if run is None:
            runs.append(jnp.zeros((N_PROJ - used, d), bf16))
            used = N_PROJ
            continue
        lo = _ORIG_OFF[run[0]][0]
        hi = _ORIG_OFF[run[1]][0] + _ORIG_OFF[run[1]][1]
        runs.append(w_t[lo:hi].astype(bf16))
        used += hi - lo
    assert used == N_WROWS
    w_proj = jnp.concatenate(runs, axis=0)
    g_lo = _ORIG_OFF["gates"][0]
    w_gates = w_t[g_lo:].astype(bf16)

    def block_diag(wb):
        eye = jnp.eye(LRU_BLOCKS, dtype=f32)
        return (eye[:, None, :, None] * wb[:, :, None, :]).reshape(w, w).astype(bf16)

    tril = jnp.tril(jnp.ones((SGU_CHUNK, SGU_CHUNK), f32))
    wmix_p = (sgu_w[l] * tril).astype(bf16)
    bias_p = jnp.repeat(sgu_b[l].T, SGU_GC, axis=1)
    t8 = SUBLANE
    rep = SGU_CHUNK // t8
    w8 = (sgu_w[l] * tril)[:, :t8, :t8]
    seq_eye = jnp.eye(rep, dtype=f32)
    wmix_s = (seq_eye[None, :, None, :, None] * w8[:, None, :, None, :]).reshape(
        SGU_GROUPS, SGU_CHUNK, SGU_CHUNK).astype(bf16)
    bias_s = jnp.tile(bias_p[:t8], (rep, 1))
    return dict(
        w_proj=w_proj,
        w_mem_kv=w_mem_kv[l].astype(bf16),
        gla=dict(wa=jnp.pad(gla_wa2[l], ((0, LANE - GLA_RANK), (0, 0))).astype(bf16),
                 ba=gla_ba[l].reshape(1, -1), ng=gla_norm_g[l].reshape(1, -1)),
        lru=dict(conv_w=lru_conv_w[l], conv_b=lru_conv_b[l].reshape(1, w), wr=block_diag(lru_wr[l]),
                 br=lru_br[l].reshape(1, w), wi=block_diag(lru_wi[l]), bi=lru_bi[l].reshape(1, w),
                 lam=lru_L[l].reshape(1, w)),
        sinks=swa_sinks[l],
        sgu=dict(g=sgu_ln_g[l].reshape(1, w), b=sgu_ln_b[l].reshape(1, w), wmix_p=wmix_p, bias_p=bias_p,
                 wmix_s=wmix_s, bias_s=bias_s),
        merge=dict(wg=w_gates, wb=w_branch[l].astype(bf16), wm=w_branch_mem[l].astype(bf16), wo=w_out[l].astype(bf16),
                   g=ln_g[l].reshape(1, d), b=ln_b[l].reshape(1, d)),
    )


def _layer(x, lw, grp, st, layer, tag):
    nseq_total, t = grp["batch"], grp["seq"]
    n_tok = nseq_total * t
    x, xb = x
    xb3 = xb.reshape(nseq_total, t, D_MODEL)
    w_t = lw["w_proj"]
    proj = _matmul_call(xb, w_t, min(grp["proj_tm"], n_tok), 1024, "proj_" + tag, w_transposed=True, n_out=N_PROJ)
    proj3 = proj.reshape(nseq_total, t, N_PROJ)

    lt = grp["lru"]
    y_lru, hlast, hist = _lru_call(proj3, st["hist0"], st["h0"], lw["lru"], nseq_total // lt[0], lt[0], t // lt[1],
                                   lt[1], "lru_" + tag)
    short = grp["kind"] == "s"
    gt = grp["gla"]
    if short:
        y_gla, s_out = _gla_step_call(proj3, st["gla0"], layer, lw["gla"], nseq_total // gt[0], gt[0], t,
                                      "gla_" + tag)
    else:
        y_gla, s_out = _gla_call(proj3, st["gla0"], lw["gla"], nseq_total // gt[0], gt[0], t // gt[1], gt[1],
                                 "gla_" + tag)
    wt = grp["swa"]
    if short:
        y_swa, k_last, v_last = _swa_step_call(proj3, lw["sinks"], st["k_past"], st["v_past"], layer, grp["pos0"],
                                               nseq_total // wt[0], wt[0], t, "swa_" + tag)
    else:
        y_swa, k_last, v_last = _swa_call(proj3, lw["sinks"], st["k_past"], st["v_past"], grp["pos0"],
                                          nseq_total // wt[0], wt[0], t // wt[1], wt[1], "swa_" + tag)
    sg = lw["sgu"]
    sgu_out = _sgu_call(xb, w_t, sg["g"], sg["b"], sg["wmix_" + grp["kind"]], sg["bias_" + grp["kind"]], n_tok,
                        grp["sgu_tiles"], grp["kind"] == "s", "sgu_" + tag)
    mt = grp["mem"]
    if short:
        y_mem = _mem_step_call(xb3, w_t, st["mk"], st["mv"], layer, nseq_total // mt[0], mt[0], t, "mem_" + tag)
    else:
        y_mem = _mem_call(xb3, w_t, st["mk"], st["mv"], nseq_total // mt[0], mt[0], t // mt[1], mt[1],
                          "mem_" + tag)
    ys = tuple(y.reshape(n_tok, y.shape[-1]) for y in (y_gla, y_lru, y_swa, sgu_out[0], y_mem))
    x_new = _merge_call(ys, x, lw["merge"], min(256, n_tok), "merge_" + tag)
    return x_new, dict(gla=s_out, hlast=hlast, hist=hist, k_last=k_last, v_last=v_last,
                       vn=sgu_out[1] if len(sgu_out) > 1 else None)


_PROMPT = dict(kind="p", pos0=0, proj_tm=2048, lru=(1, 256), gla=(2, 128), swa=(2, 128), sgu_tiles=4, mem=(1, 512))
_SAMPLE = dict(kind="s", pos0=PAST_LEN, proj_tm=1024, lru=(32, 8), gla=(16, 8), swa=(16, 8), sgu_tiles=8,
               mem=(16, 8))


def kernel(x_prompt, x_sample, mem_prompt, state_gla, state_lru_h, state_lru_conv, cache_swa_k, cache_swa_v,
           cache_mem_k, cache_mem_v, ln_in_g, ln_in_b, w_in, gla_wa2, gla_ba, gla_norm_g, lru_conv_w, lru_conv_b,
           lru_wr, lru_br, lru_wi, lru_bi, lru_L, swa_sinks, sgu_ln_g, sgu_ln_b, sgu_w, sgu_b, w_mem_kv, w_branch,
           w_branch_mem, w_out, ln_g, ln_b):
    bp, tp, d = x_prompt.shape
    bs, ts, _ = x_sample.shape
    w = BRANCH_W
    kvw = SWA_KV_HEADS * HEAD_DIM
    gp = dict(_PROMPT, batch=bp, seq=tp)
    gs = dict(_SAMPLE, batch=bs, seq=ts)

    xp = _ln_call(x_prompt.reshape(bp * tp, d), ln_in_g, ln_in_b)
    xs = _ln_call(x_sample.reshape(bs * ts, d), ln_in_g, ln_in_b)
    mem2 = mem_prompt.reshape(bp * N_MEM, d)

    swa_k_t, swa_v_t, mem_k_t, mem_v_t = (jnp.transpose(c, (0, 1, 3, 4, 2))
                                          for c in (cache_swa_k, cache_swa_v, cache_mem_k, cache_mem_v))

    outs_p, outs_s, mks, mvs = [], [], [], []
    for l in range(DEPTH):
        lw = _prep_layer(l, w_in, gla_wa2, gla_ba, gla_norm_g, lru_conv_w, lru_conv_b, lru_wr, lru_br, lru_wi,
                         lru_bi, lru_L, swa_sinks, sgu_ln_g, sgu_ln_b, sgu_w, sgu_b, w_mem_kv, w_branch,
                         w_branch_mem, w_out, ln_g, ln_b)
        mkv = _matmul_call(mem2, lw["w_mem_kv"], bp * N_MEM, 2 * MEM_W, "memkv_%d" % l)
        mk = mkv[:, :MEM_W].reshape(bp, N_MEM, MEM_W)
        mv = mkv[:, MEM_W:].reshape(bp, N_MEM, MEM_W)
        st_p = dict(hist0=jnp.zeros((bp, SUBLANE, w), f32), h0=jnp.zeros((bp, 1, w), f32),
                    gla0=jnp.zeros((bp, GLA_HEADS, GLA_DK, GLA_DV), f32),
                    k_past=jnp.zeros((bp, WINDOW, kvw), f32), v_past=jnp.zeros((bp, WINDOW, kvw), f32),
                    mk=mk, mv=mv)
        st_s = dict(hist0=jnp.pad(state_lru_conv[l], ((0, 0), (SUBLANE - (CONV_W - 1), 0), (0, 0))),
                    h0=state_lru_h[l][:, None, :], gla0=state_gla,
                    k_past=swa_k_t, v_past=swa_v_t, mk=mem_k_t, mv=mem_v_t)
        xp, op = _layer(xp, lw, gp, st_p, l, "p%d" % l)
        xs, os_ = _layer(xs, lw, gs, st_s, l, "s%d" % l)
        outs_p.append(op)
        outs_s.append(os_)
        mks.append(mk.reshape(bp, N_MEM, MEM_HEADS, HEAD_DIM))
        mvs.append(mv.reshape(bp, N_MEM, MEM_HEADS, HEAD_DIM))

    def stack(outs, fn):
        return jnp.stack([fn(o) for o in outs])

    def window(a):
        return a.reshape(a.shape[0], WINDOW, SWA_KV_HEADS, HEAD_DIM)

    def window_t(a):
        return jnp.transpose(a, (0, 3, 1, 2))

    return (
        xp[0].reshape(bp, tp, d), xs[0].reshape(bs, ts, d),
        stack(outs_p, lambda o: o["gla"]), stack(outs_s, lambda o: o["gla"]),
        stack(outs_p, lambda o: o["hlast"][:, SUBLANE - 1]), stack(outs_s, lambda o: o["hlast"][:, SUBLANE - 1]),
        stack(outs_p, lambda o: o["hist"][:, SUBLANE - (CONV_W - 1):]),
        stack(outs_s, lambda o: o["hist"][:, SUBLANE - (CONV_W - 1):]),
        stack(outs_p, lambda o: window(o["k_last"])), stack(outs_s, lambda o: window_t(o["k_last"])),
        stack(outs_p, lambda o: window(o["v_last"])), stack(outs_s, lambda o: window_t(o["v_last"])),
        jnp.stack(mks), jnp.stack(mvs),
        stack(outs_s, lambda o: o["vn"].reshape(bs, ts, w)),
    )
```

```python
import functools
import math

import jax
import jax.numpy as jnp
import numpy as np
from jax import lax
from jax.experimental import pallas as pl
from jax.experimental.pallas import tpu as pltpu

f32 = jnp.float32
bf16 = jnp.bfloat16

D_MODEL = 1024
DEPTH = 2
PAST_LEN = 8192
BRANCH_W = 512
GLA_HEADS = 4
GLA_DK = 64
GLA_DV = 128
GLA_RANK = 16
GLA_TAU = 16.0
LRU_BLOCKS = 8
LRU_BS = 64
CONV_W = 4
LRU_C = 8.0
HEAD_DIM = 64
SWA_HEADS = 8
SWA_KV_HEADS = 2
SWA_GROUP = 4
WINDOW = 128
ROT_DIM = 16
ROPE_THETA = 500000.0
SGU_GROUPS = 4
SGU_GC = 128
SGU_CHUNK = 128
N_MEM = 256
MEM_HEADS = 4
MEM_W = 256
LN_EPS = 1e-5
RMS_EPS = 1e-6
DN_ALPHA = (2 * DEPTH) ** 0.25

LANE = 128
SUBLANE = 8

_ORIG = (("gq", 256), ("gk", 256), ("gv", 512), ("glr", 16), ("gz", 512), ("lx", 512), ("lz", 512), ("sq", 512),
         ("sk", 128), ("sv", 128), ("sz", 512), ("su", 512), ("svv", 512), ("suz", 512), ("mq", 256), ("mz", 256),
         ("gates", 5 * D_MODEL))
_ORIG_OFF = {}
_off = 0
for _n, _w in _ORIG:
    _ORIG_OFF[_n] = (_off, _w)
    _off += _w
_PROJ_RUNS = (("gq", "gv"), ("gz", "sq"), ("sz", "sz"), ("sk", "sv"), ("glr", "glr"), None, ("su", "mz"))
_SEG = {}
_off = 0
for _run in _PROJ_RUNS:
    if _run is None:
        N_PROJ = -(-_off // 1024) * 1024
        _off = N_PROJ
        continue
    _names = [n for n, _ in _ORIG]
    for _n in _names[_names.index(_run[0]):_names.index(_run[1]) + 1]:
        _w = max(_ORIG_OFF[_n][1], 128)
        assert _off % _w == 0
        _SEG[_n] = (_off, _w)
        _off += _w
N_WROWS = _off


def _dot(a, b):
    return jnp.dot(a.astype(bf16), b.astype(bf16), preferred_element_type=f32)


def _dot_nt(a, b):
    return lax.dot_general(a.astype(bf16), b.astype(bf16), (((1,), (1,)), ((), ())), preferred_element_type=f32)


def _dot_tn(a, b):
    return lax.dot_general(a.astype(bf16), b.astype(bf16), (((0,), (0,)), ((), ())), preferred_element_type=f32)


def _sigmoid(x):
    return 1.0 / (1.0 + jnp.exp(-x))


def _silu(x):
    return x * _sigmoid(x)


def _log_sigmoid(x):
    return jnp.minimum(x, 0.0) - jnp.log(1.0 + jnp.exp(-jnp.abs(x)))


def _layer_norm(x, g, b):
    mu = jnp.mean(x, axis=-1, keepdims=True)
    xc = x - mu
    var = jnp.mean(xc * xc, axis=-1, keepdims=True)
    return xc * lax.rsqrt(var + LN_EPS) * g + b


def _params(*sem):
    return pltpu.CompilerParams(dimension_semantics=sem)


def _ln_kernel(x_ref, g_ref, b_ref, o_ref, ob_ref):
    y = _layer_norm(x_ref[...], g_ref[...], b_ref[...])
    o_ref[...] = y
    ob_ref[...] = y.astype(bf16)


def _ln_call(x, g, b, tm=512):
    n, d = x.shape
    tm = min(tm, n)
    return pl.pallas_call(
        _ln_kernel,
        grid=(n // tm,),
        in_specs=[pl.BlockSpec((tm, d), lambda i: (i, 0)), pl.BlockSpec((1, d), lambda i: (0, 0)),
                  pl.BlockSpec((1, d), lambda i: (0, 0))],
        out_specs=[pl.BlockSpec((tm, d), lambda i: (i, 0)), pl.BlockSpec((tm, d), lambda i: (i, 0))],
        out_shape=[jax.ShapeDtypeStruct((n, d), f32), jax.ShapeDtypeStruct((n, d), bf16)],
        compiler_params=_params("parallel"),
        name="ln_in",
    )(x, g.reshape(1, d), b.reshape(1, d))


def _matmul_kernel(x_ref, w_ref, o_ref, *, w_transposed):
    o_ref[...] = (_dot_nt if w_transposed else _dot)(x_ref[...], w_ref[...])


def _matmul_call(x, w, tm, tn, name, w_transposed=False, n_out=None, layer=None):
    m, k = x.shape
    n = n_out or (w.shape[-2] if w_transposed else w.shape[1])
    if layer is not None:
        w_spec = pl.BlockSpec((None, tn, k), lambda i, j: (layer, j, 0))
    elif w_transposed:
        w_spec = pl.BlockSpec((tn, k), lambda i, j: (j, 0))
    else:
        w_spec = pl.BlockSpec((k, tn), lambda i, j: (0, j))
    return pl.pallas_call(
        functools.partial(_matmul_kernel, w_transposed=w_transposed),
        grid=(m // tm, n // tn),
        in_specs=[pl.BlockSpec((tm, k), lambda i, j: (i, 0)), w_spec],
        out_specs=pl.BlockSpec((tm, tn), lambda i, j: (i, j)),
        out_shape=jax.ShapeDtypeStruct((m, n), f32),
        compiler_params=_params("parallel", "arbitrary"),
        name=name,
    )(x, w)


def _seg_spec(name, nseq, rows):
    off, width = _SEG[name]
    assert off + width <= N_PROJ
    cb = off // width
    return pl.BlockSpec((nseq, rows, width), lambda b, c: (b, c, cb))


def _tok_spec(nseq, rows, width):
    return pl.BlockSpec((nseq, rows, width), lambda b, c: (b, c, 0))


def _w_spec(first, last, layer):
    off = _SEG[first][0]
    rows = _SEG[last][0] + _SEG[last][1] - off
    assert off % rows == 0
    return pl.BlockSpec((None, rows, D_MODEL), lambda *_: (layer, off // rows, 0))


def _seg_scratch(name, *lead):
    return pltpu.VMEM((*lead, _SEG[name][1]), f32)


def _project(x_ref, pairs):
    x2 = x_ref[...].reshape(-1, D_MODEL)
    for w_ref, seg_refs in pairs:
        p = lax.dot_general(x2, w_ref[...], (((1,), (1,)), ((), ())), preferred_element_type=f32)
        off = 0
        for s_ref in seg_refs:
            width = s_ref.shape[-1]
            s_ref[...] = p[:, off:off + width].reshape(s_ref.shape)
            off += width


def _const_spec(shape):
    nd = len(shape)
    return pl.BlockSpec(shape, lambda b, c: (0,) * nd)


def _lru_kernel(lx_ref, lz_ref, hist0_ref, h0_ref, cw_ref, cb_ref, wr_ref, br_ref, wi_ref, bi_ref, lam_ref,
                y_ref, hlast_ref, hist_out_ref, hist_ref, hc_ref, *, nseq, tc):
    c = pl.program_id(1)
    w = BRANCH_W

    @pl.when(c == 0)
    def _():
        hist_ref[...] = hist0_ref[...]
        hc_ref[...] = h0_ref[...]

    x = lx_ref[...]
    xfull = jnp.concatenate([hist_ref[...], x], axis=1)

    def tap(j):
        return cw_ref[j:j + 1, :].reshape(1, 1, w)

    y = cb_ref[...].reshape(1, 1, w) + x * tap(CONV_W - 1)
    for s in range(1, CONV_W):
        y = y + pltpu.roll(xfull, s, 1)[:, SUBLANE:, :] * tap(CONV_W - 1 - s)
    hist_ref[...] = xfull[:, tc:, :]
    hist_out_ref[...] = xfull[:, tc:, :]

    xc = y.reshape(nseq * tc, w)
    r = _sigmoid(_dot(xc, wr_ref[...]) + br_ref[...])
    i = _sigmoid(_dot(xc, wi_ref[...]) + bi_ref[...])
    log_a = (LRU_C * r) * _log_sigmoid(lam_ref[...])
    a = jnp.exp(log_a)
    u = jnp.sqrt(jnp.tanh(-log_a) * (a * a + 1.0)) * (i * xc)

    acc_a = a.reshape(nseq, tc, w)
    acc_u = u.reshape(nseq, tc, w)
    t = lax.broadcasted_iota(jnp.int32, (nseq, tc, w), 1)
    d = 1
    while d < tc:
        if d % SUBLANE:
            ok = t >= d
            a_sh = jnp.where(ok, pltpu.roll(acc_a, d, 1), 1.0)
            u_sh = jnp.where(ok, pltpu.roll(acc_u, d, 1), 0.0)
            acc_u = acc_a * u_sh + acc_u
            acc_a = acc_a * a_sh
        else:
            new_u = acc_a[:, d:, :] * acc_u[:, :tc - d, :] + acc_u[:, d:, :]
            new_a = acc_a[:, d:, :] * acc_a[:, :tc - d, :]
            acc_u = jnp.concatenate([acc_u[:, :d, :], new_u], axis=1)
            acc_a = jnp.concatenate([acc_a[:, :d, :], new_a], axis=1)
        d *= 2
    h = acc_a * hc_ref[...] + acc_u
    hc_ref[...] = h[:, tc - 1:tc, :]
    hlast_ref[...] = h[:, tc - SUBLANE:, :]
    y_ref[...] = (h * _silu(lz_ref[...])).astype(y_ref.dtype)


def _lru_call(proj, hist0, h0, lw, nb, nseq, nc, tc, name):
    w = BRANCH_W
    kern = functools.partial(_lru_kernel, nseq=nseq, tc=tc)
    return pl.pallas_call(
        kern,
        grid=(nb, nc),
        in_specs=[_seg_spec("lx", nseq, tc), _seg_spec("lz", nseq, tc),
                  pl.BlockSpec((nseq, SUBLANE, w), lambda b, c: (b, 0, 0)),
                  pl.BlockSpec((nseq, 1, w), lambda b, c: (b, 0, 0)),
                  _const_spec((CONV_W, w)), _const_spec((1, w)), _const_spec((w, w)), _const_spec((1, w)),
                  _const_spec((w, w)), _const_spec((1, w)), _const_spec((1, w))],
        out_specs=[_tok_spec(nseq, tc, w),
                   pl.BlockSpec((nseq, SUBLANE, w), lambda b, c: (b, 0, 0)),
                   pl.BlockSpec((nseq, SUBLANE, w), lambda b, c: (b, 0, 0))],
        out_shape=[jax.ShapeDtypeStruct((nb * nseq, nc * tc, w), bf16),
                   jax.ShapeDtypeStruct((nb * nseq, SUBLANE, w), f32),
                   jax.ShapeDtypeStruct((nb * nseq, SUBLANE, w), f32)],
        scratch_shapes=[pltpu.VMEM((nseq, SUBLANE, w), f32), pltpu.VMEM((nseq, 1, w), f32)],
        compiler_params=_params("parallel", "arbitrary"),
        name=name,
    )(proj, proj, hist0, h0, lw["conv_w"], lw["conv_b"], lw["wr"], lw["br"], lw["wi"], lw["bi"], lw["lam"])


def _gla_consts(c):
    t = np.arange(c)[:, None]
    u = np.arange(c)[None, :]
    blocks = [u <= t, u > t]
    masks = [t == u]
    m = 1
    while m < c:
        t0 = (t // m) * m
        odd = (t // m) % 2 == 1
        blocks.append(odd & (u >= t0) & (u <= t))
        blocks.append((~odd) & (u > t) & (u <= t0 + m - 1))
        masks.append((t // (2 * m) == u // (2 * m)) & odd & ((u // m) % 2 == 0))
        m *= 2
    return (np.concatenate(blocks, 0).astype(np.float32), np.stack(masks).astype(np.float32))


def _gla_kernel(gq_ref, gk_ref, gv_ref, gz_ref, glr_ref, s0_ref, wa_ref, ba_ref, ng_ref, d_ref, m_ref,
                y_ref, sout_ref, s_ref, *, nseq, c):
    ci = pl.program_id(1)
    nlev = int(math.log2(c))
    hk = GLA_HEADS * GLA_DK

    @pl.when(ci == 0)
    def _():
        s_ref[...] = s0_ref[...]

    eye_r = lax.broadcasted_iota(jnp.int32, (GLA_DK, GLA_DK), 0)
    eye_c = lax.broadcasted_iota(jnp.int32, (GLA_DK, GLA_DK), 1)
    eye = eye_r == eye_c

    for n in range(nseq):
        q = gq_ref[n] * (GLA_DK ** -0.5)
        k = gk_ref[n]
        v = gv_ref[n]
        z = _dot(glr_ref[n], wa_ref[...]) + ba_ref[...]
        la = _log_sigmoid(z) * (1.0 / GLA_TAU)
        hi = la.astype(bf16)
        r1 = la - hi.astype(f32)
        mid = r1.astype(bf16)
        lo = (r1 - mid.astype(f32)).astype(bf16)
        hml = jnp.concatenate([hi, mid, lo], axis=1)

        def sums(blk):
            p = jnp.dot(d_ref[blk * c:(blk + 1) * c, :].astype(bf16), hml, preferred_element_type=f32)
            return p[:, :hk] + p[:, hk:2 * hk] + p[:, 2 * hk:]

        b = sums(0)
        q_in = q * jnp.exp(b)
        k_st = k * jnp.exp(sums(1))
        dec_row = jnp.exp(b[c - 1:c, :])
        qf = [q]
        kf = [k]
        for lev in range(nlev):
            qf.append(q * jnp.exp(sums(2 + 2 * lev)))
            kf.append(k * jnp.exp(sums(3 + 2 * lev)))

        for h in range(GLA_HEADS):
            ks = slice(h * GLA_DK, (h + 1) * GLA_DK)
            vs = slice(h * GLA_DV, (h + 1) * GLA_DV)
            att = jnp.zeros((c, c), f32)
            for lev in range(nlev + 1):
                att = att + m_ref[lev] * _dot_nt(qf[lev][:, ks], kf[lev][:, ks])
            s_h = s_ref[n, h]
            v_h = v[:, vs]
            o = _dot(q_in[:, ks], s_h) + _dot(att, v_h)
            dec_col = jnp.sum(jnp.where(eye, jnp.broadcast_to(dec_row[:, ks], (GLA_DK, GLA_DK)), 0.0),
                              axis=1, keepdims=True)
            s_ref[n, h] = s_h * dec_col + _dot_tn(k_st[:, ks], v_h)
            o = o * lax.rsqrt(jnp.mean(o * o, axis=-1, keepdims=True) + RMS_EPS) * ng_ref[...]
            y_ref[n, :, vs] = (o * _silu(gz_ref[n, :, vs])).astype(y_ref.dtype)

    sout_ref[...] = s_ref[...]


_GLA_SEGS = ("gq", "gk", "gv", "gz", "glr")


def _gla_call(proj, s0, gw, nb, nseq, nc, c, name):
    dstack, masks = _gla_consts(c)
    kern = functools.partial(_gla_kernel, nseq=nseq, c=c)
    hk = GLA_HEADS * GLA_DK
    st_spec = pl.BlockSpec((nseq, GLA_HEADS, GLA_DK, GLA_DV), lambda b, ci: (b, 0, 0, 0))
    return pl.pallas_call(
        kern,
        grid=(nb, nc),
        in_specs=[_seg_spec(s, nseq, c) for s in _GLA_SEGS] + [
            st_spec, _const_spec((LANE, hk)), _const_spec((1, hk)), _const_spec((1, GLA_DV)),
            _const_spec(dstack.shape), _const_spec(masks.shape)],
        out_specs=[_tok_spec(nseq, c, BRANCH_W), st_spec],
        out_shape=[jax.ShapeDtypeStruct((nb * nseq, nc * c, BRANCH_W), bf16),
                   jax.ShapeDtypeStruct((nb * nseq, GLA_HEADS, GLA_DK, GLA_DV), f32)],
        scratch_shapes=[pltpu.VMEM((nseq, GLA_HEADS, GLA_DK, GLA_DV), f32)],
        compiler_params=_params("parallel", "arbitrary"),
        name=name,
    )(*([proj] * len(_GLA_SEGS)), s0, gw["wa"], gw["ba"], gw["ng"], jnp.asarray(dstack), jnp.asarray(masks))


def _gla_step_consts(t, nseq):
    dstack, masks = _gla_consts(t)
    eye = np.eye(nseq, dtype=np.float32)
    dbd = np.stack([np.kron(eye, dstack[i * t:(i + 1) * t]) for i in range(dstack.shape[0] // t)])
    mbd = np.stack([np.kron(eye, m) for m in masks])
    return dbd, mbd


def _gla_step_kernel(gq_ref, gk_ref, gv_ref, gz_ref, glr_ref, s0_ref, wa_ref, ba_ref, ng_ref, d_ref, m_ref,
                     y_ref, sout_ref, *, nseq, t):
    r = nseq * t
    nlev = int(math.log2(t))
    hk = GLA_HEADS * GLA_DK
    q = gq_ref[...].reshape(r, hk) * (GLA_DK ** -0.5)
    k = gk_ref[...].reshape(r, hk)
    v = gv_ref[...].reshape(r, BRANCH_W)
    gz = gz_ref[...].reshape(r, BRANCH_W)
    z = _dot(glr_ref[...].reshape(r, LANE), wa_ref[...]) + ba_ref[...]
    la = _log_sigmoid(z) * (1.0 / GLA_TAU)
    hi = la.astype(bf16)
    r1 = la - hi.astype(f32)
    mid = r1.astype(bf16)
    lo = (r1 - mid.astype(f32)).astype(bf16)
    hml = jnp.concatenate([hi, mid, lo], axis=1)

    def sums(blk):
        p = jnp.dot(d_ref[blk].astype(bf16), hml, preferred_element_type=f32)
        return p[:, :hk] + p[:, hk:2 * hk] + p[:, 2 * hk:]

    b = sums(0)
    q_in = q * jnp.exp(b)
    k_st = k * jnp.exp(sums(1))
    dec3 = jnp.exp(b.reshape(nseq, t, hk)[:, t - 1:t, :])
    qf = [q]
    kf = [k]
    for lev in range(nlev):
        qf.append(q * jnp.exp(sums(2 + 2 * lev)))
        kf.append(k * jnp.exp(sums(3 + 2 * lev)))

    own = (lax.broadcasted_iota(jnp.int32, (r, nseq * GLA_DK), 0) // t
           == lax.broadcasted_iota(jnp.int32, (r, nseq * GLA_DK), 1) // GLA_DK)
    eye = (lax.broadcasted_iota(jnp.int32, (GLA_DK, GLA_DK), 0)
           == lax.broadcasted_iota(jnp.int32, (GLA_DK, GLA_DK), 1))[None]

    def spread(x):
        x2 = jnp.concatenate([x, x], axis=1)
        return jnp.where(own, jnp.concatenate([x2] * (nseq // 2), axis=1), 0.0)

    ys = []
    for h in range(GLA_HEADS):
        ks = slice(h * GLA_DK, (h + 1) * GLA_DK)
        vs = slice(h * GLA_DV, (h + 1) * GLA_DV)
        att = jnp.zeros((r, r), f32)
        for lev in range(nlev + 1):
            att = att + m_ref[lev] * _dot_nt(qf[lev][:, ks], kf[lev][:, ks])
        s_h = s0_ref[:, h]
        v_h = v[:, vs]
        o = _dot(spread(q_in[:, ks]), s_h.reshape(nseq * GLA_DK, GLA_DV)) + _dot(att, v_h)
        upd = _dot_tn(spread(k_st[:, ks]), v_h)
        dec_col = jnp.sum(jnp.where(eye, jnp.broadcast_to(dec3[:, :, ks], (nseq, GLA_DK, GLA_DK)), 0.0),
                          axis=2, keepdims=True)
        sout_ref[:, h] = s_h * dec_col + upd.reshape(nseq, GLA_DK, GLA_DV)
        o = o * lax.rsqrt(jnp.mean(o * o, axis=-1, keepdims=True) + RMS_EPS) * ng_ref[...]
        ys.append(o * _silu(gz[:, vs]))
    y_ref[...] = jnp.concatenate(ys, axis=1).reshape(nseq, t, BRANCH_W).astype(y_ref.dtype)


def _gla_step_call(proj, s0_all, layer, gw, nb, nseq, t, name):
    dbd, mbd = _gla_step_consts(t, nseq)
    kern = functools.partial(_gla_step_kernel, nseq=nseq, t=t)
    hk = GLA_HEADS * GLA_DK
    st_spec = pl.BlockSpec((nseq, GLA_HEADS, GLA_DK, GLA_DV), lambda b, ci: (b, 0, 0, 0))
    s0_spec = pl.BlockSpec((None, nseq, GLA_HEADS, GLA_DK, GLA_DV), lambda b, ci: (layer, b, 0, 0, 0))
    return pl.pallas_call(
        kern,
        grid=(nb, 1),
        in_specs=[_seg_spec(s, nseq, t) for s in _GLA_SEGS] + [
            s0_spec, _const_spec((LANE, hk)), _const_spec((1, hk)), _const_spec((1, GLA_DV)),
            _const_spec(dbd.shape), _const_spec(mbd.shape)],
        out_specs=[_tok_spec(nseq, t, BRANCH_W), st_spec],
        out_shape=[jax.ShapeDtypeStruct((nb * nseq, t, BRANCH_W), bf16),
                   jax.ShapeDtypeStruct((nb * nseq, GLA_HEADS, GLA_DK, GLA_DV), f32)],
        compiler_params=_params("parallel", "arbitrary"),
        name=name,
    )(*([proj] * len(_GLA_SEGS)), s0_all, gw["wa"], gw["ba"], gw["ng"], jnp.asarray(dbd), jnp.asarray(mbd))


def _rope_tables(pos0, t):
    half = ROT_DIM // 2
    dim = jnp.arange(LANE) % HEAD_DIM
    inv = ROPE_THETA ** (-(dim % half).astype(f32) / half)
    ang = (pos0 + jnp.arange(t)).astype(f32)[:, None] * inv[None, :]
    cos, sin = jnp.cos(ang), jnp.sin(ang)
    first, second = (dim < half)[None, :], ((dim >= half) & (dim < ROT_DIM))[None, :]
    c_tab = jnp.where(first | second, cos, 1.0)
    sa_tab = jnp.where(first, -sin, 0.0)
    sb_tab = jnp.where(second, sin, 0.0)
    return c_tab, sa_tab, sb_tab


def _rope(x, c_tab, sa_tab, sb_tab):
    wd = x.shape[-1]
    ax = x.ndim - 1
    rep = wd // LANE
    half = ROT_DIM // 2
    if rep > 1:
        c_tab, sa_tab, sb_tab = (jnp.concatenate([tb] * rep, axis=-1) for tb in (c_tab, sa_tab, sb_tab))
    return x * c_tab + pltpu.roll(x, wd - half, ax) * sa_tab + pltpu.roll(x, half, ax) * sb_tab


def _swa_kernel(sink_ref, sq_ref, sz_ref, sk_ref, sv_ref, ct_ref, sat_ref, sbt_ref, kp_ref, vp_ref,
                y_ref, klast_ref, vlast_ref, kprev_ref, vprev_ref, sp_ref, sc_ref, o_ref, *, nseq, qb, pos0):
    blk = pl.program_id(1)
    hd = HEAD_DIM

    @pl.when(blk == 0)
    def _():
        kprev_ref[...] = kp_ref[...]
        vprev_ref[...] = vp_ref[...]

    tabs = tuple(r[...][None] for r in (ct_ref, sat_ref, sbt_ref))
    q3 = _rope(sq_ref[...], *tabs) * (hd ** -0.5)
    k3 = _rope(sk_ref[...], *tabs)
    v3 = sv_ref[...]
    klast_ref[...] = k3
    vlast_ref[...] = v3

    for n in range(nseq):
        for kv in range(SWA_KV_HEADS):
            ds = slice(kv * hd, (kv + 1) * hd)
            qs = jnp.concatenate(
                [q3[n][:, (kv * SWA_GROUP + g) * hd:(kv * SWA_GROUP + g + 1) * hd] for g in range(SWA_GROUP)],
                axis=0).astype(bf16)
            sp_ref[n, kv] = _dot_nt(qs, kprev_ref[n][:, ds])
            sc_ref[n, kv] = _dot_nt(qs, k3[n][:, ds])

    mrows = SWA_GROUP * qb
    qi = lax.broadcasted_iota(jnp.int32, (mrows, WINDOW), 0) % qb
    kj = lax.broadcasted_iota(jnp.int32, (mrows, WINDOW), 1)
    past_ok = (kj >= qi) & (kj >= (WINDOW - pos0) - blk * qb)
    cur_ok = kj <= qi
    sink = jnp.stack([jnp.concatenate([jnp.full((qb, 1), sink_ref[kv * SWA_GROUP + g], f32)
                                       for g in range(SWA_GROUP)], axis=0) for kv in range(SWA_KV_HEADS)])[None]
    s_p = jnp.where(past_ok[None, None], sp_ref[...], -jnp.inf)
    s_c = jnp.where(cur_ok[None, None], sc_ref[...], -jnp.inf)
    m = jnp.maximum(jnp.maximum(jnp.max(s_p, axis=3, keepdims=True), jnp.max(s_c, axis=3, keepdims=True)), sink)
    p_p = jnp.exp(s_p - m)
    p_c = jnp.exp(s_c - m)
    den = jnp.sum(p_p, axis=3, keepdims=True) + jnp.sum(p_c, axis=3, keepdims=True) + jnp.exp(sink - m)
    p_p = p_p.astype(bf16)
    p_c = p_c.astype(bf16)
    for n in range(nseq):
        for kv in range(SWA_KV_HEADS):
            ds = slice(kv * hd, (kv + 1) * hd)
            o_ref[n, kv] = _dot(p_p[n, kv], vprev_ref[n][:, ds]) + _dot(p_c[n, kv], v3[n][:, ds])
    o = o_ref[...] / den
    outs = [o[:, j // SWA_GROUP, (j % SWA_GROUP) * qb:(j % SWA_GROUP + 1) * qb, :] for j in range(SWA_HEADS)]
    y_ref[...] = (jnp.concatenate(outs, axis=2) * _silu(sz_ref[...])).astype(y_ref.dtype)
    kprev_ref[...] = k3
    vprev_ref[...] = v3


_SWA_SEGS = ("sq", "sz", "sk", "sv")


def _swa_call(proj, sinks, k_past, v_past, pos0, nb, nseq, nc, qb, name):
    assert qb == WINDOW
    t_total = nc * qb
    c_tab, sa_tab, sb_tab = _rope_tables(pos0, t_total)
    kern = functools.partial(_swa_kernel, nseq=nseq, qb=qb, pos0=pos0)
    kvw = SWA_KV_HEADS * HEAD_DIM
    mrows = SWA_GROUP * qb
    tab_spec = pl.BlockSpec((qb, LANE), lambda b, c: (c, 0))
    past_spec = pl.BlockSpec((nseq, WINDOW, kvw), lambda b, c: (b, 0, 0))
    return pl.pallas_call(
        kern,
        grid=(nb, nc),
        in_specs=[pl.BlockSpec(memory_space=pltpu.SMEM)] + [_seg_spec(s, nseq, qb) for s in _SWA_SEGS] + [
            tab_spec, tab_spec, tab_spec, past_spec, past_spec],
        out_specs=[_tok_spec(nseq, qb, BRANCH_W), past_spec, past_spec],
        out_shape=[jax.ShapeDtypeStruct((nb * nseq, nc * qb, BRANCH_W), bf16),
                   jax.ShapeDtypeStruct((nb * nseq, WINDOW, kvw), f32),
                   jax.ShapeDtypeStruct((nb * nseq, WINDOW, kvw), f32)],
        scratch_shapes=[pltpu.VMEM((nseq, WINDOW, kvw), f32), pltpu.VMEM((nseq, WINDOW, kvw), f32),
                        pltpu.VMEM((nseq, SWA_KV_HEADS, mrows, WINDOW), f32),
                        pltpu.VMEM((nseq, SWA_KV_HEADS, mrows, qb), f32),
                        pltpu.VMEM((nseq, SWA_KV_HEADS, mrows, HEAD_DIM), f32)],
        compiler_params=_params("parallel", "arbitrary"),
        name=name,
    )(sinks, *([proj] * len(_SWA_SEGS)), c_tab, sa_tab, sb_tab, k_past, v_past)


def _swa_step_kernel(sink_ref, sq_ref, sz_ref, sk_ref, sv_ref, ct_ref, sat_ref, sbt_ref, kp_ref, vp_ref,
                     y_ref, klast_ref, vlast_ref, sp_ref, sc_ref, o_ref, *, nseq, t, pos0):
    hd = HEAD_DIM
    kvw = SWA_KV_HEADS * hd
    mrows = SWA_HEADS * t
    tabs = tuple(r[...][None] for r in (ct_ref, sat_ref, sbt_ref))
    q3 = _rope(sq_ref[...], *tabs) * (hd ** -0.5)
    k3 = _rope(sk_ref[...], *tabs)
    v3 = sv_ref[...]

    lane = lax.broadcasted_iota(jnp.int32, (kvw, WINDOW), 1)
    pad = jnp.zeros((WINDOW - t, kvw), f32)

    def shifted(old_t, new):
        new_t = jnp.concatenate([pad, new], axis=0).T
        out = jnp.where(lane >= WINDOW - t, new_t, pltpu.roll(old_t, WINDOW - t, 1))
        return out.reshape(SWA_KV_HEADS, hd, WINDOW)

    for n in range(nseq):
        klast_ref[n] = shifted(kp_ref[n].reshape(kvw, WINDOW), k3[n])
        vlast_ref[n] = shifted(vp_ref[n].reshape(kvw, WINDOW), v3[n])

    zero = jnp.zeros((nseq, t, hd), f32)
    pieces = []
    for j in range(SWA_HEADS):
        qj = q3[:, :, j * hd:(j + 1) * hd]
        pieces.append(jnp.concatenate([qj, zero] if j // SWA_GROUP == 0 else [zero, qj], axis=2))
    qbd = jnp.concatenate(pieces, axis=1).astype(bf16)

    for n in range(nseq):
        sp_ref[n] = _dot(qbd[n], kp_ref[n].reshape(kvw, WINDOW))
        sc_ref[n] = _dot_nt(qbd[n], k3[n])

    qi = lax.broadcasted_iota(jnp.int32, (mrows, WINDOW), 0) % t
    kj = lax.broadcasted_iota(jnp.int32, (mrows, WINDOW), 1)
    past_ok = kj >= qi
    if pos0 < WINDOW:
        past_ok = past_ok & (kj >= WINDOW - pos0)
    qi_c = lax.broadcasted_iota(jnp.int32, (mrows, t), 0) % t
    kj_c = lax.broadcasted_iota(jnp.int32, (mrows, t), 1)
    cur_ok = kj_c <= qi_c
    sink = jnp.concatenate([jnp.full((t, 1), sink_ref[j], f32) for j in range(SWA_HEADS)], axis=0)[None]
    s_p = jnp.where(past_ok[None], sp_ref[...], -jnp.inf)
    s_c = jnp.where(cur_ok[None], sc_ref[...], -jnp.inf)
    m = jnp.maximum(jnp.maximum(jnp.max(s_p, axis=2, keepdims=True), jnp.max(s_c, axis=2, keepdims=True)), sink)
    p_p = jnp.exp(s_p - m)
    p_c = jnp.exp(s_c - m)
    den = jnp.sum(p_p, axis=2, keepdims=True) + jnp.sum(p_c, axis=2, keepdims=True) + jnp.exp(sink - m)
    p_p = p_p.astype(bf16)
    p_c = p_c.astype(bf16)
    for n in range(nseq):
        o_ref[n] = _dot_nt(p_p[n], vp_ref[n].reshape(kvw, WINDOW)) + _dot(p_c[n], v3[n])
    o = o_ref[...] / den
    outs = []
    for j in range(SWA_HEADS):
        kv = j // SWA_GROUP
        outs.append(o[:, j * t:(j + 1) * t, kv * hd:(kv + 1) * hd])
    y_ref[...] = (jnp.concatenate(outs, axis=2) * _silu(sz_ref[...])).astype(y_ref.dtype)


def _swa_step_call(proj, sinks, k_past_t, v_past_t, layer, pos0, nb, nseq, t, name):
    c_tab, sa_tab, sb_tab = _rope_tables(pos0, t)
    kern = functools.partial(_swa_step_kernel, nseq=nseq, t=t, pos0=pos0)
    kvw = SWA_KV_HEADS * HEAD_DIM
    mrows = SWA_HEADS * t
    tab_spec = pl.BlockSpec((t, LANE), lambda b, c: (0, 0))
    past_spec = pl.BlockSpec((None, nseq, SWA_KV_HEADS, HEAD_DIM, WINDOW), lambda b, c: (layer, b, 0, 0, 0))
    new_spec = pl.BlockSpec((nseq, SWA_KV_HEADS, HEAD_DIM, WINDOW), lambda b, c: (b, 0, 0, 0))
    new_shape = jax.ShapeDtypeStruct((nb * nseq, SWA_KV_HEADS, HEAD_DIM, WINDOW), f32)
    return pl.pallas_call(
        kern,
        grid=(nb, 1),
        in_specs=[pl.BlockSpec(memory_space=pltpu.SMEM)] + [_seg_spec(s, nseq, t) for s in _SWA_SEGS] + [
            tab_spec, tab_spec, tab_spec, past_spec, past_spec],
        out_specs=[_tok_spec(nseq, t, BRANCH_W), new_spec, new_spec],
        out_shape=[jax.ShapeDtypeStruct((nb * nseq, t, BRANCH_W), bf16), new_shape, new_shape],
        scratch_shapes=[pltpu.VMEM((nseq, mrows, WINDOW), f32), pltpu.VMEM((nseq, mrows, t), f32),
                        pltpu.VMEM((nseq, mrows, kvw), f32)],
        compiler_params=_params("parallel", "arbitrary"),
        name=name,
    )(sinks, *([proj] * len(_SWA_SEGS)), c_tab, sa_tab, sb_tab, k_past_t, v_past_t)


def _sgu_kernel(x_ref, wu_ref, wv_ref, wz_ref, g_ref, b_ref, wm_ref, bias_ref, y_ref, *rest, ntile, want_vn):
    vn_ref = rest[0] if want_vn else None
    su_ref, sv_ref, sz_ref = rest[-3:]
    _project(x_ref, ((wu_ref, (su_ref,)), (wv_ref, (sv_ref,)), (wz_ref, (sz_ref,))))
    vn = _layer_norm(sv_ref[...], g_ref[...], b_ref[...])
    if want_vn:
        vn_ref[...] = vn
    for r in range(ntile):
        rows = slice(r * SGU_CHUNK, (r + 1) * SGU_CHUNK)
        mixed = jnp.concatenate(
            [jnp.dot(wm_ref[g], vn[rows, g * SGU_GC:(g + 1) * SGU_GC].astype(bf16), preferred_element_type=f32)
             for g in range(SGU_GROUPS)], axis=1)
        y = su_ref[rows, :] * (mixed + bias_ref[...]) * _silu(sz_ref[rows, :])
        y_ref[rows, :] = y.astype(y_ref.dtype)


def _sgu_call(xb, w_t, layer, ln_g, ln_b, wmix, bias, n_tok, ntile, want_vn, name):
    rows = ntile * SGU_CHUNK
    w = BRANCH_W
    kern = functools.partial(_sgu_kernel, ntile=ntile, want_vn=want_vn)

    def const(shape):
        return pl.BlockSpec(shape, lambda i: (0,) * len(shape))

    out_specs = [pl.BlockSpec((rows, w), lambda i: (i, 0))]
    out_shape = [jax.ShapeDtypeStruct((n_tok, w), bf16)]
    if want_vn:
        out_specs.append(pl.BlockSpec((rows, w), lambda i: (i, 0)))
        out_shape.append(jax.ShapeDtypeStruct((n_tok, w), f32))
    return pl.pallas_call(
        kern,
        grid=(n_tok // rows,),
        in_specs=[pl.BlockSpec((rows, D_MODEL), lambda i: (i, 0)), _w_spec("su", "su", layer),
                  _w_spec("svv", "svv", layer), _w_spec("suz", "suz", layer), const((1, w)), const((1, w)),
                  const((SGU_GROUPS, SGU_CHUNK, SGU_CHUNK)), const((SGU_CHUNK, w))],
        out_specs=out_specs,
        out_shape=out_shape,
        scratch_shapes=[_seg_scratch(s, rows) for s in ("su", "svv", "suz")],
        compiler_params=_params("parallel"),
        name=name,
    )(xb, w_t, w_t, w_t, ln_g, ln_b, wmix, bias)


def _mem_kernel(x_ref, wm_ref, mk_ref, mv_ref, y_ref, mq_ref, mz_ref, *, nseq, tq):
    _project(x_ref, ((wm_ref, (mq_ref, mz_ref)),))
    for n in range(nseq):
        q = mq_ref[n] * (HEAD_DIM ** -0.5)
        mk = mk_ref[n]
        mv = mv_ref[n]
        outs = []
        for h in range(MEM_HEADS):
            ds = slice(h * HEAD_DIM, (h + 1) * HEAD_DIM)
            s = _dot_nt(q[:, ds], mk[:, ds])
            m = jnp.max(s, axis=1, keepdims=True)
            p = jnp.exp(s - m)
            den = jnp.sum(p, axis=1, keepdims=True)
            outs.append(_dot(p, mv[:, ds]) / den)
        o = jnp.concatenate(outs, axis=1)
        y_ref[n] = (o * _silu(mz_ref[n])).astype(y_ref.dtype)


def _mem_call(xb, w_t, layer, mk, mv, nb, nseq, nc, tq, name):
    kern = functools.partial(_mem_kernel, nseq=nseq, tq=tq)
    kv_spec = pl.BlockSpec((nseq, N_MEM, MEM_W), lambda b, c: (b, 0, 0))
    return pl.pallas_call(
        kern,
        grid=(nb, nc),
        in_specs=[_tok_spec(nseq, tq, D_MODEL), _w_spec("mq", "mz", layer), kv_spec, kv_spec],
        out_specs=_tok_spec(nseq, tq, MEM_W),
        out_shape=jax.ShapeDtypeStruct((nb * nseq, nc * tq, MEM_W), bf16),
        scratch_shapes=[_seg_scratch("mq", nseq, tq), _seg_scratch("mz", nseq, tq)],
        compiler_params=_params("parallel", "arbitrary"),
        name=name,
    )(xb, w_t, mk, mv)


def _mem_step_kernel(x_ref, wm_ref, mk_ref, mv_ref, y_ref, s_ref, o_ref, mq_ref, mz_ref, *, nseq, t):
    mrows = MEM_HEADS * t
    _project(x_ref, ((wm_ref, (mq_ref, mz_ref)),))
    row_head = lax.broadcasted_iota(jnp.int32, (mrows, MEM_W), 0) // t
    lane_head = lax.broadcasted_iota(jnp.int32, (mrows, MEM_W), 1) // HEAD_DIM
    own = (row_head == lane_head)[None]
    q3 = mq_ref[...] * (HEAD_DIM ** -0.5)
    qbd = jnp.where(own, jnp.concatenate([q3] * MEM_HEADS, axis=1), 0.0).astype(bf16)
    for n in range(nseq):
        s_ref[n] = _dot(qbd[n], mk_ref[n].reshape(MEM_W, N_MEM))
    s = s_ref[...]
    p = jnp.exp(s - jnp.max(s, axis=2, keepdims=True))
    den = jnp.sum(p, axis=2, keepdims=True)
    p = p.astype(bf16)
    for n in range(nseq):
        o_ref[n] = _dot_nt(p[n], mv_ref[n].reshape(MEM_W, N_MEM))
    o = jnp.where(own, o_ref[...] / den, 0.0)
    acc = o[:, 0:t, :]
    for h in range(1, MEM_HEADS):
        acc = acc + o[:, h * t:(h + 1) * t, :]
    y_ref[...] = (acc * _silu(mz_ref[...])).astype(y_ref.dtype)


def _mem_step_call(xb, w_t, mk_t, mv_t, layer, nb, nseq, t, name):
    kern = functools.partial(_mem_step_kernel, nseq=nseq, t=t)
    kv_spec = pl.BlockSpec((None, nseq, MEM_HEADS, HEAD_DIM, N_MEM), lambda b, c: (layer, b, 0, 0, 0))
    mrows = MEM_HEADS * t
    return pl.pallas_call(
        kern,
        grid=(nb, 1),
        in_specs=[_tok_spec(nseq, t, D_MODEL), _w_spec("mq", "mz", layer), kv_spec, kv_spec],
        out_specs=_tok_spec(nseq, t, MEM_W),
        out_shape=jax.ShapeDtypeStruct((nb * nseq, t, MEM_W), bf16),
        scratch_shapes=[pltpu.VMEM((nseq, mrows, N_MEM), f32), pltpu.VMEM((nseq, mrows, MEM_W), f32),
                        _seg_scratch("mq", nseq, t), _seg_scratch("mz", nseq, t)],
        compiler_params=_params("parallel", "arbitrary"),
        name=name,
    )(xb, w_t, mk_t, mv_t)


def _merge_kernel(yg_ref, yl_ref, ys_ref, yu_ref, ym_ref, x_ref, wg_ref, wb_ref, wm_ref, wo_ref, g_ref, b_ref,
                  o_ref, ob_ref):
    d = D_MODEL
    x = x_ref[...]
    xb = x.astype(bf16)

    def gate(n):
        return _sigmoid(_dot_nt(xb, wg_ref[n * d:(n + 1) * d, :]))

    merged = gate(4) * jnp.dot(ym_ref[...], wm_ref[...], preferred_element_type=f32)
    for n, y_ref in enumerate((yg_ref, yl_ref, ys_ref, yu_ref)):
        merged = merged + gate(n) * jnp.dot(y_ref[...], wb_ref[n], preferred_element_type=f32)
    out = _dot(merged, wo_ref[...])
    y = _layer_norm(DN_ALPHA * x + out, g_ref[...], b_ref[...])
    o_ref[...] = y
    ob_ref[...] = y.astype(bf16)


def _merge_call(ys, x, mw, layer, tm, name):
    n_tok, d = x.shape
    w = BRANCH_W

    def rows(width):
        return pl.BlockSpec((tm, width), lambda i: (i, 0))

    def const(shape):
        return pl.BlockSpec(shape, lambda i: (0,) * len(shape), pipeline_mode=pl.Buffered(1))

    gates_spec = pl.BlockSpec((None, 5 * d, d), lambda i: (layer, 0, 0), pipeline_mode=pl.Buffered(1))
    return pl.pallas_call(
        _merge_kernel,
        grid=(n_tok // tm,),
        in_specs=[rows(w), rows(w), rows(w), rows(w), rows(MEM_W), rows(d),
                  gates_spec, const((4, w, d)), const((MEM_W, d)), const((d, d)), const((1, d)),
                  const((1, d))],
        out_specs=[rows(d), rows(d)],
        out_shape=[jax.ShapeDtypeStruct((n_tok, d), f32), jax.ShapeDtypeStruct((n_tok, d), bf16)],
        compiler_params=_params("parallel"),
        name=name,
    )(*ys, x, mw["wg"], mw["wb"], mw["wm"], mw["wo"], mw["g"], mw["b"])


def _prep_w_in(w_in):
    w_t = jnp.swapaxes(w_in, 1, 2).astype(bf16)
    runs = []
    used = 0
    for run in _PROJ_RUNS:
        if run is None:
            runs.append(jnp.zeros((DEPTH, N_PROJ - used, D_MODEL), bf16))
            used = N_PROJ
            continue
        lo = _ORIG_OFF[run[0]][0]
        hi = _ORIG_OFF[run[1]][0] + _ORIG_OFF[run[1]][1]
        runs.append(w_t[:, lo:hi])
        used += hi - lo
    assert used == N_WROWS
    return jnp.concatenate(runs, axis=1), w_t[:, _ORIG_OFF["gates"][0]:]


def _prep_layer(l, w_proj, w_gates, gla_wa2, gla_ba, gla_norm_g, lru_conv_w, lru_conv_b, lru_wr, lru_br, lru_wi,
                lru_bi, lru_L, swa_sinks, sgu_ln_g, sgu_ln_b, sgu_w, sgu_b, w_mem_kv, w_branch, w_branch_mem, w_out,
                ln_g, ln_b):
    d = D_MODEL
    w = BRANCH_W

    def block_diag(wb):
        eye = jnp.eye(LRU_BLOCKS, dtype=f32)
        return (eye[:, None, :, None] * wb[:, :, None, :]).reshape(w, w).astype(bf16)

    tril = jnp.tril(jnp.ones((SGU_CHUNK, SGU_CHUNK), f32))
    wmix_p = (sgu_w[l] * tril).astype(bf16)
    bias_p = jnp.repeat(sgu_b[l].T, SGU_GC, axis=1)
    t8 = SUBLANE
    rep = SGU_CHUNK // t8
    w8 = (sgu_w[l] * tril)[:, :t8, :t8]
    seq_eye = jnp.eye(rep, dtype=f32)
    wmix_s = (seq_eye[None, :, None, :, None] * w8[:, None, :, None, :]).reshape(
        SGU_GROUPS, SGU_CHUNK, SGU_CHUNK).astype(bf16)
    bias_s = jnp.tile(bias_p[:t8], (rep, 1))
    return dict(
        w_proj=w_proj,
        w_mem_kv=w_mem_kv[l].astype(bf16),
        gla=dict(wa=jnp.pad(gla_wa2[l], ((0, LANE - GLA_RANK), (0, 0))).astype(bf16),
                 ba=gla_ba[l].reshape(1, -1), ng=gla_norm_g[l].reshape(1, -1)),
        lru=dict(conv_w=lru_conv_w[l], conv_b=lru_conv_b[l].reshape(1, w), wr=block_diag(lru_wr[l]),
                 br=lru_br[l].reshape(1, w), wi=block_diag(lru_wi[l]), bi=lru_bi[l].reshape(1, w),
                 lam=lru_L[l].reshape(1, w)),
        sinks=swa_sinks[l],
        sgu=dict(g=sgu_ln_g[l].reshape(1, w), b=sgu_ln_b[l].reshape(1, w), wmix_p=wmix_p, bias_p=bias_p,
                 wmix_s=wmix_s, bias_s=bias_s),
        merge=dict(wg=w_gates, wb=w_branch[l].astype(bf16), wm=w_branch_mem[l].astype(bf16), wo=w_out[l].astype(bf16),
                   g=ln_g[l].reshape(1, d), b=ln_b[l].reshape(1, d)),
    )


def _layer(x, lw, grp, st, layer, tag):
    nseq_total, t = grp["batch"], grp["seq"]
    n_tok = nseq_total * t
    x, xb = x
    xb3 = xb.reshape(nseq_total, t, D_MODEL)
    w_t = lw["w_proj"]
    proj = _matmul_call(xb, w_t, min(grp["proj_tm"], n_tok), 1024, "proj_" + tag, w_transposed=True, n_out=N_PROJ,
                        layer=layer)
    proj3 = proj.reshape(nseq_total, t, N_PROJ)

    lt = grp["lru"]
    y_lru, hlast, hist = _lru_call(proj3, st["hist0"], st["h0"], lw["lru"], nseq_total // lt[0], lt[0], t // lt[1],
                                   lt[1], "lru_" + tag)
    short = grp["kind"] == "s"
    gt = grp["gla"]
    if short:
        y_gla, s_out = _gla_step_call(proj3, st["gla0"], layer, lw["gla"], nseq_total // gt[0], gt[0], t,
                                      "gla_" + tag)
    else:
        y_gla, s_out = _gla_call(proj3, st["gla0"], lw["gla"], nseq_total // gt[0], gt[0], t // gt[1], gt[1],
                                 "gla_" + tag)
    wt = grp["swa"]
    if short:
        y_swa, k_last, v_last = _swa_step_call(proj3, lw["sinks"], st["k_past"], st["v_past"], layer, grp["pos0"],
                                               nseq_total // wt[0], wt[0], t, "swa_" + tag)
    else:
        y_swa, k_last, v_last = _swa_call(proj3, lw["sinks"], st["k_past"], st["v_past"], grp["pos0"],
                                          nseq_total // wt[0], wt[0], t // wt[1], wt[1], "swa_" + tag)
    sg = lw["sgu"]
    sgu_out = _sgu_call(xb, w_t, layer, sg["g"], sg["b"], sg["wmix_" + grp["kind"]], sg["bias_" + grp["kind"]], n_tok,
                        grp["sgu_tiles"], grp["kind"] == "s", "sgu_" + tag)
    mt = grp["mem"]
    if short:
        y_mem = _mem_step_call(xb3, w_t, st["mk"], st["mv"], layer, nseq_total // mt[0], mt[0], t, "mem_" + tag)
    else:
        y_mem = _mem_call(xb3, w_t, layer, st["mk"], st["mv"], nseq_total // mt[0], mt[0], t // mt[1], mt[1],
                          "mem_" + tag)
    ys = tuple(y.reshape(n_tok, y.shape[-1]) for y in (y_gla, y_lru, y_swa, sgu_out[0], y_mem))
    x_new = _merge_call(ys, x, lw["merge"], layer, min(256, n_tok), "merge_" + tag)
    return x_new, dict(gla=s_out, hlast=hlast, hist=hist, k_last=k_last, v_last=v_last,
                       vn=sgu_out[1] if len(sgu_out) > 1 else None)


_PROMPT = dict(kind="p", pos0=0, proj_tm=2048, lru=(1, 256), gla=(2, 128), swa=(2, 128), sgu_tiles=4, mem=(1, 512))
_SAMPLE = dict(kind="s", pos0=PAST_LEN, proj_tm=1024, lru=(32, 8), gla=(16, 8), swa=(16, 8), sgu_tiles=8,
               mem=(16, 8))


def kernel(x_prompt, x_sample, mem_prompt, state_gla, state_lru_h, state_lru_conv, cache_swa_k, cache_swa_v,
           cache_mem_k, cache_mem_v, ln_in_g, ln_in_b, w_in, gla_wa2, gla_ba, gla_norm_g, lru_conv_w, lru_conv_b,
           lru_wr, lru_br, lru_wi, lru_bi, lru_L, swa_sinks, sgu_ln_g, sgu_ln_b, sgu_w, sgu_b, w_mem_kv, w_branch,
           w_branch_mem, w_out, ln_g, ln_b):
    bp, tp, d = x_prompt.shape
    bs, ts, _ = x_sample.shape
    w = BRANCH_W
    kvw = SWA_KV_HEADS * HEAD_DIM
    gp = dict(_PROMPT, batch=bp, seq=tp)
    gs = dict(_SAMPLE, batch=bs, seq=ts)

    xp = _ln_call(x_prompt.reshape(bp * tp, d), ln_in_g, ln_in_b)
    xs = _ln_call(x_sample.reshape(bs * ts, d), ln_in_g, ln_in_b)
    mem2 = mem_prompt.reshape(bp * N_MEM, d)

    swa_k_t, swa_v_t, mem_k_t, mem_v_t = (jnp.transpose(c, (0, 1, 3, 4, 2))
                                          for c in (cache_swa_k, cache_swa_v, cache_mem_k, cache_mem_v))

    w_proj, w_gates = _prep_w_in(w_in)
    outs_p, outs_s, mks, mvs = [], [], [], []
    for l in range(DEPTH):
        lw = _prep_layer(l, w_proj, w_gates, gla_wa2, gla_ba, gla_norm_g, lru_conv_w, lru_conv_b, lru_wr, lru_br,
                         lru_wi, lru_bi, lru_L, swa_sinks, sgu_ln_g, sgu_ln_b, sgu_w, sgu_b, w_mem_kv, w_branch,
                         w_branch_mem, w_out, ln_g, ln_b)
        mkv = _matmul_call(mem2, lw["w_mem_kv"], bp * N_MEM, 2 * MEM_W, "memkv_%d" % l)
        mk = mkv[:, :MEM_W].reshape(bp, N_MEM, MEM_W)
        mv = mkv[:, MEM_W:].reshape(bp, N_MEM, MEM_W)
        st_p = dict(hist0=jnp.zeros((bp, SUBLANE, w), f32), h0=jnp.zeros((bp, 1, w), f32),
                    gla0=jnp.zeros((bp, GLA_HEADS, GLA_DK, GLA_DV), f32),
                    k_past=jnp.zeros((bp, WINDOW, kvw), f32), v_past=jnp.zeros((bp, WINDOW, kvw), f32),
                    mk=mk, mv=mv)
        st_s = dict(hist0=jnp.pad(state_lru_conv[l], ((0, 0), (SUBLANE - (CONV_W - 1), 0), (0, 0))),
                    h0=state_lru_h[l][:, None, :], gla0=state_gla,
                    k_past=swa_k_t, v_past=swa_v_t, mk=mem_k_t, mv=mem_v_t)
        xp, op = _layer(xp, lw, gp, st_p, l, "p%d" % l)
        xs, os_ = _layer(xs, lw, gs, st_s, l, "s%d" % l)
        outs_p.append(op)
        outs_s.append(os_)
        mks.append(mk.reshape(bp, N_MEM, MEM_HEADS, HEAD_DIM))
        mvs.append(mv.reshape(bp, N_MEM, MEM_HEADS, HEAD_DIM))

    def stack(outs, fn):
        return jnp.stack([fn(o) for o in outs])

    def window(a):
        return a.reshape(a.shape[0], WINDOW, SWA_KV_HEADS, HEAD_DIM)

    def window_t(a):
        return jnp.transpose(a, (0, 3, 1, 2))

    return (
        xp[0].reshape(bp, tp, d), xs[0].reshape(bs, ts, d),
        stack(outs_p, lambda o: o["gla"]), stack(outs_s, lambda o: o["gla"]),
        stack(outs_p, lambda o: o["hlast"][:, SUBLANE - 1]), stack(outs_s, lambda o: o["hlast"][:, SUBLANE - 1]),
        stack(outs_p, lambda o: o["hist"][:, SUBLANE - (CONV_W - 1):]),
        stack(outs_s, lambda o: o["hist"][:, SUBLANE - (CONV_W - 1):]),
        stack(outs_p, lambda o: window(o["k_last"])), stack(outs_s, lambda o: window_t(o["k_last"])),
        stack(outs_p, lambda o: window(o["v_last"])), stack(outs_s, lambda o: window_t(o["v_last"])),
        jnp.stack(mks), jnp.stack(mvs),
        stack(outs_s, lambda o: o["vn"].reshape(bs, ts, w)),
    )
```

```python
import functools
import math

import jax
import jax.numpy as jnp
import numpy as np
from jax import lax
from jax.experimental import pallas as pl
from jax.experimental.pallas import tpu as pltpu

f32 = jnp.float32
bf16 = jnp.bfloat16

D_MODEL = 1024
DEPTH = 2
PAST_LEN = 8192
BRANCH_W = 512
GLA_HEADS = 4
GLA_DK = 64
GLA_DV = 128
GLA_RANK = 16
GLA_TAU = 16.0
LRU_BLOCKS = 8
LRU_BS = 64
CONV_W = 4
LRU_C = 8.0
HEAD_DIM = 64
SWA_HEADS = 8
SWA_KV_HEADS = 2
SWA_GROUP = 4
WINDOW = 128
ROT_DIM = 16
ROPE_THETA = 500000.0
SGU_GROUPS = 4
SGU_GC = 128
SGU_CHUNK = 128
N_MEM = 256
MEM_HEADS = 4
MEM_W = 256
LN_EPS = 1e-5
RMS_EPS = 1e-6
DN_ALPHA = (2 * DEPTH) ** 0.25

LANE = 128
SUBLANE = 8

_ORIG = (("gq", 256), ("gk", 256), ("gv", 512), ("glr", 16), ("gz", 512), ("lx", 512), ("lz", 512), ("sq", 512),
         ("sk", 128), ("sv", 128), ("sz", 512), ("su", 512), ("svv", 512), ("suz", 512), ("mq", 256), ("mz", 256),
         ("gates", 5 * D_MODEL))
_ORIG_OFF = {}
_off = 0
for _n, _w in _ORIG:
    _ORIG_OFF[_n] = (_off, _w)
    _off += _w
_PROJ_RUNS = (("gq", "gv"), ("gz", "sq"), ("sz", "sz"), ("sk", "sv"), ("glr", "glr"), None, ("su", "mz"))
_SEG = {}
_off = 0
for _run in _PROJ_RUNS:
    if _run is None:
        N_PROJ = -(-_off // 1024) * 1024
        _off = N_PROJ
        continue
    _names = [n for n, _ in _ORIG]
    for _n in _names[_names.index(_run[0]):_names.index(_run[1]) + 1]:
        _w = max(_ORIG_OFF[_n][1], 128)
        assert _off % _w == 0
        _SEG[_n] = (_off, _w)
        _off += _w
N_WROWS = _off


def _dot(a, b):
    return jnp.dot(a.astype(bf16), b.astype(bf16), preferred_element_type=f32)


def _dot_nt(a, b):
    return lax.dot_general(a.astype(bf16), b.astype(bf16), (((1,), (1,)), ((), ())), preferred_element_type=f32)


def _dot_tn(a, b):
    return lax.dot_general(a.astype(bf16), b.astype(bf16), (((0,), (0,)), ((), ())), preferred_element_type=f32)


def _sigmoid(x):
    return 1.0 / (1.0 + jnp.exp(-x))


def _silu(x):
    return x * _sigmoid(x)


def _log_sigmoid(x):
    return jnp.minimum(x, 0.0) - jnp.log(1.0 + jnp.exp(-jnp.abs(x)))


def _layer_norm(x, g, b):
    mu = jnp.mean(x, axis=-1, keepdims=True)
    xc = x - mu
    var = jnp.mean(xc * xc, axis=-1, keepdims=True)
    return xc * lax.rsqrt(var + LN_EPS) * g + b


def _params(*sem):
    return pltpu.CompilerParams(dimension_semantics=sem)


def _ln_kernel(x_ref, g_ref, b_ref, o_ref, ob_ref):
    y = _layer_norm(x_ref[...], g_ref[...], b_ref[...])
    o_ref[...] = y
    ob_ref[...] = y.astype(bf16)


def _ln_call(x, g, b, tm=512):
    n, d = x.shape
    tm = min(tm, n)
    return pl.pallas_call(
        _ln_kernel,
        grid=(n // tm,),
        in_specs=[pl.BlockSpec((tm, d), lambda i: (i, 0)), pl.BlockSpec((1, d), lambda i: (0, 0)),
                  pl.BlockSpec((1, d), lambda i: (0, 0))],
        out_specs=[pl.BlockSpec((tm, d), lambda i: (i, 0)), pl.BlockSpec((tm, d), lambda i: (i, 0))],
        out_shape=[jax.ShapeDtypeStruct((n, d), f32), jax.ShapeDtypeStruct((n, d), bf16)],
        compiler_params=_params("parallel"),
        name="ln_in",
    )(x, g.reshape(1, d), b.reshape(1, d))


def _matmul_kernel(x_ref, w_ref, o_ref, *, w_transposed):
    o_ref[...] = (_dot_nt if w_transposed else _dot)(x_ref[...], w_ref[...])


def _matmul_call(x, w, tm, tn, name, w_transposed=False, n_out=None, layer=None):
    m, k = x.shape
    n = n_out or (w.shape[-2] if w_transposed else w.shape[1])
    if layer is not None:
        w_spec = pl.BlockSpec((None, tn, k), lambda i, j: (layer, j, 0))
    elif w_transposed:
        w_spec = pl.BlockSpec((tn, k), lambda i, j: (j, 0))
    else:
        w_spec = pl.BlockSpec((k, tn), lambda i, j: (0, j))
    return pl.pallas_call(
        functools.partial(_matmul_kernel, w_transposed=w_transposed),
        grid=(m // tm, n // tn),
        in_specs=[pl.BlockSpec((tm, k), lambda i, j: (i, 0)), w_spec],
        out_specs=pl.BlockSpec((tm, tn), lambda i, j: (i, j)),
        out_shape=jax.ShapeDtypeStruct((m, n), f32),
        compiler_params=_params("parallel", "arbitrary"),
        name=name,
    )(x, w)


def _seg_spec(name, nseq, rows):
    off, width = _SEG[name]
    assert off + width <= N_PROJ
    cb = off // width
    return pl.BlockSpec((nseq, rows, width), lambda b, c: (b, c, cb))


def _tok_spec(nseq, rows, width):
    return pl.BlockSpec((nseq, rows, width), lambda b, c: (b, c, 0))


def _w_spec(first, last, layer):
    off = _SEG[first][0]
    rows = _SEG[last][0] + _SEG[last][1] - off
    assert off % rows == 0
    return pl.BlockSpec((None, rows, D_MODEL), lambda *_: (layer, off // rows, 0))


def _seg_scratch(name, *lead):
    return pltpu.VMEM((*lead, _SEG[name][1]), f32)


def _project(x_ref, pairs):
    x2 = x_ref[...].reshape(-1, D_MODEL)
    for w_ref, seg_refs in pairs:
        p = lax.dot_general(x2, w_ref[...], (((1,), (1,)), ((), ())), preferred_element_type=f32)
        off = 0
        for s_ref in seg_refs:
            width = s_ref.shape[-1]
            s_ref[...] = p[:, off:off + width].reshape(s_ref.shape)
            off += width


def _const_spec(shape):
    nd = len(shape)
    return pl.BlockSpec(shape, lambda b, c: (0,) * nd)


def _lru_kernel(lx_ref, lz_ref, hist0_ref, h0_ref, cw_ref, cb_ref, wr_ref, br_ref, wi_ref, bi_ref, lam_ref,
                y_ref, hlast_ref, hist_out_ref, hist_ref, hc_ref, *, nseq, tc):
    c = pl.program_id(1)
    w = BRANCH_W

    @pl.when(c == 0)
    def _():
        hist_ref[...] = hist0_ref[...]
        hc_ref[...] = h0_ref[...]

    x = lx_ref[...]
    xfull = jnp.concatenate([hist_ref[...], x], axis=1)

    def tap(j):
        return cw_ref[j:j + 1, :].reshape(1, 1, w)

    y = cb_ref[...].reshape(1, 1, w) + x * tap(CONV_W - 1)
    for s in range(1, CONV_W):
        y = y + pltpu.roll(xfull, s, 1)[:, SUBLANE:, :] * tap(CONV_W - 1 - s)
    hist_ref[...] = xfull[:, tc:, :]
    hist_out_ref[...] = xfull[:, tc:, :]

    xc = y.reshape(nseq * tc, w)
    r = _sigmoid(_dot(xc, wr_ref[...]) + br_ref[...])
    i = _sigmoid(_dot(xc, wi_ref[...]) + bi_ref[...])
    log_a = (LRU_C * r) * _log_sigmoid(lam_ref[...])
    a = jnp.exp(log_a)
    u = jnp.sqrt(jnp.tanh(-log_a) * (a * a + 1.0)) * (i * xc)

    acc_a = a.reshape(nseq, tc, w)
    acc_u = u.reshape(nseq, tc, w)
    t = lax.broadcasted_iota(jnp.int32, (nseq, tc, w), 1)
    d = 1
    while d < tc:
        if d % SUBLANE:
            ok = t >= d
            a_sh = jnp.where(ok, pltpu.roll(acc_a, d, 1), 1.0)
            u_sh = jnp.where(ok, pltpu.roll(acc_u, d, 1), 0.0)
            acc_u = acc_a * u_sh + acc_u
            acc_a = acc_a * a_sh
        else:
            new_u = acc_a[:, d:, :] * acc_u[:, :tc - d, :] + acc_u[:, d:, :]
            new_a = acc_a[:, d:, :] * acc_a[:, :tc - d, :]
            acc_u = jnp.concatenate([acc_u[:, :d, :], new_u], axis=1)
            acc_a = jnp.concatenate([acc_a[:, :d, :], new_a], axis=1)
        d *= 2
    h = acc_a * hc_ref[...] + acc_u
    hc_ref[...] = h[:, tc - 1:tc, :]
    hlast_ref[...] = h[:, tc - SUBLANE:, :]
    y_ref[...] = (h * _silu(lz_ref[...])).astype(y_ref.dtype)


def _lru_call(proj, hist0, h0, lw, nb, nseq, nc, tc, name):
    w = BRANCH_W
    kern = functools.partial(_lru_kernel, nseq=nseq, tc=tc)
    return pl.pallas_call(
        kern,
        grid=(nb, nc),
        in_specs=[_seg_spec("lx", nseq, tc), _seg_spec("lz", nseq, tc),
                  pl.BlockSpec((nseq, SUBLANE, w), lambda b, c: (b, 0, 0)),
                  pl.BlockSpec((nseq, 1, w), lambda b, c: (b, 0, 0)),
                  _const_spec((CONV_W, w)), _const_spec((1, w)), _const_spec((w, w)), _const_spec((1, w)),
                  _const_spec((w, w)), _const_spec((1, w)), _const_spec((1, w))],
        out_specs=[_tok_spec(nseq, tc, w),
                   pl.BlockSpec((nseq, SUBLANE, w), lambda b, c: (b, 0, 0)),
                   pl.BlockSpec((nseq, SUBLANE, w), lambda b, c: (b, 0, 0))],
        out_shape=[jax.ShapeDtypeStruct((nb * nseq, nc * tc, w), bf16),
                   jax.ShapeDtypeStruct((nb * nseq, SUBLANE, w), f32),
                   jax.ShapeDtypeStruct((nb * nseq, SUBLANE, w), f32)],
        scratch_shapes=[pltpu.VMEM((nseq, SUBLANE, w), f32), pltpu.VMEM((nseq, 1, w), f32)],
        compiler_params=_params("parallel", "arbitrary"),
        name=name,
    )(proj, proj, hist0, h0, lw["conv_w"], lw["conv_b"], lw["wr"], lw["br"], lw["wi"], lw["bi"], lw["lam"])


def _gla_consts(c):
    t = np.arange(c)[:, None]
    u = np.arange(c)[None, :]
    blocks = [u <= t, u > t]
    masks = [t == u]
    m = 1
    while m < c:
        t0 = (t // m) * m
        odd = (t // m) % 2 == 1
        blocks.append(odd & (u >= t0) & (u <= t))
        blocks.append((~odd) & (u > t) & (u <= t0 + m - 1))
        masks.append((t // (2 * m) == u // (2 * m)) & odd & ((u // m) % 2 == 0))
        m *= 2
    return (np.concatenate(blocks, 0).astype(np.float32), np.stack(masks).astype(np.float32))


def _gla_kernel(gq_ref, gk_ref, gv_ref, gz_ref, glr_ref, s0_ref, wa_ref, ba_ref, ng_ref, d_ref, m_ref,
                y_ref, sout_ref, s_ref, att_ref, o_ref, upd_ref, *, nseq, c):
    ci = pl.program_id(1)
    nlev = int(math.log2(c))
    hk = GLA_HEADS * GLA_DK
    heads = range(GLA_HEADS)

    @pl.when(ci == 0)
    def _():
        s_ref[...] = s0_ref[...]

    def ks(h):
        return slice(h * GLA_DK, (h + 1) * GLA_DK)

    def vs(h):
        return slice(h * GLA_DV, (h + 1) * GLA_DV)

    z = _dot(glr_ref[...].reshape(nseq * c, LANE), wa_ref[...]) + ba_ref[...]
    la = _log_sigmoid(z) * (1.0 / GLA_TAU)
    hi = la.astype(bf16)
    r1 = la - hi.astype(f32)
    mid = r1.astype(bf16)
    lo = (r1 - mid.astype(f32)).astype(bf16)
    hml = jnp.concatenate([hi, mid, lo], axis=1)

    q_in, k_st, dec_rows, qf, kf = [], [], [], [], []
    for n in range(nseq):
        hml_n = hml[n * c:(n + 1) * c]

        def sums(blk):
            p = jnp.dot(d_ref[blk * c:(blk + 1) * c, :].astype(bf16), hml_n, preferred_element_type=f32)
            return p[:, :hk] + p[:, hk:2 * hk] + p[:, 2 * hk:]

        q = gq_ref[n] * (GLA_DK ** -0.5)
        k = gk_ref[n]
        b = sums(0)
        q_in.append((q * jnp.exp(b)).astype(bf16))
        k_st.append((k * jnp.exp(sums(1))).astype(bf16))
        dec_rows.append(jnp.exp(b[c - 1:c, :]))
        qf.append([q.astype(bf16)] + [(q * jnp.exp(sums(2 + 2 * lev))).astype(bf16) for lev in range(nlev)])
        kf.append([k.astype(bf16)] + [(k * jnp.exp(sums(3 + 2 * lev))).astype(bf16) for lev in range(nlev)])

    for n in range(nseq):
        for lev in range(nlev + 1):
            for h in heads:
                att_ref[n, h, lev] = _dot_nt(qf[n][lev][:, ks(h)], kf[n][lev][:, ks(h)])

    att = m_ref[0][None, None] * att_ref[:, :, 0]
    for lev in range(1, nlev + 1):
        att = att + m_ref[lev][None, None] * att_ref[:, :, lev]
    att = att.astype(bf16)

    for n in range(nseq):
        v = gv_ref[n].astype(bf16)
        for h in heads:
            v_h = v[:, vs(h)]
            o_ref[n, h] = _dot(q_in[n][:, ks(h)], s_ref[n, h]) + _dot(att[n, h], v_h)
            upd_ref[n, h] = _dot_tn(k_st[n][:, ks(h)], v_h)

    eye = (lax.broadcasted_iota(jnp.int32, (GLA_DK, GLA_DK), 0)
           == lax.broadcasted_iota(jnp.int32, (GLA_DK, GLA_DK), 1))[None, None]
    dec = jnp.stack([jnp.stack([dec_rows[n][:, ks(h)] for h in heads]) for n in range(nseq)])
    dec_col = jnp.sum(jnp.where(eye, jnp.broadcast_to(dec, (nseq, GLA_HEADS, GLA_DK, GLA_DK)), 0.0),
                      axis=3, keepdims=True)
    s_new = s_ref[...] * dec_col + upd_ref[...]
    s_ref[...] = s_new
    sout_ref[...] = s_new
    o = o_ref[...]
    o = o * lax.rsqrt(jnp.mean(o * o, axis=-1, keepdims=True) + RMS_EPS) * ng_ref[...]
    for n in range(nseq):
        for h in heads:
            y_ref[n, :, vs(h)] = (o[n, h] * _silu(gz_ref[n, :, vs(h)])).astype(y_ref.dtype)


_GLA_SEGS = ("gq", "gk", "gv", "gz", "glr")


def _gla_call(proj, s0, gw, nb, nseq, nc, c, name):
    dstack, masks = _gla_consts(c)
    kern = functools.partial(_gla_kernel, nseq=nseq, c=c)
    hk = GLA_HEADS * GLA_DK
    st_spec = pl.BlockSpec((nseq, GLA_HEADS, GLA_DK, GLA_DV), lambda b, ci: (b, 0, 0, 0))
    return pl.pallas_call(
        kern,
        grid=(nb, nc),
        in_specs=[_seg_spec(s, nseq, c) for s in _GLA_SEGS] + [
            st_spec, _const_spec((LANE, hk)), _const_spec((1, hk)), _const_spec((1, GLA_DV)),
            _const_spec(dstack.shape), _const_spec(masks.shape)],
        out_specs=[_tok_spec(nseq, c, BRANCH_W), st_spec],
        out_shape=[jax.ShapeDtypeStruct((nb * nseq, nc * c, BRANCH_W), bf16),
                   jax.ShapeDtypeStruct((nb * nseq, GLA_HEADS, GLA_DK, GLA_DV), f32)],
        scratch_shapes=[pltpu.VMEM((nseq, GLA_HEADS, GLA_DK, GLA_DV), f32),
                        pltpu.VMEM((nseq, GLA_HEADS, masks.shape[0], c, c), f32),
                        pltpu.VMEM((nseq, GLA_HEADS, c, GLA_DV), f32),
                        pltpu.VMEM((nseq, GLA_HEADS, GLA_DK, GLA_DV), f32)],
        compiler_params=_params("parallel", "arbitrary"),
        name=name,
    )(*([proj] * len(_GLA_SEGS)), s0, gw["wa"], gw["ba"], gw["ng"], jnp.asarray(dstack), jnp.asarray(masks))


def _gla_step_consts(t, nseq):
    dstack, masks = _gla_consts(t)
    eye = np.eye(nseq, dtype=np.float32)
    dbd = np.stack([np.kron(eye, dstack[i * t:(i + 1) * t]) for i in range(dstack.shape[0] // t)])
    mbd = np.stack([np.kron(eye, m) for m in masks])
    return dbd, mbd


def _gla_step_kernel(gq_ref, gk_ref, gv_ref, gz_ref, glr_ref, s0_ref, wa_ref, ba_ref, ng_ref, d_ref, m_ref,
                     y_ref, sout_ref, *, nseq, t):
    r = nseq * t
    nlev = int(math.log2(t))
    hk = GLA_HEADS * GLA_DK
    q = gq_ref[...].reshape(r, hk) * (GLA_DK ** -0.5)
    k = gk_ref[...].reshape(r, hk)
    v = gv_ref[...].reshape(r, BRANCH_W)
    gz = gz_ref[...].reshape(r, BRANCH_W)
    z = _dot(glr_ref[...].reshape(r, LANE), wa_ref[...]) + ba_ref[...]
    la = _log_sigmoid(z) * (1.0 / GLA_TAU)
    hi = la.astype(bf16)
    r1 = la - hi.astype(f32)
    mid = r1.astype(bf16)
    lo = (r1 - mid.astype(f32)).astype(bf16)
    hml = jnp.concatenate([hi, mid, lo], axis=1)

    def sums(blk):
        p = jnp.dot(d_ref[blk].astype(bf16), hml, preferred_element_type=f32)
        return p[:, :hk] + p[:, hk:2 * hk] + p[:, 2 * hk:]

    b = sums(0)
    q_in = q * jnp.exp(b)
    k_st = k * jnp.exp(sums(1))
    dec3 = jnp.exp(b.reshape(nseq, t, hk)[:, t - 1:t, :])
    qf = [q]
    kf = [k]
    for lev in range(nlev):
        qf.append(q * jnp.exp(sums(2 + 2 * lev)))
        kf.append(k * jnp.exp(sums(3 + 2 * lev)))

    own = (lax.broadcasted_iota(jnp.int32, (r, nseq * GLA_DK), 0) // t
           == lax.broadcasted_iota(jnp.int32, (r, nseq * GLA_DK), 1) // GLA_DK)
    eye = (lax.broadcasted_iota(jnp.int32, (GLA_DK, GLA_DK), 0)
           == lax.broadcasted_iota(jnp.int32, (GLA_DK, GLA_DK), 1))[None]

    def spread(x):
        x2 = jnp.concatenate([x, x], axis=1)
        return jnp.where(own, jnp.concatenate([x2] * (nseq // 2), axis=1), 0.0)

    ys = []
    for h in range(GLA_HEADS):
        ks = slice(h * GLA_DK, (h + 1) * GLA_DK)
        vs = slice(h * GLA_DV, (h + 1) * GLA_DV)
        att = jnp.zeros((r, r), f32)
        for lev in range(nlev + 1):
            att = att + m_ref[lev] * _dot_nt(qf[lev][:, ks], kf[lev][:, ks])
        s_h = s0_ref[:, h]
        v_h = v[:, vs]
        o = _dot(spread(q_in[:, ks]), s_h.reshape(nseq * GLA_DK, GLA_DV)) + _dot(att, v_h)
        upd = _dot_tn(spread(k_st[:, ks]), v_h)
        dec_col = jnp.sum(jnp.where(eye, jnp.broadcast_to(dec3[:, :, ks], (nseq, GLA_DK, GLA_DK)), 0.0),
                          axis=2, keepdims=True)
        sout_ref[:, h] = s_h * dec_col + upd.reshape(nseq, GLA_DK, GLA_DV)
        o = o * lax.rsqrt(jnp.mean(o * o, axis=-1, keepdims=True) + RMS_EPS) * ng_ref[...]
        ys.append(o * _silu(gz[:, vs]))
    y_ref[...] = jnp.concatenate(ys, axis=1).reshape(nseq, t, BRANCH_W).astype(y_ref.dtype)


def _gla_step_call(proj, s0_all, layer, gw, nb, nseq, t, name):
    dbd, mbd = _gla_step_consts(t, nseq)
    kern = functools.partial(_gla_step_kernel, nseq=nseq, t=t)
    hk = GLA_HEADS * GLA_DK
    st_spec = pl.BlockSpec((nseq, GLA_HEADS, GLA_DK, GLA_DV), lambda b, ci: (b, 0, 0, 0))
    s0_spec = pl.BlockSpec((None, nseq, GLA_HEADS, GLA_DK, GLA_DV), lambda b, ci: (layer, b, 0, 0, 0))
    return pl.pallas_call(
        kern,
        grid=(nb, 1),
        in_specs=[_seg_spec(s, nseq, t) for s in _GLA_SEGS] + [
            s0_spec, _const_spec((LANE, hk)), _const_spec((1, hk)), _const_spec((1, GLA_DV)),
            _const_spec(dbd.shape), _const_spec(mbd.shape)],
        out_specs=[_tok_spec(nseq, t, BRANCH_W), st_spec],
        out_shape=[jax.ShapeDtypeStruct((nb * nseq, t, BRANCH_W), bf16),
                   jax.ShapeDtypeStruct((nb * nseq, GLA_HEADS, GLA_DK, GLA_DV), f32)],
        compiler_params=_params("parallel", "arbitrary"),
        name=name,
    )(*([proj] * len(_GLA_SEGS)), s0_all, gw["wa"], gw["ba"], gw["ng"], jnp.asarray(dbd), jnp.asarray(mbd))


def _rope_tables(pos0, t):
    half = ROT_DIM // 2
    dim = jnp.arange(LANE) % HEAD_DIM
    inv = ROPE_THETA ** (-(dim % half).astype(f32) / half)
    ang = (pos0 + jnp.arange(t)).astype(f32)[:, None] * inv[None, :]
    cos, sin = jnp.cos(ang), jnp.sin(ang)
    first, second = (dim < half)[None, :], ((dim >= half) & (dim < ROT_DIM))[None, :]
    c_tab = jnp.where(first | second, cos, 1.0)
    sa_tab = jnp.where(first, -sin, 0.0)
    sb_tab = jnp.where(second, sin, 0.0)
    return c_tab, sa_tab, sb_tab


def _rope(x, c_tab, sa_tab, sb_tab):
    wd = x.shape[-1]
    ax = x.ndim - 1
    rep = wd // LANE
    half = ROT_DIM // 2
    if rep > 1:
        c_tab, sa_tab, sb_tab = (jnp.concatenate([tb] * rep, axis=-1) for tb in (c_tab, sa_tab, sb_tab))
    return x * c_tab + pltpu.roll(x, wd - half, ax) * sa_tab + pltpu.roll(x, half, ax) * sb_tab


def _swa_kernel(sink_ref, sq_ref, sz_ref, sk_ref, sv_ref, ct_ref, sat_ref, sbt_ref, kp_ref, vp_ref,
                y_ref, klast_ref, vlast_ref, kprev_ref, vprev_ref, sp_ref, sc_ref, o_ref, den_ref,
                *, nseq, qb, pos0):
    blk = pl.program_id(1)
    hd = HEAD_DIM

    @pl.when(blk == 0)
    def _():
        kprev_ref[...] = kp_ref[...]
        vprev_ref[...] = vp_ref[...]

    tabs = tuple(r[...][None] for r in (ct_ref, sat_ref, sbt_ref))
    q3 = _rope(sq_ref[...], *tabs) * (hd ** -0.5)
    k3 = _rope(sk_ref[...], *tabs)
    v3 = sv_ref[...]
    klast_ref[...] = k3
    vlast_ref[...] = v3

    for n in range(nseq):
        for kv in range(SWA_KV_HEADS):
            ds = slice(kv * hd, (kv + 1) * hd)
            qs = jnp.concatenate(
                [q3[n][:, (kv * SWA_GROUP + g) * hd:(kv * SWA_GROUP + g + 1) * hd] for g in range(SWA_GROUP)],
                axis=0).astype(bf16)
            sp_ref[n, kv] = _dot_nt(qs, kprev_ref[n][:, ds])
            sc_ref[n, kv] = _dot_nt(qs, k3[n][:, ds])

    mrows = SWA_GROUP * qb
    qi = lax.broadcasted_iota(jnp.int32, (mrows, WINDOW), 0) % qb
    kj = lax.broadcasted_iota(jnp.int32, (mrows, WINDOW), 1)
    past_ok = (kj >= qi) & (kj >= (WINDOW - pos0) - blk * qb)
    cur_ok = kj <= qi
    sink = jnp.stack([jnp.concatenate([jnp.full((qb, 1), sink_ref[kv * SWA_GROUP + g], f32)
                                       for g in range(SWA_GROUP)], axis=0) for kv in range(SWA_KV_HEADS)])[None]
    s_p = jnp.where(past_ok[None, None], sp_ref[...], -jnp.inf)
    s_c = jnp.where(cur_ok[None, None], sc_ref[...], -jnp.inf)
    m = jnp.maximum(jnp.max(jnp.maximum(s_p, s_c), axis=3, keepdims=True), sink)
    p_p = jnp.exp(s_p - m).astype(bf16)
    p_c = jnp.exp(s_c - m).astype(bf16)
    ones = jnp.ones((WINDOW, LANE), bf16)
    for n in range(nseq):
        vp = vprev_ref[n].astype(bf16)
        vc = v3[n].astype(bf16)
        for kv in range(SWA_KV_HEADS):
            o_ref[n, kv] = _dot(p_p[n, kv], vp) + _dot(p_c[n, kv], vc)
            den_ref[n, kv] = _dot(p_p[n, kv], ones) + _dot(p_c[n, kv], ones)
    o = o_ref[...] / (den_ref[...] + jnp.exp(sink - m))
    outs = [o[:, j // SWA_GROUP, (j % SWA_GROUP) * qb:(j % SWA_GROUP + 1) * qb,
              (j // SWA_GROUP) * hd:(j // SWA_GROUP + 1) * hd] for j in range(SWA_HEADS)]
    y_ref[...] = (jnp.concatenate(outs, axis=2) * _silu(sz_ref[...])).astype(y_ref.dtype)
    kprev_ref[...] = k3
    vprev_ref[...] = v3


_SWA_SEGS = ("sq", "sz", "sk", "sv")


def _swa_call(proj, sinks, k_past, v_past, pos0, nb, nseq, nc, qb, name):
    assert qb == WINDOW
    t_total = nc * qb
    c_tab, sa_tab, sb_tab = _rope_tables(pos0, t_total)
    kern = functools.partial(_swa_kernel, nseq=nseq, qb=qb, pos0=pos0)
    kvw = SWA_KV_HEADS * HEAD_DIM
    mrows = SWA_GROUP * qb
    tab_spec = pl.BlockSpec((qb, LANE), lambda b, c: (c, 0))
    past_spec = pl.BlockSpec((nseq, WINDOW, kvw), lambda b, c: (b, 0, 0))
    return pl.pallas_call(
        kern,
        grid=(nb, nc),
        in_specs=[pl.BlockSpec(memory_space=pltpu.SMEM)] + [_seg_spec(s, nseq, qb) for s in _SWA_SEGS] + [
            tab_spec, tab_spec, tab_spec, past_spec, past_spec],
        out_specs=[_tok_spec(nseq, qb, BRANCH_W), past_spec, past_spec],
        out_shape=[jax.ShapeDtypeStruct((nb * nseq, nc * qb, BRANCH_W), bf16),
                   jax.ShapeDtypeStruct((nb * nseq, WINDOW, kvw), f32),
                   jax.ShapeDtypeStruct((nb * nseq, WINDOW, kvw), f32)],
        scratch_shapes=[pltpu.VMEM((nseq, WINDOW, kvw), f32), pltpu.VMEM((nseq, WINDOW, kvw), f32),
                        pltpu.VMEM((nseq, SWA_KV_HEADS, mrows, WINDOW), f32),
                        pltpu.VMEM((nseq, SWA_KV_HEADS, mrows, qb), f32),
                        pltpu.VMEM((nseq, SWA_KV_HEADS, mrows, kvw), f32),
                        pltpu.VMEM((nseq, SWA_KV_HEADS, mrows, kvw), f32)],
        compiler_params=_params("parallel", "arbitrary"),
        name=name,
    )(sinks, *([proj] * len(_SWA_SEGS)), c_tab, sa_tab, sb_tab, k_past, v_past)


def _swa_step_kernel(sink_ref, sq_ref, sz_ref, sk_ref, sv_ref, ct_ref, sat_ref, sbt_ref, kp_ref, vp_ref,
                     y_ref, klast_ref, vlast_ref, sp_ref, sc_ref, o_ref, *, nseq, t, pos0):
    hd = HEAD_DIM
    kvw = SWA_KV_HEADS * hd
    mrows = SWA_HEADS * t
    tabs = tuple(r[...][None] for r in (ct_ref, sat_ref, sbt_ref))
    q3 = _rope(sq_ref[...], *tabs) * (hd ** -0.5)
    k3 = _rope(sk_ref[...], *tabs)
    v3 = sv_ref[...]

    lane = lax.broadcasted_iota(jnp.int32, (kvw, WINDOW), 1)
    pad = jnp.zeros((WINDOW - t, kvw), f32)

    def shifted(old_t, new):
        new_t = jnp.concatenate([pad, new], axis=0).T
        out = jnp.where(lane >= WINDOW - t, new_t, pltpu.roll(old_t, WINDOW - t, 1))
        return out.reshape(SWA_KV_HEADS, hd, WINDOW)

    for n in range(nseq):
        klast_ref[n] = shifted(kp_ref[n].reshape(kvw, WINDOW), k3[n])
        vlast_ref[n] = shifted(vp_ref[n].reshape(kvw, WINDOW), v3[n])

    zero = jnp.zeros((nseq, t, hd), f32)
    pieces = []
    for j in range(SWA_HEADS):
        qj = q3[:, :, j * hd:(j + 1) * hd]
        pieces.append(jnp.concatenate([qj, zero] if j // SWA_GROUP == 0 else [zero, qj], axis=2))
    qbd = jnp.concatenate(pieces, axis=1).astype(bf16)

    for n in range(nseq):
        sp_ref[n] = _dot(qbd[n], kp_ref[n].reshape(kvw, WINDOW))
        sc_ref[n] = _dot_nt(qbd[n], k3[n])

    qi = lax.broadcasted_iota(jnp.int32, (mrows, WINDOW), 0) % t
    kj = lax.broadcasted_iota(jnp.int32, (mrows, WINDOW), 1)
    past_ok = kj >= qi
    if pos0 < WINDOW:
        past_ok = past_ok & (kj >= WINDOW - pos0)
    qi_c = lax.broadcasted_iota(jnp.int32, (mrows, t), 0) % t
    kj_c = lax.broadcasted_iota(jnp.int32, (mrows, t), 1)
    cur_ok = kj_c <= qi_c
    sink = jnp.concatenate([jnp.full((t, 1), sink_ref[j], f32) for j in range(SWA_HEADS)], axis=0)[None]
    s_p = jnp.where(past_ok[None], sp_ref[...], -jnp.inf)
    s_c = jnp.where(cur_ok[None], sc_ref[...], -jnp.inf)
    m = jnp.maximum(jnp.maximum(jnp.max(s_p, axis=2, keepdims=True), jnp.max(s_c, axis=2, keepdims=True)), sink)
    p_p = jnp.exp(s_p - m)
    p_c = jnp.exp(s_c - m)
    den = jnp.sum(p_p, axis=2, keepdims=True) + jnp.sum(p_c, axis=2, keepdims=True) + jnp.exp(sink - m)
    p_p = p_p.astype(bf16)
    p_c = p_c.astype(bf16)
    for n in range(nseq):
        o_ref[n] = _dot_nt(p_p[n], vp_ref[n].reshape(kvw, WINDOW)) + _dot(p_c[n], v3[n])
    o = o_ref[...] / den
    outs = []
    for j in range(SWA_HEADS):
        kv = j // SWA_GROUP
        outs.append(o[:, j * t:(j + 1) * t, kv * hd:(kv + 1) * hd])
    y_ref[...] = (jnp.concatenate(outs, axis=2) * _silu(sz_ref[...])).astype(y_ref.dtype)


def _swa_step_call(proj, sinks, k_past_t, v_past_t, layer, pos0, nb, nseq, t, name):
    c_tab, sa_tab, sb_tab = _rope_tables(pos0, t)
    kern = functools.partial(_swa_step_kernel, nseq=nseq, t=t, pos0=pos0)
    kvw = SWA_KV_HEADS * HEAD_DIM
    mrows = SWA_HEADS * t
    tab_spec = pl.BlockSpec((t, LANE), lambda b, c: (0, 0))
    past_spec = pl.BlockSpec((None, nseq, SWA_KV_HEADS, HEAD_DIM, WINDOW), lambda b, c: (layer, b, 0, 0, 0))
    new_spec = pl.BlockSpec((nseq, SWA_KV_HEADS, HEAD_DIM, WINDOW), lambda b, c: (b, 0, 0, 0))
    new_shape = jax.ShapeDtypeStruct((nb * nseq, SWA_KV_HEADS, HEAD_DIM, WINDOW), f32)
    return pl.pallas_call(
        kern,
        grid=(nb, 1),
        in_specs=[pl.BlockSpec(memory_space=pltpu.SMEM)] + [_seg_spec(s, nseq, t) for s in _SWA_SEGS] + [
            tab_spec, tab_spec, tab_spec, past_spec, past_spec],
        out_specs=[_tok_spec(nseq, t, BRANCH_W), new_spec, new_spec],
        out_shape=[jax.ShapeDtypeStruct((nb * nseq, t, BRANCH_W), bf16), new_shape, new_shape],
        scratch_shapes=[pltpu.VMEM((nseq, mrows, WINDOW), f32), pltpu.VMEM((nseq, mrows, t), f32),
                        pltpu.VMEM((nseq, mrows, kvw), f32)],
        compiler_params=_params("parallel", "arbitrary"),
        name=name,
    )(sinks, *([proj] * len(_SWA_SEGS)), c_tab, sa_tab, sb_tab, k_past_t, v_past_t)


def _sgu_kernel(x_ref, wu_ref, wv_ref, wz_ref, g_ref, b_ref, wm_ref, bias_ref, y_ref, *rest, ntile, want_vn):
    vn_ref = rest[0] if want_vn else None
    su_ref, sv_ref, sz_ref = rest[-3:]
    _project(x_ref, ((wu_ref, (su_ref,)), (wv_ref, (sv_ref,)), (wz_ref, (sz_ref,))))
    vn = _layer_norm(sv_ref[...], g_ref[...], b_ref[...])
    if want_vn:
        vn_ref[...] = vn
    for r in range(ntile):
        rows = slice(r * SGU_CHUNK, (r + 1) * SGU_CHUNK)
        mixed = jnp.concatenate(
            [jnp.dot(wm_ref[g], vn[rows, g * SGU_GC:(g + 1) * SGU_GC].astype(bf16), preferred_element_type=f32)
             for g in range(SGU_GROUPS)], axis=1)
        y = su_ref[rows, :] * (mixed + bias_ref[...]) * _silu(sz_ref[rows, :])
        y_ref[rows, :] = y.astype(y_ref.dtype)


def _sgu_call(xb, w_t, layer, ln_g, ln_b, wmix, bias, n_tok, ntile, want_vn, name):
    rows = ntile * SGU_CHUNK
    w = BRANCH_W
    kern = functools.partial(_sgu_kernel, ntile=ntile, want_vn=want_vn)

    def const(shape):
        return pl.BlockSpec(shape, lambda i: (0,) * len(shape))

    out_specs = [pl.BlockSpec((rows, w), lambda i: (i, 0))]
    out_shape = [jax.ShapeDtypeStruct((n_tok, w), bf16)]
    if want_vn:
        out_specs.append(pl.BlockSpec((rows, w), lambda i: (i, 0)))
        out_shape.append(jax.ShapeDtypeStruct((n_tok, w), f32))
    return pl.pallas_call(
        kern,
        grid=(n_tok // rows,),
        in_specs=[pl.BlockSpec((rows, D_MODEL), lambda i: (i, 0)), _w_spec("su", "su", layer),
                  _w_spec("svv", "svv", layer), _w_spec("suz", "suz", layer), const((1, w)), const((1, w)),
                  const((SGU_GROUPS, SGU_CHUNK, SGU_CHUNK)), const((SGU_CHUNK, w))],
        out_specs=out_specs,
        out_shape=out_shape,
        scratch_shapes=[_seg_scratch(s, rows) for s in ("su", "svv", "suz")],
        compiler_params=_params("parallel"),
        name=name,
    )(xb, w_t, w_t, w_t, ln_g, ln_b, wmix, bias)


def _mem_kernel(x_ref, wm_ref, mk_ref, mv_ref, y_ref, mq_ref, mz_ref, *, nseq, tq):
    _project(x_ref, ((wm_ref, (mq_ref, mz_ref)),))
    for n in range(nseq):
        q = mq_ref[n] * (HEAD_DIM ** -0.5)
        mk = mk_ref[n]
        mv = mv_ref[n]
        outs = []
        for h in range(MEM_HEADS):
            ds = slice(h * HEAD_DIM, (h + 1) * HEAD_DIM)
            s = _dot_nt(q[:, ds], mk[:, ds])
            m = jnp.max(s, axis=1, keepdims=True)
            p = jnp.exp(s - m)
            den = jnp.sum(p, axis=1, keepdims=True)
            outs.append(_dot(p, mv[:, ds]) / den)
        o = jnp.concatenate(outs, axis=1)
        y_ref[n] = (o * _silu(mz_ref[n])).astype(y_ref.dtype)


def _mem_call(xb, w_t, layer, mk, mv, nb, nseq, nc, tq, name):
    kern = functools.partial(_mem_kernel, nseq=nseq, tq=tq)
    kv_spec = pl.BlockSpec((nseq, N_MEM, MEM_W), lambda b, c: (b, 0, 0))
    return pl.pallas_call(
        kern,
        grid=(nb, nc),
        in_specs=[_tok_spec(nseq, tq, D_MODEL), _w_spec("mq", "mz", layer), kv_spec, kv_spec],
        out_specs=_tok_spec(nseq, tq, MEM_W),
        out_shape=jax.ShapeDtypeStruct((nb * nseq, nc * tq, MEM_W), bf16),
        scratch_shapes=[_seg_scratch("mq", nseq, tq), _seg_scratch("mz", nseq, tq)],
        compiler_params=_params("parallel", "arbitrary"),
        name=name,
    )(xb, w_t, mk, mv)


def _mem_step_kernel(x_ref, wm_ref, mk_ref, mv_ref, y_ref, s_ref, o_ref, mq_ref, mz_ref, *, nseq, t):
    mrows = MEM_HEADS * t
    _project(x_ref, ((wm_ref, (mq_ref, mz_ref)),))
    row_head = lax.broadcasted_iota(jnp.int32, (mrows, MEM_W), 0) // t
    lane_head = lax.broadcasted_iota(jnp.int32, (mrows, MEM_W), 1) // HEAD_DIM
    own = (row_head == lane_head)[None]
    q3 = mq_ref[...] * (HEAD_DIM ** -0.5)
    qbd = jnp.where(own, jnp.concatenate([q3] * MEM_HEADS, axis=1), 0.0).astype(bf16)
    for n in range(nseq):
        s_ref[n] = _dot(qbd[n], mk_ref[n].reshape(MEM_W, N_MEM))
    s = s_ref[...]
    p = jnp.exp(s - jnp.max(s, axis=2, keepdims=True))
    den = jnp.sum(p, axis=2, keepdims=True)
    p = p.astype(bf16)
    for n in range(nseq):
        o_ref[n] = _dot_nt(p[n], mv_ref[n].reshape(MEM_W, N_MEM))
    o = jnp.where(own, o_ref[...] / den, 0.0)
    acc = o[:, 0:t, :]
    for h in range(1, MEM_HEADS):
        acc = acc + o[:, h * t:(h + 1) * t, :]
    y_ref[...] = (acc * _silu(mz_ref[...])).astype(y_ref.dtype)


def _mem_step_call(xb, w_t, mk_t, mv_t, layer, nb, nseq, t, name):
    kern = functools.partial(_mem_step_kernel, nseq=nseq, t=t)
    kv_spec = pl.BlockSpec((None, nseq, MEM_HEADS, HEAD_DIM, N_MEM), lambda b, c: (layer, b, 0, 0, 0))
    mrows = MEM_HEADS * t
    return pl.pallas_call(
        kern,
        grid=(nb, 1),
        in_specs=[_tok_spec(nseq, t, D_MODEL), _w_spec("mq", "mz", layer), kv_spec, kv_spec],
        out_specs=_tok_spec(nseq, t, MEM_W),
        out_shape=jax.ShapeDtypeStruct((nb * nseq, t, MEM_W), bf16),
        scratch_shapes=[pltpu.VMEM((nseq, mrows, N_MEM), f32), pltpu.VMEM((nseq, mrows, MEM_W), f32),
                        _seg_scratch("mq", nseq, t), _seg_scratch("mz", nseq, t)],
        compiler_params=_params("parallel", "arbitrary"),
        name=name,
    )(xb, w_t, mk_t, mv_t)


def _merge_kernel(yg_ref, yl_ref, ys_ref, yu_ref, ym_ref, x_ref, wg_ref, wb_ref, wm_ref, wo_ref, g_ref, b_ref,
                  o_ref, ob_ref):
    d = D_MODEL
    x = x_ref[...]
    xb = x.astype(bf16)

    def gate(n):
        return _sigmoid(_dot_nt(xb, wg_ref[n * d:(n + 1) * d, :]))

    merged = gate(4) * jnp.dot(ym_ref[...], wm_ref[...], preferred_element_type=f32)
    for n, y_ref in enumerate((yg_ref, yl_ref, ys_ref, yu_ref)):
        merged = merged + gate(n) * jnp.dot(y_ref[...], wb_ref[n], preferred_element_type=f32)
    out = _dot(merged, wo_ref[...])
    y = _layer_norm(DN_ALPHA * x + out, g_ref[...], b_ref[...])
    o_ref[...] = y
    ob_ref[...] = y.astype(bf16)


def _merge_call(ys, x, mw, layer, tm, name):
    n_tok, d = x.shape
    w = BRANCH_W

    def rows(width):
        return pl.BlockSpec((tm, width), lambda i: (i, 0))

    def const(shape):
        return pl.BlockSpec(shape, lambda i: (0,) * len(shape), pipeline_mode=pl.Buffered(1))

    gates_spec = pl.BlockSpec((None, 5 * d, d), lambda i: (layer, 0, 0), pipeline_mode=pl.Buffered(1))
    return pl.pallas_call(
        _merge_kernel,
        grid=(n_tok // tm,),
        in_specs=[rows(w), rows(w), rows(w), rows(w), rows(MEM_W), rows(d),
                  gates_spec, const((4, w, d)), const((MEM_W, d)), const((d, d)), const((1, d)),
                  const((1, d))],
        out_specs=[rows(d), rows(d)],
        out_shape=[jax.ShapeDtypeStruct((n_tok, d), f32), jax.ShapeDtypeStruct((n_tok, d), bf16)],
        compiler_params=_params("parallel"),
        name=name,
    )(*ys, x, mw["wg"], mw["wb"], mw["wm"], mw["wo"], mw["g"], mw["b"])


def _prep_w_in(w_in):
    w_t = jnp.swapaxes(w_in, 1, 2).astype(bf16)
    runs = []
    used = 0
    for run in _PROJ_RUNS:
        if run is None:
            runs.append(jnp.zeros((DEPTH, N_PROJ - used, D_MODEL), bf16))
            used = N_PROJ
            continue
        lo = _ORIG_OFF[run[0]][0]
        hi = _ORIG_OFF[run[1]][0] + _ORIG_OFF[run[1]][1]
        runs.append(w_t[:, lo:hi])
        used += hi - lo
    assert used == N_WROWS
    return jnp.concatenate(runs, axis=1), w_t[:, _ORIG_OFF["gates"][0]:]


def _prep_layer(l, w_proj, w_gates, gla_wa2, gla_ba, gla_norm_g, lru_conv_w, lru_conv_b, lru_wr, lru_br, lru_wi,
                lru_bi, lru_L, swa_sinks, sgu_ln_g, sgu_ln_b, sgu_w, sgu_b, w_mem_kv, w_branch, w_branch_mem, w_out,
                ln_g, ln_b):
    d = D_MODEL
    w = BRANCH_W

    def block_diag(wb):
        eye = jnp.eye(LRU_BLOCKS, dtype=f32)
        return (eye[:, None, :, None] * wb[:, :, None, :]).reshape(w, w).astype(bf16)

    tril = jnp.tril(jnp.ones((SGU_CHUNK, SGU_CHUNK), f32))
    wmix_p = (sgu_w[l] * tril).astype(bf16)
    bias_p = jnp.repeat(sgu_b[l].T, SGU_GC, axis=1)
    t8 = SUBLANE
    rep = SGU_CHUNK // t8
    w8 = (sgu_w[l] * tril)[:, :t8, :t8]
    seq_eye = jnp.eye(rep, dtype=f32)
    wmix_s = (seq_eye[None, :, None, :, None] * w8[:, None, :, None, :]).reshape(
        SGU_GROUPS, SGU_CHUNK, SGU_CHUNK).astype(bf16)
    bias_s = jnp.tile(bias_p[:t8], (rep, 1))
    return dict(
        w_proj=w_proj,
        w_mem_kv=w_mem_kv[l].astype(bf16),
        gla=dict(wa=jnp.pad(gla_wa2[l], ((0, LANE - GLA_RANK), (0, 0))).astype(bf16),
                 ba=gla_ba[l].reshape(1, -1), ng=gla_norm_g[l].reshape(1, -1)),
        lru=dict(conv_w=lru_conv_w[l], conv_b=lru_conv_b[l].reshape(1, w), wr=block_diag(lru_wr[l]),
                 br=lru_br[l].reshape(1, w), wi=block_diag(lru_wi[l]), bi=lru_bi[l].reshape(1, w),
                 lam=lru_L[l].reshape(1, w)),
        sinks=swa_sinks[l],
        sgu=dict(g=sgu_ln_g[l].reshape(1, w), b=sgu_ln_b[l].reshape(1, w), wmix_p=wmix_p, bias_p=bias_p,
                 wmix_s=wmix_s, bias_s=bias_s),
        merge=dict(wg=w_gates, wb=w_branch[l].astype(bf16), wm=w_branch_mem[l].astype(bf16), wo=w_out[l].astype(bf16),
                   g=ln_g[l].reshape(1, d), b=ln_b[l].reshape(1, d)),
    )


def _layer(x, lw, grp, st, layer, tag):
    nseq_total, t = grp["batch"], grp["seq"]
    n_tok = nseq_total * t
    x, xb = x
    xb3 = xb.reshape(nseq_total, t, D_MODEL)
    w_t = lw["w_proj"]
    proj = _matmul_call(xb, w_t, min(grp["proj_tm"], n_tok), 1024, "proj_" + tag, w_transposed=True, n_out=N_PROJ,
                        layer=layer)
    proj3 = proj.reshape(nseq_total, t, N_PROJ)

    lt = grp["lru"]
    y_lru, hlast, hist = _lru_call(proj3, st["hist0"], st["h0"], lw["lru"], nseq_total // lt[0], lt[0], t // lt[1],
                                   lt[1], "lru_" + tag)
    short = grp["kind"] == "s"
    gt = grp["gla"]
    if short:
        y_gla, s_out = _gla_step_call(proj3, st["gla0"], layer, lw["gla"], nseq_total // gt[0], gt[0], t,
                                      "gla_" + tag)
    else:
        y_gla, s_out = _gla_call(proj3, st["gla0"], lw["gla"], nseq_total // gt[0], gt[0], t // gt[1], gt[1],
                                 "gla_" + tag)
    wt = grp["swa"]
    if short:
        y_swa, k_last, v_last = _swa_step_call(proj3, lw["sinks"], st["k_past"], st["v_past"], layer, grp["pos0"],
                                               nseq_total // wt[0], wt[0], t, "swa_" + tag)
    else:
        y_swa, k_last, v_last = _swa_call(proj3, lw["sinks"], st["k_past"], st["v_past"], grp["pos0"],
                                          nseq_total // wt[0], wt[0], t // wt[1], wt[1], "swa_" + tag)
    sg = lw["sgu"]
    sgu_out = _sgu_call(xb, w_t, layer, sg["g"], sg["b"], sg["wmix_" + grp["kind"]], sg["bias_" + grp["kind"]], n_tok,
                        grp["sgu_tiles"], grp["kind"] == "s", "sgu_" + tag)
    mt = grp["mem"]
    if short:
        y_mem = _mem_step_call(xb3, w_t, st["mk"], st["mv"], layer, nseq_total // mt[0], mt[0], t, "mem_" + tag)
    else:
        y_mem = _mem_call(xb3, w_t, layer, st["mk"], st["mv"], nseq_total // mt[0], mt[0], t // mt[1], mt[1],
                          "mem_" + tag)
    ys = tuple(y.reshape(n_tok, y.shape[-1]) for y in (y_gla, y_lru, y_swa, sgu_out[0], y_mem))
    x_new = _merge_call(ys, x, lw["merge"], layer, min(256, n_tok), "merge_" + tag)
    return x_new, dict(gla=s_out, hlast=hlast, hist=hist, k_last=k_last, v_last=v_last,
                       vn=sgu_out[1] if len(sgu_out) > 1 else None)


_PROMPT = dict(kind="p", pos0=0, proj_tm=2048, lru=(1, 256), gla=(4, 128), swa=(4, 128), sgu_tiles=4, mem=(1, 512))
_SAMPLE = dict(kind="s", pos0=PAST_LEN, proj_tm=1024, lru=(32, 8), gla=(16, 8), swa=(16, 8), sgu_tiles=8,
               mem=(16, 8))


def kernel(x_prompt, x_sample, mem_prompt, state_gla, state_lru_h, state_lru_conv, cache_swa_k, cache_swa_v,
           cache_mem_k, cache_mem_v, ln_in_g, ln_in_b, w_in, gla_wa2, gla_ba, gla_norm_g, lru_conv_w, lru_conv_b,
           lru_wr, lru_br, lru_wi, lru_bi, lru_L, swa_sinks, sgu_ln_g, sgu_ln_b, sgu_w, sgu_b, w_mem_kv, w_branch,
           w_branch_mem, w_out, ln_g, ln_b):
    bp, tp, d = x_prompt.shape
    bs, ts, _ = x_sample.shape
    w = BRANCH_W
    kvw = SWA_KV_HEADS * HEAD_DIM
    gp = dict(_PROMPT, batch=bp, seq=tp)
    gs = dict(_SAMPLE, batch=bs, seq=ts)

    xp = _ln_call(x_prompt.reshape(bp * tp, d), ln_in_g, ln_in_b)
    xs = _ln_call(x_sample.reshape(bs * ts, d), ln_in_g, ln_in_b)
    mem2 = mem_prompt.reshape(bp * N_MEM, d)

    swa_k_t, swa_v_t, mem_k_t, mem_v_t = (jnp.transpose(c, (0, 1, 3, 4, 2))
                                          for c in (cache_swa_k, cache_swa_v, cache_mem_k, cache_mem_v))

    w_proj, w_gates = _prep_w_in(w_in)
    outs_p, outs_s, mks, mvs = [], [], [], []
    for l in range(DEPTH):
        lw = _prep_layer(l, w_proj, w_gates, gla_wa2, gla_ba, gla_norm_g, lru_conv_w, lru_conv_b, lru_wr, lru_br,
                         lru_wi, lru_bi, lru_L, swa_sinks, sgu_ln_g, sgu_ln_b, sgu_w, sgu_b, w_mem_kv, w_branch,
                         w_branch_mem, w_out, ln_g, ln_b)
        mkv = _matmul_call(mem2, lw["w_mem_kv"], bp * N_MEM, 2 * MEM_W, "memkv_%d" % l)
        mk = mkv[:, :MEM_W].reshape(bp, N_MEM, MEM_W)
        mv = mkv[:, MEM_W:].reshape(bp, N_MEM, MEM_W)
        st_p = dict(hist0=jnp.zeros((bp, SUBLANE, w), f32), h0=jnp.zeros((bp, 1, w), f32),
                    gla0=jnp.zeros((bp, GLA_HEADS, GLA_DK, GLA_DV), f32),
                    k_past=jnp.zeros((bp, WINDOW, kvw), f32), v_past=jnp.zeros((bp, WINDOW, kvw), f32),
                    mk=mk, mv=mv)
        st_s = dict(hist0=jnp.pad(state_lru_conv[l], ((0, 0), (SUBLANE - (CONV_W - 1), 0), (0, 0))),
                    h0=state_lru_h[l][:, None, :], gla0=state_gla,
                    k_past=swa_k_t, v_past=swa_v_t, mk=mem_k_t, mv=mem_v_t)
        xp, op = _layer(xp, lw, gp, st_p, l, "p%d" % l)
        xs, os_ = _layer(xs, lw, gs, st_s, l, "s%d" % l)
        outs_p.append(op)
        outs_s.append(os_)
        mks.append(mk.reshape(bp, N_MEM, MEM_HEADS, HEAD_DIM))
        mvs.append(mv.reshape(bp, N_MEM, MEM_HEADS, HEAD_DIM))

    def stack(outs, fn):
        return jnp.stack([fn(o) for o in outs])

    def window(a):
        return a.reshape(a.shape[0], WINDOW, SWA_KV_HEADS, HEAD_DIM)

    def window_t(a):
        return jnp.transpose(a, (0, 3, 1, 2))

    return (
        xp[0].reshape(bp, tp, d), xs[0].reshape(bs, ts, d),
        stack(outs_p, lambda o: o["gla"]), stack(outs_s, lambda o: o["gla"]),
        stack(outs_p, lambda o: o["hlast"][:, SUBLANE - 1]), stack(outs_s, lambda o: o["hlast"][:, SUBLANE - 1]),
        stack(outs_p, lambda o: o["hist"][:, SUBLANE - (CONV_W - 1):]),
        stack(outs_s, lambda o: o["hist"][:, SUBLANE - (CONV_W - 1):]),
        stack(outs_p, lambda o: window(o["k_last"])), stack(outs_s, lambda o: window_t(o["k_last"])),
        stack(outs_p, lambda o: window(o["v_last"])), stack(outs_s, lambda o: window_t(o["v_last"])),
        jnp.stack(mks), jnp.stack(mvs),
        stack(outs_s, lambda o: o["vn"].reshape(bs, ts, w)),
    )
```

```python
import functools
import math

import jax
import jax.numpy as jnp
import numpy as np
from jax import lax
from jax.experimental import pallas as pl
from jax.experimental.pallas import tpu as pltpu

f32 = jnp.float32
bf16 = jnp.bfloat16

D_MODEL = 1024
DEPTH = 2
PAST_LEN = 8192
BRANCH_W = 512
GLA_HEADS = 4
GLA_DK = 64
GLA_DV = 128
GLA_RANK = 16
GLA_TAU = 16.0
LRU_BLOCKS = 8
LRU_BS = 64
CONV_W = 4
LRU_C = 8.0
HEAD_DIM = 64
SWA_HEADS = 8
SWA_KV_HEADS = 2
SWA_GROUP = 4
WINDOW = 128
ROT_DIM = 16
ROPE_THETA = 500000.0
SGU_GROUPS = 4
SGU_GC = 128
SGU_CHUNK = 128
N_MEM = 256
MEM_HEADS = 4
MEM_W = 256
LN_EPS = 1e-5
RMS_EPS = 1e-6
DN_ALPHA = (2 * DEPTH) ** 0.25

LANE = 128
SUBLANE = 8

_ORIG = (("gq", 256), ("gk", 256), ("gv", 512), ("glr", 16), ("gz", 512), ("lx", 512), ("lz", 512), ("sq", 512),
         ("sk", 128), ("sv", 128), ("sz", 512), ("su", 512), ("svv", 512), ("suz", 512), ("mq", 256), ("mz", 256),
         ("gates", 5 * D_MODEL))
_ORIG_OFF = {}
_off = 0
for _n, _w in _ORIG:
    _ORIG_OFF[_n] = (_off, _w)
    _off += _w
_PROJ_RUNS = (("gq", "gv"), ("gz", "sq"), ("sz", "sz"), ("sk", "sv"), ("glr", "glr"), None, ("su", "mz"))
_SEG = {}
_off = 0
for _run in _PROJ_RUNS:
    if _run is None:
        N_PROJ = -(-_off // 1024) * 1024
        _off = N_PROJ
        continue
    _names = [n for n, _ in _ORIG]
    for _n in _names[_names.index(_run[0]):_names.index(_run[1]) + 1]:
        _w = max(_ORIG_OFF[_n][1], 128)
        assert _off % _w == 0
        _SEG[_n] = (_off, _w)
        _off += _w
N_WROWS = _off


def _dot(a, b):
    return jnp.dot(a.astype(bf16), b.astype(bf16), preferred_element_type=f32)


def _dot_nt(a, b):
    return lax.dot_general(a.astype(bf16), b.astype(bf16), (((1,), (1,)), ((), ())), preferred_element_type=f32)


def _dot_tn(a, b):
    return lax.dot_general(a.astype(bf16), b.astype(bf16), (((0,), (0,)), ((), ())), preferred_element_type=f32)


def _sigmoid(x):
    return 1.0 / (1.0 + jnp.exp(-x))


def _silu(x):
    return x * _sigmoid(x)


def _log_sigmoid(x):
    return jnp.minimum(x, 0.0) - jnp.log(1.0 + jnp.exp(-jnp.abs(x)))


def _layer_norm(x, g, b):
    mu = jnp.mean(x, axis=-1, keepdims=True)
    xc = x - mu
    var = jnp.mean(xc * xc, axis=-1, keepdims=True)
    return xc * lax.rsqrt(var + LN_EPS) * g + b


def _params(*sem):
    return pltpu.CompilerParams(dimension_semantics=sem)


def _ln_kernel(x_ref, g_ref, b_ref, ob_ref):
    ob_ref[...] = _layer_norm(x_ref[...], g_ref[...], b_ref[...]).astype(bf16)


def _ln_call(x, g, b, tm=512):
    n, d = x.shape
    tm = min(tm, n)
    return pl.pallas_call(
        _ln_kernel,
        grid=(n // tm,),
        in_specs=[pl.BlockSpec((tm, d), lambda i: (i, 0)), pl.BlockSpec((1, d), lambda i: (0, 0)),
                  pl.BlockSpec((1, d), lambda i: (0, 0))],
        out_specs=pl.BlockSpec((tm, d), lambda i: (i, 0)),
        out_shape=jax.ShapeDtypeStruct((n, d), bf16),
        compiler_params=_params("parallel"),
        name="ln_in",
    )(x, g.reshape(1, d), b.reshape(1, d))


def _matmul_kernel(x_ref, w_ref, o_ref, *, w_transposed):
    o_ref[...] = (_dot_nt if w_transposed else _dot)(x_ref[...], w_ref[...])


def _matmul_call(x, w, tm, tn, name, w_transposed=False, n_out=None, layer=None):
    m, k = x.shape
    n = n_out or (w.shape[-2] if w_transposed else w.shape[1])
    if layer is not None:
        w_spec = pl.BlockSpec((None, tn, k), lambda i, j: (layer, j, 0))
    elif w_transposed:
        w_spec = pl.BlockSpec((tn, k), lambda i, j: (j, 0))
    else:
        w_spec = pl.BlockSpec((k, tn), lambda i, j: (0, j))
    return pl.pallas_call(
        functools.partial(_matmul_kernel, w_transposed=w_transposed),
        grid=(m // tm, n // tn),
        in_specs=[pl.BlockSpec((tm, k), lambda i, j: (i, 0)), w_spec],
        out_specs=pl.BlockSpec((tm, tn), lambda i, j: (i, j)),
        out_shape=jax.ShapeDtypeStruct((m, n), f32),
        compiler_params=_params("parallel", "arbitrary"),
        name=name,
    )(x, w)


def _seg_spec(name, nseq, rows):
    off, width = _SEG[name]
    assert off + width <= N_PROJ
    cb = off // width
    return pl.BlockSpec((nseq, rows, width), lambda b, c: (b, c, cb))


def _tok_spec(nseq, rows, width):
    return pl.BlockSpec((nseq, rows, width), lambda b, c: (b, c, 0))


def _w_spec(first, last, layer):
    off = _SEG[first][0]
    rows = _SEG[last][0] + _SEG[last][1] - off
    assert off % rows == 0
    return pl.BlockSpec((None, rows, D_MODEL), lambda *_: (layer, off // rows, 0))


def _seg_scratch(name, *lead):
    return pltpu.VMEM((*lead, _SEG[name][1]), f32)


def _project(x_ref, pairs):
    x2 = x_ref[...].reshape(-1, D_MODEL)
    for w_ref, seg_refs in pairs:
        p = lax.dot_general(x2, w_ref[...], (((1,), (1,)), ((), ())), preferred_element_type=f32)
        off = 0
        for s_ref in seg_refs:
            width = s_ref.shape[-1]
            s_ref[...] = p[:, off:off + width].reshape(s_ref.shape)
            off += width


def _const_spec(shape):
    nd = len(shape)
    return pl.BlockSpec(shape, lambda b, c: (0,) * nd)


def _lru_kernel(lx_ref, lz_ref, hist0_ref, h0_ref, cw_ref, cb_ref, wr_ref, br_ref, wi_ref, bi_ref, lam_ref,
                y_ref, hlast_ref, hist_out_ref, hist_ref, hc_ref, *, nseq, tc):
    c = pl.program_id(1)
    w = BRANCH_W

    @pl.when(c == 0)
    def _():
        hist_ref[...] = hist0_ref[...]
        hc_ref[...] = h0_ref[...]

    x = lx_ref[...]
    xfull = jnp.concatenate([hist_ref[...], x], axis=1)

    def tap(j):
        return cw_ref[j:j + 1, :].reshape(1, 1, w)

    y = cb_ref[...].reshape(1, 1, w) + x * tap(CONV_W - 1)
    for s in range(1, CONV_W):
        y = y + pltpu.roll(xfull, s, 1)[:, SUBLANE:, :] * tap(CONV_W - 1 - s)
    hist_ref[...] = xfull[:, tc:, :]
    hist_out_ref[...] = xfull[:, tc:, :]

    xc = y.reshape(nseq * tc, w)
    r = _sigmoid(_dot(xc, wr_ref[...]) + br_ref[...])
    i = _sigmoid(_dot(xc, wi_ref[...]) + bi_ref[...])
    log_a = (LRU_C * r) * _log_sigmoid(lam_ref[...])
    a = jnp.exp(log_a)
    u = jnp.sqrt(jnp.tanh(-log_a) * (a * a + 1.0)) * (i * xc)

    acc_a = a.reshape(nseq, tc, w)
    acc_u = u.reshape(nseq, tc, w)
    t = lax.broadcasted_iota(jnp.int32, (nseq, tc, w), 1)
    d = 1
    while d < tc:
        if d % SUBLANE:
            ok = t >= d
            a_sh = jnp.where(ok, pltpu.roll(acc_a, d, 1), 1.0)
            u_sh = jnp.where(ok, pltpu.roll(acc_u, d, 1), 0.0)
            acc_u = acc_a * u_sh + acc_u
            acc_a = acc_a * a_sh
        else:
            new_u = acc_a[:, d:, :] * acc_u[:, :tc - d, :] + acc_u[:, d:, :]
            new_a = acc_a[:, d:, :] * acc_a[:, :tc - d, :]
            acc_u = jnp.concatenate([acc_u[:, :d, :], new_u], axis=1)
            acc_a = jnp.concatenate([acc_a[:, :d, :], new_a], axis=1)
        d *= 2
    h = acc_a * hc_ref[...] + acc_u
    hc_ref[...] = h[:, tc - 1:tc, :]
    hlast_ref[...] = h[:, tc - SUBLANE:, :]
    y_ref[...] = (h * _silu(lz_ref[...])).astype(y_ref.dtype)


def _lru_call(proj, hist0, h0, lw, nb, nseq, nc, tc, name):
    w = BRANCH_W
    kern = functools.partial(_lru_kernel, nseq=nseq, tc=tc)
    return pl.pallas_call(
        kern,
        grid=(nb, nc),
        in_specs=[_seg_spec("lx", nseq, tc), _seg_spec("lz", nseq, tc),
                  pl.BlockSpec((nseq, SUBLANE, w), lambda b, c: (b, 0, 0)),
                  pl.BlockSpec((nseq, 1, w), lambda b, c: (b, 0, 0)),
                  _const_spec((CONV_W, w)), _const_spec((1, w)), _const_spec((w, w)), _const_spec((1, w)),
                  _const_spec((w, w)), _const_spec((1, w)), _const_spec((1, w))],
        out_specs=[_tok_spec(nseq, tc, w),
                   pl.BlockSpec((nseq, SUBLANE, w), lambda b, c: (b, 0, 0)),
                   pl.BlockSpec((nseq, SUBLANE, w), lambda b, c: (b, 0, 0))],
        out_shape=[jax.ShapeDtypeStruct((nb * nseq, nc * tc, w), bf16),
                   jax.ShapeDtypeStruct((nb * nseq, SUBLANE, w), f32),
                   jax.ShapeDtypeStruct((nb * nseq, SUBLANE, w), f32)],
        scratch_shapes=[pltpu.VMEM((nseq, SUBLANE, w), f32), pltpu.VMEM((nseq, 1, w), f32)],
        compiler_params=_params("parallel", "arbitrary"),
        name=name,
    )(proj, proj, hist0, h0, lw["conv_w"], lw["conv_b"], lw["wr"], lw["br"], lw["wi"], lw["bi"], lw["lam"])


def _gla_consts(c):
    t = np.arange(c)[:, None]
    u = np.arange(c)[None, :]
    blocks = [u <= t, u > t]
    masks = [t == u]
    m = 1
    while m < c:
        t0 = (t // m) * m
        odd = (t // m) % 2 == 1
        blocks.append(odd & (u >= t0) & (u <= t))
        blocks.append((~odd) & (u > t) & (u <= t0 + m - 1))
        masks.append((t // (2 * m) == u // (2 * m)) & odd & ((u // m) % 2 == 0))
        m *= 2
    return (np.concatenate(blocks, 0).astype(np.float32), np.stack(masks).astype(np.float32))


def _gla_kernel(gq_ref, gk_ref, gv_ref, gz_ref, glr_ref, s0_ref, wa_ref, ba_ref, ng_ref, d_ref, m_ref,
                y_ref, sout_ref, s_ref, att_ref, o_ref, upd_ref, *, nseq, c):
    ci = pl.program_id(1)
    nlev = int(math.log2(c))
    hk = GLA_HEADS * GLA_DK
    heads = range(GLA_HEADS)

    @pl.when(ci == 0)
    def _():
        s_ref[...] = s0_ref[...]

    def ks(h):
        return slice(h * GLA_DK, (h + 1) * GLA_DK)

    def vs(h):
        return slice(h * GLA_DV, (h + 1) * GLA_DV)

    z = _dot(glr_ref[...].reshape(nseq * c, LANE), wa_ref[...]) + ba_ref[...]
    la = _log_sigmoid(z) * (1.0 / GLA_TAU)
    hi = la.astype(bf16)
    r1 = la - hi.astype(f32)
    mid = r1.astype(bf16)
    lo = (r1 - mid.astype(f32)).astype(bf16)
    hml = jnp.concatenate([hi, mid, lo], axis=1)

    q_in, k_st, dec_rows, qf, kf = [], [], [], [], []
    for n in range(nseq):
        hml_n = hml[n * c:(n + 1) * c]

        def decay(blk):
            if blk == 0:
                p = jnp.dot(d_ref[0:c, :], hml_n, preferred_element_type=f32)
                return jnp.exp(p[:, :hk] + p[:, hk:2 * hk] + p[:, 2 * hk:])
            p = jnp.dot(d_ref[blk * c:(blk + 1) * c, :], hml_n[:, :2 * hk], preferred_element_type=f32)
            return jnp.exp(p[:, :hk] + p[:, hk:])

        q = gq_ref[n] * (GLA_DK ** -0.5)
        k = gk_ref[n]
        eb = decay(0)
        q_in.append((q * eb).astype(bf16))
        k_st.append((k * decay(1)).astype(bf16))
        dec_rows.append(eb[c - 1:c, :])
        qf.append([q.astype(bf16)] + [(q * decay(2 + 2 * lev)).astype(bf16) for lev in range(nlev)])
        kf.append([k.astype(bf16)] + [(k * decay(3 + 2 * lev)).astype(bf16) for lev in range(nlev)])

    for n in range(nseq):
        for lev in range(nlev + 1):
            for h in heads:
                att_ref[n, h, lev] = _dot_nt(qf[n][lev][:, ks(h)], kf[n][lev][:, ks(h)])

    att = m_ref[0][None, None] * att_ref[:, :, 0]
    for lev in range(1, nlev + 1):
        att = att + m_ref[lev][None, None] * att_ref[:, :, lev]
    att = att.astype(bf16)

    for n in range(nseq):
        v = gv_ref[n].astype(bf16)
        for h in heads:
            v_h = v[:, vs(h)]
            o_ref[n, h] = _dot(q_in[n][:, ks(h)], s_ref[n, h]) + _dot(att[n, h], v_h)
            upd_ref[n, h] = _dot_tn(k_st[n][:, ks(h)], v_h)

    eye = (lax.broadcasted_iota(jnp.int32, (GLA_DK, GLA_DK), 0)
           == lax.broadcasted_iota(jnp.int32, (GLA_DK, GLA_DK), 1))[None, None]
    dec = jnp.stack([jnp.stack([dec_rows[n][:, ks(h)] for h in heads]) for n in range(nseq)])
    dec_col = jnp.sum(jnp.where(eye, jnp.broadcast_to(dec, (nseq, GLA_HEADS, GLA_DK, GLA_DK)), 0.0),
                      axis=3, keepdims=True)
    s_new = s_ref[...] * dec_col + upd_ref[...]
    s_ref[...] = s_new
    sout_ref[...] = s_new
    o = o_ref[...]
    o = o * lax.rsqrt(jnp.mean(o * o, axis=-1, keepdims=True) + RMS_EPS) * ng_ref[...]
    for n in range(nseq):
        for h in heads:
            y_ref[n, :, vs(h)] = (o[n, h] * _silu(gz_ref[n, :, vs(h)])).astype(y_ref.dtype)


_GLA_SEGS = ("gq", "gk", "gv", "gz", "glr")


def _gla_call(proj, s0, gw, nb, nseq, nc, c, name):
    dstack, masks = _gla_consts(c)
    kern = functools.partial(_gla_kernel, nseq=nseq, c=c)
    hk = GLA_HEADS * GLA_DK
    st_spec = pl.BlockSpec((nseq, GLA_HEADS, GLA_DK, GLA_DV), lambda b, ci: (b, 0, 0, 0))
    return pl.pallas_call(
        kern,
        grid=(nb, nc),
        in_specs=[_seg_spec(s, nseq, c) for s in _GLA_SEGS] + [
            st_spec, _const_spec((LANE, hk)), _const_spec((1, hk)), _const_spec((1, GLA_DV)),
            _const_spec(dstack.shape), _const_spec(masks.shape)],
        out_specs=[_tok_spec(nseq, c, BRANCH_W), st_spec],
        out_shape=[jax.ShapeDtypeStruct((nb * nseq, nc * c, BRANCH_W), bf16),
                   jax.ShapeDtypeStruct((nb * nseq, GLA_HEADS, GLA_DK, GLA_DV), f32)],
        scratch_shapes=[pltpu.VMEM((nseq, GLA_HEADS, GLA_DK, GLA_DV), f32),
                        pltpu.VMEM((nseq, GLA_HEADS, masks.shape[0], c, c), f32),
                        pltpu.VMEM((nseq, GLA_HEADS, c, GLA_DV), f32),
                        pltpu.VMEM((nseq, GLA_HEADS, GLA_DK, GLA_DV), f32)],
        compiler_params=_params("parallel", "arbitrary"),
        name=name,
    )(*([proj] * len(_GLA_SEGS)), s0, gw["wa"], gw["ba"], gw["ng"], jnp.asarray(dstack, bf16), jnp.asarray(masks))


def _gla_step_consts(t, nseq):
    dstack, masks = _gla_consts(t)
    eye = np.eye(nseq, dtype=np.float32)
    dbd = np.concatenate([np.kron(eye, dstack[i * t:(i + 1) * t]) for i in range(dstack.shape[0] // t)], axis=0)
    mbd = np.stack([np.kron(eye, m) for m in masks])
    return dbd, mbd


def _gla_step_kernel(gq_ref, gk_ref, gv_ref, gz_ref, glr_ref, s0_ref, wa_ref, ba_ref, ng_ref, d_ref, m_ref,
                     y_ref, sout_ref, *, nseq, t):
    r = nseq * t
    nlev = int(math.log2(t))
    hk = GLA_HEADS * GLA_DK
    q = gq_ref[...].reshape(r, hk) * (GLA_DK ** -0.5)
    k = gk_ref[...].reshape(r, hk)
    v = gv_ref[...].reshape(r, BRANCH_W)
    gz = gz_ref[...].reshape(r, BRANCH_W)
    z = _dot(glr_ref[...].reshape(r, LANE), wa_ref[...]) + ba_ref[...]
    la = _log_sigmoid(z) * (1.0 / GLA_TAU)
    hi = la.astype(bf16)
    r1 = la - hi.astype(f32)
    mid = r1.astype(bf16)
    lo = (r1 - mid.astype(f32)).astype(bf16)
    hml = jnp.concatenate([hi, mid, lo], axis=1)

    def decay(blk):
        p = jnp.dot(d_ref[blk * r:(blk + 1) * r, :], hml, preferred_element_type=f32)
        return jnp.exp(p[:, :hk] + p[:, hk:2 * hk] + p[:, 2 * hk:])

    eb = decay(0)
    q_in = q * eb
    k_st = k * decay(1)
    dec3 = eb.reshape(nseq, t, hk)[:, t - 1:t, :]
    qf = [q]
    kf = [k]
    for lev in range(nlev):
        qf.append(q * decay(2 + 2 * lev))
        kf.append(k * decay(3 + 2 * lev))

    own = (lax.broadcasted_iota(jnp.int32, (r, nseq * GLA_DK), 0) // t
           == lax.broadcasted_iota(jnp.int32, (r, nseq * GLA_DK), 1) // GLA_DK)
    eye = (lax.broadcasted_iota(jnp.int32, (GLA_DK, GLA_DK), 0)
           == lax.broadcasted_iota(jnp.int32, (GLA_DK, GLA_DK), 1))[None]

    def spread(x):
        x2 = jnp.concatenate([x, x], axis=1)
        return jnp.where(own, jnp.concatenate([x2] * (nseq // 2), axis=1), 0.0)

    ys = []
    for h in range(GLA_HEADS):
        ks = slice(h * GLA_DK, (h + 1) * GLA_DK)
        vs = slice(h * GLA_DV, (h + 1) * GLA_DV)
        att = jnp.zeros((r, r), f32)
        for lev in range(nlev + 1):
            att = att + m_ref[lev] * _dot_nt(qf[lev][:, ks], kf[lev][:, ks])
        s_h = s0_ref[:, h]
        v_h = v[:, vs]
        o = _dot(spread(q_in[:, ks]), s_h.reshape(nseq * GLA_DK, GLA_DV)) + _dot(att, v_h)
        upd = _dot_tn(spread(k_st[:, ks]), v_h)
        dec_col = jnp.sum(jnp.where(eye, jnp.broadcast_to(dec3[:, :, ks], (nseq, GLA_DK, GLA_DK)), 0.0),
                          axis=2, keepdims=True)
        sout_ref[:, h] = s_h * dec_col + upd.reshape(nseq, GLA_DK, GLA_DV)
        o = o * lax.rsqrt(jnp.mean(o * o, axis=-1, keepdims=True) + RMS_EPS) * ng_ref[...]
        ys.append(o * _silu(gz[:, vs]))
    y_ref[...] = jnp.concatenate(ys, axis=1).reshape(nseq, t, BRANCH_W).astype(y_ref.dtype)


def _gla_step_call(proj, s0_all, layer, gw, nb, nseq, t, name):
    dbd, mbd = _gla_step_consts(t, nseq)
    kern = functools.partial(_gla_step_kernel, nseq=nseq, t=t)
    hk = GLA_HEADS * GLA_DK
    st_spec = pl.BlockSpec((nseq, GLA_HEADS, GLA_DK, GLA_DV), lambda b, ci: (b, 0, 0, 0))
    s0_spec = pl.BlockSpec((None, nseq, GLA_HEADS, GLA_DK, GLA_DV), lambda b, ci: (layer, b, 0, 0, 0))
    return pl.pallas_call(
        kern,
        grid=(nb, 1),
        in_specs=[_seg_spec(s, nseq, t) for s in _GLA_SEGS] + [
            s0_spec, _const_spec((LANE, hk)), _const_spec((1, hk)), _const_spec((1, GLA_DV)),
            _const_spec(dbd.shape), _const_spec(mbd.shape)],
        out_specs=[_tok_spec(nseq, t, BRANCH_W), st_spec],
        out_shape=[jax.ShapeDtypeStruct((nb * nseq, t, BRANCH_W), bf16),
                   jax.ShapeDtypeStruct((nb * nseq, GLA_HEADS, GLA_DK, GLA_DV), f32)],
        compiler_params=_params("parallel", "arbitrary"),
        name=name,
    )(*([proj] * len(_GLA_SEGS)), s0_all, gw["wa"], gw["ba"], gw["ng"], jnp.asarray(dbd, bf16), jnp.asarray(mbd))


def _rope_tables(pos0, t):
    half = ROT_DIM // 2
    dim = jnp.arange(LANE) % HEAD_DIM
    inv = ROPE_THETA ** (-(dim % half).astype(f32) / half)
    ang = (pos0 + jnp.arange(t)).astype(f32)[:, None] * inv[None, :]
    cos, sin = jnp.cos(ang), jnp.sin(ang)
    first, second = (dim < half)[None, :], ((dim >= half) & (dim < ROT_DIM))[None, :]
    c_tab = jnp.where(first | second, cos, 1.0)
    sa_tab = jnp.where(first, -sin, 0.0)
    sb_tab = jnp.where(second, sin, 0.0)
    return c_tab, sa_tab, sb_tab


def _rope(x, c_tab, sa_tab, sb_tab):
    wd = x.shape[-1]
    ax = x.ndim - 1
    rep = wd // LANE
    half = ROT_DIM // 2
    if rep > 1:
        c_tab, sa_tab, sb_tab = (jnp.concatenate([tb] * rep, axis=-1) for tb in (c_tab, sa_tab, sb_tab))
    return x * c_tab + pltpu.roll(x, wd - half, ax) * sa_tab + pltpu.roll(x, half, ax) * sb_tab


def _swa_kernel(sink_ref, sq_ref, sz_ref, sk_ref, sv_ref, ct_ref, sat_ref, sbt_ref, kp_ref, vp_ref,
                y_ref, klast_ref, vlast_ref, kprev_ref, vprev_ref, sp_ref, sc_ref, o_ref, den_ref,
                *, nseq, qb, pos0):
    blk = pl.program_id(1)
    hd = HEAD_DIM

    @pl.when(blk == 0)
    def _():
        kprev_ref[...] = kp_ref[...]
        vprev_ref[...] = vp_ref[...]

    tabs = tuple(r[...][None] for r in (ct_ref, sat_ref, sbt_ref))
    q3 = _rope(sq_ref[...], *tabs) * (hd ** -0.5)
    k3 = _rope(sk_ref[...], *tabs)
    v3 = sv_ref[...]
    klast_ref[...] = k3
    vlast_ref[...] = v3

    for n in range(nseq):
        for kv in range(SWA_KV_HEADS):
            ds = slice(kv * hd, (kv + 1) * hd)
            qs = jnp.concatenate(
                [q3[n][:, (kv * SWA_GROUP + g) * hd:(kv * SWA_GROUP + g + 1) * hd] for g in range(SWA_GROUP)],
                axis=0).astype(bf16)
            sp_ref[n, kv] = _dot_nt(qs, kprev_ref[n][:, ds])
            sc_ref[n, kv] = _dot_nt(qs, k3[n][:, ds])

    mrows = SWA_GROUP * qb
    qi = lax.broadcasted_iota(jnp.int32, (mrows, WINDOW), 0) % qb
    kj = lax.broadcasted_iota(jnp.int32, (mrows, WINDOW), 1)
    past_ok = (kj >= qi) & (kj >= (WINDOW - pos0) - blk * qb)
    cur_ok = kj <= qi
    sink = jnp.stack([jnp.concatenate([jnp.full((qb, 1), sink_ref[kv * SWA_GROUP + g], f32)
                                       for g in range(SWA_GROUP)], axis=0) for kv in range(SWA_KV_HEADS)])[None]
    s_p = jnp.where(past_ok[None, None], sp_ref[...], -jnp.inf)
    s_c = jnp.where(cur_ok[None, None], sc_ref[...], -jnp.inf)
    m = jnp.maximum(jnp.max(jnp.maximum(s_p, s_c), axis=3, keepdims=True), sink)
    p_p = jnp.exp(s_p - m).astype(bf16)
    p_c = jnp.exp(s_c - m).astype(bf16)
    ones = jnp.ones((WINDOW, LANE), bf16)
    for n in range(nseq):
        vp = vprev_ref[n].astype(bf16)
        vc = v3[n].astype(bf16)
        for kv in range(SWA_KV_HEADS):
            o_ref[n, kv] = _dot(p_p[n, kv], vp) + _dot(p_c[n, kv], vc)
            den_ref[n, kv] = _dot(p_p[n, kv], ones) + _dot(p_c[n, kv], ones)
    o = o_ref[...] / (den_ref[...] + jnp.exp(sink - m))
    outs = [o[:, j // SWA_GROUP, (j % SWA_GROUP) * qb:(j % SWA_GROUP + 1) * qb,
              (j // SWA_GROUP) * hd:(j // SWA_GROUP + 1) * hd] for j in range(SWA_HEADS)]
    y_ref[...] = (jnp.concatenate(outs, axis=2) * _silu(sz_ref[...])).astype(y_ref.dtype)
    kprev_ref[...] = k3
    vprev_ref[...] = v3


_SWA_SEGS = ("sq", "sz", "sk", "sv")


def _swa_call(proj, sinks, k_past, v_past, pos0, nb, nseq, nc, qb, name):
    assert qb == WINDOW
    t_total = nc * qb
    c_tab, sa_tab, sb_tab = _rope_tables(pos0, t_total)
    kern = functools.partial(_swa_kernel, nseq=nseq, qb=qb, pos0=pos0)
    kvw = SWA_KV_HEADS * HEAD_DIM
    mrows = SWA_GROUP * qb
    tab_spec = pl.BlockSpec((qb, LANE), lambda b, c: (c, 0))
    past_spec = pl.BlockSpec((nseq, WINDOW, kvw), lambda b, c: (b, 0, 0))
    return pl.pallas_call(
        kern,
        grid=(nb, nc),
        in_specs=[pl.BlockSpec(memory_space=pltpu.SMEM)] + [_seg_spec(s, nseq, qb) for s in _SWA_SEGS] + [
            tab_spec, tab_spec, tab_spec, past_spec, past_spec],
        out_specs=[_tok_spec(nseq, qb, BRANCH_W), past_spec, past_spec],
        out_shape=[jax.ShapeDtypeStruct((nb * nseq, nc * qb, BRANCH_W), bf16),
                   jax.ShapeDtypeStruct((nb * nseq, WINDOW, kvw), f32),
                   jax.ShapeDtypeStruct((nb * nseq, WINDOW, kvw), f32)],
        scratch_shapes=[pltpu.VMEM((nseq, WINDOW, kvw), f32), pltpu.VMEM((nseq, WINDOW, kvw), f32),
                        pltpu.VMEM((nseq, SWA_KV_HEADS, mrows, WINDOW), f32),
                        pltpu.VMEM((nseq, SWA_KV_HEADS, mrows, qb), f32),
                        pltpu.VMEM((nseq, SWA_KV_HEADS, mrows, kvw), f32),
                        pltpu.VMEM((nseq, SWA_KV_HEADS, mrows, kvw), f32)],
        compiler_params=_params("parallel", "arbitrary"),
        name=name,
    )(sinks, *([proj] * len(_SWA_SEGS)), c_tab, sa_tab, sb_tab, k_past, v_past)


def _swa_step_kernel(sink_ref, sq_ref, sz_ref, sk_ref, sv_ref, ct_ref, sat_ref, sbt_ref, kp_ref, vp_ref,
                     y_ref, klast_ref, vlast_ref, sp_ref, sc_ref, o_ref, *, nseq, t, pos0):
    hd = HEAD_DIM
    kvw = SWA_KV_HEADS * hd
    mrows = SWA_HEADS * t
    tabs = tuple(r[...][None] for r in (ct_ref, sat_ref, sbt_ref))
    q3 = _rope(sq_ref[...], *tabs) * (hd ** -0.5)
    k3 = _rope(sk_ref[...], *tabs)
    v3 = sv_ref[...]

    lane = lax.broadcasted_iota(jnp.int32, (kvw, WINDOW), 1)
    pad = jnp.zeros((WINDOW - t, kvw), f32)

    def shifted(old_t, new):
        new_t = jnp.concatenate([pad, new], axis=0).T
        out = jnp.where(lane >= WINDOW - t, new_t, pltpu.roll(old_t, WINDOW - t, 1))
        return out.reshape(SWA_KV_HEADS, hd, WINDOW)

    for n in range(nseq):
        klast_ref[n] = shifted(kp_ref[n].reshape(kvw, WINDOW), k3[n])
        vlast_ref[n] = shifted(vp_ref[n].reshape(kvw, WINDOW), v3[n])

    zero = jnp.zeros((nseq, t, hd), f32)
    pieces = []
    for j in range(SWA_HEADS):
        qj = q3[:, :, j * hd:(j + 1) * hd]
        pieces.append(jnp.concatenate([qj, zero] if j // SWA_GROUP == 0 else [zero, qj], axis=2))
    qbd = jnp.concatenate(pieces, axis=1).astype(bf16)

    for n in range(nseq):
        sp_ref[n] = _dot(qbd[n], kp_ref[n].reshape(kvw, WINDOW))
        sc_ref[n] = _dot_nt(qbd[n], k3[n])

    qi = lax.broadcasted_iota(jnp.int32, (mrows, WINDOW), 0) % t
    kj = lax.broadcasted_iota(jnp.int32, (mrows, WINDOW), 1)
    past_ok = kj >= qi
    if pos0 < WINDOW:
        past_ok = past_ok & (kj >= WINDOW - pos0)
    qi_c = lax.broadcasted_iota(jnp.int32, (mrows, t), 0) % t
    kj_c = lax.broadcasted_iota(jnp.int32, (mrows, t), 1)
    cur_ok = kj_c <= qi_c
    sink = jnp.concatenate([jnp.full((t, 1), sink_ref[j], f32) for j in range(SWA_HEADS)], axis=0)[None]
    s_p = jnp.where(past_ok[None], sp_ref[...], -jnp.inf)
    s_c = jnp.where(cur_ok[None], sc_ref[...], -jnp.inf)
    m = jnp.maximum(jnp.maximum(jnp.max(s_p, axis=2, keepdims=True), jnp.max(s_c, axis=2, keepdims=True)), sink)
    p_p = jnp.exp(s_p - m)
    p_c = jnp.exp(s_c - m)
    den = jnp.sum(p_p, axis=2, keepdims=True) + jnp.sum(p_c, axis=2, keepdims=True) + jnp.exp(sink - m)
    p_p = p_p.astype(bf16)
    p_c = p_c.astype(bf16)
    for n in range(nseq):
        o_ref[n] = _dot_nt(p_p[n], vp_ref[n].reshape(kvw, WINDOW)) + _dot(p_c[n], v3[n])
    o = o_ref[...] / den
    outs = []
    for j in range(SWA_HEADS):
        kv = j // SWA_GROUP
        outs.append(o[:, j * t:(j + 1) * t, kv * hd:(kv + 1) * hd])
    y_ref[...] = (jnp.concatenate(outs, axis=2) * _silu(sz_ref[...])).astype(y_ref.dtype)


def _swa_step_call(proj, sinks, k_past_t, v_past_t, layer, pos0, nb, nseq, t, name):
    c_tab, sa_tab, sb_tab = _rope_tables(pos0, t)
    kern = functools.partial(_swa_step_kernel, nseq=nseq, t=t, pos0=pos0)
    kvw = SWA_KV_HEADS * HEAD_DIM
    mrows = SWA_HEADS * t
    tab_spec = pl.BlockSpec((t, LANE), lambda b, c: (0, 0))
    past_spec = pl.BlockSpec((None, nseq, SWA_KV_HEADS, HEAD_DIM, WINDOW), lambda b, c: (layer, b, 0, 0, 0))
    new_spec = pl.BlockSpec((nseq, SWA_KV_HEADS, HEAD_DIM, WINDOW), lambda b, c: (b, 0, 0, 0))
    new_shape = jax.ShapeDtypeStruct((nb * nseq, SWA_KV_HEADS, HEAD_DIM, WINDOW), f32)
    return pl.pallas_call(
        kern,
        grid=(nb, 1),
        in_specs=[pl.BlockSpec(memory_space=pltpu.SMEM)] + [_seg_spec(s, nseq, t) for s in _SWA_SEGS] + [
            tab_spec, tab_spec, tab_spec, past_spec, past_spec],
        out_specs=[_tok_spec(nseq, t, BRANCH_W), new_spec, new_spec],
        out_shape=[jax.ShapeDtypeStruct((nb * nseq, t, BRANCH_W), bf16), new_shape, new_shape],
        scratch_shapes=[pltpu.VMEM((nseq, mrows, WINDOW), f32), pltpu.VMEM((nseq, mrows, t), f32),
                        pltpu.VMEM((nseq, mrows, kvw), f32)],
        compiler_params=_params("parallel", "arbitrary"),
        name=name,
    )(sinks, *([proj] * len(_SWA_SEGS)), c_tab, sa_tab, sb_tab, k_past_t, v_past_t)


def _sgu_kernel(x_ref, wu_ref, wv_ref, wz_ref, g_ref, b_ref, wm_ref, bias_ref, y_ref, *rest, ntile, want_vn):
    vn_ref = rest[0] if want_vn else None
    su_ref, sv_ref, sz_ref = rest[-3:]
    _project(x_ref, ((wu_ref, (su_ref,)), (wv_ref, (sv_ref,)), (wz_ref, (sz_ref,))))
    vn = _layer_norm(sv_ref[...], g_ref[...], b_ref[...])
    if want_vn:
        vn_ref[...] = vn
    for r in range(ntile):
        rows = slice(r * SGU_CHUNK, (r + 1) * SGU_CHUNK)
        mixed = jnp.concatenate(
            [jnp.dot(wm_ref[g], vn[rows, g * SGU_GC:(g + 1) * SGU_GC].astype(bf16), preferred_element_type=f32)
             for g in range(SGU_GROUPS)], axis=1)
        y = su_ref[rows, :] * (mixed + bias_ref[...]) * _silu(sz_ref[rows, :])
        y_ref[rows, :] = y.astype(y_ref.dtype)


def _sgu_call(xb, w_t, layer, ln_g, ln_b, wmix, bias, n_tok, ntile, want_vn, name):
    rows = ntile * SGU_CHUNK
    w = BRANCH_W
    kern = functools.partial(_sgu_kernel, ntile=ntile, want_vn=want_vn)

    def const(shape):
        return pl.BlockSpec(shape, lambda i: (0,) * len(shape))

    out_specs = [pl.BlockSpec((rows, w), lambda i: (i, 0))]
    out_shape = [jax.ShapeDtypeStruct((n_tok, w), bf16)]
    if want_vn:
        out_specs.append(pl.BlockSpec((rows, w), lambda i: (i, 0)))
        out_shape.append(jax.ShapeDtypeStruct((n_tok, w), f32))
    return pl.pallas_call(
        kern,
        grid=(n_tok // rows,),
        in_specs=[pl.BlockSpec((rows, D_MODEL), lambda i: (i, 0)), _w_spec("su", "su", layer),
                  _w_spec("svv", "svv", layer), _w_spec("suz", "suz", layer), const((1, w)), const((1, w)),
                  const((SGU_GROUPS, SGU_CHUNK, SGU_CHUNK)), const((SGU_CHUNK, w))],
        out_specs=out_specs,
        out_shape=out_shape,
        scratch_shapes=[_seg_scratch(s, rows) for s in ("su", "svv", "suz")],
        compiler_params=_params("parallel"),
        name=name,
    )(xb, w_t, w_t, w_t, ln_g, ln_b, wmix, bias)


def _mem_kernel(x_ref, wm_ref, mk_ref, mv_ref, y_ref, mq_ref, mz_ref, s_ref, o_ref, den_ref, *, nseq, tq):
    _project(x_ref, ((wm_ref, (mq_ref, mz_ref)),))
    heads = range(MEM_HEADS)
    lane_head = lax.broadcasted_iota(jnp.int32, (tq, MEM_W), 1) // HEAD_DIM
    for n in range(nseq):
        q = mq_ref[n] * (HEAD_DIM ** -0.5)
        mk = mk_ref[n].astype(bf16)
        for h in heads:
            s_ref[n, h] = _dot_nt(jnp.where(lane_head == h, q, 0.0), mk)
    s = s_ref[...]
    p = jnp.exp(s - jnp.max(s, axis=3, keepdims=True)).astype(bf16)
    ones = jnp.ones((N_MEM, MEM_W), bf16)
    for n in range(nseq):
        mv = mv_ref[n].astype(bf16)
        for h in heads:
            o_ref[n, h] = _dot(p[n, h], mv)
            den_ref[n, h] = _dot(p[n, h], ones)
    o = o_ref[...] / den_ref[...]
    acc = jnp.where(lane_head == 0, o[:, 0], 0.0)
    for h in range(1, MEM_HEADS):
        acc = acc + jnp.where(lane_head == h, o[:, h], 0.0)
    y_ref[...] = (acc * _silu(mz_ref[...])).astype(y_ref.dtype)


def _mem_call(xb, w_t, layer, mk, mv, nb, nseq, nc, tq, name):
    kern = functools.partial(_mem_kernel, nseq=nseq, tq=tq)
    kv_spec = pl.BlockSpec((nseq, N_MEM, MEM_W), lambda b, c: (b, 0, 0))
    return pl.pallas_call(
        kern,
        grid=(nb, nc),
        in_specs=[_tok_spec(nseq, tq, D_MODEL), _w_spec("mq", "mz", layer), kv_spec, kv_spec],
        out_specs=_tok_spec(nseq, tq, MEM_W),
        out_shape=jax.ShapeDtypeStruct((nb * nseq, nc * tq, MEM_W), bf16),
        scratch_shapes=[_seg_scratch("mq", nseq, tq), _seg_scratch("mz", nseq, tq),
                        pltpu.VMEM((nseq, MEM_HEADS, tq, N_MEM), f32), pltpu.VMEM((nseq, MEM_HEADS, tq, MEM_W), f32),
                        pltpu.VMEM((nseq, MEM_HEADS, tq, MEM_W), f32)],
        compiler_params=_params("parallel", "arbitrary"),
        name=name,
    )(xb, w_t, mk, mv)


def _mem_step_kernel(x_ref, wm_ref, mk_ref, mv_ref, y_ref, s_ref, o_ref, mq_ref, mz_ref, *, nseq, t):
    mrows = MEM_HEADS * t
    _project(x_ref, ((wm_ref, (mq_ref, mz_ref)),))
    row_head = lax.broadcasted_iota(jnp.int32, (mrows, MEM_W), 0) // t
    lane_head = lax.broadcasted_iota(jnp.int32, (mrows, MEM_W), 1) // HEAD_DIM
    own = (row_head == lane_head)[None]
    q3 = mq_ref[...] * (HEAD_DIM ** -0.5)
    qbd = jnp.where(own, jnp.concatenate([q3] * MEM_HEADS, axis=1), 0.0).astype(bf16)
    for n in range(nseq):
        s_ref[n] = _dot(qbd[n], mk_ref[n].reshape(MEM_W, N_MEM))
    s = s_ref[...]
    p = jnp.exp(s - jnp.max(s, axis=2, keepdims=True))
    den = jnp.sum(p, axis=2, keepdims=True)
    p = p.astype(bf16)
    for n in range(nseq):
        o_ref[n] = _dot_nt(p[n], mv_ref[n].reshape(MEM_W, N_MEM))
    o = jnp.where(own, o_ref[...] / den, 0.0)
    acc = o[:, 0:t, :]
    for h in range(1, MEM_HEADS):
        acc = acc + o[:, h * t:(h + 1) * t, :]
    y_ref[...] = (acc * _silu(mz_ref[...])).astype(y_ref.dtype)


def _mem_step_call(xb, w_t, mk_t, mv_t, layer, nb, nseq, t, name):
    kern = functools.partial(_mem_step_kernel, nseq=nseq, t=t)
    kv_spec = pl.BlockSpec((None, nseq, MEM_HEADS, HEAD_DIM, N_MEM), lambda b, c: (layer, b, 0, 0, 0))
    mrows = MEM_HEADS * t
    return pl.pallas_call(
        kern,
        grid=(nb, 1),
        in_specs=[_tok_spec(nseq, t, D_MODEL), _w_spec("mq", "mz", layer), kv_spec, kv_spec],
        out_specs=_tok_spec(nseq, t, MEM_W),
        out_shape=jax.ShapeDtypeStruct((nb * nseq, t, MEM_W), bf16),
        scratch_shapes=[pltpu.VMEM((nseq, mrows, N_MEM), f32), pltpu.VMEM((nseq, mrows, MEM_W), f32),
                        _seg_scratch("mq", nseq, t), _seg_scratch("mz", nseq, t)],
        compiler_params=_params("parallel", "arbitrary"),
        name=name,
    )(xb, w_t, mk_t, mv_t)


def _merge_kernel(yg_ref, yl_ref, ys_ref, yu_ref, ym_ref, x_ref, ig_ref, ib_ref, wg_ref, wb_ref, wm_ref, wo_ref,
                  g_ref, b_ref, o_ref, ob_ref, *, pre_ln):
    d = D_MODEL
    x = x_ref[...]
    if pre_ln:
        x = _layer_norm(x, ig_ref[...], ib_ref[...])
    xb = x.astype(bf16)

    def gate(n):
        return _sigmoid(_dot_nt(xb, wg_ref[n * d:(n + 1) * d, :]))

    merged = gate(4) * jnp.dot(ym_ref[...], wm_ref[...], preferred_element_type=f32)
    for n, y_ref in enumerate((yg_ref, yl_ref, ys_ref, yu_ref)):
        merged = merged + gate(n) * jnp.dot(y_ref[...], wb_ref[n], preferred_element_type=f32)
    out = _dot(merged, wo_ref[...])
    y = _layer_norm(DN_ALPHA * x + out, g_ref[...], b_ref[...])
    o_ref[...] = y
    ob_ref[...] = y.astype(bf16)


def _merge_call(ys, x, ln_in, mw, layer, tm, name):
    n_tok, d = x.shape
    w = BRANCH_W

    def rows(width):
        return pl.BlockSpec((tm, width), lambda i: (i, 0))

    def const(shape):
        return pl.BlockSpec(shape, lambda i: (0,) * len(shape), pipeline_mode=pl.Buffered(1))

    gates_spec = pl.BlockSpec((None, 5 * d, d), lambda i: (layer, 0, 0), pipeline_mode=pl.Buffered(1))
    return pl.pallas_call(
        functools.partial(_merge_kernel, pre_ln=layer == 0),
        grid=(n_tok // tm,),
        in_specs=[rows(w), rows(w), rows(w), rows(w), rows(MEM_W), rows(d), const((1, d)), const((1, d)),
                  gates_spec, const((4, w, d)), const((MEM_W, d)), const((d, d)), const((1, d)),
                  const((1, d))],
        out_specs=[rows(d), rows(d)],
        out_shape=[jax.ShapeDtypeStruct((n_tok, d), f32), jax.ShapeDtypeStruct((n_tok, d), bf16)],
        compiler_params=_params("parallel"),
        name=name,
    )(*ys, x, ln_in[0].reshape(1, d), ln_in[1].reshape(1, d), mw["wg"], mw["wb"], mw["wm"], mw["wo"], mw["g"],
      mw["b"])


def _prep_w_in(w_in):
    w_t = jnp.swapaxes(w_in, 1, 2).astype(bf16)
    runs = []
    used = 0
    for run in _PROJ_RUNS:
        if run is None:
            runs.append(jnp.zeros((DEPTH, N_PROJ - used, D_MODEL), bf16))
            used = N_PROJ
            continue
        lo = _ORIG_OFF[run[0]][0]
        hi = _ORIG_OFF[run[1]][0] + _ORIG_OFF[run[1]][1]
        runs.append(w_t[:, lo:hi])
        used += hi - lo
    assert used == N_WROWS
    return jnp.concatenate(runs, axis=1), w_t[:, _ORIG_OFF["gates"][0]:]


def _prep_layer(l, w_proj, w_gates, gla_wa2, gla_ba, gla_norm_g, lru_conv_w, lru_conv_b, lru_wr, lru_br, lru_wi,
                lru_bi, lru_L, swa_sinks, sgu_ln_g, sgu_ln_b, sgu_w, sgu_b, w_mem_kv, w_branch, w_branch_mem, w_out,
                ln_g, ln_b):
    d = D_MODEL
    w = BRANCH_W

    def block_diag(wb):
        eye = jnp.eye(LRU_BLOCKS, dtype=f32)
        return (eye[:, None, :, None] * wb[:, :, None, :]).reshape(w, w).astype(bf16)

    tril = jnp.tril(jnp.ones((SGU_CHUNK, SGU_CHUNK), f32))
    wmix_p = (sgu_w[l] * tril).astype(bf16)
    bias_p = jnp.repeat(sgu_b[l].T, SGU_GC, axis=1)
    t8 = SUBLANE
    rep = SGU_CHUNK // t8
    w8 = (sgu_w[l] * tril)[:, :t8, :t8]
    seq_eye = jnp.eye(rep, dtype=f32)
    wmix_s = (seq_eye[None, :, None, :, None] * w8[:, None, :, None, :]).reshape(
        SGU_GROUPS, SGU_CHUNK, SGU_CHUNK).astype(bf16)
    bias_s = jnp.tile(bias_p[:t8], (rep, 1))
    return dict(
        w_proj=w_proj,
        w_mem_kv=w_mem_kv[l].astype(bf16),
        gla=dict(wa=jnp.pad(gla_wa2[l], ((0, LANE - GLA_RANK), (0, 0))).astype(bf16),
                 ba=gla_ba[l].reshape(1, -1), ng=gla_norm_g[l].reshape(1, -1)),
        lru=dict(conv_w=lru_conv_w[l], conv_b=lru_conv_b[l].reshape(1, w), wr=block_diag(lru_wr[l]),
                 br=lru_br[l].reshape(1, w), wi=block_diag(lru_wi[l]), bi=lru_bi[l].reshape(1, w),
                 lam=lru_L[l].reshape(1, w)),
        sinks=swa_sinks[l],
        sgu=dict(g=sgu_ln_g[l].reshape(1, w), b=sgu_ln_b[l].reshape(1, w), wmix_p=wmix_p, bias_p=bias_p,
                 wmix_s=wmix_s, bias_s=bias_s),
        merge=dict(wg=w_gates, wb=w_branch[l].astype(bf16), wm=w_branch_mem[l].astype(bf16), wo=w_out[l].astype(bf16),
                   g=ln_g[l].reshape(1, d), b=ln_b[l].reshape(1, d)),
    )


def _layer(x, lw, grp, st, layer, tag):
    nseq_total, t = grp["batch"], grp["seq"]
    n_tok = nseq_total * t
    x, xb = x
    xb3 = xb.reshape(nseq_total, t, D_MODEL)
    w_t = lw["w_proj"]
    proj = _matmul_call(xb, w_t, min(grp["proj_tm"], n_tok), 1024, "proj_" + tag, w_transposed=True, n_out=N_PROJ,
                        layer=layer)
    proj3 = proj.reshape(nseq_total, t, N_PROJ)

    lt = grp["lru"]
    y_lru, hlast, hist = _lru_call(proj3, st["hist0"], st["h0"], lw["lru"], nseq_total // lt[0], lt[0], t // lt[1],
                                   lt[1], "lru_" + tag)
    short = grp["kind"] == "s"
    gt = grp["gla"]
    if short:
        y_gla, s_out = _gla_step_call(proj3, st["gla0"], layer, lw["gla"], nseq_total // gt[0], gt[0], t,
                                      "gla_" + tag)
    else:
        y_gla, s_out = _gla_call(proj3, st["gla0"], lw["gla"], nseq_total // gt[0], gt[0], t // gt[1], gt[1],
                                 "gla_" + tag)
    wt = grp["swa"]
    if short:
        y_swa, k_last, v_last = _swa_step_call(proj3, lw["sinks"], st["k_past"], st["v_past"], layer, grp["pos0"],
                                               nseq_total // wt[0], wt[0], t, "swa_" + tag)
    else:
        y_swa, k_last, v_last = _swa_call(proj3, lw["sinks"], st["k_past"], st["v_past"], grp["pos0"],
                                          nseq_total // wt[0], wt[0], t // wt[1], wt[1], "swa_" + tag)
    sg = lw["sgu"]
    sgu_out = _sgu_call(xb, w_t, layer, sg["g"], sg["b"], sg["wmix_" + grp["kind"]], sg["bias_" + grp["kind"]], n_tok,
                        grp["sgu_tiles"], grp["kind"] == "s", "sgu_" + tag)
    mt = grp["mem"]
    if short:
        y_mem = _mem_step_call(xb3, w_t, st["mk"], st["mv"], layer, nseq_total // mt[0], mt[0], t, "mem_" + tag)
    else:
        y_mem = _mem_call(xb3, w_t, layer, st["mk"], st["mv"], nseq_total // mt[0], mt[0], t // mt[1], mt[1],
                          "mem_" + tag)
    ys = tuple(y.reshape(n_tok, y.shape[-1]) for y in (y_gla, y_lru, y_swa, sgu_out[0], y_mem))
    x_new = _merge_call(ys, x, lw["ln_in"], lw["merge"], layer, min(256, n_tok), "merge_" + tag)
    return x_new, dict(gla=s_out, hlast=hlast, hist=hist, k_last=k_last, v_last=v_last,
                       vn=sgu_out[1] if len(sgu_out) > 1 else None)


_PROMPT = dict(kind="p", pos0=0, proj_tm=2048, lru=(1, 256), gla=(4, 128), swa=(4, 128), sgu_tiles=4, mem=(1, 512))
_SAMPLE = dict(kind="s", pos0=PAST_LEN, proj_tm=1024, lru=(32, 8), gla=(16, 8), swa=(16, 8), sgu_tiles=8,
               mem=(16, 8))


def kernel(x_prompt, x_sample, mem_prompt, state_gla, state_lru_h, state_lru_conv, cache_swa_k, cache_swa_v,
           cache_mem_k, cache_mem_v, ln_in_g, ln_in_b, w_in, gla_wa2, gla_ba, gla_norm_g, lru_conv_w, lru_conv_b,
           lru_wr, lru_br, lru_wi, lru_bi, lru_L, swa_sinks, sgu_ln_g, sgu_ln_b, sgu_w, sgu_b, w_mem_kv, w_branch,
           w_branch_mem, w_out, ln_g, ln_b):
    bp, tp, d = x_prompt.shape
    bs, ts, _ = x_sample.shape
    w = BRANCH_W
    kvw = SWA_KV_HEADS * HEAD_DIM
    gp = dict(_PROMPT, batch=bp, seq=tp)
    gs = dict(_SAMPLE, batch=bs, seq=ts)

    xp = (x_prompt.reshape(bp * tp, d), _ln_call(x_prompt.reshape(bp * tp, d), ln_in_g, ln_in_b))
    xs = (x_sample.reshape(bs * ts, d), _ln_call(x_sample.reshape(bs * ts, d), ln_in_g, ln_in_b))
    mem2 = mem_prompt.reshape(bp * N_MEM, d)

    swa_k_t, swa_v_t, mem_k_t, mem_v_t = (jnp.transpose(c, (0, 1, 3, 4, 2))
                                          for c in (cache_swa_k, cache_swa_v, cache_mem_k, cache_mem_v))

    w_proj, w_gates = _prep_w_in(w_in)
    outs_p, outs_s, mks, mvs = [], [], [], []
    for l in range(DEPTH):
        lw = _prep_layer(l, w_proj, w_gates, gla_wa2, gla_ba, gla_norm_g, lru_conv_w, lru_conv_b, lru_wr, lru_br,
                         lru_wi, lru_bi, lru_L, swa_sinks, sgu_ln_g, sgu_ln_b, sgu_w, sgu_b, w_mem_kv, w_branch,
                         w_branch_mem, w_out, ln_g, ln_b)
        lw["ln_in"] = (ln_in_g, ln_in_b)
        mkv = _matmul_call(mem2, lw["w_mem_kv"], bp * N_MEM, 2 * MEM_W, "memkv_%d" % l)
        mk = mkv[:, :MEM_W].reshape(bp, N_MEM, MEM_W)
        mv = mkv[:, MEM_W:].reshape(bp, N_MEM, MEM_W)
        st_p = dict(hist0=jnp.zeros((bp, SUBLANE, w), f32), h0=jnp.zeros((bp, 1, w), f32),
                    gla0=jnp.zeros((bp, GLA_HEADS, GLA_DK, GLA_DV), f32),
                    k_past=jnp.zeros((bp, WINDOW, kvw), f32), v_past=jnp.zeros((bp, WINDOW, kvw), f32),
                    mk=mk, mv=mv)
        st_s = dict(hist0=jnp.pad(state_lru_conv[l], ((0, 0), (SUBLANE - (CONV_W - 1), 0), (0, 0))),
                    h0=state_lru_h[l][:, None, :], gla0=state_gla,
                    k_past=swa_k_t, v_past=swa_v_t, mk=mem_k_t, mv=mem_v_t)
        xp, op = _layer(xp, lw, gp, st_p, l, "p%d" % l)
        xs, os_ = _layer(xs, lw, gs, st_s, l, "s%d" % l)
        outs_p.append(op)
        outs_s.append(os_)
        mks.append(mk.reshape(bp, N_MEM, MEM_HEADS, HEAD_DIM))
        mvs.append(mv.reshape(bp, N_MEM, MEM_HEADS, HEAD_DIM))

    def stack(outs, fn):
        return jnp.stack([fn(o) for o in outs])

    def window(a):
        return a.reshape(a.shape[0], WINDOW, SWA_KV_HEADS, HEAD_DIM)

    def window_t(a):
        return jnp.transpose(a, (0, 3, 1, 2))

    return (
        xp[0].reshape(bp, tp, d), xs[0].reshape(bs, ts, d),
        stack(outs_p, lambda o: o["gla"]), stack(outs_s, lambda o: o["gla"]),
        stack(outs_p, lambda o: o["hlast"][:, SUBLANE - 1]), stack(outs_s, lambda o: o["hlast"][:, SUBLANE - 1]),
        stack(outs_p, lambda o: o["hist"][:, SUBLANE - (CONV_W - 1):]),
        stack(outs_s, lambda o: o["hist"][:, SUBLANE - (CONV_W - 1):]),
        stack(outs_p, lambda o: window(o["k_last"])), stack(outs_s, lambda o: window_t(o["k_last"])),
        stack(outs_p, lambda o: window(o["v_last"])), stack(outs_s, lambda o: window_t(o["v_last"])),
        jnp.stack(mks), jnp.stack(mvs),
        stack(outs_s, lambda o: o["vn"].reshape(bs, ts, w)),
    )
```

```python
import functools
import math

import jax
import jax.numpy as jnp
import numpy as np
from jax import lax
from jax.experimental import pallas as pl
from jax.experimental.pallas import tpu as pltpu

f32 = jnp.float32
bf16 = jnp.bfloat16

D_MODEL = 1024
DEPTH = 2
PAST_LEN = 8192
BRANCH_W = 512
GLA_HEADS = 4
GLA_DK = 64
GLA_DV = 128
GLA_RANK = 16
GLA_TAU = 16.0
LRU_BLOCKS = 8
LRU_BS = 64
CONV_W = 4
LRU_C = 8.0
HEAD_DIM = 64
SWA_HEADS = 8
SWA_KV_HEADS = 2
SWA_GROUP = 4
WINDOW = 128
ROT_DIM = 16
ROPE_THETA = 500000.0
SGU_GROUPS = 4
SGU_GC = 128
SGU_CHUNK = 128
N_MEM = 256
MEM_HEADS = 4
MEM_W = 256
LN_EPS = 1e-5
RMS_EPS = 1e-6
DN_ALPHA = (2 * DEPTH) ** 0.25

LANE = 128
SUBLANE = 8

_ORIG = (("gq", 256), ("gk", 256), ("gv", 512), ("glr", 16), ("gz", 512), ("lx", 512), ("lz", 512), ("sq", 512),
         ("sk", 128), ("sv", 128), ("sz", 512), ("su", 512), ("svv", 512), ("suz", 512), ("mq", 256), ("mz", 256),
         ("gates", 5 * D_MODEL))
_ORIG_OFF = {}
_off = 0
for _n, _w in _ORIG:
    _ORIG_OFF[_n] = (_off, _w)
    _off += _w
_PROJ_RUNS = (("gq", "gv"), ("gz", "sq"), ("sz", "sz"), ("sk", "sv"), ("glr", "glr"), None, ("su", "mz"))
_SEG = {}
_off = 0
for _run in _PROJ_RUNS:
    if _run is None:
        N_PROJ = -(-_off // 1024) * 1024
        _off = N_PROJ
        continue
    _names = [n for n, _ in _ORIG]
    for _n in _names[_names.index(_run[0]):_names.index(_run[1]) + 1]:
        _w = max(_ORIG_OFF[_n][1], 128)
        assert _off % _w == 0
        _SEG[_n] = (_off, _w)
        _off += _w
N_WROWS = _off


def _dot(a, b):
    return jnp.dot(a.astype(bf16), b.astype(bf16), preferred_element_type=f32)


def _dot_nt(a, b):
    return lax.dot_general(a.astype(bf16), b.astype(bf16), (((1,), (1,)), ((), ())), preferred_element_type=f32)


def _dot_tn(a, b):
    return lax.dot_general(a.astype(bf16), b.astype(bf16), (((0,), (0,)), ((), ())), preferred_element_type=f32)


def _sigmoid(x):
    return 1.0 / (1.0 + jnp.exp(-x))


def _silu(x):
    return x * _sigmoid(x)


def _log_sigmoid(x):
    return jnp.minimum(x, 0.0) - jnp.log(1.0 + jnp.exp(-jnp.abs(x)))


def _layer_norm(x, g, b):
    mu = jnp.mean(x, axis=-1, keepdims=True)
    xc = x - mu
    var = jnp.mean(xc * xc, axis=-1, keepdims=True)
    return xc * lax.rsqrt(var + LN_EPS) * g + b


def _params(*sem):
    return pltpu.CompilerParams(dimension_semantics=sem)


def _ln_kernel(x_ref, g_ref, b_ref, ob_ref):
    ob_ref[...] = _layer_norm(x_ref[...], g_ref[...], b_ref[...]).astype(bf16)


def _ln_call(x, g, b, tm=512):
    n, d = x.shape
    tm = min(tm, n)
    return pl.pallas_call(
        _ln_kernel,
        grid=(n // tm,),
        in_specs=[pl.BlockSpec((tm, d), lambda i: (i, 0)), pl.BlockSpec((1, d), lambda i: (0, 0)),
                  pl.BlockSpec((1, d), lambda i: (0, 0))],
        out_specs=pl.BlockSpec((tm, d), lambda i: (i, 0)),
        out_shape=jax.ShapeDtypeStruct((n, d), bf16),
        compiler_params=_params("parallel"),
        name="ln_in",
    )(x, g.reshape(1, d), b.reshape(1, d))


def _matmul_kernel(x_ref, w_ref, o_ref, *, w_transposed):
    o_ref[...] = (_dot_nt if w_transposed else _dot)(x_ref[...], w_ref[...])


def _matmul_call(x, w, tm, tn, name, w_transposed=False, n_out=None, layer=None):
    m, k = x.shape
    n = n_out or (w.shape[-2] if w_transposed else w.shape[1])
    if layer is not None:
        w_spec = pl.BlockSpec((None, tn, k), lambda i, j: (layer, j, 0))
    elif w_transposed:
        w_spec = pl.BlockSpec((tn, k), lambda i, j: (j, 0))
    else:
        w_spec = pl.BlockSpec((k, tn), lambda i, j: (0, j))
    return pl.pallas_call(
        functools.partial(_matmul_kernel, w_transposed=w_transposed),
        grid=(m // tm, n // tn),
        in_specs=[pl.BlockSpec((tm, k), lambda i, j: (i, 0)), w_spec],
        out_specs=pl.BlockSpec((tm, tn), lambda i, j: (i, j)),
        out_shape=jax.ShapeDtypeStruct((m, n), f32),
        compiler_params=_params("parallel", "arbitrary"),
        name=name,
    )(x, w)


def _seg_spec(name, nseq, rows):
    off, width = _SEG[name]
    assert off + width <= N_PROJ
    cb = off // width
    return pl.BlockSpec((nseq, rows, width), lambda b, c: (b, c, cb))


def _tok_spec(nseq, rows, width):
    return pl.BlockSpec((nseq, rows, width), lambda b, c: (b, c, 0))


def _w_spec(first, last, layer):
    off = _SEG[first][0]
    rows = _SEG[last][0] + _SEG[last][1] - off
    assert off % rows == 0
    return pl.BlockSpec((None, rows, D_MODEL), lambda *_: (layer, off // rows, 0))


def _seg_scratch(name, *lead):
    return pltpu.VMEM((*lead, _SEG[name][1]), f32)


def _project(x_ref, pairs):
    x2 = x_ref[...].reshape(-1, D_MODEL)
    for w_ref, seg_refs in pairs:
        p = lax.dot_general(x2, w_ref[...], (((1,), (1,)), ((), ())), preferred_element_type=f32)
        off = 0
        for s_ref in seg_refs:
            width = s_ref.shape[-1]
            s_ref[...] = p[:, off:off + width].reshape(s_ref.shape)
            off += width


def _const_spec(shape):
    nd = len(shape)
    return pl.BlockSpec(shape, lambda b, c: (0,) * nd)


def _lru_kernel(lx_ref, lz_ref, hist0_ref, h0_ref, cw_ref, cb_ref, wr_ref, br_ref, wi_ref, bi_ref, lam_ref,
                y_ref, hlast_ref, hist_out_ref, hist_ref, hc_ref, *, nseq, tc):
    c = pl.program_id(1)
    w = BRANCH_W

    @pl.when(c == 0)
    def _():
        hist_ref[...] = hist0_ref[...]
        hc_ref[...] = h0_ref[...]

    x = lx_ref[...]
    xfull = jnp.concatenate([hist_ref[...], x], axis=1)

    def tap(j):
        return cw_ref[j:j + 1, :].reshape(1, 1, w)

    y = cb_ref[...].reshape(1, 1, w) + x * tap(CONV_W - 1)
    for s in range(1, CONV_W):
        y = y + pltpu.roll(xfull, s, 1)[:, SUBLANE:, :] * tap(CONV_W - 1 - s)
    hist_ref[...] = xfull[:, tc:, :]
    hist_out_ref[...] = xfull[:, tc:, :]

    xc = y.reshape(nseq * tc, w)
    r = _sigmoid(_dot(xc, wr_ref[...]) + br_ref[...])
    i = _sigmoid(_dot(xc, wi_ref[...]) + bi_ref[...])
    log_a = (LRU_C * r) * _log_sigmoid(lam_ref[...])
    a = jnp.exp(log_a)
    u = jnp.sqrt(jnp.tanh(-log_a) * (a * a + 1.0)) * (i * xc)

    acc_a = a.reshape(nseq, tc, w)
    acc_u = u.reshape(nseq, tc, w)
    t = lax.broadcasted_iota(jnp.int32, (nseq, tc, w), 1)
    d = 1
    while d < tc:
        if d % SUBLANE:
            ok = t >= d
            a_sh = jnp.where(ok, pltpu.roll(acc_a, d, 1), 1.0)
            u_sh = jnp.where(ok, pltpu.roll(acc_u, d, 1), 0.0)
            acc_u = acc_a * u_sh + acc_u
            acc_a = acc_a * a_sh
        else:
            new_u = acc_a[:, d:, :] * acc_u[:, :tc - d, :] + acc_u[:, d:, :]
            new_a = acc_a[:, d:, :] * acc_a[:, :tc - d, :]
            acc_u = jnp.concatenate([acc_u[:, :d, :], new_u], axis=1)
            acc_a = jnp.concatenate([acc_a[:, :d, :], new_a], axis=1)
        d *= 2
    h = acc_a * hc_ref[...] + acc_u
    hc_ref[...] = h[:, tc - 1:tc, :]
    hlast_ref[...] = h[:, tc - SUBLANE:, :]
    y_ref[...] = (h * _silu(lz_ref[...])).astype(y_ref.dtype)


def _lru_call(proj, hist0, h0, lw, nb, nseq, nc, tc, name):
    w = BRANCH_W
    kern = functools.partial(_lru_kernel, nseq=nseq, tc=tc)
    return pl.pallas_call(
        kern,
        grid=(nb, nc),
        in_specs=[_seg_spec("lx", nseq, tc), _seg_spec("lz", nseq, tc),
                  pl.BlockSpec((nseq, SUBLANE, w), lambda b, c: (b, 0, 0)),
                  pl.BlockSpec((nseq, 1, w), lambda b, c: (b, 0, 0)),
                  _const_spec((CONV_W, w)), _const_spec((1, w)), _const_spec((w, w)), _const_spec((1, w)),
                  _const_spec((w, w)), _const_spec((1, w)), _const_spec((1, w))],
        out_specs=[_tok_spec(nseq, tc, w),
                   pl.BlockSpec((nseq, SUBLANE, w), lambda b, c: (b, 0, 0)),
                   pl.BlockSpec((nseq, SUBLANE, w), lambda b, c: (b, 0, 0))],
        out_shape=[jax.ShapeDtypeStruct((nb * nseq, nc * tc, w), bf16),
                   jax.ShapeDtypeStruct((nb * nseq, SUBLANE, w), f32),
                   jax.ShapeDtypeStruct((nb * nseq, SUBLANE, w), f32)],
        scratch_shapes=[pltpu.VMEM((nseq, SUBLANE, w), f32), pltpu.VMEM((nseq, 1, w), f32)],
        compiler_params=_params("parallel", "arbitrary"),
        name=name,
    )(proj, proj, hist0, h0, lw["conv_w"], lw["conv_b"], lw["wr"], lw["br"], lw["wi"], lw["bi"], lw["lam"])


def _gla_consts(c):
    t = np.arange(c)[:, None]
    u = np.arange(c)[None, :]
    blocks = [u <= t, u > t]
    masks = [t == u]
    m = 1
    while m < c:
        t0 = (t // m) * m
        odd = (t // m) % 2 == 1
        blocks.append(odd & (u >= t0) & (u <= t))
        blocks.append((~odd) & (u > t) & (u <= t0 + m - 1))
        masks.append((t // (2 * m) == u // (2 * m)) & odd & ((u // m) % 2 == 0))
        m *= 2
    return (np.concatenate(blocks, 0).astype(np.float32), np.stack(masks).astype(np.float32))


def _gla_kernel(gq_ref, gk_ref, gv_ref, gz_ref, glr_ref, s0_ref, wa_ref, ba_ref, ng_ref, d_ref, m_ref,
                y_ref, sout_ref, s_ref, att_ref, o_ref, upd_ref, *, nseq, c):
    ci = pl.program_id(1)
    nlev = int(math.log2(c))
    hk = GLA_HEADS * GLA_DK
    heads = range(GLA_HEADS)

    @pl.when(ci == 0)
    def _():
        s_ref[...] = s0_ref[...]

    def ks(h):
        return slice(h * GLA_DK, (h + 1) * GLA_DK)

    def vs(h):
        return slice(h * GLA_DV, (h + 1) * GLA_DV)

    z = _dot(glr_ref[...].reshape(nseq * c, LANE), wa_ref[...]) + ba_ref[...]
    la = _log_sigmoid(z) * (1.0 / GLA_TAU)
    hi = la.astype(bf16)
    r1 = la - hi.astype(f32)
    mid = r1.astype(bf16)
    lo = (r1 - mid.astype(f32)).astype(bf16)
    hml = jnp.concatenate([hi, mid, lo], axis=1)

    q_in, k_st, dec_rows, qf, kf = [], [], [], [], []
    for n in range(nseq):
        hml_n = hml[n * c:(n + 1) * c]

        p = jnp.dot(d_ref[0:c, :], hml_n, preferred_element_type=f32)
        b = p[:, :hk] + p[:, hk:2 * hk] + p[:, 2 * hk:]

        def decay(lev, query_side):
            m = 2 ** lev
            if m < SUBLANE:
                blk = 2 + 2 * lev + (0 if query_side else 1)
                pm = jnp.dot(d_ref[blk * c:(blk + 1) * c, :], hml_n[:, :2 * hk], preferred_element_type=f32)
                return jnp.exp(pm[:, :hk] + pm[:, hk:])
            pieces = []
            for j in range(c // m):
                rows = b[j * m:(j + 1) * m]
                if query_side and j % 2 == 1:
                    pieces.append(rows - b[j * m - 1:j * m])
                elif not query_side and j % 2 == 0:
                    pieces.append(b[(j + 1) * m - 1:(j + 1) * m] - rows)
                else:
                    pieces.append(jnp.zeros((m, hk), f32))
            return jnp.exp(jnp.concatenate(pieces, axis=0))

        q = gq_ref[n] * (GLA_DK ** -0.5)
        k = gk_ref[n]
        eb = jnp.exp(b)
        q_in.append((q * eb).astype(bf16))
        k_st.append((k * jnp.exp(b[c - 1:c] - b)).astype(bf16))
        dec_rows.append(eb[c - 1:c, :])
        qf.append([q.astype(bf16)] + [(q * decay(lev, True)).astype(bf16) for lev in range(nlev)])
        kf.append([k.astype(bf16)] + [(k * decay(lev, False)).astype(bf16) for lev in range(nlev)])

    for n in range(nseq):
        for lev in range(nlev + 1):
            for h in heads:
                att_ref[n, h, lev] = _dot_nt(qf[n][lev][:, ks(h)], kf[n][lev][:, ks(h)])

    att = m_ref[0][None, None] * att_ref[:, :, 0]
    for lev in range(1, nlev + 1):
        att = att + m_ref[lev][None, None] * att_ref[:, :, lev]
    att = att.astype(bf16)

    for n in range(nseq):
        v = gv_ref[n].astype(bf16)
        for h in heads:
            v_h = v[:, vs(h)]
            o_ref[n, h] = _dot(q_in[n][:, ks(h)], s_ref[n, h]) + _dot(att[n, h], v_h)
            upd_ref[n, h] = _dot_tn(k_st[n][:, ks(h)], v_h)

    eye = (lax.broadcasted_iota(jnp.int32, (GLA_DK, GLA_DK), 0)
           == lax.broadcasted_iota(jnp.int32, (GLA_DK, GLA_DK), 1))[None, None]
    dec = jnp.stack([jnp.stack([dec_rows[n][:, ks(h)] for h in heads]) for n in range(nseq)])
    dec_col = jnp.sum(jnp.where(eye, jnp.broadcast_to(dec, (nseq, GLA_HEADS, GLA_DK, GLA_DK)), 0.0),
                      axis=3, keepdims=True)
    s_new = s_ref[...] * dec_col + upd_ref[...]
    s_ref[...] = s_new
    sout_ref[...] = s_new
    o = o_ref[...]
    o = o * lax.rsqrt(jnp.mean(o * o, axis=-1, keepdims=True) + RMS_EPS) * ng_ref[...]
    for n in range(nseq):
        for h in heads:
            y_ref[n, :, vs(h)] = (o[n, h] * _silu(gz_ref[n, :, vs(h)])).astype(y_ref.dtype)


_GLA_SEGS = ("gq", "gk", "gv", "gz", "glr")


def _gla_call(proj, s0, gw, nb, nseq, nc, c, name):
    dstack, masks = _gla_consts(c)
    kern = functools.partial(_gla_kernel, nseq=nseq, c=c)
    hk = GLA_HEADS * GLA_DK
    st_spec = pl.BlockSpec((nseq, GLA_HEADS, GLA_DK, GLA_DV), lambda b, ci: (b, 0, 0, 0))
    return pl.pallas_call(
        kern,
        grid=(nb, nc),
        in_specs=[_seg_spec(s, nseq, c) for s in _GLA_SEGS] + [
            st_spec, _const_spec((LANE, hk)), _const_spec((1, hk)), _const_spec((1, GLA_DV)),
            _const_spec(dstack.shape), _const_spec(masks.shape)],
        out_specs=[_tok_spec(nseq, c, BRANCH_W), st_spec],
        out_shape=[jax.ShapeDtypeStruct((nb * nseq, nc * c, BRANCH_W), bf16),
                   jax.ShapeDtypeStruct((nb * nseq, GLA_HEADS, GLA_DK, GLA_DV), f32)],
        scratch_shapes=[pltpu.VMEM((nseq, GLA_HEADS, GLA_DK, GLA_DV), f32),
                        pltpu.VMEM((nseq, GLA_HEADS, masks.shape[0], c, c), f32),
                        pltpu.VMEM((nseq, GLA_HEADS, c, GLA_DV), f32),
                        pltpu.VMEM((nseq, GLA_HEADS, GLA_DK, GLA_DV), f32)],
        compiler_params=_params("parallel", "arbitrary"),
        name=name,
    )(*([proj] * len(_GLA_SEGS)), s0, gw["wa"], gw["ba"], gw["ng"], jnp.asarray(dstack, bf16), jnp.asarray(masks))


def _gla_step_consts(t, nseq):
    dstack, masks = _gla_consts(t)
    eye = np.eye(nseq, dtype=np.float32)
    dbd = np.concatenate([np.kron(eye, dstack[i * t:(i + 1) * t]) for i in range(dstack.shape[0] // t)], axis=0)
    mbd = np.stack([np.kron(eye, m) for m in masks])
    return dbd, mbd


def _gla_step_kernel(gq_ref, gk_ref, gv_ref, gz_ref, glr_ref, s0_ref, wa_ref, ba_ref, ng_ref, d_ref, m_ref,
                     y_ref, sout_ref, *, nseq, t):
    r = nseq * t
    nlev = int(math.log2(t))
    hk = GLA_HEADS * GLA_DK
    q = gq_ref[...].reshape(r, hk) * (GLA_DK ** -0.5)
    k = gk_ref[...].reshape(r, hk)
    v = gv_ref[...].reshape(r, BRANCH_W)
    gz = gz_ref[...].reshape(r, BRANCH_W)
    z = _dot(glr_ref[...].reshape(r, LANE), wa_ref[...]) + ba_ref[...]
    la = _log_sigmoid(z) * (1.0 / GLA_TAU)
    hi = la.astype(bf16)
    r1 = la - hi.astype(f32)
    mid = r1.astype(bf16)
    lo = (r1 - mid.astype(f32)).astype(bf16)
    hml = jnp.concatenate([hi, mid, lo], axis=1)

    def decay(blk):
        p = jnp.dot(d_ref[blk * r:(blk + 1) * r, :], hml, preferred_element_type=f32)
        return jnp.exp(p[:, :hk] + p[:, hk:2 * hk] + p[:, 2 * hk:])

    eb = decay(0)
    q_in = q * eb
    k_st = k * decay(1)
    dec3 = eb.reshape(nseq, t, hk)[:, t - 1:t, :]
    qf = [q]
    kf = [k]
    for lev in range(nlev):
        qf.append(q * decay(2 + 2 * lev))
        kf.append(k * decay(3 + 2 * lev))

    own = (lax.broadcasted_iota(jnp.int32, (r, nseq * GLA_DK), 0) // t
           == lax.broadcasted_iota(jnp.int32, (r, nseq * GLA_DK), 1) // GLA_DK)
    eye = (lax.broadcasted_iota(jnp.int32, (GLA_DK, GLA_DK), 0)
           == lax.broadcasted_iota(jnp.int32, (GLA_DK, GLA_DK), 1))[None]

    def spread(x):
        x2 = jnp.concatenate([x, x], axis=1)
        return jnp.where(own, jnp.concatenate([x2] * (nseq // 2), axis=1), 0.0)

    ys = []
    for h in range(GLA_HEADS):
        ks = slice(h * GLA_DK, (h + 1) * GLA_DK)
        vs = slice(h * GLA_DV, (h + 1) * GLA_DV)
        att = jnp.zeros((r, r), f32)
        for lev in range(nlev + 1):
            att = att + m_ref[lev] * _dot_nt(qf[lev][:, ks], kf[lev][:, ks])
        s_h = s0_ref[:, h]
        v_h = v[:, vs]
        o = _dot(spread(q_in[:, ks]), s_h.reshape(nseq * GLA_DK, GLA_DV)) + _dot(att, v_h)
        upd = _dot_tn(spread(k_st[:, ks]), v_h)
        dec_col = jnp.sum(jnp.where(eye, jnp.broadcast_to(dec3[:, :, ks], (nseq, GLA_DK, GLA_DK)), 0.0),
                          axis=2, keepdims=True)
        sout_ref[:, h] = s_h * dec_col + upd.reshape(nseq, GLA_DK, GLA_DV)
        o = o * lax.rsqrt(jnp.mean(o * o, axis=-1, keepdims=True) + RMS_EPS) * ng_ref[...]
        ys.append(o * _silu(gz[:, vs]))
    y_ref[...] = jnp.concatenate(ys, axis=1).reshape(nseq, t, BRANCH_W).astype(y_ref.dtype)


def _gla_step_call(proj, s0_all, layer, gw, nb, nseq, t, name):
    dbd, mbd = _gla_step_consts(t, nseq)
    kern = functools.partial(_gla_step_kernel, nseq=nseq, t=t)
    hk = GLA_HEADS * GLA_DK
    st_spec = pl.BlockSpec((nseq, GLA_HEADS, GLA_DK, GLA_DV), lambda b, ci: (b, 0, 0, 0))
    s0_spec = pl.BlockSpec((None, nseq, GLA_HEADS, GLA_DK, GLA_DV), lambda b, ci: (layer, b, 0, 0, 0))
    return pl.pallas_call(
        kern,
        grid=(nb, 1),
        in_specs=[_seg_spec(s, nseq, t) for s in _GLA_SEGS] + [
            s0_spec, _const_spec((LANE, hk)), _const_spec((1, hk)), _const_spec((1, GLA_DV)),
            _const_spec(dbd.shape), _const_spec(mbd.shape)],
        out_specs=[_tok_spec(nseq, t, BRANCH_W), st_spec],
        out_shape=[jax.ShapeDtypeStruct((nb * nseq, t, BRANCH_W), bf16),
                   jax.ShapeDtypeStruct((nb * nseq, GLA_HEADS, GLA_DK, GLA_DV), f32)],
        compiler_params=_params("parallel", "arbitrary"),
        name=name,
    )(*([proj] * len(_GLA_SEGS)), s0_all, gw["wa"], gw["ba"], gw["ng"], jnp.asarray(dbd, bf16), jnp.asarray(mbd))


def _rope_tables(pos0, t):
    half = ROT_DIM // 2
    dim = jnp.arange(LANE) % HEAD_DIM
    inv = ROPE_THETA ** (-(dim % half).astype(f32) / half)
    ang = (pos0 + jnp.arange(t)).astype(f32)[:, None] * inv[None, :]
    cos, sin = jnp.cos(ang), jnp.sin(ang)
    first, second = (dim < half)[None, :], ((dim >= half) & (dim < ROT_DIM))[None, :]
    c_tab = jnp.where(first | second, cos, 1.0)
    sa_tab = jnp.where(first, -sin, 0.0)
    sb_tab = jnp.where(second, sin, 0.0)
    return c_tab, sa_tab, sb_tab


def _rope(x, c_tab, sa_tab, sb_tab):
    wd = x.shape[-1]
    ax = x.ndim - 1
    rep = wd // LANE
    half = ROT_DIM // 2
    if rep > 1:
        c_tab, sa_tab, sb_tab = (jnp.concatenate([tb] * rep, axis=-1) for tb in (c_tab, sa_tab, sb_tab))
    return x * c_tab + pltpu.roll(x, wd - half, ax) * sa_tab + pltpu.roll(x, half, ax) * sb_tab


def _swa_kernel(sink_ref, sq_ref, sz_ref, sk_ref, sv_ref, ct_ref, sat_ref, sbt_ref, kp_ref, vp_ref,
                y_ref, klast_ref, vlast_ref, kprev_ref, vprev_ref, sp_ref, sc_ref, o_ref, den_ref,
                *, nseq, qb, pos0):
    blk = pl.program_id(1)
    hd = HEAD_DIM

    @pl.when(blk == 0)
    def _():
        kprev_ref[...] = kp_ref[...]
        vprev_ref[...] = vp_ref[...]

    tabs = tuple(r[...][None] for r in (ct_ref, sat_ref, sbt_ref))
    q3 = _rope(sq_ref[...], *tabs) * (hd ** -0.5)
    k3 = _rope(sk_ref[...], *tabs)
    v3 = sv_ref[...]
    klast_ref[...] = k3
    vlast_ref[...] = v3

    for n in range(nseq):
        for kv in range(SWA_KV_HEADS):
            ds = slice(kv * hd, (kv + 1) * hd)
            qs = jnp.concatenate(
                [q3[n][:, (kv * SWA_GROUP + g) * hd:(kv * SWA_GROUP + g + 1) * hd] for g in range(SWA_GROUP)],
                axis=0).astype(bf16)
            sp_ref[n, kv] = _dot_nt(qs, kprev_ref[n][:, ds])
            sc_ref[n, kv] = _dot_nt(qs, k3[n][:, ds])

    mrows = SWA_GROUP * qb
    qi = lax.broadcasted_iota(jnp.int32, (mrows, WINDOW), 0) % qb
    kj = lax.broadcasted_iota(jnp.int32, (mrows, WINDOW), 1)
    past_ok = (kj >= qi) & (kj >= (WINDOW - pos0) - blk * qb)
    cur_ok = kj <= qi
    sink = jnp.stack([jnp.concatenate([jnp.full((qb, 1), sink_ref[kv * SWA_GROUP + g], f32)
                                       for g in range(SWA_GROUP)], axis=0) for kv in range(SWA_KV_HEADS)])[None]
    s_p = jnp.where(past_ok[None, None], sp_ref[...], -jnp.inf)
    s_c = jnp.where(cur_ok[None, None], sc_ref[...], -jnp.inf)
    m = jnp.maximum(jnp.max(jnp.maximum(s_p, s_c), axis=3, keepdims=True), sink)
    p_p = jnp.exp(s_p - m).astype(bf16)
    p_c = jnp.exp(s_c - m).astype(bf16)
    ones = jnp.ones((WINDOW, LANE), bf16)
    for n in range(nseq):
        vp = vprev_ref[n].astype(bf16)
        vc = v3[n].astype(bf16)
        for kv in range(SWA_KV_HEADS):
            o_ref[n, kv] = _dot(p_p[n, kv], vp) + _dot(p_c[n, kv], vc)
            den_ref[n, kv] = _dot(p_p[n, kv], ones) + _dot(p_c[n, kv], ones)
    o = o_ref[...] / (den_ref[...] + jnp.exp(sink - m))
    outs = [o[:, j // SWA_GROUP, (j % SWA_GROUP) * qb:(j % SWA_GROUP + 1) * qb,
              (j // SWA_GROUP) * hd:(j // SWA_GROUP + 1) * hd] for j in range(SWA_HEADS)]
    y_ref[...] = (jnp.concatenate(outs, axis=2) * _silu(sz_ref[...])).astype(y_ref.dtype)
    kprev_ref[...] = k3
    vprev_ref[...] = v3


_SWA_SEGS = ("sq", "sz", "sk", "sv")


def _swa_call(proj, sinks, k_past, v_past, pos0, nb, nseq, nc, qb, name):
    assert qb == WINDOW
    t_total = nc * qb
    c_tab, sa_tab, sb_tab = _rope_tables(pos0, t_total)
    kern = functools.partial(_swa_kernel, nseq=nseq, qb=qb, pos0=pos0)
    kvw = SWA_KV_HEADS * HEAD_DIM
    mrows = SWA_GROUP * qb
    tab_spec = pl.BlockSpec((qb, LANE), lambda b, c: (c, 0))
    past_spec = pl.BlockSpec((nseq, WINDOW, kvw), lambda b, c: (b, 0, 0))
    return pl.pallas_call(
        kern,
        grid=(nb, nc),
        in_specs=[pl.BlockSpec(memory_space=pltpu.SMEM)] + [_seg_spec(s, nseq, qb) for s in _SWA_SEGS] + [
            tab_spec, tab_spec, tab_spec, past_spec, past_spec],
        out_specs=[_tok_spec(nseq, qb, BRANCH_W), past_spec, past_spec],
        out_shape=[jax.ShapeDtypeStruct((nb * nseq, nc * qb, BRANCH_W), bf16),
                   jax.ShapeDtypeStruct((nb * nseq, WINDOW, kvw), f32),
                   jax.ShapeDtypeStruct((nb * nseq, WINDOW, kvw), f32)],
        scratch_shapes=[pltpu.VMEM((nseq, WINDOW, kvw), f32), pltpu.VMEM((nseq, WINDOW, kvw), f32),
                        pltpu.VMEM((nseq, SWA_KV_HEADS, mrows, WINDOW), f32),
                        pltpu.VMEM((nseq, SWA_KV_HEADS, mrows, qb), f32),
                        pltpu.VMEM((nseq, SWA_KV_HEADS, mrows, kvw), f32),
                        pltpu.VMEM((nseq, SWA_KV_HEADS, mrows, kvw), f32)],
        compiler_params=_params("parallel", "arbitrary"),
        name=name,
    )(sinks, *([proj] * len(_SWA_SEGS)), c_tab, sa_tab, sb_tab, k_past, v_past)


def _swa_step_kernel(sink_ref, sq_ref, sz_ref, sk_ref, sv_ref, ct_ref, sat_ref, sbt_ref, kp_ref, vp_ref,
                     y_ref, klast_ref, vlast_ref, sp_ref, sc_ref, o_ref, *, nseq, t, pos0):
    hd = HEAD_DIM
    kvw = SWA_KV_HEADS * hd
    mrows = SWA_HEADS * t
    tabs = tuple(r[...][None] for r in (ct_ref, sat_ref, sbt_ref))
    q3 = _rope(sq_ref[...], *tabs) * (hd ** -0.5)
    k3 = _rope(sk_ref[...], *tabs)
    v3 = sv_ref[...]

    lane = lax.broadcasted_iota(jnp.int32, (kvw, WINDOW), 1)
    pad = jnp.zeros((WINDOW - t, kvw), f32)

    def shifted(old_t, new):
        new_t = jnp.concatenate([pad, new], axis=0).T
        out = jnp.where(lane >= WINDOW - t, new_t, pltpu.roll(old_t, WINDOW - t, 1))
        return out.reshape(SWA_KV_HEADS, hd, WINDOW)

    for n in range(nseq):
        klast_ref[n] = shifted(kp_ref[n].reshape(kvw, WINDOW), k3[n])
        vlast_ref[n] = shifted(vp_ref[n].reshape(kvw, WINDOW), v3[n])

    zero = jnp.zeros((nseq, t, hd), f32)
    pieces = []
    for j in range(SWA_HEADS):
        qj = q3[:, :, j * hd:(j + 1) * hd]
        pieces.append(jnp.concatenate([qj, zero] if j // SWA_GROUP == 0 else [zero, qj], axis=2))
    qbd = jnp.concatenate(pieces, axis=1).astype(bf16)

    for n in range(nseq):
        sp_ref[n] = _dot(qbd[n], kp_ref[n].reshape(kvw, WINDOW))
        sc_ref[n] = _dot_nt(qbd[n], k3[n])

    qi = lax.broadcasted_iota(jnp.int32, (mrows, WINDOW), 0) % t
    kj = lax.broadcasted_iota(jnp.int32, (mrows, WINDOW), 1)
    past_ok = kj >= qi
    if pos0 < WINDOW:
        past_ok = past_ok & (kj >= WINDOW - pos0)
    qi_c = lax.broadcasted_iota(jnp.int32, (mrows, t), 0) % t
    kj_c = lax.broadcasted_iota(jnp.int32, (mrows, t), 1)
    cur_ok = kj_c <= qi_c
    sink = jnp.concatenate([jnp.full((t, 1), sink_ref[j], f32) for j in range(SWA_HEADS)], axis=0)[None]
    s_p = jnp.where(past_ok[None], sp_ref[...], -jnp.inf)
    s_c = jnp.where(cur_ok[None], sc_ref[...], -jnp.inf)
    m = jnp.maximum(jnp.maximum(jnp.max(s_p, axis=2, keepdims=True), jnp.max(s_c, axis=2, keepdims=True)), sink)
    p_p = jnp.exp(s_p - m)
    p_c = jnp.exp(s_c - m)
    den = jnp.sum(p_p, axis=2, keepdims=True) + jnp.sum(p_c, axis=2, keepdims=True) + jnp.exp(sink - m)
    p_p = p_p.astype(bf16)
    p_c = p_c.astype(bf16)
    for n in range(nseq):
        o_ref[n] = _dot_nt(p_p[n], vp_ref[n].reshape(kvw, WINDOW)) + _dot(p_c[n], v3[n])
    o = o_ref[...] / den
    outs = []
    for j in range(SWA_HEADS):
        kv = j // SWA_GROUP
        outs.append(o[:, j * t:(j + 1) * t, kv * hd:(kv + 1) * hd])
    y_ref[...] = (jnp.concatenate(outs, axis=2) * _silu(sz_ref[...])).astype(y_ref.dtype)


def _swa_step_call(proj, sinks, k_past_t, v_past_t, layer, pos0, nb, nseq, t, name):
    c_tab, sa_tab, sb_tab = _rope_tables(pos0, t)
    kern = functools.partial(_swa_step_kernel, nseq=nseq, t=t, pos0=pos0)
    kvw = SWA_KV_HEADS * HEAD_DIM
    mrows = SWA_HEADS * t
    tab_spec = pl.BlockSpec((t, LANE), lambda b, c: (0, 0))
    past_spec = pl.BlockSpec((None, nseq, SWA_KV_HEADS, HEAD_DIM, WINDOW), lambda b, c: (layer, b, 0, 0, 0))
    new_spec = pl.BlockSpec((nseq, SWA_KV_HEADS, HEAD_DIM, WINDOW), lambda b, c: (b, 0, 0, 0))
    new_shape = jax.ShapeDtypeStruct((nb * nseq, SWA_KV_HEADS, HEAD_DIM, WINDOW), f32)
    return pl.pallas_call(
        kern,
        grid=(nb, 1),
        in_specs=[pl.BlockSpec(memory_space=pltpu.SMEM)] + [_seg_spec(s, nseq, t) for s in _SWA_SEGS] + [
            tab_spec, tab_spec, tab_spec, past_spec, past_spec],
        out_specs=[_tok_spec(nseq, t, BRANCH_W), new_spec, new_spec],
        out_shape=[jax.ShapeDtypeStruct((nb * nseq, t, BRANCH_W), bf16), new_shape, new_shape],
        scratch_shapes=[pltpu.VMEM((nseq, mrows, WINDOW), f32), pltpu.VMEM((nseq, mrows, t), f32),
                        pltpu.VMEM((nseq, mrows, kvw), f32)],
        compiler_params=_params("parallel", "arbitrary"),
        name=name,
    )(sinks, *([proj] * len(_SWA_SEGS)), c_tab, sa_tab, sb_tab, k_past_t, v_past_t)


def _sgu_kernel(x_ref, wu_ref, wv_ref, wz_ref, g_ref, b_ref, wm_ref, bias_ref, y_ref, *rest, ntile, want_vn):
    vn_ref = rest[0] if want_vn else None
    su_ref, sv_ref, sz_ref = rest[-3:]
    _project(x_ref, ((wu_ref, (su_ref,)), (wv_ref, (sv_ref,)), (wz_ref, (sz_ref,))))
    vn = _layer_norm(sv_ref[...], g_ref[...], b_ref[...])
    if want_vn:
        vn_ref[...] = vn
    for r in range(ntile):
        rows = slice(r * SGU_CHUNK, (r + 1) * SGU_CHUNK)
        mixed = jnp.concatenate(
            [jnp.dot(wm_ref[g], vn[rows, g * SGU_GC:(g + 1) * SGU_GC].astype(bf16), preferred_element_type=f32)
             for g in range(SGU_GROUPS)], axis=1)
        y = su_ref[rows, :] * (mixed + bias_ref[...]) * _silu(sz_ref[rows, :])
        y_ref[rows, :] = y.astype(y_ref.dtype)


def _sgu_call(xb, w_t, layer, ln_g, ln_b, wmix, bias, n_tok, ntile, want_vn, name):
    rows = ntile * SGU_CHUNK
    w = BRANCH_W
    kern = functools.partial(_sgu_kernel, ntile=ntile, want_vn=want_vn)

    def const(shape):
        return pl.BlockSpec(shape, lambda i: (0,) * len(shape))

    out_specs = [pl.BlockSpec((rows, w), lambda i: (i, 0))]
    out_shape = [jax.ShapeDtypeStruct((n_tok, w), bf16)]
    if want_vn:
        out_specs.append(pl.BlockSpec((rows, w), lambda i: (i, 0)))
        out_shape.append(jax.ShapeDtypeStruct((n_tok, w), f32))
    return pl.pallas_call(
        kern,
        grid=(n_tok // rows,),
        in_specs=[pl.BlockSpec((rows, D_MODEL), lambda i: (i, 0)), _w_spec("su", "su", layer),
                  _w_spec("svv", "svv", layer), _w_spec("suz", "suz", layer), const((1, w)), const((1, w)),
                  const((SGU_GROUPS, SGU_CHUNK, SGU_CHUNK)), const((SGU_CHUNK, w))],
        out_specs=out_specs,
        out_shape=out_shape,
        scratch_shapes=[_seg_scratch(s, rows) for s in ("su", "svv", "suz")],
        compiler_params=_params("parallel"),
        name=name,
    )(xb, w_t, w_t, w_t, ln_g, ln_b, wmix, bias)


def _mem_kernel(x_ref, wm_ref, mk_ref, mv_ref, y_ref, mq_ref, mz_ref, s_ref, o_ref, den_ref, *, nseq, tq):
    _project(x_ref, ((wm_ref, (mq_ref, mz_ref)),))
    heads = range(MEM_HEADS)
    lane_head = lax.broadcasted_iota(jnp.int32, (tq, MEM_W), 1) // HEAD_DIM
    for n in range(nseq):
        q = mq_ref[n] * (HEAD_DIM ** -0.5)
        mk = mk_ref[n].astype(bf16)
        for h in heads:
            s_ref[n, h] = _dot_nt(jnp.where(lane_head == h, q, 0.0), mk)
    s = s_ref[...]
    p = jnp.exp(s - jnp.max(s, axis=3, keepdims=True)).astype(bf16)
    ones = jnp.ones((N_MEM, MEM_W), bf16)
    for n in range(nseq):
        mv = mv_ref[n].astype(bf16)
        for h in heads:
            o_ref[n, h] = _dot(p[n, h], mv)
            den_ref[n, h] = _dot(p[n, h], ones)
    o = o_ref[...] / den_ref[...]
    acc = jnp.where(lane_head == 0, o[:, 0], 0.0)
    for h in range(1, MEM_HEADS):
        acc = acc + jnp.where(lane_head == h, o[:, h], 0.0)
    y_ref[...] = (acc * _silu(mz_ref[...])).astype(y_ref.dtype)


def _mem_call(xb, w_t, layer, mk, mv, nb, nseq, nc, tq, name):
    kern = functools.partial(_mem_kernel, nseq=nseq, tq=tq)
    kv_spec = pl.BlockSpec((nseq, N_MEM, MEM_W), lambda b, c: (b, 0, 0))
    return pl.pallas_call(
        kern,
        grid=(nb, nc),
        in_specs=[_tok_spec(nseq, tq, D_MODEL), _w_spec("mq", "mz", layer), kv_spec, kv_spec],
        out_specs=_tok_spec(nseq, tq, MEM_W),
        out_shape=jax.ShapeDtypeStruct((nb * nseq, nc * tq, MEM_W), bf16),
        scratch_shapes=[_seg_scratch("mq", nseq, tq), _seg_scratch("mz", nseq, tq),
                        pltpu.VMEM((nseq, MEM_HEADS, tq, N_MEM), f32), pltpu.VMEM((nseq, MEM_HEADS, tq, MEM_W), f32),
                        pltpu.VMEM((nseq, MEM_HEADS, tq, MEM_W), f32)],
        compiler_params=_params("parallel", "arbitrary"),
        name=name,
    )(xb, w_t, mk, mv)


def _mem_step_kernel(x_ref, wm_ref, mk_ref, mv_ref, y_ref, s_ref, o_ref, mq_ref, mz_ref, *, nseq, t):
    mrows = MEM_HEADS * t
    _project(x_ref, ((wm_ref, (mq_ref, mz_ref)),))
    row_head = lax.broadcasted_iota(jnp.int32, (mrows, MEM_W), 0) // t
    lane_head = lax.broadcasted_iota(jnp.int32, (mrows, MEM_W), 1) // HEAD_DIM
    own = (row_head == lane_head)[None]
    q3 = mq_ref[...] * (HEAD_DIM ** -0.5)
    qbd = jnp.where(own, jnp.concatenate([q3] * MEM_HEADS, axis=1), 0.0).astype(bf16)
    for n in range(nseq):
        s_ref[n] = _dot(qbd[n], mk_ref[n].reshape(MEM_W, N_MEM))
    s = s_ref[...]
    p = jnp.exp(s - jnp.max(s, axis=2, keepdims=True))
    den = jnp.sum(p, axis=2, keepdims=True)
    p = p.astype(bf16)
    for n in range(nseq):
        o_ref[n] = _dot_nt(p[n], mv_ref[n].reshape(MEM_W, N_MEM))
    o = jnp.where(own, o_ref[...] / den, 0.0)
    acc = o[:, 0:t, :]
    for h in range(1, MEM_HEADS):
        acc = acc + o[:, h * t:(h + 1) * t, :]
    y_ref[...] = (acc * _silu(mz_ref[...])).astype(y_ref.dtype)


def _mem_step_call(xb, w_t, mk_t, mv_t, layer, nb, nseq, t, name):
    kern = functools.partial(_mem_step_kernel, nseq=nseq, t=t)
    kv_spec = pl.BlockSpec((None, nseq, MEM_HEADS, HEAD_DIM, N_MEM), lambda b, c: (layer, b, 0, 0, 0))
    mrows = MEM_HEADS * t
    return pl.pallas_call(
        kern,
        grid=(nb, 1),
        in_specs=[_tok_spec(nseq, t, D_MODEL), _w_spec("mq", "mz", layer), kv_spec, kv_spec],
        out_specs=_tok_spec(nseq, t, MEM_W),
        out_shape=jax.ShapeDtypeStruct((nb * nseq, t, MEM_W), bf16),
        scratch_shapes=[pltpu.VMEM((nseq, mrows, N_MEM), f32), pltpu.VMEM((nseq, mrows, MEM_W), f32),
                        _seg_scratch("mq", nseq, t), _seg_scratch("mz", nseq, t)],
        compiler_params=_params("parallel", "arbitrary"),
        name=name,
    )(xb, w_t, mk_t, mv_t)


def _merge_kernel(yg_ref, yl_ref, ys_ref, yu_ref, ym_ref, x_ref, ig_ref, ib_ref, wg_ref, wb_ref, wm_ref, wo_ref,
                  g_ref, b_ref, o_ref, ob_ref, *, pre_ln):
    d = D_MODEL
    x = x_ref[...]
    if pre_ln:
        x = _layer_norm(x, ig_ref[...], ib_ref[...])
    xb = x.astype(bf16)

    def gate(n):
        return _sigmoid(_dot_nt(xb, wg_ref[n * d:(n + 1) * d, :]))

    merged = gate(4) * jnp.dot(ym_ref[...], wm_ref[...], preferred_element_type=f32)
    for n, y_ref in enumerate((yg_ref, yl_ref, ys_ref, yu_ref)):
        merged = merged + gate(n) * jnp.dot(y_ref[...], wb_ref[n], preferred_element_type=f32)
    out = _dot(merged, wo_ref[...])
    y = _layer_norm(DN_ALPHA * x + out, g_ref[...], b_ref[...])
    o_ref[...] = y
    ob_ref[...] = y.astype(bf16)


def _merge_call(ys, x, ln_in, mw, layer, tm, name):
    n_tok, d = x.shape
    w = BRANCH_W

    def rows(width):
        return pl.BlockSpec((tm, width), lambda i: (i, 0))

    def const(shape):
        return pl.BlockSpec(shape, lambda i: (0,) * len(shape), pipeline_mode=pl.Buffered(1))

    gates_spec = pl.BlockSpec((None, 5 * d, d), lambda i: (layer, 0, 0), pipeline_mode=pl.Buffered(1))
    return pl.pallas_call(
        functools.partial(_merge_kernel, pre_ln=layer == 0),
        grid=(n_tok // tm,),
        in_specs=[rows(w), rows(w), rows(w), rows(w), rows(MEM_W), rows(d), const((1, d)), const((1, d)),
                  gates_spec, const((4, w, d)), const((MEM_W, d)), const((d, d)), const((1, d)),
                  const((1, d))],
        out_specs=[rows(d), rows(d)],
        out_shape=[jax.ShapeDtypeStruct((n_tok, d), f32), jax.ShapeDtypeStruct((n_tok, d), bf16)],
        compiler_params=_params("parallel"),
        name=name,
    )(*ys, x, ln_in[0].reshape(1, d), ln_in[1].reshape(1, d), mw["wg"], mw["wb"], mw["wm"], mw["wo"], mw["g"],
      mw["b"])


def _prep_w_in(w_in):
    w_t = jnp.swapaxes(w_in, 1, 2).astype(bf16)
    runs = []
    used = 0
    for run in _PROJ_RUNS:
        if run is None:
            runs.append(jnp.zeros((DEPTH, N_PROJ - used, D_MODEL), bf16))
            used = N_PROJ
            continue
        lo = _ORIG_OFF[run[0]][0]
        hi = _ORIG_OFF[run[1]][0] + _ORIG_OFF[run[1]][1]
        runs.append(w_t[:, lo:hi])
        used += hi - lo
    assert used == N_WROWS
    return jnp.concatenate(runs, axis=1), w_t[:, _ORIG_OFF["gates"][0]:]


def _prep_layer(l, w_proj, w_gates, gla_wa2, gla_ba, gla_norm_g, lru_conv_w, lru_conv_b, lru_wr, lru_br, lru_wi,
                lru_bi, lru_L, swa_sinks, sgu_ln_g, sgu_ln_b, sgu_w, sgu_b, w_mem_kv, w_branch, w_branch_mem, w_out,
                ln_g, ln_b):
    d = D_MODEL
    w = BRANCH_W

    def block_diag(wb):
        eye = jnp.eye(LRU_BLOCKS, dtype=f32)
        return (eye[:, None, :, None] * wb[:, :, None, :]).reshape(w, w).astype(bf16)

    tril = jnp.tril(jnp.ones((SGU_CHUNK, SGU_CHUNK), f32))
    wmix_p = (sgu_w[l] * tril).astype(bf16)
    bias_p = jnp.repeat(sgu_b[l].T, SGU_GC, axis=1)
    t8 = SUBLANE
    rep = SGU_CHUNK // t8
    w8 = (sgu_w[l] * tril)[:, :t8, :t8]
    seq_eye = jnp.eye(rep, dtype=f32)
    wmix_s = (seq_eye[None, :, None, :, None] * w8[:, None, :, None, :]).reshape(
        SGU_GROUPS, SGU_CHUNK, SGU_CHUNK).astype(bf16)
    bias_s = jnp.tile(bias_p[:t8], (rep, 1))
    return dict(
        w_proj=w_proj,
        w_mem_kv=w_mem_kv[l].astype(bf16),
        gla=dict(wa=jnp.pad(gla_wa2[l], ((0, LANE - GLA_RANK), (0, 0))).astype(bf16),
                 ba=gla_ba[l].reshape(1, -1), ng=gla_norm_g[l].reshape(1, -1)),
        lru=dict(conv_w=lru_conv_w[l], conv_b=lru_conv_b[l].reshape(1, w), wr=block_diag(lru_wr[l]),
                 br=lru_br[l].reshape(1, w), wi=block_diag(lru_wi[l]), bi=lru_bi[l].reshape(1, w),
                 lam=lru_L[l].reshape(1, w)),
        sinks=swa_sinks[l],
        sgu=dict(g=sgu_ln_g[l].reshape(1, w), b=sgu_ln_b[l].reshape(1, w), wmix_p=wmix_p, bias_p=bias_p,
                 wmix_s=wmix_s, bias_s=bias_s),
        merge=dict(wg=w_gates, wb=w_branch[l].astype(bf16), wm=w_branch_mem[l].astype(bf16), wo=w_out[l].astype(bf16),
                   g=ln_g[l].reshape(1, d), b=ln_b[l].reshape(1, d)),
    )


def _layer(x, lw, grp, st, layer, tag):
    nseq_total, t = grp["batch"], grp["seq"]
    n_tok = nseq_total * t
    x, xb = x
    xb3 = xb.reshape(nseq_total, t, D_MODEL)
    w_t = lw["w_proj"]
    proj = _matmul_call(xb, w_t, min(grp["proj_tm"], n_tok), 1024, "proj_" + tag, w_transposed=True, n_out=N_PROJ,
                        layer=layer)
    proj3 = proj.reshape(nseq_total, t, N_PROJ)

    lt = grp["lru"]
    y_lru, hlast, hist = _lru_call(proj3, st["hist0"], st["h0"], lw["lru"], nseq_total // lt[0], lt[0], t // lt[1],
                                   lt[1], "lru_" + tag)
    short = grp["kind"] == "s"
    gt = grp["gla"]
    if short:
        y_gla, s_out = _gla_step_call(proj3, st["gla0"], layer, lw["gla"], nseq_total // gt[0], gt[0], t,
                                      "gla_" + tag)
    else:
        y_gla, s_out = _gla_call(proj3, st["gla0"], lw["gla"], nseq_total // gt[0], gt[0], t // gt[1], gt[1],
                                 "gla_" + tag)
    wt = grp["swa"]
    if short:
        y_swa, k_last, v_last = _swa_step_call(proj3, lw["sinks"], st["k_past"], st["v_past"], layer, grp["pos0"],
                                               nseq_total // wt[0], wt[0], t, "swa_" + tag)
    else:
        y_swa, k_last, v_last = _swa_call(proj3, lw["sinks"], st["k_past"], st["v_past"], grp["pos0"],
                                          nseq_total // wt[0], wt[0], t // wt[1], wt[1], "swa_" + tag)
    sg = lw["sgu"]
    sgu_out = _sgu_call(xb, w_t, layer, sg["g"], sg["b"], sg["wmix_" + grp["kind"]], sg["bias_" + grp["kind"]], n_tok,
                        grp["sgu_tiles"], grp["kind"] == "s", "sgu_" + tag)
    mt = grp["mem"]
    if short:
        y_mem = _mem_step_call(xb3, w_t, st["mk"], st["mv"], layer, nseq_total // mt[0], mt[0], t, "mem_" + tag)
    else:
        y_mem = _mem_call(xb3, w_t, layer, st["mk"], st["mv"], nseq_total // mt[0], mt[0], t // mt[1], mt[1],
                          "mem_" + tag)
    ys = tuple(y.reshape(n_tok, y.shape[-1]) for y in (y_gla, y_lru, y_swa, sgu_out[0], y_mem))
    x_new = _merge_call(ys, x, lw["ln_in"], lw["merge"], layer, min(256, n_tok), "merge_" + tag)
    return x_new, dict(gla=s_out, hlast=hlast, hist=hist, k_last=k_last, v_last=v_last,
                       vn=sgu_out[1] if len(sgu_out) > 1 else None)


_PROMPT = dict(kind="p", pos0=0, proj_tm=2048, lru=(1, 256), gla=(4, 128), swa=(4, 128), sgu_tiles=4, mem=(1, 512))
_SAMPLE = dict(kind="s", pos0=PAST_LEN, proj_tm=1024, lru=(32, 8), gla=(16, 8), swa=(16, 8), sgu_tiles=8,
               mem=(16, 8))


def kernel(x_prompt, x_sample, mem_prompt, state_gla, state_lru_h, state_lru_conv, cache_swa_k, cache_swa_v,
           cache_mem_k, cache_mem_v, ln_in_g, ln_in_b, w_in, gla_wa2, gla_ba, gla_norm_g, lru_conv_w, lru_conv_b,
           lru_wr, lru_br, lru_wi, lru_bi, lru_L, swa_sinks, sgu_ln_g, sgu_ln_b, sgu_w, sgu_b, w_mem_kv, w_branch,
           w_branch_mem, w_out, ln_g, ln_b):
    bp, tp, d = x_prompt.shape
    bs, ts, _ = x_sample.shape
    w = BRANCH_W
    kvw = SWA_KV_HEADS * HEAD_DIM
    gp = dict(_PROMPT, batch=bp, seq=tp)
    gs = dict(_SAMPLE, batch=bs, seq=ts)

    xp = (x_prompt.reshape(bp * tp, d), _ln_call(x_prompt.reshape(bp * tp, d), ln_in_g, ln_in_b))
    xs = (x_sample.reshape(bs * ts, d), _ln_call(x_sample.reshape(bs * ts, d), ln_in_g, ln_in_b))
    mem2 = mem_prompt.reshape(bp * N_MEM, d)

    swa_k_t, swa_v_t, mem_k_t, mem_v_t = (jnp.transpose(c, (0, 1, 3, 4, 2))
                                          for c in (cache_swa_k, cache_swa_v, cache_mem_k, cache_mem_v))

    w_proj, w_gates = _prep_w_in(w_in)
    outs_p, outs_s, mks, mvs = [], [], [], []
    for l in range(DEPTH):
        lw = _prep_layer(l, w_proj, w_gates, gla_wa2, gla_ba, gla_norm_g, lru_conv_w, lru_conv_b, lru_wr, lru_br,
                         lru_wi, lru_bi, lru_L, swa_sinks, sgu_ln_g, sgu_ln_b, sgu_w, sgu_b, w_mem_kv, w_branch,
                         w_branch_mem, w_out, ln_g, ln_b)
        lw["ln_in"] = (ln_in_g, ln_in_b)
        mkv = _matmul_call(mem2, lw["w_mem_kv"], bp * N_MEM, 2 * MEM_W, "memkv_%d" % l)
        mk = mkv[:, :MEM_W].reshape(bp, N_MEM, MEM_W)
        mv = mkv[:, MEM_W:].reshape(bp, N_MEM, MEM_W)
        st_p = dict(hist0=jnp.zeros((bp, SUBLANE, w), f32), h0=jnp.zeros((bp, 1, w), f32),
                    gla0=jnp.zeros((bp, GLA_HEADS, GLA_DK, GLA_DV), f32),
                    k_past=jnp.zeros((bp, WINDOW, kvw), f32), v_past=jnp.zeros((bp, WINDOW, kvw), f32),
                    mk=mk, mv=mv)
        st_s = dict(hist0=jnp.pad(state_lru_conv[l], ((0, 0), (SUBLANE - (CONV_W - 1), 0), (0, 0))),
                    h0=state_lru_h[l][:, None, :], gla0=state_gla,
                    k_past=swa_k_t, v_past=swa_v_t, mk=mem_k_t, mv=mem_v_t)
        xp, op = _layer(xp, lw, gp, st_p, l, "p%d" % l)
        xs, os_ = _layer(xs, lw, gs, st_s, l, "s%d" % l)
        outs_p.append(op)
        outs_s.append(os_)
        mks.append(mk.reshape(bp, N_MEM, MEM_HEADS, HEAD_DIM))
        mvs.append(mv.reshape(bp, N_MEM, MEM_HEADS, HEAD_DIM))

    def stack(outs, fn):
        return jnp.stack([fn(o) for o in outs])

    def window(a):
        return a.reshape(a.shape[0], WINDOW, SWA_KV_HEADS, HEAD_DIM)

    def window_t(a):
        return jnp.transpose(a, (0, 3, 1, 2))

    return (
        xp[0].reshape(bp, tp, d), xs[0].reshape(bs, ts, d),
        stack(outs_p, lambda o: o["gla"]), stack(outs_s, lambda o: o["gla"]),
        stack(outs_p, lambda o: o["hlast"][:, SUBLANE - 1]), stack(outs_s, lambda o: o["hlast"][:, SUBLANE - 1]),
        stack(outs_p, lambda o: o["hist"][:, SUBLANE - (CONV_W - 1):]),
        stack(outs_s, lambda o: o["hist"][:, SUBLANE - (CONV_W - 1):]),
        stack(outs_p, lambda o: window(o["k_last"])), stack(outs_s, lambda o: window_t(o["k_last"])),
        stack(outs_p, lambda o: window(o["v_last"])), stack(outs_s, lambda o: window_t(o["v_last"])),
        jnp.stack(mks), jnp.stack(mvs),
        stack(outs_s, lambda o: o["vn"].reshape(bs, ts, w)),
    )
```

```python
import functools
import math

import jax
import jax.numpy as jnp
import numpy as np
from jax import lax
from jax.experimental import pallas as pl
from jax.experimental.pallas import tpu as pltpu

f32 = jnp.float32
bf16 = jnp.bfloat16

D_MODEL = 1024
DEPTH = 2
PAST_LEN = 8192
BRANCH_W = 512
GLA_HEADS = 4
GLA_DK = 64
GLA_DV = 128
GLA_RANK = 16
GLA_TAU = 16.0
LRU_BLOCKS = 8
LRU_BS = 64
CONV_W = 4
LRU_C = 8.0
HEAD_DIM = 64
SWA_HEADS = 8
SWA_KV_HEADS = 2
SWA_GROUP = 4
WINDOW = 128
ROT_DIM = 16
ROPE_THETA = 500000.0
SGU_GROUPS = 4
SGU_GC = 128
SGU_CHUNK = 128
N_MEM = 256
MEM_HEADS = 4
MEM_W = 256
LN_EPS = 1e-5
RMS_EPS = 1e-6
DN_ALPHA = (2 * DEPTH) ** 0.25

LANE = 128
SUBLANE = 8

_ORIG = (("gq", 256), ("gk", 256), ("gv", 512), ("glr", 16), ("gz", 512), ("lx", 512), ("lz", 512), ("sq", 512),
         ("sk", 128), ("sv", 128), ("sz", 512), ("su", 512), ("svv", 512), ("suz", 512), ("mq", 256), ("mz", 256),
         ("gates", 5 * D_MODEL))
_ORIG_OFF = {}
_off = 0
for _n, _w in _ORIG:
    _ORIG_OFF[_n] = (_off, _w)
    _off += _w
_PROJ_RUNS = (("gq", "gv"), ("gz", "sq"), ("sz", "sz"), ("sk", "sv"), ("glr", "glr"), None, ("su", "mz"))
_SEG = {}
_off = 0
for _run in _PROJ_RUNS:
    if _run is None:
        N_PROJ = -(-_off // 1024) * 1024
        _off = N_PROJ
        continue
    _names = [n for n, _ in _ORIG]
    for _n in _names[_names.index(_run[0]):_names.index(_run[1]) + 1]:
        _w = max(_ORIG_OFF[_n][1], 128)
        assert _off % _w == 0
        _SEG[_n] = (_off, _w)
        _off += _w
N_WROWS = _off


def _dot(a, b):
    return jnp.dot(a.astype(bf16), b.astype(bf16), preferred_element_type=f32)


def _dot_nt(a, b):
    return lax.dot_general(a.astype(bf16), b.astype(bf16), (((1,), (1,)), ((), ())), preferred_element_type=f32)


def _dot_tn(a, b):
    return lax.dot_general(a.astype(bf16), b.astype(bf16), (((0,), (0,)), ((), ())), preferred_element_type=f32)


def _sigmoid(x):
    return 0.5 * jnp.tanh(0.5 * x) + 0.5


def _silu(x):
    return x * _sigmoid(x)


def _log_sigmoid(x):
    return jnp.minimum(x, 0.0) - jnp.log(1.0 + jnp.exp(-jnp.abs(x)))


def _layer_norm(x, g, b):
    mu = jnp.mean(x, axis=-1, keepdims=True)
    xc = x - mu
    var = jnp.mean(xc * xc, axis=-1, keepdims=True)
    return xc * lax.rsqrt(var + LN_EPS) * g + b


def _params(*sem):
    return pltpu.CompilerParams(dimension_semantics=sem)


def _ln_kernel(x_ref, g_ref, b_ref, ob_ref):
    ob_ref[...] = _layer_norm(x_ref[...], g_ref[...], b_ref[...]).astype(bf16)


def _ln_call(x, g, b, tm=512):
    n, d = x.shape
    tm = min(tm, n)
    return pl.pallas_call(
        _ln_kernel,
        grid=(n // tm,),
        in_specs=[pl.BlockSpec((tm, d), lambda i: (i, 0)), pl.BlockSpec((1, d), lambda i: (0, 0)),
                  pl.BlockSpec((1, d), lambda i: (0, 0))],
        out_specs=pl.BlockSpec((tm, d), lambda i: (i, 0)),
        out_shape=jax.ShapeDtypeStruct((n, d), bf16),
        compiler_params=_params("parallel"),
        name="ln_in",
    )(x, g.reshape(1, d), b.reshape(1, d))


def _matmul_kernel(x_ref, w_ref, o_ref, *, w_transposed):
    o_ref[...] = (_dot_nt if w_transposed else _dot)(x_ref[...], w_ref[...])


def _matmul_call(x, w, tm, tn, name, w_transposed=False, n_out=None, layer=None):
    m, k = x.shape
    n = n_out or (w.shape[-2] if w_transposed else w.shape[1])
    if layer is not None:
        w_spec = pl.BlockSpec((None, tn, k), lambda i, j: (layer, j, 0))
    elif w_transposed:
        w_spec = pl.BlockSpec((tn, k), lambda i, j: (j, 0))
    else:
        w_spec = pl.BlockSpec((k, tn), lambda i, j: (0, j))
    return pl.pallas_call(
        functools.partial(_matmul_kernel, w_transposed=w_transposed),
        grid=(m // tm, n // tn),
        in_specs=[pl.BlockSpec((tm, k), lambda i, j: (i, 0)), w_spec],
        out_specs=pl.BlockSpec((tm, tn), lambda i, j: (i, j)),
        out_shape=jax.ShapeDtypeStruct((m, n), f32),
        compiler_params=_params("parallel", "arbitrary"),
        name=name,
    )(x, w)


def _seg_spec(name, nseq, rows):
    off, width = _SEG[name]
    assert off + width <= N_PROJ
    cb = off // width
    return pl.BlockSpec((nseq, rows, width), lambda b, c: (b, c, cb))


def _tok_spec(nseq, rows, width):
    return pl.BlockSpec((nseq, rows, width), lambda b, c: (b, c, 0))


def _w_spec(first, last, layer):
    off = _SEG[first][0]
    rows = _SEG[last][0] + _SEG[last][1] - off
    assert off % rows == 0
    return pl.BlockSpec((None, rows, D_MODEL), lambda *_: (layer, off // rows, 0))


def _seg_scratch(name, *lead):
    return pltpu.VMEM((*lead, _SEG[name][1]), f32)


def _project(x_ref, pairs):
    x2 = x_ref[...].reshape(-1, D_MODEL)
    for w_ref, seg_refs in pairs:
        p = lax.dot_general(x2, w_ref[...], (((1,), (1,)), ((), ())), preferred_element_type=f32)
        off = 0
        for s_ref in seg_refs:
            width = s_ref.shape[-1]
            s_ref[...] = p[:, off:off + width].reshape(s_ref.shape)
            off += width


def _const_spec(shape):
    nd = len(shape)
    return pl.BlockSpec(shape, lambda b, c: (0,) * nd)


def _lru_kernel(lx_ref, lz_ref, hist0_ref, h0_ref, cw_ref, cb_ref, wr_ref, br_ref, wi_ref, bi_ref, lam_ref,
                y_ref, hlast_ref, hist_out_ref, hist_ref, hc_ref, *, nseq, tc):
    c = pl.program_id(1)
    w = BRANCH_W

    @pl.when(c == 0)
    def _():
        hist_ref[...] = hist0_ref[...]
        hc_ref[...] = h0_ref[...]

    x = lx_ref[...]
    xfull = jnp.concatenate([hist_ref[...], x], axis=1)

    def tap(j):
        return cw_ref[j:j + 1, :].reshape(1, 1, w)

    y = cb_ref[...].reshape(1, 1, w) + x * tap(CONV_W - 1)
    for s in range(1, CONV_W):
        y = y + pltpu.roll(xfull, s, 1)[:, SUBLANE:, :] * tap(CONV_W - 1 - s)
    hist_ref[...] = xfull[:, tc:, :]
    hist_out_ref[...] = xfull[:, tc:, :]

    xc = y.reshape(nseq * tc, w)
    r = _sigmoid(_dot(xc, wr_ref[...]) + br_ref[...])
    i = _sigmoid(_dot(xc, wi_ref[...]) + bi_ref[...])
    log_a = (LRU_C * r) * _log_sigmoid(lam_ref[...])
    a = jnp.exp(log_a)
    u = jnp.sqrt(jnp.tanh(-log_a) * (a * a + 1.0)) * (i * xc)

    acc_a = a.reshape(nseq, tc, w)
    acc_u = u.reshape(nseq, tc, w)
    t = lax.broadcasted_iota(jnp.int32, (nseq, tc, w), 1)
    d = 1
    while d < tc:
        if d % SUBLANE:
            ok = t >= d
            a_sh = jnp.where(ok, pltpu.roll(acc_a, d, 1), 1.0)
            u_sh = jnp.where(ok, pltpu.roll(acc_u, d, 1), 0.0)
            acc_u = acc_a * u_sh + acc_u
            acc_a = acc_a * a_sh
        else:
            new_u = acc_a[:, d:, :] * acc_u[:, :tc - d, :] + acc_u[:, d:, :]
            new_a = acc_a[:, d:, :] * acc_a[:, :tc - d, :]
            acc_u = jnp.concatenate([acc_u[:, :d, :], new_u], axis=1)
            acc_a = jnp.concatenate([acc_a[:, :d, :], new_a], axis=1)
        d *= 2
    h = acc_a * hc_ref[...] + acc_u
    hc_ref[...] = h[:, tc - 1:tc, :]
    hlast_ref[...] = h[:, tc - SUBLANE:, :]
    y_ref[...] = (h * _silu(lz_ref[...])).astype(y_ref.dtype)


def _lru_call(proj, hist0, h0, lw, nb, nseq, nc, tc, name):
    w = BRANCH_W
    kern = functools.partial(_lru_kernel, nseq=nseq, tc=tc)
    return pl.pallas_call(
        kern,
        grid=(nb, nc),
        in_specs=[_seg_spec("lx", nseq, tc), _seg_spec("lz", nseq, tc),
                  pl.BlockSpec((nseq, SUBLANE, w), lambda b, c: (b, 0, 0)),
                  pl.BlockSpec((nseq, 1, w), lambda b, c: (b, 0, 0)),
                  _const_spec((CONV_W, w)), _const_spec((1, w)), _const_spec((w, w)), _const_spec((1, w)),
                  _const_spec((w, w)), _const_spec((1, w)), _const_spec((1, w))],
        out_specs=[_tok_spec(nseq, tc, w),
                   pl.BlockSpec((nseq, SUBLANE, w), lambda b, c: (b, 0, 0)),
                   pl.BlockSpec((nseq, SUBLANE, w), lambda b, c: (b, 0, 0))],
        out_shape=[jax.ShapeDtypeStruct((nb * nseq, nc * tc, w), bf16),
                   jax.ShapeDtypeStruct((nb * nseq, SUBLANE, w), f32),
                   jax.ShapeDtypeStruct((nb * nseq, SUBLANE, w), f32)],
        scratch_shapes=[pltpu.VMEM((nseq, SUBLANE, w), f32), pltpu.VMEM((nseq, 1, w), f32)],
        compiler_params=_params("parallel", "arbitrary"),
        name=name,
    )(proj, proj, hist0, h0, lw["conv_w"], lw["conv_b"], lw["wr"], lw["br"], lw["wi"], lw["bi"], lw["lam"])


def _gla_consts(c):
    t = np.arange(c)[:, None]
    u = np.arange(c)[None, :]
    blocks = [u <= t, u > t]
    masks = [t == u]
    m = 1
    while m < c:
        t0 = (t // m) * m
        odd = (t // m) % 2 == 1
        blocks.append(odd & (u >= t0) & (u <= t))
        blocks.append((~odd) & (u > t) & (u <= t0 + m - 1))
        masks.append((t // (2 * m) == u // (2 * m)) & odd & ((u // m) % 2 == 0))
        m *= 2
    return (np.concatenate(blocks, 0).astype(np.float32), np.stack(masks).astype(np.float32))


def _gla_kernel(gq_ref, gk_ref, gv_ref, gz_ref, glr_ref, s0_ref, wa_ref, ba_ref, ng_ref, d_ref, m_ref,
                y_ref, sout_ref, s_ref, att_ref, o_ref, upd_ref, *, nseq, c):
    ci = pl.program_id(1)
    nlev = int(math.log2(c))
    hk = GLA_HEADS * GLA_DK
    heads = range(GLA_HEADS)

    @pl.when(ci == 0)
    def _():
        s_ref[...] = s0_ref[...]

    def ks(h):
        return slice(h * GLA_DK, (h + 1) * GLA_DK)

    def vs(h):
        return slice(h * GLA_DV, (h + 1) * GLA_DV)

    z = _dot(glr_ref[...].reshape(nseq * c, LANE), wa_ref[...]) + ba_ref[...]
    la = _log_sigmoid(z) * (1.0 / GLA_TAU)
    hi = la.astype(bf16)
    r1 = la - hi.astype(f32)
    mid = r1.astype(bf16)
    lo = (r1 - mid.astype(f32)).astype(bf16)
    hml = jnp.concatenate([hi, mid, lo], axis=1)

    q_in, k_st, dec_rows, qf, kf = [], [], [], [], []
    for n in range(nseq):
        hml_n = hml[n * c:(n + 1) * c]

        p = jnp.dot(d_ref[0:c, :], hml_n, preferred_element_type=f32)
        b = p[:, :hk] + p[:, hk:2 * hk] + p[:, 2 * hk:]

        def decay(lev, query_side):
            m = 2 ** lev
            if m < SUBLANE:
                blk = 2 + 2 * lev + (0 if query_side else 1)
                pm = jnp.dot(d_ref[blk * c:(blk + 1) * c, :], hml_n[:, :2 * hk], preferred_element_type=f32)
                return jnp.exp(pm[:, :hk] + pm[:, hk:])
            pieces = []
            for j in range(c // m):
                rows = b[j * m:(j + 1) * m]
                if query_side and j % 2 == 1:
                    pieces.append(rows - b[j * m - 1:j * m])
                elif not query_side and j % 2 == 0:
                    pieces.append(b[(j + 1) * m - 1:(j + 1) * m] - rows)
                else:
                    pieces.append(jnp.zeros((m, hk), f32))
            return jnp.exp(jnp.concatenate(pieces, axis=0))

        q = gq_ref[n] * (GLA_DK ** -0.5)
        k = gk_ref[n]
        eb = jnp.exp(b)
        q_in.append((q * eb).astype(bf16))
        k_st.append((k * jnp.exp(b[c - 1:c] - b)).astype(bf16))
        dec_rows.append(eb[c - 1:c, :])
        qf.append([q.astype(bf16)] + [(q * decay(lev, True)).astype(bf16) for lev in range(nlev)])
        kf.append([k.astype(bf16)] + [(k * decay(lev, False)).astype(bf16) for lev in range(nlev)])

    for n in range(nseq):
        for lev in range(nlev + 1):
            for h in heads:
                att_ref[n, h, lev] = _dot_nt(qf[n][lev][:, ks(h)], kf[n][lev][:, ks(h)])

    att = m_ref[0][None, None] * att_ref[:, :, 0]
    for lev in range(1, nlev + 1):
        att = att + m_ref[lev][None, None] * att_ref[:, :, lev]
    att = att.astype(bf16)

    for n in range(nseq):
        v = gv_ref[n].astype(bf16)
        for h in heads:
            v_h = v[:, vs(h)]
            o_ref[n, h] = _dot(q_in[n][:, ks(h)], s_ref[n, h]) + _dot(att[n, h], v_h)
            upd_ref[n, h] = _dot_tn(k_st[n][:, ks(h)], v_h)

    eye = (lax.broadcasted_iota(jnp.int32, (GLA_DK, GLA_DK), 0)
           == lax.broadcasted_iota(jnp.int32, (GLA_DK, GLA_DK), 1))[None, None]
    dec = jnp.stack([jnp.stack([dec_rows[n][:, ks(h)] for h in heads]) for n in range(nseq)])
    dec_col = jnp.sum(jnp.where(eye, jnp.broadcast_to(dec, (nseq, GLA_HEADS, GLA_DK, GLA_DK)), 0.0),
                      axis=3, keepdims=True)
    s_new = s_ref[...] * dec_col + upd_ref[...]
    s_ref[...] = s_new
    sout_ref[...] = s_new
    o = o_ref[...]
    o = o * lax.rsqrt(jnp.mean(o * o, axis=-1, keepdims=True) + RMS_EPS) * ng_ref[...]
    for n in range(nseq):
        for h in heads:
            y_ref[n, :, vs(h)] = (o[n, h] * _silu(gz_ref[n, :, vs(h)])).astype(y_ref.dtype)


_GLA_SEGS = ("gq", "gk", "gv", "gz", "glr")


def _gla_call(proj, s0, gw, nb, nseq, nc, c, name):
    dstack, masks = _gla_consts(c)
    kern = functools.partial(_gla_kernel, nseq=nseq, c=c)
    hk = GLA_HEADS * GLA_DK
    st_spec = pl.BlockSpec((nseq, GLA_HEADS, GLA_DK, GLA_DV), lambda b, ci: (b, 0, 0, 0))
    return pl.pallas_call(
        kern,
        grid=(nb, nc),
        in_specs=[_seg_spec(s, nseq, c) for s in _GLA_SEGS] + [
            st_spec, _const_spec((LANE, hk)), _const_spec((1, hk)), _const_spec((1, GLA_DV)),
            _const_spec(dstack.shape), _const_spec(masks.shape)],
        out_specs=[_tok_spec(nseq, c, BRANCH_W), st_spec],
        out_shape=[jax.ShapeDtypeStruct((nb * nseq, nc * c, BRANCH_W), bf16),
                   jax.ShapeDtypeStruct((nb * nseq, GLA_HEADS, GLA_DK, GLA_DV), f32)],
        scratch_shapes=[pltpu.VMEM((nseq, GLA_HEADS, GLA_DK, GLA_DV), f32),
                        pltpu.VMEM((nseq, GLA_HEADS, masks.shape[0], c, c), f32),
                        pltpu.VMEM((nseq, GLA_HEADS, c, GLA_DV), f32),
                        pltpu.VMEM((nseq, GLA_HEADS, GLA_DK, GLA_DV), f32)],
        compiler_params=_params("parallel", "arbitrary"),
        name=name,
    )(*([proj] * len(_GLA_SEGS)), s0, gw["wa"], gw["ba"], gw["ng"], jnp.asarray(dstack, bf16), jnp.asarray(masks))


def _gla_step_consts(t, nseq):
    dstack, masks = _gla_consts(t)
    eye = np.eye(nseq, dtype=np.float32)
    dbd = np.concatenate([np.kron(eye, dstack[i * t:(i + 1) * t]) for i in range(dstack.shape[0] // t)], axis=0)
    mbd = np.stack([np.kron(eye, m) for m in masks])
    return dbd, mbd


def _gla_step_kernel(gq_ref, gk_ref, gv_ref, gz_ref, glr_ref, s0_ref, wa_ref, ba_ref, ng_ref, d_ref, m_ref,
                     y_ref, sout_ref, *, nseq, t):
    r = nseq * t
    nlev = int(math.log2(t))
    hk = GLA_HEADS * GLA_DK
    q = gq_ref[...].reshape(r, hk) * (GLA_DK ** -0.5)
    k = gk_ref[...].reshape(r, hk)
    v = gv_ref[...].reshape(r, BRANCH_W)
    gz = gz_ref[...].reshape(r, BRANCH_W)
    z = _dot(glr_ref[...].reshape(r, LANE), wa_ref[...]) + ba_ref[...]
    la = _log_sigmoid(z) * (1.0 / GLA_TAU)
    hi = la.astype(bf16)
    r1 = la - hi.astype(f32)
    mid = r1.astype(bf16)
    lo = (r1 - mid.astype(f32)).astype(bf16)
    hml = jnp.concatenate([hi, mid, lo], axis=1)

    def decay(blk):
        p = jnp.dot(d_ref[blk * r:(blk + 1) * r, :], hml, preferred_element_type=f32)
        return jnp.exp(p[:, :hk] + p[:, hk:2 * hk] + p[:, 2 * hk:])

    eb = decay(0)
    q_in = q * eb
    k_st = k * decay(1)
    dec3 = eb.reshape(nseq, t, hk)[:, t - 1:t, :]
    qf = [q]
    kf = [k]
    for lev in range(nlev):
        qf.append(q * decay(2 + 2 * lev))
        kf.append(k * decay(3 + 2 * lev))

    own = (lax.broadcasted_iota(jnp.int32, (r, nseq * GLA_DK), 0) // t
           == lax.broadcasted_iota(jnp.int32, (r, nseq * GLA_DK), 1) // GLA_DK)
    eye = (lax.broadcasted_iota(jnp.int32, (GLA_DK, GLA_DK), 0)
           == lax.broadcasted_iota(jnp.int32, (GLA_DK, GLA_DK), 1))[None]

    def spread(x):
        x2 = jnp.concatenate([x, x], axis=1)
        return jnp.where(own, jnp.concatenate([x2] * (nseq // 2), axis=1), 0.0)

    ys = []
    for h in range(GLA_HEADS):
        ks = slice(h * GLA_DK, (h + 1) * GLA_DK)
        vs = slice(h * GLA_DV, (h + 1) * GLA_DV)
        att = jnp.zeros((r, r), f32)
        for lev in range(nlev + 1):
            att = att + m_ref[lev] * _dot_nt(qf[lev][:, ks], kf[lev][:, ks])
        s_h = s0_ref[:, h]
        v_h = v[:, vs]
        o = _dot(spread(q_in[:, ks]), s_h.reshape(nseq * GLA_DK, GLA_DV)) + _dot(att, v_h)
        upd = _dot_tn(spread(k_st[:, ks]), v_h)
        dec_col = jnp.sum(jnp.where(eye, jnp.broadcast_to(dec3[:, :, ks], (nseq, GLA_DK, GLA_DK)), 0.0),
                          axis=2, keepdims=True)
        sout_ref[:, h] = s_h * dec_col + upd.reshape(nseq, GLA_DK, GLA_DV)
        o = o * lax.rsqrt(jnp.mean(o * o, axis=-1, keepdims=True) + RMS_EPS) * ng_ref[...]
        ys.append(o * _silu(gz[:, vs]))
    y_ref[...] = jnp.concatenate(ys, axis=1).reshape(nseq, t, BRANCH_W).astype(y_ref.dtype)


def _gla_step_call(proj, s0_all, layer, gw, nb, nseq, t, name):
    dbd, mbd = _gla_step_consts(t, nseq)
    kern = functools.partial(_gla_step_kernel, nseq=nseq, t=t)
    hk = GLA_HEADS * GLA_DK
    st_spec = pl.BlockSpec((nseq, GLA_HEADS, GLA_DK, GLA_DV), lambda b, ci: (b, 0, 0, 0))
    s0_spec = pl.BlockSpec((None, nseq, GLA_HEADS, GLA_DK, GLA_DV), lambda b, ci: (layer, b, 0, 0, 0))
    return pl.pallas_call(
        kern,
        grid=(nb, 1),
        in_specs=[_seg_spec(s, nseq, t) for s in _GLA_SEGS] + [
            s0_spec, _const_spec((LANE, hk)), _const_spec((1, hk)), _const_spec((1, GLA_DV)),
            _const_spec(dbd.shape), _const_spec(mbd.shape)],
        out_specs=[_tok_spec(nseq, t, BRANCH_W), st_spec],
        out_shape=[jax.ShapeDtypeStruct((nb * nseq, t, BRANCH_W), bf16),
                   jax.ShapeDtypeStruct((nb * nseq, GLA_HEADS, GLA_DK, GLA_DV), f32)],
        compiler_params=_params("parallel", "arbitrary"),
        name=name,
    )(*([proj] * len(_GLA_SEGS)), s0_all, gw["wa"], gw["ba"], gw["ng"], jnp.asarray(dbd, bf16), jnp.asarray(mbd))


def _rope_tables(pos0, t):
    half = ROT_DIM // 2
    dim = jnp.arange(LANE) % HEAD_DIM
    inv = ROPE_THETA ** (-(dim % half).astype(f32) / half)
    ang = (pos0 + jnp.arange(t)).astype(f32)[:, None] * inv[None, :]
    cos, sin = jnp.cos(ang), jnp.sin(ang)
    first, second = (dim < half)[None, :], ((dim >= half) & (dim < ROT_DIM))[None, :]
    c_tab = jnp.where(first | second, cos, 1.0)
    sa_tab = jnp.where(first, -sin, 0.0)
    sb_tab = jnp.where(second, sin, 0.0)
    return c_tab, sa_tab, sb_tab


def _rope(x, c_tab, sa_tab, sb_tab):
    wd = x.shape[-1]
    ax = x.ndim - 1
    rep = wd // LANE
    half = ROT_DIM // 2
    if rep > 1:
        c_tab, sa_tab, sb_tab = (jnp.concatenate([tb] * rep, axis=-1) for tb in (c_tab, sa_tab, sb_tab))
    return x * c_tab + pltpu.roll(x, wd - half, ax) * sa_tab + pltpu.roll(x, half, ax) * sb_tab


def _swa_kernel(sink_ref, sq_ref, sz_ref, sk_ref, sv_ref, ct_ref, sat_ref, sbt_ref, kp_ref, vp_ref,
                y_ref, klast_ref, vlast_ref, kprev_ref, vprev_ref, sp_ref, sc_ref, o_ref, den_ref,
                *, nseq, qb, pos0):
    blk = pl.program_id(1)
    hd = HEAD_DIM

    @pl.when(blk == 0)
    def _():
        kprev_ref[...] = kp_ref[...]
        vprev_ref[...] = vp_ref[...]

    tabs = tuple(r[...][None] for r in (ct_ref, sat_ref, sbt_ref))
    q3 = _rope(sq_ref[...], *tabs) * (hd ** -0.5)
    k3 = _rope(sk_ref[...], *tabs)
    v3 = sv_ref[...]
    klast_ref[...] = k3
    vlast_ref[...] = v3

    for n in range(nseq):
        for kv in range(SWA_KV_HEADS):
            ds = slice(kv * hd, (kv + 1) * hd)
            qs = jnp.concatenate(
                [q3[n][:, (kv * SWA_GROUP + g) * hd:(kv * SWA_GROUP + g + 1) * hd] for g in range(SWA_GROUP)],
                axis=0).astype(bf16)
            sp_ref[n, kv] = _dot_nt(qs, kprev_ref[n][:, ds])
            sc_ref[n, kv] = _dot_nt(qs, k3[n][:, ds])

    mrows = SWA_GROUP * qb
    qi = lax.broadcasted_iota(jnp.int32, (mrows, WINDOW), 0) % qb
    kj = lax.broadcasted_iota(jnp.int32, (mrows, WINDOW), 1)
    past_ok = (kj >= qi) & (kj >= (WINDOW - pos0) - blk * qb)
    cur_ok = kj <= qi
    sink = jnp.stack([jnp.concatenate([jnp.full((qb, 1), sink_ref[kv * SWA_GROUP + g], f32)
                                       for g in range(SWA_GROUP)], axis=0) for kv in range(SWA_KV_HEADS)])[None]
    s_p = jnp.where(past_ok[None, None], sp_ref[...], -jnp.inf)
    s_c = jnp.where(cur_ok[None, None], sc_ref[...], -jnp.inf)
    m = jnp.maximum(jnp.max(jnp.maximum(s_p, s_c), axis=3, keepdims=True), sink)
    p_p = jnp.exp(s_p - m).astype(bf16)
    p_c = jnp.exp(s_c - m).astype(bf16)
    ones = jnp.ones((WINDOW, LANE), bf16)
    for n in range(nseq):
        vp = vprev_ref[n].astype(bf16)
        vc = v3[n].astype(bf16)
        for kv in range(SWA_KV_HEADS):
            o_ref[n, kv] = _dot(p_p[n, kv], vp) + _dot(p_c[n, kv], vc)
            den_ref[n, kv] = _dot(p_p[n, kv], ones) + _dot(p_c[n, kv], ones)
    o = o_ref[...] / (den_ref[...] + jnp.exp(sink - m))
    outs = [o[:, j // SWA_GROUP, (j % SWA_GROUP) * qb:(j % SWA_GROUP + 1) * qb,
              (j // SWA_GROUP) * hd:(j // SWA_GROUP + 1) * hd] for j in range(SWA_HEADS)]
    y_ref[...] = (jnp.concatenate(outs, axis=2) * _silu(sz_ref[...])).astype(y_ref.dtype)
    kprev_ref[...] = k3
    vprev_ref[...] = v3


_SWA_SEGS = ("sq", "sz", "sk", "sv")


def _swa_call(proj, sinks, k_past, v_past, pos0, nb, nseq, nc, qb, name):
    assert qb == WINDOW
    t_total = nc * qb
    c_tab, sa_tab, sb_tab = _rope_tables(pos0, t_total)
    kern = functools.partial(_swa_kernel, nseq=nseq, qb=qb, pos0=pos0)
    kvw = SWA_KV_HEADS * HEAD_DIM
    mrows = SWA_GROUP * qb
    tab_spec = pl.BlockSpec((qb, LANE), lambda b, c: (c, 0))
    past_spec = pl.BlockSpec((nseq, WINDOW, kvw), lambda b, c: (b, 0, 0))
    return pl.pallas_call(
        kern,
        grid=(nb, nc),
        in_specs=[pl.BlockSpec(memory_space=pltpu.SMEM)] + [_seg_spec(s, nseq, qb) for s in _SWA_SEGS] + [
            tab_spec, tab_spec, tab_spec, past_spec, past_spec],
        out_specs=[_tok_spec(nseq, qb, BRANCH_W), past_spec, past_spec],
        out_shape=[jax.ShapeDtypeStruct((nb * nseq, nc * qb, BRANCH_W), bf16),
                   jax.ShapeDtypeStruct((nb * nseq, WINDOW, kvw), f32),
                   jax.ShapeDtypeStruct((nb * nseq, WINDOW, kvw), f32)],
        scratch_shapes=[pltpu.VMEM((nseq, WINDOW, kvw), f32), pltpu.VMEM((nseq, WINDOW, kvw), f32),
                        pltpu.VMEM((nseq, SWA_KV_HEADS, mrows, WINDOW), f32),
                        pltpu.VMEM((nseq, SWA_KV_HEADS, mrows, qb), f32),
                        pltpu.VMEM((nseq, SWA_KV_HEADS, mrows, kvw), f32),
                        pltpu.VMEM((nseq, SWA_KV_HEADS, mrows, kvw), f32)],
        compiler_params=_params("parallel", "arbitrary"),
        name=name,
    )(sinks, *([proj] * len(_SWA_SEGS)), c_tab, sa_tab, sb_tab, k_past, v_past)


def _swa_step_kernel(sink_ref, sq_ref, sz_ref, sk_ref, sv_ref, ct_ref, sat_ref, sbt_ref, kp_ref, vp_ref,
                     y_ref, klast_ref, vlast_ref, sp_ref, sc_ref, o_ref, *, nseq, t, pos0):
    hd = HEAD_DIM
    kvw = SWA_KV_HEADS * hd
    mrows = SWA_HEADS * t
    tabs = tuple(r[...][None] for r in (ct_ref, sat_ref, sbt_ref))
    q3 = _rope(sq_ref[...], *tabs) * (hd ** -0.5)
    k3 = _rope(sk_ref[...], *tabs)
    v3 = sv_ref[...]

    lane = lax.broadcasted_iota(jnp.int32, (kvw, WINDOW), 1)
    pad = jnp.zeros((WINDOW - t, kvw), f32)

    def shifted(old_t, new):
        new_t = jnp.concatenate([pad, new], axis=0).T
        out = jnp.where(lane >= WINDOW - t, new_t, pltpu.roll(old_t, WINDOW - t, 1))
        return out.reshape(SWA_KV_HEADS, hd, WINDOW)

    for n in range(nseq):
        klast_ref[n] = shifted(kp_ref[n].reshape(kvw, WINDOW), k3[n])
        vlast_ref[n] = shifted(vp_ref[n].reshape(kvw, WINDOW), v3[n])

    zero = jnp.zeros((nseq, t, hd), f32)
    pieces = []
    for j in range(SWA_HEADS):
        qj = q3[:, :, j * hd:(j + 1) * hd]
        pieces.append(jnp.concatenate([qj, zero] if j // SWA_GROUP == 0 else [zero, qj], axis=2))
    qbd = jnp.concatenate(pieces, axis=1).astype(bf16)

    for n in range(nseq):
        sp_ref[n] = _dot(qbd[n], kp_ref[n].reshape(kvw, WINDOW))
        sc_ref[n] = _dot_nt(qbd[n], k3[n])

    qi = lax.broadcasted_iota(jnp.int32, (mrows, WINDOW), 0) % t
    kj = lax.broadcasted_iota(jnp.int32, (mrows, WINDOW), 1)
    past_ok = kj >= qi
    if pos0 < WINDOW:
        past_ok = past_ok & (kj >= WINDOW - pos0)
    qi_c = lax.broadcasted_iota(jnp.int32, (mrows, t), 0) % t
    kj_c = lax.broadcasted_iota(jnp.int32, (mrows, t), 1)
    cur_ok = kj_c <= qi_c
    sink = jnp.concatenate([jnp.full((t, 1), sink_ref[j], f32) for j in range(SWA_HEADS)], axis=0)[None]
    s_p = jnp.where(past_ok[None], sp_ref[...], -jnp.inf)
    s_c = jnp.where(cur_ok[None], sc_ref[...], -jnp.inf)
    m = jnp.maximum(jnp.maximum(jnp.max(s_p, axis=2, keepdims=True), jnp.max(s_c, axis=2, keepdims=True)), sink)
    p_p = jnp.exp(s_p - m)
    p_c = jnp.exp(s_c - m)
    den = jnp.sum(p_p, axis=2, keepdims=True) + jnp.sum(p_c, axis=2, keepdims=True) + jnp.exp(sink - m)
    p_p = p_p.astype(bf16)
    p_c = p_c.astype(bf16)
    for n in range(nseq):
        o_ref[n] = _dot_nt(p_p[n], vp_ref[n].reshape(kvw, WINDOW)) + _dot(p_c[n], v3[n])
    o = o_ref[...] / den
    outs = []
    for j in range(SWA_HEADS):
        kv = j // SWA_GROUP
        outs.append(o[:, j * t:(j + 1) * t, kv * hd:(kv + 1) * hd])
    y_ref[...] = (jnp.concatenate(outs, axis=2) * _silu(sz_ref[...])).astype(y_ref.dtype)


def _swa_step_call(proj, sinks, k_past_t, v_past_t, layer, pos0, nb, nseq, t, name):
    c_tab, sa_tab, sb_tab = _rope_tables(pos0, t)
    kern = functools.partial(_swa_step_kernel, nseq=nseq, t=t, pos0=pos0)
    kvw = SWA_KV_HEADS * HEAD_DIM
    mrows = SWA_HEADS * t
    tab_spec = pl.BlockSpec((t, LANE), lambda b, c: (0, 0))
    past_spec = pl.BlockSpec((None, nseq, SWA_KV_HEADS, HEAD_DIM, WINDOW), lambda b, c: (layer, b, 0, 0, 0))
    new_spec = pl.BlockSpec((nseq, SWA_KV_HEADS, HEAD_DIM, WINDOW), lambda b, c: (b, 0, 0, 0))
    new_shape = jax.ShapeDtypeStruct((nb * nseq, SWA_KV_HEADS, HEAD_DIM, WINDOW), f32)
    return pl.pallas_call(
        kern,
        grid=(nb, 1),
        in_specs=[pl.BlockSpec(memory_space=pltpu.SMEM)] + [_seg_spec(s, nseq, t) for s in _SWA_SEGS] + [
            tab_spec, tab_spec, tab_spec, past_spec, past_spec],
        out_specs=[_tok_spec(nseq, t, BRANCH_W), new_spec, new_spec],
        out_shape=[jax.ShapeDtypeStruct((nb * nseq, t, BRANCH_W), bf16), new_shape, new_shape],
        scratch_shapes=[pltpu.VMEM((nseq, mrows, WINDOW), f32), pltpu.VMEM((nseq, mrows, t), f32),
                        pltpu.VMEM((nseq, mrows, kvw), f32)],
        compiler_params=_params("parallel", "arbitrary"),
        name=name,
    )(sinks, *([proj] * len(_SWA_SEGS)), c_tab, sa_tab, sb_tab, k_past_t, v_past_t)


def _sgu_kernel(x_ref, wu_ref, wv_ref, wz_ref, g_ref, b_ref, wm_ref, bias_ref, y_ref, *rest, ntile, want_vn):
    vn_ref = rest[0] if want_vn else None
    su_ref, sv_ref, sz_ref = rest[-3:]
    _project(x_ref, ((wu_ref, (su_ref,)), (wv_ref, (sv_ref,)), (wz_ref, (sz_ref,))))
    vn = _layer_norm(sv_ref[...], g_ref[...], b_ref[...])
    if want_vn:
        vn_ref[...] = vn
    for r in range(ntile):
        rows = slice(r * SGU_CHUNK, (r + 1) * SGU_CHUNK)
        mixed = jnp.concatenate(
            [jnp.dot(wm_ref[g], vn[rows, g * SGU_GC:(g + 1) * SGU_GC].astype(bf16), preferred_element_type=f32)
             for g in range(SGU_GROUPS)], axis=1)
        y = su_ref[rows, :] * (mixed + bias_ref[...]) * _silu(sz_ref[rows, :])
        y_ref[rows, :] = y.astype(y_ref.dtype)


def _sgu_call(xb, w_t, layer, ln_g, ln_b, wmix, bias, n_tok, ntile, want_vn, name):
    rows = ntile * SGU_CHUNK
    w = BRANCH_W
    kern = functools.partial(_sgu_kernel, ntile=ntile, want_vn=want_vn)

    def const(shape):
        return pl.BlockSpec(shape, lambda i: (0,) * len(shape))

    out_specs = [pl.BlockSpec((rows, w), lambda i: (i, 0))]
    out_shape = [jax.ShapeDtypeStruct((n_tok, w), bf16)]
    if want_vn:
        out_specs.append(pl.BlockSpec((rows, w), lambda i: (i, 0)))
        out_shape.append(jax.ShapeDtypeStruct((n_tok, w), f32))
    return pl.pallas_call(
        kern,
        grid=(n_tok // rows,),
        in_specs=[pl.BlockSpec((rows, D_MODEL), lambda i: (i, 0)), _w_spec("su", "su", layer),
                  _w_spec("svv", "svv", layer), _w_spec("suz", "suz", layer), const((1, w)), const((1, w)),
                  const((SGU_GROUPS, SGU_CHUNK, SGU_CHUNK)), const((SGU_CHUNK, w))],
        out_specs=out_specs,
        out_shape=out_shape,
        scratch_shapes=[_seg_scratch(s, rows) for s in ("su", "svv", "suz")],
        compiler_params=_params("parallel"),
        name=name,
    )(xb, w_t, w_t, w_t, ln_g, ln_b, wmix, bias)


def _mem_kernel(x_ref, wm_ref, mk_ref, mv_ref, y_ref, mq_ref, mz_ref, s_ref, o_ref, den_ref, *, nseq, tq):
    _project(x_ref, ((wm_ref, (mq_ref, mz_ref)),))
    heads = range(MEM_HEADS)
    lane_head = lax.broadcasted_iota(jnp.int32, (tq, MEM_W), 1) // HEAD_DIM
    for n in range(nseq):
        q = mq_ref[n] * (HEAD_DIM ** -0.5)
        mk = mk_ref[n].astype(bf16)
        for h in heads:
            s_ref[n, h] = _dot_nt(jnp.where(lane_head == h, q, 0.0), mk)
    s = s_ref[...]
    p = jnp.exp(s - jnp.max(s, axis=3, keepdims=True)).astype(bf16)
    ones = jnp.ones((N_MEM, MEM_W), bf16)
    for n in range(nseq):
        mv = mv_ref[n].astype(bf16)
        for h in heads:
            o_ref[n, h] = _dot(p[n, h], mv)
            den_ref[n, h] = _dot(p[n, h], ones)
    o = o_ref[...] / den_ref[...]
    acc = jnp.where(lane_head == 0, o[:, 0], 0.0)
    for h in range(1, MEM_HEADS):
        acc = acc + jnp.where(lane_head == h, o[:, h], 0.0)
    y_ref[...] = (acc * _silu(mz_ref[...])).astype(y_ref.dtype)


def _mem_call(xb, w_t, layer, mk, mv, nb, nseq, nc, tq, name):
    kern = functools.partial(_mem_kernel, nseq=nseq, tq=tq)
    kv_spec = pl.BlockSpec((nseq, N_MEM, MEM_W), lambda b, c: (b, 0, 0))
    return pl.pallas_call(
        kern,
        grid=(nb, nc),
        in_specs=[_tok_spec(nseq, tq, D_MODEL), _w_spec("mq", "mz", layer), kv_spec, kv_spec],
        out_specs=_tok_spec(nseq, tq, MEM_W),
        out_shape=jax.ShapeDtypeStruct((nb * nseq, nc * tq, MEM_W), bf16),
        scratch_shapes=[_seg_scratch("mq", nseq, tq), _seg_scratch("mz", nseq, tq),
                        pltpu.VMEM((nseq, MEM_HEADS, tq, N_MEM), f32), pltpu.VMEM((nseq, MEM_HEADS, tq, MEM_W), f32),
                        pltpu.VMEM((nseq, MEM_HEADS, tq, MEM_W), f32)],
        compiler_params=_params("parallel", "arbitrary"),
        name=name,
    )(xb, w_t, mk, mv)


def _mem_step_kernel(x_ref, wm_ref, mk_ref, mv_ref, y_ref, s_ref, o_ref, mq_ref, mz_ref, *, nseq, t):
    mrows = MEM_HEADS * t
    _project(x_ref, ((wm_ref, (mq_ref, mz_ref)),))
    row_head = lax.broadcasted_iota(jnp.int32, (mrows, MEM_W), 0) // t
    lane_head = lax.broadcasted_iota(jnp.int32, (mrows, MEM_W), 1) // HEAD_DIM
    own = (row_head == lane_head)[None]
    q3 = mq_ref[...] * (HEAD_DIM ** -0.5)
    qbd = jnp.where(own, jnp.concatenate([q3] * MEM_HEADS, axis=1), 0.0).astype(bf16)
    for n in range(nseq):
        s_ref[n] = _dot(qbd[n], mk_ref[n].reshape(MEM_W, N_MEM))
    s = s_ref[...]
    p = jnp.exp(s - jnp.max(s, axis=2, keepdims=True))
    den = jnp.sum(p, axis=2, keepdims=True)
    p = p.astype(bf16)
    for n in range(nseq):
        o_ref[n] = _dot_nt(p[n], mv_ref[n].reshape(MEM_W, N_MEM))
    o = jnp.where(own, o_ref[...] / den, 0.0)
    acc = o[:, 0:t, :]
    for h in range(1, MEM_HEADS):
        acc = acc + o[:, h * t:(h + 1) * t, :]
    y_ref[...] = (acc * _silu(mz_ref[...])).astype(y_ref.dtype)


def _mem_step_call(xb, w_t, mk_t, mv_t, layer, nb, nseq, t, name):
    kern = functools.partial(_mem_step_kernel, nseq=nseq, t=t)
    kv_spec = pl.BlockSpec((None, nseq, MEM_HEADS, HEAD_DIM, N_MEM), lambda b, c: (layer, b, 0, 0, 0))
    mrows = MEM_HEADS * t
    return pl.pallas_call(
        kern,
        grid=(nb, 1),
        in_specs=[_tok_spec(nseq, t, D_MODEL), _w_spec("mq", "mz", layer), kv_spec, kv_spec],
        out_specs=_tok_spec(nseq, t, MEM_W),
        out_shape=jax.ShapeDtypeStruct((nb * nseq, t, MEM_W), bf16),
        scratch_shapes=[pltpu.VMEM((nseq, mrows, N_MEM), f32), pltpu.VMEM((nseq, mrows, MEM_W), f32),
                        _seg_scratch("mq", nseq, t), _seg_scratch("mz", nseq, t)],
        compiler_params=_params("parallel", "arbitrary"),
        name=name,
    )(xb, w_t, mk_t, mv_t)


def _merge_kernel(yg_ref, yl_ref, ys_ref, yu_ref, ym_ref, x_ref, ig_ref, ib_ref, wg_ref, wb_ref, wm_ref, wo_ref,
                  g_ref, b_ref, o_ref, ob_ref, *, pre_ln):
    d = D_MODEL
    x = x_ref[...]
    if pre_ln:
        x = _layer_norm(x, ig_ref[...], ib_ref[...])
    xb = x.astype(bf16)

    def gate(n):
        return _sigmoid(_dot_nt(xb, wg_ref[n * d:(n + 1) * d, :]))

    merged = gate(4) * jnp.dot(ym_ref[...], wm_ref[...], preferred_element_type=f32)
    for n, y_ref in enumerate((yg_ref, yl_ref, ys_ref, yu_ref)):
        merged = merged + gate(n) * jnp.dot(y_ref[...], wb_ref[n], preferred_element_type=f32)
    out = _dot(merged, wo_ref[...])
    y = _layer_norm(DN_ALPHA * x + out, g_ref[...], b_ref[...])
    o_ref[...] = y
    ob_ref[...] = y.astype(bf16)


def _merge_call(ys, x, ln_in, mw, layer, tm, name):
    n_tok, d = x.shape
    w = BRANCH_W

    def rows(width):
        return pl.BlockSpec((tm, width), lambda i: (i, 0))

    def const(shape):
        return pl.BlockSpec(shape, lambda i: (0,) * len(shape), pipeline_mode=pl.Buffered(1))

    gates_spec = pl.BlockSpec((None, 5 * d, d), lambda i: (layer, 0, 0), pipeline_mode=pl.Buffered(1))
    return pl.pallas_call(
        functools.partial(_merge_kernel, pre_ln=layer == 0),
        grid=(n_tok // tm,),
        in_specs=[rows(w), rows(w), rows(w), rows(w), rows(MEM_W), rows(d), const((1, d)), const((1, d)),
                  gates_spec, const((4, w, d)), const((MEM_W, d)), const((d, d)), const((1, d)),
                  const((1, d))],
        out_specs=[rows(d), rows(d)],
        out_shape=[jax.ShapeDtypeStruct((n_tok, d), f32), jax.ShapeDtypeStruct((n_tok, d), bf16)],
        compiler_params=_params("parallel"),
        name=name,
    )(*ys, x, ln_in[0].reshape(1, d), ln_in[1].reshape(1, d), mw["wg"], mw["wb"], mw["wm"], mw["wo"], mw["g"],
      mw["b"])


def _prep_w_in(w_in):
    w_t = jnp.swapaxes(w_in, 1, 2).astype(bf16)
    runs = []
    used = 0
    for run in _PROJ_RUNS:
        if run is None:
            runs.append(jnp.zeros((DEPTH, N_PROJ - used, D_MODEL), bf16))
            used = N_PROJ
            continue
        lo = _ORIG_OFF[run[0]][0]
        hi = _ORIG_OFF[run[1]][0] + _ORIG_OFF[run[1]][1]
        runs.append(w_t[:, lo:hi])
        used += hi - lo
    assert used == N_WROWS
    return jnp.concatenate(runs, axis=1), w_t[:, _ORIG_OFF["gates"][0]:]


def _prep_layer(l, w_proj, w_gates, gla_wa2, gla_ba, gla_norm_g, lru_conv_w, lru_conv_b, lru_wr, lru_br, lru_wi,
                lru_bi, lru_L, swa_sinks, sgu_ln_g, sgu_ln_b, sgu_w, sgu_b, w_mem_kv, w_branch, w_branch_mem, w_out,
                ln_g, ln_b):
    d = D_MODEL
    w = BRANCH_W

    def block_diag(wb):
        eye = jnp.eye(LRU_BLOCKS, dtype=f32)
        return (eye[:, None, :, None] * wb[:, :, None, :]).reshape(w, w).astype(bf16)

    tril = jnp.tril(jnp.ones((SGU_CHUNK, SGU_CHUNK), f32))
    wmix_p = (sgu_w[l] * tril).astype(bf16)
    bias_p = jnp.repeat(sgu_b[l].T, SGU_GC, axis=1)
    t8 = SUBLANE
    rep = SGU_CHUNK // t8
    w8 = (sgu_w[l] * tril)[:, :t8, :t8]
    seq_eye = jnp.eye(rep, dtype=f32)
    wmix_s = (seq_eye[None, :, None, :, None] * w8[:, None, :, None, :]).reshape(
        SGU_GROUPS, SGU_CHUNK, SGU_CHUNK).astype(bf16)
    bias_s = jnp.tile(bias_p[:t8], (rep, 1))
    return dict(
        w_proj=w_proj,
        w_mem_kv=w_mem_kv[l].astype(bf16),
        gla=dict(wa=jnp.pad(gla_wa2[l], ((0, LANE - GLA_RANK), (0, 0))).astype(bf16),
                 ba=gla_ba[l].reshape(1, -1), ng=gla_norm_g[l].reshape(1, -1)),
        lru=dict(conv_w=lru_conv_w[l], conv_b=lru_conv_b[l].reshape(1, w), wr=block_diag(lru_wr[l]),
                 br=lru_br[l].reshape(1, w), wi=block_diag(lru_wi[l]), bi=lru_bi[l].reshape(1, w),
                 lam=lru_L[l].reshape(1, w)),
        sinks=swa_sinks[l],
        sgu=dict(g=sgu_ln_g[l].reshape(1, w), b=sgu_ln_b[l].reshape(1, w), wmix_p=wmix_p, bias_p=bias_p,
                 wmix_s=wmix_s, bias_s=bias_s),
        merge=dict(wg=w_gates, wb=w_branch[l].astype(bf16), wm=w_branch_mem[l].astype(bf16), wo=w_out[l].astype(bf16),
                   g=ln_g[l].reshape(1, d), b=ln_b[l].reshape(1, d)),
    )


def _layer(x, lw, grp, st, layer, tag):
    nseq_total, t = grp["batch"], grp["seq"]
    n_tok = nseq_total * t
    x, xb = x
    xb3 = xb.reshape(nseq_total, t, D_MODEL)
    w_t = lw["w_proj"]
    proj = _matmul_call(xb, w_t, min(grp["proj_tm"], n_tok), 1024, "proj_" + tag, w_transposed=True, n_out=N_PROJ,
                        layer=layer)
    proj3 = proj.reshape(nseq_total, t, N_PROJ)

    lt = grp["lru"]
    y_lru, hlast, hist = _lru_call(proj3, st["hist0"], st["h0"], lw["lru"], nseq_total // lt[0], lt[0], t // lt[1],
                                   lt[1], "lru_" + tag)
    short = grp["kind"] == "s"
    gt = grp["gla"]
    if short:
        y_gla, s_out = _gla_step_call(proj3, st["gla0"], layer, lw["gla"], nseq_total // gt[0], gt[0], t,
                                      "gla_" + tag)
    else:
        y_gla, s_out = _gla_call(proj3, st["gla0"], lw["gla"], nseq_total // gt[0], gt[0], t // gt[1], gt[1],
                                 "gla_" + tag)
    wt = grp["swa"]
    if short:
        y_swa, k_last, v_last = _swa_step_call(proj3, lw["sinks"], st["k_past"], st["v_past"], layer, grp["pos0"],
                                               nseq_total // wt[0], wt[0], t, "swa_" + tag)
    else:
        y_swa, k_last, v_last = _swa_call(proj3, lw["sinks"], st["k_past"], st["v_past"], grp["pos0"],
                                          nseq_total // wt[0], wt[0], t // wt[1], wt[1], "swa_" + tag)
    sg = lw["sgu"]
    sgu_out = _sgu_call(xb, w_t, layer, sg["g"], sg["b"], sg["wmix_" + grp["kind"]], sg["bias_" + grp["kind"]], n_tok,
                        grp["sgu_tiles"], grp["kind"] == "s", "sgu_" + tag)
    mt = grp["mem"]
    if short:
        y_mem = _mem_step_call(xb3, w_t, st["mk"], st["mv"], layer, nseq_total // mt[0], mt[0], t, "mem_" + tag)
    else:
        y_mem = _mem_call(xb3, w_t, layer, st["mk"], st["mv"], nseq_total // mt[0], mt[0], t // mt[1], mt[1],
                          "mem_" + tag)
    ys = tuple(y.reshape(n_tok, y.shape[-1]) for y in (y_gla, y_lru, y_swa, sgu_out[0], y_mem))
    x_new = _merge_call(ys, x, lw["ln_in"], lw["merge"], layer, min(256, n_tok), "merge_" + tag)
    return x_new, dict(gla=s_out, hlast=hlast, hist=hist, k_last=k_last, v_last=v_last,
                       vn=sgu_out[1] if len(sgu_out) > 1 else None)


_PROMPT = dict(kind="p", pos0=0, proj_tm=2048, lru=(4, 64), gla=(4, 128), swa=(4, 128), sgu_tiles=4, mem=(1, 512))
_SAMPLE = dict(kind="s", pos0=PAST_LEN, proj_tm=1024, lru=(32, 8), gla=(16, 8), swa=(16, 8), sgu_tiles=8,
               mem=(16, 8))


def kernel(x_prompt, x_sample, mem_prompt, state_gla, state_lru_h, state_lru_conv, cache_swa_k, cache_swa_v,
           cache_mem_k, cache_mem_v, ln_in_g, ln_in_b, w_in, gla_wa2, gla_ba, gla_norm_g, lru_conv_w, lru_conv_b,
           lru_wr, lru_br, lru_wi, lru_bi, lru_L, swa_sinks, sgu_ln_g, sgu_ln_b, sgu_w, sgu_b, w_mem_kv, w_branch,
           w_branch_mem, w_out, ln_g, ln_b):
    bp, tp, d = x_prompt.shape
    bs, ts, _ = x_sample.shape
    w = BRANCH_W
    kvw = SWA_KV_HEADS * HEAD_DIM
    gp = dict(_PROMPT, batch=bp, seq=tp)
    gs = dict(_SAMPLE, batch=bs, seq=ts)

    xp = (x_prompt.reshape(bp * tp, d), _ln_call(x_prompt.reshape(bp * tp, d), ln_in_g, ln_in_b))
    xs = (x_sample.reshape(bs * ts, d), _ln_call(x_sample.reshape(bs * ts, d), ln_in_g, ln_in_b))
    mem2 = mem_prompt.reshape(bp * N_MEM, d)

    swa_k_t, swa_v_t, mem_k_t, mem_v_t = (jnp.transpose(c, (0, 1, 3, 4, 2))
                                          for c in (cache_swa_k, cache_swa_v, cache_mem_k, cache_mem_v))

    w_proj, w_gates = _prep_w_in(w_in)
    outs_p, outs_s, mks, mvs = [], [], [], []
    for l in range(DEPTH):
        lw = _prep_layer(l, w_proj, w_gates, gla_wa2, gla_ba, gla_norm_g, lru_conv_w, lru_conv_b, lru_wr, lru_br,
                         lru_wi, lru_bi, lru_L, swa_sinks, sgu_ln_g, sgu_ln_b, sgu_w, sgu_b, w_mem_kv, w_branch,
                         w_branch_mem, w_out, ln_g, ln_b)
        lw["ln_in"] = (ln_in_g, ln_in_b)
        mkv = _matmul_call(mem2, lw["w_mem_kv"], bp * N_MEM, 2 * MEM_W, "memkv_%d" % l)
        mk = mkv[:, :MEM_W].reshape(bp, N_MEM, MEM_W)
        mv = mkv[:, MEM_W:].reshape(bp, N_MEM, MEM_W)
        st_p = dict(hist0=jnp.zeros((bp, SUBLANE, w), f32), h0=jnp.zeros((bp, 1, w), f32),
                    gla0=jnp.zeros((bp, GLA_HEADS, GLA_DK, GLA_DV), f32),
                    k_past=jnp.zeros((bp, WINDOW, kvw), f32), v_past=jnp.zeros((bp, WINDOW, kvw), f32),
                    mk=mk, mv=mv)
        st_s = dict(hist0=jnp.pad(state_lru_conv[l], ((0, 0), (SUBLANE - (CONV_W - 1), 0), (0, 0))),
                    h0=state_lru_h[l][:, None, :], gla0=state_gla,
                    k_past=swa_k_t, v_past=swa_v_t, mk=mem_k_t, mv=mem_v_t)
        xp, op = _layer(xp, lw, gp, st_p, l, "p%d" % l)
        xs, os_ = _layer(xs, lw, gs, st_s, l, "s%d" % l)
        outs_p.append(op)
        outs_s.append(os_)
        mks.append(mk.reshape(bp, N_MEM, MEM_HEADS, HEAD_DIM))
        mvs.append(mv.reshape(bp, N_MEM, MEM_HEADS, HEAD_DIM))

    def stack(outs, fn):
        return jnp.stack([fn(o) for o in outs])

    def window(a):
        return a.reshape(a.shape[0], WINDOW, SWA_KV_HEADS, HEAD_DIM)

    def window_t(a):
        return jnp.transpose(a, (0, 3, 1, 2))

    return (
        xp[0].reshape(bp, tp, d), xs[0].reshape(bs, ts, d),
        stack(outs_p, lambda o: o["gla"]), stack(outs_s, lambda o: o["gla"]),
        stack(outs_p, lambda o: o["hlast"][:, SUBLANE - 1]), stack(outs_s, lambda o: o["hlast"][:, SUBLANE - 1]),
        stack(outs_p, lambda o: o["hist"][:, SUBLANE - (CONV_W - 1):]),
        stack(outs_s, lambda o: o["hist"][:, SUBLANE - (CONV_W - 1):]),
        stack(outs_p, lambda o: window(o["k_last"])), stack(outs_s, lambda o: window_t(o["k_last"])),
        stack(outs_p, lambda o: window(o["v_last"])), stack(outs_s, lambda o: window_t(o["v_last"])),
        jnp.stack(mks), jnp.stack(mvs),
        stack(outs_s, lambda o: o["vn"].reshape(bs, ts, w)),
    )
```

```python
import functools
import math

import jax
import jax.numpy as jnp
import numpy as np
from jax import lax
from jax.experimental import pallas as pl
from jax.experimental.pallas import tpu as pltpu

f32 = jnp.float32
bf16 = jnp.bfloat16

D_MODEL = 1024
DEPTH = 2
PAST_LEN = 8192
BRANCH_W = 512
GLA_HEADS = 4
GLA_DK = 64
GLA_DV = 128
GLA_RANK = 16
GLA_TAU = 16.0
LRU_BLOCKS = 8
LRU_BS = 64
CONV_W = 4
LRU_C = 8.0
HEAD_DIM = 64
SWA_HEADS = 8
SWA_KV_HEADS = 2
SWA_GROUP = 4
WINDOW = 128
ROT_DIM = 16
ROPE_THETA = 500000.0
SGU_GROUPS = 4
SGU_GC = 128
SGU_CHUNK = 128
N_MEM = 256
MEM_HEADS = 4
MEM_W = 256
LN_EPS = 1e-5
RMS_EPS = 1e-6
DN_ALPHA = (2 * DEPTH) ** 0.25

LANE = 128
SUBLANE = 8

_ORIG = (("gq", 256), ("gk", 256), ("gv", 512), ("glr", 16), ("gz", 512), ("lx", 512), ("lz", 512), ("sq", 512),
         ("sk", 128), ("sv", 128), ("sz", 512), ("su", 512), ("svv", 512), ("suz", 512), ("mq", 256), ("mz", 256),
         ("gates", 5 * D_MODEL))
_ORIG_OFF = {}
_off = 0
for _n, _w in _ORIG:
    _ORIG_OFF[_n] = (_off, _w)
    _off += _w
_PROJ_RUNS = (("gq", "gv"), ("gz", "sq"), ("sz", "sz"), ("sk", "sv"), ("glr", "glr"), None, ("su", "mz"))
_SEG = {}
_off = 0
for _run in _PROJ_RUNS:
    if _run is None:
        N_PROJ = -(-_off // 1024) * 1024
        _off = N_PROJ
        continue
    _names = [n for n, _ in _ORIG]
    for _n in _names[_names.index(_run[0]):_names.index(_run[1]) + 1]:
        _w = max(_ORIG_OFF[_n][1], 128)
        assert _off % _w == 0
        _SEG[_n] = (_off, _w)
        _off += _w
N_WROWS = _off


def _dot(a, b):
    return jnp.dot(a.astype(bf16), b.astype(bf16), preferred_element_type=f32)


def _dot_nt(a, b):
    return lax.dot_general(a.astype(bf16), b.astype(bf16), (((1,), (1,)), ((), ())), preferred_element_type=f32)


def _dot_tn(a, b):
    return lax.dot_general(a.astype(bf16), b.astype(bf16), (((0,), (0,)), ((), ())), preferred_element_type=f32)


def _sigmoid(x):
    return 0.5 * jnp.tanh(0.5 * x) + 0.5


def _silu(x):
    return x * _sigmoid(x)


def _log_sigmoid(x):
    return jnp.minimum(x, 0.0) - jnp.log(1.0 + jnp.exp(-jnp.abs(x)))


def _layer_norm(x, g, b):
    mu = jnp.mean(x, axis=-1, keepdims=True)
    xc = x - mu
    var = jnp.mean(xc * xc, axis=-1, keepdims=True)
    return xc * lax.rsqrt(var + LN_EPS) * g + b


def _params(*sem):
    return pltpu.CompilerParams(dimension_semantics=sem)


def _ln_kernel(x_ref, g_ref, b_ref, ob_ref):
    ob_ref[...] = _layer_norm(x_ref[...], g_ref[...], b_ref[...]).astype(bf16)


def _ln_call(x, g, b, tm=512):
    n, d = x.shape
    tm = min(tm, n)
    return pl.pallas_call(
        _ln_kernel,
        grid=(n // tm,),
        in_specs=[pl.BlockSpec((tm, d), lambda i: (i, 0)), pl.BlockSpec((1, d), lambda i: (0, 0)),
                  pl.BlockSpec((1, d), lambda i: (0, 0))],
        out_specs=pl.BlockSpec((tm, d), lambda i: (i, 0)),
        out_shape=jax.ShapeDtypeStruct((n, d), bf16),
        compiler_params=_params("parallel"),
        name="ln_in",
    )(x, g.reshape(1, d), b.reshape(1, d))


def _matmul_kernel(x_ref, w_ref, o_ref, *, w_transposed):
    o_ref[...] = (_dot_nt if w_transposed else _dot)(x_ref[...], w_ref[...])


def _matmul_call(x, w, tm, tn, name, w_transposed=False, n_out=None, layer=None):
    m, k = x.shape
    n = n_out or (w.shape[-2] if w_transposed else w.shape[1])
    if layer is not None:
        w_spec = pl.BlockSpec((None, tn, k), lambda i, j: (layer, j, 0))
    elif w_transposed:
        w_spec = pl.BlockSpec((tn, k), lambda i, j: (j, 0))
    else:
        w_spec = pl.BlockSpec((k, tn), lambda i, j: (0, j))
    return pl.pallas_call(
        functools.partial(_matmul_kernel, w_transposed=w_transposed),
        grid=(m // tm, n // tn),
        in_specs=[pl.BlockSpec((tm, k), lambda i, j: (i, 0)), w_spec],
        out_specs=pl.BlockSpec((tm, tn), lambda i, j: (i, j)),
        out_shape=jax.ShapeDtypeStruct((m, n), f32),
        compiler_params=_params("parallel", "arbitrary"),
        name=name,
    )(x, w)


def _seg_spec(name, nseq, rows):
    off, width = _SEG[name]
    assert off + width <= N_PROJ
    cb = off // width
    return pl.BlockSpec((nseq, rows, width), lambda b, c: (b, c, cb))


def _tok_spec(nseq, rows, width):
    return pl.BlockSpec((nseq, rows, width), lambda b, c: (b, c, 0))


def _w_spec(first, last, layer):
    off = _SEG[first][0]
    rows = _SEG[last][0] + _SEG[last][1] - off
    assert off % rows == 0
    return pl.BlockSpec((None, rows, D_MODEL), lambda *_: (layer, off // rows, 0))


def _seg_scratch(name, *lead):
    return pltpu.VMEM((*lead, _SEG[name][1]), f32)


def _project(x_ref, pairs):
    x2 = x_ref[...].reshape(-1, D_MODEL)
    for w_ref, seg_refs in pairs:
        p = lax.dot_general(x2, w_ref[...], (((1,), (1,)), ((), ())), preferred_element_type=f32)
        off = 0
        for s_ref in seg_refs:
            width = s_ref.shape[-1]
            s_ref[...] = p[:, off:off + width].reshape(s_ref.shape)
            off += width


def _const_spec(shape):
    nd = len(shape)
    return pl.BlockSpec(shape, lambda b, c: (0,) * nd)


def _lru_kernel(lx_ref, lz_ref, hist0_ref, h0_ref, cw_ref, cb_ref, wr_ref, br_ref, wi_ref, bi_ref, lam_ref,
                y_ref, hlast_ref, hist_out_ref, hist_ref, hc_ref, *, nseq, tc):
    c = pl.program_id(1)
    w = BRANCH_W

    @pl.when(c == 0)
    def _():
        hist_ref[...] = hist0_ref[...]
        hc_ref[...] = h0_ref[...]

    x = lx_ref[...]
    xfull = jnp.concatenate([hist_ref[...], x], axis=1)

    def tap(j):
        return cw_ref[j:j + 1, :].reshape(1, 1, w)

    y = cb_ref[...].reshape(1, 1, w) + x * tap(CONV_W - 1)
    for s in range(1, CONV_W):
        y = y + pltpu.roll(xfull, s, 1)[:, SUBLANE:, :] * tap(CONV_W - 1 - s)
    hist_ref[...] = xfull[:, tc:, :]
    hist_out_ref[...] = xfull[:, tc:, :]

    xc = y.reshape(nseq * tc, w)
    r = _sigmoid(_dot(xc, wr_ref[...]) + br_ref[...])
    i = _sigmoid(_dot(xc, wi_ref[...]) + bi_ref[...])
    log_a = (LRU_C * r) * _log_sigmoid(lam_ref[...])
    a = jnp.exp(log_a)
    u = jnp.sqrt(jnp.tanh(-log_a) * (a * a + 1.0)) * (i * xc)

    acc_a = a.reshape(nseq, tc, w)
    acc_u = u.reshape(nseq, tc, w)
    t = lax.broadcasted_iota(jnp.int32, (nseq, tc, w), 1)
    d = 1
    while d < tc:
        if d % SUBLANE:
            ok = t >= d
            a_sh = jnp.where(ok, pltpu.roll(acc_a, d, 1), 1.0)
            u_sh = jnp.where(ok, pltpu.roll(acc_u, d, 1), 0.0)
            acc_u = acc_a * u_sh + acc_u
            acc_a = acc_a * a_sh
        else:
            new_u = acc_a[:, d:, :] * acc_u[:, :tc - d, :] + acc_u[:, d:, :]
            new_a = acc_a[:, d:, :] * acc_a[:, :tc - d, :]
            acc_u = jnp.concatenate([acc_u[:, :d, :], new_u], axis=1)
            acc_a = jnp.concatenate([acc_a[:, :d, :], new_a], axis=1)
        d *= 2
    h = acc_a * hc_ref[...] + acc_u
    hc_ref[...] = h[:, tc - 1:tc, :]
    hlast_ref[...] = h[:, tc - SUBLANE:, :]
    y_ref[...] = (h * _silu(lz_ref[...])).astype(y_ref.dtype)


def _lru_call(proj, hist0, h0, lw, nb, nseq, nc, tc, name):
    w = BRANCH_W
    kern = functools.partial(_lru_kernel, nseq=nseq, tc=tc)
    return pl.pallas_call(
        kern,
        grid=(nb, nc),
        in_specs=[_seg_spec("lx", nseq, tc), _seg_spec("lz", nseq, tc),
                  pl.BlockSpec((nseq, SUBLANE, w), lambda b, c: (b, 0, 0)),
                  pl.BlockSpec((nseq, 1, w), lambda b, c: (b, 0, 0)),
                  _const_spec((CONV_W, w)), _const_spec((1, w)), _const_spec((w, w)), _const_spec((1, w)),
                  _const_spec((w, w)), _const_spec((1, w)), _const_spec((1, w))],
        out_specs=[_tok_spec(nseq, tc, w),
                   pl.BlockSpec((nseq, SUBLANE, w), lambda b, c: (b, 0, 0)),
                   pl.BlockSpec((nseq, SUBLANE, w), lambda b, c: (b, 0, 0))],
        out_shape=[jax.ShapeDtypeStruct((nb * nseq, nc * tc, w), bf16),
                   jax.ShapeDtypeStruct((nb * nseq, SUBLANE, w), f32),
                   jax.ShapeDtypeStruct((nb * nseq, SUBLANE, w), f32)],
        scratch_shapes=[pltpu.VMEM((nseq, SUBLANE, w), f32), pltpu.VMEM((nseq, 1, w), f32)],
        compiler_params=_params("parallel", "arbitrary"),
        name=name,
    )(proj, proj, hist0, h0, lw["conv_w"], lw["conv_b"], lw["wr"], lw["br"], lw["wi"], lw["bi"], lw["lam"])


def _gla_consts(c):
    t = np.arange(c)[:, None]
    u = np.arange(c)[None, :]
    blocks = [u <= t, u > t]
    masks = [t == u]
    m = 1
    while m < c:
        t0 = (t // m) * m
        odd = (t // m) % 2 == 1
        blocks.append(odd & (u >= t0) & (u <= t))
        blocks.append((~odd) & (u > t) & (u <= t0 + m - 1))
        masks.append((t // (2 * m) == u // (2 * m)) & odd & ((u // m) % 2 == 0))
        m *= 2
    return (np.concatenate(blocks, 0).astype(np.float32), np.stack(masks).astype(np.float32))


def _gla_kernel(gq_ref, gk_ref, gv_ref, gz_ref, glr_ref, s0_ref, wa_ref, ba_ref, ng_ref, d_ref, m_ref,
                y_ref, sout_ref, s_ref, att_ref, o_ref, upd_ref, *, nseq, c):
    ci = pl.program_id(1)
    nlev = int(math.log2(c))
    hk = GLA_HEADS * GLA_DK
    heads = range(GLA_HEADS)

    @pl.when(ci == 0)
    def _():
        s_ref[...] = s0_ref[...]

    def ks(h):
        return slice(h * GLA_DK, (h + 1) * GLA_DK)

    def vs(h):
        return slice(h * GLA_DV, (h + 1) * GLA_DV)

    z = _dot(glr_ref[...].reshape(nseq * c, LANE), wa_ref[...]) + ba_ref[...]
    la = _log_sigmoid(z) * (1.0 / GLA_TAU)
    hi = la.astype(bf16)
    r1 = la - hi.astype(f32)
    mid = r1.astype(bf16)
    lo = (r1 - mid.astype(f32)).astype(bf16)
    hml = jnp.concatenate([hi, mid, lo], axis=1)

    q_in, k_st, dec_rows, qf, kf = [], [], [], [], []
    for n in range(nseq):
        hml_n = hml[n * c:(n + 1) * c]

        p = jnp.dot(d_ref[0:c, :], hml_n, preferred_element_type=f32)
        b = p[:, :hk] + p[:, hk:2 * hk] + p[:, 2 * hk:]

        def decay(lev, query_side):
            m = 2 ** lev
            if m < SUBLANE:
                blk = 2 + 2 * lev + (0 if query_side else 1)
                pm = jnp.dot(d_ref[blk * c:(blk + 1) * c, :], hml_n[:, :2 * hk], preferred_element_type=f32)
                return jnp.exp(pm[:, :hk] + pm[:, hk:])
            pieces = []
            for j in range(c // m):
                rows = b[j * m:(j + 1) * m]
                if query_side and j % 2 == 1:
                    pieces.append(rows - b[j * m - 1:j * m])
                elif not query_side and j % 2 == 0:
                    pieces.append(b[(j + 1) * m - 1:(j + 1) * m] - rows)
                else:
                    pieces.append(jnp.zeros((m, hk), f32))
            return jnp.exp(jnp.concatenate(pieces, axis=0))

        q = gq_ref[n] * (GLA_DK ** -0.5)
        k = gk_ref[n]
        eb = jnp.exp(b)
        q_in.append((q * eb).astype(bf16))
        k_st.append((k * jnp.exp(b[c - 1:c] - b)).astype(bf16))
        dec_rows.append(eb[c - 1:c, :])
        qf.append([q.astype(bf16)] + [(q * decay(lev, True)).astype(bf16) for lev in range(nlev)])
        kf.append([k.astype(bf16)] + [(k * decay(lev, False)).astype(bf16) for lev in range(nlev)])

    for n in range(nseq):
        for lev in range(nlev + 1):
            for h in heads:
                att_ref[n, h, lev] = _dot_nt(qf[n][lev][:, ks(h)], kf[n][lev][:, ks(h)])

    att = m_ref[0][None, None] * att_ref[:, :, 0]
    for lev in range(1, nlev + 1):
        att = att + m_ref[lev][None, None] * att_ref[:, :, lev]
    att = att.astype(bf16)

    for n in range(nseq):
        v = gv_ref[n].astype(bf16)
        for h in heads:
            v_h = v[:, vs(h)]
            o_ref[n, h] = _dot(q_in[n][:, ks(h)], s_ref[n, h]) + _dot(att[n, h], v_h)
            upd_ref[n, h] = _dot_tn(k_st[n][:, ks(h)], v_h)

    eye = (lax.broadcasted_iota(jnp.int32, (GLA_DK, GLA_DK), 0)
           == lax.broadcasted_iota(jnp.int32, (GLA_DK, GLA_DK), 1))[None, None]
    dec = jnp.stack([jnp.stack([dec_rows[n][:, ks(h)] for h in heads]) for n in range(nseq)])
    dec_col = jnp.sum(jnp.where(eye, jnp.broadcast_to(dec, (nseq, GLA_HEADS, GLA_DK, GLA_DK)), 0.0),
                      axis=3, keepdims=True)
    s_new = s_ref[...] * dec_col + upd_ref[...]
    s_ref[...] = s_new
    sout_ref[...] = s_new
    o = o_ref[...]
    o = o * lax.rsqrt(jnp.mean(o * o, axis=-1, keepdims=True) + RMS_EPS) * ng_ref[...]
    for n in range(nseq):
        for h in heads:
            y_ref[n, :, vs(h)] = (o[n, h] * _silu(gz_ref[n, :, vs(h)])).astype(y_ref.dtype)


_GLA_SEGS = ("gq", "gk", "gv", "gz", "glr")


def _gla_call(proj, s0, gw, nb, nseq, nc, c, name):
    dstack, masks = _gla_consts(c)
    kern = functools.partial(_gla_kernel, nseq=nseq, c=c)
    hk = GLA_HEADS * GLA_DK
    st_spec = pl.BlockSpec((nseq, GLA_HEADS, GLA_DK, GLA_DV), lambda b, ci: (b, 0, 0, 0))
    return pl.pallas_call(
        kern,
        grid=(nb, nc),
        in_specs=[_seg_spec(s, nseq, c) for s in _GLA_SEGS] + [
            st_spec, _const_spec((LANE, hk)), _const_spec((1, hk)), _const_spec((1, GLA_DV)),
            _const_spec(dstack.shape), _const_spec(masks.shape)],
        out_specs=[_tok_spec(nseq, c, BRANCH_W), st_spec],
        out_shape=[jax.ShapeDtypeStruct((nb * nseq, nc * c, BRANCH_W), bf16),
                   jax.ShapeDtypeStruct((nb * nseq, GLA_HEADS, GLA_DK, GLA_DV), f32)],
        scratch_shapes=[pltpu.VMEM((nseq, GLA_HEADS, GLA_DK, GLA_DV), f32),
                        pltpu.VMEM((nseq, GLA_HEADS, masks.shape[0], c, c), f32),
                        pltpu.VMEM((nseq, GLA_HEADS, c, GLA_DV), f32),
                        pltpu.VMEM((nseq, GLA_HEADS, GLA_DK, GLA_DV), f32)],
        compiler_params=_params("parallel", "arbitrary"),
        name=name,
    )(*([proj] * len(_GLA_SEGS)), s0, gw["wa"], gw["ba"], gw["ng"], jnp.asarray(dstack, bf16), jnp.asarray(masks))


def _gla_step_consts(t, nseq):
    dstack, masks = _gla_consts(t)
    eye = np.eye(nseq, dtype=np.float32)
    dbd = np.concatenate([np.kron(eye, dstack[i * t:(i + 1) * t]) for i in range(dstack.shape[0] // t)], axis=0)
    mbd = np.stack([np.kron(eye, m) for m in masks])
    return dbd, mbd


def _gla_step_kernel(gq_ref, gk_ref, gv_ref, gz_ref, glr_ref, s0_ref, wa_ref, ba_ref, ng_ref, d_ref, m_ref,
                     *refs, nseq, t, nprev):
    prev_refs, (y_ref, sout_ref) = refs[:nprev], refs[nprev:]
    if nprev:
        for i, p_ref in enumerate(prev_refs):
            sout_ref[i] = p_ref[...]
        sout_ref = sout_ref.at[nprev]
    r = nseq * t
    nlev = int(math.log2(t))
    hk = GLA_HEADS * GLA_DK
    q = gq_ref[...].reshape(r, hk) * (GLA_DK ** -0.5)
    k = gk_ref[...].reshape(r, hk)
    v = gv_ref[...].reshape(r, BRANCH_W)
    gz = gz_ref[...].reshape(r, BRANCH_W)
    z = _dot(glr_ref[...].reshape(r, LANE), wa_ref[...]) + ba_ref[...]
    la = _log_sigmoid(z) * (1.0 / GLA_TAU)
    hi = la.astype(bf16)
    r1 = la - hi.astype(f32)
    mid = r1.astype(bf16)
    lo = (r1 - mid.astype(f32)).astype(bf16)
    hml = jnp.concatenate([hi, mid, lo], axis=1)

    def decay(blk):
        p = jnp.dot(d_ref[blk * r:(blk + 1) * r, :], hml, preferred_element_type=f32)
        return jnp.exp(p[:, :hk] + p[:, hk:2 * hk] + p[:, 2 * hk:])

    eb = decay(0)
    q_in = q * eb
    k_st = k * decay(1)
    dec3 = eb.reshape(nseq, t, hk)[:, t - 1:t, :]
    qf = [q]
    kf = [k]
    for lev in range(nlev):
        qf.append(q * decay(2 + 2 * lev))
        kf.append(k * decay(3 + 2 * lev))

    own = (lax.broadcasted_iota(jnp.int32, (r, nseq * GLA_DK), 0) // t
           == lax.broadcasted_iota(jnp.int32, (r, nseq * GLA_DK), 1) // GLA_DK)
    eye = (lax.broadcasted_iota(jnp.int32, (GLA_DK, GLA_DK), 0)
           == lax.broadcasted_iota(jnp.int32, (GLA_DK, GLA_DK), 1))[None]

    def spread(x):
        x2 = jnp.concatenate([x, x], axis=1)
        return jnp.where(own, jnp.concatenate([x2] * (nseq // 2), axis=1), 0.0)

    ys = []
    for h in range(GLA_HEADS):
        ks = slice(h * GLA_DK, (h + 1) * GLA_DK)
        vs = slice(h * GLA_DV, (h + 1) * GLA_DV)
        att = jnp.zeros((r, r), f32)
        for lev in range(nlev + 1):
            att = att + m_ref[lev] * _dot_nt(qf[lev][:, ks], kf[lev][:, ks])
        s_h = s0_ref[:, h]
        v_h = v[:, vs]
        o = _dot(spread(q_in[:, ks]), s_h.reshape(nseq * GLA_DK, GLA_DV)) + _dot(att, v_h)
        upd = _dot_tn(spread(k_st[:, ks]), v_h)
        dec_col = jnp.sum(jnp.where(eye, jnp.broadcast_to(dec3[:, :, ks], (nseq, GLA_DK, GLA_DK)), 0.0),
                          axis=2, keepdims=True)
        sout_ref[:, h] = s_h * dec_col + upd.reshape(nseq, GLA_DK, GLA_DV)
        o = o * lax.rsqrt(jnp.mean(o * o, axis=-1, keepdims=True) + RMS_EPS) * ng_ref[...]
        ys.append(o * _silu(gz[:, vs]))
    y_ref[...] = jnp.concatenate(ys, axis=1).reshape(nseq, t, BRANCH_W).astype(y_ref.dtype)


def _stacked_specs(prev, inner):
    nd = len(inner)
    one = pl.BlockSpec(inner, lambda b, c: (b,) + (0,) * (nd - 1))
    if not prev:
        return [], one, lambda nbatch: jax.ShapeDtypeStruct((nbatch,) + inner[1:], f32)
    n = len(prev) + 1
    stacked = pl.BlockSpec((n,) + inner, lambda b, c: (0, b) + (0,) * (nd - 1))
    return [one] * len(prev), stacked, lambda nbatch: jax.ShapeDtypeStruct((n, nbatch) + inner[1:], f32)


def _gla_step_call(proj, s0_all, layer, prev, gw, nb, nseq, t, name):
    dbd, mbd = _gla_step_consts(t, nseq)
    kern = functools.partial(_gla_step_kernel, nseq=nseq, t=t, nprev=len(prev))
    hk = GLA_HEADS * GLA_DK
    prev_specs, st_spec, st_shape = _stacked_specs(prev, (nseq, GLA_HEADS, GLA_DK, GLA_DV))
    s0_spec = pl.BlockSpec((None, nseq, GLA_HEADS, GLA_DK, GLA_DV), lambda b, ci: (layer, b, 0, 0, 0))
    return pl.pallas_call(
        kern,
        grid=(nb, 1),
        in_specs=[_seg_spec(s, nseq, t) for s in _GLA_SEGS] + [
            s0_spec, _const_spec((LANE, hk)), _const_spec((1, hk)), _const_spec((1, GLA_DV)),
            _const_spec(dbd.shape), _const_spec(mbd.shape)] + prev_specs,
        out_specs=[_tok_spec(nseq, t, BRANCH_W), st_spec],
        out_shape=[jax.ShapeDtypeStruct((nb * nseq, t, BRANCH_W), bf16), st_shape(nb * nseq)],
        compiler_params=_params("parallel", "arbitrary"),
        name=name,
    )(*([proj] * len(_GLA_SEGS)), s0_all, gw["wa"], gw["ba"], gw["ng"], jnp.asarray(dbd, bf16), jnp.asarray(mbd),
      *prev)


def _rope_tables(pos0, t):
    half = ROT_DIM // 2
    dim = jnp.arange(LANE) % HEAD_DIM
    inv = ROPE_THETA ** (-(dim % half).astype(f32) / half)
    ang = (pos0 + jnp.arange(t)).astype(f32)[:, None] * inv[None, :]
    cos, sin = jnp.cos(ang), jnp.sin(ang)
    first, second = (dim < half)[None, :], ((dim >= half) & (dim < ROT_DIM))[None, :]
    c_tab = jnp.where(first | second, cos, 1.0)
    sa_tab = jnp.where(first, -sin, 0.0)
    sb_tab = jnp.where(second, sin, 0.0)
    return c_tab, sa_tab, sb_tab


def _rope(x, c_tab, sa_tab, sb_tab):
    wd = x.shape[-1]
    ax = x.ndim - 1
    rep = wd // LANE
    half = ROT_DIM // 2
    if rep > 1:
        c_tab, sa_tab, sb_tab = (jnp.concatenate([tb] * rep, axis=-1) for tb in (c_tab, sa_tab, sb_tab))
    return x * c_tab + pltpu.roll(x, wd - half, ax) * sa_tab + pltpu.roll(x, half, ax) * sb_tab


def _swa_kernel(sink_ref, sq_ref, sz_ref, sk_ref, sv_ref, ct_ref, sat_ref, sbt_ref, kp_ref, vp_ref,
                y_ref, klast_ref, vlast_ref, kprev_ref, vprev_ref, sp_ref, sc_ref, o_ref, den_ref,
                *, nseq, qb, pos0):
    blk = pl.program_id(1)
    hd = HEAD_DIM

    @pl.when(blk == 0)
    def _():
        kprev_ref[...] = kp_ref[...]
        vprev_ref[...] = vp_ref[...]

    tabs = tuple(r[...][None] for r in (ct_ref, sat_ref, sbt_ref))
    q3 = _rope(sq_ref[...], *tabs) * (hd ** -0.5)
    k3 = _rope(sk_ref[...], *tabs)
    v3 = sv_ref[...]
    klast_ref[...] = k3
    vlast_ref[...] = v3

    for n in range(nseq):
        for kv in range(SWA_KV_HEADS):
            ds = slice(kv * hd, (kv + 1) * hd)
            qs = jnp.concatenate(
                [q3[n][:, (kv * SWA_GROUP + g) * hd:(kv * SWA_GROUP + g + 1) * hd] for g in range(SWA_GROUP)],
                axis=0).astype(bf16)
            sp_ref[n, kv] = _dot_nt(qs, kprev_ref[n][:, ds])
            sc_ref[n, kv] = _dot_nt(qs, k3[n][:, ds])

    mrows = SWA_GROUP * qb
    qi = lax.broadcasted_iota(jnp.int32, (mrows, WINDOW), 0) % qb
    kj = lax.broadcasted_iota(jnp.int32, (mrows, WINDOW), 1)
    past_ok = (kj >= qi) & (kj >= (WINDOW - pos0) - blk * qb)
    cur_ok = kj <= qi
    sink = jnp.stack([jnp.concatenate([jnp.full((qb, 1), sink_ref[kv * SWA_GROUP + g], f32)
                                       for g in range(SWA_GROUP)], axis=0) for kv in range(SWA_KV_HEADS)])[None]
    s_p = jnp.where(past_ok[None, None], sp_ref[...], -jnp.inf)
    s_c = jnp.where(cur_ok[None, None], sc_ref[...], -jnp.inf)
    m = jnp.maximum(jnp.max(jnp.maximum(s_p, s_c), axis=3, keepdims=True), sink)
    p_p = jnp.exp(s_p - m).astype(bf16)
    p_c = jnp.exp(s_c - m).astype(bf16)
    ones = jnp.ones((WINDOW, LANE), bf16)
    for n in range(nseq):
        vp = vprev_ref[n].astype(bf16)
        vc = v3[n].astype(bf16)
        for kv in range(SWA_KV_HEADS):
            o_ref[n, kv] = _dot(p_p[n, kv], vp) + _dot(p_c[n, kv], vc)
            den_ref[n, kv] = _dot(p_p[n, kv], ones) + _dot(p_c[n, kv], ones)
    o = o_ref[...] / (den_ref[...] + jnp.exp(sink - m))
    outs = [o[:, j // SWA_GROUP, (j % SWA_GROUP) * qb:(j % SWA_GROUP + 1) * qb,
              (j // SWA_GROUP) * hd:(j // SWA_GROUP + 1) * hd] for j in range(SWA_HEADS)]
    y_ref[...] = (jnp.concatenate(outs, axis=2) * _silu(sz_ref[...])).astype(y_ref.dtype)
    kprev_ref[...] = k3
    vprev_ref[...] = v3


_SWA_SEGS = ("sq", "sz", "sk", "sv")


def _swa_call(proj, sinks, k_past, v_past, pos0, nb, nseq, nc, qb, name):
    assert qb == WINDOW
    t_total = nc * qb
    c_tab, sa_tab, sb_tab = _rope_tables(pos0, t_total)
    kern = functools.partial(_swa_kernel, nseq=nseq, qb=qb, pos0=pos0)
    kvw = SWA_KV_HEADS * HEAD_DIM
    mrows = SWA_GROUP * qb
    tab_spec = pl.BlockSpec((qb, LANE), lambda b, c: (c, 0))
    past_spec = pl.BlockSpec((nseq, WINDOW, kvw), lambda b, c: (b, 0, 0))
    return pl.pallas_call(
        kern,
        grid=(nb, nc),
        in_specs=[pl.BlockSpec(memory_space=pltpu.SMEM)] + [_seg_spec(s, nseq, qb) for s in _SWA_SEGS] + [
            tab_spec, tab_spec, tab_spec, past_spec, past_spec],
        out_specs=[_tok_spec(nseq, qb, BRANCH_W), past_spec, past_spec],
        out_shape=[jax.ShapeDtypeStruct((nb * nseq, nc * qb, BRANCH_W), bf16),
                   jax.ShapeDtypeStruct((nb * nseq, WINDOW, kvw), f32),
                   jax.ShapeDtypeStruct((nb * nseq, WINDOW, kvw), f32)],
        scratch_shapes=[pltpu.VMEM((nseq, WINDOW, kvw), f32), pltpu.VMEM((nseq, WINDOW, kvw), f32),
                        pltpu.VMEM((nseq, SWA_KV_HEADS, mrows, WINDOW), f32),
                        pltpu.VMEM((nseq, SWA_KV_HEADS, mrows, qb), f32),
                        pltpu.VMEM((nseq, SWA_KV_HEADS, mrows, kvw), f32),
                        pltpu.VMEM((nseq, SWA_KV_HEADS, mrows, kvw), f32)],
        compiler_params=_params("parallel", "arbitrary"),
        name=name,
    )(sinks, *([proj] * len(_SWA_SEGS)), c_tab, sa_tab, sb_tab, k_past, v_past)


def _swa_step_kernel(sink_ref, sq_ref, sz_ref, sk_ref, sv_ref, ct_ref, sat_ref, sbt_ref, kp_ref, vp_ref,
                     *refs, nseq, t, pos0, nprev):
    prev_refs = refs[:2 * nprev]
    y_ref, klast_ref, vlast_ref, sp_ref, sc_ref, o_ref = refs[2 * nprev:]
    if nprev:
        for i in range(nprev):
            klast_ref[i] = prev_refs[i][...]
            vlast_ref[i] = prev_refs[nprev + i][...]
        klast_ref = klast_ref.at[nprev]
        vlast_ref = vlast_ref.at[nprev]
    hd = HEAD_DIM
    kvw = SWA_KV_HEADS * hd
    mrows = SWA_HEADS * t
    tabs = tuple(r[...][None] for r in (ct_ref, sat_ref, sbt_ref))
    q3 = _rope(sq_ref[...], *tabs) * (hd ** -0.5)
    k3 = _rope(sk_ref[...], *tabs)
    v3 = sv_ref[...]

    lane = lax.broadcasted_iota(jnp.int32, (kvw, WINDOW), 1)
    pad = jnp.zeros((WINDOW - t, kvw), f32)

    def shifted(old_t, new):
        new_t = jnp.concatenate([pad, new], axis=0).T
        out = jnp.where(lane >= WINDOW - t, new_t, pltpu.roll(old_t, WINDOW - t, 1))
        return out.reshape(SWA_KV_HEADS, hd, WINDOW)

    for n in range(nseq):
        klast_ref[n] = shifted(kp_ref[n].reshape(kvw, WINDOW), k3[n])
        vlast_ref[n] = shifted(vp_ref[n].reshape(kvw, WINDOW), v3[n])

    zero = jnp.zeros((nseq, t, hd), f32)
    pieces = []
    for j in range(SWA_HEADS):
        qj = q3[:, :, j * hd:(j + 1) * hd]
        pieces.append(jnp.concatenate([qj, zero] if j // SWA_GROUP == 0 else [zero, qj], axis=2))
    qbd = jnp.concatenate(pieces, axis=1).astype(bf16)

    for n in range(nseq):
        sp_ref[n] = _dot(qbd[n], kp_ref[n].reshape(kvw, WINDOW))
        sc_ref[n] = _dot_nt(qbd[n], k3[n])

    qi = lax.broadcasted_iota(jnp.int32, (mrows, WINDOW), 0) % t
    kj = lax.broadcasted_iota(jnp.int32, (mrows, WINDOW), 1)
    past_ok = kj >= qi
    if pos0 < WINDOW:
        past_ok = past_ok & (kj >= WINDOW - pos0)
    qi_c = lax.broadcasted_iota(jnp.int32, (mrows, t), 0) % t
    kj_c = lax.broadcasted_iota(jnp.int32, (mrows, t), 1)
    cur_ok = kj_c <= qi_c
    sink = jnp.concatenate([jnp.full((t, 1), sink_ref[j], f32) for j in range(SWA_HEADS)], axis=0)[None]
    s_p = jnp.where(past_ok[None], sp_ref[...], -jnp.inf)
    s_c = jnp.where(cur_ok[None], sc_ref[...], -jnp.inf)
    m = jnp.maximum(jnp.maximum(jnp.max(s_p, axis=2, keepdims=True), jnp.max(s_c, axis=2, keepdims=True)), sink)
    p_p = jnp.exp(s_p - m)
    p_c = jnp.exp(s_c - m)
    den = jnp.sum(p_p, axis=2, keepdims=True) + jnp.sum(p_c, axis=2, keepdims=True) + jnp.exp(sink - m)
    p_p = p_p.astype(bf16)
    p_c = p_c.astype(bf16)
    for n in range(nseq):
        o_ref[n] = _dot_nt(p_p[n], vp_ref[n].reshape(kvw, WINDOW)) + _dot(p_c[n], v3[n])
    o = o_ref[...] / den
    outs = []
    for j in range(SWA_HEADS):
        kv = j // SWA_GROUP
        outs.append(o[:, j * t:(j + 1) * t, kv * hd:(kv + 1) * hd])
    y_ref[...] = (jnp.concatenate(outs, axis=2) * _silu(sz_ref[...])).astype(y_ref.dtype)


def _swa_step_call(proj, sinks, k_past_t, v_past_t, layer, prev_k, prev_v, pos0, nb, nseq, t, name):
    c_tab, sa_tab, sb_tab = _rope_tables(pos0, t)
    kern = functools.partial(_swa_step_kernel, nseq=nseq, t=t, pos0=pos0, nprev=len(prev_k))
    kvw = SWA_KV_HEADS * HEAD_DIM
    mrows = SWA_HEADS * t
    tab_spec = pl.BlockSpec((t, LANE), lambda b, c: (0, 0))
    past_spec = pl.BlockSpec((None, nseq, SWA_KV_HEADS, HEAD_DIM, WINDOW), lambda b, c: (layer, b, 0, 0, 0))
    prev_specs, new_spec, new_shape = _stacked_specs(prev_k, (nseq, SWA_KV_HEADS, HEAD_DIM, WINDOW))
    return pl.pallas_call(
        kern,
        grid=(nb, 1),
        in_specs=[pl.BlockSpec(memory_space=pltpu.SMEM)] + [_seg_spec(s, nseq, t) for s in _SWA_SEGS] + [
            tab_spec, tab_spec, tab_spec, past_spec, past_spec] + prev_specs + prev_specs,
        out_specs=[_tok_spec(nseq, t, BRANCH_W), new_spec, new_spec],
        out_shape=[jax.ShapeDtypeStruct((nb * nseq, t, BRANCH_W), bf16), new_shape(nb * nseq),
                   new_shape(nb * nseq)],
        scratch_shapes=[pltpu.VMEM((nseq, mrows, WINDOW), f32), pltpu.VMEM((nseq, mrows, t), f32),
                        pltpu.VMEM((nseq, mrows, kvw), f32)],
        compiler_params=_params("parallel", "arbitrary"),
        name=name,
    )(sinks, *([proj] * len(_SWA_SEGS)), c_tab, sa_tab, sb_tab, k_past_t, v_past_t, *prev_k, *prev_v)


def _sgu_kernel(x_ref, wu_ref, wv_ref, wz_ref, g_ref, b_ref, wm_ref, bias_ref, y_ref, *rest, ntile, want_vn):
    vn_ref = rest[0] if want_vn else None
    su_ref, sv_ref, sz_ref = rest[-3:]
    _project(x_ref, ((wu_ref, (su_ref,)), (wv_ref, (sv_ref,)), (wz_ref, (sz_ref,))))
    vn = _layer_norm(sv_ref[...], g_ref[...], b_ref[...])
    if want_vn:
        vn_ref[...] = vn
    for r in range(ntile):
        rows = slice(r * SGU_CHUNK, (r + 1) * SGU_CHUNK)
        mixed = jnp.concatenate(
            [jnp.dot(wm_ref[g], vn[rows, g * SGU_GC:(g + 1) * SGU_GC].astype(bf16), preferred_element_type=f32)
             for g in range(SGU_GROUPS)], axis=1)
        y = su_ref[rows, :] * (mixed + bias_ref[...]) * _silu(sz_ref[rows, :])
        y_ref[rows, :] = y.astype(y_ref.dtype)


def _sgu_call(xb, w_t, layer, ln_g, ln_b, wmix, bias, n_tok, ntile, want_vn, name):
    rows = ntile * SGU_CHUNK
    w = BRANCH_W
    kern = functools.partial(_sgu_kernel, ntile=ntile, want_vn=want_vn)

    def const(shape):
        return pl.BlockSpec(shape, lambda i: (0,) * len(shape))

    out_specs = [pl.BlockSpec((rows, w), lambda i: (i, 0))]
    out_shape = [jax.ShapeDtypeStruct((n_tok, w), bf16)]
    if want_vn:
        out_specs.append(pl.BlockSpec((rows, w), lambda i: (i, 0)))
        out_shape.append(jax.ShapeDtypeStruct((n_tok, w), f32))
    return pl.pallas_call(
        kern,
        grid=(n_tok // rows,),
        in_specs=[pl.BlockSpec((rows, D_MODEL), lambda i: (i, 0)), _w_spec("su", "su", layer),
                  _w_spec("svv", "svv", layer), _w_spec("suz", "suz", layer), const((1, w)), const((1, w)),
                  const((SGU_GROUPS, SGU_CHUNK, SGU_CHUNK)), const((SGU_CHUNK, w))],
        out_specs=out_specs,
        out_shape=out_shape,
        scratch_shapes=[_seg_scratch(s, rows) for s in ("su", "svv", "suz")],
        compiler_params=_params("parallel"),
        name=name,
    )(xb, w_t, w_t, w_t, ln_g, ln_b, wmix, bias)


def _mem_kernel(x_ref, wm_ref, mk_ref, mv_ref, y_ref, mq_ref, mz_ref, s_ref, o_ref, den_ref, *, nseq, tq):
    _project(x_ref, ((wm_ref, (mq_ref, mz_ref)),))
    heads = range(MEM_HEADS)
    lane_head = lax.broadcasted_iota(jnp.int32, (tq, MEM_W), 1) // HEAD_DIM
    for n in range(nseq):
        q = mq_ref[n] * (HEAD_DIM ** -0.5)
        mk = mk_ref[n].astype(bf16)
        for h in heads:
            s_ref[n, h] = _dot_nt(jnp.where(lane_head == h, q, 0.0), mk)
    s = s_ref[...]
    p = jnp.exp(s - jnp.max(s, axis=3, keepdims=True)).astype(bf16)
    ones = jnp.ones((N_MEM, MEM_W), bf16)
    for n in range(nseq):
        mv = mv_ref[n].astype(bf16)
        for h in heads:
            o_ref[n, h] = _dot(p[n, h], mv)
            den_ref[n, h] = _dot(p[n, h], ones)
    o = o_ref[...] / den_ref[...]
    acc = jnp.where(lane_head == 0, o[:, 0], 0.0)
    for h in range(1, MEM_HEADS):
        acc = acc + jnp.where(lane_head == h, o[:, h], 0.0)
    y_ref[...] = (acc * _silu(mz_ref[...])).astype(y_ref.dtype)


def _mem_call(xb, w_t, layer, mk, mv, nb, nseq, nc, tq, name):
    kern = functools.partial(_mem_kernel, nseq=nseq, tq=tq)
    kv_spec = pl.BlockSpec((nseq, N_MEM, MEM_W), lambda b, c: (b, 0, 0))
    return pl.pallas_call(
        kern,
        grid=(nb, nc),
        in_specs=[_tok_spec(nseq, tq, D_MODEL), _w_spec("mq", "mz", layer), kv_spec, kv_spec],
        out_specs=_tok_spec(nseq, tq, MEM_W),
        out_shape=jax.ShapeDtypeStruct((nb * nseq, nc * tq, MEM_W), bf16),
        scratch_shapes=[_seg_scratch("mq", nseq, tq), _seg_scratch("mz", nseq, tq),
                        pltpu.VMEM((nseq, MEM_HEADS, tq, N_MEM), f32), pltpu.VMEM((nseq, MEM_HEADS, tq, MEM_W), f32),
                        pltpu.VMEM((nseq, MEM_HEADS, tq, MEM_W), f32)],
        compiler_params=_params("parallel", "arbitrary"),
        name=name,
    )(xb, w_t, mk, mv)


def _mem_step_kernel(x_ref, wm_ref, mk_ref, mv_ref, y_ref, s_ref, o_ref, mq_ref, mz_ref, *, nseq, t):
    mrows = MEM_HEADS * t
    _project(x_ref, ((wm_ref, (mq_ref, mz_ref)),))
    row_head = lax.broadcasted_iota(jnp.int32, (mrows, MEM_W), 0) // t
    lane_head = lax.broadcasted_iota(jnp.int32, (mrows, MEM_W), 1) // HEAD_DIM
    own = (row_head == lane_head)[None]
    q3 = mq_ref[...] * (HEAD_DIM ** -0.5)
    qbd = jnp.where(own, jnp.concatenate([q3] * MEM_HEADS, axis=1), 0.0).astype(bf16)
    for n in range(nseq):
        s_ref[n] = _dot(qbd[n], mk_ref[n].reshape(MEM_W, N_MEM))
    s = s_ref[...]
    p = jnp.exp(s - jnp.max(s, axis=2, keepdims=True))
    den = jnp.sum(p, axis=2, keepdims=True)
    p = p.astype(bf16)
    for n in range(nseq):
        o_ref[n] = _dot_nt(p[n], mv_ref[n].reshape(MEM_W, N_MEM))
    o = jnp.where(own, o_ref[...] / den, 0.0)
    acc = o[:, 0:t, :]
    for h in range(1, MEM_HEADS):
        acc = acc + o[:, h * t:(h + 1) * t, :]
    y_ref[...] = (acc * _silu(mz_ref[...])).astype(y_ref.dtype)


def _mem_step_call(xb, w_t, mk_t, mv_t, layer, nb, nseq, t, name):
    kern = functools.partial(_mem_step_kernel, nseq=nseq, t=t)
    kv_spec = pl.BlockSpec((None, nseq, MEM_HEADS, HEAD_DIM, N_MEM), lambda b, c: (layer, b, 0, 0, 0))
    mrows = MEM_HEADS * t
    return pl.pallas_call(
        kern,
        grid=(nb, 1),
        in_specs=[_tok_spec(nseq, t, D_MODEL), _w_spec("mq", "mz", layer), kv_spec, kv_spec],
        out_specs=_tok_spec(nseq, t, MEM_W),
        out_shape=jax.ShapeDtypeStruct((nb * nseq, t, MEM_W), bf16),
        scratch_shapes=[pltpu.VMEM((nseq, mrows, N_MEM), f32), pltpu.VMEM((nseq, mrows, MEM_W), f32),
                        _seg_scratch("mq", nseq, t), _seg_scratch("mz", nseq, t)],
        compiler_params=_params("parallel", "arbitrary"),
        name=name,
    )(xb, w_t, mk_t, mv_t)


def _merge_kernel(yg_ref, yl_ref, ys_ref, yu_ref, ym_ref, x_ref, ig_ref, ib_ref, wg_ref, wb_ref, wm_ref, wo_ref,
                  g_ref, b_ref, o_ref, ob_ref, *, pre_ln):
    d = D_MODEL
    x = x_ref[...]
    if pre_ln:
        x = _layer_norm(x, ig_ref[...], ib_ref[...])
    xb = x.astype(bf16)

    def gate(n):
        return _sigmoid(_dot_nt(xb, wg_ref[0, n * d:(n + 1) * d, :]))

    merged = gate(4) * jnp.dot(ym_ref[...], wm_ref[...], preferred_element_type=f32)
    for n, y_ref in enumerate((yg_ref, yl_ref, ys_ref, yu_ref)):
        merged = merged + gate(n) * jnp.dot(y_ref[...], wb_ref[n], preferred_element_type=f32)
    out = _dot(merged, wo_ref[...])
    y = _layer_norm(DN_ALPHA * x + out, g_ref[...], b_ref[...])
    o_ref[...] = y
    ob_ref[...] = y.astype(bf16)


def _merge_call(ys, x, ln_in, mw, layer, tm, name):
    n_tok, d = x.shape
    w = BRANCH_W

    def rows(width):
        return pl.BlockSpec((tm, width), lambda i: (i, 0))

    def const(shape):
        return pl.BlockSpec(shape, lambda i: (0,) * len(shape), pipeline_mode=pl.Buffered(1))

    gates_spec = pl.BlockSpec((pl.Element(1), pl.Element(5 * d), pl.Element(d)),
                              lambda i: (layer, _ORIG_OFF["gates"][0], 0), pipeline_mode=pl.Buffered(1))
    return pl.pallas_call(
        functools.partial(_merge_kernel, pre_ln=layer == 0),
        grid=(n_tok // tm,),
        in_specs=[rows(w), rows(w), rows(w), rows(w), rows(MEM_W), rows(d), const((1, d)), const((1, d)),
                  gates_spec, const((4, w, d)), const((MEM_W, d)), const((d, d)), const((1, d)),
                  const((1, d))],
        out_specs=[rows(d), rows(d)],
        out_shape=[jax.ShapeDtypeStruct((n_tok, d), f32), jax.ShapeDtypeStruct((n_tok, d), bf16)],
        compiler_params=_params("parallel"),
        name=name,
    )(*ys, x, ln_in[0].reshape(1, d), ln_in[1].reshape(1, d), mw["wg"], mw["wb"], mw["wm"], mw["wo"], mw["g"],
      mw["b"])


def _prep_w_in(w_in):
    w_t = jnp.swapaxes(w_in, 1, 2).astype(bf16)
    runs = []
    used = 0
    for run in _PROJ_RUNS:
        if run is None:
            runs.append(jnp.zeros((DEPTH, N_PROJ - used, D_MODEL), bf16))
            used = N_PROJ
            continue
        lo = _ORIG_OFF[run[0]][0]
        hi = _ORIG_OFF[run[1]][0] + _ORIG_OFF[run[1]][1]
        runs.append(w_t[:, lo:hi])
        used += hi - lo
    assert used == N_WROWS
    return jnp.concatenate(runs, axis=1), w_t


def _prep_layer(l, w_proj, w_gates, gla_wa2, gla_ba, gla_norm_g, lru_conv_w, lru_conv_b, lru_wr, lru_br, lru_wi,
                lru_bi, lru_L, swa_sinks, sgu_ln_g, sgu_ln_b, sgu_w, sgu_b, w_mem_kv, w_branch, w_branch_mem, w_out,
                ln_g, ln_b):
    d = D_MODEL
    w = BRANCH_W

    def block_diag(wb):
        eye = jnp.eye(LRU_BLOCKS, dtype=f32)
        return (eye[:, None, :, None] * wb[:, :, None, :]).reshape(w, w).astype(bf16)

    tril = jnp.tril(jnp.ones((SGU_CHUNK, SGU_CHUNK), f32))
    wmix_p = (sgu_w[l] * tril).astype(bf16)
    bias_p = jnp.repeat(sgu_b[l].T, SGU_GC, axis=1)
    t8 = SUBLANE
    rep = SGU_CHUNK // t8
    w8 = (sgu_w[l] * tril)[:, :t8, :t8]
    seq_eye = jnp.eye(rep, dtype=f32)
    wmix_s = (seq_eye[None, :, None, :, None] * w8[:, None, :, None, :]).reshape(
        SGU_GROUPS, SGU_CHUNK, SGU_CHUNK).astype(bf16)
    bias_s = jnp.tile(bias_p[:t8], (rep, 1))
    return dict(
        w_proj=w_proj,
        w_mem_kv=w_mem_kv[l].astype(bf16),
        gla=dict(wa=jnp.pad(gla_wa2[l], ((0, LANE - GLA_RANK), (0, 0))).astype(bf16),
                 ba=gla_ba[l].reshape(1, -1), ng=gla_norm_g[l].reshape(1, -1)),
        lru=dict(conv_w=lru_conv_w[l], conv_b=lru_conv_b[l].reshape(1, w), wr=block_diag(lru_wr[l]),
                 br=lru_br[l].reshape(1, w), wi=block_diag(lru_wi[l]), bi=lru_bi[l].reshape(1, w),
                 lam=lru_L[l].reshape(1, w)),
        sinks=swa_sinks[l],
        sgu=dict(g=sgu_ln_g[l].reshape(1, w), b=sgu_ln_b[l].reshape(1, w), wmix_p=wmix_p, bias_p=bias_p,
                 wmix_s=wmix_s, bias_s=bias_s),
        merge=dict(wg=w_gates, wb=w_branch[l].astype(bf16), wm=w_branch_mem[l].astype(bf16), wo=w_out[l].astype(bf16),
                   g=ln_g[l].reshape(1, d), b=ln_b[l].reshape(1, d)),
    )


def _layer(x, lw, grp, st, layer, tag, prev=()):
    nseq_total, t = grp["batch"], grp["seq"]
    n_tok = nseq_total * t
    x, xb = x
    xb3 = xb.reshape(nseq_total, t, D_MODEL)
    w_t = lw["w_proj"]
    proj = _matmul_call(xb, w_t, min(grp["proj_tm"], n_tok), 1024, "proj_" + tag, w_transposed=True, n_out=N_PROJ,
                        layer=layer)
    proj3 = proj.reshape(nseq_total, t, N_PROJ)

    lt = grp["lru"]
    y_lru, hlast, hist = _lru_call(proj3, st["hist0"], st["h0"], lw["lru"], nseq_total // lt[0], lt[0], t // lt[1],
                                   lt[1], "lru_" + tag)
    short = grp["kind"] == "s"
    gt = grp["gla"]
    if short:
        y_gla, s_out = _gla_step_call(proj3, st["gla0"], layer, [p["gla"] for p in prev], lw["gla"],
                                      nseq_total // gt[0], gt[0], t, "gla_" + tag)
    else:
        y_gla, s_out = _gla_call(proj3, st["gla0"], lw["gla"], nseq_total // gt[0], gt[0], t // gt[1], gt[1],
                                 "gla_" + tag)
    wt = grp["swa"]
    if short:
        y_swa, k_last, v_last = _swa_step_call(proj3, lw["sinks"], st["k_past"], st["v_past"], layer,
                                               [p["k_last"] for p in prev], [p["v_last"] for p in prev],
                                               grp["pos0"], nseq_total // wt[0], wt[0], t, "swa_" + tag)
    else:
        y_swa, k_last, v_last = _swa_call(proj3, lw["sinks"], st["k_past"], st["v_past"], grp["pos0"],
                                          nseq_total // wt[0], wt[0], t // wt[1], wt[1], "swa_" + tag)
    sg = lw["sgu"]
    sgu_out = _sgu_call(xb, w_t, layer, sg["g"], sg["b"], sg["wmix_" + grp["kind"]], sg["bias_" + grp["kind"]], n_tok,
                        grp["sgu_tiles"], grp["kind"] == "s", "sgu_" + tag)
    mt = grp["mem"]
    if short:
        y_mem = _mem_step_call(xb3, w_t, st["mk"], st["mv"], layer, nseq_total // mt[0], mt[0], t, "mem_" + tag)
    else:
        y_mem = _mem_call(xb3, w_t, layer, st["mk"], st["mv"], nseq_total // mt[0], mt[0], t // mt[1], mt[1],
                          "mem_" + tag)
    ys = tuple(y.reshape(n_tok, y.shape[-1]) for y in (y_gla, y_lru, y_swa, sgu_out[0], y_mem))
    x_new = _merge_call(ys, x, lw["ln_in"], lw["merge"], layer, min(256, n_tok), "merge_" + tag)
    return x_new, dict(gla=s_out, hlast=hlast, hist=hist, k_last=k_last, v_last=v_last,
                       vn=sgu_out[1] if len(sgu_out) > 1 else None)


_PROMPT = dict(kind="p", pos0=0, proj_tm=2048, lru=(4, 64), gla=(4, 128), swa=(4, 128), sgu_tiles=4, mem=(1, 512))
_SAMPLE = dict(kind="s", pos0=PAST_LEN, proj_tm=1024, lru=(32, 8), gla=(16, 8), swa=(16, 8), sgu_tiles=8,
               mem=(16, 8))


def kernel(x_prompt, x_sample, mem_prompt, state_gla, state_lru_h, state_lru_conv, cache_swa_k, cache_swa_v,
           cache_mem_k, cache_mem_v, ln_in_g, ln_in_b, w_in, gla_wa2, gla_ba, gla_norm_g, lru_conv_w, lru_conv_b,
           lru_wr, lru_br, lru_wi, lru_bi, lru_L, swa_sinks, sgu_ln_g, sgu_ln_b, sgu_w, sgu_b, w_mem_kv, w_branch,
           w_branch_mem, w_out, ln_g, ln_b):
    bp, tp, d = x_prompt.shape
    bs, ts, _ = x_sample.shape
    w = BRANCH_W
    kvw = SWA_KV_HEADS * HEAD_DIM
    gp = dict(_PROMPT, batch=bp, seq=tp)
    gs = dict(_SAMPLE, batch=bs, seq=ts)

    xp = (x_prompt.reshape(bp * tp, d), _ln_call(x_prompt.reshape(bp * tp, d), ln_in_g, ln_in_b))
    xs = (x_sample.reshape(bs * ts, d), _ln_call(x_sample.reshape(bs * ts, d), ln_in_g, ln_in_b))
    mem2 = mem_prompt.reshape(bp * N_MEM, d)

    swa_k_t, swa_v_t, mem_k_t, mem_v_t = (jnp.transpose(c, (0, 1, 3, 4, 2))
                                          for c in (cache_swa_k, cache_swa_v, cache_mem_k, cache_mem_v))

    w_proj, w_gates = _prep_w_in(w_in)
    outs_p, outs_s, mks, mvs = [], [], [], []
    for l in range(DEPTH):
        lw = _prep_layer(l, w_proj, w_gates, gla_wa2, gla_ba, gla_norm_g, lru_conv_w, lru_conv_b, lru_wr, lru_br,
                         lru_wi, lru_bi, lru_L, swa_sinks, sgu_ln_g, sgu_ln_b, sgu_w, sgu_b, w_mem_kv, w_branch,
                         w_branch_mem, w_out, ln_g, ln_b)
        lw["ln_in"] = (ln_in_g, ln_in_b)
        mkv = _matmul_call(mem2, lw["w_mem_kv"], bp * N_MEM, 2 * MEM_W, "memkv_%d" % l)
        mk = mkv[:, :MEM_W].reshape(bp, N_MEM, MEM_W)
        mv = mkv[:, MEM_W:].reshape(bp, N_MEM, MEM_W)
        st_p = dict(hist0=jnp.zeros((bp, SUBLANE, w), f32), h0=jnp.zeros((bp, 1, w), f32),
                    gla0=jnp.zeros((bp, GLA_HEADS, GLA_DK, GLA_DV), f32),
                    k_past=jnp.zeros((bp, WINDOW, kvw), f32), v_past=jnp.zeros((bp, WINDOW, kvw), f32),
                    mk=mk, mv=mv)
        st_s = dict(hist0=jnp.pad(state_lru_conv[l], ((0, 0), (SUBLANE - (CONV_W - 1), 0), (0, 0))),
                    h0=state_lru_h[l][:, None, :], gla0=state_gla,
                    k_past=swa_k_t, v_past=swa_v_t, mk=mem_k_t, mv=mem_v_t)
        xp, op = _layer(xp, lw, gp, st_p, l, "p%d" % l)
        xs, os_ = _layer(xs, lw, gs, st_s, l, "s%d" % l, prev=outs_s if l == DEPTH - 1 else ())
        outs_p.append(op)
        outs_s.append(os_)
        mks.append(mk.reshape(bp, N_MEM, MEM_HEADS, HEAD_DIM))
        mvs.append(mv.reshape(bp, N_MEM, MEM_HEADS, HEAD_DIM))

    def stack(outs, fn):
        return jnp.stack([fn(o) for o in outs])

    def window(a):
        return a.reshape(a.shape[0], WINDOW, SWA_KV_HEADS, HEAD_DIM)

    last_s = outs_s[-1]

    def window_t(a):
        return jnp.transpose(a, (0, 1, 4, 2, 3))

    return (
        xp[0].reshape(bp, tp, d), xs[0].reshape(bs, ts, d),
        stack(outs_p, lambda o: o["gla"]), last_s["gla"],
        stack(outs_p, lambda o: o["hlast"][:, SUBLANE - 1]), stack(outs_s, lambda o: o["hlast"][:, SUBLANE - 1]),
        stack(outs_p, lambda o: o["hist"][:, SUBLANE - (CONV_W - 1):]),
        stack(outs_s, lambda o: o["hist"][:, SUBLANE - (CONV_W - 1):]),
        stack(outs_p, lambda o: window(o["k_last"])), window_t(last_s["k_last"]),
        stack(outs_p, lambda o: window(o["v_last"])), window_t(last_s["v_last"]),
        jnp.stack(mks), jnp.stack(mvs),
        stack(outs_s, lambda o: o["vn"].reshape(bs, ts, w)),
    )
```

```python
import functools
import math

import jax
import jax.numpy as jnp
import numpy as np
from jax import lax
from jax.experimental import pallas as pl
from jax.experimental.pallas import tpu as pltpu

f32 = jnp.float32
bf16 = jnp.bfloat16

D_MODEL = 1024
DEPTH = 2
PAST_LEN = 8192
BRANCH_W = 512
GLA_HEADS = 4
GLA_DK = 64
GLA_DV = 128
GLA_RANK = 16
GLA_TAU = 16.0
LRU_BLOCKS = 8
LRU_BS = 64
CONV_W = 4
LRU_C = 8.0
HEAD_DIM = 64
SWA_HEADS = 8
SWA_KV_HEADS = 2
SWA_GROUP = 4
WINDOW = 128
ROT_DIM = 16
ROPE_THETA = 500000.0
SGU_GROUPS = 4
SGU_GC = 128
SGU_CHUNK = 128
N_MEM = 256
MEM_HEADS = 4
MEM_W = 256
LN_EPS = 1e-5
RMS_EPS = 1e-6
DN_ALPHA = (2 * DEPTH) ** 0.25

LANE = 128
SUBLANE = 8

_ORIG = (("gq", 256), ("gk", 256), ("gv", 512), ("glr", 16), ("gz", 512), ("lx", 512), ("lz", 512), ("sq", 512),
         ("sk", 128), ("sv", 128), ("sz", 512), ("su", 512), ("svv", 512), ("suz", 512), ("mq", 256), ("mz", 256),
         ("gates", 5 * D_MODEL))
_ORIG_OFF = {}
_off = 0
for _n, _w in _ORIG:
    _ORIG_OFF[_n] = (_off, _w)
    _off += _w
_PROJ_RUNS = (("gq", "gv"), ("gz", "sq"), ("sz", "sz"), ("sk", "sv"), ("glr", "glr"), None, ("su", "mz"))
_SEG = {}
_off = 0
for _run in _PROJ_RUNS:
    if _run is None:
        N_PROJ = -(-_off // 1024) * 1024
        _off = N_PROJ
        continue
    _names = [n for n, _ in _ORIG]
    for _n in _names[_names.index(_run[0]):_names.index(_run[1]) + 1]:
        _w = max(_ORIG_OFF[_n][1], 128)
        assert _off % _w == 0
        _SEG[_n] = (_off, _w)
        _off += _w


def _dot(a, b):
    return jnp.dot(a.astype(bf16), b.astype(bf16), preferred_element_type=f32)


def _dot_nt(a, b):
    return lax.dot_general(a.astype(bf16), b.astype(bf16), (((1,), (1,)), ((), ())), preferred_element_type=f32)


def _dot_tn(a, b):
    return lax.dot_general(a.astype(bf16), b.astype(bf16), (((0,), (0,)), ((), ())), preferred_element_type=f32)


def _sigmoid(x):
    return 0.5 * jnp.tanh(0.5 * x) + 0.5


def _silu(x):
    return x * _sigmoid(x)


def _log_sigmoid(x):
    return jnp.minimum(x, 0.0) - jnp.log(1.0 + jnp.exp(-jnp.abs(x)))


def _layer_norm(x, g, b):
    mu = jnp.mean(x, axis=-1, keepdims=True)
    xc = x - mu
    var = jnp.mean(xc * xc, axis=-1, keepdims=True)
    return xc * lax.rsqrt(var + LN_EPS) * g + b


def _params(*sem):
    return pltpu.CompilerParams(dimension_semantics=sem)


def _ln_kernel(x_ref, g_ref, b_ref, ob_ref):
    ob_ref[...] = _layer_norm(x_ref[...], g_ref[...], b_ref[...]).astype(bf16)


def _ln_call(x, g, b, tm=2048):
    n, d = x.shape
    tm = min(tm, n)
    return pl.pallas_call(
        _ln_kernel,
        grid=(n // tm,),
        in_specs=[pl.BlockSpec((tm, d), lambda i: (i, 0)), pl.BlockSpec((1, d), lambda i: (0, 0)),
                  pl.BlockSpec((1, d), lambda i: (0, 0))],
        out_specs=pl.BlockSpec((tm, d), lambda i: (i, 0)),
        out_shape=jax.ShapeDtypeStruct((n, d), bf16),
        compiler_params=_params("parallel"),
        name="ln_in",
    )(x, g.reshape(1, d), b.reshape(1, d))


def _matmul_kernel(x_ref, w_ref, o_ref, *, w_transposed):
    o_ref[...] = (_dot_nt if w_transposed else _dot)(x_ref[...], w_ref[...])


def _matmul_call(x, w, tm, tn, name, w_transposed=False, n_out=None, layer=None):
    m, k = x.shape
    n = n_out or (w.shape[-2] if w_transposed else w.shape[1])
    if layer is not None:
        w_spec = pl.BlockSpec((None, tn, k), lambda i, j: (layer, j, 0))
    elif w_transposed:
        w_spec = pl.BlockSpec((tn, k), lambda i, j: (j, 0))
    else:
        w_spec = pl.BlockSpec((k, tn), lambda i, j: (0, j))
    return pl.pallas_call(
        functools.partial(_matmul_kernel, w_transposed=w_transposed),
        grid=(m // tm, n // tn),
        in_specs=[pl.BlockSpec((tm, k), lambda i, j: (i, 0)), w_spec],
        out_specs=pl.BlockSpec((tm, tn), lambda i, j: (i, j)),
        out_shape=jax.ShapeDtypeStruct((m, n), f32),
        compiler_params=_params("parallel", "arbitrary"),
        name=name,
    )(x, w)


def _seg_spec(name, nseq, rows):
    off, width = _SEG[name]
    assert off + width <= N_PROJ
    cb = off // width
    return pl.BlockSpec((nseq, rows, width), lambda b, c: (b, c, cb))


def _tok_spec(nseq, rows, width):
    return pl.BlockSpec((nseq, rows, width), lambda b, c: (b, c, 0))


def _w_spec(first, last, layer):
    off = _ORIG_OFF[first][0]
    rows = _ORIG_OFF[last][0] + _ORIG_OFF[last][1] - off
    assert off % 16 == 0
    return pl.BlockSpec((pl.Element(1), pl.Element(rows), pl.Element(D_MODEL)), lambda *_: (layer, off, 0))


def _seg_scratch(name, *lead):
    return pltpu.VMEM((*lead, _SEG[name][1]), f32)


def _project(x_ref, pairs):
    x2 = x_ref[...].reshape(-1, D_MODEL)
    for w_ref, seg_refs in pairs:
        p = lax.dot_general(x2, w_ref[0], (((1,), (1,)), ((), ())), preferred_element_type=f32)
        off = 0
        for s_ref in seg_refs:
            width = s_ref.shape[-1]
            s_ref[...] = p[:, off:off + width].reshape(s_ref.shape)
            off += width


def _const_spec(shape):
    nd = len(shape)
    return pl.BlockSpec(shape, lambda b, c: (0,) * nd)


def _lru_kernel(lx_ref, lz_ref, hist0_ref, h0_ref, cw_ref, cb_ref, wr_ref, br_ref, wi_ref, bi_ref, lam_ref,
                y_ref, hlast_ref, hist_out_ref, hist_ref, hc_ref, *, nseq, tc):
    c = pl.program_id(1)
    w = BRANCH_W

    @pl.when(c == 0)
    def _():
        hist_ref[...] = hist0_ref[...]
        hc_ref[...] = h0_ref[...]

    x = lx_ref[...]
    xfull = jnp.concatenate([hist_ref[...], x], axis=1)

    def tap(j):
        return cw_ref[j:j + 1, :].reshape(1, 1, w)

    y = cb_ref[...].reshape(1, 1, w) + x * tap(CONV_W - 1)
    for s in range(1, CONV_W):
        y = y + pltpu.roll(xfull, s, 1)[:, SUBLANE:, :] * tap(CONV_W - 1 - s)
    hist_ref[...] = xfull[:, tc:, :]
    hist_out_ref[...] = xfull[:, tc:, :]

    xc = y.reshape(nseq * tc, w)
    r = _sigmoid(_dot(xc, wr_ref[...]) + br_ref[...])
    i = _sigmoid(_dot(xc, wi_ref[...]) + bi_ref[...])
    log_a = (LRU_C * r) * _log_sigmoid(lam_ref[...])
    a = jnp.exp(log_a)
    u = jnp.sqrt(jnp.tanh(-log_a) * (a * a + 1.0)) * (i * xc)

    acc_a = a.reshape(nseq, tc, w)
    acc_u = u.reshape(nseq, tc, w)
    t = lax.broadcasted_iota(jnp.int32, (nseq, tc, w), 1)
    d = 1
    while d < tc:
        if d % SUBLANE:
            ok = t >= d
            a_sh = jnp.where(ok, pltpu.roll(acc_a, d, 1), 1.0)
            u_sh = jnp.where(ok, pltpu.roll(acc_u, d, 1), 0.0)
            acc_u = acc_a * u_sh + acc_u
            acc_a = acc_a * a_sh
        else:
            new_u = acc_a[:, d:, :] * acc_u[:, :tc - d, :] + acc_u[:, d:, :]
            new_a = acc_a[:, d:, :] * acc_a[:, :tc - d, :]
            acc_u = jnp.concatenate([acc_u[:, :d, :], new_u], axis=1)
            acc_a = jnp.concatenate([acc_a[:, :d, :], new_a], axis=1)
        d *= 2
    h = acc_a * hc_ref[...] + acc_u
    hc_ref[...] = h[:, tc - 1:tc, :]
    hlast_ref[...] = h[:, tc - SUBLANE:, :]
    y_ref[...] = (h * _silu(lz_ref[...])).astype(y_ref.dtype)


def _lru_call(proj, hist0, h0, lw, nb, nseq, nc, tc, name):
    w = BRANCH_W
    kern = functools.partial(_lru_kernel, nseq=nseq, tc=tc)
    return pl.pallas_call(
        kern,
        grid=(nb, nc),
        in_specs=[_seg_spec("lx", nseq, tc), _seg_spec("lz", nseq, tc),
                  pl.BlockSpec((nseq, SUBLANE, w), lambda b, c: (b, 0, 0)),
                  pl.BlockSpec((nseq, 1, w), lambda b, c: (b, 0, 0)),
                  _const_spec((CONV_W, w)), _const_spec((1, w)), _const_spec((w, w)), _const_spec((1, w)),
                  _const_spec((w, w)), _const_spec((1, w)), _const_spec((1, w))],
        out_specs=[_tok_spec(nseq, tc, w),
                   pl.BlockSpec((nseq, SUBLANE, w), lambda b, c: (b, 0, 0)),
                   pl.BlockSpec((nseq, SUBLANE, w), lambda b, c: (b, 0, 0))],
        out_shape=[jax.ShapeDtypeStruct((nb * nseq, nc * tc, w), bf16),
                   jax.ShapeDtypeStruct((nb * nseq, SUBLANE, w), f32),
                   jax.ShapeDtypeStruct((nb * nseq, SUBLANE, w), f32)],
        scratch_shapes=[pltpu.VMEM((nseq, SUBLANE, w), f32), pltpu.VMEM((nseq, 1, w), f32)],
        compiler_params=_params("parallel", "arbitrary"),
        name=name,
    )(proj, proj, hist0, h0, lw["conv_w"], lw["conv_b"], lw["wr"], lw["br"], lw["wi"], lw["bi"], lw["lam"])


def _gla_consts(c):
    t = np.arange(c)[:, None]
    u = np.arange(c)[None, :]
    blocks = [u <= t, u > t]
    masks = [t == u]
    m = 1
    while m < c:
        t0 = (t // m) * m
        odd = (t // m) % 2 == 1
        blocks.append(odd & (u >= t0) & (u <= t))
        blocks.append((~odd) & (u > t) & (u <= t0 + m - 1))
        masks.append((t // (2 * m) == u // (2 * m)) & odd & ((u // m) % 2 == 0))
        m *= 2
    return (np.concatenate(blocks, 0).astype(np.float32), np.stack(masks).astype(np.float32))


def _gla_kernel(gq_ref, gk_ref, gv_ref, gz_ref, glr_ref, s0_ref, wa_ref, ba_ref, ng_ref, d_ref, m_ref,
                y_ref, sout_ref, s_ref, att_ref, o_ref, upd_ref, *, nseq, c):
    ci = pl.program_id(1)
    nlev = int(math.log2(c))
    hk = GLA_HEADS * GLA_DK
    heads = range(GLA_HEADS)

    @pl.when(ci == 0)
    def _():
        s_ref[...] = s0_ref[...]

    def ks(h):
        return slice(h * GLA_DK, (h + 1) * GLA_DK)

    def vs(h):
        return slice(h * GLA_DV, (h + 1) * GLA_DV)

    z = _dot(glr_ref[...].reshape(nseq * c, LANE), wa_ref[...]) + ba_ref[...]
    la = _log_sigmoid(z) * (1.0 / GLA_TAU)
    hi = la.astype(bf16)
    r1 = la - hi.astype(f32)
    mid = r1.astype(bf16)
    lo = (r1 - mid.astype(f32)).astype(bf16)
    hml = jnp.concatenate([hi, mid, lo], axis=1)

    q_in, k_st, dec_rows, qf, kf = [], [], [], [], []
    for n in range(nseq):
        hml_n = hml[n * c:(n + 1) * c]

        p = jnp.dot(d_ref[0:c, :], hml_n, preferred_element_type=f32)
        b = p[:, :hk] + p[:, hk:2 * hk] + p[:, 2 * hk:]

        def decay(lev, query_side):
            m = 2 ** lev
            if m < SUBLANE:
                blk = 2 + 2 * lev + (0 if query_side else 1)
                pm = jnp.dot(d_ref[blk * c:(blk + 1) * c, :], hml_n[:, :2 * hk], preferred_element_type=f32)
                return jnp.exp(pm[:, :hk] + pm[:, hk:])
            pieces = []
            for j in range(c // m):
                rows = b[j * m:(j + 1) * m]
                if query_side and j % 2 == 1:
                    pieces.append(rows - b[j * m - 1:j * m])
                elif not query_side and j % 2 == 0:
                    pieces.append(b[(j + 1) * m - 1:(j + 1) * m] - rows)
                else:
                    pieces.append(jnp.zeros((m, hk), f32))
            return jnp.exp(jnp.concatenate(pieces, axis=0))

        q = gq_ref[n] * (GLA_DK ** -0.5)
        k = gk_ref[n]
        eb = jnp.exp(b)
        q_in.append((q * eb).astype(bf16))
        k_st.append((k * jnp.exp(b[c - 1:c] - b)).astype(bf16))
        dec_rows.append(eb[c - 1:c, :])
        qf.append([q.astype(bf16)] + [(q * decay(lev, True)).astype(bf16) for lev in range(nlev)])
        kf.append([k.astype(bf16)] + [(k * decay(lev, False)).astype(bf16) for lev in range(nlev)])

    for n in range(nseq):
        for lev in range(nlev + 1):
            for h in heads:
                att_ref[n, h, lev] = _dot_nt(qf[n][lev][:, ks(h)], kf[n][lev][:, ks(h)])

    att = m_ref[0][None, None] * att_ref[:, :, 0]
    for lev in range(1, nlev + 1):
        att = att + m_ref[lev][None, None] * att_ref[:, :, lev]
    att = att.astype(bf16)

    for n in range(nseq):
        v = gv_ref[n].astype(bf16)
        for h in heads:
            v_h = v[:, vs(h)]
            o_ref[n, h] = _dot(q_in[n][:, ks(h)], s_ref[n, h]) + _dot(att[n, h], v_h)
            upd_ref[n, h] = _dot_tn(k_st[n][:, ks(h)], v_h)

    eye = (lax.broadcasted_iota(jnp.int32, (GLA_DK, GLA_DK), 0)
           == lax.broadcasted_iota(jnp.int32, (GLA_DK, GLA_DK), 1))[None, None]
    dec = jnp.stack([jnp.stack([dec_rows[n][:, ks(h)] for h in heads]) for n in range(nseq)])
    dec_col = jnp.sum(jnp.where(eye, jnp.broadcast_to(dec, (nseq, GLA_HEADS, GLA_DK, GLA_DK)), 0.0),
                      axis=3, keepdims=True)
    s_new = s_ref[...] * dec_col + upd_ref[...]
    s_ref[...] = s_new
    sout_ref[...] = s_new
    o = o_ref[...]
    o = o * lax.rsqrt(jnp.mean(o * o, axis=-1, keepdims=True) + RMS_EPS) * ng_ref[...]
    for n in range(nseq):
        for h in heads:
            y_ref[n, :, vs(h)] = (o[n, h] * _silu(gz_ref[n, :, vs(h)])).astype(y_ref.dtype)


_GLA_SEGS = ("gq", "gk", "gv", "gz", "glr")


def _gla_call(proj, s0, gw, nb, nseq, nc, c, name):
    dstack, masks = _gla_consts(c)
    kern = functools.partial(_gla_kernel, nseq=nseq, c=c)
    hk = GLA_HEADS * GLA_DK
    st_spec = pl.BlockSpec((nseq, GLA_HEADS, GLA_DK, GLA_DV), lambda b, ci: (b, 0, 0, 0))
    return pl.pallas_call(
        kern,
        grid=(nb, nc),
        in_specs=[_seg_spec(s, nseq, c) for s in _GLA_SEGS] + [
            st_spec, _const_spec((LANE, hk)), _const_spec((1, hk)), _const_spec((1, GLA_DV)),
            _const_spec(dstack.shape), _const_spec(masks.shape)],
        out_specs=[_tok_spec(nseq, c, BRANCH_W), st_spec],
        out_shape=[jax.ShapeDtypeStruct((nb * nseq, nc * c, BRANCH_W), bf16),
                   jax.ShapeDtypeStruct((nb * nseq, GLA_HEADS, GLA_DK, GLA_DV), f32)],
        scratch_shapes=[pltpu.VMEM((nseq, GLA_HEADS, GLA_DK, GLA_DV), f32),
                        pltpu.VMEM((nseq, GLA_HEADS, masks.shape[0], c, c), f32),
                        pltpu.VMEM((nseq, GLA_HEADS, c, GLA_DV), f32),
                        pltpu.VMEM((nseq, GLA_HEADS, GLA_DK, GLA_DV), f32)],
        compiler_params=_params("parallel", "arbitrary"),
        name=name,
    )(*([proj] * len(_GLA_SEGS)), s0, gw["wa"], gw["ba"], gw["ng"], jnp.asarray(dstack, bf16), jnp.asarray(masks))


def _gla_step_consts(t, nseq):
    dstack, masks = _gla_consts(t)
    eye = np.eye(nseq, dtype=np.float32)
    dbd = np.concatenate([np.kron(eye, dstack[i * t:(i + 1) * t]) for i in range(dstack.shape[0] // t)], axis=0)
    mbd = np.stack([np.kron(eye, m) for m in masks])
    return dbd, mbd


def _gla_step_kernel(gq_ref, gk_ref, gv_ref, gz_ref, glr_ref, s0_ref, wa_ref, ba_ref, ng_ref, d_ref, m_ref,
                     *refs, nseq, t, nprev):
    prev_refs, (y_ref, sout_ref) = refs[:nprev], refs[nprev:]
    if nprev:
        for i, p_ref in enumerate(prev_refs):
            sout_ref[i] = p_ref[...]
        sout_ref = sout_ref.at[nprev]
    r = nseq * t
    nlev = int(math.log2(t))
    hk = GLA_HEADS * GLA_DK
    q = gq_ref[...].reshape(r, hk) * (GLA_DK ** -0.5)
    k = gk_ref[...].reshape(r, hk)
    v = gv_ref[...].reshape(r, BRANCH_W)
    gz = gz_ref[...].reshape(r, BRANCH_W)
    z = _dot(glr_ref[...].reshape(r, LANE), wa_ref[...]) + ba_ref[...]
    la = _log_sigmoid(z) * (1.0 / GLA_TAU)
    hi = la.astype(bf16)
    r1 = la - hi.astype(f32)
    mid = r1.astype(bf16)
    lo = (r1 - mid.astype(f32)).astype(bf16)
    hml = jnp.concatenate([hi, mid, lo], axis=1)

    def decay(blk):
        p = jnp.dot(d_ref[blk * r:(blk + 1) * r, :], hml, preferred_element_type=f32)
        return jnp.exp(p[:, :hk] + p[:, hk:2 * hk] + p[:, 2 * hk:])

    eb = decay(0)
    q_in = q * eb
    k_st = k * decay(1)
    dec3 = eb.reshape(nseq, t, hk)[:, t - 1:t, :]
    qf = [q]
    kf = [k]
    for lev in range(nlev):
        qf.append(q * decay(2 + 2 * lev))
        kf.append(k * decay(3 + 2 * lev))

    own = (lax.broadcasted_iota(jnp.int32, (r, nseq * GLA_DK), 0) // t
           == lax.broadcasted_iota(jnp.int32, (r, nseq * GLA_DK), 1) // GLA_DK)
    eye = (lax.broadcasted_iota(jnp.int32, (GLA_DK, GLA_DK), 0)
           == lax.broadcasted_iota(jnp.int32, (GLA_DK, GLA_DK), 1))[None]

    def spread(x):
        x2 = jnp.concatenate([x, x], axis=1)
        return jnp.where(own, jnp.concatenate([x2] * (nseq // 2), axis=1), 0.0)

    ys = []
    for h in range(GLA_HEADS):
        ks = slice(h * GLA_DK, (h + 1) * GLA_DK)
        vs = slice(h * GLA_DV, (h + 1) * GLA_DV)
        att = jnp.zeros((r, r), f32)
        for lev in range(nlev + 1):
            att = att + m_ref[lev] * _dot_nt(qf[lev][:, ks], kf[lev][:, ks])
        s_h = s0_ref[:, h]
        v_h = v[:, vs]
        o = _dot(spread(q_in[:, ks]), s_h.reshape(nseq * GLA_DK, GLA_DV)) + _dot(att, v_h)
        upd = _dot_tn(spread(k_st[:, ks]), v_h)
        dec_col = jnp.sum(jnp.where(eye, jnp.broadcast_to(dec3[:, :, ks], (nseq, GLA_DK, GLA_DK)), 0.0),
                          axis=2, keepdims=True)
        sout_ref[:, h] = s_h * dec_col + upd.reshape(nseq, GLA_DK, GLA_DV)
        o = o * lax.rsqrt(jnp.mean(o * o, axis=-1, keepdims=True) + RMS_EPS) * ng_ref[...]
        ys.append(o * _silu(gz[:, vs]))
    y_ref[...] = jnp.concatenate(ys, axis=1).reshape(nseq, t, BRANCH_W).astype(y_ref.dtype)


def _stacked_specs(prev, inner):
    nd = len(inner)
    one = pl.BlockSpec(inner, lambda b, c: (b,) + (0,) * (nd - 1))
    if not prev:
        return [], one, lambda nbatch: jax.ShapeDtypeStruct((nbatch,) + inner[1:], f32)
    n = len(prev) + 1
    stacked = pl.BlockSpec((n,) + inner, lambda b, c: (0, b) + (0,) * (nd - 1))
    return [one] * len(prev), stacked, lambda nbatch: jax.ShapeDtypeStruct((n, nbatch) + inner[1:], f32)


def _gla_step_call(proj, s0_all, layer, prev, gw, nb, nseq, t, name):
    dbd, mbd = _gla_step_consts(t, nseq)
    kern = functools.partial(_gla_step_kernel, nseq=nseq, t=t, nprev=len(prev))
    hk = GLA_HEADS * GLA_DK
    prev_specs, st_spec, st_shape = _stacked_specs(prev, (nseq, GLA_HEADS, GLA_DK, GLA_DV))
    s0_spec = pl.BlockSpec((None, nseq, GLA_HEADS, GLA_DK, GLA_DV), lambda b, ci: (layer, b, 0, 0, 0))
    return pl.pallas_call(
        kern,
        grid=(nb, 1),
        in_specs=[_seg_spec(s, nseq, t) for s in _GLA_SEGS] + [
            s0_spec, _const_spec((LANE, hk)), _const_spec((1, hk)), _const_spec((1, GLA_DV)),
            _const_spec(dbd.shape), _const_spec(mbd.shape)] + prev_specs,
        out_specs=[_tok_spec(nseq, t, BRANCH_W), st_spec],
        out_shape=[jax.ShapeDtypeStruct((nb * nseq, t, BRANCH_W), bf16), st_shape(nb * nseq)],
        compiler_params=_params("parallel", "arbitrary"),
        name=name,
    )(*([proj] * len(_GLA_SEGS)), s0_all, gw["wa"], gw["ba"], gw["ng"], jnp.asarray(dbd, bf16), jnp.asarray(mbd),
      *prev)


def _rope_tables(pos0, t):
    half = ROT_DIM // 2
    dim = jnp.arange(LANE) % HEAD_DIM
    inv = ROPE_THETA ** (-(dim % half).astype(f32) / half)
    ang = (pos0 + jnp.arange(t)).astype(f32)[:, None] * inv[None, :]
    cos, sin = jnp.cos(ang), jnp.sin(ang)
    first, second = (dim < half)[None, :], ((dim >= half) & (dim < ROT_DIM))[None, :]
    c_tab = jnp.where(first | second, cos, 1.0)
    sa_tab = jnp.where(first, -sin, 0.0)
    sb_tab = jnp.where(second, sin, 0.0)
    return c_tab, sa_tab, sb_tab


def _rope(x, c_tab, sa_tab, sb_tab):
    wd = x.shape[-1]
    ax = x.ndim - 1
    rep = wd // LANE
    half = ROT_DIM // 2
    if rep > 1:
        c_tab, sa_tab, sb_tab = (jnp.concatenate([tb] * rep, axis=-1) for tb in (c_tab, sa_tab, sb_tab))
    return x * c_tab + pltpu.roll(x, wd - half, ax) * sa_tab + pltpu.roll(x, half, ax) * sb_tab


def _swa_kernel(sink_ref, sq_ref, sz_ref, sk_ref, sv_ref, ct_ref, sat_ref, sbt_ref, kp_ref, vp_ref,
                y_ref, klast_ref, vlast_ref, kprev_ref, vprev_ref, sp_ref, sc_ref, o_ref, den_ref,
                *, nseq, qb, pos0):
    blk = pl.program_id(1)
    hd = HEAD_DIM

    @pl.when(blk == 0)
    def _():
        kprev_ref[...] = kp_ref[...]
        vprev_ref[...] = vp_ref[...]

    tabs = tuple(r[...][None] for r in (ct_ref, sat_ref, sbt_ref))
    q3 = _rope(sq_ref[...], *tabs) * (hd ** -0.5)
    k3 = _rope(sk_ref[...], *tabs)
    v3 = sv_ref[...]
    klast_ref[...] = k3
    vlast_ref[...] = v3

    for n in range(nseq):
        for kv in range(SWA_KV_HEADS):
            ds = slice(kv * hd, (kv + 1) * hd)
            qs = jnp.concatenate(
                [q3[n][:, (kv * SWA_GROUP + g) * hd:(kv * SWA_GROUP + g + 1) * hd] for g in range(SWA_GROUP)],
                axis=0).astype(bf16)
            sp_ref[n, kv] = _dot_nt(qs, kprev_ref[n][:, ds])
            sc_ref[n, kv] = _dot_nt(qs, k3[n][:, ds])

    mrows = SWA_GROUP * qb
    qi = lax.broadcasted_iota(jnp.int32, (mrows, WINDOW), 0) % qb
    kj = lax.broadcasted_iota(jnp.int32, (mrows, WINDOW), 1)
    past_ok = (kj >= qi) & (kj >= (WINDOW - pos0) - blk * qb)
    cur_ok = kj <= qi
    sink = jnp.stack([jnp.concatenate([jnp.full((qb, 1), sink_ref[kv * SWA_GROUP + g], f32)
                                       for g in range(SWA_GROUP)], axis=0) for kv in range(SWA_KV_HEADS)])[None]
    s_p = jnp.where(past_ok[None, None], sp_ref[...], -jnp.inf)
    s_c = jnp.where(cur_ok[None, None], sc_ref[...], -jnp.inf)
    m = jnp.maximum(jnp.max(jnp.maximum(s_p, s_c), axis=3, keepdims=True), sink)
    p_p = jnp.exp(s_p - m).astype(bf16)
    p_c = jnp.exp(s_c - m).astype(bf16)
    ones = jnp.ones((WINDOW, LANE), bf16)
    for n in range(nseq):
        vp = vprev_ref[n].astype(bf16)
        vc = v3[n].astype(bf16)
        for kv in range(SWA_KV_HEADS):
            o_ref[n, kv] = _dot(p_p[n, kv], vp) + _dot(p_c[n, kv], vc)
            den_ref[n, kv] = _dot(p_p[n, kv], ones) + _dot(p_c[n, kv], ones)
    o = o_ref[...] / (den_ref[...] + jnp.exp(sink - m))
    outs = [o[:, j // SWA_GROUP, (j % SWA_GROUP) * qb:(j % SWA_GROUP + 1) * qb,
              (j // SWA_GROUP) * hd:(j // SWA_GROUP + 1) * hd] for j in range(SWA_HEADS)]
    y_ref[...] = (jnp.concatenate(outs, axis=2) * _silu(sz_ref[...])).astype(y_ref.dtype)
    kprev_ref[...] = k3
    vprev_ref[...] = v3


_SWA_SEGS = ("sq", "sz", "sk", "sv")


def _swa_call(proj, sinks, k_past, v_past, pos0, nb, nseq, nc, qb, name):
    assert qb == WINDOW
    t_total = nc * qb
    c_tab, sa_tab, sb_tab = _rope_tables(pos0, t_total)
    kern = functools.partial(_swa_kernel, nseq=nseq, qb=qb, pos0=pos0)
    kvw = SWA_KV_HEADS * HEAD_DIM
    mrows = SWA_GROUP * qb
    tab_spec = pl.BlockSpec((qb, LANE), lambda b, c: (c, 0))
    past_spec = pl.BlockSpec((nseq, WINDOW, kvw), lambda b, c: (b, 0, 0))
    return pl.pallas_call(
        kern,
        grid=(nb, nc),
        in_specs=[pl.BlockSpec(memory_space=pltpu.SMEM)] + [_seg_spec(s, nseq, qb) for s in _SWA_SEGS] + [
            tab_spec, tab_spec, tab_spec, past_spec, past_spec],
        out_specs=[_tok_spec(nseq, qb, BRANCH_W), past_spec, past_spec],
        out_shape=[jax.ShapeDtypeStruct((nb * nseq, nc * qb, BRANCH_W), bf16),
                   jax.ShapeDtypeStruct((nb * nseq, WINDOW, kvw), f32),
                   jax.ShapeDtypeStruct((nb * nseq, WINDOW, kvw), f32)],
        scratch_shapes=[pltpu.VMEM((nseq, WINDOW, kvw), f32), pltpu.VMEM((nseq, WINDOW, kvw), f32),
                        pltpu.VMEM((nseq, SWA_KV_HEADS, mrows, WINDOW), f32),
                        pltpu.VMEM((nseq, SWA_KV_HEADS, mrows, qb), f32),
                        pltpu.VMEM((nseq, SWA_KV_HEADS, mrows, kvw), f32),
                        pltpu.VMEM((nseq, SWA_KV_HEADS, mrows, kvw), f32)],
        compiler_params=_params("parallel", "arbitrary"),
        name=name,
    )(sinks, *([proj] * len(_SWA_SEGS)), c_tab, sa_tab, sb_tab, k_past, v_past)


def _swa_step_kernel(sink_ref, sq_ref, sz_ref, sk_ref, sv_ref, ct_ref, sat_ref, sbt_ref, kp_ref, vp_ref,
                     *refs, nseq, t, pos0, nprev):
    prev_refs = refs[:2 * nprev]
    y_ref, klast_ref, vlast_ref, sp_ref, sc_ref, o_ref = refs[2 * nprev:]
    if nprev:
        for i in range(nprev):
            klast_ref[i] = prev_refs[i][...]
            vlast_ref[i] = prev_refs[nprev + i][...]
        klast_ref = klast_ref.at[nprev]
        vlast_ref = vlast_ref.at[nprev]
    hd = HEAD_DIM
    kvw = SWA_KV_HEADS * hd
    mrows = SWA_HEADS * t
    tabs = tuple(r[...][None] for r in (ct_ref, sat_ref, sbt_ref))
    q3 = _rope(sq_ref[...], *tabs) * (hd ** -0.5)
    k3 = _rope(sk_ref[...], *tabs)
    v3 = sv_ref[...]

    lane = lax.broadcasted_iota(jnp.int32, (kvw, WINDOW), 1)
    pad = jnp.zeros((WINDOW - t, kvw), f32)

    def shifted(old_t, new):
        new_t = jnp.concatenate([pad, new], axis=0).T
        out = jnp.where(lane >= WINDOW - t, new_t, pltpu.roll(old_t, WINDOW - t, 1))
        return out.reshape(SWA_KV_HEADS, hd, WINDOW)

    for n in range(nseq):
        klast_ref[n] = shifted(kp_ref[n].reshape(kvw, WINDOW), k3[n])
        vlast_ref[n] = shifted(vp_ref[n].reshape(kvw, WINDOW), v3[n])

    zero = jnp.zeros((nseq, t, hd), f32)
    pieces = []
    for j in range(SWA_HEADS):
        qj = q3[:, :, j * hd:(j + 1) * hd]
        pieces.append(jnp.concatenate([qj, zero] if j // SWA_GROUP == 0 else [zero, qj], axis=2))
    qbd = jnp.concatenate(pieces, axis=1).astype(bf16)

    for n in range(nseq):
        sp_ref[n] = _dot(qbd[n], kp_ref[n].reshape(kvw, WINDOW))
        sc_ref[n] = _dot_nt(qbd[n], k3[n])

    qi = lax.broadcasted_iota(jnp.int32, (mrows, WINDOW), 0) % t
    kj = lax.broadcasted_iota(jnp.int32, (mrows, WINDOW), 1)
    past_ok = kj >= qi
    if pos0 < WINDOW:
        past_ok = past_ok & (kj >= WINDOW - pos0)
    qi_c = lax.broadcasted_iota(jnp.int32, (mrows, t), 0) % t
    kj_c = lax.broadcasted_iota(jnp.int32, (mrows, t), 1)
    cur_ok = kj_c <= qi_c
    sink = jnp.concatenate([jnp.full((t, 1), sink_ref[j], f32) for j in range(SWA_HEADS)], axis=0)[None]
    s_p = jnp.where(past_ok[None], sp_ref[...], -jnp.inf)
    s_c = jnp.where(cur_ok[None], sc_ref[...], -jnp.inf)
    m = jnp.maximum(jnp.maximum(jnp.max(s_p, axis=2, keepdims=True), jnp.max(s_c, axis=2, keepdims=True)), sink)
    p_p = jnp.exp(s_p - m)
    p_c = jnp.exp(s_c - m)
    den = jnp.sum(p_p, axis=2, keepdims=True) + jnp.sum(p_c, axis=2, keepdims=True) + jnp.exp(sink - m)
    p_p = p_p.astype(bf16)
    p_c = p_c.astype(bf16)
    for n in range(nseq):
        o_ref[n] = _dot_nt(p_p[n], vp_ref[n].reshape(kvw, WINDOW)) + _dot(p_c[n], v3[n])
    o = o_ref[...] / den
    outs = []
    for j in range(SWA_HEADS):
        kv = j // SWA_GROUP
        outs.append(o[:, j * t:(j + 1) * t, kv * hd:(kv + 1) * hd])
    y_ref[...] = (jnp.concatenate(outs, axis=2) * _silu(sz_ref[...])).astype(y_ref.dtype)


def _swa_step_call(proj, sinks, k_past_t, v_past_t, layer, prev_k, prev_v, pos0, nb, nseq, t, name):
    c_tab, sa_tab, sb_tab = _rope_tables(pos0, t)
    kern = functools.partial(_swa_step_kernel, nseq=nseq, t=t, pos0=pos0, nprev=len(prev_k))
    kvw = SWA_KV_HEADS * HEAD_DIM
    mrows = SWA_HEADS * t
    tab_spec = pl.BlockSpec((t, LANE), lambda b, c: (0, 0))
    past_spec = pl.BlockSpec((None, nseq, SWA_KV_HEADS, HEAD_DIM, WINDOW), lambda b, c: (layer, b, 0, 0, 0))
    prev_specs, new_spec, new_shape = _stacked_specs(prev_k, (nseq, SWA_KV_HEADS, HEAD_DIM, WINDOW))
    return pl.pallas_call(
        kern,
        grid=(nb, 1),
        in_specs=[pl.BlockSpec(memory_space=pltpu.SMEM)] + [_seg_spec(s, nseq, t) for s in _SWA_SEGS] + [
            tab_spec, tab_spec, tab_spec, past_spec, past_spec] + prev_specs + prev_specs,
        out_specs=[_tok_spec(nseq, t, BRANCH_W), new_spec, new_spec],
        out_shape=[jax.ShapeDtypeStruct((nb * nseq, t, BRANCH_W), bf16), new_shape(nb * nseq),
                   new_shape(nb * nseq)],
        scratch_shapes=[pltpu.VMEM((nseq, mrows, WINDOW), f32), pltpu.VMEM((nseq, mrows, t), f32),
                        pltpu.VMEM((nseq, mrows, kvw), f32)],
        compiler_params=_params("parallel", "arbitrary"),
        name=name,
    )(sinks, *([proj] * len(_SWA_SEGS)), c_tab, sa_tab, sb_tab, k_past_t, v_past_t, *prev_k, *prev_v)


def _sgu_kernel(x_ref, wu_ref, wv_ref, wz_ref, g_ref, b_ref, wm_ref, bias_ref, y_ref, *rest, ntile, want_vn):
    vn_ref = rest[0] if want_vn else None
    su_ref, sv_ref, sz_ref = rest[-3:]
    _project(x_ref, ((wu_ref, (su_ref,)), (wv_ref, (sv_ref,)), (wz_ref, (sz_ref,))))
    vn = _layer_norm(sv_ref[...], g_ref[...], b_ref[...])
    if want_vn:
        vn_ref[...] = vn
    for r in range(ntile):
        rows = slice(r * SGU_CHUNK, (r + 1) * SGU_CHUNK)
        mixed = jnp.concatenate(
            [jnp.dot(wm_ref[g], vn[rows, g * SGU_GC:(g + 1) * SGU_GC].astype(bf16), preferred_element_type=f32)
             for g in range(SGU_GROUPS)], axis=1)
        y = su_ref[rows, :] * (mixed + bias_ref[...]) * _silu(sz_ref[rows, :])
        y_ref[rows, :] = y.astype(y_ref.dtype)


def _sgu_call(xb, w_t, layer, ln_g, ln_b, wmix, bias, n_tok, ntile, want_vn, name):
    rows = ntile * SGU_CHUNK
    w = BRANCH_W
    kern = functools.partial(_sgu_kernel, ntile=ntile, want_vn=want_vn)

    def const(shape):
        return pl.BlockSpec(shape, lambda i: (0,) * len(shape))

    out_specs = [pl.BlockSpec((rows, w), lambda i: (i, 0))]
    out_shape = [jax.ShapeDtypeStruct((n_tok, w), bf16)]
    if want_vn:
        out_specs.append(pl.BlockSpec((rows, w), lambda i: (i, 0)))
        out_shape.append(jax.ShapeDtypeStruct((n_tok, w), f32))
    return pl.pallas_call(
        kern,
        grid=(n_tok // rows,),
        in_specs=[pl.BlockSpec((rows, D_MODEL), lambda i: (i, 0)), _w_spec("su", "su", layer),
                  _w_spec("svv", "svv", layer), _w_spec("suz", "suz", layer), const((1, w)), const((1, w)),
                  const((SGU_GROUPS, SGU_CHUNK, SGU_CHUNK)), const((SGU_CHUNK, w))],
        out_specs=out_specs,
        out_shape=out_shape,
        scratch_shapes=[_seg_scratch(s, rows) for s in ("su", "svv", "suz")],
        compiler_params=_params("parallel"),
        name=name,
    )(xb, w_t, w_t, w_t, ln_g, ln_b, wmix, bias)


def _mem_kernel(x_ref, wm_ref, mk_ref, mv_ref, y_ref, mq_ref, mz_ref, s_ref, o_ref, den_ref, *, nseq, tq):
    _project(x_ref, ((wm_ref, (mq_ref, mz_ref)),))
    heads = range(MEM_HEADS)
    lane_head = lax.broadcasted_iota(jnp.int32, (tq, MEM_W), 1) // HEAD_DIM
    for n in range(nseq):
        q = mq_ref[n] * (HEAD_DIM ** -0.5)
        mk = mk_ref[n].astype(bf16)
        for h in heads:
            s_ref[n, h] = _dot_nt(jnp.where(lane_head == h, q, 0.0), mk)
    s = s_ref[...]
    p = jnp.exp(s - jnp.max(s, axis=3, keepdims=True)).astype(bf16)
    ones = jnp.ones((N_MEM, MEM_W), bf16)
    for n in range(nseq):
        mv = mv_ref[n].astype(bf16)
        for h in heads:
            o_ref[n, h] = _dot(p[n, h], mv)
            den_ref[n, h] = _dot(p[n, h], ones)
    o = o_ref[...] / den_ref[...]
    acc = jnp.where(lane_head == 0, o[:, 0], 0.0)
    for h in range(1, MEM_HEADS):
        acc = acc + jnp.where(lane_head == h, o[:, h], 0.0)
    y_ref[...] = (acc * _silu(mz_ref[...])).astype(y_ref.dtype)


def _mem_call(xb, w_t, layer, mk, mv, nb, nseq, nc, tq, name):
    kern = functools.partial(_mem_kernel, nseq=nseq, tq=tq)
    kv_spec = pl.BlockSpec((nseq, N_MEM, MEM_W), lambda b, c: (b, 0, 0))
    return pl.pallas_call(
        kern,
        grid=(nb, nc),
        in_specs=[_tok_spec(nseq, tq, D_MODEL), _w_spec("mq", "mz", layer), kv_spec, kv_spec],
        out_specs=_tok_spec(nseq, tq, MEM_W),
        out_shape=jax.ShapeDtypeStruct((nb * nseq, nc * tq, MEM_W), bf16),
        scratch_shapes=[_seg_scratch("mq", nseq, tq), _seg_scratch("mz", nseq, tq),
                        pltpu.VMEM((nseq, MEM_HEADS, tq, N_MEM), f32), pltpu.VMEM((nseq, MEM_HEADS, tq, MEM_W), f32),
                        pltpu.VMEM((nseq, MEM_HEADS, tq, MEM_W), f32)],
        compiler_params=_params("parallel", "arbitrary"),
        name=name,
    )(xb, w_t, mk, mv)


def _mem_step_kernel(x_ref, wm_ref, mk_ref, mv_ref, y_ref, s_ref, o_ref, mq_ref, mz_ref, *, nseq, t):
    mrows = MEM_HEADS * t
    _project(x_ref, ((wm_ref, (mq_ref, mz_ref)),))
    row_head = lax.broadcasted_iota(jnp.int32, (mrows, MEM_W), 0) // t
    lane_head = lax.broadcasted_iota(jnp.int32, (mrows, MEM_W), 1) // HEAD_DIM
    own = (row_head == lane_head)[None]
    q3 = mq_ref[...] * (HEAD_DIM ** -0.5)
    qbd = jnp.where(own, jnp.concatenate([q3] * MEM_HEADS, axis=1), 0.0).astype(bf16)
    for n in range(nseq):
        s_ref[n] = _dot(qbd[n], mk_ref[n].reshape(MEM_W, N_MEM))
    s = s_ref[...]
    p = jnp.exp(s - jnp.max(s, axis=2, keepdims=True))
    den = jnp.sum(p, axis=2, keepdims=True)
    p = p.astype(bf16)
    for n in range(nseq):
        o_ref[n] = _dot_nt(p[n], mv_ref[n].reshape(MEM_W, N_MEM))
    o = jnp.where(own, o_ref[...] / den, 0.0)
    acc = o[:, 0:t, :]
    for h in range(1, MEM_HEADS):
        acc = acc + o[:, h * t:(h + 1) * t, :]
    y_ref[...] = (acc * _silu(mz_ref[...])).astype(y_ref.dtype)


def _mem_step_call(xb, w_t, mk_t, mv_t, layer, nb, nseq, t, name):
    kern = functools.partial(_mem_step_kernel, nseq=nseq, t=t)
    kv_spec = pl.BlockSpec((None, nseq, MEM_HEADS, HEAD_DIM, N_MEM), lambda b, c: (layer, b, 0, 0, 0))
    mrows = MEM_HEADS * t
    return pl.pallas_call(
        kern,
        grid=(nb, 1),
        in_specs=[_tok_spec(nseq, t, D_MODEL), _w_spec("mq", "mz", layer), kv_spec, kv_spec],
        out_specs=_tok_spec(nseq, t, MEM_W),
        out_shape=jax.ShapeDtypeStruct((nb * nseq, t, MEM_W), bf16),
        scratch_shapes=[pltpu.VMEM((nseq, mrows, N_MEM), f32), pltpu.VMEM((nseq, mrows, MEM_W), f32),
                        _seg_scratch("mq", nseq, t), _seg_scratch("mz", nseq, t)],
        compiler_params=_params("parallel", "arbitrary"),
        name=name,
    )(xb, w_t, mk_t, mv_t)


def _merge_kernel(yg_ref, yl_ref, ys_ref, yu_ref, ym_ref, x_ref, ig_ref, ib_ref, wg_ref, wb_ref, wm_ref, wo_ref,
                  g_ref, b_ref, o_ref, ob_ref, *, pre_ln):
    d = D_MODEL
    x = x_ref[...]
    if pre_ln:
        x = _layer_norm(x, ig_ref[...], ib_ref[...])
    xb = x.astype(bf16)

    def gate(n):
        return _sigmoid(_dot_nt(xb, wg_ref[0, n * d:(n + 1) * d, :]))

    merged = gate(4) * jnp.dot(ym_ref[...], wm_ref[...], preferred_element_type=f32)
    for n, y_ref in enumerate((yg_ref, yl_ref, ys_ref, yu_ref)):
        merged = merged + gate(n) * jnp.dot(y_ref[...], wb_ref[n], preferred_element_type=f32)
    out = _dot(merged, wo_ref[...])
    y = _layer_norm(DN_ALPHA * x + out, g_ref[...], b_ref[...])
    o_ref[...] = y
    ob_ref[...] = y.astype(bf16)


def _merge_call(ys, x, ln_in, mw, layer, tm, name):
    n_tok, d = x.shape
    w = BRANCH_W

    def rows(width):
        return pl.BlockSpec((tm, width), lambda i: (i, 0))

    def const(shape):
        return pl.BlockSpec(shape, lambda i: (0,) * len(shape), pipeline_mode=pl.Buffered(1))

    gates_spec = pl.BlockSpec((pl.Element(1), pl.Element(5 * d), pl.Element(d)),
                              lambda i: (layer, _ORIG_OFF["gates"][0], 0), pipeline_mode=pl.Buffered(1))
    return pl.pallas_call(
        functools.partial(_merge_kernel, pre_ln=layer == 0),
        grid=(n_tok // tm,),
        in_specs=[rows(w), rows(w), rows(w), rows(w), rows(MEM_W), rows(d), const((1, d)), const((1, d)),
                  gates_spec, const((4, w, d)), const((MEM_W, d)), const((d, d)), const((1, d)),
                  const((1, d))],
        out_specs=[rows(d), rows(d)],
        out_shape=[jax.ShapeDtypeStruct((n_tok, d), f32), jax.ShapeDtypeStruct((n_tok, d), bf16)],
        compiler_params=_params("parallel"),
        name=name,
    )(*ys, x, ln_in[0].reshape(1, d), ln_in[1].reshape(1, d), mw["wg"], mw["wb"], mw["wm"], mw["wo"], mw["g"],
      mw["b"])


def _prep_w_in(w_in):
    w_t = jnp.swapaxes(w_in, 1, 2).astype(bf16)
    runs = []
    used = 0
    for run in _PROJ_RUNS:
        if run is None:
            runs.append(jnp.zeros((DEPTH, N_PROJ - used, D_MODEL), bf16))
            break
        lo = _ORIG_OFF[run[0]][0]
        hi = _ORIG_OFF[run[1]][0] + _ORIG_OFF[run[1]][1]
        runs.append(w_t[:, lo:hi])
        used += hi - lo
    return jnp.concatenate(runs, axis=1), w_t


def _prep_layer(l, w_proj, w_gates, gla_wa2, gla_ba, gla_norm_g, lru_conv_w, lru_conv_b, lru_wr, lru_br, lru_wi,
                lru_bi, lru_L, swa_sinks, sgu_ln_g, sgu_ln_b, sgu_w, sgu_b, w_mem_kv, w_branch, w_branch_mem, w_out,
                ln_g, ln_b):
    d = D_MODEL
    w = BRANCH_W

    def block_diag(wb):
        eye = jnp.eye(LRU_BLOCKS, dtype=f32)
        return (eye[:, None, :, None] * wb[:, :, None, :]).reshape(w, w).astype(bf16)

    tril = jnp.tril(jnp.ones((SGU_CHUNK, SGU_CHUNK), f32))
    wmix_p = (sgu_w[l] * tril).astype(bf16)
    bias_p = jnp.repeat(sgu_b[l].T, SGU_GC, axis=1)
    t8 = SUBLANE
    rep = SGU_CHUNK // t8
    w8 = (sgu_w[l] * tril)[:, :t8, :t8]
    seq_eye = jnp.eye(rep, dtype=f32)
    wmix_s = (seq_eye[None, :, None, :, None] * w8[:, None, :, None, :]).reshape(
        SGU_GROUPS, SGU_CHUNK, SGU_CHUNK).astype(bf16)
    bias_s = jnp.tile(bias_p[:t8], (rep, 1))
    return dict(
        w_proj=w_proj,
        w_mem_kv=w_mem_kv[l].astype(bf16),
        gla=dict(wa=jnp.pad(gla_wa2[l], ((0, LANE - GLA_RANK), (0, 0))).astype(bf16),
                 ba=gla_ba[l].reshape(1, -1), ng=gla_norm_g[l].reshape(1, -1)),
        lru=dict(conv_w=lru_conv_w[l], conv_b=lru_conv_b[l].reshape(1, w), wr=block_diag(lru_wr[l]),
                 br=lru_br[l].reshape(1, w), wi=block_diag(lru_wi[l]), bi=lru_bi[l].reshape(1, w),
                 lam=lru_L[l].reshape(1, w)),
        sinks=swa_sinks[l],
        sgu=dict(g=sgu_ln_g[l].reshape(1, w), b=sgu_ln_b[l].reshape(1, w), wmix_p=wmix_p, bias_p=bias_p,
                 wmix_s=wmix_s, bias_s=bias_s),
        merge=dict(wg=w_gates, wb=w_branch[l].astype(bf16), wm=w_branch_mem[l].astype(bf16), wo=w_out[l].astype(bf16),
                   g=ln_g[l].reshape(1, d), b=ln_b[l].reshape(1, d)),
    )


def _layer(x, lw, grp, st, layer, tag, prev=()):
    nseq_total, t = grp["batch"], grp["seq"]
    n_tok = nseq_total * t
    x, xb = x
    xb3 = xb.reshape(nseq_total, t, D_MODEL)
    w_t = lw["w_proj"]
    proj = _matmul_call(xb, w_t, min(grp["proj_tm"], n_tok), 1024, "proj_" + tag, w_transposed=True, n_out=N_PROJ,
                        layer=layer)
    proj3 = proj.reshape(nseq_total, t, N_PROJ)

    lt = grp["lru"]
    y_lru, hlast, hist = _lru_call(proj3, st["hist0"], st["h0"], lw["lru"], nseq_total // lt[0], lt[0], t // lt[1],
                                   lt[1], "lru_" + tag)
    short = grp["kind"] == "s"
    gt = grp["gla"]
    if short:
        y_gla, s_out = _gla_step_call(proj3, st["gla0"], layer, [p["gla"] for p in prev], lw["gla"],
                                      nseq_total // gt[0], gt[0], t, "gla_" + tag)
    else:
        y_gla, s_out = _gla_call(proj3, st["gla0"], lw["gla"], nseq_total // gt[0], gt[0], t // gt[1], gt[1],
                                 "gla_" + tag)
    wt = grp["swa"]
    if short:
        y_swa, k_last, v_last = _swa_step_call(proj3, lw["sinks"], st["k_past"], st["v_past"], layer,
                                               [p["k_last"] for p in prev], [p["v_last"] for p in prev],
                                               grp["pos0"], nseq_total // wt[0], wt[0], t, "swa_" + tag)
    else:
        y_swa, k_last, v_last = _swa_call(proj3, lw["sinks"], st["k_past"], st["v_past"], grp["pos0"],
                                          nseq_total // wt[0], wt[0], t // wt[1], wt[1], "swa_" + tag)
    sg = lw["sgu"]
    w_all = lw["merge"]["wg"]
    sgu_out = _sgu_call(xb, w_all, layer, sg["g"], sg["b"], sg["wmix_" + grp["kind"]], sg["bias_" + grp["kind"]],
                        n_tok, grp["sgu_tiles"], grp["kind"] == "s", "sgu_" + tag)
    mt = grp["mem"]
    if short:
        y_mem = _mem_step_call(xb3, w_all, st["mk"], st["mv"], layer, nseq_total // mt[0], mt[0], t, "mem_" + tag)
    else:
        y_mem = _mem_call(xb3, w_all, layer, st["mk"], st["mv"], nseq_total // mt[0], mt[0], t // mt[1], mt[1],
                          "mem_" + tag)
    ys = tuple(y.reshape(n_tok, y.shape[-1]) for y in (y_gla, y_lru, y_swa, sgu_out[0], y_mem))
    x_new = _merge_call(ys, x, lw["ln_in"], lw["merge"], layer, min(256, n_tok), "merge_" + tag)
    return x_new, dict(gla=s_out, hlast=hlast, hist=hist, k_last=k_last, v_last=v_last,
                       vn=sgu_out[1] if len(sgu_out) > 1 else None)


_PROMPT = dict(kind="p", pos0=0, proj_tm=2048, lru=(4, 64), gla=(4, 128), swa=(4, 128), sgu_tiles=4, mem=(1, 512))
_SAMPLE = dict(kind="s", pos0=PAST_LEN, proj_tm=1024, lru=(32, 8), gla=(16, 8), swa=(16, 8), sgu_tiles=8,
               mem=(16, 8))


def kernel(x_prompt, x_sample, mem_prompt, state_gla, state_lru_h, state_lru_conv, cache_swa_k, cache_swa_v,
           cache_mem_k, cache_mem_v, ln_in_g, ln_in_b, w_in, gla_wa2, gla_ba, gla_norm_g, lru_conv_w, lru_conv_b,
           lru_wr, lru_br, lru_wi, lru_bi, lru_L, swa_sinks, sgu_ln_g, sgu_ln_b, sgu_w, sgu_b, w_mem_kv, w_branch,
           w_branch_mem, w_out, ln_g, ln_b):
    bp, tp, d = x_prompt.shape
    bs, ts, _ = x_sample.shape
    w = BRANCH_W
    kvw = SWA_KV_HEADS * HEAD_DIM
    gp = dict(_PROMPT, batch=bp, seq=tp)
    gs = dict(_SAMPLE, batch=bs, seq=ts)

    xp = (x_prompt.reshape(bp * tp, d), _ln_call(x_prompt.reshape(bp * tp, d), ln_in_g, ln_in_b))
    xs = (x_sample.reshape(bs * ts, d), _ln_call(x_sample.reshape(bs * ts, d), ln_in_g, ln_in_b))
    mem2 = mem_prompt.reshape(bp * N_MEM, d)

    swa_k_t, swa_v_t, mem_k_t, mem_v_t = (jnp.transpose(c, (0, 1, 3, 4, 2))
                                          for c in (cache_swa_k, cache_swa_v, cache_mem_k, cache_mem_v))

    w_proj, w_gates = _prep_w_in(w_in)
    outs_p, outs_s, mks, mvs = [], [], [], []
    for l in range(DEPTH):
        lw = _prep_layer(l, w_proj, w_gates, gla_wa2, gla_ba, gla_norm_g, lru_conv_w, lru_conv_b, lru_wr, lru_br,
                         lru_wi, lru_bi, lru_L, swa_sinks, sgu_ln_g, sgu_ln_b, sgu_w, sgu_b, w_mem_kv, w_branch,
                         w_branch_mem, w_out, ln_g, ln_b)
        lw["ln_in"] = (ln_in_g, ln_in_b)
        mkv = _matmul_call(mem2, lw["w_mem_kv"], bp * N_MEM, 2 * MEM_W, "memkv_%d" % l)
        mk = mkv[:, :MEM_W].reshape(bp, N_MEM, MEM_W)
        mv = mkv[:, MEM_W:].reshape(bp, N_MEM, MEM_W)
        st_p = dict(hist0=jnp.zeros((bp, SUBLANE, w), f32), h0=jnp.zeros((bp, 1, w), f32),
                    gla0=jnp.zeros((bp, GLA_HEADS, GLA_DK, GLA_DV), f32),
                    k_past=jnp.zeros((bp, WINDOW, kvw), f32), v_past=jnp.zeros((bp, WINDOW, kvw), f32),
                    mk=mk, mv=mv)
        st_s = dict(hist0=jnp.pad(state_lru_conv[l], ((0, 0), (SUBLANE - (CONV_W - 1), 0), (0, 0))),
                    h0=state_lru_h[l][:, None, :], gla0=state_gla,
                    k_past=swa_k_t, v_past=swa_v_t, mk=mem_k_t, mv=mem_v_t)
        xp, op = _layer(xp, lw, gp, st_p, l, "p%d" % l)
        xs, os_ = _layer(xs, lw, gs, st_s, l, "s%d" % l, prev=outs_s if l == DEPTH - 1 else ())
        outs_p.append(op)
        outs_s.append(os_)
        mks.append(mk.reshape(bp, N_MEM, MEM_HEADS, HEAD_DIM))
        mvs.append(mv.reshape(bp, N_MEM, MEM_HEADS, HEAD_DIM))

    def stack(outs, fn):
        return jnp.stack([fn(o) for o in outs])

    def window(a):
        return a.reshape(a.shape[0], WINDOW, SWA_KV_HEADS, HEAD_DIM)

    last_s = outs_s[-1]

    def window_t(a):
        return jnp.transpose(a, (0, 1, 4, 2, 3))

    return (
        xp[0].reshape(bp, tp, d), xs[0].reshape(bs, ts, d),
        stack(outs_p, lambda o: o["gla"]), last_s["gla"],
        stack(outs_p, lambda o: o["hlast"][:, SUBLANE - 1]), stack(outs_s, lambda o: o["hlast"][:, SUBLANE - 1]),
        stack(outs_p, lambda o: o["hist"][:, SUBLANE - (CONV_W - 1):]),
        stack(outs_s, lambda o: o["hist"][:, SUBLANE - (CONV_W - 1):]),
        stack(outs_p, lambda o: window(o["k_last"])), window_t(last_s["k_last"]),
        stack(outs_p, lambda o: window(o["v_last"])), window_t(last_s["v_last"]),
        jnp.stack(mks), jnp.stack(mvs),
        stack(outs_s, lambda o: o["vn"].reshape(bs, ts, w)),
    )
```

```python
import functools
import math

import jax
import jax.numpy as jnp
import numpy as np
from jax import lax
from jax.experimental import pallas as pl
from jax.experimental.pallas import tpu as pltpu

f32 = jnp.float32
bf16 = jnp.bfloat16

D_MODEL = 1024
DEPTH = 2
PAST_LEN = 8192
BRANCH_W = 512
GLA_HEADS = 4
GLA_DK = 64
GLA_DV = 128
GLA_RANK = 16
GLA_TAU = 16.0
LRU_BLOCKS = 8
LRU_BS = 64
CONV_W = 4
LRU_C = 8.0
HEAD_DIM = 64
SWA_HEADS = 8
SWA_KV_HEADS = 2
SWA_GROUP = 4
WINDOW = 128
ROT_DIM = 16
ROPE_THETA = 500000.0
SGU_GROUPS = 4
SGU_GC = 128
SGU_CHUNK = 128
N_MEM = 256
MEM_HEADS = 4
MEM_W = 256
LN_EPS = 1e-5
RMS_EPS = 1e-6
DN_ALPHA = (2 * DEPTH) ** 0.25

LANE = 128
SUBLANE = 8

_ORIG = (("gq", 256), ("gk", 256), ("gv", 512), ("glr", 16), ("gz", 512), ("lx", 512), ("lz", 512), ("sq", 512),
         ("sk", 128), ("sv", 128), ("sz", 512), ("su", 512), ("svv", 512), ("suz", 512), ("mq", 256), ("mz", 256),
         ("gates", 5 * D_MODEL))
_ORIG_OFF = {}
_off = 0
for _n, _w in _ORIG:
    _ORIG_OFF[_n] = (_off, _w)
    _off += _w
_PROJ_RUNS = (("gq", "gv"), ("gz", "sq"), ("sz", "sz"), ("sk", "sv"), ("glr", "glr"), None, ("su", "mz"))
_SEG = {}
_off = 0
for _run in _PROJ_RUNS:
    if _run is None:
        N_PROJ = -(-_off // 1024) * 1024
        _off = N_PROJ
        continue
    _names = [n for n, _ in _ORIG]
    for _n in _names[_names.index(_run[0]):_names.index(_run[1]) + 1]:
        _w = max(_ORIG_OFF[_n][1], 128)
        assert _off % _w == 0
        _SEG[_n] = (_off, _w)
        _off += _w


def _dot(a, b):
    return jnp.dot(a.astype(bf16), b.astype(bf16), preferred_element_type=f32)


def _dot_nt(a, b):
    return lax.dot_general(a.astype(bf16), b.astype(bf16), (((1,), (1,)), ((), ())), preferred_element_type=f32)


def _dot_tn(a, b):
    return lax.dot_general(a.astype(bf16), b.astype(bf16), (((0,), (0,)), ((), ())), preferred_element_type=f32)


def _sigmoid(x):
    return 0.5 * jnp.tanh(0.5 * x) + 0.5


def _silu(x):
    return x * _sigmoid(x)


def _log_sigmoid(x):
    return jnp.minimum(x, 0.0) - jnp.log(1.0 + jnp.exp(-jnp.abs(x)))


def _layer_norm(x, g, b):
    mu = jnp.mean(x, axis=-1, keepdims=True)
    xc = x - mu
    var = jnp.mean(xc * xc, axis=-1, keepdims=True)
    return xc * lax.rsqrt(var + LN_EPS) * g + b


def _params(*sem):
    return pltpu.CompilerParams(dimension_semantics=sem)


def _ln_kernel(x_ref, g_ref, b_ref, ob_ref):
    ob_ref[...] = _layer_norm(x_ref[...], g_ref[...], b_ref[...]).astype(bf16)


def _ln_call(x, g, b, tm=2048):
    n, d = x.shape
    tm = min(tm, n)
    return pl.pallas_call(
        _ln_kernel,
        grid=(n // tm,),
        in_specs=[pl.BlockSpec((tm, d), lambda i: (i, 0)), pl.BlockSpec((1, d), lambda i: (0, 0)),
                  pl.BlockSpec((1, d), lambda i: (0, 0))],
        out_specs=pl.BlockSpec((tm, d), lambda i: (i, 0)),
        out_shape=jax.ShapeDtypeStruct((n, d), bf16),
        compiler_params=_params("parallel"),
        name="ln_in",
    )(x, g.reshape(1, d), b.reshape(1, d))


def _matmul_kernel(x_ref, w_ref, o_ref, *, w_transposed):
    o_ref[...] = (_dot_nt if w_transposed else _dot)(x_ref[...], w_ref[...])


def _matmul_call(x, w, tm, tn, name, w_transposed=False, n_out=None, layer=None):
    m, k = x.shape
    n = n_out or (w.shape[-2] if w_transposed else w.shape[1])
    if layer is not None:
        w_spec = pl.BlockSpec((None, tn, k), lambda i, j: (layer, j, 0))
    elif w_transposed:
        w_spec = pl.BlockSpec((tn, k), lambda i, j: (j, 0))
    else:
        w_spec = pl.BlockSpec((k, tn), lambda i, j: (0, j))
    return pl.pallas_call(
        functools.partial(_matmul_kernel, w_transposed=w_transposed),
        grid=(m // tm, n // tn),
        in_specs=[pl.BlockSpec((tm, k), lambda i, j: (i, 0)), w_spec],
        out_specs=pl.BlockSpec((tm, tn), lambda i, j: (i, j)),
        out_shape=jax.ShapeDtypeStruct((m, n), f32),
        compiler_params=_params("parallel", "arbitrary"),
        name=name,
    )(x, w)


def _seg_spec(name, nseq, rows):
    off, width = _SEG[name]
    assert off + width <= N_PROJ
    cb = off // width
    return pl.BlockSpec((nseq, rows, width), lambda b, c: (b, c, cb))


def _tok_spec(nseq, rows, width):
    return pl.BlockSpec((nseq, rows, width), lambda b, c: (b, c, 0))


def _w_spec(first, last, layer):
    off = _ORIG_OFF[first][0]
    rows = _ORIG_OFF[last][0] + _ORIG_OFF[last][1] - off
    assert off % 16 == 0
    return pl.BlockSpec((pl.Element(1), pl.Element(rows), pl.Element(D_MODEL)), lambda *_: (layer, off, 0))


def _seg_scratch(name, *lead):
    return pltpu.VMEM((*lead, _SEG[name][1]), f32)


def _project(x_ref, pairs):
    x2 = x_ref[...].reshape(-1, D_MODEL)
    for w_ref, seg_refs in pairs:
        p = lax.dot_general(x2, w_ref[0], (((1,), (1,)), ((), ())), preferred_element_type=f32)
        off = 0
        for s_ref in seg_refs:
            width = s_ref.shape[-1]
            s_ref[...] = p[:, off:off + width].reshape(s_ref.shape)
            off += width


def _const_spec(shape):
    nd = len(shape)
    return pl.BlockSpec(shape, lambda b, c: (0,) * nd)


def _lru_kernel(lx_ref, lz_ref, hist0_ref, h0_ref, cw_ref, cb_ref, wr_ref, br_ref, wi_ref, bi_ref, lam_ref,
                y_ref, hlast_ref, hist_out_ref, hist_ref, hc_ref, *, nseq, tc):
    c = pl.program_id(1)
    w = BRANCH_W

    @pl.when(c == 0)
    def _():
        hist_ref[...] = hist0_ref[...]
        hc_ref[...] = h0_ref[...]

    x = lx_ref[...]
    xfull = jnp.concatenate([hist_ref[...], x], axis=1)

    def tap(j):
        return cw_ref[j:j + 1, :].reshape(1, 1, w)

    y = cb_ref[...].reshape(1, 1, w) + x * tap(CONV_W - 1)
    for s in range(1, CONV_W):
        y = y + pltpu.roll(xfull, s, 1)[:, SUBLANE:, :] * tap(CONV_W - 1 - s)
    hist_ref[...] = xfull[:, tc:, :]
    hist_out_ref[...] = xfull[:, tc:, :]

    xc = y.reshape(nseq * tc, w)
    r = _sigmoid(_dot(xc, wr_ref[...]) + br_ref[...])
    i = _sigmoid(_dot(xc, wi_ref[...]) + bi_ref[...])
    log_a = (LRU_C * r) * _log_sigmoid(lam_ref[...])
    a = jnp.exp(log_a)
    u = jnp.sqrt(jnp.tanh(-log_a) * (a * a + 1.0)) * (i * xc)

    acc_a = a.reshape(nseq, tc, w)
    acc_u = u.reshape(nseq, tc, w)
    t = lax.broadcasted_iota(jnp.int32, (nseq, tc, w), 1)
    d = 1
    while d < tc:
        if d % SUBLANE:
            ok = t >= d
            a_sh = jnp.where(ok, pltpu.roll(acc_a, d, 1), 1.0)
            u_sh = jnp.where(ok, pltpu.roll(acc_u, d, 1), 0.0)
            acc_u = acc_a * u_sh + acc_u
            acc_a = acc_a * a_sh
        else:
            new_u = acc_a[:, d:, :] * acc_u[:, :tc - d, :] + acc_u[:, d:, :]
            new_a = acc_a[:, d:, :] * acc_a[:, :tc - d, :]
            acc_u = jnp.concatenate([acc_u[:, :d, :], new_u], axis=1)
            acc_a = jnp.concatenate([acc_a[:, :d, :], new_a], axis=1)
        d *= 2
    h = acc_a * hc_ref[...] + acc_u
    hc_ref[...] = h[:, tc - 1:tc, :]
    hlast_ref[...] = h[:, tc - SUBLANE:, :]
    y_ref[...] = (h * _silu(lz_ref[...])).astype(y_ref.dtype)


def _lru_call(proj, hist0, h0, lw, nb, nseq, nc, tc, name):
    w = BRANCH_W
    kern = functools.partial(_lru_kernel, nseq=nseq, tc=tc)
    return pl.pallas_call(
        kern,
        grid=(nb, nc),
        in_specs=[_seg_spec("lx", nseq, tc), _seg_spec("lz", nseq, tc),
                  pl.BlockSpec((nseq, SUBLANE, w), lambda b, c: (b, 0, 0)),
                  pl.BlockSpec((nseq, 1, w), lambda b, c: (b, 0, 0)),
                  _const_spec((CONV_W, w)), _const_spec((1, w)), _const_spec((w, w)), _const_spec((1, w)),
                  _const_spec((w, w)), _const_spec((1, w)), _const_spec((1, w))],
        out_specs=[_tok_spec(nseq, tc, w),
                   pl.BlockSpec((nseq, SUBLANE, w), lambda b, c: (b, 0, 0)),
                   pl.BlockSpec((nseq, SUBLANE, w), lambda b, c: (b, 0, 0))],
        out_shape=[jax.ShapeDtypeStruct((nb * nseq, nc * tc, w), bf16),
                   jax.ShapeDtypeStruct((nb * nseq, SUBLANE, w), f32),
                   jax.ShapeDtypeStruct((nb * nseq, SUBLANE, w), f32)],
        scratch_shapes=[pltpu.VMEM((nseq, SUBLANE, w), f32), pltpu.VMEM((nseq, 1, w), f32)],
        compiler_params=_params("parallel", "arbitrary"),
        name=name,
    )(proj, proj, hist0, h0, lw["conv_w"], lw["conv_b"], lw["wr"], lw["br"], lw["wi"], lw["bi"], lw["lam"])


def _gla_consts(c):
    t = np.arange(c)[:, None]
    u = np.arange(c)[None, :]
    blocks = [u <= t, u > t]
    masks = [t == u]
    m = 1
    while m < c:
        t0 = (t // m) * m
        odd = (t // m) % 2 == 1
        blocks.append(odd & (u >= t0) & (u <= t))
        blocks.append((~odd) & (u > t) & (u <= t0 + m - 1))
        masks.append((t // (2 * m) == u // (2 * m)) & odd & ((u // m) % 2 == 0))
        m *= 2
    return (np.concatenate(blocks, 0).astype(np.float32), np.stack(masks).astype(np.float32))


def _gla_kernel(gq_ref, gk_ref, gv_ref, gz_ref, glr_ref, s0_ref, wa_ref, ba_ref, ng_ref, d_ref, m_ref,
                y_ref, sout_ref, s_ref, att_ref, o_ref, upd_ref, *, nseq, c):
    ci = pl.program_id(1)
    nlev = int(math.log2(c))
    hk = GLA_HEADS * GLA_DK
    heads = range(GLA_HEADS)

    @pl.when(ci == 0)
    def _():
        s_ref[...] = s0_ref[...]

    def ks(h):
        return slice(h * GLA_DK, (h + 1) * GLA_DK)

    def vs(h):
        return slice(h * GLA_DV, (h + 1) * GLA_DV)

    z = _dot(glr_ref[...].reshape(nseq * c, LANE), wa_ref[...]) + ba_ref[...]
    la = _log_sigmoid(z) * (1.0 / GLA_TAU)
    hi = la.astype(bf16)
    r1 = la - hi.astype(f32)
    mid = r1.astype(bf16)
    lo = (r1 - mid.astype(f32)).astype(bf16)
    hml = jnp.concatenate([hi, mid, lo], axis=1)

    q_in, k_st, dec_rows, qf, kf = [], [], [], [], []
    for n in range(nseq):
        hml_n = hml[n * c:(n + 1) * c]

        p = jnp.dot(d_ref[0:c, :], hml_n, preferred_element_type=f32)
        b = p[:, :hk] + p[:, hk:2 * hk] + p[:, 2 * hk:]

        def decay(lev, query_side):
            m = 2 ** lev
            if m < SUBLANE:
                blk = 2 + 2 * lev + (0 if query_side else 1)
                pm = jnp.dot(d_ref[blk * c:(blk + 1) * c, :], hml_n[:, :2 * hk], preferred_element_type=f32)
                return jnp.exp(pm[:, :hk] + pm[:, hk:])
            pieces = []
            for j in range(c // m):
                rows = b[j * m:(j + 1) * m]
                if query_side and j % 2 == 1:
                    pieces.append(rows - b[j * m - 1:j * m])
                elif not query_side and j % 2 == 0:
                    pieces.append(b[(j + 1) * m - 1:(j + 1) * m] - rows)
                else:
                    pieces.append(jnp.zeros((m, hk), f32))
            return jnp.exp(jnp.concatenate(pieces, axis=0))

        q = gq_ref[n] * (GLA_DK ** -0.5)
        k = gk_ref[n]
        eb = jnp.exp(b)
        q_in.append((q * eb).astype(bf16))
        k_st.append((k * jnp.exp(b[c - 1:c] - b)).astype(bf16))
        dec_rows.append(eb[c - 1:c, :])
        qf.append([q.astype(bf16)] + [(q * decay(lev, True)).astype(bf16) for lev in range(nlev)])
        kf.append([k.astype(bf16)] + [(k * decay(lev, False)).astype(bf16) for lev in range(nlev)])

    for n in range(nseq):
        for lev in range(nlev + 1):
            for h in heads:
                att_ref[n, h, lev] = _dot_nt(qf[n][lev][:, ks(h)], kf[n][lev][:, ks(h)])

    att = m_ref[0][None, None] * att_ref[:, :, 0]
    for lev in range(1, nlev + 1):
        att = att + m_ref[lev][None, None] * att_ref[:, :, lev]
    att = att.astype(bf16)

    for n in range(nseq):
        v = gv_ref[n].astype(bf16)
        for h in heads:
            v_h = v[:, vs(h)]
            o_ref[n, h] = _dot(q_in[n][:, ks(h)], s_ref[n, h]) + _dot(att[n, h], v_h)
            upd_ref[n, h] = _dot_tn(k_st[n][:, ks(h)], v_h)

    eye = (lax.broadcasted_iota(jnp.int32, (GLA_DK, GLA_DK), 0)
           == lax.broadcasted_iota(jnp.int32, (GLA_DK, GLA_DK), 1))[None, None]
    dec = jnp.stack([jnp.stack([dec_rows[n][:, ks(h)] for h in heads]) for n in range(nseq)])
    dec_col = jnp.sum(jnp.where(eye, jnp.broadcast_to(dec, (nseq, GLA_HEADS, GLA_DK, GLA_DK)), 0.0),
                      axis=3, keepdims=True)
    s_new = s_ref[...] * dec_col + upd_ref[...]
    s_ref[...] = s_new
    sout_ref[...] = s_new
    o = o_ref[...]
    o = o * lax.rsqrt(jnp.mean(o * o, axis=-1, keepdims=True) + RMS_EPS) * ng_ref[...]
    for n in range(nseq):
        for h in heads:
            y_ref[n, :, vs(h)] = (o[n, h] * _silu(gz_ref[n, :, vs(h)])).astype(y_ref.dtype)


_GLA_SEGS = ("gq", "gk", "gv", "gz", "glr")


def _gla_call(proj, s0, gw, nb, nseq, nc, c, name):
    dstack, masks = _gla_consts(c)
    kern = functools.partial(_gla_kernel, nseq=nseq, c=c)
    hk = GLA_HEADS * GLA_DK
    st_spec = pl.BlockSpec((nseq, GLA_HEADS, GLA_DK, GLA_DV), lambda b, ci: (b, 0, 0, 0))
    return pl.pallas_call(
        kern,
        grid=(nb, nc),
        in_specs=[_seg_spec(s, nseq, c) for s in _GLA_SEGS] + [
            st_spec, _const_spec((LANE, hk)), _const_spec((1, hk)), _const_spec((1, GLA_DV)),
            _const_spec(dstack.shape), _const_spec(masks.shape)],
        out_specs=[_tok_spec(nseq, c, BRANCH_W), st_spec],
        out_shape=[jax.ShapeDtypeStruct((nb * nseq, nc * c, BRANCH_W), bf16),
                   jax.ShapeDtypeStruct((nb * nseq, GLA_HEADS, GLA_DK, GLA_DV), f32)],
        scratch_shapes=[pltpu.VMEM((nseq, GLA_HEADS, GLA_DK, GLA_DV), f32),
                        pltpu.VMEM((nseq, GLA_HEADS, masks.shape[0], c, c), f32),
                        pltpu.VMEM((nseq, GLA_HEADS, c, GLA_DV), f32),
                        pltpu.VMEM((nseq, GLA_HEADS, GLA_DK, GLA_DV), f32)],
        compiler_params=_params("parallel", "arbitrary"),
        name=name,
    )(*([proj] * len(_GLA_SEGS)), s0, gw["wa"], gw["ba"], gw["ng"], jnp.asarray(dstack, bf16), jnp.asarray(masks))


def _gla_step_consts(t, nseq):
    dstack, masks = _gla_consts(t)
    eye = np.eye(nseq, dtype=np.float32)
    dbd = np.concatenate([np.kron(eye, dstack[i * t:(i + 1) * t]) for i in range(dstack.shape[0] // t)], axis=0)
    mbd = np.stack([np.kron(eye, m) for m in masks])
    return dbd, mbd


def _gla_step_kernel(gq_ref, gk_ref, gv_ref, gz_ref, glr_ref, s0_ref, wa_ref, ba_ref, ng_ref, d_ref, m_ref,
                     *refs, nseq, t, nprev):
    prev_refs, (y_ref, sout_ref) = refs[:nprev], refs[nprev:]
    if nprev:
        for i, p_ref in enumerate(prev_refs):
            sout_ref[i] = p_ref[...]
        sout_ref = sout_ref.at[nprev]
    r = nseq * t
    nlev = int(math.log2(t))
    hk = GLA_HEADS * GLA_DK
    q = gq_ref[...].reshape(r, hk) * (GLA_DK ** -0.5)
    k = gk_ref[...].reshape(r, hk)
    v = gv_ref[...].reshape(r, BRANCH_W)
    gz = gz_ref[...].reshape(r, BRANCH_W)
    z = _dot(glr_ref[...].reshape(r, LANE), wa_ref[...]) + ba_ref[...]
    la = _log_sigmoid(z) * (1.0 / GLA_TAU)
    hi = la.astype(bf16)
    r1 = la - hi.astype(f32)
    mid = r1.astype(bf16)
    lo = (r1 - mid.astype(f32)).astype(bf16)
    hml = jnp.concatenate([hi, mid, lo], axis=1)

    def decay(blk):
        p = jnp.dot(d_ref[blk * r:(blk + 1) * r, :], hml, preferred_element_type=f32)
        return jnp.exp(p[:, :hk] + p[:, hk:2 * hk] + p[:, 2 * hk:])

    eb = decay(0)
    q_in = q * eb
    k_st = k * decay(1)
    dec3 = eb.reshape(nseq, t, hk)[:, t - 1:t, :]
    qf = [q]
    kf = [k]
    for lev in range(nlev):
        qf.append(q * decay(2 + 2 * lev))
        kf.append(k * decay(3 + 2 * lev))

    own = (lax.broadcasted_iota(jnp.int32, (r, nseq * GLA_DK), 0) // t
           == lax.broadcasted_iota(jnp.int32, (r, nseq * GLA_DK), 1) // GLA_DK)
    eye = (lax.broadcasted_iota(jnp.int32, (GLA_DK, GLA_DK), 0)
           == lax.broadcasted_iota(jnp.int32, (GLA_DK, GLA_DK), 1))[None]

    def spread(x):
        x2 = jnp.concatenate([x, x], axis=1)
        return jnp.where(own, jnp.concatenate([x2] * (nseq // 2), axis=1), 0.0)

    ys = []
    for h in range(GLA_HEADS):
        ks = slice(h * GLA_DK, (h + 1) * GLA_DK)
        vs = slice(h * GLA_DV, (h + 1) * GLA_DV)
        att = jnp.zeros((r, r), f32)
        for lev in range(nlev + 1):
            att = att + m_ref[lev] * _dot_nt(qf[lev][:, ks], kf[lev][:, ks])
        s_h = s0_ref[:, h]
        v_h = v[:, vs]
        o = _dot(spread(q_in[:, ks]), s_h.reshape(nseq * GLA_DK, GLA_DV)) + _dot(att, v_h)
        upd = _dot_tn(spread(k_st[:, ks]), v_h)
        dec_col = jnp.sum(jnp.where(eye, jnp.broadcast_to(dec3[:, :, ks], (nseq, GLA_DK, GLA_DK)), 0.0),
                          axis=2, keepdims=True)
        sout_ref[:, h] = s_h * dec_col + upd.reshape(nseq, GLA_DK, GLA_DV)
        o = o * lax.rsqrt(jnp.mean(o * o, axis=-1, keepdims=True) + RMS_EPS) * ng_ref[...]
        ys.append(o * _silu(gz[:, vs]))
    y_ref[...] = jnp.concatenate(ys, axis=1).reshape(nseq, t, BRANCH_W).astype(y_ref.dtype)


def _stacked_specs(prev, inner):
    nd = len(inner)
    one = pl.BlockSpec(inner, lambda b, c: (b,) + (0,) * (nd - 1))
    if not prev:
        return [], one, lambda nbatch: jax.ShapeDtypeStruct((nbatch,) + inner[1:], f32)
    n = len(prev) + 1
    stacked = pl.BlockSpec((n,) + inner, lambda b, c: (0, b) + (0,) * (nd - 1))
    return [one] * len(prev), stacked, lambda nbatch: jax.ShapeDtypeStruct((n, nbatch) + inner[1:], f32)


def _gla_step_call(proj, s0_all, layer, prev, gw, nb, nseq, t, name):
    dbd, mbd = _gla_step_consts(t, nseq)
    kern = functools.partial(_gla_step_kernel, nseq=nseq, t=t, nprev=len(prev))
    hk = GLA_HEADS * GLA_DK
    prev_specs, st_spec, st_shape = _stacked_specs(prev, (nseq, GLA_HEADS, GLA_DK, GLA_DV))
    s0_spec = pl.BlockSpec((None, nseq, GLA_HEADS, GLA_DK, GLA_DV), lambda b, ci: (layer, b, 0, 0, 0))
    return pl.pallas_call(
        kern,
        grid=(nb, 1),
        in_specs=[_seg_spec(s, nseq, t) for s in _GLA_SEGS] + [
            s0_spec, _const_spec((LANE, hk)), _const_spec((1, hk)), _const_spec((1, GLA_DV)),
            _const_spec(dbd.shape), _const_spec(mbd.shape)] + prev_specs,
        out_specs=[_tok_spec(nseq, t, BRANCH_W), st_spec],
        out_shape=[jax.ShapeDtypeStruct((nb * nseq, t, BRANCH_W), bf16), st_shape(nb * nseq)],
        compiler_params=_params("parallel", "arbitrary"),
        name=name,
    )(*([proj] * len(_GLA_SEGS)), s0_all, gw["wa"], gw["ba"], gw["ng"], jnp.asarray(dbd, bf16), jnp.asarray(mbd),
      *prev)


def _rope_tables(pos0, t):
    half = ROT_DIM // 2
    dim = jnp.arange(LANE) % HEAD_DIM
    inv = ROPE_THETA ** (-(dim % half).astype(f32) / half)
    ang = (pos0 + jnp.arange(t)).astype(f32)[:, None] * inv[None, :]
    cos, sin = jnp.cos(ang), jnp.sin(ang)
    first, second = (dim < half)[None, :], ((dim >= half) & (dim < ROT_DIM))[None, :]
    c_tab = jnp.where(first | second, cos, 1.0)
    sa_tab = jnp.where(first, -sin, 0.0)
    sb_tab = jnp.where(second, sin, 0.0)
    return c_tab, sa_tab, sb_tab


def _rope(x, c_tab, sa_tab, sb_tab):
    wd = x.shape[-1]
    ax = x.ndim - 1
    rep = wd // LANE
    half = ROT_DIM // 2
    if rep > 1:
        c_tab, sa_tab, sb_tab = (jnp.concatenate([tb] * rep, axis=-1) for tb in (c_tab, sa_tab, sb_tab))
    return x * c_tab + pltpu.roll(x, wd - half, ax) * sa_tab + pltpu.roll(x, half, ax) * sb_tab


def _swa_kernel(sink_ref, sq_ref, sz_ref, sk_ref, sv_ref, ct_ref, sat_ref, sbt_ref, kp_ref, vp_ref,
                y_ref, klast_ref, vlast_ref, kprev_ref, vprev_ref, sp_ref, sc_ref, o_ref, den_ref,
                *, nseq, qb, pos0):
    blk = pl.program_id(1)
    hd = HEAD_DIM

    @pl.when(blk == 0)
    def _():
        kprev_ref[...] = kp_ref[...]
        vprev_ref[...] = vp_ref[...]

    tabs = tuple(r[...][None] for r in (ct_ref, sat_ref, sbt_ref))
    q3 = _rope(sq_ref[...], *tabs) * (hd ** -0.5)
    k3 = _rope(sk_ref[...], *tabs)
    v3 = sv_ref[...]
    klast_ref[...] = k3
    vlast_ref[...] = v3

    for n in range(nseq):
        for kv in range(SWA_KV_HEADS):
            ds = slice(kv * hd, (kv + 1) * hd)
            qs = jnp.concatenate(
                [q3[n][:, (kv * SWA_GROUP + g) * hd:(kv * SWA_GROUP + g + 1) * hd] for g in range(SWA_GROUP)],
                axis=0).astype(bf16)
            sp_ref[n, kv] = _dot_nt(qs, kprev_ref[n][:, ds])
            sc_ref[n, kv] = _dot_nt(qs, k3[n][:, ds])

    mrows = SWA_GROUP * qb
    qi = lax.broadcasted_iota(jnp.int32, (mrows, WINDOW), 0) % qb
    kj = lax.broadcasted_iota(jnp.int32, (mrows, WINDOW), 1)
    past_ok = (kj >= qi) & (kj >= (WINDOW - pos0) - blk * qb)
    cur_ok = kj <= qi
    sink = jnp.stack([jnp.concatenate([jnp.full((qb, 1), sink_ref[kv * SWA_GROUP + g], f32)
                                       for g in range(SWA_GROUP)], axis=0) for kv in range(SWA_KV_HEADS)])[None]
    s_p = jnp.where(past_ok[None, None], sp_ref[...], -jnp.inf)
    s_c = jnp.where(cur_ok[None, None], sc_ref[...], -jnp.inf)
    m = jnp.maximum(jnp.max(jnp.maximum(s_p, s_c), axis=3, keepdims=True), sink)
    p_p = jnp.exp(s_p - m).astype(bf16)
    p_c = jnp.exp(s_c - m).astype(bf16)
    ones = jnp.ones((WINDOW, LANE), bf16)
    for n in range(nseq):
        vp = vprev_ref[n].astype(bf16)
        vc = v3[n].astype(bf16)
        for kv in range(SWA_KV_HEADS):
            o_ref[n, kv] = _dot(p_p[n, kv], vp) + _dot(p_c[n, kv], vc)
            den_ref[n, kv] = _dot(p_p[n, kv], ones) + _dot(p_c[n, kv], ones)
    o = o_ref[...] / (den_ref[...] + jnp.exp(sink - m))
    outs = [o[:, j // SWA_GROUP, (j % SWA_GROUP) * qb:(j % SWA_GROUP + 1) * qb,
              (j // SWA_GROUP) * hd:(j // SWA_GROUP + 1) * hd] for j in range(SWA_HEADS)]
    y_ref[...] = (jnp.concatenate(outs, axis=2) * _silu(sz_ref[...])).astype(y_ref.dtype)
    kprev_ref[...] = k3
    vprev_ref[...] = v3


_SWA_SEGS = ("sq", "sz", "sk", "sv")


def _swa_call(proj, sinks, k_past, v_past, pos0, nb, nseq, nc, qb, name):
    assert qb == WINDOW
    t_total = nc * qb
    c_tab, sa_tab, sb_tab = _rope_tables(pos0, t_total)
    kern = functools.partial(_swa_kernel, nseq=nseq, qb=qb, pos0=pos0)
    kvw = SWA_KV_HEADS * HEAD_DIM
    mrows = SWA_GROUP * qb
    tab_spec = pl.BlockSpec((qb, LANE), lambda b, c: (c, 0))
    past_spec = pl.BlockSpec((nseq, WINDOW, kvw), lambda b, c: (b, 0, 0))
    return pl.pallas_call(
        kern,
        grid=(nb, nc),
        in_specs=[pl.BlockSpec(memory_space=pltpu.SMEM)] + [_seg_spec(s, nseq, qb) for s in _SWA_SEGS] + [
            tab_spec, tab_spec, tab_spec, past_spec, past_spec],
        out_specs=[_tok_spec(nseq, qb, BRANCH_W), past_spec, past_spec],
        out_shape=[jax.ShapeDtypeStruct((nb * nseq, nc * qb, BRANCH_W), bf16),
                   jax.ShapeDtypeStruct((nb * nseq, WINDOW, kvw), f32),
                   jax.ShapeDtypeStruct((nb * nseq, WINDOW, kvw), f32)],
        scratch_shapes=[pltpu.VMEM((nseq, WINDOW, kvw), f32), pltpu.VMEM((nseq, WINDOW, kvw), f32),
                        pltpu.VMEM((nseq, SWA_KV_HEADS, mrows, WINDOW), f32),
                        pltpu.VMEM((nseq, SWA_KV_HEADS, mrows, qb), f32),
                        pltpu.VMEM((nseq, SWA_KV_HEADS, mrows, kvw), f32),
                        pltpu.VMEM((nseq, SWA_KV_HEADS, mrows, kvw), f32)],
        compiler_params=_params("parallel", "arbitrary"),
        name=name,
    )(sinks, *([proj] * len(_SWA_SEGS)), c_tab, sa_tab, sb_tab, k_past, v_past)


def _swa_step_kernel(sink_ref, sq_ref, sz_ref, sk_ref, sv_ref, ct_ref, sat_ref, sbt_ref, kp_ref, vp_ref,
                     *refs, nseq, t, pos0, nprev):
    prev_refs = refs[:2 * nprev]
    y_ref, klast_ref, vlast_ref, sp_ref, sc_ref, o_ref = refs[2 * nprev:]
    if nprev:
        for i in range(nprev):
            klast_ref[i] = prev_refs[i][...]
            vlast_ref[i] = prev_refs[nprev + i][...]
        klast_ref = klast_ref.at[nprev]
        vlast_ref = vlast_ref.at[nprev]
    hd = HEAD_DIM
    kvw = SWA_KV_HEADS * hd
    mrows = SWA_HEADS * t
    tabs = tuple(r[...][None] for r in (ct_ref, sat_ref, sbt_ref))
    q3 = _rope(sq_ref[...], *tabs) * (hd ** -0.5)
    k3 = _rope(sk_ref[...], *tabs)
    v3 = sv_ref[...]

    lane = lax.broadcasted_iota(jnp.int32, (kvw, WINDOW), 1)
    pad = jnp.zeros((WINDOW - t, kvw), f32)

    def shifted(old_t, new):
        new_t = jnp.concatenate([pad, new], axis=0).T
        out = jnp.where(lane >= WINDOW - t, new_t, pltpu.roll(old_t, WINDOW - t, 1))
        return out.reshape(SWA_KV_HEADS, hd, WINDOW)

    for n in range(nseq):
        klast_ref[n] = shifted(kp_ref[n].reshape(kvw, WINDOW), k3[n])
        vlast_ref[n] = shifted(vp_ref[n].reshape(kvw, WINDOW), v3[n])

    zero = jnp.zeros((nseq, t, hd), f32)
    pieces = []
    for j in range(SWA_HEADS):
        qj = q3[:, :, j * hd:(j + 1) * hd]
        pieces.append(jnp.concatenate([qj, zero] if j // SWA_GROUP == 0 else [zero, qj], axis=2))
    qbd = jnp.concatenate(pieces, axis=1).astype(bf16)

    for n in range(nseq):
        sp_ref[n] = _dot(qbd[n], kp_ref[n].reshape(kvw, WINDOW))
        sc_ref[n] = _dot_nt(qbd[n], k3[n])

    qi = lax.broadcasted_iota(jnp.int32, (mrows, WINDOW), 0) % t
    kj = lax.broadcasted_iota(jnp.int32, (mrows, WINDOW), 1)
    past_ok = kj >= qi
    if pos0 < WINDOW:
        past_ok = past_ok & (kj >= WINDOW - pos0)
    qi_c = lax.broadcasted_iota(jnp.int32, (mrows, t), 0) % t
    kj_c = lax.broadcasted_iota(jnp.int32, (mrows, t), 1)
    cur_ok = kj_c <= qi_c
    sink = jnp.concatenate([jnp.full((t, 1), sink_ref[j], f32) for j in range(SWA_HEADS)], axis=0)[None]
    s_p = jnp.where(past_ok[None], sp_ref[...], -jnp.inf)
    s_c = jnp.where(cur_ok[None], sc_ref[...], -jnp.inf)
    m = jnp.maximum(jnp.maximum(jnp.max(s_p, axis=2, keepdims=True), jnp.max(s_c, axis=2, keepdims=True)), sink)
    p_p = jnp.exp(s_p - m)
    p_c = jnp.exp(s_c - m)
    den = jnp.sum(p_p, axis=2, keepdims=True) + jnp.sum(p_c, axis=2, keepdims=True) + jnp.exp(sink - m)
    p_p = p_p.astype(bf16)
    p_c = p_c.astype(bf16)
    for n in range(nseq):
        o_ref[n] = _dot_nt(p_p[n], vp_ref[n].reshape(kvw, WINDOW)) + _dot(p_c[n], v3[n])
    o = o_ref[...] / den
    outs = []
    for j in range(SWA_HEADS):
        kv = j // SWA_GROUP
        outs.append(o[:, j * t:(j + 1) * t, kv * hd:(kv + 1) * hd])
    y_ref[...] = (jnp.concatenate(outs, axis=2) * _silu(sz_ref[...])).astype(y_ref.dtype)


def _swa_step_call(proj, sinks, k_past_t, v_past_t, layer, prev_k, prev_v, pos0, nb, nseq, t, name):
    c_tab, sa_tab, sb_tab = _rope_tables(pos0, t)
    kern = functools.partial(_swa_step_kernel, nseq=nseq, t=t, pos0=pos0, nprev=len(prev_k))
    kvw = SWA_KV_HEADS * HEAD_DIM
    mrows = SWA_HEADS * t
    tab_spec = pl.BlockSpec((t, LANE), lambda b, c: (0, 0))
    past_spec = pl.BlockSpec((None, nseq, SWA_KV_HEADS, HEAD_DIM, WINDOW), lambda b, c: (layer, b, 0, 0, 0))
    prev_specs, new_spec, new_shape = _stacked_specs(prev_k, (nseq, SWA_KV_HEADS, HEAD_DIM, WINDOW))
    return pl.pallas_call(
        kern,
        grid=(nb, 1),
        in_specs=[pl.BlockSpec(memory_space=pltpu.SMEM)] + [_seg_spec(s, nseq, t) for s in _SWA_SEGS] + [
            tab_spec, tab_spec, tab_spec, past_spec, past_spec] + prev_specs + prev_specs,
        out_specs=[_tok_spec(nseq, t, BRANCH_W), new_spec, new_spec],
        out_shape=[jax.ShapeDtypeStruct((nb * nseq, t, BRANCH_W), bf16), new_shape(nb * nseq),
                   new_shape(nb * nseq)],
        scratch_shapes=[pltpu.VMEM((nseq, mrows, WINDOW), f32), pltpu.VMEM((nseq, mrows, t), f32),
                        pltpu.VMEM((nseq, mrows, kvw), f32)],
        compiler_params=_params("parallel", "arbitrary"),
        name=name,
    )(sinks, *([proj] * len(_SWA_SEGS)), c_tab, sa_tab, sb_tab, k_past_t, v_past_t, *prev_k, *prev_v)


def _sgu_kernel(x_ref, wu_ref, wv_ref, wz_ref, g_ref, b_ref, wm_ref, bias_ref, y_ref, *rest, ntile, want_vn):
    vn_ref = rest[0] if want_vn else None
    su_ref, sv_ref, sz_ref = rest[-3:]
    _project(x_ref, ((wu_ref, (su_ref,)), (wv_ref, (sv_ref,)), (wz_ref, (sz_ref,))))
    vn = _layer_norm(sv_ref[...], g_ref[...], b_ref[...])
    if want_vn:
        vn_ref[...] = vn
    for r in range(ntile):
        rows = slice(r * SGU_CHUNK, (r + 1) * SGU_CHUNK)
        mixed = jnp.concatenate(
            [jnp.dot(wm_ref[g], vn[rows, g * SGU_GC:(g + 1) * SGU_GC].astype(bf16), preferred_element_type=f32)
             for g in range(SGU_GROUPS)], axis=1)
        y = su_ref[rows, :] * (mixed + bias_ref[...]) * _silu(sz_ref[rows, :])
        y_ref[rows, :] = y.astype(y_ref.dtype)


def _sgu_call(xb, w_t, layer, ln_g, ln_b, wmix, bias, n_tok, ntile, want_vn, name):
    rows = ntile * SGU_CHUNK
    w = BRANCH_W
    kern = functools.partial(_sgu_kernel, ntile=ntile, want_vn=want_vn)

    def const(shape):
        return pl.BlockSpec(shape, lambda i: (0,) * len(shape))

    out_specs = [pl.BlockSpec((rows, w), lambda i: (i, 0))]
    out_shape = [jax.ShapeDtypeStruct((n_tok, w), bf16)]
    if want_vn:
        out_specs.append(pl.BlockSpec((rows, w), lambda i: (i, 0)))
        out_shape.append(jax.ShapeDtypeStruct((n_tok, w), f32))
    return pl.pallas_call(
        kern,
        grid=(n_tok // rows,),
        in_specs=[pl.BlockSpec((rows, D_MODEL), lambda i: (i, 0)), _w_spec("su", "su", layer),
                  _w_spec("svv", "svv", layer), _w_spec("suz", "suz", layer), const((1, w)), const((1, w)),
                  const((SGU_GROUPS, SGU_CHUNK, SGU_CHUNK)), const((SGU_CHUNK, w))],
        out_specs=out_specs,
        out_shape=out_shape,
        scratch_shapes=[_seg_scratch(s, rows) for s in ("su", "svv", "suz")],
        compiler_params=_params("parallel"),
        name=name,
    )(xb, w_t, w_t, w_t, ln_g, ln_b, wmix, bias)


def _mem_kernel(x_ref, wm_ref, mk_ref, mv_ref, y_ref, mq_ref, mz_ref, s_ref, o_ref, den_ref, *, nseq, tq):
    _project(x_ref, ((wm_ref, (mq_ref, mz_ref)),))
    heads = range(MEM_HEADS)
    lane_head = lax.broadcasted_iota(jnp.int32, (tq, MEM_W), 1) // HEAD_DIM
    for n in range(nseq):
        q = mq_ref[n] * (HEAD_DIM ** -0.5)
        mk = mk_ref[n].astype(bf16)
        for h in heads:
            s_ref[n, h] = _dot_nt(jnp.where(lane_head == h, q, 0.0), mk)
    s = s_ref[...]
    p = jnp.exp(s - jnp.max(s, axis=3, keepdims=True)).astype(bf16)
    ones = jnp.ones((N_MEM, MEM_W), bf16)
    for n in range(nseq):
        mv = mv_ref[n].astype(bf16)
        for h in heads:
            o_ref[n, h] = _dot(p[n, h], mv)
            den_ref[n, h] = _dot(p[n, h], ones)
    o = o_ref[...] / den_ref[...]
    acc = jnp.where(lane_head == 0, o[:, 0], 0.0)
    for h in range(1, MEM_HEADS):
        acc = acc + jnp.where(lane_head == h, o[:, h], 0.0)
    y_ref[...] = (acc * _silu(mz_ref[...])).astype(y_ref.dtype)


def _mem_call(xb, w_t, layer, mk, mv, nb, nseq, nc, tq, name):
    kern = functools.partial(_mem_kernel, nseq=nseq, tq=tq)
    kv_spec = pl.BlockSpec((nseq, N_MEM, MEM_W), lambda b, c: (b, 0, 0))
    return pl.pallas_call(
        kern,
        grid=(nb, nc),
        in_specs=[_tok_spec(nseq, tq, D_MODEL), _w_spec("mq", "mz", layer), kv_spec, kv_spec],
        out_specs=_tok_spec(nseq, tq, MEM_W),
        out_shape=jax.ShapeDtypeStruct((nb * nseq, nc * tq, MEM_W), bf16),
        scratch_shapes=[_seg_scratch("mq", nseq, tq), _seg_scratch("mz", nseq, tq),
                        pltpu.VMEM((nseq, MEM_HEADS, tq, N_MEM), f32), pltpu.VMEM((nseq, MEM_HEADS, tq, MEM_W), f32),
                        pltpu.VMEM((nseq, MEM_HEADS, tq, MEM_W), f32)],
        compiler_params=_params("parallel", "arbitrary"),
        name=name,
    )(xb, w_t, mk, mv)


def _mem_step_kernel(x_ref, wm_ref, mk_ref, mv_ref, y_ref, s_ref, o_ref, mq_ref, mz_ref, *, nseq, t):
    mrows = MEM_HEADS * t
    _project(x_ref, ((wm_ref, (mq_ref, mz_ref)),))
    row_head = lax.broadcasted_iota(jnp.int32, (mrows, MEM_W), 0) // t
    lane_head = lax.broadcasted_iota(jnp.int32, (mrows, MEM_W), 1) // HEAD_DIM
    own = (row_head == lane_head)[None]
    q3 = mq_ref[...] * (HEAD_DIM ** -0.5)
    qbd = jnp.where(own, jnp.concatenate([q3] * MEM_HEADS, axis=1), 0.0).astype(bf16)
    for n in range(nseq):
        s_ref[n] = _dot(qbd[n], mk_ref[n].reshape(MEM_W, N_MEM))
    s = s_ref[...]
    p = jnp.exp(s - jnp.max(s, axis=2, keepdims=True))
    den = jnp.sum(p, axis=2, keepdims=True)
    p = p.astype(bf16)
    for n in range(nseq):
        o_ref[n] = _dot_nt(p[n], mv_ref[n].reshape(MEM_W, N_MEM))
    o = jnp.where(own, o_ref[...] / den, 0.0)
    acc = o[:, 0:t, :]
    for h in range(1, MEM_HEADS):
        acc = acc + o[:, h * t:(h + 1) * t, :]
    y_ref[...] = (acc * _silu(mz_ref[...])).astype(y_ref.dtype)


def _mem_step_call(xb, w_t, mk_t, mv_t, layer, nb, nseq, t, name):
    kern = functools.partial(_mem_step_kernel, nseq=nseq, t=t)
    kv_spec = pl.BlockSpec((None, nseq, MEM_HEADS, HEAD_DIM, N_MEM), lambda b, c: (layer, b, 0, 0, 0))
    mrows = MEM_HEADS * t
    return pl.pallas_call(
        kern,
        grid=(nb, 1),
        in_specs=[_tok_spec(nseq, t, D_MODEL), _w_spec("mq", "mz", layer), kv_spec, kv_spec],
        out_specs=_tok_spec(nseq, t, MEM_W),
        out_shape=jax.ShapeDtypeStruct((nb * nseq, t, MEM_W), bf16),
        scratch_shapes=[pltpu.VMEM((nseq, mrows, N_MEM), f32), pltpu.VMEM((nseq, mrows, MEM_W), f32),
                        _seg_scratch("mq", nseq, t), _seg_scratch("mz", nseq, t)],
        compiler_params=_params("parallel", "arbitrary"),
        name=name,
    )(xb, w_t, mk_t, mv_t)


def _merge_kernel(yg_ref, yl_ref, ys_ref, yu_ref, ym_ref, x_ref, ig_ref, ib_ref, wg_ref, wb_ref, wm_ref, wo_ref,
                  g_ref, b_ref, o_ref, ob_ref, *, pre_ln):
    d = D_MODEL
    x = x_ref[...]
    if pre_ln:
        x = _layer_norm(x, ig_ref[...], ib_ref[...])
    xb = x.astype(bf16)

    def gate(n):
        return _sigmoid(_dot_nt(xb, wg_ref[0, n * d:(n + 1) * d, :]))

    merged = gate(4) * jnp.dot(ym_ref[...], wm_ref[...], preferred_element_type=f32)
    for n, y_ref in enumerate((yg_ref, yl_ref, ys_ref, yu_ref)):
        merged = merged + gate(n) * jnp.dot(y_ref[...], wb_ref[n], preferred_element_type=f32)
    out = _dot(merged, wo_ref[...])
    y = _layer_norm(DN_ALPHA * x + out, g_ref[...], b_ref[...])
    o_ref[...] = y
    ob_ref[...] = y.astype(bf16)


def _merge_call(ys, x, ln_in, mw, layer, tm, name):
    n_tok, d = x.shape
    w = BRANCH_W

    def rows(width):
        return pl.BlockSpec((tm, width), lambda i: (i, 0))

    def const(shape):
        return pl.BlockSpec(shape, lambda i: (0,) * len(shape), pipeline_mode=pl.Buffered(1))

    gates_spec = pl.BlockSpec((pl.Element(1), pl.Element(5 * d), pl.Element(d)),
                              lambda i: (layer, _ORIG_OFF["gates"][0], 0), pipeline_mode=pl.Buffered(1))
    return pl.pallas_call(
        functools.partial(_merge_kernel, pre_ln=layer == 0),
        grid=(n_tok // tm,),
        in_specs=[rows(w), rows(w), rows(w), rows(w), rows(MEM_W), rows(d), const((1, d)), const((1, d)),
                  gates_spec, const((4, w, d)), const((MEM_W, d)), const((d, d)), const((1, d)),
                  const((1, d))],
        out_specs=[rows(d), rows(d)],
        out_shape=[jax.ShapeDtypeStruct((n_tok, d), f32), jax.ShapeDtypeStruct((n_tok, d), bf16)],
        compiler_params=_params("parallel"),
        name=name,
    )(*ys, x, ln_in[0].reshape(1, d), ln_in[1].reshape(1, d), mw["wg"], mw["wb"], mw["wm"], mw["wo"], mw["g"],
      mw["b"])


def _prep_w_in(w_in):
    w_t = jnp.swapaxes(w_in, 1, 2).astype(bf16)
    runs = []
    used = 0
    for run in _PROJ_RUNS:
        if run is None:
            runs.append(jnp.zeros((DEPTH, N_PROJ - used, D_MODEL), bf16))
            break
        lo = _ORIG_OFF[run[0]][0]
        hi = _ORIG_OFF[run[1]][0] + _ORIG_OFF[run[1]][1]
        runs.append(w_t[:, lo:hi])
        used += hi - lo
    return jnp.concatenate(runs, axis=1), w_t


def _prep_layer(l, w_proj, w_gates, gla_wa2, gla_ba, gla_norm_g, lru_conv_w, lru_conv_b, lru_wr, lru_br, lru_wi,
                lru_bi, lru_L, swa_sinks, sgu_ln_g, sgu_ln_b, sgu_w, sgu_b, w_mem_kv, w_branch, w_branch_mem, w_out,
                ln_g, ln_b):
    d = D_MODEL
    w = BRANCH_W

    def block_diag(wb):
        eye = jnp.eye(LRU_BLOCKS, dtype=f32)
        return (eye[:, None, :, None] * wb[:, :, None, :]).reshape(w, w).astype(bf16)

    tril = jnp.tril(jnp.ones((SGU_CHUNK, SGU_CHUNK), f32))
    wmix_p = (sgu_w[l] * tril).astype(bf16)
    bias_p = jnp.repeat(sgu_b[l].T, SGU_GC, axis=1)
    t8 = SUBLANE
    rep = SGU_CHUNK // t8
    w8 = (sgu_w[l] * tril)[:, :t8, :t8]
    seq_eye = jnp.eye(rep, dtype=f32)
    wmix_s = (seq_eye[None, :, None, :, None] * w8[:, None, :, None, :]).reshape(
        SGU_GROUPS, SGU_CHUNK, SGU_CHUNK).astype(bf16)
    bias_s = jnp.tile(bias_p[:t8], (rep, 1))
    return dict(
        w_proj=w_proj,
        w_mem_kv=w_mem_kv[l].astype(bf16),
        gla=dict(wa=jnp.pad(gla_wa2[l], ((0, LANE - GLA_RANK), (0, 0))).astype(bf16),
                 ba=gla_ba[l].reshape(1, -1), ng=gla_norm_g[l].reshape(1, -1)),
        lru=dict(conv_w=lru_conv_w[l], conv_b=lru_conv_b[l].reshape(1, w), wr=block_diag(lru_wr[l]),
                 br=lru_br[l].reshape(1, w), wi=block_diag(lru_wi[l]), bi=lru_bi[l].reshape(1, w),
                 lam=lru_L[l].reshape(1, w)),
        sinks=swa_sinks[l],
        sgu=dict(g=sgu_ln_g[l].reshape(1, w), b=sgu_ln_b[l].reshape(1, w), wmix_p=wmix_p, bias_p=bias_p,
                 wmix_s=wmix_s, bias_s=bias_s),
        merge=dict(wg=w_gates, wb=w_branch[l].astype(bf16), wm=w_branch_mem[l].astype(bf16), wo=w_out[l].astype(bf16),
                   g=ln_g[l].reshape(1, d), b=ln_b[l].reshape(1, d)),
    )


def _layer(x, lw, grp, st, layer, tag, prev=()):
    nseq_total, t = grp["batch"], grp["seq"]
    n_tok = nseq_total * t
    x, xb = x
    xb3 = xb.reshape(nseq_total, t, D_MODEL)
    w_t = lw["w_proj"]
    proj = _matmul_call(xb, w_t, min(grp["proj_tm"], n_tok), 1024, "proj_" + tag, w_transposed=True, n_out=N_PROJ,
                        layer=layer)
    proj3 = proj.reshape(nseq_total, t, N_PROJ)

    lt = grp["lru"]
    y_lru, hlast, hist = _lru_call(proj3, st["hist0"], st["h0"], lw["lru"], nseq_total // lt[0], lt[0], t // lt[1],
                                   lt[1], "lru_" + tag)
    short = grp["kind"] == "s"
    gt = grp["gla"]
    if short:
        y_gla, s_out = _gla_step_call(proj3, st["gla0"], layer, [p["gla"] for p in prev], lw["gla"],
                                      nseq_total // gt[0], gt[0], t, "gla_" + tag)
    else:
        y_gla, s_out = _gla_call(proj3, st["gla0"], lw["gla"], nseq_total // gt[0], gt[0], t // gt[1], gt[1],
                                 "gla_" + tag)
    wt = grp["swa"]
    if short:
        y_swa, k_last, v_last = _swa_step_call(proj3, lw["sinks"], st["k_past"], st["v_past"], layer,
                                               [p["k_last"] for p in prev], [p["v_last"] for p in prev],
                                               grp["pos0"], nseq_total // wt[0], wt[0], t, "swa_" + tag)
    else:
        y_swa, k_last, v_last = _swa_call(proj3, lw["sinks"], st["k_past"], st["v_past"], grp["pos0"],
                                          nseq_total // wt[0], wt[0], t // wt[1], wt[1], "swa_" + tag)
    sg = lw["sgu"]
    w_all = lw["merge"]["wg"]
    sgu_out = _sgu_call(xb, w_all, layer, sg["g"], sg["b"], sg["wmix_" + grp["kind"]], sg["bias_" + grp["kind"]],
                        n_tok, grp["sgu_tiles"], grp["kind"] == "s", "sgu_" + tag)
    mt = grp["mem"]
    if short:
        y_mem = _mem_step_call(xb3, w_all, st["mk"], st["mv"], layer, nseq_total // mt[0], mt[0], t, "mem_" + tag)
    else:
        y_mem = _mem_call(xb3, w_all, layer, st["mk"], st["mv"], nseq_total // mt[0], mt[0], t // mt[1], mt[1],
                          "mem_" + tag)
    ys = tuple(y.reshape(n_tok, y.shape[-1]) for y in (y_gla, y_lru, y_swa, sgu_out[0], y_mem))
    x_new = _merge_call(ys, x, lw["ln_in"], lw["merge"], layer, min(grp["merge_tm"], n_tok), "merge_" + tag)
    return x_new, dict(gla=s_out, hlast=hlast, hist=hist, k_last=k_last, v_last=v_last,
                       vn=sgu_out[1] if len(sgu_out) > 1 else None)


_PROMPT = dict(kind="p", pos0=0, proj_tm=2048, merge_tm=512, lru=(4, 64), gla=(4, 128), swa=(4, 128), sgu_tiles=8,
               mem=(1, 1024))
_SAMPLE = dict(kind="s", pos0=PAST_LEN, proj_tm=1024, merge_tm=256, lru=(32, 8), gla=(16, 8), swa=(16, 8),
               sgu_tiles=8, mem=(16, 8))


def kernel(x_prompt, x_sample, mem_prompt, state_gla, state_lru_h, state_lru_conv, cache_swa_k, cache_swa_v,
           cache_mem_k, cache_mem_v, ln_in_g, ln_in_b, w_in, gla_wa2, gla_ba, gla_norm_g, lru_conv_w, lru_conv_b,
           lru_wr, lru_br, lru_wi, lru_bi, lru_L, swa_sinks, sgu_ln_g, sgu_ln_b, sgu_w, sgu_b, w_mem_kv, w_branch,
           w_branch_mem, w_out, ln_g, ln_b):
    bp, tp, d = x_prompt.shape
    bs, ts, _ = x_sample.shape
    w = BRANCH_W
    kvw = SWA_KV_HEADS * HEAD_DIM
    gp = dict(_PROMPT, batch=bp, seq=tp)
    gs = dict(_SAMPLE, batch=bs, seq=ts)

    xp = (x_prompt.reshape(bp * tp, d), _ln_call(x_prompt.reshape(bp * tp, d), ln_in_g, ln_in_b))
    xs = (x_sample.reshape(bs * ts, d), _ln_call(x_sample.reshape(bs * ts, d), ln_in_g, ln_in_b))
    mem2 = mem_prompt.reshape(bp * N_MEM, d)

    swa_k_t, swa_v_t, mem_k_t, mem_v_t = (jnp.transpose(c, (0, 1, 3, 4, 2))
                                          for c in (cache_swa_k, cache_swa_v, cache_mem_k, cache_mem_v))

    w_proj, w_gates = _prep_w_in(w_in)
    outs_p, outs_s, mks, mvs = [], [], [], []
    for l in range(DEPTH):
        lw = _prep_layer(l, w_proj, w_gates, gla_wa2, gla_ba, gla_norm_g, lru_conv_w, lru_conv_b, lru_wr, lru_br,
                         lru_wi, lru_bi, lru_L, swa_sinks, sgu_ln_g, sgu_ln_b, sgu_w, sgu_b, w_mem_kv, w_branch,
                         w_branch_mem, w_out, ln_g, ln_b)
        lw["ln_in"] = (ln_in_g, ln_in_b)
        mkv = _matmul_call(mem2, lw["w_mem_kv"], bp * N_MEM, 2 * MEM_W, "memkv_%d" % l)
        mk = mkv[:, :MEM_W].reshape(bp, N_MEM, MEM_W)
        mv = mkv[:, MEM_W:].reshape(bp, N_MEM, MEM_W)
        st_p = dict(hist0=jnp.zeros((bp, SUBLANE, w), f32), h0=jnp.zeros((bp, 1, w), f32),
                    gla0=jnp.zeros((bp, GLA_HEADS, GLA_DK, GLA_DV), f32),
                    k_past=jnp.zeros((bp, WINDOW, kvw), f32), v_past=jnp.zeros((bp, WINDOW, kvw), f32),
                    mk=mk, mv=mv)
        st_s = dict(hist0=jnp.pad(state_lru_conv[l], ((0, 0), (SUBLANE - (CONV_W - 1), 0), (0, 0))),
                    h0=state_lru_h[l][:, None, :], gla0=state_gla,
                    k_past=swa_k_t, v_past=swa_v_t, mk=mem_k_t, mv=mem_v_t)
        xp, op = _layer(xp, lw, gp, st_p, l, "p%d" % l)
        xs, os_ = _layer(xs, lw, gs, st_s, l, "s%d" % l, prev=outs_s if l == DEPTH - 1 else ())
        outs_p.append(op)
        outs_s.append(os_)
        mks.append(mk.reshape(bp, N_MEM, MEM_HEADS, HEAD_DIM))
        mvs.append(mv.reshape(bp, N_MEM, MEM_HEADS, HEAD_DIM))

    def stack(outs, fn):
        return jnp.stack([fn(o) for o in outs])

    def window(a):
        return a.reshape(a.shape[0], WINDOW, SWA_KV_HEADS, HEAD_DIM)

    last_s = outs_s[-1]

    def window_t(a):
        return jnp.transpose(a, (0, 1, 4, 2, 3))

    return (
        xp[0].reshape(bp, tp, d), xs[0].reshape(bs, ts, d),
        stack(outs_p, lambda o: o["gla"]), last_s["gla"],
        stack(outs_p, lambda o: o["hlast"][:, SUBLANE - 1]), stack(outs_s, lambda o: o["hlast"][:, SUBLANE - 1]),
        stack(outs_p, lambda o: o["hist"][:, SUBLANE - (CONV_W - 1):]),
        stack(outs_s, lambda o: o["hist"][:, SUBLANE - (CONV_W - 1):]),
        stack(outs_p, lambda o: window(o["k_last"])), window_t(last_s["k_last"]),
        stack(outs_p, lambda o: window(o["v_last"])), window_t(last_s["v_last"]),
        jnp.stack(mks), jnp.stack(mvs),
        stack(outs_s, lambda o: o["vn"].reshape(bs, ts, w)),
    )
```

```python
import functools
import math

import jax
import jax.numpy as jnp
import numpy as np
from jax import lax
from jax.experimental import pallas as pl
from jax.experimental.pallas import tpu as pltpu

f32 = jnp.float32
bf16 = jnp.bfloat16

D_MODEL = 1024
DEPTH = 2
PAST_LEN = 8192
BRANCH_W = 512
GLA_HEADS = 4
GLA_DK = 64
GLA_DV = 128
GLA_RANK = 16
GLA_TAU = 16.0
LRU_BLOCKS = 8
LRU_BS = 64
CONV_W = 4
LRU_C = 8.0
HEAD_DIM = 64
SWA_HEADS = 8
SWA_KV_HEADS = 2
SWA_GROUP = 4
WINDOW = 128
ROT_DIM = 16
ROPE_THETA = 500000.0
SGU_GROUPS = 4
SGU_GC = 128
SGU_CHUNK = 128
N_MEM = 256
MEM_HEADS = 4
MEM_W = 256
LN_EPS = 1e-5
RMS_EPS = 1e-6
DN_ALPHA = (2 * DEPTH) ** 0.25

LANE = 128
SUBLANE = 8

_ORIG = (("gq", 256), ("gk", 256), ("gv", 512), ("glr", 16), ("gz", 512), ("lx", 512), ("lz", 512), ("sq", 512),
         ("sk", 128), ("sv", 128), ("sz", 512), ("su", 512), ("svv", 512), ("suz", 512), ("mq", 256), ("mz", 256),
         ("gates", 5 * D_MODEL))
_ORIG_OFF = {}
_off = 0
for _n, _w in _ORIG:
    _ORIG_OFF[_n] = (_off, _w)
    _off += _w
_PROJ_RUNS = (("gq", "gv"), ("gz", "sq"), ("sz", "sz"), ("sk", "sv"), ("glr", "glr"), None, ("su", "mz"))
_SEG = {}
_off = 0
for _run in _PROJ_RUNS:
    if _run is None:
        N_PROJ = -(-_off // 1024) * 1024
        _off = N_PROJ
        continue
    _names = [n for n, _ in _ORIG]
    for _n in _names[_names.index(_run[0]):_names.index(_run[1]) + 1]:
        _w = max(_ORIG_OFF[_n][1], 128)
        assert _off % _w == 0
        _SEG[_n] = (_off, _w)
        _off += _w


def _dot(a, b):
    return jnp.dot(a.astype(bf16), b.astype(bf16), preferred_element_type=f32)


def _dot_nt(a, b):
    return lax.dot_general(a.astype(bf16), b.astype(bf16), (((1,), (1,)), ((), ())), preferred_element_type=f32)


def _dot_tn(a, b):
    return lax.dot_general(a.astype(bf16), b.astype(bf16), (((0,), (0,)), ((), ())), preferred_element_type=f32)


def _sigmoid(x):
    return 0.5 * jnp.tanh(0.5 * x) + 0.5


def _silu(x):
    return x * _sigmoid(x)


def _log_sigmoid(x):
    return jnp.minimum(x, 0.0) - jnp.log(1.0 + jnp.exp(-jnp.abs(x)))


def _layer_norm(x, g, b):
    mu = jnp.mean(x, axis=-1, keepdims=True)
    xc = x - mu
    var = jnp.mean(xc * xc, axis=-1, keepdims=True)
    return xc * lax.rsqrt(var + LN_EPS) * g + b


def _params(*sem):
    return pltpu.CompilerParams(dimension_semantics=sem)


def _ln_kernel(x_ref, g_ref, b_ref, ob_ref):
    ob_ref[...] = _layer_norm(x_ref[...], g_ref[...], b_ref[...]).astype(bf16)


def _ln_call(x, g, b, tm=2048):
    n, d = x.shape
    tm = min(tm, n)
    return pl.pallas_call(
        _ln_kernel,
        grid=(n // tm,),
        in_specs=[pl.BlockSpec((tm, d), lambda i: (i, 0)), pl.BlockSpec((1, d), lambda i: (0, 0)),
                  pl.BlockSpec((1, d), lambda i: (0, 0))],
        out_specs=pl.BlockSpec((tm, d), lambda i: (i, 0)),
        out_shape=jax.ShapeDtypeStruct((n, d), bf16),
        compiler_params=_params("parallel"),
        name="ln_in",
    )(x, g.reshape(1, d), b.reshape(1, d))


def _matmul_kernel(x_ref, w_ref, o_ref, *, w_transposed):
    o_ref[...] = (_dot_nt if w_transposed else _dot)(x_ref[...], w_ref[...])


def _matmul_call(x, w, tm, tn, name, w_transposed=False, n_out=None, layer=None):
    m, k = x.shape
    n = n_out or (w.shape[-2] if w_transposed else w.shape[1])
    if layer is not None:
        w_spec = pl.BlockSpec((None, tn, k), lambda i, j: (layer, j, 0))
    elif w_transposed:
        w_spec = pl.BlockSpec((tn, k), lambda i, j: (j, 0))
    else:
        w_spec = pl.BlockSpec((k, tn), lambda i, j: (0, j))
    return pl.pallas_call(
        functools.partial(_matmul_kernel, w_transposed=w_transposed),
        grid=(m // tm, n // tn),
        in_specs=[pl.BlockSpec((tm, k), lambda i, j: (i, 0)), w_spec],
        out_specs=pl.BlockSpec((tm, tn), lambda i, j: (i, j)),
        out_shape=jax.ShapeDtypeStruct((m, n), f32),
        compiler_params=_params("parallel", "arbitrary"),
        name=name,
    )(x, w)


def _seg_spec(name, nseq, rows):
    off, width = _SEG[name]
    assert off + width <= N_PROJ
    cb = off // width
    return pl.BlockSpec((nseq, rows, width), lambda b, c: (b, c, cb))


def _tok_spec(nseq, rows, width):
    return pl.BlockSpec((nseq, rows, width), lambda b, c: (b, c, 0))


def _w_spec(first, last, layer):
    off = _ORIG_OFF[first][0]
    rows = _ORIG_OFF[last][0] + _ORIG_OFF[last][1] - off
    assert off % 16 == 0
    return pl.BlockSpec((pl.Element(1), pl.Element(rows), pl.Element(D_MODEL)), lambda *_: (layer, off, 0))


def _seg_scratch(name, *lead):
    return pltpu.VMEM((*lead, _SEG[name][1]), f32)


def _project(x_ref, pairs):
    x2 = x_ref[...].reshape(-1, D_MODEL)
    for w_ref, seg_refs in pairs:
        p = lax.dot_general(x2, w_ref[0], (((1,), (1,)), ((), ())), preferred_element_type=f32)
        off = 0
        for s_ref in seg_refs:
            width = s_ref.shape[-1]
            s_ref[...] = p[:, off:off + width].reshape(s_ref.shape)
            off += width


def _const_spec(shape):
    nd = len(shape)
    return pl.BlockSpec(shape, lambda b, c: (0,) * nd)


def _lru_kernel(lx_ref, lz_ref, hist0_ref, h0_ref, cw_ref, cb_ref, wr_ref, br_ref, wi_ref, bi_ref, lam_ref,
                y_ref, hlast_ref, hist_out_ref, hist_ref, hc_ref, *, nseq, tc):
    c = pl.program_id(1)
    w = BRANCH_W

    @pl.when(c == 0)
    def _():
        hist_ref[...] = hist0_ref[...]
        hc_ref[...] = h0_ref[...]

    x = lx_ref[...]
    xfull = jnp.concatenate([hist_ref[...], x], axis=1)

    def tap(j):
        return cw_ref[j:j + 1, :].reshape(1, 1, w)

    y = cb_ref[...].reshape(1, 1, w) + x * tap(CONV_W - 1)
    for s in range(1, CONV_W):
        y = y + pltpu.roll(xfull, s, 1)[:, SUBLANE:, :] * tap(CONV_W - 1 - s)
    hist_ref[...] = xfull[:, tc:, :]
    hist_out_ref[...] = xfull[:, tc:, :]

    xc = y.reshape(nseq * tc, w)
    r = _sigmoid(_dot(xc, wr_ref[...]) + br_ref[...])
    i = _sigmoid(_dot(xc, wi_ref[...]) + bi_ref[...])
    log_a = (LRU_C * r) * _log_sigmoid(lam_ref[...])
    a = jnp.exp(log_a)
    u = jnp.sqrt(jnp.tanh(-log_a) * (a * a + 1.0)) * (i * xc)

    acc_a = a.reshape(nseq, tc, w)
    acc_u = u.reshape(nseq, tc, w)
    t = lax.broadcasted_iota(jnp.int32, (nseq, tc, w), 1)
    d = 1
    while d < tc:
        if d % SUBLANE:
            ok = t >= d
            a_sh = jnp.where(ok, pltpu.roll(acc_a, d, 1), 1.0)
            u_sh = jnp.where(ok, pltpu.roll(acc_u, d, 1), 0.0)
            acc_u = acc_a * u_sh + acc_u
            acc_a = acc_a * a_sh
        else:
            new_u = acc_a[:, d:, :] * acc_u[:, :tc - d, :] + acc_u[:, d:, :]
            new_a = acc_a[:, d:, :] * acc_a[:, :tc - d, :]
            acc_u = jnp.concatenate([acc_u[:, :d, :], new_u], axis=1)
            acc_a = jnp.concatenate([acc_a[:, :d, :], new_a], axis=1)
        d *= 2
    h = acc_a * hc_ref[...] + acc_u
    hc_ref[...] = h[:, tc - 1:tc, :]
    hlast_ref[...] = h[:, tc - SUBLANE:, :]
    y_ref[...] = (h * _silu(lz_ref[...])).astype(y_ref.dtype)


def _lru_call(proj, hist0, h0, lw, nb, nseq, nc, tc, name):
    w = BRANCH_W
    kern = functools.partial(_lru_kernel, nseq=nseq, tc=tc)
    return pl.pallas_call(
        kern,
        grid=(nb, nc),
        in_specs=[_seg_spec("lx", nseq, tc), _seg_spec("lz", nseq, tc),
                  pl.BlockSpec((nseq, SUBLANE, w), lambda b, c: (b, 0, 0)),
                  pl.BlockSpec((nseq, 1, w), lambda b, c: (b, 0, 0)),
                  _const_spec((CONV_W, w)), _const_spec((1, w)), _const_spec((w, w)), _const_spec((1, w)),
                  _const_spec((w, w)), _const_spec((1, w)), _const_spec((1, w))],
        out_specs=[_tok_spec(nseq, tc, w),
                   pl.BlockSpec((nseq, SUBLANE, w), lambda b, c: (b, 0, 0)),
                   pl.BlockSpec((nseq, SUBLANE, w), lambda b, c: (b, 0, 0))],
        out_shape=[jax.ShapeDtypeStruct((nb * nseq, nc * tc, w), bf16),
                   jax.ShapeDtypeStruct((nb * nseq, SUBLANE, w), f32),
                   jax.ShapeDtypeStruct((nb * nseq, SUBLANE, w), f32)],
        scratch_shapes=[pltpu.VMEM((nseq, SUBLANE, w), f32), pltpu.VMEM((nseq, 1, w), f32)],
        compiler_params=_params("parallel", "arbitrary"),
        name=name,
    )(proj, proj, hist0, h0, lw["conv_w"], lw["conv_b"], lw["wr"], lw["br"], lw["wi"], lw["bi"], lw["lam"])


def _gla_consts(c):
    t = np.arange(c)[:, None]
    u = np.arange(c)[None, :]
    blocks = [u <= t, u > t]
    masks = [t == u]
    m = 1
    while m < c:
        t0 = (t // m) * m
        odd = (t // m) % 2 == 1
        blocks.append(odd & (u >= t0) & (u <= t))
        blocks.append((~odd) & (u > t) & (u <= t0 + m - 1))
        masks.append((t // (2 * m) == u // (2 * m)) & odd & ((u // m) % 2 == 0))
        m *= 2
    return (np.concatenate(blocks, 0).astype(np.float32), np.stack(masks).astype(np.float32))


def _gla_kernel(gq_ref, gk_ref, gv_ref, gz_ref, glr_ref, s0_ref, wa_ref, ba_ref, ng_ref, d_ref, m_ref,
                y_ref, sout_ref, s_ref, att_ref, o_ref, upd_ref, *, nseq, c):
    ci = pl.program_id(1)
    nlev = int(math.log2(c))
    hk = GLA_HEADS * GLA_DK
    heads = range(GLA_HEADS)

    @pl.when(ci == 0)
    def _():
        s_ref[...] = s0_ref[...]

    def ks(h):
        return slice(h * GLA_DK, (h + 1) * GLA_DK)

    def vs(h):
        return slice(h * GLA_DV, (h + 1) * GLA_DV)

    z = _dot(glr_ref[...].reshape(nseq * c, LANE), wa_ref[...]) + ba_ref[...]
    la = _log_sigmoid(z) * (1.0 / GLA_TAU)
    hi = la.astype(bf16)
    r1 = la - hi.astype(f32)
    mid = r1.astype(bf16)
    lo = (r1 - mid.astype(f32)).astype(bf16)
    hml = jnp.concatenate([hi, mid, lo], axis=1)

    q_in, k_st, dec_rows, qf, kf = [], [], [], [], []
    for n in range(nseq):
        hml_n = hml[n * c:(n + 1) * c]

        p = jnp.dot(d_ref[0:c, :], hml_n, preferred_element_type=f32)
        b = p[:, :hk] + p[:, hk:2 * hk] + p[:, 2 * hk:]

        def decay(lev, query_side):
            m = 2 ** lev
            if m < SUBLANE:
                blk = 2 + 2 * lev + (0 if query_side else 1)
                pm = jnp.dot(d_ref[blk * c:(blk + 1) * c, :], hml_n[:, :2 * hk], preferred_element_type=f32)
                return jnp.exp(pm[:, :hk] + pm[:, hk:])
            pieces = []
            for j in range(c // m):
                rows = b[j * m:(j + 1) * m]
                if query_side and j % 2 == 1:
                    pieces.append(rows - b[j * m - 1:j * m])
                elif not query_side and j % 2 == 0:
                    pieces.append(b[(j + 1) * m - 1:(j + 1) * m] - rows)
                else:
                    pieces.append(jnp.zeros((m, hk), f32))
            return jnp.exp(jnp.concatenate(pieces, axis=0))

        q = gq_ref[n] * (GLA_DK ** -0.5)
        k = gk_ref[n]
        eb = jnp.exp(b)
        q_in.append((q * eb).astype(bf16))
        k_st.append((k * jnp.exp(b[c - 1:c] - b)).astype(bf16))
        dec_rows.append(eb[c - 1:c, :])
        qf.append([q.astype(bf16)] + [(q * decay(lev, True)).astype(bf16) for lev in range(nlev)])
        kf.append([k.astype(bf16)] + [(k * decay(lev, False)).astype(bf16) for lev in range(nlev)])

    for n in range(nseq):
        for lev in range(nlev + 1):
            for h in heads:
                att_ref[n, h, lev] = _dot_nt(qf[n][lev][:, ks(h)], kf[n][lev][:, ks(h)])

    att = m_ref[0][None, None] * att_ref[:, :, 0]
    for lev in range(1, nlev + 1):
        att = att + m_ref[lev][None, None] * att_ref[:, :, lev]
    att = att.astype(bf16)

    for n in range(nseq):
        v = gv_ref[n].astype(bf16)
        for h in heads:
            v_h = v[:, vs(h)]
            o_ref[n, h] = _dot(q_in[n][:, ks(h)], s_ref[n, h]) + _dot(att[n, h], v_h)
            upd_ref[n, h] = _dot_tn(k_st[n][:, ks(h)], v_h)

    eye = (lax.broadcasted_iota(jnp.int32, (GLA_DK, GLA_DK), 0)
           == lax.broadcasted_iota(jnp.int32, (GLA_DK, GLA_DK), 1))[None, None]
    dec = jnp.stack([jnp.stack([dec_rows[n][:, ks(h)] for h in heads]) for n in range(nseq)])
    dec_col = jnp.sum(jnp.where(eye, jnp.broadcast_to(dec, (nseq, GLA_HEADS, GLA_DK, GLA_DK)), 0.0),
                      axis=3, keepdims=True)
    s_new = s_ref[...] * dec_col + upd_ref[...]
    s_ref[...] = s_new
    sout_ref[...] = s_new
    o = o_ref[...]
    o = o * lax.rsqrt(jnp.mean(o * o, axis=-1, keepdims=True) + RMS_EPS) * ng_ref[...]
    for n in range(nseq):
        for h in heads:
            y_ref[n, :, vs(h)] = (o[n, h] * _silu(gz_ref[n, :, vs(h)])).astype(y_ref.dtype)


_GLA_SEGS = ("gq", "gk", "gv", "gz", "glr")


def _gla_call(proj, s0, gw, nb, nseq, nc, c, name):
    dstack, masks = _gla_consts(c)
    kern = functools.partial(_gla_kernel, nseq=nseq, c=c)
    hk = GLA_HEADS * GLA_DK
    st_spec = pl.BlockSpec((nseq, GLA_HEADS, GLA_DK, GLA_DV), lambda b, ci: (b, 0, 0, 0))
    return pl.pallas_call(
        kern,
        grid=(nb, nc),
        in_specs=[_seg_spec(s, nseq, c) for s in _GLA_SEGS] + [
            st_spec, _const_spec((LANE, hk)), _const_spec((1, hk)), _const_spec((1, GLA_DV)),
            _const_spec(dstack.shape), _const_spec(masks.shape)],
        out_specs=[_tok_spec(nseq, c, BRANCH_W), st_spec],
        out_shape=[jax.ShapeDtypeStruct((nb * nseq, nc * c, BRANCH_W), bf16),
                   jax.ShapeDtypeStruct((nb * nseq, GLA_HEADS, GLA_DK, GLA_DV), f32)],
        scratch_shapes=[pltpu.VMEM((nseq, GLA_HEADS, GLA_DK, GLA_DV), f32),
                        pltpu.VMEM((nseq, GLA_HEADS, masks.shape[0], c, c), f32),
                        pltpu.VMEM((nseq, GLA_HEADS, c, GLA_DV), f32),
                        pltpu.VMEM((nseq, GLA_HEADS, GLA_DK, GLA_DV), f32)],
        compiler_params=_params("parallel", "arbitrary"),
        name=name,
    )(*([proj] * len(_GLA_SEGS)), s0, gw["wa"], gw["ba"], gw["ng"], jnp.asarray(dstack, bf16), jnp.asarray(masks))


def _gla_step_consts(t, nseq):
    dstack, masks = _gla_consts(t)
    eye = np.eye(nseq, dtype=np.float32)
    dbd = np.concatenate([np.kron(eye, dstack[i * t:(i + 1) * t]) for i in range(dstack.shape[0] // t)], axis=0)
    mbd = np.stack([np.kron(eye, m) for m in masks])
    return dbd, mbd


def _gla_step_kernel(gq_ref, gk_ref, gv_ref, gz_ref, glr_ref, s0_ref, wa_ref, ba_ref, ng_ref, d_ref, m_ref,
                     *refs, nseq, t, nprev):
    prev_refs, (y_ref, sout_ref) = refs[:nprev], refs[nprev:]
    if nprev:
        for i, p_ref in enumerate(prev_refs):
            sout_ref[i] = p_ref[...]
        sout_ref = sout_ref.at[nprev]
    r = nseq * t
    nlev = int(math.log2(t))
    hk = GLA_HEADS * GLA_DK
    q = gq_ref[...].reshape(r, hk) * (GLA_DK ** -0.5)
    k = gk_ref[...].reshape(r, hk)
    v = gv_ref[...].reshape(r, BRANCH_W)
    gz = gz_ref[...].reshape(r, BRANCH_W)
    z = _dot(glr_ref[...].reshape(r, LANE), wa_ref[...]) + ba_ref[...]
    la = _log_sigmoid(z) * (1.0 / GLA_TAU)
    hi = la.astype(bf16)
    r1 = la - hi.astype(f32)
    mid = r1.astype(bf16)
    lo = (r1 - mid.astype(f32)).astype(bf16)
    hml = jnp.concatenate([hi, mid, lo], axis=1)

    def decay(blk):
        p = jnp.dot(d_ref[blk * r:(blk + 1) * r, :], hml, preferred_element_type=f32)
        return jnp.exp(p[:, :hk] + p[:, hk:2 * hk] + p[:, 2 * hk:])

    eb = decay(0)
    q_in = q * eb
    k_st = k * decay(1)
    dec3 = eb.reshape(nseq, t, hk)[:, t - 1:t, :]
    qf = [q]
    kf = [k]
    for lev in range(nlev):
        qf.append(q * decay(2 + 2 * lev))
        kf.append(k * decay(3 + 2 * lev))

    own = (lax.broadcasted_iota(jnp.int32, (r, nseq * GLA_DK), 0) // t
           == lax.broadcasted_iota(jnp.int32, (r, nseq * GLA_DK), 1) // GLA_DK)
    eye = (lax.broadcasted_iota(jnp.int32, (GLA_DK, GLA_DK), 0)
           == lax.broadcasted_iota(jnp.int32, (GLA_DK, GLA_DK), 1))[None]

    def spread(x):
        x2 = jnp.concatenate([x, x], axis=1)
        return jnp.where(own, jnp.concatenate([x2] * (nseq // 2), axis=1), 0.0)

    ys = []
    for h in range(GLA_HEADS):
        ks = slice(h * GLA_DK, (h + 1) * GLA_DK)
        vs = slice(h * GLA_DV, (h + 1) * GLA_DV)
        att = jnp.zeros((r, r), f32)
        for lev in range(nlev + 1):
            att = att + m_ref[lev] * _dot_nt(qf[lev][:, ks], kf[lev][:, ks])
        s_h = s0_ref[:, h]
        v_h = v[:, vs]
        o = _dot(spread(q_in[:, ks]), s_h.reshape(nseq * GLA_DK, GLA_DV)) + _dot(att, v_h)
        upd = _dot_tn(spread(k_st[:, ks]), v_h)
        dec_col = jnp.sum(jnp.where(eye, jnp.broadcast_to(dec3[:, :, ks], (nseq, GLA_DK, GLA_DK)), 0.0),
                          axis=2, keepdims=True)
        sout_ref[:, h] = s_h * dec_col + upd.reshape(nseq, GLA_DK, GLA_DV)
        o = o * lax.rsqrt(jnp.mean(o * o, axis=-1, keepdims=True) + RMS_EPS) * ng_ref[...]
        ys.append(o * _silu(gz[:, vs]))
    y_ref[...] = jnp.concatenate(ys, axis=1).reshape(nseq, t, BRANCH_W).astype(y_ref.dtype)


def _stacked_specs(prev, inner):
    nd = len(inner)
    one = pl.BlockSpec(inner, lambda b, c: (b,) + (0,) * (nd - 1))
    if not prev:
        return [], one, lambda nbatch: jax.ShapeDtypeStruct((nbatch,) + inner[1:], f32)
    n = len(prev) + 1
    stacked = pl.BlockSpec((n,) + inner, lambda b, c: (0, b) + (0,) * (nd - 1))
    return [one] * len(prev), stacked, lambda nbatch: jax.ShapeDtypeStruct((n, nbatch) + inner[1:], f32)


def _gla_step_call(proj, s0_all, layer, prev, gw, nb, nseq, t, name):
    dbd, mbd = _gla_step_consts(t, nseq)
    kern = functools.partial(_gla_step_kernel, nseq=nseq, t=t, nprev=len(prev))
    hk = GLA_HEADS * GLA_DK
    prev_specs, st_spec, st_shape = _stacked_specs(prev, (nseq, GLA_HEADS, GLA_DK, GLA_DV))
    s0_spec = pl.BlockSpec((None, nseq, GLA_HEADS, GLA_DK, GLA_DV), lambda b, ci: (layer, b, 0, 0, 0))
    return pl.pallas_call(
        kern,
        grid=(nb, 1),
        in_specs=[_seg_spec(s, nseq, t) for s in _GLA_SEGS] + [
            s0_spec, _const_spec((LANE, hk)), _const_spec((1, hk)), _const_spec((1, GLA_DV)),
            _const_spec(dbd.shape), _const_spec(mbd.shape)] + prev_specs,
        out_specs=[_tok_spec(nseq, t, BRANCH_W), st_spec],
        out_shape=[jax.ShapeDtypeStruct((nb * nseq, t, BRANCH_W), bf16), st_shape(nb * nseq)],
        compiler_params=_params("parallel", "arbitrary"),
        name=name,
    )(*([proj] * len(_GLA_SEGS)), s0_all, gw["wa"], gw["ba"], gw["ng"], jnp.asarray(dbd, bf16), jnp.asarray(mbd),
      *prev)


def _rope_tables(pos0, t):
    half = ROT_DIM // 2
    dim = jnp.arange(LANE) % HEAD_DIM
    inv = ROPE_THETA ** (-(dim % half).astype(f32) / half)
    ang = (pos0 + jnp.arange(t)).astype(f32)[:, None] * inv[None, :]
    cos, sin = jnp.cos(ang), jnp.sin(ang)
    first, second = (dim < half)[None, :], ((dim >= half) & (dim < ROT_DIM))[None, :]
    c_tab = jnp.where(first | second, cos, 1.0)
    sa_tab = jnp.where(first, -sin, 0.0)
    sb_tab = jnp.where(second, sin, 0.0)
    return c_tab, sa_tab, sb_tab


def _rope(x, c_tab, sa_tab, sb_tab):
    wd = x.shape[-1]
    ax = x.ndim - 1
    rep = wd // LANE
    half = ROT_DIM // 2
    if rep > 1:
        c_tab, sa_tab, sb_tab = (jnp.concatenate([tb] * rep, axis=-1) for tb in (c_tab, sa_tab, sb_tab))
    return x * c_tab + pltpu.roll(x, wd - half, ax) * sa_tab + pltpu.roll(x, half, ax) * sb_tab


def _swa_kernel(sink_ref, sq_ref, sz_ref, sk_ref, sv_ref, ct_ref, sat_ref, sbt_ref, kp_ref, vp_ref,
                y_ref, klast_ref, vlast_ref, kprev_ref, vprev_ref, sp_ref, sc_ref, o_ref, den_ref,
                *, nseq, qb, pos0):
    blk = pl.program_id(1)
    hd = HEAD_DIM

    @pl.when(blk == 0)
    def _():
        kprev_ref[...] = kp_ref[...]
        vprev_ref[...] = vp_ref[...]

    tabs = tuple(r[...][None] for r in (ct_ref, sat_ref, sbt_ref))
    q3 = _rope(sq_ref[...], *tabs) * (hd ** -0.5)
    k3 = _rope(sk_ref[...], *tabs)
    v3 = sv_ref[...]
    klast_ref[...] = k3
    vlast_ref[...] = v3

    for n in range(nseq):
        for kv in range(SWA_KV_HEADS):
            ds = slice(kv * hd, (kv + 1) * hd)
            qs = jnp.concatenate(
                [q3[n][:, (kv * SWA_GROUP + g) * hd:(kv * SWA_GROUP + g + 1) * hd] for g in range(SWA_GROUP)],
                axis=0).astype(bf16)
            sp_ref[n, kv] = _dot_nt(qs, kprev_ref[n][:, ds])
            sc_ref[n, kv] = _dot_nt(qs, k3[n][:, ds])

    mrows = SWA_GROUP * qb
    qi = lax.broadcasted_iota(jnp.int32, (mrows, WINDOW), 0) % qb
    kj = lax.broadcasted_iota(jnp.int32, (mrows, WINDOW), 1)
    past_ok = (kj >= qi) & (kj >= (WINDOW - pos0) - blk * qb)
    cur_ok = kj <= qi
    sink = jnp.stack([jnp.concatenate([jnp.full((qb, 1), sink_ref[kv * SWA_GROUP + g], f32)
                                       for g in range(SWA_GROUP)], axis=0) for kv in range(SWA_KV_HEADS)])[None]
    s_p = jnp.where(past_ok[None, None], sp_ref[...], -jnp.inf)
    s_c = jnp.where(cur_ok[None, None], sc_ref[...], -jnp.inf)
    m = jnp.maximum(jnp.max(jnp.maximum(s_p, s_c), axis=3, keepdims=True), sink)
    p_p = jnp.exp(s_p - m).astype(bf16)
    p_c = jnp.exp(s_c - m).astype(bf16)
    ones = jnp.ones((WINDOW, LANE), bf16)
    for n in range(nseq):
        vp = vprev_ref[n].astype(bf16)
        vc = v3[n].astype(bf16)
        for kv in range(SWA_KV_HEADS):
            o_ref[n, kv] = _dot(p_p[n, kv], vp) + _dot(p_c[n, kv], vc)
            den_ref[n, kv] = _dot(p_p[n, kv], ones) + _dot(p_c[n, kv], ones)
    o = o_ref[...] / (den_ref[...] + jnp.exp(sink - m))
    outs = [o[:, j // SWA_GROUP, (j % SWA_GROUP) * qb:(j % SWA_GROUP + 1) * qb,
              (j // SWA_GROUP) * hd:(j // SWA_GROUP + 1) * hd] for j in range(SWA_HEADS)]
    y_ref[...] = (jnp.concatenate(outs, axis=2) * _silu(sz_ref[...])).astype(y_ref.dtype)
    kprev_ref[...] = k3
    vprev_ref[...] = v3


_SWA_SEGS = ("sq", "sz", "sk", "sv")


def _swa_call(proj, sinks, k_past, v_past, pos0, nb, nseq, nc, qb, name):
    assert qb == WINDOW
    t_total = nc * qb
    c_tab, sa_tab, sb_tab = _rope_tables(pos0, t_total)
    kern = functools.partial(_swa_kernel, nseq=nseq, qb=qb, pos0=pos0)
    kvw = SWA_KV_HEADS * HEAD_DIM
    mrows = SWA_GROUP * qb
    tab_spec = pl.BlockSpec((qb, LANE), lambda b, c: (c, 0))
    past_spec = pl.BlockSpec((nseq, WINDOW, kvw), lambda b, c: (b, 0, 0))
    return pl.pallas_call(
        kern,
        grid=(nb, nc),
        in_specs=[pl.BlockSpec(memory_space=pltpu.SMEM)] + [_seg_spec(s, nseq, qb) for s in _SWA_SEGS] + [
            tab_spec, tab_spec, tab_spec, past_spec, past_spec],
        out_specs=[_tok_spec(nseq, qb, BRANCH_W), past_spec, past_spec],
        out_shape=[jax.ShapeDtypeStruct((nb * nseq, nc * qb, BRANCH_W), bf16),
                   jax.ShapeDtypeStruct((nb * nseq, WINDOW, kvw), f32),
                   jax.ShapeDtypeStruct((nb * nseq, WINDOW, kvw), f32)],
        scratch_shapes=[pltpu.VMEM((nseq, WINDOW, kvw), f32), pltpu.VMEM((nseq, WINDOW, kvw), f32),
                        pltpu.VMEM((nseq, SWA_KV_HEADS, mrows, WINDOW), f32),
                        pltpu.VMEM((nseq, SWA_KV_HEADS, mrows, qb), f32),
                        pltpu.VMEM((nseq, SWA_KV_HEADS, mrows, kvw), f32),
                        pltpu.VMEM((nseq, SWA_KV_HEADS, mrows, kvw), f32)],
        compiler_params=_params("parallel", "arbitrary"),
        name=name,
    )(sinks, *([proj] * len(_SWA_SEGS)), c_tab, sa_tab, sb_tab, k_past, v_past)


def _swa_step_kernel(sink_ref, sq_ref, sz_ref, sk_ref, sv_ref, ct_ref, sat_ref, sbt_ref, kp_ref, vp_ref,
                     *refs, nseq, t, pos0, nprev):
    prev_refs = refs[:2 * nprev]
    y_ref, klast_ref, vlast_ref, sp_ref, sc_ref, o_ref = refs[2 * nprev:]
    if nprev:
        for i in range(nprev):
            klast_ref[i] = prev_refs[i][...]
            vlast_ref[i] = prev_refs[nprev + i][...]
        klast_ref = klast_ref.at[nprev]
        vlast_ref = vlast_ref.at[nprev]
    hd = HEAD_DIM
    kvw = SWA_KV_HEADS * hd
    mrows = SWA_HEADS * t
    tabs = tuple(r[...][None] for r in (ct_ref, sat_ref, sbt_ref))
    q3 = _rope(sq_ref[...], *tabs) * (hd ** -0.5)
    k3 = _rope(sk_ref[...], *tabs)
    v3 = sv_ref[...]

    lane = lax.broadcasted_iota(jnp.int32, (kvw, WINDOW), 1)
    pad = jnp.zeros((WINDOW - t, kvw), f32)

    def shifted(old_t, new):
        new_t = jnp.concatenate([pad, new], axis=0).T
        out = jnp.where(lane >= WINDOW - t, new_t, pltpu.roll(old_t, WINDOW - t, 1))
        return out.reshape(SWA_KV_HEADS, hd, WINDOW)

    for n in range(nseq):
        klast_ref[n] = shifted(kp_ref[n].reshape(kvw, WINDOW), k3[n])
        vlast_ref[n] = shifted(vp_ref[n].reshape(kvw, WINDOW), v3[n])

    zero = jnp.zeros((nseq, t, hd), f32)
    pieces = []
    for j in range(SWA_HEADS):
        qj = q3[:, :, j * hd:(j + 1) * hd]
        pieces.append(jnp.concatenate([qj, zero] if j // SWA_GROUP == 0 else [zero, qj], axis=2))
    qbd = jnp.concatenate(pieces, axis=1).astype(bf16)

    for n in range(nseq):
        sp_ref[n] = _dot(qbd[n], kp_ref[n].reshape(kvw, WINDOW))
        sc_ref[n] = _dot_nt(qbd[n], k3[n])

    qi = lax.broadcasted_iota(jnp.int32, (mrows, WINDOW), 0) % t
    kj = lax.broadcasted_iota(jnp.int32, (mrows, WINDOW), 1)
    past_ok = kj >= qi
    if pos0 < WINDOW:
        past_ok = past_ok & (kj >= WINDOW - pos0)
    qi_c = lax.broadcasted_iota(jnp.int32, (mrows, t), 0) % t
    kj_c = lax.broadcasted_iota(jnp.int32, (mrows, t), 1)
    cur_ok = kj_c <= qi_c
    sink = jnp.concatenate([jnp.full((t, 1), sink_ref[j], f32) for j in range(SWA_HEADS)], axis=0)[None]
    s_p = jnp.where(past_ok[None], sp_ref[...], -jnp.inf)
    s_c = jnp.where(cur_ok[None], sc_ref[...], -jnp.inf)
    m = jnp.maximum(jnp.maximum(jnp.max(s_p, axis=2, keepdims=True), jnp.max(s_c, axis=2, keepdims=True)), sink)
    p_p = jnp.exp(s_p - m)
    p_c = jnp.exp(s_c - m)
    den = jnp.sum(p_p, axis=2, keepdims=True) + jnp.sum(p_c, axis=2, keepdims=True) + jnp.exp(sink - m)
    p_p = p_p.astype(bf16)
    p_c = p_c.astype(bf16)
    for n in range(nseq):
        o_ref[n] = _dot_nt(p_p[n], vp_ref[n].reshape(kvw, WINDOW)) + _dot(p_c[n], v3[n])
    o = o_ref[...] / den
    outs = []
    for j in range(SWA_HEADS):
        kv = j // SWA_GROUP
        outs.append(o[:, j * t:(j + 1) * t, kv * hd:(kv + 1) * hd])
    y_ref[...] = (jnp.concatenate(outs, axis=2) * _silu(sz_ref[...])).astype(y_ref.dtype)


def _swa_step_call(proj, sinks, k_past_t, v_past_t, layer, prev_k, prev_v, pos0, nb, nseq, t, name):
    c_tab, sa_tab, sb_tab = _rope_tables(pos0, t)
    kern = functools.partial(_swa_step_kernel, nseq=nseq, t=t, pos0=pos0, nprev=len(prev_k))
    kvw = SWA_KV_HEADS * HEAD_DIM
    mrows = SWA_HEADS * t
    tab_spec = pl.BlockSpec((t, LANE), lambda b, c: (0, 0))
    past_spec = pl.BlockSpec((None, nseq, SWA_KV_HEADS, HEAD_DIM, WINDOW), lambda b, c: (layer, b, 0, 0, 0))
    prev_specs, new_spec, new_shape = _stacked_specs(prev_k, (nseq, SWA_KV_HEADS, HEAD_DIM, WINDOW))
    return pl.pallas_call(
        kern,
        grid=(nb, 1),
        in_specs=[pl.BlockSpec(memory_space=pltpu.SMEM)] + [_seg_spec(s, nseq, t) for s in _SWA_SEGS] + [
            tab_spec, tab_spec, tab_spec, past_spec, past_spec] + prev_specs + prev_specs,
        out_specs=[_tok_spec(nseq, t, BRANCH_W), new_spec, new_spec],
        out_shape=[jax.ShapeDtypeStruct((nb * nseq, t, BRANCH_W), bf16), new_shape(nb * nseq),
                   new_shape(nb * nseq)],
        scratch_shapes=[pltpu.VMEM((nseq, mrows, WINDOW), f32), pltpu.VMEM((nseq, mrows, t), f32),
                        pltpu.VMEM((nseq, mrows, kvw), f32)],
        compiler_params=_params("parallel", "arbitrary"),
        name=name,
    )(sinks, *([proj] * len(_SWA_SEGS)), c_tab, sa_tab, sb_tab, k_past_t, v_past_t, *prev_k, *prev_v)


def _sgu_kernel(x_ref, wu_ref, wv_ref, wz_ref, g_ref, b_ref, wm_ref, bias_ref, y_ref, *rest, ntile, want_vn):
    vn_ref = rest[0] if want_vn else None
    su_ref, sv_ref, sz_ref = rest[-3:]
    _project(x_ref, ((wu_ref, (su_ref,)), (wv_ref, (sv_ref,)), (wz_ref, (sz_ref,))))
    vn = _layer_norm(sv_ref[...], g_ref[...], b_ref[...])
    if want_vn:
        vn_ref[...] = vn
    for r in range(ntile):
        rows = slice(r * SGU_CHUNK, (r + 1) * SGU_CHUNK)
        mixed = jnp.concatenate(
            [jnp.dot(wm_ref[g], vn[rows, g * SGU_GC:(g + 1) * SGU_GC].astype(bf16), preferred_element_type=f32)
             for g in range(SGU_GROUPS)], axis=1)
        y = su_ref[rows, :] * (mixed + bias_ref[...]) * _silu(sz_ref[rows, :])
        y_ref[rows, :] = y.astype(y_ref.dtype)


def _sgu_call(xb, w_t, layer, ln_g, ln_b, wmix, bias, n_tok, ntile, want_vn, name):
    rows = ntile * SGU_CHUNK
    w = BRANCH_W
    kern = functools.partial(_sgu_kernel, ntile=ntile, want_vn=want_vn)

    def const(shape):
        return pl.BlockSpec(shape, lambda i: (0,) * len(shape))

    out_specs = [pl.BlockSpec((rows, w), lambda i: (i, 0))]
    out_shape = [jax.ShapeDtypeStruct((n_tok, w), bf16)]
    if want_vn:
        out_specs.append(pl.BlockSpec((rows, w), lambda i: (i, 0)))
        out_shape.append(jax.ShapeDtypeStruct((n_tok, w), f32))
    return pl.pallas_call(
        kern,
        grid=(n_tok // rows,),
        in_specs=[pl.BlockSpec((rows, D_MODEL), lambda i: (i, 0)), _w_spec("su", "su", layer),
                  _w_spec("svv", "svv", layer), _w_spec("suz", "suz", layer), const((1, w)), const((1, w)),
                  const((SGU_GROUPS, SGU_CHUNK, SGU_CHUNK)), const((SGU_CHUNK, w))],
        out_specs=out_specs,
        out_shape=out_shape,
        scratch_shapes=[_seg_scratch(s, rows) for s in ("su", "svv", "suz")],
        compiler_params=_params("parallel"),
        name=name,
    )(xb, w_t, w_t, w_t, ln_g, ln_b, wmix, bias)


def _mem_kernel(x_ref, wm_ref, mk_ref, mv_ref, y_ref, mq_ref, mz_ref, s_ref, o_ref, den_ref, *, nseq, tq):
    _project(x_ref, ((wm_ref, (mq_ref, mz_ref)),))
    heads = range(MEM_HEADS)
    lane_head = lax.broadcasted_iota(jnp.int32, (tq, MEM_W), 1) // HEAD_DIM
    for n in range(nseq):
        q = mq_ref[n] * (HEAD_DIM ** -0.5)
        mk = mk_ref[n].astype(bf16)
        for h in heads:
            s_ref[n, h] = _dot_nt(jnp.where(lane_head == h, q, 0.0), mk)
    s = s_ref[...]
    p = jnp.exp(s - jnp.max(s, axis=3, keepdims=True)).astype(bf16)
    ones = jnp.ones((N_MEM, MEM_W), bf16)
    for n in range(nseq):
        mv = mv_ref[n].astype(bf16)
        for h in heads:
            o_ref[n, h] = _dot(p[n, h], mv)
            den_ref[n, h] = _dot(p[n, h], ones)
    o = o_ref[...] / den_ref[...]
    acc = jnp.where(lane_head == 0, o[:, 0], 0.0)
    for h in range(1, MEM_HEADS):
        acc = acc + jnp.where(lane_head == h, o[:, h], 0.0)
    y_ref[...] = (acc * _silu(mz_ref[...])).astype(y_ref.dtype)


def _mem_call(xb, w_t, layer, mk, mv, nb, nseq, nc, tq, name):
    kern = functools.partial(_mem_kernel, nseq=nseq, tq=tq)
    kv_spec = pl.BlockSpec((nseq, N_MEM, MEM_W), lambda b, c: (b, 0, 0))
    return pl.pallas_call(
        kern,
        grid=(nb, nc),
        in_specs=[_tok_spec(nseq, tq, D_MODEL), _w_spec("mq", "mz", layer), kv_spec, kv_spec],
        out_specs=_tok_spec(nseq, tq, MEM_W),
        out_shape=jax.ShapeDtypeStruct((nb * nseq, nc * tq, MEM_W), bf16),
        scratch_shapes=[_seg_scratch("mq", nseq, tq), _seg_scratch("mz", nseq, tq),
                        pltpu.VMEM((nseq, MEM_HEADS, tq, N_MEM), f32), pltpu.VMEM((nseq, MEM_HEADS, tq, MEM_W), f32),
                        pltpu.VMEM((nseq, MEM_HEADS, tq, MEM_W), f32)],
        compiler_params=_params("parallel", "arbitrary"),
        name=name,
    )(xb, w_t, mk, mv)


def _mem_step_kernel(x_ref, wm_ref, mk_ref, mv_ref, y_ref, s_ref, o_ref, mq_ref, mz_ref, *, nseq, t):
    mrows = MEM_HEADS * t
    _project(x_ref, ((wm_ref, (mq_ref, mz_ref)),))
    row_head = lax.broadcasted_iota(jnp.int32, (mrows, MEM_W), 0) // t
    lane_head = lax.broadcasted_iota(jnp.int32, (mrows, MEM_W), 1) // HEAD_DIM
    own = (row_head == lane_head)[None]
    q3 = mq_ref[...] * (HEAD_DIM ** -0.5)
    qbd = jnp.where(own, jnp.concatenate([q3] * MEM_HEADS, axis=1), 0.0).astype(bf16)
    for n in range(nseq):
        s_ref[n] = _dot(qbd[n], mk_ref[n].reshape(MEM_W, N_MEM))
    s = s_ref[...]
    p = jnp.exp(s - jnp.max(s, axis=2, keepdims=True))
    den = jnp.sum(p, axis=2, keepdims=True)
    p = p.astype(bf16)
    for n in range(nseq):
        o_ref[n] = _dot_nt(p[n], mv_ref[n].reshape(MEM_W, N_MEM))
    o = jnp.where(own, o_ref[...] / den, 0.0)
    acc = o[:, 0:t, :]
    for h in range(1, MEM_HEADS):
        acc = acc + o[:, h * t:(h + 1) * t, :]
    y_ref[...] = (acc * _silu(mz_ref[...])).astype(y_ref.dtype)


def _mem_step_call(xb, w_t, mk_t, mv_t, layer, nb, nseq, t, name):
    kern = functools.partial(_mem_step_kernel, nseq=nseq, t=t)
    kv_spec = pl.BlockSpec((None, nseq, MEM_HEADS, HEAD_DIM, N_MEM), lambda b, c: (layer, b, 0, 0, 0))
    mrows = MEM_HEADS * t
    return pl.pallas_call(
        kern,
        grid=(nb, 1),
        in_specs=[_tok_spec(nseq, t, D_MODEL), _w_spec("mq", "mz", layer), kv_spec, kv_spec],
        out_specs=_tok_spec(nseq, t, MEM_W),
        out_shape=jax.ShapeDtypeStruct((nb * nseq, t, MEM_W), bf16),
        scratch_shapes=[pltpu.VMEM((nseq, mrows, N_MEM), f32), pltpu.VMEM((nseq, mrows, MEM_W), f32),
                        _seg_scratch("mq", nseq, t), _seg_scratch("mz", nseq, t)],
        compiler_params=_params("parallel", "arbitrary"),
        name=name,
    )(xb, w_t, mk_t, mv_t)


def _merge_kernel(yg_ref, yl_ref, ys_ref, yu_ref, ym_ref, x_ref, ig_ref, ib_ref, wg_ref, wb_ref, wm_ref, wo_ref,
                  g_ref, b_ref, o_ref, ob_ref, *, pre_ln):
    d = D_MODEL
    x = x_ref[...]
    if pre_ln:
        x = _layer_norm(x, ig_ref[...], ib_ref[...])
    xb = x.astype(bf16)

    def gate(n):
        return _sigmoid(_dot_nt(xb, wg_ref[0, n * d:(n + 1) * d, :]))

    merged = gate(4) * jnp.dot(ym_ref[...], wm_ref[...], preferred_element_type=f32)
    for n, y_ref in enumerate((yg_ref, yl_ref, ys_ref, yu_ref)):
        merged = merged + gate(n) * jnp.dot(y_ref[...], wb_ref[n], preferred_element_type=f32)
    out = _dot(merged, wo_ref[...])
    y = _layer_norm(DN_ALPHA * x + out, g_ref[...], b_ref[...])
    o_ref[...] = y
    ob_ref[...] = y.astype(bf16)


def _merge_call(ys, x, ln_in, mw, layer, tm, name):
    n_tok, d = x.shape
    w = BRANCH_W

    def rows(width):
        return pl.BlockSpec((tm, width), lambda i: (i, 0))

    def const(shape):
        return pl.BlockSpec(shape, lambda i: (0,) * len(shape), pipeline_mode=pl.Buffered(1))

    gates_spec = pl.BlockSpec((pl.Element(1), pl.Element(5 * d), pl.Element(d)),
                              lambda i: (layer, _ORIG_OFF["gates"][0], 0), pipeline_mode=pl.Buffered(1))
    return pl.pallas_call(
        functools.partial(_merge_kernel, pre_ln=layer == 0),
        grid=(n_tok // tm,),
        in_specs=[rows(w), rows(w), rows(w), rows(w), rows(MEM_W), rows(d), const((1, d)), const((1, d)),
                  gates_spec, const((4, w, d)), const((MEM_W, d)), const((d, d)), const((1, d)),
                  const((1, d))],
        out_specs=[rows(d), rows(d)],
        out_shape=[jax.ShapeDtypeStruct((n_tok, d), f32), jax.ShapeDtypeStruct((n_tok, d), bf16)],
        compiler_params=_params("parallel"),
        name=name,
    )(*ys, x, ln_in[0].reshape(1, d), ln_in[1].reshape(1, d), mw["wg"], mw["wb"], mw["wm"], mw["wo"], mw["g"],
      mw["b"])


def _prep_w_in(w_in):
    w_t = jnp.swapaxes(w_in, 1, 2).astype(bf16)
    runs = []
    used = 0
    for run in _PROJ_RUNS:
        if run is None:
            runs.append(jnp.zeros((DEPTH, N_PROJ - used, D_MODEL), bf16))
            break
        lo = _ORIG_OFF[run[0]][0]
        hi = _ORIG_OFF[run[1]][0] + _ORIG_OFF[run[1]][1]
        runs.append(w_t[:, lo:hi])
        used += hi - lo
    return jnp.concatenate(runs, axis=1), w_t


def _prep_layer(l, w_proj, w_gates, gla_wa2, gla_ba, gla_norm_g, lru_conv_w, lru_conv_b, lru_wr, lru_br, lru_wi,
                lru_bi, lru_L, swa_sinks, sgu_ln_g, sgu_ln_b, sgu_w, sgu_b, w_mem_kv, w_branch, w_branch_mem, w_out,
                ln_g, ln_b):
    d = D_MODEL
    w = BRANCH_W

    def block_diag(wb):
        eye = jnp.eye(LRU_BLOCKS, dtype=f32)
        return (eye[:, None, :, None] * wb[:, :, None, :]).reshape(w, w).astype(bf16)

    tril = jnp.tril(jnp.ones((SGU_CHUNK, SGU_CHUNK), f32))
    wmix_p = (sgu_w[l] * tril).astype(bf16)
    bias_p = jnp.repeat(sgu_b[l].T, SGU_GC, axis=1)
    t8 = SUBLANE
    rep = SGU_CHUNK // t8
    w8 = (sgu_w[l] * tril)[:, :t8, :t8]
    seq_eye = jnp.eye(rep, dtype=f32)
    wmix_s = (seq_eye[None, :, None, :, None] * w8[:, None, :, None, :]).reshape(
        SGU_GROUPS, SGU_CHUNK, SGU_CHUNK).astype(bf16)
    bias_s = jnp.tile(bias_p[:t8], (rep, 1))
    return dict(
        w_proj=w_proj,
        w_mem_kv=w_mem_kv[l].astype(bf16),
        gla=dict(wa=jnp.pad(gla_wa2[l], ((0, LANE - GLA_RANK), (0, 0))).astype(bf16),
                 ba=gla_ba[l].reshape(1, -1), ng=gla_norm_g[l].reshape(1, -1)),
        lru=dict(conv_w=lru_conv_w[l], conv_b=lru_conv_b[l].reshape(1, w), wr=block_diag(lru_wr[l]),
                 br=lru_br[l].reshape(1, w), wi=block_diag(lru_wi[l]), bi=lru_bi[l].reshape(1, w),
                 lam=lru_L[l].reshape(1, w)),
        sinks=swa_sinks[l],
        sgu=dict(g=sgu_ln_g[l].reshape(1, w), b=sgu_ln_b[l].reshape(1, w), wmix_p=wmix_p, bias_p=bias_p,
                 wmix_s=wmix_s, bias_s=bias_s),
        merge=dict(wg=w_gates, wb=w_branch[l].astype(bf16), wm=w_branch_mem[l].astype(bf16), wo=w_out[l].astype(bf16),
                   g=ln_g[l].reshape(1, d), b=ln_b[l].reshape(1, d)),
    )


def _layer(x, lw, grp, st, layer, tag, prev=()):
    nseq_total, t = grp["batch"], grp["seq"]
    n_tok = nseq_total * t
    x, xb = x
    xb3 = xb.reshape(nseq_total, t, D_MODEL)
    w_t = lw["w_proj"]
    proj = _matmul_call(xb, w_t, min(grp["proj_tm"], n_tok), 1024, "proj_" + tag, w_transposed=True, n_out=N_PROJ,
                        layer=layer)
    proj3 = proj.reshape(nseq_total, t, N_PROJ)

    lt = grp["lru"]
    y_lru, hlast, hist = _lru_call(proj3, st["hist0"], st["h0"], lw["lru"], nseq_total // lt[0], lt[0], t // lt[1],
                                   lt[1], "lru_" + tag)
    short = grp["kind"] == "s"
    gt = grp["gla"]
    if short:
        y_gla, s_out = _gla_step_call(proj3, st["gla0"], layer, [p["gla"] for p in prev], lw["gla"],
                                      nseq_total // gt[0], gt[0], t, "gla_" + tag)
    else:
        y_gla, s_out = _gla_call(proj3, st["gla0"], lw["gla"], nseq_total // gt[0], gt[0], t // gt[1], gt[1],
                                 "gla_" + tag)
    wt = grp["swa"]
    if short:
        y_swa, k_last, v_last = _swa_step_call(proj3, lw["sinks"], st["k_past"], st["v_past"], layer,
                                               [p["k_last"] for p in prev], [p["v_last"] for p in prev],
                                               grp["pos0"], nseq_total // wt[0], wt[0], t, "swa_" + tag)
    else:
        y_swa, k_last, v_last = _swa_call(proj3, lw["sinks"], st["k_past"], st["v_past"], grp["pos0"],
                                          nseq_total // wt[0], wt[0], t // wt[1], wt[1], "swa_" + tag)
    sg = lw["sgu"]
    w_all = lw["merge"]["wg"]
    sgu_out = _sgu_call(xb, w_all, layer, sg["g"], sg["b"], sg["wmix_" + grp["kind"]], sg["bias_" + grp["kind"]],
                        n_tok, grp["sgu_tiles"], grp["kind"] == "s", "sgu_" + tag)
    mt = grp["mem"]
    if short:
        y_mem = _mem_step_call(xb3, w_all, st["mk"], st["mv"], layer, nseq_total // mt[0], mt[0], t, "mem_" + tag)
    else:
        y_mem = _mem_call(xb3, w_all, layer, st["mk"], st["mv"], nseq_total // mt[0], mt[0], t // mt[1], mt[1],
                          "mem_" + tag)
    ys = tuple(y.reshape(n_tok, y.shape[-1]) for y in (y_gla, y_lru, y_swa, sgu_out[0], y_mem))
    x_new = _merge_call(ys, x, lw["ln_in"], lw["merge"], layer, min(grp["merge_tm"], n_tok), "merge_" + tag)
    return x_new, dict(gla=s_out, hlast=hlast, hist=hist, k_last=k_last, v_last=v_last,
                       vn=sgu_out[1] if len(sgu_out) > 1 else None)


_PROMPT = dict(kind="p", pos0=0, proj_tm=2048, merge_tm=512, lru=(4, 128), gla=(4, 128), swa=(4, 128), sgu_tiles=8,
               mem=(1, 1024))
_SAMPLE = dict(kind="s", pos0=PAST_LEN, proj_tm=1024, merge_tm=256, lru=(32, 8), gla=(16, 8), swa=(16, 8),
               sgu_tiles=8, mem=(16, 8))


def kernel(x_prompt, x_sample, mem_prompt, state_gla, state_lru_h, state_lru_conv, cache_swa_k, cache_swa_v,
           cache_mem_k, cache_mem_v, ln_in_g, ln_in_b, w_in, gla_wa2, gla_ba, gla_norm_g, lru_conv_w, lru_conv_b,
           lru_wr, lru_br, lru_wi, lru_bi, lru_L, swa_sinks, sgu_ln_g, sgu_ln_b, sgu_w, sgu_b, w_mem_kv, w_branch,
           w_branch_mem, w_out, ln_g, ln_b):
    bp, tp, d = x_prompt.shape
    bs, ts, _ = x_sample.shape
    w = BRANCH_W
    kvw = SWA_KV_HEADS * HEAD_DIM
    gp = dict(_PROMPT, batch=bp, seq=tp)
    gs = dict(_SAMPLE, batch=bs, seq=ts)

    xp = (x_prompt.reshape(bp * tp, d), _ln_call(x_prompt.reshape(bp * tp, d), ln_in_g, ln_in_b))
    xs = (x_sample.reshape(bs * ts, d), _ln_call(x_sample.reshape(bs * ts, d), ln_in_g, ln_in_b))
    mem2 = mem_prompt.reshape(bp * N_MEM, d)

    swa_k_t, swa_v_t, mem_k_t, mem_v_t = (jnp.transpose(c, (0, 1, 3, 4, 2))
                                          for c in (cache_swa_k, cache_swa_v, cache_mem_k, cache_mem_v))

    w_proj, w_gates = _prep_w_in(w_in)
    outs_p, outs_s, mks, mvs = [], [], [], []
    for l in range(DEPTH):
        lw = _prep_layer(l, w_proj, w_gates, gla_wa2, gla_ba, gla_norm_g, lru_conv_w, lru_conv_b, lru_wr, lru_br,
                         lru_wi, lru_bi, lru_L, swa_sinks, sgu_ln_g, sgu_ln_b, sgu_w, sgu_b, w_mem_kv, w_branch,
                         w_branch_mem, w_out, ln_g, ln_b)
        lw["ln_in"] = (ln_in_g, ln_in_b)
        mkv = _matmul_call(mem2, lw["w_mem_kv"], bp * N_MEM, 2 * MEM_W, "memkv_%d" % l)
        mk = mkv[:, :MEM_W].reshape(bp, N_MEM, MEM_W)
        mv = mkv[:, MEM_W:].reshape(bp, N_MEM, MEM_W)
        st_p = dict(hist0=jnp.zeros((bp, SUBLANE, w), f32), h0=jnp.zeros((bp, 1, w), f32),
                    gla0=jnp.zeros((bp, GLA_HEADS, GLA_DK, GLA_DV), f32),
                    k_past=jnp.zeros((bp, WINDOW, kvw), f32), v_past=jnp.zeros((bp, WINDOW, kvw), f32),
                    mk=mk, mv=mv)
        st_s = dict(hist0=jnp.pad(state_lru_conv[l], ((0, 0), (SUBLANE - (CONV_W - 1), 0), (0, 0))),
                    h0=state_lru_h[l][:, None, :], gla0=state_gla,
                    k_past=swa_k_t, v_past=swa_v_t, mk=mem_k_t, mv=mem_v_t)
        xp, op = _layer(xp, lw, gp, st_p, l, "p%d" % l)
        xs, os_ = _layer(xs, lw, gs, st_s, l, "s%d" % l, prev=outs_s if l == DEPTH - 1 else ())
        outs_p.append(op)
        outs_s.append(os_)
        mks.append(mk.reshape(bp, N_MEM, MEM_HEADS, HEAD_DIM))
        mvs.append(mv.reshape(bp, N_MEM, MEM_HEADS, HEAD_DIM))

    def stack(outs, fn):
        return jnp.stack([fn(o) for o in outs])

    def window(a):
        return a.reshape(a.shape[0], WINDOW, SWA_KV_HEADS, HEAD_DIM)

    last_s = outs_s[-1]

    def window_t(a):
        return jnp.transpose(a, (0, 1, 4, 2, 3))

    return (
        xp[0].reshape(bp, tp, d), xs[0].reshape(bs, ts, d),
        stack(outs_p, lambda o: o["gla"]), last_s["gla"],
        stack(outs_p, lambda o: o["hlast"][:, SUBLANE - 1]), stack(outs_s, lambda o: o["hlast"][:, SUBLANE - 1]),
        stack(outs_p, lambda o: o["hist"][:, SUBLANE - (CONV_W - 1):]),
        stack(outs_s, lambda o: o["hist"][:, SUBLANE - (CONV_W - 1):]),
        stack(outs_p, lambda o: window(o["k_last"])), window_t(last_s["k_last"]),
        stack(outs_p, lambda o: window(o["v_last"])), window_t(last_s["v_last"]),
        jnp.stack(mks), jnp.stack(mvs),
        stack(outs_s, lambda o: o["vn"].reshape(bs, ts, w)),
    )
```

```python
import functools
import math

import jax
import jax.numpy as jnp
import numpy as np
from jax import lax
from jax.experimental import pallas as pl
from jax.experimental.pallas import tpu as pltpu

f32 = jnp.float32
bf16 = jnp.bfloat16

D_MODEL = 1024
DEPTH = 2
PAST_LEN = 8192
BRANCH_W = 512
GLA_HEADS = 4
GLA_DK = 64
GLA_DV = 128
GLA_RANK = 16
GLA_TAU = 16.0
LRU_BLOCKS = 8
LRU_BS = 64
CONV_W = 4
LRU_C = 8.0
HEAD_DIM = 64
SWA_HEADS = 8
SWA_KV_HEADS = 2
SWA_GROUP = 4
WINDOW = 128
ROT_DIM = 16
ROPE_THETA = 500000.0
SGU_GROUPS = 4
SGU_GC = 128
SGU_CHUNK = 128
N_MEM = 256
MEM_HEADS = 4
MEM_W = 256
LN_EPS = 1e-5
RMS_EPS = 1e-6
DN_ALPHA = (2 * DEPTH) ** 0.25

LANE = 128
SUBLANE = 8

_ORIG = (("gq", 256), ("gk", 256), ("gv", 512), ("glr", 16), ("gz", 512), ("lx", 512), ("lz", 512), ("sq", 512),
         ("sk", 128), ("sv", 128), ("sz", 512), ("su", 512), ("svv", 512), ("suz", 512), ("mq", 256), ("mz", 256),
         ("gates", 5 * D_MODEL))
_ORIG_OFF = {}
_off = 0
for _n, _w in _ORIG:
    _ORIG_OFF[_n] = (_off, _w)
    _off += _w
_PROJ_RUNS = (("gq", "gv"), ("gz", "sq"), ("sz", "sz"), ("sk", "sv"), ("glr", "glr"), None, ("su", "mz"))
_SEG = {}
_off = 0
for _run in _PROJ_RUNS:
    if _run is None:
        N_PROJ = -(-_off // 1024) * 1024
        _off = N_PROJ
        continue
    _names = [n for n, _ in _ORIG]
    for _n in _names[_names.index(_run[0]):_names.index(_run[1]) + 1]:
        _w = max(_ORIG_OFF[_n][1], 128)
        assert _off % _w == 0
        _SEG[_n] = (_off, _w)
        _off += _w


def _dot(a, b):
    return jnp.dot(a.astype(bf16), b.astype(bf16), preferred_element_type=f32)


def _dot_nt(a, b):
    return lax.dot_general(a.astype(bf16), b.astype(bf16), (((1,), (1,)), ((), ())), preferred_element_type=f32)


def _dot_tn(a, b):
    return lax.dot_general(a.astype(bf16), b.astype(bf16), (((0,), (0,)), ((), ())), preferred_element_type=f32)


def _sigmoid(x):
    return 0.5 * jnp.tanh(0.5 * x) + 0.5


def _silu(x):
    return x * _sigmoid(x)


def _log_sigmoid(x):
    return jnp.minimum(x, 0.0) - jnp.log(1.0 + jnp.exp(-jnp.abs(x)))


def _layer_norm(x, g, b):
    mu = jnp.mean(x, axis=-1, keepdims=True)
    xc = x - mu
    var = jnp.mean(xc * xc, axis=-1, keepdims=True)
    return xc * lax.rsqrt(var + LN_EPS) * g + b


def _params(*sem):
    return pltpu.CompilerParams(dimension_semantics=sem)


def _ln_kernel(x_ref, g_ref, b_ref, ob_ref):
    ob_ref[...] = _layer_norm(x_ref[...], g_ref[...], b_ref[...]).astype(bf16)


def _ln_call(x, g, b, tm=2048):
    n, d = x.shape
    tm = min(tm, n)
    return pl.pallas_call(
        _ln_kernel,
        grid=(n // tm,),
        in_specs=[pl.BlockSpec((tm, d), lambda i: (i, 0)), pl.BlockSpec((1, d), lambda i: (0, 0)),
                  pl.BlockSpec((1, d), lambda i: (0, 0))],
        out_specs=pl.BlockSpec((tm, d), lambda i: (i, 0)),
        out_shape=jax.ShapeDtypeStruct((n, d), bf16),
        compiler_params=_params("parallel"),
        name="ln_in",
    )(x, g.reshape(1, d), b.reshape(1, d))


def _matmul_kernel(x_ref, w_ref, o_ref, *, w_transposed):
    o_ref[...] = (_dot_nt if w_transposed else _dot)(x_ref[...], w_ref[...])


def _matmul_call(x, w, tm, tn, name, w_transposed=False, n_out=None, layer=None):
    m, k = x.shape
    n = n_out or (w.shape[-2] if w_transposed else w.shape[1])
    if layer is not None:
        w_spec = pl.BlockSpec((None, tn, k), lambda i, j: (layer, j, 0))
    elif w_transposed:
        w_spec = pl.BlockSpec((tn, k), lambda i, j: (j, 0))
    else:
        w_spec = pl.BlockSpec((k, tn), lambda i, j: (0, j))
    return pl.pallas_call(
        functools.partial(_matmul_kernel, w_transposed=w_transposed),
        grid=(m // tm, n // tn),
        in_specs=[pl.BlockSpec((tm, k), lambda i, j: (i, 0)), w_spec],
        out_specs=pl.BlockSpec((tm, tn), lambda i, j: (i, j)),
        out_shape=jax.ShapeDtypeStruct((m, n), f32),
        compiler_params=_params("parallel", "arbitrary"),
        name=name,
    )(x, w)


def _seg_spec(name, nseq, rows):
    off, width = _SEG[name]
    assert off + width <= N_PROJ
    cb = off // width
    return pl.BlockSpec((nseq, rows, width), lambda b, c: (b, c, cb))


def _tok_spec(nseq, rows, width):
    return pl.BlockSpec((nseq, rows, width), lambda b, c: (b, c, 0))


def _w_spec(first, last, layer):
    off = _ORIG_OFF[first][0]
    rows = _ORIG_OFF[last][0] + _ORIG_OFF[last][1] - off
    assert off % 16 == 0
    return pl.BlockSpec((pl.Element(1), pl.Element(rows), pl.Element(D_MODEL)), lambda *_: (layer, off, 0))


def _seg_scratch(name, *lead):
    return pltpu.VMEM((*lead, _SEG[name][1]), f32)


def _project(x_ref, pairs):
    x2 = x_ref[...].reshape(-1, D_MODEL)
    for w_ref, seg_refs in pairs:
        p = lax.dot_general(x2, w_ref[0], (((1,), (1,)), ((), ())), preferred_element_type=f32)
        off = 0
        for s_ref in seg_refs:
            width = s_ref.shape[-1]
            s_ref[...] = p[:, off:off + width].reshape(s_ref.shape)
            off += width


def _const_spec(shape):
    nd = len(shape)
    return pl.BlockSpec(shape, lambda b, c: (0,) * nd)


def _lru_kernel(lx_ref, lz_ref, hist0_ref, h0_ref, cw_ref, cb_ref, wr_ref, br_ref, wi_ref, bi_ref, lam_ref,
                y_ref, hlast_ref, hist_out_ref, hist_ref, hc_ref, *, nseq, tc):
    c = pl.program_id(1)
    w = BRANCH_W

    @pl.when(c == 0)
    def _():
        hist_ref[...] = hist0_ref[...]
        hc_ref[...] = h0_ref[...]

    x = lx_ref[...]
    xfull = jnp.concatenate([hist_ref[...], x], axis=1)

    def tap(j):
        return cw_ref[j:j + 1, :].reshape(1, 1, w)

    y = cb_ref[...].reshape(1, 1, w) + x * tap(CONV_W - 1)
    for s in range(1, CONV_W):
        y = y + pltpu.roll(xfull, s, 1)[:, SUBLANE:, :] * tap(CONV_W - 1 - s)
    hist_ref[...] = xfull[:, tc:, :]
    hist_out_ref[...] = xfull[:, tc:, :]

    xc = y.reshape(nseq * tc, w)
    r = _sigmoid(_dot(xc, wr_ref[...]) + br_ref[...])
    i = _sigmoid(_dot(xc, wi_ref[...]) + bi_ref[...])
    log_a = (LRU_C * r) * _log_sigmoid(lam_ref[...])
    a = jnp.exp(log_a)
    u = jnp.sqrt(jnp.tanh(-log_a) * (a * a + 1.0)) * (i * xc)

    acc_a = a.reshape(nseq, tc, w)
    acc_u = u.reshape(nseq, tc, w)
    t = lax.broadcasted_iota(jnp.int32, (nseq, tc, w), 1)
    d = 1
    while d < tc:
        if d % SUBLANE:
            ok = t >= d
            a_sh = jnp.where(ok, pltpu.roll(acc_a, d, 1), 1.0)
            u_sh = jnp.where(ok, pltpu.roll(acc_u, d, 1), 0.0)
            acc_u = acc_a * u_sh + acc_u
            acc_a = acc_a * a_sh
        else:
            new_u = acc_a[:, d:, :] * acc_u[:, :tc - d, :] + acc_u[:, d:, :]
            new_a = acc_a[:, d:, :] * acc_a[:, :tc - d, :]
            acc_u = jnp.concatenate([acc_u[:, :d, :], new_u], axis=1)
            acc_a = jnp.concatenate([acc_a[:, :d, :], new_a], axis=1)
        d *= 2
    h = acc_a * hc_ref[...] + acc_u
    hc_ref[...] = h[:, tc - 1:tc, :]
    hlast_ref[...] = h[:, tc - SUBLANE:, :]
    y_ref[...] = (h * _silu(lz_ref[...])).astype(y_ref.dtype)


def _lru_call(proj, hist0, h0, lw, nb, nseq, nc, tc, name):
    w = BRANCH_W
    kern = functools.partial(_lru_kernel, nseq=nseq, tc=tc)
    return pl.pallas_call(
        kern,
        grid=(nb, nc),
        in_specs=[_seg_spec("lx", nseq, tc), _seg_spec("lz", nseq, tc),
                  pl.BlockSpec((nseq, SUBLANE, w), lambda b, c: (b, 0, 0)),
                  pl.BlockSpec((nseq, 1, w), lambda b, c: (b, 0, 0)),
                  _const_spec((CONV_W, w)), _const_spec((1, w)), _const_spec((w, w)), _const_spec((1, w)),
                  _const_spec((w, w)), _const_spec((1, w)), _const_spec((1, w))],
        out_specs=[_tok_spec(nseq, tc, w),
                   pl.BlockSpec((nseq, SUBLANE, w), lambda b, c: (b, 0, 0)),
                   pl.BlockSpec((nseq, SUBLANE, w), lambda b, c: (b, 0, 0))],
        out_shape=[jax.ShapeDtypeStruct((nb * nseq, nc * tc, w), bf16),
                   jax.ShapeDtypeStruct((nb * nseq, SUBLANE, w), f32),
                   jax.ShapeDtypeStruct((nb * nseq, SUBLANE, w), f32)],
        scratch_shapes=[pltpu.VMEM((nseq, SUBLANE, w), f32), pltpu.VMEM((nseq, 1, w), f32)],
        compiler_params=_params("parallel", "arbitrary"),
        name=name,
    )(proj, proj, hist0, h0, lw["conv_w"], lw["conv_b"], lw["wr"], lw["br"], lw["wi"], lw["bi"], lw["lam"])


def _gla_consts(c):
    t = np.arange(c)[:, None]
    u = np.arange(c)[None, :]
    blocks = [u <= t, u > t]
    masks = [t == u]
    m = 1
    while m < c:
        t0 = (t // m) * m
        odd = (t // m) % 2 == 1
        blocks.append(odd & (u >= t0) & (u <= t))
        blocks.append((~odd) & (u > t) & (u <= t0 + m - 1))
        masks.append((t // (2 * m) == u // (2 * m)) & odd & ((u // m) % 2 == 0))
        m *= 2
    return (np.concatenate(blocks, 0).astype(np.float32), np.stack(masks).astype(np.float32))


def _gla_kernel(gq_ref, gk_ref, gv_ref, gz_ref, glr_ref, s0_ref, wa_ref, ba_ref, ng_ref, d_ref, m_ref,
                y_ref, sout_ref, s_ref, att_ref, o_ref, upd_ref, *, nseq, c):
    ci = pl.program_id(1)
    nlev = int(math.log2(c))
    hk = GLA_HEADS * GLA_DK
    heads = range(GLA_HEADS)

    @pl.when(ci == 0)
    def _():
        s_ref[...] = s0_ref[...]

    def ks(h):
        return slice(h * GLA_DK, (h + 1) * GLA_DK)

    def vs(h):
        return slice(h * GLA_DV, (h + 1) * GLA_DV)

    z = _dot(glr_ref[...].reshape(nseq * c, LANE), wa_ref[...]) + ba_ref[...]
    la = _log_sigmoid(z) * (1.0 / GLA_TAU)
    hi = la.astype(bf16)
    r1 = la - hi.astype(f32)
    mid = r1.astype(bf16)
    lo = (r1 - mid.astype(f32)).astype(bf16)
    hml = jnp.concatenate([hi, mid, lo], axis=1)

    q_in, k_st, dec_rows, qf, kf = [], [], [], [], []
    for n in range(nseq):
        hml_n = hml[n * c:(n + 1) * c]

        p = jnp.dot(d_ref[0:c, :], hml_n, preferred_element_type=f32)
        b = p[:, :hk] + p[:, hk:2 * hk] + p[:, 2 * hk:]

        def decay(lev, query_side):
            m = 2 ** lev
            if m < SUBLANE:
                blk = 2 + 2 * lev + (0 if query_side else 1)
                pm = jnp.dot(d_ref[blk * c:(blk + 1) * c, :], hml_n[:, :2 * hk], preferred_element_type=f32)
                return jnp.exp(pm[:, :hk] + pm[:, hk:])
            pieces = []
            for j in range(c // m):
                rows = b[j * m:(j + 1) * m]
                if query_side and j % 2 == 1:
                    pieces.append(rows - b[j * m - 1:j * m])
                elif not query_side and j % 2 == 0:
                    pieces.append(b[(j + 1) * m - 1:(j + 1) * m] - rows)
                else:
                    pieces.append(jnp.zeros((m, hk), f32))
            return jnp.exp(jnp.concatenate(pieces, axis=0))

        q = gq_ref[n] * (GLA_DK ** -0.5)
        k = gk_ref[n]
        eb = jnp.exp(b)
        q_in.append((q * eb).astype(bf16))
        k_st.append((k * jnp.exp(b[c - 1:c] - b)).astype(bf16))
        dec_rows.append(eb[c - 1:c, :])
        qf.append([q.astype(bf16)] + [(q * decay(lev, True)).astype(bf16) for lev in range(nlev)])
        kf.append([k.astype(bf16)] + [(k * decay(lev, False)).astype(bf16) for lev in range(nlev)])

    for n in range(nseq):
        for lev in range(nlev + 1):
            for h in heads:
                att_ref[n, h, lev] = _dot_nt(qf[n][lev][:, ks(h)], kf[n][lev][:, ks(h)])

    att = m_ref[0][None, None] * att_ref[:, :, 0]
    for lev in range(1, nlev + 1):
        att = att + m_ref[lev][None, None] * att_ref[:, :, lev]
    att = att.astype(bf16)

    for n in range(nseq):
        v = gv_ref[n].astype(bf16)
        for h in heads:
            v_h = v[:, vs(h)]
            o_ref[n, h] = _dot(q_in[n][:, ks(h)], s_ref[n, h]) + _dot(att[n, h], v_h)
            upd_ref[n, h] = _dot_tn(k_st[n][:, ks(h)], v_h)

    eye = (lax.broadcasted_iota(jnp.int32, (GLA_DK, GLA_DK), 0)
           == lax.broadcasted_iota(jnp.int32, (GLA_DK, GLA_DK), 1))[None, None]
    dec = jnp.stack([jnp.stack([dec_rows[n][:, ks(h)] for h in heads]) for n in range(nseq)])
    dec_col = jnp.sum(jnp.where(eye, jnp.broadcast_to(dec, (nseq, GLA_HEADS, GLA_DK, GLA_DK)), 0.0),
                      axis=3, keepdims=True)
    s_new = s_ref[...] * dec_col + upd_ref[...]
    s_ref[...] = s_new
    sout_ref[...] = s_new
    o = o_ref[...]
    o = o * lax.rsqrt(jnp.mean(o * o, axis=-1, keepdims=True) + RMS_EPS) * ng_ref[...]
    for n in range(nseq):
        for h in heads:
            y_ref[n, :, vs(h)] = (o[n, h] * _silu(gz_ref[n, :, vs(h)])).astype(y_ref.dtype)


_GLA_SEGS = ("gq", "gk", "gv", "gz", "glr")


def _gla_call(proj, s0, gw, nb, nseq, nc, c, name):
    dstack, masks = _gla_consts(c)
    kern = functools.partial(_gla_kernel, nseq=nseq, c=c)
    hk = GLA_HEADS * GLA_DK
    st_spec = pl.BlockSpec((nseq, GLA_HEADS, GLA_DK, GLA_DV), lambda b, ci: (b, 0, 0, 0))
    return pl.pallas_call(
        kern,
        grid=(nb, nc),
        in_specs=[_seg_spec(s, nseq, c) for s in _GLA_SEGS] + [
            st_spec, _const_spec((LANE, hk)), _const_spec((1, hk)), _const_spec((1, GLA_DV)),
            _const_spec(dstack.shape), _const_spec(masks.shape)],
        out_specs=[_tok_spec(nseq, c, BRANCH_W), st_spec],
        out_shape=[jax.ShapeDtypeStruct((nb * nseq, nc * c, BRANCH_W), bf16),
                   jax.ShapeDtypeStruct((nb * nseq, GLA_HEADS, GLA_DK, GLA_DV), f32)],
        scratch_shapes=[pltpu.VMEM((nseq, GLA_HEADS, GLA_DK, GLA_DV), f32),
                        pltpu.VMEM((nseq, GLA_HEADS, masks.shape[0], c, c), f32),
                        pltpu.VMEM((nseq, GLA_HEADS, c, GLA_DV), f32),
                        pltpu.VMEM((nseq, GLA_HEADS, GLA_DK, GLA_DV), f32)],
        compiler_params=_params("parallel", "arbitrary"),
        name=name,
    )(*([proj] * len(_GLA_SEGS)), s0, gw["wa"], gw["ba"], gw["ng"], jnp.asarray(dstack, bf16), jnp.asarray(masks))


def _gla_step_consts(t, nseq):
    dstack, masks = _gla_consts(t)
    eye = np.eye(nseq, dtype=np.float32)
    dbd = np.concatenate([np.kron(eye, dstack[i * t:(i + 1) * t]) for i in range(dstack.shape[0] // t)], axis=0)
    mbd = np.stack([np.kron(eye, m) for m in masks])
    return dbd, mbd


def _gla_step_kernel(gq_ref, gk_ref, gv_ref, gz_ref, glr_ref, s0_ref, wa_ref, ba_ref, ng_ref, d_ref, m_ref,
                     *refs, nseq, t, nprev):
    prev_refs, (y_ref, sout_ref) = refs[:nprev], refs[nprev:]
    if nprev:
        for i, p_ref in enumerate(prev_refs):
            sout_ref[i] = p_ref[...]
        sout_ref = sout_ref.at[nprev]
    r = nseq * t
    nlev = int(math.log2(t))
    hk = GLA_HEADS * GLA_DK
    q = gq_ref[...].reshape(r, hk) * (GLA_DK ** -0.5)
    k = gk_ref[...].reshape(r, hk)
    v = gv_ref[...].reshape(r, BRANCH_W)
    gz = gz_ref[...].reshape(r, BRANCH_W)
    z = _dot(glr_ref[...].reshape(r, LANE), wa_ref[...]) + ba_ref[...]
    la = _log_sigmoid(z) * (1.0 / GLA_TAU)
    hi = la.astype(bf16)
    r1 = la - hi.astype(f32)
    mid = r1.astype(bf16)
    lo = (r1 - mid.astype(f32)).astype(bf16)
    hml = jnp.concatenate([hi, mid, lo], axis=1)

    def decay(blk):
        p = jnp.dot(d_ref[blk * r:(blk + 1) * r, :], hml, preferred_element_type=f32)
        return jnp.exp(p[:, :hk] + p[:, hk:2 * hk] + p[:, 2 * hk:])

    eb = decay(0)
    q_in = q * eb
    k_st = k * decay(1)
    dec3 = eb.reshape(nseq, t, hk)[:, t - 1:t, :]
    qf = [q]
    kf = [k]
    for lev in range(nlev):
        qf.append(q * decay(2 + 2 * lev))
        kf.append(k * decay(3 + 2 * lev))

    own = (lax.broadcasted_iota(jnp.int32, (r, nseq * GLA_DK), 0) // t
           == lax.broadcasted_iota(jnp.int32, (r, nseq * GLA_DK), 1) // GLA_DK)
    eye = (lax.broadcasted_iota(jnp.int32, (GLA_DK, GLA_DK), 0)
           == lax.broadcasted_iota(jnp.int32, (GLA_DK, GLA_DK), 1))[None]

    def spread(x):
        x2 = jnp.concatenate([x, x], axis=1)
        return jnp.where(own, jnp.concatenate([x2] * (nseq // 2), axis=1), 0.0)

    ys = []
    for h in range(GLA_HEADS):
        ks = slice(h * GLA_DK, (h + 1) * GLA_DK)
        vs = slice(h * GLA_DV, (h + 1) * GLA_DV)
        att = jnp.zeros((r, r), f32)
        for lev in range(nlev + 1):
            att = att + m_ref[lev] * _dot_nt(qf[lev][:, ks], kf[lev][:, ks])
        s_h = s0_ref[:, h]
        v_h = v[:, vs]
        o = _dot(spread(q_in[:, ks]), s_h.reshape(nseq * GLA_DK, GLA_DV)) + _dot(att, v_h)
        upd = _dot_tn(spread(k_st[:, ks]), v_h)
        dec_col = jnp.sum(jnp.where(eye, jnp.broadcast_to(dec3[:, :, ks], (nseq, GLA_DK, GLA_DK)), 0.0),
                          axis=2, keepdims=True)
        sout_ref[:, h] = s_h * dec_col + upd.reshape(nseq, GLA_DK, GLA_DV)
        o = o * lax.rsqrt(jnp.mean(o * o, axis=-1, keepdims=True) + RMS_EPS) * ng_ref[...]
        ys.append(o * _silu(gz[:, vs]))
    y_ref[...] = jnp.concatenate(ys, axis=1).reshape(nseq, t, BRANCH_W).astype(y_ref.dtype)


def _stacked_specs(prev, inner):
    nd = len(inner)
    one = pl.BlockSpec(inner, lambda b, c: (b,) + (0,) * (nd - 1))
    if not prev:
        return [], one, lambda nbatch: jax.ShapeDtypeStruct((nbatch,) + inner[1:], f32)
    n = len(prev) + 1
    stacked = pl.BlockSpec((n,) + inner, lambda b, c: (0, b) + (0,) * (nd - 1))
    return [one] * len(prev), stacked, lambda nbatch: jax.ShapeDtypeStruct((n, nbatch) + inner[1:], f32)


def _gla_step_call(proj, s0_all, layer, prev, gw, nb, nseq, t, name):
    dbd, mbd = _gla_step_consts(t, nseq)
    kern = functools.partial(_gla_step_kernel, nseq=nseq, t=t, nprev=len(prev))
    hk = GLA_HEADS * GLA_DK
    prev_specs, st_spec, st_shape = _stacked_specs(prev, (nseq, GLA_HEADS, GLA_DK, GLA_DV))
    s0_spec = pl.BlockSpec((None, nseq, GLA_HEADS, GLA_DK, GLA_DV), lambda b, ci: (layer, b, 0, 0, 0))
    return pl.pallas_call(
        kern,
        grid=(nb, 1),
        in_specs=[_seg_spec(s, nseq, t) for s in _GLA_SEGS] + [
            s0_spec, _const_spec((LANE, hk)), _const_spec((1, hk)), _const_spec((1, GLA_DV)),
            _const_spec(dbd.shape), _const_spec(mbd.shape)] + prev_specs,
        out_specs=[_tok_spec(nseq, t, BRANCH_W), st_spec],
        out_shape=[jax.ShapeDtypeStruct((nb * nseq, t, BRANCH_W), bf16), st_shape(nb * nseq)],
        compiler_params=_params("parallel", "arbitrary"),
        name=name,
    )(*([proj] * len(_GLA_SEGS)), s0_all, gw["wa"], gw["ba"], gw["ng"], jnp.asarray(dbd, bf16), jnp.asarray(mbd),
      *prev)


def _rope_tables(pos0, t):
    half = ROT_DIM // 2
    dim = jnp.arange(LANE) % HEAD_DIM
    inv = ROPE_THETA ** (-(dim % half).astype(f32) / half)
    ang = (pos0 + jnp.arange(t)).astype(f32)[:, None] * inv[None, :]
    cos, sin = jnp.cos(ang), jnp.sin(ang)
    first, second = (dim < half)[None, :], ((dim >= half) & (dim < ROT_DIM))[None, :]
    c_tab = jnp.where(first | second, cos, 1.0)
    sa_tab = jnp.where(first, -sin, 0.0)
    sb_tab = jnp.where(second, sin, 0.0)
    return c_tab, sa_tab, sb_tab


def _rope(x, c_tab, sa_tab, sb_tab):
    wd = x.shape[-1]
    ax = x.ndim - 1
    rep = wd // LANE
    half = ROT_DIM // 2
    if rep > 1:
        c_tab, sa_tab, sb_tab = (jnp.concatenate([tb] * rep, axis=-1) for tb in (c_tab, sa_tab, sb_tab))
    return x * c_tab + pltpu.roll(x, wd - half, ax) * sa_tab + pltpu.roll(x, half, ax) * sb_tab


def _swa_kernel(sink_ref, sq_ref, sz_ref, sk_ref, sv_ref, ct_ref, sat_ref, sbt_ref, kp_ref, vp_ref,
                y_ref, klast_ref, vlast_ref, kprev_ref, vprev_ref, sp_ref, sc_ref, o_ref, den_ref,
                *, nseq, qb, pos0):
    blk = pl.program_id(1)
    hd = HEAD_DIM

    @pl.when(blk == 0)
    def _():
        kprev_ref[...] = kp_ref[...]
        vprev_ref[...] = vp_ref[...]

    tabs = tuple(r[...][None] for r in (ct_ref, sat_ref, sbt_ref))
    q3 = _rope(sq_ref[...], *tabs) * (hd ** -0.5)
    k3 = _rope(sk_ref[...], *tabs)
    v3 = sv_ref[...]
    klast_ref[...] = k3
    vlast_ref[...] = v3

    for n in range(nseq):
        for kv in range(SWA_KV_HEADS):
            ds = slice(kv * hd, (kv + 1) * hd)
            qs = jnp.concatenate(
                [q3[n][:, (kv * SWA_GROUP + g) * hd:(kv * SWA_GROUP + g + 1) * hd] for g in range(SWA_GROUP)],
                axis=0).astype(bf16)
            sp_ref[n, kv] = _dot_nt(qs, kprev_ref[n][:, ds])
            sc_ref[n, kv] = _dot_nt(qs, k3[n][:, ds])

    mrows = SWA_GROUP * qb
    qi = lax.broadcasted_iota(jnp.int32, (mrows, WINDOW), 0) % qb
    kj = lax.broadcasted_iota(jnp.int32, (mrows, WINDOW), 1)
    past_ok = (kj >= qi) & (kj >= (WINDOW - pos0) - blk * qb)
    cur_ok = kj <= qi
    sink = jnp.stack([jnp.concatenate([jnp.full((qb, 1), sink_ref[kv * SWA_GROUP + g], f32)
                                       for g in range(SWA_GROUP)], axis=0) for kv in range(SWA_KV_HEADS)])[None]
    s_p = jnp.where(past_ok[None, None], sp_ref[...], -jnp.inf)
    s_c = jnp.where(cur_ok[None, None], sc_ref[...], -jnp.inf)
    m = jnp.maximum(jnp.max(jnp.maximum(s_p, s_c), axis=3, keepdims=True), sink)
    p_p = jnp.exp(s_p - m).astype(bf16)
    p_c = jnp.exp(s_c - m).astype(bf16)
    ones = jnp.ones((WINDOW, LANE), bf16)
    for n in range(nseq):
        vp = vprev_ref[n].astype(bf16)
        vc = v3[n].astype(bf16)
        for kv in range(SWA_KV_HEADS):
            o_ref[n, kv] = _dot(p_p[n, kv], vp) + _dot(p_c[n, kv], vc)
            den_ref[n, kv] = _dot(p_p[n, kv], ones) + _dot(p_c[n, kv], ones)
    o = o_ref[...] / (den_ref[...] + jnp.exp(sink - m))
    outs = [o[:, j // SWA_GROUP, (j % SWA_GROUP) * qb:(j % SWA_GROUP + 1) * qb,
              (j // SWA_GROUP) * hd:(j // SWA_GROUP + 1) * hd] for j in range(SWA_HEADS)]
    y_ref[...] = (jnp.concatenate(outs, axis=2) * _silu(sz_ref[...])).astype(y_ref.dtype)
    kprev_ref[...] = k3
    vprev_ref[...] = v3


_SWA_SEGS = ("sq", "sz", "sk", "sv")


def _swa_call(proj, sinks, k_past, v_past, pos0, nb, nseq, nc, qb, name):
    assert qb == WINDOW
    t_total = nc * qb
    c_tab, sa_tab, sb_tab = _rope_tables(pos0, t_total)
    kern = functools.partial(_swa_kernel, nseq=nseq, qb=qb, pos0=pos0)
    kvw = SWA_KV_HEADS * HEAD_DIM
    mrows = SWA_GROUP * qb
    tab_spec = pl.BlockSpec((qb, LANE), lambda b, c: (c, 0))
    past_spec = pl.BlockSpec((nseq, WINDOW, kvw), lambda b, c: (b, 0, 0))
    return pl.pallas_call(
        kern,
        grid=(nb, nc),
        in_specs=[pl.BlockSpec(memory_space=pltpu.SMEM)] + [_seg_spec(s, nseq, qb) for s in _SWA_SEGS] + [
            tab_spec, tab_spec, tab_spec, past_spec, past_spec],
        out_specs=[_tok_spec(nseq, qb, BRANCH_W), past_spec, past_spec],
        out_shape=[jax.ShapeDtypeStruct((nb * nseq, nc * qb, BRANCH_W), bf16),
                   jax.ShapeDtypeStruct((nb * nseq, WINDOW, kvw), f32),
                   jax.ShapeDtypeStruct((nb * nseq, WINDOW, kvw), f32)],
        scratch_shapes=[pltpu.VMEM((nseq, WINDOW, kvw), f32), pltpu.VMEM((nseq, WINDOW, kvw), f32),
                        pltpu.VMEM((nseq, SWA_KV_HEADS, mrows, WINDOW), f32),
                        pltpu.VMEM((nseq, SWA_KV_HEADS, mrows, qb), f32),
                        pltpu.VMEM((nseq, SWA_KV_HEADS, mrows, kvw), f32),
                        pltpu.VMEM((nseq, SWA_KV_HEADS, mrows, kvw), f32)],
        compiler_params=_params("parallel", "arbitrary"),
        name=name,
    )(sinks, *([proj] * len(_SWA_SEGS)), c_tab, sa_tab, sb_tab, k_past, v_past)


def _swa_step_kernel(sink_ref, sq_ref, sz_ref, sk_ref, sv_ref, ct_ref, sat_ref, sbt_ref, kp_ref, vp_ref,
                     *refs, nseq, t, pos0, nprev):
    prev_refs = refs[:2 * nprev]
    y_ref, klast_ref, vlast_ref, sp_ref, sc_ref, o_ref = refs[2 * nprev:]
    if nprev:
        for i in range(nprev):
            klast_ref[i] = prev_refs[i][...]
            vlast_ref[i] = prev_refs[nprev + i][...]
        klast_ref = klast_ref.at[nprev]
        vlast_ref = vlast_ref.at[nprev]
    hd = HEAD_DIM
    kvw = SWA_KV_HEADS * hd
    mrows = SWA_HEADS * t
    tabs = tuple(r[...][None] for r in (ct_ref, sat_ref, sbt_ref))
    q3 = _rope(sq_ref[...], *tabs) * (hd ** -0.5)
    k3 = _rope(sk_ref[...], *tabs)
    v3 = sv_ref[...]

    lane = lax.broadcasted_iota(jnp.int32, (kvw, WINDOW), 1)
    pad = jnp.zeros((WINDOW - t, kvw), f32)

    def shifted(old_t, new):
        new_t = jnp.concatenate([pad, new], axis=0).T
        out = jnp.where(lane >= WINDOW - t, new_t, pltpu.roll(old_t, WINDOW - t, 1))
        return out.reshape(SWA_KV_HEADS, hd, WINDOW)

    for n in range(nseq):
        klast_ref[n] = shifted(kp_ref[n].reshape(kvw, WINDOW), k3[n])
        vlast_ref[n] = shifted(vp_ref[n].reshape(kvw, WINDOW), v3[n])

    zero = jnp.zeros((nseq, t, hd), f32)
    pieces = []
    for j in range(SWA_HEADS):
        qj = q3[:, :, j * hd:(j + 1) * hd]
        pieces.append(jnp.concatenate([qj, zero] if j // SWA_GROUP == 0 else [zero, qj], axis=2))
    qbd = jnp.concatenate(pieces, axis=1).astype(bf16)

    for n in range(nseq):
        sp_ref[n] = _dot(qbd[n], kp_ref[n].reshape(kvw, WINDOW))
        sc_ref[n] = _dot_nt(qbd[n], k3[n])

    qi = lax.broadcasted_iota(jnp.int32, (mrows, WINDOW), 0) % t
    kj = lax.broadcasted_iota(jnp.int32, (mrows, WINDOW), 1)
    past_ok = kj >= qi
    if pos0 < WINDOW:
        past_ok = past_ok & (kj >= WINDOW - pos0)
    qi_c = lax.broadcasted_iota(jnp.int32, (mrows, t), 0) % t
    kj_c = lax.broadcasted_iota(jnp.int32, (mrows, t), 1)
    cur_ok = kj_c <= qi_c
    sink = jnp.concatenate([jnp.full((t, 1), sink_ref[j], f32) for j in range(SWA_HEADS)], axis=0)[None]
    s_p = jnp.where(past_ok[None], sp_ref[...], -jnp.inf)
    s_c = jnp.where(cur_ok[None], sc_ref[...], -jnp.inf)
    m = jnp.maximum(jnp.maximum(jnp.max(s_p, axis=2, keepdims=True), jnp.max(s_c, axis=2, keepdims=True)), sink)
    p_p = jnp.exp(s_p - m)
    p_c = jnp.exp(s_c - m)
    den = jnp.sum(p_p, axis=2, keepdims=True) + jnp.sum(p_c, axis=2, keepdims=True) + jnp.exp(sink - m)
    p_p = p_p.astype(bf16)
    p_c = p_c.astype(bf16)
    for n in range(nseq):
        o_ref[n] = _dot_nt(p_p[n], vp_ref[n].reshape(kvw, WINDOW)) + _dot(p_c[n], v3[n])
    o = o_ref[...] / den
    outs = []
    for j in range(SWA_HEADS):
        kv = j // SWA_GROUP
        outs.append(o[:, j * t:(j + 1) * t, kv * hd:(kv + 1) * hd])
    y_ref[...] = (jnp.concatenate(outs, axis=2) * _silu(sz_ref[...])).astype(y_ref.dtype)


def _swa_step_call(proj, sinks, k_past_t, v_past_t, layer, prev_k, prev_v, pos0, nb, nseq, t, name):
    c_tab, sa_tab, sb_tab = _rope_tables(pos0, t)
    kern = functools.partial(_swa_step_kernel, nseq=nseq, t=t, pos0=pos0, nprev=len(prev_k))
    kvw = SWA_KV_HEADS * HEAD_DIM
    mrows = SWA_HEADS * t
    tab_spec = pl.BlockSpec((t, LANE), lambda b, c: (0, 0))
    past_spec = pl.BlockSpec((None, nseq, SWA_KV_HEADS, HEAD_DIM, WINDOW), lambda b, c: (layer, b, 0, 0, 0))
    prev_specs, new_spec, new_shape = _stacked_specs(prev_k, (nseq, SWA_KV_HEADS, HEAD_DIM, WINDOW))
    return pl.pallas_call(
        kern,
        grid=(nb, 1),
        in_specs=[pl.BlockSpec(memory_space=pltpu.SMEM)] + [_seg_spec(s, nseq, t) for s in _SWA_SEGS] + [
            tab_spec, tab_spec, tab_spec, past_spec, past_spec] + prev_specs + prev_specs,
        out_specs=[_tok_spec(nseq, t, BRANCH_W), new_spec, new_spec],
        out_shape=[jax.ShapeDtypeStruct((nb * nseq, t, BRANCH_W), bf16), new_shape(nb * nseq),
                   new_shape(nb * nseq)],
        scratch_shapes=[pltpu.VMEM((nseq, mrows, WINDOW), f32), pltpu.VMEM((nseq, mrows, t), f32),
                        pltpu.VMEM((nseq, mrows, kvw), f32)],
        compiler_params=_params("parallel", "arbitrary"),
        name=name,
    )(sinks, *([proj] * len(_SWA_SEGS)), c_tab, sa_tab, sb_tab, k_past_t, v_past_t, *prev_k, *prev_v)


def _sgu_kernel(x_ref, wu_ref, wv_ref, wz_ref, g_ref, b_ref, wm_ref, bias_ref, y_ref, *rest, ntile, want_vn):
    vn_ref = rest[0] if want_vn else None
    su_ref, sv_ref, sz_ref = rest[-3:]
    _project(x_ref, ((wu_ref, (su_ref,)), (wv_ref, (sv_ref,)), (wz_ref, (sz_ref,))))
    vn = _layer_norm(sv_ref[...], g_ref[...], b_ref[...])
    if want_vn:
        vn_ref[...] = vn
    for r in range(ntile):
        rows = slice(r * SGU_CHUNK, (r + 1) * SGU_CHUNK)
        mixed = jnp.concatenate(
            [jnp.dot(wm_ref[g], vn[rows, g * SGU_GC:(g + 1) * SGU_GC].astype(bf16), preferred_element_type=f32)
             for g in range(SGU_GROUPS)], axis=1)
        y = su_ref[rows, :] * (mixed + bias_ref[...]) * _silu(sz_ref[rows, :])
        y_ref[rows, :] = y.astype(y_ref.dtype)


def _sgu_call(xb, w_t, layer, ln_g, ln_b, wmix, bias, n_tok, ntile, want_vn, name):
    rows = ntile * SGU_CHUNK
    w = BRANCH_W
    kern = functools.partial(_sgu_kernel, ntile=ntile, want_vn=want_vn)

    def const(shape):
        return pl.BlockSpec(shape, lambda i: (0,) * len(shape))

    out_specs = [pl.BlockSpec((rows, w), lambda i: (i, 0))]
    out_shape = [jax.ShapeDtypeStruct((n_tok, w), bf16)]
    if want_vn:
        out_specs.append(pl.BlockSpec((rows, w), lambda i: (i, 0)))
        out_shape.append(jax.ShapeDtypeStruct((n_tok, w), f32))
    return pl.pallas_call(
        kern,
        grid=(n_tok // rows,),
        in_specs=[pl.BlockSpec((rows, D_MODEL), lambda i: (i, 0)), _w_spec("su", "su", layer),
                  _w_spec("svv", "svv", layer), _w_spec("suz", "suz", layer), const((1, w)), const((1, w)),
                  const((SGU_GROUPS, SGU_CHUNK, SGU_CHUNK)), const((SGU_CHUNK, w))],
        out_specs=out_specs,
        out_shape=out_shape,
        scratch_shapes=[_seg_scratch(s, rows) for s in ("su", "svv", "suz")],
        compiler_params=_params("parallel"),
        name=name,
    )(xb, w_t, w_t, w_t, ln_g, ln_b, wmix, bias)


def _mem_kernel(x_ref, wm_ref, mk_ref, mv_ref, y_ref, mq_ref, mz_ref, s_ref, o_ref, den_ref, *, nseq, tq):
    _project(x_ref, ((wm_ref, (mq_ref, mz_ref)),))
    heads = range(MEM_HEADS)
    lane_head = lax.broadcasted_iota(jnp.int32, (tq, MEM_W), 1) // HEAD_DIM
    for n in range(nseq):
        q = mq_ref[n] * (HEAD_DIM ** -0.5)
        mk = mk_ref[n].astype(bf16)
        for h in heads:
            s_ref[n, h] = _dot_nt(jnp.where(lane_head == h, q, 0.0), mk)
    s = s_ref[...]
    p = jnp.exp(s - jnp.max(s, axis=3, keepdims=True)).astype(bf16)
    ones = jnp.ones((N_MEM, MEM_W), bf16)
    for n in range(nseq):
        mv = mv_ref[n].astype(bf16)
        for h in heads:
            o_ref[n, h] = _dot(p[n, h], mv)
            den_ref[n, h] = _dot(p[n, h], ones)
    o = o_ref[...] / den_ref[...]
    acc = jnp.where(lane_head == 0, o[:, 0], 0.0)
    for h in range(1, MEM_HEADS):
        acc = acc + jnp.where(lane_head == h, o[:, h], 0.0)
    y_ref[...] = (acc * _silu(mz_ref[...])).astype(y_ref.dtype)


def _mem_call(xb, w_t, layer, mk, mv, nb, nseq, nc, tq, name):
    kern = functools.partial(_mem_kernel, nseq=nseq, tq=tq)
    kv_spec = pl.BlockSpec((nseq, N_MEM, MEM_W), lambda b, c: (b, 0, 0))
    return pl.pallas_call(
        kern,
        grid=(nb, nc),
        in_specs=[_tok_spec(nseq, tq, D_MODEL), _w_spec("mq", "mz", layer), kv_spec, kv_spec],
        out_specs=_tok_spec(nseq, tq, MEM_W),
        out_shape=jax.ShapeDtypeStruct((nb * nseq, nc * tq, MEM_W), bf16),
        scratch_shapes=[_seg_scratch("mq", nseq, tq), _seg_scratch("mz", nseq, tq),
                        pltpu.VMEM((nseq, MEM_HEADS, tq, N_MEM), f32), pltpu.VMEM((nseq, MEM_HEADS, tq, MEM_W), f32),
                        pltpu.VMEM((nseq, MEM_HEADS, tq, MEM_W), f32)],
        compiler_params=_params("parallel", "arbitrary"),
        name=name,
    )(xb, w_t, mk, mv)


def _mem_step_kernel(x_ref, wm_ref, mk_ref, mv_ref, y_ref, s_ref, o_ref, mq_ref, mz_ref, *, nseq, t):
    mrows = MEM_HEADS * t
    _project(x_ref, ((wm_ref, (mq_ref, mz_ref)),))
    row_head = lax.broadcasted_iota(jnp.int32, (mrows, MEM_W), 0) // t
    lane_head = lax.broadcasted_iota(jnp.int32, (mrows, MEM_W), 1) // HEAD_DIM
    own = (row_head == lane_head)[None]
    q3 = mq_ref[...] * (HEAD_DIM ** -0.5)
    qbd = jnp.where(own, jnp.concatenate([q3] * MEM_HEADS, axis=1), 0.0).astype(bf16)
    for n in range(nseq):
        s_ref[n] = _dot(qbd[n], mk_ref[n].reshape(MEM_W, N_MEM))
    s = s_ref[...]
    p = jnp.exp(s - jnp.max(s, axis=2, keepdims=True))
    den = jnp.sum(p, axis=2, keepdims=True)
    p = p.astype(bf16)
    for n in range(nseq):
        o_ref[n] = _dot_nt(p[n], mv_ref[n].reshape(MEM_W, N_MEM))
    o = jnp.where(own, o_ref[...] / den, 0.0)
    acc = o[:, 0:t, :]
    for h in range(1, MEM_HEADS):
        acc = acc + o[:, h * t:(h + 1) * t, :]
    y_ref[...] = (acc * _silu(mz_ref[...])).astype(y_ref.dtype)


def _mem_step_call(xb, w_t, mk_t, mv_t, layer, nb, nseq, t, name):
    kern = functools.partial(_mem_step_kernel, nseq=nseq, t=t)
    kv_spec = pl.BlockSpec((None, nseq, MEM_HEADS, HEAD_DIM, N_MEM), lambda b, c: (layer, b, 0, 0, 0))
    mrows = MEM_HEADS * t
    return pl.pallas_call(
        kern,
        grid=(nb, 1),
        in_specs=[_tok_spec(nseq, t, D_MODEL), _w_spec("mq", "mz", layer), kv_spec, kv_spec],
        out_specs=_tok_spec(nseq, t, MEM_W),
        out_shape=jax.ShapeDtypeStruct((nb * nseq, t, MEM_W), bf16),
        scratch_shapes=[pltpu.VMEM((nseq, mrows, N_MEM), f32), pltpu.VMEM((nseq, mrows, MEM_W), f32),
                        _seg_scratch("mq", nseq, t), _seg_scratch("mz", nseq, t)],
        compiler_params=_params("parallel", "arbitrary"),
        name=name,
    )(xb, w_t, mk_t, mv_t)


def _merge_kernel(yg_ref, yl_ref, ys_ref, yu_ref, ym_ref, x_ref, ig_ref, ib_ref, wg_ref, wb_ref, wm_ref, wo_ref,
                  g_ref, b_ref, o_ref, ob_ref, *, pre_ln):
    d = D_MODEL
    x = x_ref[...]
    if pre_ln:
        x = _layer_norm(x, ig_ref[...], ib_ref[...])
    xb = x.astype(bf16)

    cw = 2 * LANE

    def gate(n, c0):
        return _sigmoid(_dot_nt(xb, wg_ref[0, n * d + c0:n * d + c0 + cw, :]))

    out = None
    for c0 in range(0, d, cw):
        cols = slice(c0, c0 + cw)
        merged = gate(4, c0) * jnp.dot(ym_ref[...], wm_ref[:, cols], preferred_element_type=f32)
        for n, y_ref in enumerate((yg_ref, yl_ref, ys_ref, yu_ref)):
            merged = merged + gate(n, c0) * jnp.dot(y_ref[...], wb_ref[n, :, cols], preferred_element_type=f32)
        part = _dot(merged, wo_ref[cols, :])
        out = part if out is None else out + part
    y = _layer_norm(DN_ALPHA * x + out, g_ref[...], b_ref[...])
    o_ref[...] = y
    ob_ref[...] = y.astype(bf16)


def _merge_call(ys, x, ln_in, mw, layer, tm, name):
    n_tok, d = x.shape
    w = BRANCH_W

    def rows(width):
        return pl.BlockSpec((tm, width), lambda i: (i, 0))

    def const(shape):
        return pl.BlockSpec(shape, lambda i: (0,) * len(shape), pipeline_mode=pl.Buffered(1))

    gates_spec = pl.BlockSpec((pl.Element(1), pl.Element(5 * d), pl.Element(d)),
                              lambda i: (layer, _ORIG_OFF["gates"][0], 0), pipeline_mode=pl.Buffered(1))
    return pl.pallas_call(
        functools.partial(_merge_kernel, pre_ln=layer == 0),
        grid=(n_tok // tm,),
        in_specs=[rows(w), rows(w), rows(w), rows(w), rows(MEM_W), rows(d), const((1, d)), const((1, d)),
                  gates_spec, const((4, w, d)), const((MEM_W, d)), const((d, d)), const((1, d)),
                  const((1, d))],
        out_specs=[rows(d), rows(d)],
        out_shape=[jax.ShapeDtypeStruct((n_tok, d), f32), jax.ShapeDtypeStruct((n_tok, d), bf16)],
        compiler_params=_params("parallel"),
        name=name,
    )(*ys, x, ln_in[0].reshape(1, d), ln_in[1].reshape(1, d), mw["wg"], mw["wb"], mw["wm"], mw["wo"], mw["g"],
      mw["b"])


def _prep_w_in(w_in):
    w_t = jnp.swapaxes(w_in, 1, 2).astype(bf16)
    runs = []
    used = 0
    for run in _PROJ_RUNS:
        if run is None:
            runs.append(jnp.zeros((DEPTH, N_PROJ - used, D_MODEL), bf16))
            break
        lo = _ORIG_OFF[run[0]][0]
        hi = _ORIG_OFF[run[1]][0] + _ORIG_OFF[run[1]][1]
        runs.append(w_t[:, lo:hi])
        used += hi - lo
    return jnp.concatenate(runs, axis=1), w_t


def _prep_layer(l, w_proj, w_gates, gla_wa2, gla_ba, gla_norm_g, lru_conv_w, lru_conv_b, lru_wr, lru_br, lru_wi,
                lru_bi, lru_L, swa_sinks, sgu_ln_g, sgu_ln_b, sgu_w, sgu_b, w_mem_kv, w_branch, w_branch_mem, w_out,
                ln_g, ln_b):
    d = D_MODEL
    w = BRANCH_W

    def block_diag(wb):
        eye = jnp.eye(LRU_BLOCKS, dtype=f32)
        return (eye[:, None, :, None] * wb[:, :, None, :]).reshape(w, w).astype(bf16)

    tril = jnp.tril(jnp.ones((SGU_CHUNK, SGU_CHUNK), f32))
    wmix_p = (sgu_w[l] * tril).astype(bf16)
    bias_p = jnp.repeat(sgu_b[l].T, SGU_GC, axis=1)
    t8 = SUBLANE
    rep = SGU_CHUNK // t8
    w8 = (sgu_w[l] * tril)[:, :t8, :t8]
    seq_eye = jnp.eye(rep, dtype=f32)
    wmix_s = (seq_eye[None, :, None, :, None] * w8[:, None, :, None, :]).reshape(
        SGU_GROUPS, SGU_CHUNK, SGU_CHUNK).astype(bf16)
    bias_s = jnp.tile(bias_p[:t8], (rep, 1))
    return dict(
        w_proj=w_proj,
        w_mem_kv=w_mem_kv[l].astype(bf16),
        gla=dict(wa=jnp.pad(gla_wa2[l], ((0, LANE - GLA_RANK), (0, 0))).astype(bf16),
                 ba=gla_ba[l].reshape(1, -1), ng=gla_norm_g[l].reshape(1, -1)),
        lru=dict(conv_w=lru_conv_w[l], conv_b=lru_conv_b[l].reshape(1, w), wr=block_diag(lru_wr[l]),
                 br=lru_br[l].reshape(1, w), wi=block_diag(lru_wi[l]), bi=lru_bi[l].reshape(1, w),
                 lam=lru_L[l].reshape(1, w)),
        sinks=swa_sinks[l],
        sgu=dict(g=sgu_ln_g[l].reshape(1, w), b=sgu_ln_b[l].reshape(1, w), wmix_p=wmix_p, bias_p=bias_p,
                 wmix_s=wmix_s, bias_s=bias_s),
        merge=dict(wg=w_gates, wb=w_branch[l].astype(bf16), wm=w_branch_mem[l].astype(bf16), wo=w_out[l].astype(bf16),
                   g=ln_g[l].reshape(1, d), b=ln_b[l].reshape(1, d)),
    )


def _layer(x, lw, grp, st, layer, tag, prev=()):
    nseq_total, t = grp["batch"], grp["seq"]
    n_tok = nseq_total * t
    x, xb = x
    xb3 = xb.reshape(nseq_total, t, D_MODEL)
    w_t = lw["w_proj"]
    proj = _matmul_call(xb, w_t, min(grp["proj_tm"], n_tok), 1024, "proj_" + tag, w_transposed=True, n_out=N_PROJ,
                        layer=layer)
    proj3 = proj.reshape(nseq_total, t, N_PROJ)

    lt = grp["lru"]
    y_lru, hlast, hist = _lru_call(proj3, st["hist0"], st["h0"], lw["lru"], nseq_total // lt[0], lt[0], t // lt[1],
                                   lt[1], "lru_" + tag)
    short = grp["kind"] == "s"
    gt = grp["gla"]
    if short:
        y_gla, s_out = _gla_step_call(proj3, st["gla0"], layer, [p["gla"] for p in prev], lw["gla"],
                                      nseq_total // gt[0], gt[0], t, "gla_" + tag)
    else:
        y_gla, s_out = _gla_call(proj3, st["gla0"], lw["gla"], nseq_total // gt[0], gt[0], t // gt[1], gt[1],
                                 "gla_" + tag)
    wt = grp["swa"]
    if short:
        y_swa, k_last, v_last = _swa_step_call(proj3, lw["sinks"], st["k_past"], st["v_past"], layer,
                                               [p["k_last"] for p in prev], [p["v_last"] for p in prev],
                                               grp["pos0"], nseq_total // wt[0], wt[0], t, "swa_" + tag)
    else:
        y_swa, k_last, v_last = _swa_call(proj3, lw["sinks"], st["k_past"], st["v_past"], grp["pos0"],
                                          nseq_total // wt[0], wt[0], t // wt[1], wt[1], "swa_" + tag)
    sg = lw["sgu"]
    w_all = lw["merge"]["wg"]
    sgu_out = _sgu_call(xb, w_all, layer, sg["g"], sg["b"], sg["wmix_" + grp["kind"]], sg["bias_" + grp["kind"]],
                        n_tok, grp["sgu_tiles"], grp["kind"] == "s", "sgu_" + tag)
    mt = grp["mem"]
    if short:
        y_mem = _mem_step_call(xb3, w_all, st["mk"], st["mv"], layer, nseq_total // mt[0], mt[0], t, "mem_" + tag)
    else:
        y_mem = _mem_call(xb3, w_all, layer, st["mk"], st["mv"], nseq_total // mt[0], mt[0], t // mt[1], mt[1],
                          "mem_" + tag)
    ys = tuple(y.reshape(n_tok, y.shape[-1]) for y in (y_gla, y_lru, y_swa, sgu_out[0], y_mem))
    x_new = _merge_call(ys, x, lw["ln_in"], lw["merge"], layer, min(grp["merge_tm"], n_tok), "merge_" + tag)
    return x_new, dict(gla=s_out, hlast=hlast, hist=hist, k_last=k_last, v_last=v_last,
                       vn=sgu_out[1] if len(sgu_out) > 1 else None)


_PROMPT = dict(kind="p", pos0=0, proj_tm=2048, merge_tm=512, lru=(4, 64), gla=(4, 128), swa=(4, 128), sgu_tiles=8,
               mem=(1, 1024))
_SAMPLE = dict(kind="s", pos0=PAST_LEN, proj_tm=1024, merge_tm=256, lru=(32, 8), gla=(16, 8), swa=(16, 8),
               sgu_tiles=8, mem=(16, 8))


def kernel(x_prompt, x_sample, mem_prompt, state_gla, state_lru_h, state_lru_conv, cache_swa_k, cache_swa_v,
           cache_mem_k, cache_mem_v, ln_in_g, ln_in_b, w_in, gla_wa2, gla_ba, gla_norm_g, lru_conv_w, lru_conv_b,
           lru_wr, lru_br, lru_wi, lru_bi, lru_L, swa_sinks, sgu_ln_g, sgu_ln_b, sgu_w, sgu_b, w_mem_kv, w_branch,
           w_branch_mem, w_out, ln_g, ln_b):
    bp, tp, d = x_prompt.shape
    bs, ts, _ = x_sample.shape
    w = BRANCH_W
    kvw = SWA_KV_HEADS * HEAD_DIM
    gp = dict(_PROMPT, batch=bp, seq=tp)
    gs = dict(_SAMPLE, batch=bs, seq=ts)

    xp = (x_prompt.reshape(bp * tp, d), _ln_call(x_prompt.reshape(bp * tp, d), ln_in_g, ln_in_b))
    xs = (x_sample.reshape(bs * ts, d), _ln_call(x_sample.reshape(bs * ts, d), ln_in_g, ln_in_b))
    mem2 = mem_prompt.reshape(bp * N_MEM, d)

    swa_k_t, swa_v_t, mem_k_t, mem_v_t = (jnp.transpose(c, (0, 1, 3, 4, 2))
                                          for c in (cache_swa_k, cache_swa_v, cache_mem_k, cache_mem_v))

    w_proj, w_gates = _prep_w_in(w_in)
    outs_p, outs_s, mks, mvs = [], [], [], []
    for l in range(DEPTH):
        lw = _prep_layer(l, w_proj, w_gates, gla_wa2, gla_ba, gla_norm_g, lru_conv_w, lru_conv_b, lru_wr, lru_br,
                         lru_wi, lru_bi, lru_L, swa_sinks, sgu_ln_g, sgu_ln_b, sgu_w, sgu_b, w_mem_kv, w_branch,
                         w_branch_mem, w_out, ln_g, ln_b)
        lw["ln_in"] = (ln_in_g, ln_in_b)
        mkv = _matmul_call(mem2, lw["w_mem_kv"], bp * N_MEM, 2 * MEM_W, "memkv_%d" % l)
        mk = mkv[:, :MEM_W].reshape(bp, N_MEM, MEM_W)
        mv = mkv[:, MEM_W:].reshape(bp, N_MEM, MEM_W)
        st_p = dict(hist0=jnp.zeros((bp, SUBLANE, w), f32), h0=jnp.zeros((bp, 1, w), f32),
                    gla0=jnp.zeros((bp, GLA_HEADS, GLA_DK, GLA_DV), f32),
                    k_past=jnp.zeros((bp, WINDOW, kvw), f32), v_past=jnp.zeros((bp, WINDOW, kvw), f32),
                    mk=mk, mv=mv)
        st_s = dict(hist0=jnp.pad(state_lru_conv[l], ((0, 0), (SUBLANE - (CONV_W - 1), 0), (0, 0))),
                    h0=state_lru_h[l][:, None, :], gla0=state_gla,
                    k_past=swa_k_t, v_past=swa_v_t, mk=mem_k_t, mv=mem_v_t)
        xp, op = _layer(xp, lw, gp, st_p, l, "p%d" % l)
        xs, os_ = _layer(xs, lw, gs, st_s, l, "s%d" % l, prev=outs_s if l == DEPTH - 1 else ())
        outs_p.append(op)
        outs_s.append(os_)
        mks.append(mk.reshape(bp, N_MEM, MEM_HEADS, HEAD_DIM))
        mvs.append(mv.reshape(bp, N_MEM, MEM_HEADS, HEAD_DIM))

    def stack(outs, fn):
        return jnp.stack([fn(o) for o in outs])

    def window(a):
        return a.reshape(a.shape[0], WINDOW, SWA_KV_HEADS, HEAD_DIM)

    last_s = outs_s[-1]

    def window_t(a):
        return jnp.transpose(a, (0, 1, 4, 2, 3))

    return (
        xp[0].reshape(bp, tp, d), xs[0].reshape(bs, ts, d),
        stack(outs_p, lambda o: o["gla"]), last_s["gla"],
        stack(outs_p, lambda o: o["hlast"][:, SUBLANE - 1]), stack(outs_s, lambda o: o["hlast"][:, SUBLANE - 1]),
        stack(outs_p, lambda o: o["hist"][:, SUBLANE - (CONV_W - 1):]),
        stack(outs_s, lambda o: o["hist"][:, SUBLANE - (CONV_W - 1):]),
        stack(outs_p, lambda o: window(o["k_last"])), window_t(last_s["k_last"]),
        stack(outs_p, lambda o: window(o["v_last"])), window_t(last_s["v_last"]),
        jnp.stack(mks), jnp.stack(mvs),
        stack(outs_s, lambda o: o["vn"].reshape(bs, ts, w)),
    )
```
